```python
import math
import jax
import jax.numpy as jnp
from jax import lax
import numpy as np

D_MODEL = 1024
BATCH = 8
SEQ = 4096
DEPTH = 2
DEC_BATCH = 4
DEC_SEQ = 8192
PAST_LEN = 128

ATT_HEADS = 8
ATT_KV_HEADS = 2
ATT_HEAD_DIM = 64
ATT_GROUP = ATT_HEADS // ATT_KV_HEADS
WINDOW = 128
ATT_BLOCK = 128
DN_HEADS = 4
DN_HEAD_DIM = 64
DN_WIDTH = DN_HEADS * DN_HEAD_DIM
DN_CHUNK = 64
SHORT_CONV = 3
CONV_CH = 256
CONV_WIDTH = 31
ATT_Q = ATT_HEADS * ATT_HEAD_DIM
ATT_KV = ATT_KV_HEADS * ATT_HEAD_DIM
MIX_WIDTH = ATT_Q + DN_WIDTH + CONV_CH
IN_SPLITS = (ATT_Q, ATT_KV, ATT_KV, DN_WIDTH, DN_WIDTH, DN_WIDTH, DN_WIDTH, 2 * DN_HEADS, 2 * DN_HEADS, 2 * CONV_CH)
IN_WIDTH = sum(IN_SPLITS)
N_EXPERTS = 16
CAPACITY_FACTOR = 2
EXPERT_FF = 1024
PLE_DIM = 256
NORM_EPS = 1e-6
F32 = jnp.float32

kernel_name = 'hybrid_parallel_bidir_encoder'


def rms_norm(x, gain):
    xf = x.astype(F32)
    y = xf * lax.rsqrt(jnp.mean(xf * xf, axis=-1, keepdims=True) + NORM_EPS)
    return (y * gain.astype(F32)).astype(x.dtype)


def l2_norm(x):
    xf = x.astype(F32)
    return xf * lax.rsqrt(jnp.sum(xf * xf, axis=-1, keepdims=True) + NORM_EPS)


def depthwise_conv(x, w):
    pad = (w.shape[0] - 1) // 2
    return lax.conv_general_dilated(x, w[:, None, :].astype(x.dtype), window_strides=(1,), padding=[(pad, pad)],
                                    dimension_numbers=('NWC', 'WIO', 'NWC'), feature_group_count=x.shape[-1])


def alibi_slopes(n):
    return jnp.exp2(-8.0 * jnp.arange(1, n + 1, dtype=F32) / n)


def window_attention(q, k, v, q_gain, k_gain, sink):
    bsz, seqlen = q.shape[0], q.shape[1]
    nb = seqlen // ATT_BLOCK
    q = rms_norm(q, q_gain).reshape(bsz, nb, ATT_BLOCK, ATT_KV_HEADS, ATT_GROUP, ATT_HEAD_DIM)
    k = rms_norm(k, k_gain)

    def band(t):
        tp = jnp.pad(t, ((0, 0), (ATT_BLOCK, ATT_BLOCK), (0, 0), (0, 0)))
        tp = tp.reshape(bsz, nb + 2, ATT_BLOCK, ATT_KV_HEADS, ATT_HEAD_DIM)
        return jnp.concatenate([tp[:, :-2], tp[:, 1:-1], tp[:, 2:]], axis=2)

    kb, vb = band(k), band(v)
    s = jnp.einsum('bnqhgd,bnkhd->bnhgqk', q, kb).astype(F32) * (ATT_HEAD_DIM ** -0.5)
    rel = jnp.arange(3 * ATT_BLOCK)[None, :] - ATT_BLOCK - jnp.arange(ATT_BLOCK)[:, None]
    dist = jnp.abs(rel).astype(F32)
    kpos = jnp.arange(nb)[:, None] * ATT_BLOCK + jnp.arange(3 * ATT_BLOCK)[None, :] - ATT_BLOCK
    valid = (jnp.abs(rel) <= WINDOW)[None] & ((kpos >= 0) & (kpos < seqlen))[:, None, :]
    slopes = alibi_slopes(ATT_HEADS).reshape(ATT_KV_HEADS, ATT_GROUP, 1, 1)
    s = jnp.where(valid[None, :, None, None], s - slopes * dist, -jnp.inf)
    sink_l = sink.astype(F32).reshape(ATT_KV_HEADS, ATT_GROUP, 1, 1)
    m = jnp.maximum(jnp.max(s, axis=-1, keepdims=True), sink_l)
    e = jnp.exp(s - m)
    probs = e / (jnp.sum(e, axis=-1, keepdims=True) + jnp.exp(sink_l - m))
    o = jnp.einsum('bnhgqk,bnkhd->bnqhgd', probs.astype(v.dtype), vb)
    return o.reshape(bsz, seqlen, ATT_Q)


def chunk_gated_delta_rule(q, k, v, g, beta):
    bsz, seqlen, nh, dk = q.shape
    dv = v.shape[-1]
    c = DN_CHUNK
    n = seqlen // c

    def chunks(t):
        t = t.reshape((bsz, n, c, nh) + t.shape[3:])
        return jnp.moveaxis(t, 3, 1)

    q = chunks(q * (dk ** -0.5))
    k = chunks(k)
    v = chunks(v)
    beta = chunks(beta)
    gc = jnp.cumsum(chunks(g), axis=-1)
    tri = jnp.tril(jnp.ones((c, c), dtype=bool))
    strict = jnp.tril(jnp.ones((c, c), dtype=bool), -1)
    decay = jnp.exp(jnp.where(tri, gc[..., :, None] - gc[..., None, :], -jnp.inf))
    kb = k * beta[..., None]
    lmat = jnp.where(strict, jnp.einsum('bhncd,bhnsd->bhncs', kb, k) * decay, 0.0) + jnp.eye(c, dtype=F32)
    u = lax.linalg.triangular_solve(lmat, v * beta[..., None], left_side=True, lower=True, unit_diagonal=True)
    w = lax.linalg.triangular_solve(lmat, kb * jnp.exp(gc)[..., None], left_side=True, lower=True, unit_diagonal=True)
    intra = jnp.einsum('bhncd,bhnsd->bhncs', q, k) * decay

    def step(state, xs):
        qn, kn, un, wn, gn, an = xs
        v_new = un - jnp.einsum('bhcd,bhde->bhce', wn, state)
        o = jnp.einsum('bhcd,bhde->bhce', qn * jnp.exp(gn)[..., None], state) + jnp.einsum('bhcs,bhse->bhce', an, v_new)
        glast = gn[..., -1]
        k_dec = kn * jnp.exp(glast[..., None] - gn)[..., None]
        state = state * jnp.exp(glast)[..., None, None] + jnp.einsum('bhcd,bhce->bhde', k_dec, v_new)
        return state, o

    xs = tuple(jnp.moveaxis(t, 2, 0) for t in (q, k, u, w, gc, intra))
    state0 = jnp.zeros((bsz, nh, dk, dv), F32)
    _, o = lax.scan(step, state0, xs)
    return o.transpose(1, 0, 3, 2, 4).reshape(bsz, seqlen, nh, dv)


def gated_deltanet(q, k, v, z, beta_raw, alpha_raw, conv_w, a_log, dt_bias, out_gain):
    bsz, seqlen = q.shape[0], q.shape[1]
    qkv = jax.nn.silu(depthwise_conv(jnp.concatenate([q, k, v], axis=-1), conv_w))
    q, k, v = jnp.split(qkv, 3, axis=-1)
    shp = (bsz, seqlen, DN_HEADS, DN_HEAD_DIM)
    q = l2_norm(q.reshape(shp))
    k = l2_norm(k.reshape(shp))
    v = v.reshape(shp).astype(F32)
    beta = jax.nn.sigmoid(beta_raw.astype(F32)).reshape(bsz, seqlen, 2, DN_HEADS)
    g = -jnp.exp(a_log.astype(F32)) * jax.nn.softplus(alpha_raw.astype(F32).reshape(bsz, seqlen, 2, DN_HEADS) + dt_bias.astype(F32))
    fwd = chunk_gated_delta_rule(q, k, v, g[:, :, 0], beta[:, :, 0])
    flip = lambda t: jnp.flip(t, axis=1)
    bwd = flip(chunk_gated_delta_rule(flip(q), flip(k), flip(v), flip(g[:, :, 1]), flip(beta[:, :, 1])))
    o = rms_norm(fwd + bwd, out_gain) * jax.nn.silu(z.reshape(shp).astype(F32))
    return o.reshape(bsz, seqlen, DN_WIDTH).astype(z.dtype)


def conformer_conv(u, dw, dw_bias, ln_gain, ln_bias):
    a, b = jnp.split(u, 2, axis=-1)
    c = (depthwise_conv(a * jax.nn.sigmoid(b), dw) + dw_bias).astype(F32)
    mu = jnp.mean(c, axis=-1, keepdims=True)
    var = jnp.mean(jnp.square(c - mu), axis=-1, keepdims=True)
    c = (c - mu) * lax.rsqrt(var + NORM_EPS) * ln_gain.astype(F32) + ln_bias.astype(F32)
    return jax.nn.silu(c).astype(u.dtype)


def expert_choice_ffn(x, w_router, w_gate, w_up, w_down):
    shape = x.shape
    xt = x.reshape(-1, shape[-1])
    cap = CAPACITY_FACTOR * xt.shape[0] // N_EXPERTS
    aff = jax.nn.softmax(jnp.matmul(xt, w_router).astype(F32), axis=-1)
    gate, idx = lax.top_k(aff.T, cap)
    xe = xt[idx]
    hid = jax.nn.silu(jnp.einsum('ecd,edf->ecf', xe, w_gate)) * jnp.einsum('ecd,edf->ecf', xe, w_up)
    ye = jnp.einsum('ecf,efd->ecd', hid, w_down) * gate[..., None].astype(x.dtype)
    y = jnp.zeros_like(xt).at[idx.reshape(-1)].add(ye.reshape(-1, shape[-1]))
    return y.reshape(shape)


def hybrid_layer(h, p_i, lw):
    (norm_mix, w_in, q_gain, k_gain, sink, dn_conv, dn_a_log, dn_dt_bias, dn_out_gain,
     cv_dw, cv_dw_bias, cv_ln_gain, cv_ln_bias, w_out, norm_ffn, w_router, w_gate, w_up, w_down,
     norm_ple, w_ple_gate, w_ple_proj) = lw
    bsz, seqlen = h.shape[0], h.shape[1]
    z = jnp.matmul(rms_norm(h, norm_mix), w_in)
    (q_a, k_a, v_a, q_b, k_b, v_b, z_b, beta_raw, alpha_raw, glu_in) = jnp.split(
        z, np.cumsum(IN_SPLITS)[:-1].tolist(), axis=-1)
    o_a = window_attention(q_a.reshape(bsz, seqlen, ATT_HEADS, ATT_HEAD_DIM),
                           k_a.reshape(bsz, seqlen, ATT_KV_HEADS, ATT_HEAD_DIM),
                           v_a.reshape(bsz, seqlen, ATT_KV_HEADS, ATT_HEAD_DIM), q_gain, k_gain, sink)
    o_b = gated_deltanet(q_b, k_b, v_b, z_b, beta_raw, alpha_raw, dn_conv, dn_a_log, dn_dt_bias, dn_out_gain)
    o_c = conformer_conv(glu_in, cv_dw, cv_dw_bias, cv_ln_gain, cv_ln_bias)
    h = h + jnp.matmul(jnp.concatenate([o_a, o_b, o_c], axis=-1), w_out)
    h = h + expert_choice_ffn(rms_norm(h, norm_ffn), w_router, w_gate, w_up, w_down)
    gate = jax.nn.sigmoid(jnp.matmul(rms_norm(h, norm_ple), w_ple_gate))
    return h + gate * jnp.matmul(p_i, w_ple_proj)


def encoder_trunk(x, p, weights):
    h = x
    for i in range(DEPTH):
        h = hybrid_layer(h, p[i], [w[i] for w in weights])
    return h


def setup_inputs(seed: int = 0) -> dict:
    key = jax.random.key(seed)
    ks = jax.random.split(key, 28)
    nrm = lambda kk, shape, scale: jax.random.normal(kk, shape, F32) * scale
    gain = lambda kk, shape: 1.0 + 0.02 * jax.random.normal(kk, shape, F32)
    dt = jnp.exp(jax.random.uniform(ks[9], (DEPTH, 2, DN_HEADS), F32, minval=math.log(1e-3), maxval=math.log(1e-1)))
    return {
        'x_prompt': nrm(ks[0], (BATCH, SEQ, D_MODEL), 1.0),
        'x_sample': nrm(ks[1], (DEC_BATCH, DEC_SEQ, D_MODEL), 1.0),
        'p_prompt': nrm(ks[2], (DEPTH, BATCH, SEQ, PLE_DIM), 1.0),
        'p_sample': nrm(ks[3], (DEPTH, DEC_BATCH, DEC_SEQ, PLE_DIM), 1.0),
        'norm_mix': gain(ks[4], (DEPTH, D_MODEL)),
        'w_in': nrm(ks[5], (DEPTH, D_MODEL, IN_WIDTH), D_MODEL ** -0.5),
        'q_gain': gain(ks[6], (DEPTH, ATT_HEAD_DIM)),
        'k_gain': gain(ks[7], (DEPTH, ATT_HEAD_DIM)),
        'sink': nrm(ks[8], (DEPTH, ATT_HEADS), 1.0),
        'dn_conv': nrm(ks[10], (DEPTH, SHORT_CONV, 3 * DN_WIDTH), SHORT_CONV ** -0.5),
        'dn_a_log': jnp.log(jax.random.uniform(ks[11], (DEPTH, 2, DN_HEADS), F32, minval=1.0, maxval=16.0)),
        'dn_dt_bias': dt + jnp.log(-jnp.expm1(-dt)),
        'dn_out_gain': gain(ks[12], (DEPTH, DN_HEAD_DIM)),
        'cv_dw': nrm(ks[13], (DEPTH, CONV_WIDTH, CONV_CH), CONV_WIDTH ** -0.5),
        'cv_dw_bias': nrm(ks[14], (DEPTH, CONV_CH), 0.02),
        'cv_ln_gain': gain(ks[15], (DEPTH, CONV_CH)),
        'cv_ln_bias': nrm(ks[16], (DEPTH, CONV_CH), 0.02),
        'w_out': nrm(ks[17], (DEPTH, MIX_WIDTH, D_MODEL), MIX_WIDTH ** -0.5),
        'norm_ffn': gain(ks[18], (DEPTH, D_MODEL)),
        'w_router': nrm(ks[19], (DEPTH, D_MODEL, N_EXPERTS), D_MODEL ** -0.5),
        'w_gate': nrm(ks[20], (DEPTH, N_EXPERTS, D_MODEL, EXPERT_FF), D_MODEL ** -0.5),
        'w_up': nrm(ks[21], (DEPTH, N_EXPERTS, D_MODEL, EXPERT_FF), D_MODEL ** -0.5),
        'w_down': nrm(ks[22], (DEPTH, N_EXPERTS, EXPERT_FF, D_MODEL), EXPERT_FF ** -0.5),
        'norm_ple': gain(ks[23], (DEPTH, D_MODEL)),
        'w_ple_gate': nrm(ks[24], (DEPTH, D_MODEL, D_MODEL), D_MODEL ** -0.5),
        'w_ple_proj': nrm(ks[25], (DEPTH, PLE_DIM, D_MODEL), PLE_DIM ** -0.5),
    }


def reference(x_prompt, x_sample, p_prompt, p_sample, norm_mix, w_in, q_gain, k_gain, sink, dn_conv, dn_a_log,
              dn_dt_bias, dn_out_gain, cv_dw, cv_dw_bias, cv_ln_gain, cv_ln_bias, w_out, norm_ffn, w_router,
              w_gate, w_up, w_down, norm_ple, w_ple_gate, w_ple_proj):
    weights = (norm_mix, w_in, q_gain, k_gain, sink, dn_conv, dn_a_log, dn_dt_bias, dn_out_gain,
               cv_dw, cv_dw_bias, cv_ln_gain, cv_ln_bias, w_out, norm_ffn, w_router, w_gate, w_up, w_down,
               norm_ple, w_ple_gate, w_ple_proj)
    y_prompt = encoder_trunk(x_prompt, p_prompt, weights)
    y_sample = encoder_trunk(x_sample, p_sample, weights)
    return (y_prompt, y_sample)
```

```python
import functools

import numpy as np
import jax
import jax.numpy as jnp
from jax import lax
from jax.experimental import pallas as pl
from jax.experimental.pallas import tpu as pltpu

F32 = jnp.float32
BF16 = jnp.bfloat16

D_MODEL = 1024
ATT_HEADS = 8
ATT_KV_HEADS = 2
ATT_HEAD_DIM = 64
ATT_GROUP = ATT_HEADS // ATT_KV_HEADS
WINDOW = 128
ATT_BLOCK = 128
DN_HEADS = 4
DN_HEAD_DIM = 64
DN_WIDTH = DN_HEADS * DN_HEAD_DIM
DN_CHUNK = 64
CONV_CH = 256
CONV_WIDTH = 31
ATT_Q = ATT_HEADS * ATT_HEAD_DIM
ATT_KV = ATT_KV_HEADS * ATT_HEAD_DIM
N_EXPERTS = 16
CAPACITY_FACTOR = 2
EXPERT_FF = 1024
PLE_DIM = 256
NORM_EPS = 1e-6

LANES = 128
SUBLANES = 8
VMEM_LIMIT = 48 * 1024 * 1024

ZA = ATT_Q + 2 * ATT_KV
ZB = 3 * DN_WIDTH
ZW = ZA + ZB + DN_WIDTH + 2 * CONV_CH + LANES


def _params(sem):
    return pltpu.CompilerParams(dimension_semantics=sem, vmem_limit_bytes=VMEM_LIMIT)


def _head_mean_matrix(width, head):
    idx = np.arange(width) // head
    return jnp.asarray((idx[:, None] == idx[None, :]).astype(np.float32) / head, dtype=BF16)


def _head_sum_matrix(width, head):
    idx = np.arange(width) // head
    return jnp.asarray((idx[:, None] == idx[None, :]).astype(np.float32), dtype=BF16)


def _sigmoid(x):
    return 1.0 / (1.0 + jnp.exp(-x))


def _silu(x):
    return x * _sigmoid(x)


def _in_proj_kernel(x_ref, gain_ref, w_ref, hm_ref, hgain_ref, za_ref, zb_ref, zg_ref, glu_ref, gates_ref):
    x = x_ref[...]
    ms = jnp.mean(x * x, axis=-1, keepdims=True)
    a = (x * lax.rsqrt(ms + NORM_EPS) * gain_ref[...]).astype(BF16)
    z = jnp.dot(a, w_ref[...], preferred_element_type=F32)
    nqk = ATT_Q + ATT_KV
    qk = z[:, :nqk]
    hms = jnp.dot((qk * qk).astype(BF16), hm_ref[...], preferred_element_type=F32)
    za_ref[:, :nqk] = qk * lax.rsqrt(hms + NORM_EPS) * hgain_ref[...]
    za_ref[:, nqk:] = z[:, nqk:ZA]
    zb_ref[...] = z[:, ZA:ZA + ZB]
    zg_ref[...] = z[:, ZA + ZB:ZA + ZB + DN_WIDTH]
    glu_ref[...] = z[:, ZA + ZB + DN_WIDTH:ZA + ZB + DN_WIDTH + 2 * CONV_CH]
    gates_ref[...] = z[:, ZW - LANES:]


def _in_proj(h2, gain, w_perm, hm, hgain, tm):
    n = h2.shape[0]
    row = lambda i: (i, 0)
    fixed = lambda i: (0, 0)
    return pl.pallas_call(
        _in_proj_kernel,
        grid=(n // tm,),
        in_specs=[pl.BlockSpec((tm, D_MODEL), row), pl.BlockSpec((1, D_MODEL), fixed),
                  pl.BlockSpec((D_MODEL, ZW), fixed), pl.BlockSpec(hm.shape, fixed),
                  pl.BlockSpec(hgain.shape, fixed)],
        out_specs=[pl.BlockSpec((tm, ZA), row), pl.BlockSpec((tm, ZB), row), pl.BlockSpec((tm, DN_WIDTH), row),
                   pl.BlockSpec((tm, 2 * CONV_CH), row), pl.BlockSpec((tm, LANES), row)],
        out_shape=[jax.ShapeDtypeStruct((n, ZA), F32), jax.ShapeDtypeStruct((n, ZB), F32),
                   jax.ShapeDtypeStruct((n, DN_WIDTH), F32), jax.ShapeDtypeStruct((n, 2 * CONV_CH), F32),
                   jax.ShapeDtypeStruct((n, LANES), F32)],
        compiler_params=_params(("parallel",)),
        name="in_proj",
    )(h2, gain, w_perm, hm, hgain)


def _attn_kernel(sink_ref, q_ref, kvp_ref, kvo_ref, kvn_ref, o_ref, *, seq_len):
    n = pl.program_id(1)
    kv = jnp.concatenate([kvp_ref[...], kvo_ref[...], kvn_ref[...]], axis=0)
    row = lax.broadcasted_iota(jnp.int32, (ATT_BLOCK, 3 * ATT_BLOCK), 0)
    col = lax.broadcasted_iota(jnp.int32, (ATT_BLOCK, 3 * ATT_BLOCK), 1)
    rel = col - ATT_BLOCK - row
    kpos = n * ATT_BLOCK - ATT_BLOCK + col
    valid = (jnp.abs(rel) <= WINDOW) & (kpos >= 0) & (kpos < seq_len)
    dist = jnp.abs(rel).astype(F32)
    for g in range(ATT_KV_HEADS):
        kg = kv[:, g * ATT_HEAD_DIM:(g + 1) * ATT_HEAD_DIM].astype(BF16)
        vg = kv[:, ATT_KV + g * ATT_HEAD_DIM:ATT_KV + (g + 1) * ATT_HEAD_DIM].astype(BF16)
        for j in range(ATT_GROUP):
            hd = g * ATT_GROUP + j
            slope = float(2.0 ** (-8.0 * (hd + 1) / ATT_HEADS))
            qh = q_ref[:, hd * ATT_HEAD_DIM:(hd + 1) * ATT_HEAD_DIM].astype(BF16)
            s = lax.dot_general(qh, kg, (((1,), (1,)), ((), ())), preferred_element_type=F32)
            s = jnp.where(valid, s - slope * dist, -jnp.inf)
            sink = sink_ref[hd]
            m = jnp.maximum(jnp.max(s, axis=-1, keepdims=True), sink)
            e = jnp.exp(s - m)
            denom = jnp.sum(e, axis=-1, keepdims=True) + jnp.exp(sink - m)
            p = (e / denom).astype(BF16)
            o_ref[:, hd * ATT_HEAD_DIM:(hd + 1) * ATT_HEAD_DIM] = jnp.dot(p, vg, preferred_element_type=F32)


def _attention(za, sink, bsz, seqlen):
    nb = seqlen // ATT_BLOCK
    za3 = za.reshape(bsz, seqlen, ZA)
    kvw = 2 * ATT_KV
    kvc = ATT_Q // kvw
    return pl.pallas_call(
        functools.partial(_attn_kernel, seq_len=seqlen),
        grid=(bsz, nb),
        in_specs=[pl.BlockSpec(memory_space=pltpu.SMEM),
                  pl.BlockSpec((None, ATT_BLOCK, ATT_Q), lambda b, n: (b, n, 0)),
                  pl.BlockSpec((None, ATT_BLOCK, kvw), lambda b, n: (b, jnp.maximum(n - 1, 0), kvc)),
                  pl.BlockSpec((None, ATT_BLOCK, kvw), lambda b, n: (b, n, kvc)),
                  pl.BlockSpec((None, ATT_BLOCK, kvw), lambda b, n: (b, jnp.minimum(n + 1, nb - 1), kvc))],
        out_specs=pl.BlockSpec((None, ATT_BLOCK, ATT_Q), lambda b, n: (b, n, 0)),
        out_shape=jax.ShapeDtypeStruct((bsz, seqlen, ATT_Q), F32),
        compiler_params=_params(("parallel", "parallel")),
        name="window_attention",
    )(sink, za3, za3, za3, za3).reshape(bsz * seqlen, ATT_Q)


DN_HALO = SUBLANES


def _dn_prep_kernel(x_ref, xp_ref, xn_ref, cw_ref, hs_ref, g_ref, aneg_ref, dtb_ref, mf_ref, mb_ref,
                    y_ref, gb_ref, buf_ref, *, tl):
    i = pl.program_id(1)
    nt = pl.num_programs(1)
    buf_ref[0:DN_HALO, :] = jnp.where(i > 0, xp_ref[...], 0.0)
    buf_ref[DN_HALO:DN_HALO + tl, :] = x_ref[...]
    buf_ref[DN_HALO + tl:, :] = jnp.where(i < nt - 1, xn_ref[...], 0.0)
    y = (cw_ref[0:1, :] * buf_ref[DN_HALO - 1:DN_HALO - 1 + tl, :]
         + cw_ref[1:2, :] * buf_ref[DN_HALO:DN_HALO + tl, :]
         + cw_ref[2:3, :] * buf_ref[DN_HALO + 1:DN_HALO + 1 + tl, :])
    y = _silu(y)
    qk = y[:, :2 * DN_WIDTH]
    ss = jnp.dot((qk * qk).astype(BF16), hs_ref[...], preferred_element_type=F32)
    lane = lax.broadcasted_iota(jnp.int32, (tl, 2 * DN_WIDTH), 1)
    scale = jnp.where(lane < DN_WIDTH, DN_HEAD_DIM ** -0.5, 1.0)
    y_ref[:, :2 * DN_WIDTH] = qk * lax.rsqrt(ss + NORM_EPS) * scale
    y_ref[:, 2 * DN_WIDTH:] = y[:, 2 * DN_WIDTH:]
    raw = g_ref[...]
    col = lax.broadcasted_iota(jnp.int32, (tl, LANES), 1)
    is_beta = (col & DN_HEADS) == 0
    t = raw + dtb_ref[...]
    softplus = jnp.maximum(t, 0.0) + jnp.log(1.0 + jnp.exp(-jnp.abs(t)))
    vals = jnp.where(is_beta, _sigmoid(raw), aneg_ref[...] * softplus)
    cf = jnp.dot(mf_ref[...], vals, preferred_element_type=F32, precision=lax.Precision.HIGHEST)
    cb = jnp.dot(mb_ref[...], vals, preferred_element_type=F32, precision=lax.Precision.HIGHEST)
    gb_ref[0] = jnp.where(is_beta, vals, cf)
    gb_ref[1] = pltpu.roll(jnp.where(is_beta, vals, cb), LANES - 2 * DN_HEADS, axis=1)


def _dn_prep(zb, gates, conv_w, hs, aneg, dtb, bsz, seqlen, tl):
    zb3 = zb.reshape(bsz, seqlen, ZB)
    g3 = gates.reshape(bsz, seqlen, LANES)
    nt = seqlen // tl
    hb = tl // DN_HALO
    ch = np.arange(tl) // DN_CHUNK
    same = ch[:, None] == ch[None, :]
    pos = np.arange(tl)
    mf = jnp.asarray((same & (pos[None, :] <= pos[:, None])).astype(np.float32))
    mb = jnp.asarray((same & (pos[None, :] >= pos[:, None])).astype(np.float32))
    fixed = lambda b, i: (0, 0)
    y, gb = pl.pallas_call(
        functools.partial(_dn_prep_kernel, tl=tl),
        grid=(bsz, nt),
        in_specs=[pl.BlockSpec((None, tl, ZB), lambda b, i: (b, i, 0)),
                  pl.BlockSpec((None, DN_HALO, ZB), lambda b, i: (b, jnp.maximum(i * hb - 1, 0), 0)),
                  pl.BlockSpec((None, DN_HALO, ZB), lambda b, i: (b, jnp.minimum((i + 1) * hb, nt * hb - 1), 0)),
                  pl.BlockSpec(conv_w.shape, fixed), pl.BlockSpec(hs.shape, fixed),
                  pl.BlockSpec((None, tl, LANES), lambda b, i: (b, i, 0)),
                  pl.BlockSpec((1, LANES), fixed), pl.BlockSpec((1, LANES), fixed),
                  pl.BlockSpec((tl, tl), fixed), pl.BlockSpec((tl, tl), fixed)],
        out_specs=[pl.BlockSpec((None, tl, ZB), lambda b, i: (b, i, 0)),
                   pl.BlockSpec((2, None, tl, LANES), lambda b, i: (0, b, i, 0))],
        out_shape=[jax.ShapeDtypeStruct((bsz, seqlen, ZB), F32),
                   jax.ShapeDtypeStruct((2, bsz, seqlen, LANES), F32)],
        scratch_shapes=[pltpu.VMEM((tl + 2 * DN_HALO, ZB), F32)],
        compiler_params=_params(("parallel", "parallel")),
        name="deltanet_prep",
    )(zb3, zb3, zb3, conv_w, hs, g3, aneg, dtb, mf, mb)
    return y, gb


def _lane_expand(cols, first):
    c = cols.shape[0]
    lane = lax.broadcasted_iota(jnp.int32, (c, LANES), 1)
    halves = []
    for h in range(0, DN_HEADS, 2):
        a = jnp.broadcast_to(cols[:, first + h:first + h + 1], (c, LANES))
        b = jnp.broadcast_to(cols[:, first + h + 1:first + h + 2], (c, LANES))
        halves.append(jnp.where(lane < DN_HEAD_DIM, a, b))
    return jnp.concatenate(halves, axis=1)


def _dn_chunk_kernel(x_ref, gb_ref, o_ref, s_ref, *, nsub):
    c = DN_CHUNK
    w = DN_WIDTH
    d = pl.program_id(0)
    fwd = d == 0

    @pl.when(pl.program_id(2) == 0)
    def _():
        s_ref[...] = jnp.zeros_like(s_ref)

    r_cat = lax.broadcasted_iota(jnp.int32, (c, w), 0)
    s_cat = lax.broadcasted_iota(jnp.int32, (c, w), 1) & (DN_HEAD_DIM - 1)
    ahead = (r_cat - s_cat) * jnp.where(fwd, 1, -1)
    incl = ahead >= 0
    strict = ahead > 0
    eye_cat = s_cat == r_cat
    rr = lax.broadcasted_iota(jnp.int32, (w, w), 0)
    cc = lax.broadcasted_iota(jnp.int32, (w, w), 1)
    head = (rr >> 6) == (cc >> 6)
    m16 = (rr >> 4) == (cc >> 4)
    m32 = (rr >> 5) == (cc >> 5)
    eye = (rr == cc).astype(F32)

    def bd(t):
        return jnp.where(head, jnp.concatenate([t] * DN_HEADS, axis=0), 0.0)

    def mm(a, b):
        return jnp.dot(a.astype(BF16), b.astype(BF16), preferred_element_type=F32)

    for step in range(nsub):
        si = jnp.where(fwd, step, nsub - 1 - step)
        st = pl.multiple_of(si * c, c)
        x = x_ref[pl.ds(st, c), :]
        q = x[:, :w]
        k = x[:, w:2 * w]
        v = x[:, 2 * w:]
        gbt = gb_ref[pl.ds(st, c), :]
        beta = _lane_expand(gbt, 0)
        gc = _lane_expand(gbt, DN_HEADS)
        grow = jnp.sum(jnp.where(eye_cat, gc, 0.0), axis=0, keepdims=True)
        decay = jnp.exp(jnp.where(incl, gc - grow, -jnp.inf))
        glast = jnp.where(fwd, gc[c - 1:c, :], gc[0:1, :])
        egc = jnp.exp(gc)
        kb = k * beta
        lhs = jnp.concatenate([kb, q], axis=0).astype(BF16)
        kk = lax.dot_general(lhs, bd(k).astype(BF16), (((1,), (1,)), ((), ())), preferred_element_type=F32)
        a_cat = jnp.where(strict, kk[:c] * decay, 0.0)
        intra = jnp.where(incl, kk[c:] * decay, 0.0)
        abd = bd(a_cat)
        xm = jnp.where(m16, -abd, 0.0)
        dinv = eye + xm
        x2 = mm(xm, xm)
        dinv = dinv + mm(dinv, x2)
        x4 = mm(x2, x2)
        dinv = dinv + mm(dinv, x4)
        x8 = mm(x4, x4)
        dinv = dinv + mm(dinv, x8)
        a32 = jnp.where(m32 & jnp.logical_not(m16), abd, 0.0)
        t32 = dinv - mm(dinv, mm(a32, dinv))
        a64 = jnp.where(m32, 0.0, abd)
        t64 = t32 - mm(t32, mm(a64, t32))
        t_cat = t64[0:c] + t64[c:2 * c] + t64[2 * c:3 * c] + t64[3 * c:4 * c]
        rhs = jnp.concatenate([bd(v * beta), bd(kb * egc)], axis=1)
        uw = mm(t_cat, rhs)
        u = uw[:, :w]
        wmat = uw[:, w:]
        state = s_ref[...]
        wq = mm(jnp.concatenate([wmat, q * egc], axis=0), state)
        vnew = u - wq[:c]
        o_ref[pl.ds(st, c), :] = wq[c:] + mm(intra, bd(vnew))
        kdec = (k * jnp.exp(glast - gc)).astype(BF16)
        upd = lax.dot_general(kdec, vnew.astype(BF16), (((0,), (0,)), ((), ())), preferred_element_type=F32)
        s_ref[...] = state * jnp.exp(glast) + jnp.where(head, upd, 0.0)


def _dn_chunk(y, gb, bsz, seqlen, ch):
    nsub = ch // DN_CHUNK
    nblk = seqlen // ch

    def blk(d, b, j):
        return jnp.where(d == 0, j, nblk - 1 - j)

    return pl.pallas_call(
        functools.partial(_dn_chunk_kernel, nsub=nsub),
        grid=(2, bsz, nblk),
        in_specs=[pl.BlockSpec((None, ch, ZB), lambda d, b, j: (b, blk(d, b, j), 0)),
                  pl.BlockSpec((None, None, ch, LANES), lambda d, b, j: (d, b, blk(d, b, j), 0))],
        out_specs=pl.BlockSpec((None, None, ch, DN_WIDTH), lambda d, b, j: (d, b, blk(d, b, j), 0)),
        out_shape=jax.ShapeDtypeStruct((2, bsz, seqlen, DN_WIDTH), F32),
        scratch_shapes=[pltpu.VMEM((DN_WIDTH, DN_WIDTH), F32)],
        compiler_params=_params(("parallel", "parallel", "arbitrary")),
        name="deltanet_chunks",
    )(y, gb).reshape(2, bsz * seqlen, DN_WIDTH)


CV_HALO = 2 * SUBLANES
CV_PAD = (CONV_WIDTH - 1) // 2


def _conv_kernel(x_ref, xp_ref, xn_ref, dw_ref, bias_ref, lng_ref, lnb_ref, o_ref, buf_ref, *, tl):
    i = pl.program_id(1)
    nt = pl.num_programs(1)

    def glu(t):
        return t[:, :CONV_CH] * _sigmoid(t[:, CONV_CH:])

    buf_ref[0:CV_HALO, :] = jnp.where(i > 0, glu(xp_ref[...]), 0.0)
    buf_ref[CV_HALO:CV_HALO + tl, :] = glu(x_ref[...])
    buf_ref[CV_HALO + tl:, :] = jnp.where(i < nt - 1, glu(xn_ref[...]), 0.0)
    acc = jnp.zeros((tl, CONV_CH), F32) + bias_ref[...]
    for j in range(CONV_WIDTH):
        off = CV_HALO - CV_PAD + j
        acc = acc + dw_ref[j:j + 1, :] * buf_ref[off:off + tl, :]
    mu = jnp.mean(acc, axis=-1, keepdims=True)
    cen = acc - mu
    var = jnp.mean(cen * cen, axis=-1, keepdims=True)
    o_ref[...] = _silu(cen * lax.rsqrt(var + NORM_EPS) * lng_ref[...] + lnb_ref[...])


def _conformer_conv(glu_in, dw, bias, lng, lnb, bsz, seqlen, tl):
    x3 = glu_in.reshape(bsz, seqlen, 2 * CONV_CH)
    nt = seqlen // tl
    hb = tl // CV_HALO
    fixed = lambda b, i: (0, 0)
    return pl.pallas_call(
        functools.partial(_conv_kernel, tl=tl),
        grid=(bsz, nt),
        in_specs=[pl.BlockSpec((None, tl, 2 * CONV_CH), lambda b, i: (b, i, 0)),
                  pl.BlockSpec((None, CV_HALO, 2 * CONV_CH), lambda b, i: (b, jnp.maximum(i * hb - 1, 0), 0)),
                  pl.BlockSpec((None, CV_HALO, 2 * CONV_CH),
                               lambda b, i: (b, jnp.minimum((i + 1) * hb, nt * hb - 1), 0)),
                  pl.BlockSpec(dw.shape, fixed), pl.BlockSpec((1, CONV_CH), fixed),
                  pl.BlockSpec((1, CONV_CH), fixed), pl.BlockSpec((1, CONV_CH), fixed)],
        out_specs=pl.BlockSpec((None, tl, CONV_CH), lambda b, i: (b, i, 0)),
        out_shape=jax.ShapeDtypeStruct((bsz, seqlen, CONV_CH), F32),
        scratch_shapes=[pltpu.VMEM((tl + 2 * CV_HALO, CONV_CH), F32)],
        compiler_params=_params(("parallel", "parallel")),
        name="conformer_conv",
    )(x3, x3, x3, dw, bias, lng, lnb).reshape(bsz * seqlen, CONV_CH)


def _out_proj_kernel(h_ref, oa_ref, odn_ref, zg_ref, oc_ref, og_ref, hm_ref, w_ref, out_ref):
    ob = odn_ref[0] + odn_ref[1]
    ms = jnp.dot((ob * ob).astype(BF16), hm_ref[...], preferred_element_type=F32)
    obn = ob * lax.rsqrt(ms + NORM_EPS) * og_ref[...]
    ob2 = obn * _silu(zg_ref[...])
    mix = jnp.concatenate([oa_ref[...], ob2, oc_ref[...]], axis=1).astype(BF16)
    out_ref[...] = h_ref[...] + jnp.dot(mix, w_ref[...], preferred_element_type=F32)


def _out_proj(h2, oa, odn, zg, oc, og, hm, w, tm):
    n = h2.shape[0]
    row = lambda i: (i, 0)
    fixed = lambda i: (0, 0)
    return pl.pallas_call(
        _out_proj_kernel,
        grid=(n // tm,),
        in_specs=[pl.BlockSpec((tm, D_MODEL), row), pl.BlockSpec((tm, ATT_Q), row),
                  pl.BlockSpec((2, tm, DN_WIDTH), lambda i: (0, i, 0)), pl.BlockSpec((tm, DN_WIDTH), row),
                  pl.BlockSpec((tm, CONV_CH), row), pl.BlockSpec((1, DN_WIDTH), fixed),
                  pl.BlockSpec(hm.shape, fixed), pl.BlockSpec(w.shape, fixed)],
        out_specs=pl.BlockSpec((tm, D_MODEL), row),
        out_shape=jax.ShapeDtypeStruct((n, D_MODEL), F32),
        compiler_params=_params(("parallel",)),
        name="out_proj",
    )(h2, oa, odn, zg, oc, og, hm, w)


def _route_kernel(h_ref, gain_ref, wr_ref, xn_ref, aff_ref):
    x = h_ref[...]
    ms = jnp.mean(x * x, axis=-1, keepdims=True)
    xn = x * lax.rsqrt(ms + NORM_EPS) * gain_ref[...]
    xn_ref[...] = xn.astype(BF16)
    logits = jnp.dot(xn, wr_ref[...], preferred_element_type=F32, precision=lax.Precision.HIGHEST)
    lane = lax.broadcasted_iota(jnp.int32, logits.shape, 1)
    logits = jnp.where(lane < N_EXPERTS, logits, -jnp.inf)
    m = jnp.max(logits, axis=-1, keepdims=True)
    e = jnp.exp(logits - m)
    aff = e / jnp.sum(e, axis=-1, keepdims=True)
    aff_ref[...] = jnp.transpose(aff)[:N_EXPERTS, :]


def _route(h2, gain, wr_pad, tm):
    n = h2.shape[0]
    row = lambda i: (i, 0)
    fixed = lambda i: (0, 0)
    return pl.pallas_call(
        _route_kernel,
        grid=(n // tm,),
        in_specs=[pl.BlockSpec((tm, D_MODEL), row), pl.BlockSpec((1, D_MODEL), fixed),
                  pl.BlockSpec((D_MODEL, LANES), fixed)],
        out_specs=[pl.BlockSpec((tm, D_MODEL), row), pl.BlockSpec((N_EXPERTS, tm), lambda i: (0, i))],
        out_shape=[jax.ShapeDtypeStruct((n, D_MODEL), BF16), jax.ShapeDtypeStruct((N_EXPERTS, n), F32)],
        compiler_params=_params(("parallel",)),
        name="moe_route",
    )(h2, gain, wr_pad)


def _expert_kernel(x_ref, gate_ref, wg_ref, wu_ref, wd_ref, y_ref):
    x = x_ref[...]
    hg = jnp.dot(x, wg_ref[...], preferred_element_type=F32)
    hu = jnp.dot(x, wu_ref[...], preferred_element_type=F32)
    hid = (_silu(hg) * hu).astype(BF16)
    y_ref[...] = jnp.dot(hid, wd_ref[...], preferred_element_type=F32) * gate_ref[...]


def _expert_ffn(xe, gate, wg, wu, wd, tc):
    ne, cap, _ = xe.shape
    wspec = lambda shape: pl.BlockSpec((None,) + shape, lambda e, i: (e, 0, 0))
    return pl.pallas_call(
        _expert_kernel,
        grid=(ne, cap // tc),
        in_specs=[pl.BlockSpec((None, tc, D_MODEL), lambda e, i: (e, i, 0)),
                  pl.BlockSpec((None, tc, 1), lambda e, i: (e, i, 0)),
                  wspec((D_MODEL, EXPERT_FF)), wspec((D_MODEL, EXPERT_FF)), wspec((EXPERT_FF, D_MODEL))],
        out_specs=pl.BlockSpec((None, tc, D_MODEL), lambda e, i: (e, i, 0)),
        out_shape=jax.ShapeDtypeStruct((ne, cap, D_MODEL), F32),
        compiler_params=_params(("parallel", "parallel")),
        name="expert_ffn",
    )(xe, gate, wg, wu, wd)


def _ple_kernel(h_ref, p_ref, gain_ref, wg_ref, wp_ref, out_ref):
    x = h_ref[...]
    ms = jnp.mean(x * x, axis=-1, keepdims=True)
    xn = (x * lax.rsqrt(ms + NORM_EPS) * gain_ref[...]).astype(BF16)
    gate = _sigmoid(jnp.dot(xn, wg_ref[...], preferred_element_type=F32))
    proj = jnp.dot(p_ref[...].astype(BF16), wp_ref[...], preferred_element_type=F32)
    out_ref[...] = x + gate * proj


def _ple(h2, p2, gain, wg, wp, tm):
    n = h2.shape[0]
    row = lambda i: (i, 0)
    fixed = lambda i: (0, 0)
    return pl.pallas_call(
        _ple_kernel,
        grid=(n // tm,),
        in_specs=[pl.BlockSpec((tm, D_MODEL), row), pl.BlockSpec((tm, PLE_DIM), row),
                  pl.BlockSpec((1, D_MODEL), fixed), pl.BlockSpec(wg.shape, fixed), pl.BlockSpec(wp.shape, fixed)],
        out_specs=pl.BlockSpec((tm, D_MODEL), row),
        out_shape=jax.ShapeDtypeStruct((n, D_MODEL), F32),
        compiler_params=_params(("parallel",)),
        name="ple_gate",
    )(h2, p2, gain, wg, wp)


def _in_perm():
    o_dnz = ZA + ZB
    o_beta = o_dnz + DN_WIDTH
    o_alpha = o_beta + 2 * DN_HEADS
    o_glu = o_alpha + 2 * DN_HEADS
    cols = list(range(0, o_beta)) + list(range(o_glu, o_glu + 2 * CONV_CH))
    for d in range(2):
        cols += [o_beta + d * DN_HEADS + h for h in range(DN_HEADS)]
        cols += [o_alpha + d * DN_HEADS + h for h in range(DN_HEADS)]
    return np.asarray(cols, dtype=np.int32)


def _prep_layer(lw):
    (norm_mix, w_in, q_gain, k_gain, sink, dn_conv, dn_a_log, dn_dt_bias, dn_out_gain,
     cv_dw, cv_dw_bias, cv_ln_gain, cv_ln_bias, w_out, norm_ffn, w_router, w_gate, w_up, w_down,
     norm_ple, w_ple_gate, w_ple_proj) = lw
    perm = _in_perm()
    w_perm = jnp.pad(w_in[:, perm], ((0, 0), (0, ZW - perm.shape[0]))).astype(BF16)
    hgain = jnp.concatenate([jnp.tile(q_gain, ATT_HEADS) * (ATT_HEAD_DIM ** -0.5),
                             jnp.tile(k_gain, ATT_KV_HEADS)]).reshape(1, -1)
    zeros4 = jnp.zeros((DN_HEADS,), F32)
    aneg = -jnp.exp(dn_a_log.astype(F32))
    aneg_row = jnp.concatenate([zeros4, aneg[0], zeros4, aneg[1]])
    dtb_row = jnp.concatenate([zeros4, dn_dt_bias[0], zeros4, dn_dt_bias[1]])
    pad = lambda r: jnp.pad(r, (0, LANES - r.shape[0])).reshape(1, LANES)
    return dict(
        norm_mix=norm_mix.reshape(1, -1), w_in=w_perm, hgain=hgain, sink=sink.astype(F32),
        dn_conv=dn_conv, aneg=pad(aneg_row), dtb=pad(dtb_row),
        dn_out_gain=jnp.tile(dn_out_gain, DN_HEADS).reshape(1, -1),
        cv_dw=cv_dw, cv_dw_bias=cv_dw_bias.reshape(1, -1), cv_ln_gain=cv_ln_gain.reshape(1, -1),
        cv_ln_bias=cv_ln_bias.reshape(1, -1), w_out=w_out.astype(BF16),
        norm_ffn=norm_ffn.reshape(1, -1), w_router=jnp.pad(w_router, ((0, 0), (0, LANES - N_EXPERTS))),
        w_gate=w_gate.astype(BF16), w_up=w_up.astype(BF16), w_down=w_down.astype(BF16),
        norm_ple=norm_ple.reshape(1, -1), w_ple_gate=w_ple_gate.astype(BF16), w_ple_proj=w_ple_proj.astype(BF16))


def _tiles(bsz, seqlen):
    n = bsz * seqlen
    return dict(tm=min(512, n), tl=min(256, seqlen), ch=min(256, seqlen), tcv=min(512, seqlen))


def _moe(h2, pw, t):
    n = h2.shape[0]
    cap = CAPACITY_FACTOR * n // N_EXPERTS
    xn, aff_t = _route(h2, pw["norm_ffn"], pw["w_router"], t["tm"])
    gate, idx = lax.top_k(aff_t, cap)
    xe = xn[idx]
    ye = _expert_ffn(xe, gate[..., None], pw["w_gate"], pw["w_up"], pw["w_down"], min(512, cap))
    return h2.at[idx.reshape(-1)].add(ye.reshape(-1, D_MODEL))


def _layer(h2, p2, pw, bsz, seqlen):
    t = _tiles(bsz, seqlen)
    hm_att = _head_mean_matrix(ATT_Q + ATT_KV, ATT_HEAD_DIM)
    hs_dn = _head_sum_matrix(2 * DN_WIDTH, DN_HEAD_DIM)
    hm_dn = _head_mean_matrix(DN_WIDTH, DN_HEAD_DIM)
    za, zb, zg, glu_in, gates = _in_proj(h2, pw["norm_mix"], pw["w_in"], hm_att, pw["hgain"], t["tm"])
    o_a = _attention(za, pw["sink"], bsz, seqlen)
    y, gb = _dn_prep(zb, gates, pw["dn_conv"], hs_dn, pw["aneg"], pw["dtb"], bsz, seqlen, t["tl"])
    o_dn = _dn_chunk(y, gb, bsz, seqlen, t["ch"])
    o_c = _conformer_conv(glu_in, pw["cv_dw"], pw["cv_dw_bias"], pw["cv_ln_gain"], pw["cv_ln_bias"],
                          bsz, seqlen, t["tcv"])
    h2 = _out_proj(h2, o_a, o_dn, zg, o_c, pw["dn_out_gain"], hm_dn, pw["w_out"], t["tm"])
    h2 = _moe(h2, pw, t)
    return _ple(h2, p2, pw["norm_ple"], pw["w_ple_gate"], pw["w_ple_proj"], t["tm"])


def _trunk(x, p, layer_weights):
    bsz, seqlen, _ = x.shape
    h2 = x.reshape(bsz * seqlen, D_MODEL)
    for i, pw in enumerate(layer_weights):
        h2 = _layer(h2, p[i].reshape(bsz * seqlen, PLE_DIM), pw, bsz, seqlen)
    return h2.reshape(bsz, seqlen, D_MODEL)


def kernel(x_prompt, x_sample, p_prompt, p_sample, norm_mix, w_in, q_gain, k_gain, sink, dn_conv, dn_a_log,
           dn_dt_bias, dn_out_gain, cv_dw, cv_dw_bias, cv_ln_gain, cv_ln_bias, w_out, norm_ffn, w_router,
           w_gate, w_up, w_down, norm_ple, w_ple_gate, w_ple_proj):
    weights = (norm_mix, w_in, q_gain, k_gain, sink, dn_conv, dn_a_log, dn_dt_bias, dn_out_gain,
               cv_dw, cv_dw_bias, cv_ln_gain, cv_ln_bias, w_out, norm_ffn, w_router, w_gate, w_up, w_down,
               norm_ple, w_ple_gate, w_ple_proj)
    depth = w_in.shape[0]
    layer_weights = [_prep_layer([w[i] for w in weights]) for i in range(depth)]
    return (_trunk(x_prompt, p_prompt, layer_weights), _trunk(x_sample, p_sample, layer_weights))
```

```python
import functools

import numpy as np
import jax
import jax.numpy as jnp
from jax import lax
from jax.experimental import pallas as pl
from jax.experimental.pallas import tpu as pltpu

F32 = jnp.float32
BF16 = jnp.bfloat16

D_MODEL = 1024
ATT_HEADS = 8
ATT_KV_HEADS = 2
ATT_HEAD_DIM = 64
ATT_GROUP = ATT_HEADS // ATT_KV_HEADS
WINDOW = 128
ATT_BLOCK = 128
DN_HEADS = 4
DN_HEAD_DIM = 64
DN_WIDTH = DN_HEADS * DN_HEAD_DIM
DN_CHUNK = 64
CONV_CH = 256
CONV_WIDTH = 31
ATT_Q = ATT_HEADS * ATT_HEAD_DIM
ATT_KV = ATT_KV_HEADS * ATT_HEAD_DIM
N_EXPERTS = 16
CAPACITY_FACTOR = 2
EXPERT_FF = 1024
PLE_DIM = 256
NORM_EPS = 1e-6

LANES = 128
SUBLANES = 8
VMEM_LIMIT = 48 * 1024 * 1024

ZA = ATT_Q + 2 * ATT_KV
ZB = 3 * DN_WIDTH
ZW = ZA + ZB + DN_WIDTH + 2 * CONV_CH + LANES


def _params(sem):
    return pltpu.CompilerParams(dimension_semantics=sem, vmem_limit_bytes=VMEM_LIMIT)


def _head_mean_matrix(width, head):
    idx = np.arange(width) // head
    return jnp.asarray((idx[:, None] == idx[None, :]).astype(np.float32) / head, dtype=BF16)


def _head_sum_matrix(width, head):
    idx = np.arange(width) // head
    return jnp.asarray((idx[:, None] == idx[None, :]).astype(np.float32), dtype=BF16)


def _sigmoid(x):
    return 1.0 / (1.0 + jnp.exp(-x))


def _silu(x):
    return x * _sigmoid(x)


def _in_proj_kernel(x_ref, gain_ref, w_ref, hm_ref, hgain_ref, za_ref, zb_ref, zg_ref, glu_ref, gates_ref):
    x = x_ref[...]
    ms = jnp.mean(x * x, axis=-1, keepdims=True)
    a = (x * lax.rsqrt(ms + NORM_EPS) * gain_ref[...]).astype(BF16)
    z = jnp.dot(a, w_ref[...], preferred_element_type=F32)
    nqk = ATT_Q + ATT_KV
    qk = z[:, :nqk]
    hms = jnp.dot((qk * qk).astype(BF16), hm_ref[...], preferred_element_type=F32)
    za_ref[:, :nqk] = qk * lax.rsqrt(hms + NORM_EPS) * hgain_ref[...]
    za_ref[:, nqk:] = z[:, nqk:ZA]
    zb_ref[...] = z[:, ZA:ZA + ZB]
    zg_ref[...] = z[:, ZA + ZB:ZA + ZB + DN_WIDTH]
    glu_ref[...] = z[:, ZA + ZB + DN_WIDTH:ZA + ZB + DN_WIDTH + 2 * CONV_CH]
    gates_ref[...] = z[:, ZW - LANES:]


def _in_proj(h2, gain, w_perm, hm, hgain, tm):
    n = h2.shape[0]
    row = lambda i: (i, 0)
    fixed = lambda i: (0, 0)
    return pl.pallas_call(
        _in_proj_kernel,
        grid=(n // tm,),
        in_specs=[pl.BlockSpec((tm, D_MODEL), row), pl.BlockSpec((1, D_MODEL), fixed),
                  pl.BlockSpec((D_MODEL, ZW), fixed), pl.BlockSpec(hm.shape, fixed),
                  pl.BlockSpec(hgain.shape, fixed)],
        out_specs=[pl.BlockSpec((tm, ZA), row), pl.BlockSpec((tm, ZB), row), pl.BlockSpec((tm, DN_WIDTH), row),
                   pl.BlockSpec((tm, 2 * CONV_CH), row), pl.BlockSpec((tm, LANES), row)],
        out_shape=[jax.ShapeDtypeStruct((n, ZA), F32), jax.ShapeDtypeStruct((n, ZB), F32),
                   jax.ShapeDtypeStruct((n, DN_WIDTH), F32), jax.ShapeDtypeStruct((n, 2 * CONV_CH), F32),
                   jax.ShapeDtypeStruct((n, LANES), F32)],
        compiler_params=_params(("parallel",)),
        name="in_proj",
    )(h2, gain, w_perm, hm, hgain)


def _attn_kernel(sink_ref, q_ref, kvp_ref, kvo_ref, kvn_ref, o_ref, *, seq_len):
    n = pl.program_id(1)
    kv = jnp.concatenate([kvp_ref[...], kvo_ref[...], kvn_ref[...]], axis=0)
    row = lax.broadcasted_iota(jnp.int32, (ATT_BLOCK, 3 * ATT_BLOCK), 0)
    col = lax.broadcasted_iota(jnp.int32, (ATT_BLOCK, 3 * ATT_BLOCK), 1)
    rel = col - ATT_BLOCK - row
    kpos = n * ATT_BLOCK - ATT_BLOCK + col
    valid = (jnp.abs(rel) <= WINDOW) & (kpos >= 0) & (kpos < seq_len)
    dist = jnp.abs(rel).astype(F32)
    for g in range(ATT_KV_HEADS):
        kg = kv[:, g * ATT_HEAD_DIM:(g + 1) * ATT_HEAD_DIM].astype(BF16)
        vg = kv[:, ATT_KV + g * ATT_HEAD_DIM:ATT_KV + (g + 1) * ATT_HEAD_DIM].astype(BF16)
        for j in range(ATT_GROUP):
            hd = g * ATT_GROUP + j
            slope = float(2.0 ** (-8.0 * (hd + 1) / ATT_HEADS))
            qh = q_ref[:, hd * ATT_HEAD_DIM:(hd + 1) * ATT_HEAD_DIM].astype(BF16)
            s = lax.dot_general(qh, kg, (((1,), (1,)), ((), ())), preferred_element_type=F32)
            s = jnp.where(valid, s - slope * dist, -jnp.inf)
            sink = sink_ref[hd]
            m = jnp.maximum(jnp.max(s, axis=-1, keepdims=True), sink)
            e = jnp.exp(s - m)
            denom = jnp.sum(e, axis=-1, keepdims=True) + jnp.exp(sink - m)
            p = (e / denom).astype(BF16)
            o_ref[:, hd * ATT_HEAD_DIM:(hd + 1) * ATT_HEAD_DIM] = jnp.dot(p, vg, preferred_element_type=F32)


def _attention(za, sink, bsz, seqlen):
    nb = seqlen // ATT_BLOCK
    za3 = za.reshape(bsz, seqlen, ZA)
    kvw = 2 * ATT_KV
    kvc = ATT_Q // kvw
    return pl.pallas_call(
        functools.partial(_attn_kernel, seq_len=seqlen),
        grid=(bsz, nb),
        in_specs=[pl.BlockSpec(memory_space=pltpu.SMEM),
                  pl.BlockSpec((None, ATT_BLOCK, ATT_Q), lambda b, n: (b, n, 0)),
                  pl.BlockSpec((None, ATT_BLOCK, kvw), lambda b, n: (b, jnp.maximum(n - 1, 0), kvc)),
                  pl.BlockSpec((None, ATT_BLOCK, kvw), lambda b, n: (b, n, kvc)),
                  pl.BlockSpec((None, ATT_BLOCK, kvw), lambda b, n: (b, jnp.minimum(n + 1, nb - 1), kvc))],
        out_specs=pl.BlockSpec((None, ATT_BLOCK, ATT_Q), lambda b, n: (b, n, 0)),
        out_shape=jax.ShapeDtypeStruct((bsz, seqlen, ATT_Q), F32),
        compiler_params=_params(("parallel", "parallel")),
        name="window_attention",
    )(sink, za3, za3, za3, za3).reshape(bsz * seqlen, ATT_Q)


DN_HALO = SUBLANES


def _dn_prep_kernel(x_ref, xp_ref, xn_ref, cw_ref, hs_ref, g_ref, aneg_ref, dtb_ref, mf_ref, mb_ref,
                    y_ref, gb_ref, buf_ref, *, tl):
    i = pl.program_id(1)
    nt = pl.num_programs(1)
    buf_ref[0:DN_HALO, :] = jnp.where(i > 0, xp_ref[...], 0.0)
    buf_ref[DN_HALO:DN_HALO + tl, :] = x_ref[...]
    buf_ref[DN_HALO + tl:, :] = jnp.where(i < nt - 1, xn_ref[...], 0.0)
    y = (cw_ref[0:1, :] * buf_ref[DN_HALO - 1:DN_HALO - 1 + tl, :]
         + cw_ref[1:2, :] * buf_ref[DN_HALO:DN_HALO + tl, :]
         + cw_ref[2:3, :] * buf_ref[DN_HALO + 1:DN_HALO + 1 + tl, :])
    y = _silu(y)
    qk = y[:, :2 * DN_WIDTH]
    ss = jnp.dot((qk * qk).astype(BF16), hs_ref[...], preferred_element_type=F32)
    lane = lax.broadcasted_iota(jnp.int32, (tl, 2 * DN_WIDTH), 1)
    scale = jnp.where(lane < DN_WIDTH, DN_HEAD_DIM ** -0.5, 1.0)
    y_ref[:, :2 * DN_WIDTH] = qk * lax.rsqrt(ss + NORM_EPS) * scale
    y_ref[:, 2 * DN_WIDTH:] = y[:, 2 * DN_WIDTH:]
    raw = g_ref[...]
    col = lax.broadcasted_iota(jnp.int32, (tl, LANES), 1)
    is_beta = (col & DN_HEADS) == 0
    t = raw + dtb_ref[...]
    softplus = jnp.maximum(t, 0.0) + jnp.log(1.0 + jnp.exp(-jnp.abs(t)))
    vals = jnp.where(is_beta, _sigmoid(raw), aneg_ref[...] * softplus)
    cf = jnp.dot(mf_ref[...], vals, preferred_element_type=F32, precision=lax.Precision.HIGHEST)
    cb = jnp.dot(mb_ref[...], vals, preferred_element_type=F32, precision=lax.Precision.HIGHEST)
    gb_ref[0] = jnp.where(is_beta, vals, cf)
    gb_ref[1] = pltpu.roll(jnp.where(is_beta, vals, cb), LANES - 2 * DN_HEADS, axis=1)


def _dn_prep(zb, gates, conv_w, hs, aneg, dtb, bsz, seqlen, tl):
    zb3 = zb.reshape(bsz, seqlen, ZB)
    g3 = gates.reshape(bsz, seqlen, LANES)
    nt = seqlen // tl
    hb = tl // DN_HALO
    ch = np.arange(tl) // DN_CHUNK
    same = ch[:, None] == ch[None, :]
    pos = np.arange(tl)
    mf = jnp.asarray((same & (pos[None, :] <= pos[:, None])).astype(np.float32))
    mb = jnp.asarray((same & (pos[None, :] >= pos[:, None])).astype(np.float32))
    fixed = lambda b, i: (0, 0)
    y, gb = pl.pallas_call(
        functools.partial(_dn_prep_kernel, tl=tl),
        grid=(bsz, nt),
        in_specs=[pl.BlockSpec((None, tl, ZB), lambda b, i: (b, i, 0)),
                  pl.BlockSpec((None, DN_HALO, ZB), lambda b, i: (b, jnp.maximum(i * hb - 1, 0), 0)),
                  pl.BlockSpec((None, DN_HALO, ZB), lambda b, i: (b, jnp.minimum((i + 1) * hb, nt * hb - 1), 0)),
                  pl.BlockSpec(conv_w.shape, fixed), pl.BlockSpec(hs.shape, fixed),
                  pl.BlockSpec((None, tl, LANES), lambda b, i: (b, i, 0)),
                  pl.BlockSpec((1, LANES), fixed), pl.BlockSpec((1, LANES), fixed),
                  pl.BlockSpec((tl, tl), fixed), pl.BlockSpec((tl, tl), fixed)],
        out_specs=[pl.BlockSpec((None, tl, ZB), lambda b, i: (b, i, 0)),
                   pl.BlockSpec((2, None, tl, LANES), lambda b, i: (0, b, i, 0))],
        out_shape=[jax.ShapeDtypeStruct((bsz, seqlen, ZB), F32),
                   jax.ShapeDtypeStruct((2, bsz, seqlen, LANES), F32)],
        scratch_shapes=[pltpu.VMEM((tl + 2 * DN_HALO, ZB), F32)],
        compiler_params=_params(("parallel", "parallel")),
        name="deltanet_prep",
    )(zb3, zb3, zb3, conv_w, hs, g3, aneg, dtb, mf, mb)
    return y, gb


def _lane_expand(cols, first):
    c = cols.shape[0]
    lane = lax.broadcasted_iota(jnp.int32, (c, LANES), 1)
    halves = []
    for h in range(0, DN_HEADS, 2):
        a = jnp.broadcast_to(cols[:, first + h:first + h + 1], (c, LANES))
        b = jnp.broadcast_to(cols[:, first + h + 1:first + h + 2], (c, LANES))
        halves.append(jnp.where(lane < DN_HEAD_DIM, a, b))
    return jnp.concatenate(halves, axis=1)


def _dn_chunk_kernel(x_ref, gb_ref, o_ref, s_ref, *, nsub):
    c = DN_CHUNK
    w = DN_WIDTH
    d = pl.program_id(0)
    fwd = d == 0

    @pl.when(pl.program_id(2) == 0)
    def _():
        s_ref[...] = jnp.zeros_like(s_ref)

    r_cat = lax.broadcasted_iota(jnp.int32, (c, w), 0)
    s_cat = lax.broadcasted_iota(jnp.int32, (c, w), 1) & (DN_HEAD_DIM - 1)
    ahead = (r_cat - s_cat) * jnp.where(fwd, 1, -1)
    incl = ahead >= 0
    strict = ahead > 0
    eye_cat = s_cat == r_cat
    rr = lax.broadcasted_iota(jnp.int32, (w, w), 0)
    cc = lax.broadcasted_iota(jnp.int32, (w, w), 1)
    head = (rr >> 6) == (cc >> 6)
    m16 = (rr >> 4) == (cc >> 4)
    m32 = (rr >> 5) == (cc >> 5)
    eye = (rr == cc).astype(F32)

    def bd(t):
        return jnp.where(head, jnp.concatenate([t] * DN_HEADS, axis=0), 0.0)

    def mm(a, b):
        return jnp.dot(a.astype(BF16), b.astype(BF16), preferred_element_type=F32)

    for step in range(nsub):
        si = jnp.where(fwd, step, nsub - 1 - step)
        st = pl.multiple_of(si * c, c)
        x = x_ref[pl.ds(st, c), :]
        q = x[:, :w]
        k = x[:, w:2 * w]
        v = x[:, 2 * w:]
        gbt = gb_ref[pl.ds(st, c), :]
        beta = _lane_expand(gbt, 0)
        gc = _lane_expand(gbt, DN_HEADS)
        grow = jnp.sum(jnp.where(eye_cat, gc, 0.0), axis=0, keepdims=True)
        decay = jnp.exp(jnp.where(incl, gc - grow, -jnp.inf))
        glast = jnp.where(fwd, gc[c - 1:c, :], gc[0:1, :])
        egc = jnp.exp(gc)
        kb = k * beta
        lhs = jnp.concatenate([kb, q], axis=0).astype(BF16)
        kk = lax.dot_general(lhs, bd(k).astype(BF16), (((1,), (1,)), ((), ())), preferred_element_type=F32)
        a_cat = jnp.where(strict, kk[:c] * decay, 0.0)
        intra = jnp.where(incl, kk[c:] * decay, 0.0)
        abd = bd(a_cat)
        xm = jnp.where(m16, -abd, 0.0)
        dinv = eye + xm
        x2 = mm(xm, xm)
        dinv = dinv + mm(dinv, x2)
        x4 = mm(x2, x2)
        dinv = dinv + mm(dinv, x4)
        x8 = mm(x4, x4)
        dinv = dinv + mm(dinv, x8)
        a32 = jnp.where(m32 & jnp.logical_not(m16), abd, 0.0)
        t32 = dinv - mm(dinv, mm(a32, dinv))
        a64 = jnp.where(m32, 0.0, abd)
        t64 = t32 - mm(t32, mm(a64, t32))
        t_cat = t64[0:c] + t64[c:2 * c] + t64[2 * c:3 * c] + t64[3 * c:4 * c]
        rhs = jnp.concatenate([bd(v * beta), bd(kb * egc)], axis=1)
        uw = mm(t_cat, rhs)
        u = uw[:, :w]
        wmat = uw[:, w:]
        state = s_ref[...]
        wq = mm(jnp.concatenate([wmat, q * egc], axis=0), state)
        vnew = u - wq[:c]
        o_ref[pl.ds(st, c), :] = wq[c:] + mm(intra, bd(vnew))
        kdec = (k * jnp.exp(glast - gc)).astype(BF16)
        upd = lax.dot_general(kdec, vnew.astype(BF16), (((0,), (0,)), ((), ())), preferred_element_type=F32)
        s_ref[...] = state * jnp.exp(glast) + jnp.where(head, upd, 0.0)


def _dn_chunk(y, gb, bsz, seqlen, ch):
    nsub = ch // DN_CHUNK
    nblk = seqlen // ch

    def blk(d, b, j):
        return jnp.where(d == 0, j, nblk - 1 - j)

    return pl.pallas_call(
        functools.partial(_dn_chunk_kernel, nsub=nsub),
        grid=(2, bsz, nblk),
        in_specs=[pl.BlockSpec((None, ch, ZB), lambda d, b, j: (b, blk(d, b, j), 0)),
                  pl.BlockSpec((None, None, ch, LANES), lambda d, b, j: (d, b, blk(d, b, j), 0))],
        out_specs=pl.BlockSpec((None, None, ch, DN_WIDTH), lambda d, b, j: (d, b, blk(d, b, j), 0)),
        out_shape=jax.ShapeDtypeStruct((2, bsz, seqlen, DN_WIDTH), F32),
        scratch_shapes=[pltpu.VMEM((DN_WIDTH, DN_WIDTH), F32)],
        compiler_params=_params(("parallel", "parallel", "arbitrary")),
        name="deltanet_chunks",
    )(y, gb).reshape(2, bsz * seqlen, DN_WIDTH)


CV_HALO = 2 * SUBLANES
CV_PAD = (CONV_WIDTH - 1) // 2


def _conv_kernel(x_ref, xp_ref, xn_ref, dw_ref, bias_ref, lng_ref, lnb_ref, o_ref, buf_ref, *, tl):
    i = pl.program_id(1)
    nt = pl.num_programs(1)

    def glu(t):
        return t[:, :CONV_CH] * _sigmoid(t[:, CONV_CH:])

    buf_ref[0:CV_HALO, :] = jnp.where(i > 0, glu(xp_ref[...]), 0.0)
    buf_ref[CV_HALO:CV_HALO + tl, :] = glu(x_ref[...])
    buf_ref[CV_HALO + tl:, :] = jnp.where(i < nt - 1, glu(xn_ref[...]), 0.0)
    acc = jnp.zeros((tl, CONV_CH), F32) + bias_ref[...]
    for j in range(CONV_WIDTH):
        off = CV_HALO - CV_PAD + j
        acc = acc + dw_ref[j:j + 1, :] * buf_ref[off:off + tl, :]
    mu = jnp.mean(acc, axis=-1, keepdims=True)
    cen = acc - mu
    var = jnp.mean(cen * cen, axis=-1, keepdims=True)
    o_ref[...] = _silu(cen * lax.rsqrt(var + NORM_EPS) * lng_ref[...] + lnb_ref[...])


def _conformer_conv(glu_in, dw, bias, lng, lnb, bsz, seqlen, tl):
    x3 = glu_in.reshape(bsz, seqlen, 2 * CONV_CH)
    nt = seqlen // tl
    hb = tl // CV_HALO
    fixed = lambda b, i: (0, 0)
    return pl.pallas_call(
        functools.partial(_conv_kernel, tl=tl),
        grid=(bsz, nt),
        in_specs=[pl.BlockSpec((None, tl, 2 * CONV_CH), lambda b, i: (b, i, 0)),
                  pl.BlockSpec((None, CV_HALO, 2 * CONV_CH), lambda b, i: (b, jnp.maximum(i * hb - 1, 0), 0)),
                  pl.BlockSpec((None, CV_HALO, 2 * CONV_CH),
                               lambda b, i: (b, jnp.minimum((i + 1) * hb, nt * hb - 1), 0)),
                  pl.BlockSpec(dw.shape, fixed), pl.BlockSpec((1, CONV_CH), fixed),
                  pl.BlockSpec((1, CONV_CH), fixed), pl.BlockSpec((1, CONV_CH), fixed)],
        out_specs=pl.BlockSpec((None, tl, CONV_CH), lambda b, i: (b, i, 0)),
        out_shape=jax.ShapeDtypeStruct((bsz, seqlen, CONV_CH), F32),
        scratch_shapes=[pltpu.VMEM((tl + 2 * CV_HALO, CONV_CH), F32)],
        compiler_params=_params(("parallel", "parallel")),
        name="conformer_conv",
    )(x3, x3, x3, dw, bias, lng, lnb).reshape(bsz * seqlen, CONV_CH)


def _out_proj_kernel(h_ref, oa_ref, odn_ref, zg_ref, oc_ref, og_ref, hm_ref, w_ref, out_ref):
    ob = odn_ref[0] + odn_ref[1]
    ms = jnp.dot((ob * ob).astype(BF16), hm_ref[...], preferred_element_type=F32)
    obn = ob * lax.rsqrt(ms + NORM_EPS) * og_ref[...]
    ob2 = obn * _silu(zg_ref[...])
    mix = jnp.concatenate([oa_ref[...], ob2, oc_ref[...]], axis=1).astype(BF16)
    out_ref[...] = h_ref[...] + jnp.dot(mix, w_ref[...], preferred_element_type=F32)


def _out_proj(h2, oa, odn, zg, oc, og, hm, w, tm):
    n = h2.shape[0]
    row = lambda i: (i, 0)
    fixed = lambda i: (0, 0)
    return pl.pallas_call(
        _out_proj_kernel,
        grid=(n // tm,),
        in_specs=[pl.BlockSpec((tm, D_MODEL), row), pl.BlockSpec((tm, ATT_Q), row),
                  pl.BlockSpec((2, tm, DN_WIDTH), lambda i: (0, i, 0)), pl.BlockSpec((tm, DN_WIDTH), row),
                  pl.BlockSpec((tm, CONV_CH), row), pl.BlockSpec((1, DN_WIDTH), fixed),
                  pl.BlockSpec(hm.shape, fixed), pl.BlockSpec(w.shape, fixed)],
        out_specs=pl.BlockSpec((tm, D_MODEL), row),
        out_shape=jax.ShapeDtypeStruct((n, D_MODEL), F32),
        compiler_params=_params(("parallel",)),
        name="out_proj",
    )(h2, oa, odn, zg, oc, og, hm, w)


def _route_kernel(h_ref, gain_ref, wr_ref, xn_ref, aff_ref):
    x = h_ref[...]
    ms = jnp.mean(x * x, axis=-1, keepdims=True)
    xn = x * lax.rsqrt(ms + NORM_EPS) * gain_ref[...]
    xn_ref[...] = xn.astype(BF16)
    logits = jnp.dot(xn, wr_ref[...], preferred_element_type=F32, precision=lax.Precision.HIGHEST)
    lane = lax.broadcasted_iota(jnp.int32, logits.shape, 1)
    logits = jnp.where(lane < N_EXPERTS, logits, -jnp.inf)
    m = jnp.max(logits, axis=-1, keepdims=True)
    e = jnp.exp(logits - m)
    aff = e / jnp.sum(e, axis=-1, keepdims=True)
    aff_ref[...] = jnp.transpose(aff)[:N_EXPERTS, :]


def _route(h2, gain, wr_pad, tm):
    n = h2.shape[0]
    row = lambda i: (i, 0)
    fixed = lambda i: (0, 0)
    return pl.pallas_call(
        _route_kernel,
        grid=(n // tm,),
        in_specs=[pl.BlockSpec((tm, D_MODEL), row), pl.BlockSpec((1, D_MODEL), fixed),
                  pl.BlockSpec((D_MODEL, LANES), fixed)],
        out_specs=[pl.BlockSpec((tm, D_MODEL), row), pl.BlockSpec((N_EXPERTS, tm), lambda i: (0, i))],
        out_shape=[jax.ShapeDtypeStruct((n, D_MODEL), BF16), jax.ShapeDtypeStruct((N_EXPERTS, n), F32)],
        compiler_params=_params(("parallel",)),
        name="moe_route",
    )(h2, gain, wr_pad)


MOE_TILE = 256
MOE_ALIGN = 2 * SUBLANES
MOE_WIN = 80
MOE_PAD = 512
MOE_UNSELECTED = -64.0


def _select_kernel(aff_ref, tri_ref, val_ref, cnt_ref, *, cap, tile):
    ne, n = aff_ref.shape
    nt = n // tile
    capf = float(cap)

    def bits_of(x):
        return lax.bitcast_convert_type(x, jnp.int32)

    def search(i, thr):
        cand = thr | jnp.left_shift(jnp.int32(1), 30 - i)
        cnt = jnp.sum((bits_of(aff_ref[...]) >= cand).astype(F32), axis=1, keepdims=True)
        return jnp.where(cnt >= capf, cand, thr)

    thr = lax.fori_loop(0, 31, search, jnp.zeros((ne, 1), jnp.int32))
    n_gt = jnp.sum((bits_of(aff_ref[...]) > thr).astype(F32), axis=1, keepdims=True)
    need = capf - n_gt
    lane = lax.broadcasted_iota(jnp.int32, (ne, LANES), 1)

    def tile_body(j, carry):
        eq_before, cnt_acc = carry
        off = pl.multiple_of(j * tile, tile)
        b = bits_of(aff_ref[:, pl.ds(off, tile)])
        gt = b > thr
        eqf = (b == thr).astype(F32)
        eq_rank = eq_before + jnp.dot(eqf.astype(BF16), tri_ref[...], preferred_element_type=F32)
        self_ = jnp.where(gt, 1.0, jnp.where(eq_rank <= need, eqf, 0.0))
        rank = jnp.dot(self_.astype(BF16), tri_ref[...], preferred_element_type=F32)
        val_ref[:, pl.ds(off, tile)] = jnp.where(self_ > 0.0, rank, MOE_UNSELECTED)
        cnt = jnp.sum(self_, axis=1, keepdims=True)
        return (eq_before + jnp.sum(eqf, axis=1, keepdims=True), cnt_acc + jnp.where(lane == j, cnt, 0.0))

    init = (jnp.zeros((ne, 1), F32), jnp.zeros((ne, LANES), F32))
    _, cnt_acc = lax.fori_loop(0, nt, tile_body, init)
    cnt_ref[...] = cnt_acc


def _select(aff_t, cap, tile):
    ne, n = aff_t.shape
    assert n // tile <= LANES
    tri = jnp.asarray(np.triu(np.ones((tile, tile), np.float32)), dtype=BF16)
    return pl.pallas_call(
        functools.partial(_select_kernel, cap=cap, tile=tile),
        out_shape=[jax.ShapeDtypeStruct((ne, n), F32), jax.ShapeDtypeStruct((ne, LANES), F32)],
        compiler_params=pltpu.CompilerParams(vmem_limit_bytes=VMEM_LIMIT),
        name="moe_select",
    )(aff_t, tri)


def _moe_plan(cnt, nt):
    c = cnt[:, :nt].astype(jnp.int32).T
    starts = jnp.concatenate([jnp.zeros((1, N_EXPERTS), jnp.int32), jnp.cumsum(c, axis=0)], axis=0)
    head = starts[:-1] & (MOE_ALIGN - 1)
    kmax = jnp.maximum(jnp.max((head + c + MOE_WIN - 1) // MOE_WIN, axis=1), 1).astype(jnp.int32)
    w = jnp.arange(MOE_WIN, dtype=jnp.int32)
    tgt = (w[None, None, :] + 1 - head[:, :, None]).astype(F32).reshape(nt, 1, N_EXPERTS * MOE_WIN)
    return starts.reshape(-1), kmax, tgt


def _expand_matrix():
    e = np.arange(N_EXPERTS * MOE_WIN) // MOE_WIN
    return jnp.asarray((np.arange(N_EXPERTS)[:, None] == e[None, :]).astype(np.float32), dtype=BF16)


def _slot_onehot(val_ref, eexp_ref):
    return lax.dot_general(val_ref[...].astype(BF16), eexp_ref[...], (((0,), (0,)), ((), ())),
                           preferred_element_type=F32)


def _dispatch_kernel(start_ref, kmax_ref, xn_ref, val_ref, tgt_ref, eexp_ref, xe_ref, stage, carry, sem):
    j = pl.program_id(0)
    nt = pl.num_programs(0)
    slot = lax.rem(j, 2)
    ne = N_EXPERTS

    cap = xe_ref.shape[1] - MOE_PAD

    @pl.when(j == 0)
    def _():
        carry[...] = jnp.zeros_like(carry)
        stage[0, 0:MOE_PAD, :] = jnp.zeros((MOE_PAD, D_MODEL), BF16)
        fills = [pltpu.make_async_copy(stage.at[0, pl.ds(0, MOE_PAD)], xe_ref.at[e, pl.ds(cap, MOE_PAD)], sem.at[0])
                 for e in range(ne)]
        for f in fills:
            f.start()
        for f in fills:
            f.wait()

    def window_copy(sl, e, row0):
        return pltpu.make_async_copy(stage.at[sl, pl.ds(e * MOE_WIN, MOE_WIN)],
                                     xe_ref.at[e, pl.ds(row0, MOE_WIN)], sem.at[sl])

    def wait_windows(sl):
        for e in range(ne):
            window_copy(sl, e, 0).wait()

    rep = _slot_onehot(val_ref, eexp_ref)
    xn = xn_ref[...]
    row = lax.broadcasted_iota(jnp.int32, (MOE_ALIGN, D_MODEL), 0)

    def block(k, _):
        @pl.when(k > 0)
        def _():
            wait_windows(slot)

        lo = k * MOE_WIN
        pt = (rep == tgt_ref[...] + lo.astype(F32)).astype(BF16)
        comp = lax.dot_general(pt, xn, (((0,), (0,)), ((), ())), preferred_element_type=F32)
        stage[slot] = comp.astype(BF16)
        for e in range(ne):
            s = start_ref[j * ne + e]
            head = s & (MOE_ALIGN - 1)
            r0 = e * MOE_WIN

            @pl.when(k == 0)
            def _():
                fresh = stage[slot, r0:r0 + MOE_ALIGN, :]
                kept = carry[e * MOE_ALIGN:(e + 1) * MOE_ALIGN, :]
                stage[slot, r0:r0 + MOE_ALIGN, :] = jnp.where(row < head, kept, fresh)

            nxt = (head + start_ref[(j + 1) * ne + e] - s) & (-MOE_ALIGN)

            @pl.when((nxt >= lo) & (nxt < lo + MOE_WIN))
            def _():
                off = pl.multiple_of(nxt - lo, MOE_ALIGN)
                carry[e * MOE_ALIGN:(e + 1) * MOE_ALIGN, :] = stage[slot, pl.ds(r0 + off, MOE_ALIGN), :]

        @pl.when((k == 0) & (j > 0))
        def _():
            wait_windows(1 - slot)

        for e in range(ne):
            base = pl.multiple_of((start_ref[j * ne + e] & (-MOE_ALIGN)) + lo, MOE_ALIGN)
            window_copy(slot, e, base).start()
        return 0

    lax.fori_loop(0, kmax_ref[j], block, 0)

    @pl.when(j == nt - 1)
    def _():
        wait_windows(slot)


def _dispatch(xn, val, starts, kmax, tgt, eexp, cap, tile):
    n = xn.shape[0]
    nt = n // tile
    rows = N_EXPERTS * MOE_WIN
    return pl.pallas_call(
        _dispatch_kernel,
        grid_spec=pltpu.PrefetchScalarGridSpec(
            num_scalar_prefetch=2, grid=(nt,),
            in_specs=[pl.BlockSpec((tile, D_MODEL), lambda j, s, k: (j, 0)),
                      pl.BlockSpec((N_EXPERTS, tile), lambda j, s, k: (0, j)),
                      pl.BlockSpec((None, 1, rows), lambda j, s, k: (j, 0, 0)),
                      pl.BlockSpec((N_EXPERTS, rows), lambda j, s, k: (0, 0))],
            out_specs=pl.BlockSpec(memory_space=pl.ANY),
            scratch_shapes=[pltpu.VMEM((2, rows, D_MODEL), BF16),
                            pltpu.VMEM((N_EXPERTS * MOE_ALIGN, D_MODEL), BF16),
                            pltpu.SemaphoreType.DMA((2,))]),
        out_shape=jax.ShapeDtypeStruct((N_EXPERTS, cap + MOE_PAD, D_MODEL), BF16),
        compiler_params=_params(("arbitrary",)),
        name="moe_dispatch",
    )(starts, kmax, xn, val, tgt, eexp)


def _expert_kernel(x_ref, wr_ref, wg_ref, wu_ref, wd_ref, y_ref, *, ntile):
    e = pl.program_id(0)
    i = pl.program_id(1)

    @pl.when(i < ntile)
    def _():
        x = x_ref[...]
        logits = (jnp.dot(x, wr_ref[0], preferred_element_type=F32)
                  + jnp.dot(x, wr_ref[1], preferred_element_type=F32))
        lane = lax.broadcasted_iota(jnp.int32, logits.shape, 1)
        logits = jnp.where(lane < N_EXPERTS, logits, -jnp.inf)
        ex = jnp.exp(logits - jnp.max(logits, axis=-1, keepdims=True))
        gate = (jnp.sum(jnp.where(lane == e, ex, 0.0), axis=-1, keepdims=True)
                / jnp.sum(ex, axis=-1, keepdims=True))
        hg = jnp.dot(x, wg_ref[...], preferred_element_type=F32)
        hu = jnp.dot(x, wu_ref[...], preferred_element_type=F32)
        hid = (_silu(hg) * hu).astype(BF16)
        y_ref[...] = (jnp.dot(hid, wd_ref[...], preferred_element_type=F32) * gate).astype(BF16)

    @pl.when(i >= ntile)
    def _():
        y_ref[...] = jnp.zeros_like(y_ref)


def _expert_ffn(xe, wr2, wg, wu, wd, cap, tc):
    ne, rows, _ = xe.shape
    wspec = lambda shape: pl.BlockSpec((None,) + shape, lambda e, i: (e, 0, 0))
    return pl.pallas_call(
        functools.partial(_expert_kernel, ntile=cap // tc),
        grid=(ne, rows // tc),
        in_specs=[pl.BlockSpec((None, tc, D_MODEL), lambda e, i: (e, i, 0)),
                  pl.BlockSpec(wr2.shape, lambda e, i: (0, 0, 0)),
                  wspec((D_MODEL, EXPERT_FF)), wspec((D_MODEL, EXPERT_FF)), wspec((EXPERT_FF, D_MODEL))],
        out_specs=pl.BlockSpec((None, tc, D_MODEL), lambda e, i: (e, i, 0)),
        out_shape=jax.ShapeDtypeStruct((ne, rows, D_MODEL), BF16),
        compiler_params=_params(("parallel", "parallel")),
        name="expert_ffn",
    )(xe, wr2, wg, wu, wd)


def _combine_kernel(start_ref, kmax_ref, h_ref, p_ref, val_ref, tgt_ref, eexp_ref, gain_ref, wg_ref, wp_ref,
                    ye_ref, out_ref, stage, sem):
    j = pl.program_id(0)
    nt = pl.num_programs(0)
    slot = lax.rem(j, 2)
    ne = N_EXPERTS

    def window_copy(sl, e, row0):
        return pltpu.make_async_copy(ye_ref.at[e, pl.ds(row0, MOE_WIN)],
                                     stage.at[sl, pl.ds(e * MOE_WIN, MOE_WIN)], sem.at[sl])

    def fetch(sl, tile_idx, lo):
        for e in range(ne):
            base = pl.multiple_of((start_ref[tile_idx * ne + e] & (-MOE_ALIGN)) + lo, MOE_ALIGN)
            window_copy(sl, e, base).start()

    def wait_windows(sl):
        for e in range(ne):
            window_copy(sl, e, 0).wait()

    @pl.when(j == 0)
    def _():
        fetch(slot, j, 0)

    @pl.when(j + 1 < nt)
    def _():
        fetch(1 - slot, j + 1, 0)

    rep = _slot_onehot(val_ref, eexp_ref)
    wait_windows(slot)
    pt = (rep == tgt_ref[...]).astype(BF16)
    acc = h_ref[...] + jnp.dot(pt, stage[slot], preferred_element_type=F32)

    def extra(k, acc):
        lo = k * MOE_WIN
        fetch(slot, j, lo)
        wait_windows(slot)
        pk = (rep == tgt_ref[...] + lo.astype(F32)).astype(BF16)
        return acc + jnp.dot(pk, stage[slot], preferred_element_type=F32)

    x = lax.fori_loop(1, kmax_ref[j], extra, acc)
    ms = jnp.mean(x * x, axis=-1, keepdims=True)
    xn = (x * lax.rsqrt(ms + NORM_EPS) * gain_ref[...]).astype(BF16)
    gate = _sigmoid(jnp.dot(xn, wg_ref[...], preferred_element_type=F32))
    proj = jnp.dot(p_ref[...].astype(BF16), wp_ref[...], preferred_element_type=F32)
    out_ref[...] = x + gate * proj


def _combine_ple(h2, p2, ye, val, starts, kmax, tgt, eexp, gain, wg, wp, tile):
    n = h2.shape[0]
    nt = n // tile
    rows = N_EXPERTS * MOE_WIN
    fixed = lambda j, s, k: (0, 0)
    return pl.pallas_call(
        _combine_kernel,
        grid_spec=pltpu.PrefetchScalarGridSpec(
            num_scalar_prefetch=2, grid=(nt,),
            in_specs=[pl.BlockSpec((tile, D_MODEL), lambda j, s, k: (j, 0)),
                      pl.BlockSpec((tile, PLE_DIM), lambda j, s, k: (j, 0)),
                      pl.BlockSpec((N_EXPERTS, tile), lambda j, s, k: (0, j)),
                      pl.BlockSpec((None, 1, rows), lambda j, s, k: (j, 0, 0)),
                      pl.BlockSpec((N_EXPERTS, rows), fixed),
                      pl.BlockSpec((1, D_MODEL), fixed), pl.BlockSpec(wg.shape, fixed),
                      pl.BlockSpec(wp.shape, fixed),
                      pl.BlockSpec(memory_space=pl.ANY)],
            out_specs=pl.BlockSpec((tile, D_MODEL), lambda j, s, k: (j, 0)),
            scratch_shapes=[pltpu.VMEM((2, rows, D_MODEL), BF16), pltpu.SemaphoreType.DMA((2,))]),
        out_shape=jax.ShapeDtypeStruct((n, D_MODEL), F32),
        compiler_params=_params(("arbitrary",)),
        name="moe_combine_ple",
    )(starts, kmax, h2, p2, val, tgt, eexp, gain, wg, wp, ye)


def _in_perm():
    o_dnz = ZA + ZB
    o_beta = o_dnz + DN_WIDTH
    o_alpha = o_beta + 2 * DN_HEADS
    o_glu = o_alpha + 2 * DN_HEADS
    cols = list(range(0, o_beta)) + list(range(o_glu, o_glu + 2 * CONV_CH))
    for d in range(2):
        cols += [o_beta + d * DN_HEADS + h for h in range(DN_HEADS)]
        cols += [o_alpha + d * DN_HEADS + h for h in range(DN_HEADS)]
    return np.asarray(cols, dtype=np.int32)


def _prep_layer(lw):
    (norm_mix, w_in, q_gain, k_gain, sink, dn_conv, dn_a_log, dn_dt_bias, dn_out_gain,
     cv_dw, cv_dw_bias, cv_ln_gain, cv_ln_bias, w_out, norm_ffn, w_router, w_gate, w_up, w_down,
     norm_ple, w_ple_gate, w_ple_proj) = lw
    perm = _in_perm()
    w_perm = jnp.pad(w_in[:, perm], ((0, 0), (0, ZW - perm.shape[0]))).astype(BF16)
    hgain = jnp.concatenate([jnp.tile(q_gain, ATT_HEADS) * (ATT_HEAD_DIM ** -0.5),
                             jnp.tile(k_gain, ATT_KV_HEADS)]).reshape(1, -1)
    zeros4 = jnp.zeros((DN_HEADS,), F32)
    aneg = -jnp.exp(dn_a_log.astype(F32))
    aneg_row = jnp.concatenate([zeros4, aneg[0], zeros4, aneg[1]])
    dtb_row = jnp.concatenate([zeros4, dn_dt_bias[0], zeros4, dn_dt_bias[1]])
    pad = lambda r: jnp.pad(r, (0, LANES - r.shape[0])).reshape(1, LANES)
    wr = jnp.pad(w_router.astype(F32), ((0, 0), (0, LANES - N_EXPERTS)))
    wr_hi = wr.astype(BF16)
    wr2 = jnp.stack([wr_hi, (wr - wr_hi.astype(F32)).astype(BF16)])
    return dict(
        w_router2=wr2,
        norm_mix=norm_mix.reshape(1, -1), w_in=w_perm, hgain=hgain, sink=sink.astype(F32),
        dn_conv=dn_conv, aneg=pad(aneg_row), dtb=pad(dtb_row),
        dn_out_gain=jnp.tile(dn_out_gain, DN_HEADS).reshape(1, -1),
        cv_dw=cv_dw, cv_dw_bias=cv_dw_bias.reshape(1, -1), cv_ln_gain=cv_ln_gain.reshape(1, -1),
        cv_ln_bias=cv_ln_bias.reshape(1, -1), w_out=w_out.astype(BF16),
        norm_ffn=norm_ffn.reshape(1, -1), w_router=jnp.pad(w_router, ((0, 0), (0, LANES - N_EXPERTS))),
        w_gate=w_gate.astype(BF16), w_up=w_up.astype(BF16), w_down=w_down.astype(BF16),
        norm_ple=norm_ple.reshape(1, -1), w_ple_gate=w_ple_gate.astype(BF16), w_ple_proj=w_ple_proj.astype(BF16))


def _tiles(bsz, seqlen):
    n = bsz * seqlen
    return dict(tm=min(512, n), tl=min(256, seqlen), ch=min(256, seqlen), tcv=min(512, seqlen))


def _moe_ple(h2, p2, pw, t):
    n = h2.shape[0]
    cap = CAPACITY_FACTOR * n // N_EXPERTS
    tile = min(MOE_TILE, n)
    xn, aff_t = _route(h2, pw["norm_ffn"], pw["w_router"], t["tm"])
    val, cnt = _select(aff_t, cap, tile)
    starts, kmax, tgt = _moe_plan(cnt, n // tile)
    eexp = _expand_matrix()
    xe = _dispatch(xn, val, starts, kmax, tgt, eexp, cap, tile)
    ye = _expert_ffn(xe, pw["w_router2"], pw["w_gate"], pw["w_up"], pw["w_down"], cap, min(512, cap))
    return _combine_ple(h2, p2, ye, val, starts, kmax, tgt, eexp, pw["norm_ple"], pw["w_ple_gate"],
                        pw["w_ple_proj"], tile)


def _layer(h2, p2, pw, bsz, seqlen):
    t = _tiles(bsz, seqlen)
    hm_att = _head_mean_matrix(ATT_Q + ATT_KV, ATT_HEAD_DIM)
    hs_dn = _head_sum_matrix(2 * DN_WIDTH, DN_HEAD_DIM)
    hm_dn = _head_mean_matrix(DN_WIDTH, DN_HEAD_DIM)
    za, zb, zg, glu_in, gates = _in_proj(h2, pw["norm_mix"], pw["w_in"], hm_att, pw["hgain"], t["tm"])
    o_a = _attention(za, pw["sink"], bsz, seqlen)
    y, gb = _dn_prep(zb, gates, pw["dn_conv"], hs_dn, pw["aneg"], pw["dtb"], bsz, seqlen, t["tl"])
    o_dn = _dn_chunk(y, gb, bsz, seqlen, t["ch"])
    o_c = _conformer_conv(glu_in, pw["cv_dw"], pw["cv_dw_bias"], pw["cv_ln_gain"], pw["cv_ln_bias"],
                          bsz, seqlen, t["tcv"])
    h2 = _out_proj(h2, o_a, o_dn, zg, o_c, pw["dn_out_gain"], hm_dn, pw["w_out"], t["tm"])
    return _moe_ple(h2, p2, pw, t)


def _trunk(x, p, layer_weights):
    bsz, seqlen, _ = x.shape
    h2 = x.reshape(bsz * seqlen, D_MODEL)
    for i, pw in enumerate(layer_weights):
        h2 = _layer(h2, p[i].reshape(bsz * seqlen, PLE_DIM), pw, bsz, seqlen)
    return h2.reshape(bsz, seqlen, D_MODEL)


def kernel(x_prompt, x_sample, p_prompt, p_sample, norm_mix, w_in, q_gain, k_gain, sink, dn_conv, dn_a_log,
           dn_dt_bias, dn_out_gain, cv_dw, cv_dw_bias, cv_ln_gain, cv_ln_bias, w_out, norm_ffn, w_router,
           w_gate, w_up, w_down, norm_ple, w_ple_gate, w_ple_proj):
    weights = (norm_mix, w_in, q_gain, k_gain, sink, dn_conv, dn_a_log, dn_dt_bias, dn_out_gain,
               cv_dw, cv_dw_bias, cv_ln_gain, cv_ln_bias, w_out, norm_ffn, w_router, w_gate, w_up, w_down,
               norm_ple, w_ple_gate, w_ple_proj)
    depth = w_in.shape[0]
    layer_weights = [_prep_layer([w[i] for w in weights]) for i in range(depth)]
    return (_trunk(x_prompt, p_prompt, layer_weights), _trunk(x_sample, p_sample, layer_weights))
```

```python
import functools

import numpy as np
import jax
import jax.numpy as jnp
from jax import lax
from jax.experimental import pallas as pl
from jax.experimental.pallas import tpu as pltpu

F32 = jnp.float32
BF16 = jnp.bfloat16

D_MODEL = 1024
ATT_HEADS = 8
ATT_KV_HEADS = 2
ATT_HEAD_DIM = 64
ATT_GROUP = ATT_HEADS // ATT_KV_HEADS
WINDOW = 128
ATT_BLOCK = 128
DN_HEADS = 4
DN_HEAD_DIM = 64
DN_WIDTH = DN_HEADS * DN_HEAD_DIM
DN_CHUNK = 64
CONV_CH = 256
CONV_WIDTH = 31
ATT_Q = ATT_HEADS * ATT_HEAD_DIM
ATT_KV = ATT_KV_HEADS * ATT_HEAD_DIM
N_EXPERTS = 16
CAPACITY_FACTOR = 2
EXPERT_FF = 1024
PLE_DIM = 256
NORM_EPS = 1e-6

LANES = 128
SUBLANES = 8
VMEM_LIMIT = 48 * 1024 * 1024

ZA = ATT_Q + 2 * ATT_KV
ZB = 3 * DN_WIDTH
ZW = ZA + ZB + DN_WIDTH + 2 * CONV_CH + LANES


def _params(sem):
    return pltpu.CompilerParams(dimension_semantics=sem, vmem_limit_bytes=VMEM_LIMIT)


def _head_mean_matrix(width, head):
    idx = np.arange(width) // head
    return jnp.asarray((idx[:, None] == idx[None, :]).astype(np.float32) / head, dtype=BF16)


def _head_sum_matrix(width, head):
    idx = np.arange(width) // head
    return jnp.asarray((idx[:, None] == idx[None, :]).astype(np.float32), dtype=BF16)


def _sigmoid(x):
    return 1.0 / (1.0 + jnp.exp(-x))


def _silu(x):
    return x * _sigmoid(x)


def _in_proj_kernel(x_ref, gain_ref, w_ref, hm_ref, hgain_ref, za_ref, zb_ref, zg_ref, glu_ref, gates_ref):
    x = x_ref[...]
    ms = jnp.mean(x * x, axis=-1, keepdims=True)
    a = (x * lax.rsqrt(ms + NORM_EPS) * gain_ref[...]).astype(BF16)
    z = jnp.dot(a, w_ref[...], preferred_element_type=F32)
    nqk = ATT_Q + ATT_KV
    qk = z[:, :nqk]
    hms = jnp.dot((qk * qk).astype(BF16), hm_ref[...], preferred_element_type=F32)
    za_ref[:, :nqk] = qk * lax.rsqrt(hms + NORM_EPS) * hgain_ref[...]
    za_ref[:, nqk:] = z[:, nqk:ZA]
    zb_ref[...] = z[:, ZA:ZA + ZB]
    zg_ref[...] = z[:, ZA + ZB:ZA + ZB + DN_WIDTH]
    glu_ref[...] = z[:, ZA + ZB + DN_WIDTH:ZA + ZB + DN_WIDTH + 2 * CONV_CH]
    gates_ref[...] = z[:, ZW - LANES:]


def _in_proj(h2, gain, w_perm, hm, hgain, tm):
    n = h2.shape[0]
    row = lambda i: (i, 0)
    fixed = lambda i: (0, 0)
    return pl.pallas_call(
        _in_proj_kernel,
        grid=(n // tm,),
        in_specs=[pl.BlockSpec((tm, D_MODEL), row), pl.BlockSpec((1, D_MODEL), fixed),
                  pl.BlockSpec((D_MODEL, ZW), fixed), pl.BlockSpec(hm.shape, fixed),
                  pl.BlockSpec(hgain.shape, fixed)],
        out_specs=[pl.BlockSpec((tm, ZA), row), pl.BlockSpec((tm, ZB), row), pl.BlockSpec((tm, DN_WIDTH), row),
                   pl.BlockSpec((tm, 2 * CONV_CH), row), pl.BlockSpec((tm, LANES), row)],
        out_shape=[jax.ShapeDtypeStruct((n, ZA), F32), jax.ShapeDtypeStruct((n, ZB), F32),
                   jax.ShapeDtypeStruct((n, DN_WIDTH), F32), jax.ShapeDtypeStruct((n, 2 * CONV_CH), F32),
                   jax.ShapeDtypeStruct((n, LANES), F32)],
        compiler_params=_params(("parallel",)),
        name="in_proj",
    )(h2, gain, w_perm, hm, hgain)


def _attn_kernel(sink_ref, q_ref, kvp_ref, kvo_ref, kvn_ref, o_ref, *, seq_len):
    n = pl.program_id(1)
    kv = jnp.concatenate([kvp_ref[...], kvo_ref[...], kvn_ref[...]], axis=0)
    row = lax.broadcasted_iota(jnp.int32, (ATT_BLOCK, 3 * ATT_BLOCK), 0)
    col = lax.broadcasted_iota(jnp.int32, (ATT_BLOCK, 3 * ATT_BLOCK), 1)
    rel = col - ATT_BLOCK - row
    kpos = n * ATT_BLOCK - ATT_BLOCK + col
    valid = (jnp.abs(rel) <= WINDOW) & (kpos >= 0) & (kpos < seq_len)
    dist = jnp.abs(rel).astype(F32)
    for g in range(ATT_KV_HEADS):
        kg = kv[:, g * ATT_HEAD_DIM:(g + 1) * ATT_HEAD_DIM].astype(BF16)
        vg = kv[:, ATT_KV + g * ATT_HEAD_DIM:ATT_KV + (g + 1) * ATT_HEAD_DIM].astype(BF16)
        for j in range(ATT_GROUP):
            hd = g * ATT_GROUP + j
            slope = float(2.0 ** (-8.0 * (hd + 1) / ATT_HEADS))
            qh = q_ref[:, hd * ATT_HEAD_DIM:(hd + 1) * ATT_HEAD_DIM].astype(BF16)
            s = lax.dot_general(qh, kg, (((1,), (1,)), ((), ())), preferred_element_type=F32)
            s = jnp.where(valid, s - slope * dist, -jnp.inf)
            sink = sink_ref[hd]
            m = jnp.maximum(jnp.max(s, axis=-1, keepdims=True), sink)
            e = jnp.exp(s - m)
            denom = jnp.sum(e, axis=-1, keepdims=True) + jnp.exp(sink - m)
            p = (e / denom).astype(BF16)
            o_ref[:, hd * ATT_HEAD_DIM:(hd + 1) * ATT_HEAD_DIM] = jnp.dot(p, vg, preferred_element_type=F32)


def _attention(za, sink, bsz, seqlen):
    nb = seqlen // ATT_BLOCK
    za3 = za.reshape(bsz, seqlen, ZA)
    kvw = 2 * ATT_KV
    kvc = ATT_Q // kvw
    return pl.pallas_call(
        functools.partial(_attn_kernel, seq_len=seqlen),
        grid=(bsz, nb),
        in_specs=[pl.BlockSpec(memory_space=pltpu.SMEM),
                  pl.BlockSpec((None, ATT_BLOCK, ATT_Q), lambda b, n: (b, n, 0)),
                  pl.BlockSpec((None, ATT_BLOCK, kvw), lambda b, n: (b, jnp.maximum(n - 1, 0), kvc)),
                  pl.BlockSpec((None, ATT_BLOCK, kvw), lambda b, n: (b, n, kvc)),
                  pl.BlockSpec((None, ATT_BLOCK, kvw), lambda b, n: (b, jnp.minimum(n + 1, nb - 1), kvc))],
        out_specs=pl.BlockSpec((None, ATT_BLOCK, ATT_Q), lambda b, n: (b, n, 0)),
        out_shape=jax.ShapeDtypeStruct((bsz, seqlen, ATT_Q), F32),
        compiler_params=_params(("parallel", "parallel")),
        name="window_attention",
    )(sink, za3, za3, za3, za3).reshape(bsz * seqlen, ATT_Q)


DN_HALO = SUBLANES


def _dn_prep_kernel(x_ref, xp_ref, xn_ref, cw_ref, hs_ref, g_ref, aneg_ref, dtb_ref, mf_ref, mb_ref,
                    y_ref, gb_ref, buf_ref, *, tl):
    i = pl.program_id(1)
    nt = pl.num_programs(1)
    buf_ref[0:DN_HALO, :] = jnp.where(i > 0, xp_ref[...], 0.0)
    buf_ref[DN_HALO:DN_HALO + tl, :] = x_ref[...]
    buf_ref[DN_HALO + tl:, :] = jnp.where(i < nt - 1, xn_ref[...], 0.0)
    y = (cw_ref[0:1, :] * buf_ref[DN_HALO - 1:DN_HALO - 1 + tl, :]
         + cw_ref[1:2, :] * buf_ref[DN_HALO:DN_HALO + tl, :]
         + cw_ref[2:3, :] * buf_ref[DN_HALO + 1:DN_HALO + 1 + tl, :])
    y = _silu(y)
    qk = y[:, :2 * DN_WIDTH]
    ss = jnp.dot((qk * qk).astype(BF16), hs_ref[...], preferred_element_type=F32)
    lane = lax.broadcasted_iota(jnp.int32, (tl, 2 * DN_WIDTH), 1)
    scale = jnp.where(lane < DN_WIDTH, DN_HEAD_DIM ** -0.5, 1.0)
    y_ref[:, :2 * DN_WIDTH] = qk * lax.rsqrt(ss + NORM_EPS) * scale
    y_ref[:, 2 * DN_WIDTH:] = y[:, 2 * DN_WIDTH:]
    raw = g_ref[...]
    col = lax.broadcasted_iota(jnp.int32, (tl, LANES), 1)
    is_beta = (col & DN_HEADS) == 0
    t = raw + dtb_ref[...]
    softplus = jnp.maximum(t, 0.0) + jnp.log(1.0 + jnp.exp(-jnp.abs(t)))
    vals = jnp.where(is_beta, _sigmoid(raw), aneg_ref[...] * softplus)
    cf = jnp.dot(mf_ref[...], vals, preferred_element_type=F32, precision=lax.Precision.HIGHEST)
    cb = jnp.dot(mb_ref[...], vals, preferred_element_type=F32, precision=lax.Precision.HIGHEST)
    gb_ref[0] = jnp.where(is_beta, vals, cf)
    gb_ref[1] = pltpu.roll(jnp.where(is_beta, vals, cb), LANES - 2 * DN_HEADS, axis=1)


def _dn_prep(zb, gates, conv_w, hs, aneg, dtb, bsz, seqlen, tl):
    zb3 = zb.reshape(bsz, seqlen, ZB)
    g3 = gates.reshape(bsz, seqlen, LANES)
    nt = seqlen // tl
    hb = tl // DN_HALO
    ch = np.arange(tl) // DN_CHUNK
    same = ch[:, None] == ch[None, :]
    pos = np.arange(tl)
    mf = jnp.asarray((same & (pos[None, :] <= pos[:, None])).astype(np.float32))
    mb = jnp.asarray((same & (pos[None, :] >= pos[:, None])).astype(np.float32))
    fixed = lambda b, i: (0, 0)
    y, gb = pl.pallas_call(
        functools.partial(_dn_prep_kernel, tl=tl),
        grid=(bsz, nt),
        in_specs=[pl.BlockSpec((None, tl, ZB), lambda b, i: (b, i, 0)),
                  pl.BlockSpec((None, DN_HALO, ZB), lambda b, i: (b, jnp.maximum(i * hb - 1, 0), 0)),
                  pl.BlockSpec((None, DN_HALO, ZB), lambda b, i: (b, jnp.minimum((i + 1) * hb, nt * hb - 1), 0)),
                  pl.BlockSpec(conv_w.shape, fixed), pl.BlockSpec(hs.shape, fixed),
                  pl.BlockSpec((None, tl, LANES), lambda b, i: (b, i, 0)),
                  pl.BlockSpec((1, LANES), fixed), pl.BlockSpec((1, LANES), fixed),
                  pl.BlockSpec((tl, tl), fixed), pl.BlockSpec((tl, tl), fixed)],
        out_specs=[pl.BlockSpec((None, tl, ZB), lambda b, i: (b, i, 0)),
                   pl.BlockSpec((2, None, tl, LANES), lambda b, i: (0, b, i, 0))],
        out_shape=[jax.ShapeDtypeStruct((bsz, seqlen, ZB), F32),
                   jax.ShapeDtypeStruct((2, bsz, seqlen, LANES), F32)],
        scratch_shapes=[pltpu.VMEM((tl + 2 * DN_HALO, ZB), F32)],
        compiler_params=_params(("parallel", "parallel")),
        name="deltanet_prep",
    )(zb3, zb3, zb3, conv_w, hs, g3, aneg, dtb, mf, mb)
    return y, gb


def _lane_expand(cols, first):
    c = cols.shape[0]
    lane = lax.broadcasted_iota(jnp.int32, (c, LANES), 1)
    halves = []
    for h in range(0, DN_HEADS, 2):
        a = jnp.broadcast_to(cols[:, first + h:first + h + 1], (c, LANES))
        b = jnp.broadcast_to(cols[:, first + h + 1:first + h + 2], (c, LANES))
        halves.append(jnp.where(lane < DN_HEAD_DIM, a, b))
    return jnp.concatenate(halves, axis=1)


def _dn_pair_kernel(xf_ref, xb_ref, gf_ref, gb_ref, of_ref, ob_ref, sf_ref, sb_ref, *, nsub):
    c = DN_CHUNK
    w = DN_WIDTH

    @pl.when(pl.program_id(1) == 0)
    def _():
        sf_ref[...] = jnp.zeros_like(sf_ref)
        sb_ref[...] = jnp.zeros_like(sb_ref)

    r_cat = lax.broadcasted_iota(jnp.int32, (c, w), 0)
    s_cat = lax.broadcasted_iota(jnp.int32, (c, w), 1) & (DN_HEAD_DIM - 1)
    eye_cat = s_cat == r_cat
    rr = lax.broadcasted_iota(jnp.int32, (w, w), 0)
    cc = lax.broadcasted_iota(jnp.int32, (w, w), 1)
    head = (rr >> 6) == (cc >> 6)
    m16 = (rr >> 4) == (cc >> 4)
    m32 = (rr >> 5) == (cc >> 5)
    off16 = m32 & jnp.logical_not(m16)
    off32 = head & jnp.logical_not(m32)
    eye = (rr == cc).astype(F32)

    def tile4(t):
        return jnp.concatenate([t] * DN_HEADS, axis=0)

    def bd(t):
        return jnp.where(head, tile4(t), 0.0)

    def mm(a, b):
        return jnp.dot(a.astype(BF16), b.astype(BF16), preferred_element_type=F32)

    chunks = [(0, i * c) for i in range(nsub)] + [(1, (nsub - 1 - i) * c) for i in range(nsub)]
    xrefs = (xf_ref, xb_ref)
    grefs = (gf_ref, gb_ref)
    orefs = (of_ref, ob_ref)
    srefs = (sf_ref, sb_ref)
    incl = (s_cat <= r_cat, s_cat >= r_cat)
    strict = (s_cat < r_cat, s_cat > r_cat)
    last_row = (c - 1, 0)

    pre = []
    for d, st in chunks:
        x = xrefs[d][st:st + c, :]
        q, k, v = x[:, :w], x[:, w:2 * w], x[:, 2 * w:]
        gbt = grefs[d][st:st + c, :]
        beta = _lane_expand(gbt, 0)
        gc = _lane_expand(gbt, DN_HEADS)
        grow = jnp.sum(jnp.where(eye_cat, gc, 0.0), axis=0, keepdims=True)
        decay = jnp.exp(jnp.where(incl[d], gc - grow, -jnp.inf))
        glast = gc[last_row[d]:last_row[d] + 1, :]
        egc = jnp.exp(gc)
        kb = k * beta
        pre.append(dict(d=d, st=st, q=q, k=k, kb=kb, vb=v * beta, decay=decay, glast=glast, egc=egc,
                        kdec=(k * jnp.exp(glast - gc)).astype(BF16)))

    kks = [lax.dot_general(jnp.concatenate([p["kb"], p["q"]], axis=0).astype(BF16), bd(p["k"]).astype(BF16),
                           (((1,), (1,)), ((), ())), preferred_element_type=F32) for p in pre]
    a4 = [tile4(jnp.where(strict[p["d"]], kk[:c] * p["decay"], 0.0)) for p, kk in zip(pre, kks)]
    intra = [jnp.where(incl[p["d"]], kk[c:] * p["decay"], 0.0).astype(BF16) for p, kk in zip(pre, kks)]
    xm = [jnp.where(m16, -t, 0.0).astype(BF16) for t in a4]
    x2 = [mm(t, t).astype(BF16) for t in xm]
    dinv = [eye + t.astype(F32) for t in xm]
    dinv = [t + mm(t, s2) for t, s2 in zip(dinv, x2)]
    x4 = [mm(t, t).astype(BF16) for t in x2]
    dinv = [t + mm(t, s4) for t, s4 in zip(dinv, x4)]
    x8 = [mm(t, t) for t in x4]
    dinv = [t + mm(t, s8) for t, s8 in zip(dinv, x8)]
    dinv_b = [t.astype(BF16) for t in dinv]
    n32 = [mm(jnp.where(off16, t, 0.0), db) for t, db in zip(a4, dinv_b)]
    t32 = [t - mm(db, n) for t, db, n in zip(dinv, dinv_b, n32)]
    t32_b = [t.astype(BF16) for t in t32]
    n64 = [mm(jnp.where(off32, t, 0.0), tb) for t, tb in zip(a4, t32_b)]
    t64 = [t - mm(tb, n) for t, tb, n in zip(t32, t32_b, n64)]
    t_cat = [(t[0:c] + t[c:2 * c] + t[2 * c:3 * c] + t[3 * c:4 * c]).astype(BF16) for t in t64]
    uw = [mm(tc, jnp.concatenate([bd(p["vb"]), bd(p["kb"] * p["egc"])], axis=1)) for tc, p in zip(t_cat, pre)]
    uw_b = [t.astype(BF16) for t in uw]
    pn = [lax.dot_general(p["kdec"], t, (((0,), (0,)), ((), ())), preferred_element_type=F32)
          for p, t in zip(pre, uw_b)]
    qo = [jnp.dot(it, jnp.concatenate([bd(t[:, :w]), bd(t[:, w:])], axis=1).astype(BF16), preferred_element_type=F32)
          for it, t in zip(intra, uw)]
    lhs = [jnp.concatenate([jnp.where(head, n[:, w:], 0.0), p["q"] * p["egc"] - o[:, w:]], axis=0).astype(BF16)
           for n, o, p in zip(pn, qo, pre)]
    for step in range(nsub):
        for d in range(2):
            i = d * nsub + step
            p = pre[i]
            state = srefs[d][...]
            r = jnp.dot(lhs[i], state.astype(BF16), preferred_element_type=F32)
            orefs[d][p["st"]:p["st"] + c, :] = r[w:] + qo[i][:, :w]
            srefs[d][...] = state * jnp.exp(p["glast"]) - r[:w] + jnp.where(head, pn[i][:, :w], 0.0)


def _dn_chunk(y, gb, bsz, seqlen, ch):
    nsub = ch // DN_CHUNK
    nblk = seqlen // ch
    fwd = lambda b, j: (b, j, 0)
    bwd = lambda b, j: (b, nblk - 1 - j, 0)
    o_f, o_b = pl.pallas_call(
        functools.partial(_dn_pair_kernel, nsub=nsub),
        grid=(bsz, nblk),
        in_specs=[pl.BlockSpec((None, ch, ZB), fwd), pl.BlockSpec((None, ch, ZB), bwd),
                  pl.BlockSpec((None, None, ch, LANES), lambda b, j: (0, b, j, 0)),
                  pl.BlockSpec((None, None, ch, LANES), lambda b, j: (1, b, nblk - 1 - j, 0))],
        out_specs=[pl.BlockSpec((None, ch, DN_WIDTH), fwd), pl.BlockSpec((None, ch, DN_WIDTH), bwd)],
        out_shape=[jax.ShapeDtypeStruct((bsz, seqlen, DN_WIDTH), F32)] * 2,
        scratch_shapes=[pltpu.VMEM((DN_WIDTH, DN_WIDTH), F32)] * 2,
        compiler_params=_params(("parallel", "arbitrary")),
        name="deltanet_chunks",
    )(y, y, gb, gb)
    return o_f.reshape(bsz * seqlen, DN_WIDTH), o_b.reshape(bsz * seqlen, DN_WIDTH)


CV_HALO = 2 * SUBLANES
CV_PAD = (CONV_WIDTH - 1) // 2


def _conv_kernel(x_ref, xp_ref, xn_ref, dw_ref, bias_ref, lng_ref, lnb_ref, o_ref, buf_ref, *, tl):
    i = pl.program_id(1)
    nt = pl.num_programs(1)

    def glu(t):
        return t[:, :CONV_CH] * _sigmoid(t[:, CONV_CH:])

    buf_ref[0:CV_HALO, :] = jnp.where(i > 0, glu(xp_ref[...]), 0.0)
    buf_ref[CV_HALO:CV_HALO + tl, :] = glu(x_ref[...])
    buf_ref[CV_HALO + tl:, :] = jnp.where(i < nt - 1, glu(xn_ref[...]), 0.0)
    acc = jnp.zeros((tl, CONV_CH), F32) + bias_ref[...]
    for j in range(CONV_WIDTH):
        off = CV_HALO - CV_PAD + j
        acc = acc + dw_ref[j:j + 1, :] * buf_ref[off:off + tl, :]
    mu = jnp.mean(acc, axis=-1, keepdims=True)
    cen = acc - mu
    var = jnp.mean(cen * cen, axis=-1, keepdims=True)
    o_ref[...] = _silu(cen * lax.rsqrt(var + NORM_EPS) * lng_ref[...] + lnb_ref[...])


def _conformer_conv(glu_in, dw, bias, lng, lnb, bsz, seqlen, tl):
    x3 = glu_in.reshape(bsz, seqlen, 2 * CONV_CH)
    nt = seqlen // tl
    hb = tl // CV_HALO
    fixed = lambda b, i: (0, 0)
    return pl.pallas_call(
        functools.partial(_conv_kernel, tl=tl),
        grid=(bsz, nt),
        in_specs=[pl.BlockSpec((None, tl, 2 * CONV_CH), lambda b, i: (b, i, 0)),
                  pl.BlockSpec((None, CV_HALO, 2 * CONV_CH), lambda b, i: (b, jnp.maximum(i * hb - 1, 0), 0)),
                  pl.BlockSpec((None, CV_HALO, 2 * CONV_CH),
                               lambda b, i: (b, jnp.minimum((i + 1) * hb, nt * hb - 1), 0)),
                  pl.BlockSpec(dw.shape, fixed), pl.BlockSpec((1, CONV_CH), fixed),
                  pl.BlockSpec((1, CONV_CH), fixed), pl.BlockSpec((1, CONV_CH), fixed)],
        out_specs=pl.BlockSpec((None, tl, CONV_CH), lambda b, i: (b, i, 0)),
        out_shape=jax.ShapeDtypeStruct((bsz, seqlen, CONV_CH), F32),
        scratch_shapes=[pltpu.VMEM((tl + 2 * CV_HALO, CONV_CH), F32)],
        compiler_params=_params(("parallel", "parallel")),
        name="conformer_conv",
    )(x3, x3, x3, dw, bias, lng, lnb).reshape(bsz * seqlen, CONV_CH)


def _out_proj_kernel(h_ref, oa_ref, of_ref, ob_ref, zg_ref, oc_ref, og_ref, hm_ref, w_ref, out_ref):
    ob = of_ref[...] + ob_ref[...]
    ms = jnp.dot((ob * ob).astype(BF16), hm_ref[...], preferred_element_type=F32)
    obn = ob * lax.rsqrt(ms + NORM_EPS) * og_ref[...]
    ob2 = obn * _silu(zg_ref[...])
    mix = jnp.concatenate([oa_ref[...], ob2, oc_ref[...]], axis=1).astype(BF16)
    out_ref[...] = h_ref[...] + jnp.dot(mix, w_ref[...], preferred_element_type=F32)


def _out_proj(h2, oa, o_f, o_b, zg, oc, og, hm, w, tm):
    n = h2.shape[0]
    row = lambda i: (i, 0)
    fixed = lambda i: (0, 0)
    return pl.pallas_call(
        _out_proj_kernel,
        grid=(n // tm,),
        in_specs=[pl.BlockSpec((tm, D_MODEL), row), pl.BlockSpec((tm, ATT_Q), row),
                  pl.BlockSpec((tm, DN_WIDTH), row), pl.BlockSpec((tm, DN_WIDTH), row),
                  pl.BlockSpec((tm, DN_WIDTH), row),
                  pl.BlockSpec((tm, CONV_CH), row), pl.BlockSpec((1, DN_WIDTH), fixed),
                  pl.BlockSpec(hm.shape, fixed), pl.BlockSpec(w.shape, fixed)],
        out_specs=pl.BlockSpec((tm, D_MODEL), row),
        out_shape=jax.ShapeDtypeStruct((n, D_MODEL), F32),
        compiler_params=_params(("parallel",)),
        name="out_proj",
    )(h2, oa, o_f, o_b, zg, oc, og, hm, w)


def _route_kernel(h_ref, gain_ref, wr_ref, xn_ref, aff_ref):
    x = h_ref[...]
    ms = jnp.mean(x * x, axis=-1, keepdims=True)
    xn = x * lax.rsqrt(ms + NORM_EPS) * gain_ref[...]
    xn_ref[...] = xn.astype(BF16)
    logits = jnp.dot(xn, wr_ref[...], preferred_element_type=F32, precision=lax.Precision.HIGHEST)
    lane = lax.broadcasted_iota(jnp.int32, logits.shape, 1)
    logits = jnp.where(lane < N_EXPERTS, logits, -jnp.inf)
    m = jnp.max(logits, axis=-1, keepdims=True)
    e = jnp.exp(logits - m)
    aff = e / jnp.sum(e, axis=-1, keepdims=True)
    aff_ref[...] = jnp.transpose(aff)[:N_EXPERTS, :]


def _route(h2, gain, wr_pad, tm):
    n = h2.shape[0]
    row = lambda i: (i, 0)
    fixed = lambda i: (0, 0)
    return pl.pallas_call(
        _route_kernel,
        grid=(n // tm,),
        in_specs=[pl.BlockSpec((tm, D_MODEL), row), pl.BlockSpec((1, D_MODEL), fixed),
                  pl.BlockSpec((D_MODEL, LANES), fixed)],
        out_specs=[pl.BlockSpec((tm, D_MODEL), row), pl.BlockSpec((N_EXPERTS, tm), lambda i: (0, i))],
        out_shape=[jax.ShapeDtypeStruct((n, D_MODEL), BF16), jax.ShapeDtypeStruct((N_EXPERTS, n), F32)],
        compiler_params=_params(("parallel",)),
        name="moe_route",
    )(h2, gain, wr_pad)


MOE_TILE = 256
MOE_ALIGN = 2 * SUBLANES
MOE_WIN = 80
MOE_PAD = 512
MOE_UNSELECTED = -64.0


def _select_kernel(aff_ref, tri_ref, val_ref, cnt_ref, *, cap, tile):
    ne, n = aff_ref.shape
    nt = n // tile
    capf = float(cap)

    def bits_of(x):
        return lax.bitcast_convert_type(x, jnp.int32)

    def search(i, thr):
        cand = thr | jnp.left_shift(jnp.int32(1), 30 - i)
        cnt = jnp.sum((bits_of(aff_ref[...]) >= cand).astype(F32), axis=1, keepdims=True)
        return jnp.where(cnt >= capf, cand, thr)

    thr = lax.fori_loop(0, 31, search, jnp.zeros((ne, 1), jnp.int32))
    n_gt = jnp.sum((bits_of(aff_ref[...]) > thr).astype(F32), axis=1, keepdims=True)
    need = capf - n_gt
    lane = lax.broadcasted_iota(jnp.int32, (ne, LANES), 1)

    def tile_body(j, carry):
        eq_before, cnt_acc = carry
        off = pl.multiple_of(j * tile, tile)
        b = bits_of(aff_ref[:, pl.ds(off, tile)])
        gt = b > thr
        eqf = (b == thr).astype(F32)
        eq_rank = eq_before + jnp.dot(eqf.astype(BF16), tri_ref[...], preferred_element_type=F32)
        self_ = jnp.where(gt, 1.0, jnp.where(eq_rank <= need, eqf, 0.0))
        rank = jnp.dot(self_.astype(BF16), tri_ref[...], preferred_element_type=F32)
        val_ref[:, pl.ds(off, tile)] = jnp.where(self_ > 0.0, rank, MOE_UNSELECTED)
        cnt = jnp.sum(self_, axis=1, keepdims=True)
        return (eq_before + jnp.sum(eqf, axis=1, keepdims=True), cnt_acc + jnp.where(lane == j, cnt, 0.0))

    init = (jnp.zeros((ne, 1), F32), jnp.zeros((ne, LANES), F32))
    _, cnt_acc = lax.fori_loop(0, nt, tile_body, init)
    cnt_ref[...] = cnt_acc


def _select(aff_t, cap, tile):
    ne, n = aff_t.shape
    assert n // tile <= LANES
    tri = jnp.asarray(np.triu(np.ones((tile, tile), np.float32)), dtype=BF16)
    return pl.pallas_call(
        functools.partial(_select_kernel, cap=cap, tile=tile),
        out_shape=[jax.ShapeDtypeStruct((ne, n), F32), jax.ShapeDtypeStruct((ne, LANES), F32)],
        compiler_params=pltpu.CompilerParams(vmem_limit_bytes=VMEM_LIMIT),
        name="moe_select",
    )(aff_t, tri)


def _moe_plan(cnt, nt):
    c = cnt[:, :nt].astype(jnp.int32).T
    starts = jnp.concatenate([jnp.zeros((1, N_EXPERTS), jnp.int32), jnp.cumsum(c, axis=0)], axis=0)
    head = starts[:-1] & (MOE_ALIGN - 1)
    kmax = jnp.maximum(jnp.max((head + c + MOE_WIN - 1) // MOE_WIN, axis=1), 1).astype(jnp.int32)
    w = jnp.arange(MOE_WIN, dtype=jnp.int32)
    tgt = (w[None, None, :] + 1 - head[:, :, None]).astype(F32).reshape(nt, 1, N_EXPERTS * MOE_WIN)
    return starts.reshape(-1), kmax, tgt


def _expand_matrix():
    e = np.arange(N_EXPERTS * MOE_WIN) // MOE_WIN
    return jnp.asarray((np.arange(N_EXPERTS)[:, None] == e[None, :]).astype(np.float32), dtype=BF16)


def _slot_onehot(val_ref, eexp_ref):
    return lax.dot_general(val_ref[...].astype(BF16), eexp_ref[...], (((0,), (0,)), ((), ())),
                           preferred_element_type=F32)


def _dispatch_kernel(start_ref, kmax_ref, xn_ref, val_ref, tgt_ref, eexp_ref, xe_ref, stage, carry, sem):
    j = pl.program_id(0)
    nt = pl.num_programs(0)
    slot = lax.rem(j, 2)
    ne = N_EXPERTS

    cap = xe_ref.shape[1] - MOE_PAD

    @pl.when(j == 0)
    def _():
        carry[...] = jnp.zeros_like(carry)
        stage[0, 0:MOE_PAD, :] = jnp.zeros((MOE_PAD, D_MODEL), BF16)
        fills = [pltpu.make_async_copy(stage.at[0, pl.ds(0, MOE_PAD)], xe_ref.at[e, pl.ds(cap, MOE_PAD)], sem.at[0])
                 for e in range(ne)]
        for f in fills:
            f.start()
        for f in fills:
            f.wait()

    def window_copy(sl, e, row0):
        return pltpu.make_async_copy(stage.at[sl, pl.ds(e * MOE_WIN, MOE_WIN)],
                                     xe_ref.at[e, pl.ds(row0, MOE_WIN)], sem.at[sl])

    def wait_windows(sl):
        for e in range(ne):
            window_copy(sl, e, 0).wait()

    rep = _slot_onehot(val_ref, eexp_ref)
    xn = xn_ref[...]
    row = lax.broadcasted_iota(jnp.int32, (MOE_ALIGN, D_MODEL), 0)

    def block(k, _):
        @pl.when(k > 0)
        def _():
            wait_windows(slot)

        lo = k * MOE_WIN
        pt = (rep == tgt_ref[...] + lo.astype(F32)).astype(BF16)
        comp = lax.dot_general(pt, xn, (((0,), (0,)), ((), ())), preferred_element_type=F32)
        stage[slot] = comp.astype(BF16)
        for e in range(ne):
            s = start_ref[j * ne + e]
            head = s & (MOE_ALIGN - 1)
            r0 = e * MOE_WIN

            @pl.when(k == 0)
            def _():
                fresh = stage[slot, r0:r0 + MOE_ALIGN, :]
                kept = carry[e * MOE_ALIGN:(e + 1) * MOE_ALIGN, :]
                stage[slot, r0:r0 + MOE_ALIGN, :] = jnp.where(row < head, kept, fresh)

            nxt = (head + start_ref[(j + 1) * ne + e] - s) & (-MOE_ALIGN)

            @pl.when((nxt >= lo) & (nxt < lo + MOE_WIN))
            def _():
                off = pl.multiple_of(nxt - lo, MOE_ALIGN)
                carry[e * MOE_ALIGN:(e + 1) * MOE_ALIGN, :] = stage[slot, pl.ds(r0 + off, MOE_ALIGN), :]

        @pl.when((k == 0) & (j > 0))
        def _():
            wait_windows(1 - slot)

        for e in range(ne):
            base = pl.multiple_of((start_ref[j * ne + e] & (-MOE_ALIGN)) + lo, MOE_ALIGN)
            window_copy(slot, e, base).start()
        return 0

    lax.fori_loop(0, kmax_ref[j], block, 0)

    @pl.when(j == nt - 1)
    def _():
        wait_windows(slot)


def _dispatch(xn, val, starts, kmax, tgt, eexp, cap, tile):
    n = xn.shape[0]
    nt = n // tile
    rows = N_EXPERTS * MOE_WIN
    return pl.pallas_call(
        _dispatch_kernel,
        grid_spec=pltpu.PrefetchScalarGridSpec(
            num_scalar_prefetch=2, grid=(nt,),
            in_specs=[pl.BlockSpec((tile, D_MODEL), lambda j, s, k: (j, 0)),
                      pl.BlockSpec((N_EXPERTS, tile), lambda j, s, k: (0, j)),
                      pl.BlockSpec((None, 1, rows), lambda j, s, k: (j, 0, 0)),
                      pl.BlockSpec((N_EXPERTS, rows), lambda j, s, k: (0, 0))],
            out_specs=pl.BlockSpec(memory_space=pl.ANY),
            scratch_shapes=[pltpu.VMEM((2, rows, D_MODEL), BF16),
                            pltpu.VMEM((N_EXPERTS * MOE_ALIGN, D_MODEL), BF16),
                            pltpu.SemaphoreType.DMA((2,))]),
        out_shape=jax.ShapeDtypeStruct((N_EXPERTS, cap + MOE_PAD, D_MODEL), BF16),
        compiler_params=_params(("arbitrary",)),
        name="moe_dispatch",
    )(starts, kmax, xn, val, tgt, eexp)


def _expert_kernel(x_ref, wr_ref, wg_ref, wu_ref, wd_ref, y_ref, *, ntile):
    e = pl.program_id(0)
    i = pl.program_id(1)

    @pl.when(i < ntile)
    def _():
        x = x_ref[...]
        logits = (jnp.dot(x, wr_ref[0], preferred_element_type=F32)
                  + jnp.dot(x, wr_ref[1], preferred_element_type=F32))
        lane = lax.broadcasted_iota(jnp.int32, logits.shape, 1)
        logits = jnp.where(lane < N_EXPERTS, logits, -jnp.inf)
        ex = jnp.exp(logits - jnp.max(logits, axis=-1, keepdims=True))
        gate = (jnp.sum(jnp.where(lane == e, ex, 0.0), axis=-1, keepdims=True)
                / jnp.sum(ex, axis=-1, keepdims=True))
        hg = jnp.dot(x, wg_ref[...], preferred_element_type=F32)
        hu = jnp.dot(x, wu_ref[...], preferred_element_type=F32)
        hid = (_silu(hg) * hu).astype(BF16)
        y_ref[...] = (jnp.dot(hid, wd_ref[...], preferred_element_type=F32) * gate).astype(BF16)

    @pl.when(i >= ntile)
    def _():
        y_ref[...] = jnp.zeros_like(y_ref)


def _expert_ffn(xe, wr2, wg, wu, wd, cap, tc):
    ne, rows, _ = xe.shape
    wspec = lambda shape: pl.BlockSpec((None,) + shape, lambda e, i: (e, 0, 0))
    return pl.pallas_call(
        functools.partial(_expert_kernel, ntile=cap // tc),
        grid=(ne, rows // tc),
        in_specs=[pl.BlockSpec((None, tc, D_MODEL), lambda e, i: (e, i, 0)),
                  pl.BlockSpec(wr2.shape, lambda e, i: (0, 0, 0)),
                  wspec((D_MODEL, EXPERT_FF)), wspec((D_MODEL, EXPERT_FF)), wspec((EXPERT_FF, D_MODEL))],
        out_specs=pl.BlockSpec((None, tc, D_MODEL), lambda e, i: (e, i, 0)),
        out_shape=jax.ShapeDtypeStruct((ne, rows, D_MODEL), BF16),
        compiler_params=_params(("parallel", "parallel")),
        name="expert_ffn",
    )(xe, wr2, wg, wu, wd)


def _combine_kernel(start_ref, kmax_ref, h_ref, p_ref, val_ref, tgt_ref, eexp_ref, gain_ref, wg_ref, wp_ref,
                    ye_ref, out_ref, stage, sem):
    j = pl.program_id(0)
    nt = pl.num_programs(0)
    slot = lax.rem(j, 2)
    ne = N_EXPERTS

    def window_copy(sl, e, row0):
        return pltpu.make_async_copy(ye_ref.at[e, pl.ds(row0, MOE_WIN)],
                                     stage.at[sl, pl.ds(e * MOE_WIN, MOE_WIN)], sem.at[sl])

    def fetch(sl, tile_idx, lo):
        for e in range(ne):
            base = pl.multiple_of((start_ref[tile_idx * ne + e] & (-MOE_ALIGN)) + lo, MOE_ALIGN)
            window_copy(sl, e, base).start()

    def wait_windows(sl):
        for e in range(ne):
            window_copy(sl, e, 0).wait()

    @pl.when(j == 0)
    def _():
        fetch(slot, j, 0)

    @pl.when(j + 1 < nt)
    def _():
        fetch(1 - slot, j + 1, 0)

    rep = _slot_onehot(val_ref, eexp_ref)
    wait_windows(slot)
    pt = (rep == tgt_ref[...]).astype(BF16)
    acc = h_ref[...] + jnp.dot(pt, stage[slot], preferred_element_type=F32)

    def extra(k, acc):
        lo = k * MOE_WIN
        fetch(slot, j, lo)
        wait_windows(slot)
        pk = (rep == tgt_ref[...] + lo.astype(F32)).astype(BF16)
        return acc + jnp.dot(pk, stage[slot], preferred_element_type=F32)

    x = lax.fori_loop(1, kmax_ref[j], extra, acc)
    ms = jnp.mean(x * x, axis=-1, keepdims=True)
    xn = (x * lax.rsqrt(ms + NORM_EPS) * gain_ref[...]).astype(BF16)
    gate = _sigmoid(jnp.dot(xn, wg_ref[...], preferred_element_type=F32))
    proj = jnp.dot(p_ref[...].astype(BF16), wp_ref[...], preferred_element_type=F32)
    out_ref[...] = x + gate * proj


def _combine_ple(h2, p2, ye, val, starts, kmax, tgt, eexp, gain, wg, wp, tile):
    n = h2.shape[0]
    nt = n // tile
    rows = N_EXPERTS * MOE_WIN
    fixed = lambda j, s, k: (0, 0)
    return pl.pallas_call(
        _combine_kernel,
        grid_spec=pltpu.PrefetchScalarGridSpec(
            num_scalar_prefetch=2, grid=(nt,),
            in_specs=[pl.BlockSpec((tile, D_MODEL), lambda j, s, k: (j, 0)),
                      pl.BlockSpec((tile, PLE_DIM), lambda j, s, k: (j, 0)),
                      pl.BlockSpec((N_EXPERTS, tile), lambda j, s, k: (0, j)),
                      pl.BlockSpec((None, 1, rows), lambda j, s, k: (j, 0, 0)),
                      pl.BlockSpec((N_EXPERTS, rows), fixed),
                      pl.BlockSpec((1, D_MODEL), fixed), pl.BlockSpec(wg.shape, fixed),
                      pl.BlockSpec(wp.shape, fixed),
                      pl.BlockSpec(memory_space=pl.ANY)],
            out_specs=pl.BlockSpec((tile, D_MODEL), lambda j, s, k: (j, 0)),
            scratch_shapes=[pltpu.VMEM((2, rows, D_MODEL), BF16), pltpu.SemaphoreType.DMA((2,))]),
        out_shape=jax.ShapeDtypeStruct((n, D_MODEL), F32),
        compiler_params=_params(("arbitrary",)),
        name="moe_combine_ple",
    )(starts, kmax, h2, p2, val, tgt, eexp, gain, wg, wp, ye)


def _in_perm():
    o_dnz = ZA + ZB
    o_beta = o_dnz + DN_WIDTH
    o_alpha = o_beta + 2 * DN_HEADS
    o_glu = o_alpha + 2 * DN_HEADS
    cols = list(range(0, o_beta)) + list(range(o_glu, o_glu + 2 * CONV_CH))
    for d in range(2):
        cols += [o_beta + d * DN_HEADS + h for h in range(DN_HEADS)]
        cols += [o_alpha + d * DN_HEADS + h for h in range(DN_HEADS)]
    return np.asarray(cols, dtype=np.int32)


def _prep_layer(lw):
    (norm_mix, w_in, q_gain, k_gain, sink, dn_conv, dn_a_log, dn_dt_bias, dn_out_gain,
     cv_dw, cv_dw_bias, cv_ln_gain, cv_ln_bias, w_out, norm_ffn, w_router, w_gate, w_up, w_down,
     norm_ple, w_ple_gate, w_ple_proj) = lw
    perm = _in_perm()
    w_perm = jnp.pad(w_in[:, perm], ((0, 0), (0, ZW - perm.shape[0]))).astype(BF16)
    hgain = jnp.concatenate([jnp.tile(q_gain, ATT_HEADS) * (ATT_HEAD_DIM ** -0.5),
                             jnp.tile(k_gain, ATT_KV_HEADS)]).reshape(1, -1)
    zeros4 = jnp.zeros((DN_HEADS,), F32)
    aneg = -jnp.exp(dn_a_log.astype(F32))
    aneg_row = jnp.concatenate([zeros4, aneg[0], zeros4, aneg[1]])
    dtb_row = jnp.concatenate([zeros4, dn_dt_bias[0], zeros4, dn_dt_bias[1]])
    pad = lambda r: jnp.pad(r, (0, LANES - r.shape[0])).reshape(1, LANES)
    wr = jnp.pad(w_router.astype(F32), ((0, 0), (0, LANES - N_EXPERTS)))
    wr_hi = wr.astype(BF16)
    wr2 = jnp.stack([wr_hi, (wr - wr_hi.astype(F32)).astype(BF16)])
    return dict(
        w_router2=wr2,
        norm_mix=norm_mix.reshape(1, -1), w_in=w_perm, hgain=hgain, sink=sink.astype(F32),
        dn_conv=dn_conv, aneg=pad(aneg_row), dtb=pad(dtb_row),
        dn_out_gain=jnp.tile(dn_out_gain, DN_HEADS).reshape(1, -1),
        cv_dw=cv_dw, cv_dw_bias=cv_dw_bias.reshape(1, -1), cv_ln_gain=cv_ln_gain.reshape(1, -1),
        cv_ln_bias=cv_ln_bias.reshape(1, -1), w_out=w_out.astype(BF16),
        norm_ffn=norm_ffn.reshape(1, -1), w_router=jnp.pad(w_router, ((0, 0), (0, LANES - N_EXPERTS))),
        w_gate=w_gate.astype(BF16), w_up=w_up.astype(BF16), w_down=w_down.astype(BF16),
        norm_ple=norm_ple.reshape(1, -1), w_ple_gate=w_ple_gate.astype(BF16), w_ple_proj=w_ple_proj.astype(BF16))


def _tiles(bsz, seqlen):
    n = bsz * seqlen
    return dict(tm=min(512, n), tl=min(256, seqlen), ch=min(256, seqlen), tcv=min(512, seqlen))


def _moe_ple(h2, p2, pw, t):
    n = h2.shape[0]
    cap = CAPACITY_FACTOR * n // N_EXPERTS
    tile = min(MOE_TILE, n)
    xn, aff_t = _route(h2, pw["norm_ffn"], pw["w_router"], t["tm"])
    val, cnt = _select(aff_t, cap, tile)
    starts, kmax, tgt = _moe_plan(cnt, n // tile)
    eexp = _expand_matrix()
    xe = _dispatch(xn, val, starts, kmax, tgt, eexp, cap, tile)
    ye = _expert_ffn(xe, pw["w_router2"], pw["w_gate"], pw["w_up"], pw["w_down"], cap, min(512, cap))
    return _combine_ple(h2, p2, ye, val, starts, kmax, tgt, eexp, pw["norm_ple"], pw["w_ple_gate"],
                        pw["w_ple_proj"], tile)


def _layer(h2, p2, pw, bsz, seqlen):
    t = _tiles(bsz, seqlen)
    hm_att = _head_mean_matrix(ATT_Q + ATT_KV, ATT_HEAD_DIM)
    hs_dn = _head_sum_matrix(2 * DN_WIDTH, DN_HEAD_DIM)
    hm_dn = _head_mean_matrix(DN_WIDTH, DN_HEAD_DIM)
    za, zb, zg, glu_in, gates = _in_proj(h2, pw["norm_mix"], pw["w_in"], hm_att, pw["hgain"], t["tm"])
    o_a = _attention(za, pw["sink"], bsz, seqlen)
    y, gb = _dn_prep(zb, gates, pw["dn_conv"], hs_dn, pw["aneg"], pw["dtb"], bsz, seqlen, t["tl"])
    o_f, o_b = _dn_chunk(y, gb, bsz, seqlen, t["ch"])
    o_c = _conformer_conv(glu_in, pw["cv_dw"], pw["cv_dw_bias"], pw["cv_ln_gain"], pw["cv_ln_bias"],
                          bsz, seqlen, t["tcv"])
    h2 = _out_proj(h2, o_a, o_f, o_b, zg, o_c, pw["dn_out_gain"], hm_dn, pw["w_out"], t["tm"])
    return _moe_ple(h2, p2, pw, t)


def _trunk(x, p, layer_weights):
    bsz, seqlen, _ = x.shape
    h2 = x.reshape(bsz * seqlen, D_MODEL)
    for i, pw in enumerate(layer_weights):
        h2 = _layer(h2, p[i].reshape(bsz * seqlen, PLE_DIM), pw, bsz, seqlen)
    return h2.reshape(bsz, seqlen, D_MODEL)


def kernel(x_prompt, x_sample, p_prompt, p_sample, norm_mix, w_in, q_gain, k_gain, sink, dn_conv, dn_a_log,
           dn_dt_bias, dn_out_gain, cv_dw, cv_dw_bias, cv_ln_gain, cv_ln_bias, w_out, norm_ffn, w_router,
           w_gate, w_up, w_down, norm_ple, w_ple_gate, w_ple_proj):
    weights = (norm_mix, w_in, q_gain, k_gain, sink, dn_conv, dn_a_log, dn_dt_bias, dn_out_gain,
               cv_dw, cv_dw_bias, cv_ln_gain, cv_ln_bias, w_out, norm_ffn, w_router, w_gate, w_up, w_down,
               norm_ple, w_ple_gate, w_ple_proj)
    depth = w_in.shape[0]
    layer_weights = [_prep_layer([w[i] for w in weights]) for i in range(depth)]
    return (_trunk(x_prompt, p_prompt, layer_weights), _trunk(x_sample, p_sample, layer_weights))
```

```python
import functools

import numpy as np
import jax
import jax.numpy as jnp
from jax import lax
from jax.experimental import pallas as pl
from jax.experimental.pallas import tpu as pltpu

F32 = jnp.float32
BF16 = jnp.bfloat16

D_MODEL = 1024
ATT_HEADS = 8
ATT_KV_HEADS = 2
ATT_HEAD_DIM = 64
ATT_GROUP = ATT_HEADS // ATT_KV_HEADS
WINDOW = 128
ATT_BLOCK = 128
DN_HEADS = 4
DN_HEAD_DIM = 64
DN_WIDTH = DN_HEADS * DN_HEAD_DIM
DN_CHUNK = 64
CONV_CH = 256
CONV_WIDTH = 31
ATT_Q = ATT_HEADS * ATT_HEAD_DIM
ATT_KV = ATT_KV_HEADS * ATT_HEAD_DIM
N_EXPERTS = 16
CAPACITY_FACTOR = 2
EXPERT_FF = 1024
PLE_DIM = 256
NORM_EPS = 1e-6

LANES = 128
SUBLANES = 8
VMEM_LIMIT = 48 * 1024 * 1024

ZA = ATT_Q + 2 * ATT_KV
ZB = 3 * DN_WIDTH
ZW = ZA + ZB + DN_WIDTH + 2 * CONV_CH + LANES


def _params(sem):
    return pltpu.CompilerParams(dimension_semantics=sem, vmem_limit_bytes=VMEM_LIMIT)


def _head_mean_matrix(width, head):
    idx = np.arange(width) // head
    return jnp.asarray((idx[:, None] == idx[None, :]).astype(np.float32) / head, dtype=BF16)


def _head_sum_matrix(width, head):
    idx = np.arange(width) // head
    return jnp.asarray((idx[:, None] == idx[None, :]).astype(np.float32), dtype=BF16)


def _sigmoid(x):
    return 1.0 / (1.0 + jnp.exp(-x))


def _silu(x):
    return x * _sigmoid(x)


def _in_proj_kernel(x_ref, gain_ref, w_ref, hm_ref, hgain_ref, za_ref, zb_ref, zg_ref, glu_ref, gates_ref):
    x = x_ref[...]
    ms = jnp.mean(x * x, axis=-1, keepdims=True)
    a = (x * lax.rsqrt(ms + NORM_EPS) * gain_ref[...]).astype(BF16)
    z = jnp.dot(a, w_ref[...], preferred_element_type=F32)
    nqk = ATT_Q + ATT_KV
    qk = z[:, :nqk]
    hms = jnp.dot((qk * qk).astype(BF16), hm_ref[...], preferred_element_type=F32)
    za_ref[:, :nqk] = (qk * lax.rsqrt(hms + NORM_EPS) * hgain_ref[...]).astype(BF16)
    za_ref[:, nqk:] = z[:, nqk:ZA].astype(BF16)
    zb_ref[...] = z[:, ZA:ZA + ZB]
    zg_ref[...] = z[:, ZA + ZB:ZA + ZB + DN_WIDTH]
    glu_ref[...] = z[:, ZA + ZB + DN_WIDTH:ZA + ZB + DN_WIDTH + 2 * CONV_CH]
    gates_ref[...] = z[:, ZW - LANES:]


def _in_proj(h2, gain, w_perm, hm, hgain, tm):
    n = h2.shape[0]
    row = lambda i: (i, 0)
    fixed = lambda i: (0, 0)
    return pl.pallas_call(
        _in_proj_kernel,
        grid=(n // tm,),
        in_specs=[pl.BlockSpec((tm, D_MODEL), row), pl.BlockSpec((1, D_MODEL), fixed),
                  pl.BlockSpec((D_MODEL, ZW), fixed), pl.BlockSpec(hm.shape, fixed),
                  pl.BlockSpec(hgain.shape, fixed)],
        out_specs=[pl.BlockSpec((tm, ZA), row), pl.BlockSpec((tm, ZB), row), pl.BlockSpec((tm, DN_WIDTH), row),
                   pl.BlockSpec((tm, 2 * CONV_CH), row), pl.BlockSpec((tm, LANES), row)],
        out_shape=[jax.ShapeDtypeStruct((n, ZA), BF16), jax.ShapeDtypeStruct((n, ZB), F32),
                   jax.ShapeDtypeStruct((n, DN_WIDTH), F32), jax.ShapeDtypeStruct((n, 2 * CONV_CH), F32),
                   jax.ShapeDtypeStruct((n, LANES), F32)],
        compiler_params=_params(("parallel",)),
        name="in_proj",
    )(h2, gain, w_perm, hm, hgain)


ATT_MASKED = -1e30


def _attn_bias_table():
    i = np.arange(ATT_BLOCK)[:, None]
    c = np.arange(3 * ATT_BLOCK)[None, :]
    rel = c - ATT_BLOCK - i
    slopes = 2.0 ** (-8.0 * np.arange(1, ATT_HEADS + 1) / ATT_HEADS)
    table = np.empty((3, ATT_KV_HEADS, ATT_GROUP * ATT_BLOCK, 3 * ATT_BLOCK), np.float32)
    for variant in range(3):
        ok = np.abs(rel) <= WINDOW
        if variant == 0:
            ok = ok & (c >= ATT_BLOCK)
        if variant == 2:
            ok = ok & (c < 2 * ATT_BLOCK)
        for hd in range(ATT_HEADS):
            g, j = divmod(hd, ATT_GROUP)
            table[variant, g, j * ATT_BLOCK:(j + 1) * ATT_BLOCK] = np.where(ok, -slopes[hd] * np.abs(rel), ATT_MASKED)
    return jnp.asarray(table)


def _attn_kernel(sink_ref, q_ref, kvp_ref, kvo_ref, kvn_ref, bias_ref, o_ref):
    kv = jnp.concatenate([kvp_ref[...], kvo_ref[...], kvn_ref[...]], axis=0)
    hd_ = ATT_HEAD_DIM
    groups = range(ATT_KV_HEADS)
    heads = range(ATT_HEADS)
    ks = [kv[:, g * hd_:(g + 1) * hd_].astype(BF16) for g in groups]
    vs = [kv[:, ATT_KV + g * hd_:ATT_KV + (g + 1) * hd_].astype(BF16) for g in groups]
    qs = [jnp.concatenate([q_ref[:, (g * ATT_GROUP + j) * hd_:(g * ATT_GROUP + j + 1) * hd_]
                           for j in range(ATT_GROUP)], axis=0).astype(BF16) for g in groups]
    sg = [lax.dot_general(qs[g], ks[g], (((1,), (1,)), ((), ())), preferred_element_type=F32) + bias_ref[g]
          for g in groups]
    rows = lambda t, hd: t[(hd % ATT_GROUP) * ATT_BLOCK:(hd % ATT_GROUP + 1) * ATT_BLOCK]
    s = [rows(sg[hd // ATT_GROUP], hd) for hd in heads]
    m = [jnp.maximum(jnp.max(s[hd], axis=-1, keepdims=True), sink_ref[hd]) for hd in heads]
    e = [jnp.exp(s[hd] - m[hd]) for hd in heads]
    den = [jnp.sum(e[hd], axis=-1, keepdims=True) + jnp.exp(sink_ref[hd] - m[hd]) for hd in heads]
    eg = [jnp.concatenate([e[g * ATT_GROUP + j].astype(BF16) for j in range(ATT_GROUP)], axis=0) for g in groups]
    og = [jnp.dot(eg[g], vs[g], preferred_element_type=F32) for g in groups]
    for hd in heads:
        o_ref[:, hd * hd_:(hd + 1) * hd_] = rows(og[hd // ATT_GROUP], hd) / den[hd]


def _attention(za, sink, bsz, seqlen):
    nb = seqlen // ATT_BLOCK
    assert nb >= 2
    za3 = za.reshape(bsz, seqlen, ZA)
    kvw = 2 * ATT_KV
    kvc = ATT_Q // kvw
    bias = _attn_bias_table()
    return pl.pallas_call(
        _attn_kernel,
        grid=(bsz, nb),
        in_specs=[pl.BlockSpec(memory_space=pltpu.SMEM),
                  pl.BlockSpec((None, ATT_BLOCK, ATT_Q), lambda b, n: (b, n, 0)),
                  pl.BlockSpec((None, ATT_BLOCK, kvw), lambda b, n: (b, jnp.maximum(n - 1, 0), kvc)),
                  pl.BlockSpec((None, ATT_BLOCK, kvw), lambda b, n: (b, n, kvc)),
                  pl.BlockSpec((None, ATT_BLOCK, kvw), lambda b, n: (b, jnp.minimum(n + 1, nb - 1), kvc)),
                  pl.BlockSpec((None,) + bias.shape[1:],
                               lambda b, n: (jnp.where(n == 0, 0, jnp.where(n == nb - 1, 2, 1)), 0, 0, 0))],
        out_specs=pl.BlockSpec((None, ATT_BLOCK, ATT_Q), lambda b, n: (b, n, 0)),
        out_shape=jax.ShapeDtypeStruct((bsz, seqlen, ATT_Q), F32),
        compiler_params=_params(("parallel", "parallel")),
        name="window_attention",
    )(sink, za3, za3, za3, za3, bias).reshape(bsz * seqlen, ATT_Q)


DN_HALO = SUBLANES


def _dn_prep_kernel(x_ref, xp_ref, xn_ref, cw_ref, hs_ref, g_ref, aneg_ref, dtb_ref, mf_ref, mb_ref,
                    y_ref, gb_ref, buf_ref, *, tl):
    i = pl.program_id(1)
    nt = pl.num_programs(1)
    buf_ref[0:DN_HALO, :] = jnp.where(i > 0, xp_ref[...], 0.0)
    buf_ref[DN_HALO:DN_HALO + tl, :] = x_ref[...]
    buf_ref[DN_HALO + tl:, :] = jnp.where(i < nt - 1, xn_ref[...], 0.0)
    y = (cw_ref[0:1, :] * buf_ref[DN_HALO - 1:DN_HALO - 1 + tl, :]
         + cw_ref[1:2, :] * buf_ref[DN_HALO:DN_HALO + tl, :]
         + cw_ref[2:3, :] * buf_ref[DN_HALO + 1:DN_HALO + 1 + tl, :])
    y = _silu(y)
    qk = y[:, :2 * DN_WIDTH]
    ss = jnp.dot((qk * qk).astype(BF16), hs_ref[...], preferred_element_type=F32)
    lane = lax.broadcasted_iota(jnp.int32, (tl, 2 * DN_WIDTH), 1)
    scale = jnp.where(lane < DN_WIDTH, DN_HEAD_DIM ** -0.5, 1.0)
    y_ref[:, :2 * DN_WIDTH] = qk * lax.rsqrt(ss + NORM_EPS) * scale
    y_ref[:, 2 * DN_WIDTH:] = y[:, 2 * DN_WIDTH:]
    raw = g_ref[...]
    col = lax.broadcasted_iota(jnp.int32, (tl, LANES), 1)
    is_beta = (col & DN_HEADS) == 0
    t = raw + dtb_ref[...]
    softplus = jnp.maximum(t, 0.0) + jnp.log(1.0 + jnp.exp(-jnp.abs(t)))
    vals = jnp.where(is_beta, _sigmoid(raw), aneg_ref[...] * softplus)
    cf = jnp.dot(mf_ref[...], vals, preferred_element_type=F32, precision=lax.Precision.HIGHEST)
    cb = jnp.dot(mb_ref[...], vals, preferred_element_type=F32, precision=lax.Precision.HIGHEST)
    gb_ref[0] = jnp.where(is_beta, vals, cf)
    gb_ref[1] = pltpu.roll(jnp.where(is_beta, vals, cb), LANES - 2 * DN_HEADS, axis=1)


def _dn_prep(zb, gates, conv_w, hs, aneg, dtb, bsz, seqlen, tl):
    zb3 = zb.reshape(bsz, seqlen, ZB)
    g3 = gates.reshape(bsz, seqlen, LANES)
    nt = seqlen // tl
    hb = tl // DN_HALO
    ch = np.arange(tl) // DN_CHUNK
    same = ch[:, None] == ch[None, :]
    pos = np.arange(tl)
    mf = jnp.asarray((same & (pos[None, :] <= pos[:, None])).astype(np.float32))
    mb = jnp.asarray((same & (pos[None, :] >= pos[:, None])).astype(np.float32))
    fixed = lambda b, i: (0, 0)
    y, gb = pl.pallas_call(
        functools.partial(_dn_prep_kernel, tl=tl),
        grid=(bsz, nt),
        in_specs=[pl.BlockSpec((None, tl, ZB), lambda b, i: (b, i, 0)),
                  pl.BlockSpec((None, DN_HALO, ZB), lambda b, i: (b, jnp.maximum(i * hb - 1, 0), 0)),
                  pl.BlockSpec((None, DN_HALO, ZB), lambda b, i: (b, jnp.minimum((i + 1) * hb, nt * hb - 1), 0)),
                  pl.BlockSpec(conv_w.shape, fixed), pl.BlockSpec(hs.shape, fixed),
                  pl.BlockSpec((None, tl, LANES), lambda b, i: (b, i, 0)),
                  pl.BlockSpec((1, LANES), fixed), pl.BlockSpec((1, LANES), fixed),
                  pl.BlockSpec((tl, tl), fixed), pl.BlockSpec((tl, tl), fixed)],
        out_specs=[pl.BlockSpec((None, tl, ZB), lambda b, i: (b, i, 0)),
                   pl.BlockSpec((2, None, tl, LANES), lambda b, i: (0, b, i, 0))],
        out_shape=[jax.ShapeDtypeStruct((bsz, seqlen, ZB), F32),
                   jax.ShapeDtypeStruct((2, bsz, seqlen, LANES), F32)],
        scratch_shapes=[pltpu.VMEM((tl + 2 * DN_HALO, ZB), F32)],
        compiler_params=_params(("parallel", "parallel")),
        name="deltanet_prep",
    )(zb3, zb3, zb3, conv_w, hs, g3, aneg, dtb, mf, mb)
    return y, gb


def _lane_expand(cols, first):
    c = cols.shape[0]
    lane = lax.broadcasted_iota(jnp.int32, (c, LANES), 1)
    halves = []
    for h in range(0, DN_HEADS, 2):
        a = jnp.broadcast_to(cols[:, first + h:first + h + 1], (c, LANES))
        b = jnp.broadcast_to(cols[:, first + h + 1:first + h + 2], (c, LANES))
        halves.append(jnp.where(lane < DN_HEAD_DIM, a, b))
    return jnp.concatenate(halves, axis=1)


def _dn_pair_kernel(xf_ref, xb_ref, gf_ref, gb_ref, of_ref, ob_ref, sf_ref, sb_ref, *, nsub):
    c = DN_CHUNK
    w = DN_WIDTH

    @pl.when(pl.program_id(1) == 0)
    def _():
        sf_ref[...] = jnp.zeros_like(sf_ref)
        sb_ref[...] = jnp.zeros_like(sb_ref)

    r_cat = lax.broadcasted_iota(jnp.int32, (c, w), 0)
    s_cat = lax.broadcasted_iota(jnp.int32, (c, w), 1) & (DN_HEAD_DIM - 1)
    eye_cat = s_cat == r_cat
    rr = lax.broadcasted_iota(jnp.int32, (w, w), 0)
    cc = lax.broadcasted_iota(jnp.int32, (w, w), 1)
    head = (rr >> 6) == (cc >> 6)
    m16 = (rr >> 4) == (cc >> 4)
    m32 = (rr >> 5) == (cc >> 5)
    off16 = m32 & jnp.logical_not(m16)
    off32 = head & jnp.logical_not(m32)
    eye = (rr == cc).astype(F32)

    def tile4(t):
        return jnp.concatenate([t] * DN_HEADS, axis=0)

    def bd(t):
        return jnp.where(head, tile4(t), 0.0)

    def mm(a, b):
        return jnp.dot(a.astype(BF16), b.astype(BF16), preferred_element_type=F32)

    chunks = [(0, i * c) for i in range(nsub)] + [(1, (nsub - 1 - i) * c) for i in range(nsub)]
    xrefs = (xf_ref, xb_ref)
    grefs = (gf_ref, gb_ref)
    orefs = (of_ref, ob_ref)
    srefs = (sf_ref, sb_ref)
    incl = (s_cat <= r_cat, s_cat >= r_cat)
    strict = (s_cat < r_cat, s_cat > r_cat)
    last_row = (c - 1, 0)

    pre = []
    for d, st in chunks:
        x = xrefs[d][st:st + c, :]
        q, k, v = x[:, :w], x[:, w:2 * w], x[:, 2 * w:]
        gbt = grefs[d][st:st + c, :]
        beta = _lane_expand(gbt, 0)
        gc = _lane_expand(gbt, DN_HEADS)
        grow = jnp.sum(jnp.where(eye_cat, gc, 0.0), axis=0, keepdims=True)
        decay = jnp.exp(jnp.where(incl[d], gc - grow, -jnp.inf))
        glast = gc[last_row[d]:last_row[d] + 1, :]
        egc = jnp.exp(gc)
        kb = k * beta
        pre.append(dict(d=d, st=st, q=q, k=k, kb=kb, vb=v * beta, decay=decay, glast=glast, egc=egc,
                        kdec=(k * jnp.exp(glast - gc)).astype(BF16)))

    kks = [lax.dot_general(jnp.concatenate([p["kb"], p["q"]], axis=0).astype(BF16), bd(p["k"]).astype(BF16),
                           (((1,), (1,)), ((), ())), preferred_element_type=F32) for p in pre]
    a4 = [tile4(jnp.where(strict[p["d"]], kk[:c] * p["decay"], 0.0)) for p, kk in zip(pre, kks)]
    intra = [jnp.where(incl[p["d"]], kk[c:] * p["decay"], 0.0).astype(BF16) for p, kk in zip(pre, kks)]
    xm = [jnp.where(m16, -t, 0.0).astype(BF16) for t in a4]
    x2 = [mm(t, t).astype(BF16) for t in xm]
    dinv = [eye + t.astype(F32) for t in xm]
    dinv = [t + mm(t, s2) for t, s2 in zip(dinv, x2)]
    x4 = [mm(t, t).astype(BF16) for t in x2]
    dinv = [t + mm(t, s4) for t, s4 in zip(dinv, x4)]
    x8 = [mm(t, t) for t in x4]
    dinv = [t + mm(t, s8) for t, s8 in zip(dinv, x8)]
    dinv_b = [t.astype(BF16) for t in dinv]
    n32 = [mm(jnp.where(off16, t, 0.0), db) for t, db in zip(a4, dinv_b)]
    t32 = [t - mm(db, n) for t, db, n in zip(dinv, dinv_b, n32)]
    t32_b = [t.astype(BF16) for t in t32]
    n64 = [mm(jnp.where(off32, t, 0.0), tb) for t, tb in zip(a4, t32_b)]
    t64 = [t - mm(tb, n) for t, tb, n in zip(t32, t32_b, n64)]
    t_cat = [(t[0:c] + t[c:2 * c] + t[2 * c:3 * c] + t[3 * c:4 * c]).astype(BF16) for t in t64]
    uw = [mm(tc, jnp.concatenate([bd(p["vb"]), bd(p["kb"] * p["egc"])], axis=1)) for tc, p in zip(t_cat, pre)]
    uw_b = [t.astype(BF16) for t in uw]
    pn = [lax.dot_general(p["kdec"], t, (((0,), (0,)), ((), ())), preferred_element_type=F32)
          for p, t in zip(pre, uw_b)]
    qo = [jnp.dot(it, jnp.concatenate([bd(t[:, :w]), bd(t[:, w:])], axis=1).astype(BF16), preferred_element_type=F32)
          for it, t in zip(intra, uw)]
    lhs = [jnp.concatenate([jnp.where(head, n[:, w:], 0.0), p["q"] * p["egc"] - o[:, w:]], axis=0).astype(BF16)
           for n, o, p in zip(pn, qo, pre)]
    for step in range(nsub):
        for d in range(2):
            i = d * nsub + step
            p = pre[i]
            state = srefs[d][...]
            r = jnp.dot(lhs[i], state.astype(BF16), preferred_element_type=F32)
            orefs[d][p["st"]:p["st"] + c, :] = r[w:] + qo[i][:, :w]
            srefs[d][...] = state * jnp.exp(p["glast"]) - r[:w] + jnp.where(head, pn[i][:, :w], 0.0)


def _dn_chunk(y, gb, bsz, seqlen, ch):
    nsub = ch // DN_CHUNK
    nblk = seqlen // ch
    fwd = lambda b, j: (b, j, 0)
    bwd = lambda b, j: (b, nblk - 1 - j, 0)
    o_f, o_b = pl.pallas_call(
        functools.partial(_dn_pair_kernel, nsub=nsub),
        grid=(bsz, nblk),
        in_specs=[pl.BlockSpec((None, ch, ZB), fwd), pl.BlockSpec((None, ch, ZB), bwd),
                  pl.BlockSpec((None, None, ch, LANES), lambda b, j: (0, b, j, 0)),
                  pl.BlockSpec((None, None, ch, LANES), lambda b, j: (1, b, nblk - 1 - j, 0))],
        out_specs=[pl.BlockSpec((None, ch, DN_WIDTH), fwd), pl.BlockSpec((None, ch, DN_WIDTH), bwd)],
        out_shape=[jax.ShapeDtypeStruct((bsz, seqlen, DN_WIDTH), F32)] * 2,
        scratch_shapes=[pltpu.VMEM((DN_WIDTH, DN_WIDTH), F32)] * 2,
        compiler_params=_params(("parallel", "arbitrary")),
        name="deltanet_chunks",
    )(y, y, gb, gb)
    return o_f.reshape(bsz * seqlen, DN_WIDTH), o_b.reshape(bsz * seqlen, DN_WIDTH)


CV_HALO = 2 * SUBLANES
CV_PAD = (CONV_WIDTH - 1) // 2


def _conv_kernel(x_ref, xp_ref, xn_ref, dw_ref, bias_ref, lng_ref, lnb_ref, o_ref, buf_ref, *, tl):
    i = pl.program_id(1)
    nt = pl.num_programs(1)

    def glu(t):
        return t[:, :CONV_CH] * _sigmoid(t[:, CONV_CH:])

    buf_ref[0:CV_HALO, :] = jnp.where(i > 0, glu(xp_ref[...]), 0.0)
    buf_ref[CV_HALO:CV_HALO + tl, :] = glu(x_ref[...])
    buf_ref[CV_HALO + tl:, :] = jnp.where(i < nt - 1, glu(xn_ref[...]), 0.0)
    acc = jnp.zeros((tl, CONV_CH), F32) + bias_ref[...]
    first = CV_HALO - CV_PAD
    span = -(-(first + CONV_WIDTH) // SUBLANES) * SUBLANES - SUBLANES
    for sub in range(SUBLANES):
        shifted = buf_ref[sub:sub + tl + span, :]
        for base in range(0, span + 1, SUBLANES):
            j = base + sub - first
            if 0 <= j < CONV_WIDTH:
                acc = acc + dw_ref[j:j + 1, :] * shifted[base:base + tl]
    mu = jnp.mean(acc, axis=-1, keepdims=True)
    cen = acc - mu
    var = jnp.mean(cen * cen, axis=-1, keepdims=True)
    o_ref[...] = _silu(cen * lax.rsqrt(var + NORM_EPS) * lng_ref[...] + lnb_ref[...])


def _conformer_conv(glu_in, dw, bias, lng, lnb, bsz, seqlen, tl):
    x3 = glu_in.reshape(bsz, seqlen, 2 * CONV_CH)
    nt = seqlen // tl
    hb = tl // CV_HALO
    fixed = lambda b, i: (0, 0)
    return pl.pallas_call(
        functools.partial(_conv_kernel, tl=tl),
        grid=(bsz, nt),
        in_specs=[pl.BlockSpec((None, tl, 2 * CONV_CH), lambda b, i: (b, i, 0)),
                  pl.BlockSpec((None, CV_HALO, 2 * CONV_CH), lambda b, i: (b, jnp.maximum(i * hb - 1, 0), 0)),
                  pl.BlockSpec((None, CV_HALO, 2 * CONV_CH),
                               lambda b, i: (b, jnp.minimum((i + 1) * hb, nt * hb - 1), 0)),
                  pl.BlockSpec(dw.shape, fixed), pl.BlockSpec((1, CONV_CH), fixed),
                  pl.BlockSpec((1, CONV_CH), fixed), pl.BlockSpec((1, CONV_CH), fixed)],
        out_specs=pl.BlockSpec((None, tl, CONV_CH), lambda b, i: (b, i, 0)),
        out_shape=jax.ShapeDtypeStruct((bsz, seqlen, CONV_CH), F32),
        scratch_shapes=[pltpu.VMEM((tl + 2 * CV_HALO, CONV_CH), F32)],
        compiler_params=_params(("parallel", "parallel")),
        name="conformer_conv",
    )(x3, x3, x3, dw, bias, lng, lnb).reshape(bsz * seqlen, CONV_CH)


def _out_proj_kernel(h_ref, oa_ref, of_ref, ob_ref, zg_ref, oc_ref, og_ref, hm_ref, w_ref, out_ref):
    ob = of_ref[...] + ob_ref[...]
    ms = jnp.dot((ob * ob).astype(BF16), hm_ref[...], preferred_element_type=F32)
    obn = ob * lax.rsqrt(ms + NORM_EPS) * og_ref[...]
    ob2 = obn * _silu(zg_ref[...])
    mix = jnp.concatenate([oa_ref[...], ob2, oc_ref[...]], axis=1).astype(BF16)
    out_ref[...] = h_ref[...] + jnp.dot(mix, w_ref[...], preferred_element_type=F32)


def _out_proj(h2, oa, o_f, o_b, zg, oc, og, hm, w, tm):
    n = h2.shape[0]
    row = lambda i: (i, 0)
    fixed = lambda i: (0, 0)
    return pl.pallas_call(
        _out_proj_kernel,
        grid=(n // tm,),
        in_specs=[pl.BlockSpec((tm, D_MODEL), row), pl.BlockSpec((tm, ATT_Q), row),
                  pl.BlockSpec((tm, DN_WIDTH), row), pl.BlockSpec((tm, DN_WIDTH), row),
                  pl.BlockSpec((tm, DN_WIDTH), row),
                  pl.BlockSpec((tm, CONV_CH), row), pl.BlockSpec((1, DN_WIDTH), fixed),
                  pl.BlockSpec(hm.shape, fixed), pl.BlockSpec(w.shape, fixed)],
        out_specs=pl.BlockSpec((tm, D_MODEL), row),
        out_shape=jax.ShapeDtypeStruct((n, D_MODEL), F32),
        compiler_params=_params(("parallel",)),
        name="out_proj",
    )(h2, oa, o_f, o_b, zg, oc, og, hm, w)


def _route_kernel(h_ref, gain_ref, wr_ref, xn_ref, aff_ref):
    x = h_ref[...]
    ms = jnp.mean(x * x, axis=-1, keepdims=True)
    xn = x * lax.rsqrt(ms + NORM_EPS) * gain_ref[...]
    xn_ref[...] = xn.astype(BF16)
    logits = jnp.dot(xn, wr_ref[...], preferred_element_type=F32, precision=lax.Precision.HIGHEST)
    lane = lax.broadcasted_iota(jnp.int32, logits.shape, 1)
    logits = jnp.where(lane < N_EXPERTS, logits, -jnp.inf)
    m = jnp.max(logits, axis=-1, keepdims=True)
    e = jnp.exp(logits - m)
    aff = e / jnp.sum(e, axis=-1, keepdims=True)
    aff_ref[...] = jnp.transpose(aff)[:N_EXPERTS, :]


def _route(h2, gain, wr_pad, tm):
    n = h2.shape[0]
    row = lambda i: (i, 0)
    fixed = lambda i: (0, 0)
    return pl.pallas_call(
        _route_kernel,
        grid=(n // tm,),
        in_specs=[pl.BlockSpec((tm, D_MODEL), row), pl.BlockSpec((1, D_MODEL), fixed),
                  pl.BlockSpec((D_MODEL, LANES), fixed)],
        out_specs=[pl.BlockSpec((tm, D_MODEL), row), pl.BlockSpec((N_EXPERTS, tm), lambda i: (0, i))],
        out_shape=[jax.ShapeDtypeStruct((n, D_MODEL), BF16), jax.ShapeDtypeStruct((N_EXPERTS, n), F32)],
        compiler_params=_params(("parallel",)),
        name="moe_route",
    )(h2, gain, wr_pad)


MOE_TILE = 256
MOE_ALIGN = 2 * SUBLANES
MOE_WIN = 80
MOE_PAD = 512
MOE_UNSELECTED = -64.0


def _select_kernel(aff_ref, tri_ref, val_ref, cnt_ref, *, cap, tile):
    ne, n = aff_ref.shape
    nt = n // tile
    capf = float(cap)

    def bits_of(x):
        return lax.bitcast_convert_type(x, jnp.int32)

    def search(i, thr):
        cand = thr | jnp.left_shift(jnp.int32(1), 30 - i)
        cnt = jnp.sum((bits_of(aff_ref[...]) >= cand).astype(F32), axis=1, keepdims=True)
        return jnp.where(cnt >= capf, cand, thr)

    thr = lax.fori_loop(0, 31, search, jnp.zeros((ne, 1), jnp.int32))
    n_gt = jnp.sum((bits_of(aff_ref[...]) > thr).astype(F32), axis=1, keepdims=True)
    need = capf - n_gt
    lane = lax.broadcasted_iota(jnp.int32, (ne, LANES), 1)

    def tile_body(j, carry):
        eq_before, cnt_acc = carry
        off = pl.multiple_of(j * tile, tile)
        b = bits_of(aff_ref[:, pl.ds(off, tile)])
        gt = b > thr
        eqf = (b == thr).astype(F32)
        eq_rank = eq_before + jnp.dot(eqf.astype(BF16), tri_ref[...], preferred_element_type=F32)
        self_ = jnp.where(gt, 1.0, jnp.where(eq_rank <= need, eqf, 0.0))
        rank = jnp.dot(self_.astype(BF16), tri_ref[...], preferred_element_type=F32)
        val_ref[:, pl.ds(off, tile)] = jnp.where(self_ > 0.0, rank, MOE_UNSELECTED)
        cnt = jnp.sum(self_, axis=1, keepdims=True)
        return (eq_before + jnp.sum(eqf, axis=1, keepdims=True), cnt_acc + jnp.where(lane == j, cnt, 0.0))

    init = (jnp.zeros((ne, 1), F32), jnp.zeros((ne, LANES), F32))
    _, cnt_acc = lax.fori_loop(0, nt, tile_body, init)
    cnt_ref[...] = cnt_acc


def _select(aff_t, cap, tile):
    ne, n = aff_t.shape
    assert n // tile <= LANES
    tri = jnp.asarray(np.triu(np.ones((tile, tile), np.float32)), dtype=BF16)
    return pl.pallas_call(
        functools.partial(_select_kernel, cap=cap, tile=tile),
        out_shape=[jax.ShapeDtypeStruct((ne, n), F32), jax.ShapeDtypeStruct((ne, LANES), F32)],
        compiler_params=pltpu.CompilerParams(vmem_limit_bytes=VMEM_LIMIT),
        name="moe_select",
    )(aff_t, tri)


def _moe_plan(cnt, nt):
    c = cnt[:, :nt].astype(jnp.int32).T
    starts = jnp.concatenate([jnp.zeros((1, N_EXPERTS), jnp.int32), jnp.cumsum(c, axis=0)], axis=0)
    head = starts[:-1] & (MOE_ALIGN - 1)
    kmax = jnp.maximum(jnp.max((head + c + MOE_WIN - 1) // MOE_WIN, axis=1), 1).astype(jnp.int32)
    w = jnp.arange(MOE_WIN, dtype=jnp.int32)
    tgt = (w[None, None, :] + 1 - head[:, :, None]).astype(F32).reshape(nt, 1, N_EXPERTS * MOE_WIN)
    return starts.reshape(-1), kmax, tgt


def _expand_matrix():
    e = np.arange(N_EXPERTS * MOE_WIN) // MOE_WIN
    return jnp.asarray((np.arange(N_EXPERTS)[:, None] == e[None, :]).astype(np.float32), dtype=BF16)


def _slot_onehot(val_ref, eexp_ref):
    return lax.dot_general(val_ref[...].astype(BF16), eexp_ref[...], (((0,), (0,)), ((), ())),
                           preferred_element_type=F32)


def _dispatch_kernel(start_ref, kmax_ref, xn_ref, val_ref, tgt_ref, eexp_ref, xe_ref, stage, carry, sem):
    j = pl.program_id(0)
    nt = pl.num_programs(0)
    slot = lax.rem(j, 2)
    ne = N_EXPERTS

    cap = xe_ref.shape[1] - MOE_PAD

    @pl.when(j == 0)
    def _():
        carry[...] = jnp.zeros_like(carry)
        stage[0, 0:MOE_PAD, :] = jnp.zeros((MOE_PAD, D_MODEL), BF16)
        fills = [pltpu.make_async_copy(stage.at[0, pl.ds(0, MOE_PAD)], xe_ref.at[e, pl.ds(cap, MOE_PAD)], sem.at[0])
                 for e in range(ne)]
        for f in fills:
            f.start()
        for f in fills:
            f.wait()

    def window_copy(sl, e, row0):
        return pltpu.make_async_copy(stage.at[sl, pl.ds(e * MOE_WIN, MOE_WIN)],
                                     xe_ref.at[e, pl.ds(row0, MOE_WIN)], sem.at[sl])

    def wait_windows(sl):
        for e in range(ne):
            window_copy(sl, e, 0).wait()

    rep = _slot_onehot(val_ref, eexp_ref)
    xn = xn_ref[...]
    row = lax.broadcasted_iota(jnp.int32, (MOE_ALIGN, D_MODEL), 0)

    def block(k, _):
        @pl.when(k > 0)
        def _():
            wait_windows(slot)

        lo = k * MOE_WIN
        pt = (rep == tgt_ref[...] + lo.astype(F32)).astype(BF16)
        comp = lax.dot_general(pt, xn, (((0,), (0,)), ((), ())), preferred_element_type=F32)
        stage[slot] = comp.astype(BF16)
        for e in range(ne):
            s = start_ref[j * ne + e]
            head = s & (MOE_ALIGN - 1)
            r0 = e * MOE_WIN

            @pl.when(k == 0)
            def _():
                fresh = stage[slot, r0:r0 + MOE_ALIGN, :]
                kept = carry[e * MOE_ALIGN:(e + 1) * MOE_ALIGN, :]
                stage[slot, r0:r0 + MOE_ALIGN, :] = jnp.where(row < head, kept, fresh)

            nxt = (head + start_ref[(j + 1) * ne + e] - s) & (-MOE_ALIGN)

            @pl.when((nxt >= lo) & (nxt < lo + MOE_WIN))
            def _():
                off = pl.multiple_of(nxt - lo, MOE_ALIGN)
                carry[e * MOE_ALIGN:(e + 1) * MOE_ALIGN, :] = stage[slot, pl.ds(r0 + off, MOE_ALIGN), :]

        @pl.when((k == 0) & (j > 0))
        def _():
            wait_windows(1 - slot)

        for e in range(ne):
            base = pl.multiple_of((start_ref[j * ne + e] & (-MOE_ALIGN)) + lo, MOE_ALIGN)
            window_copy(slot, e, base).start()
        return 0

    lax.fori_loop(0, kmax_ref[j], block, 0)

    @pl.when(j == nt - 1)
    def _():
        wait_windows(slot)


def _dispatch(xn, val, starts, kmax, tgt, eexp, cap, tile):
    n = xn.shape[0]
    nt = n // tile
    rows = N_EXPERTS * MOE_WIN
    return pl.pallas_call(
        _dispatch_kernel,
        grid_spec=pltpu.PrefetchScalarGridSpec(
            num_scalar_prefetch=2, grid=(nt,),
            in_specs=[pl.BlockSpec((tile, D_MODEL), lambda j, s, k: (j, 0)),
                      pl.BlockSpec((N_EXPERTS, tile), lambda j, s, k: (0, j)),
                      pl.BlockSpec((None, 1, rows), lambda j, s, k: (j, 0, 0)),
                      pl.BlockSpec((N_EXPERTS, rows), lambda j, s, k: (0, 0))],
            out_specs=pl.BlockSpec(memory_space=pl.ANY),
            scratch_shapes=[pltpu.VMEM((2, rows, D_MODEL), BF16),
                            pltpu.VMEM((N_EXPERTS * MOE_ALIGN, D_MODEL), BF16),
                            pltpu.SemaphoreType.DMA((2,))]),
        out_shape=jax.ShapeDtypeStruct((N_EXPERTS, cap + MOE_PAD, D_MODEL), BF16),
        compiler_params=_params(("arbitrary",)),
        name="moe_dispatch",
    )(starts, kmax, xn, val, tgt, eexp)


def _expert_kernel(x_ref, wr_ref, wg_ref, wu_ref, wd_ref, y_ref, *, ntile):
    e = pl.program_id(0)
    i = pl.program_id(1)

    @pl.when(i < ntile)
    def _():
        x = x_ref[...]
        logits = (jnp.dot(x, wr_ref[0], preferred_element_type=F32)
                  + jnp.dot(x, wr_ref[1], preferred_element_type=F32))
        lane = lax.broadcasted_iota(jnp.int32, logits.shape, 1)
        logits = jnp.where(lane < N_EXPERTS, logits, -jnp.inf)
        ex = jnp.exp(logits - jnp.max(logits, axis=-1, keepdims=True))
        gate = (jnp.sum(jnp.where(lane == e, ex, 0.0), axis=-1, keepdims=True)
                / jnp.sum(ex, axis=-1, keepdims=True))
        hg = jnp.dot(x, wg_ref[...], preferred_element_type=F32)
        hu = jnp.dot(x, wu_ref[...], preferred_element_type=F32)
        hid = (_silu(hg) * hu).astype(BF16)
        y_ref[...] = (jnp.dot(hid, wd_ref[...], preferred_element_type=F32) * gate).astype(BF16)

    @pl.when(i >= ntile)
    def _():
        y_ref[...] = jnp.zeros_like(y_ref)


def _expert_ffn(xe, wr2, wg, wu, wd, cap, tc):
    ne, rows, _ = xe.shape
    wspec = lambda shape: pl.BlockSpec((None,) + shape, lambda e, i: (e, 0, 0))
    return pl.pallas_call(
        functools.partial(_expert_kernel, ntile=cap // tc),
        grid=(ne, rows // tc),
        in_specs=[pl.BlockSpec((None, tc, D_MODEL), lambda e, i: (e, i, 0)),
                  pl.BlockSpec(wr2.shape, lambda e, i: (0, 0, 0)),
                  wspec((D_MODEL, EXPERT_FF)), wspec((D_MODEL, EXPERT_FF)), wspec((EXPERT_FF, D_MODEL))],
        out_specs=pl.BlockSpec((None, tc, D_MODEL), lambda e, i: (e, i, 0)),
        out_shape=jax.ShapeDtypeStruct((ne, rows, D_MODEL), BF16),
        compiler_params=_params(("parallel", "parallel")),
        name="expert_ffn",
    )(xe, wr2, wg, wu, wd)


def _combine_kernel(start_ref, kmax_ref, h_ref, p_ref, val_ref, tgt_ref, eexp_ref, gain_ref, wg_ref, wp_ref,
                    ye_ref, out_ref, stage, sem):
    j = pl.program_id(0)
    nt = pl.num_programs(0)
    slot = lax.rem(j, 2)
    ne = N_EXPERTS

    def window_copy(sl, e, row0):
        return pltpu.make_async_copy(ye_ref.at[e, pl.ds(row0, MOE_WIN)],
                                     stage.at[sl, pl.ds(e * MOE_WIN, MOE_WIN)], sem.at[sl])

    def fetch(sl, tile_idx, lo):
        for e in range(ne):
            base = pl.multiple_of((start_ref[tile_idx * ne + e] & (-MOE_ALIGN)) + lo, MOE_ALIGN)
            window_copy(sl, e, base).start()

    def wait_windows(sl):
        for e in range(ne):
            window_copy(sl, e, 0).wait()

    @pl.when(j == 0)
    def _():
        fetch(slot, j, 0)

    @pl.when(j + 1 < nt)
    def _():
        fetch(1 - slot, j + 1, 0)

    rep = _slot_onehot(val_ref, eexp_ref)
    wait_windows(slot)
    pt = (rep == tgt_ref[...]).astype(BF16)
    acc = h_ref[...] + jnp.dot(pt, stage[slot], preferred_element_type=F32)

    def extra(k, acc):
        lo = k * MOE_WIN
        fetch(slot, j, lo)
        wait_windows(slot)
        pk = (rep == tgt_ref[...] + lo.astype(F32)).astype(BF16)
        return acc + jnp.dot(pk, stage[slot], preferred_element_type=F32)

    x = lax.fori_loop(1, kmax_ref[j], extra, acc)
    ms = jnp.mean(x * x, axis=-1, keepdims=True)
    xn = (x * lax.rsqrt(ms + NORM_EPS) * gain_ref[...]).astype(BF16)
    gate = _sigmoid(jnp.dot(xn, wg_ref[...], preferred_element_type=F32))
    proj = jnp.dot(p_ref[...].astype(BF16), wp_ref[...], preferred_element_type=F32)
    out_ref[...] = x + gate * proj


def _combine_ple(h2, p2, ye, val, starts, kmax, tgt, eexp, gain, wg, wp, tile):
    n = h2.shape[0]
    nt = n // tile
    rows = N_EXPERTS * MOE_WIN
    fixed = lambda j, s, k: (0, 0)
    return pl.pallas_call(
        _combine_kernel,
        grid_spec=pltpu.PrefetchScalarGridSpec(
            num_scalar_prefetch=2, grid=(nt,),
            in_specs=[pl.BlockSpec((tile, D_MODEL), lambda j, s, k: (j, 0)),
                      pl.BlockSpec((tile, PLE_DIM), lambda j, s, k: (j, 0)),
                      pl.BlockSpec((N_EXPERTS, tile), lambda j, s, k: (0, j)),
                      pl.BlockSpec((None, 1, rows), lambda j, s, k: (j, 0, 0)),
                      pl.BlockSpec((N_EXPERTS, rows), fixed),
                      pl.BlockSpec((1, D_MODEL), fixed), pl.BlockSpec(wg.shape, fixed),
                      pl.BlockSpec(wp.shape, fixed),
                      pl.BlockSpec(memory_space=pl.ANY)],
            out_specs=pl.BlockSpec((tile, D_MODEL), lambda j, s, k: (j, 0)),
            scratch_shapes=[pltpu.VMEM((2, rows, D_MODEL), BF16), pltpu.SemaphoreType.DMA((2,))]),
        out_shape=jax.ShapeDtypeStruct((n, D_MODEL), F32),
        compiler_params=_params(("arbitrary",)),
        name="moe_combine_ple",
    )(starts, kmax, h2, p2, val, tgt, eexp, gain, wg, wp, ye)


def _in_perm():
    o_dnz = ZA + ZB
    o_beta = o_dnz + DN_WIDTH
    o_alpha = o_beta + 2 * DN_HEADS
    o_glu = o_alpha + 2 * DN_HEADS
    cols = list(range(0, o_beta)) + list(range(o_glu, o_glu + 2 * CONV_CH))
    for d in range(2):
        cols += [o_beta + d * DN_HEADS + h for h in range(DN_HEADS)]
        cols += [o_alpha + d * DN_HEADS + h for h in range(DN_HEADS)]
    return np.asarray(cols, dtype=np.int32)


def _prep_layer(lw):
    (norm_mix, w_in, q_gain, k_gain, sink, dn_conv, dn_a_log, dn_dt_bias, dn_out_gain,
     cv_dw, cv_dw_bias, cv_ln_gain, cv_ln_bias, w_out, norm_ffn, w_router, w_gate, w_up, w_down,
     norm_ple, w_ple_gate, w_ple_proj) = lw
    perm = _in_perm()
    w_perm = jnp.pad(w_in[:, perm], ((0, 0), (0, ZW - perm.shape[0]))).astype(BF16)
    hgain = jnp.concatenate([jnp.tile(q_gain, ATT_HEADS) * (ATT_HEAD_DIM ** -0.5),
                             jnp.tile(k_gain, ATT_KV_HEADS)]).reshape(1, -1)
    zeros4 = jnp.zeros((DN_HEADS,), F32)
    aneg = -jnp.exp(dn_a_log.astype(F32))
    aneg_row = jnp.concatenate([zeros4, aneg[0], zeros4, aneg[1]])
    dtb_row = jnp.concatenate([zeros4, dn_dt_bias[0], zeros4, dn_dt_bias[1]])
    pad = lambda r: jnp.pad(r, (0, LANES - r.shape[0])).reshape(1, LANES)
    wr = jnp.pad(w_router.astype(F32), ((0, 0), (0, LANES - N_EXPERTS)))
    wr_hi = wr.astype(BF16)
    wr2 = jnp.stack([wr_hi, (wr - wr_hi.astype(F32)).astype(BF16)])
    return dict(
        w_router2=wr2,
        norm_mix=norm_mix.reshape(1, -1), w_in=w_perm, hgain=hgain, sink=sink.astype(F32),
        dn_conv=dn_conv, aneg=pad(aneg_row), dtb=pad(dtb_row),
        dn_out_gain=jnp.tile(dn_out_gain, DN_HEADS).reshape(1, -1),
        cv_dw=cv_dw, cv_dw_bias=cv_dw_bias.reshape(1, -1), cv_ln_gain=cv_ln_gain.reshape(1, -1),
        cv_ln_bias=cv_ln_bias.reshape(1, -1), w_out=w_out.astype(BF16),
        norm_ffn=norm_ffn.reshape(1, -1), w_router=jnp.pad(w_router, ((0, 0), (0, LANES - N_EXPERTS))),
        w_gate=w_gate.astype(BF16), w_up=w_up.astype(BF16), w_down=w_down.astype(BF16),
        norm_ple=norm_ple.reshape(1, -1), w_ple_gate=w_ple_gate.astype(BF16), w_ple_proj=w_ple_proj.astype(BF16))


def _tiles(bsz, seqlen):
    n = bsz * seqlen
    return dict(tm=min(512, n), tl=min(256, seqlen), ch=min(256, seqlen), tcv=min(512, seqlen))


def _moe_ple(h2, p2, pw, t):
    n = h2.shape[0]
    cap = CAPACITY_FACTOR * n // N_EXPERTS
    tile = min(MOE_TILE, n)
    xn, aff_t = _route(h2, pw["norm_ffn"], pw["w_router"], t["tm"])
    val, cnt = _select(aff_t, cap, tile)
    starts, kmax, tgt = _moe_plan(cnt, n // tile)
    eexp = _expand_matrix()
    xe = _dispatch(xn, val, starts, kmax, tgt, eexp, cap, tile)
    ye = _expert_ffn(xe, pw["w_router2"], pw["w_gate"], pw["w_up"], pw["w_down"], cap, min(512, cap))
    return _combine_ple(h2, p2, ye, val, starts, kmax, tgt, eexp, pw["norm_ple"], pw["w_ple_gate"],
                        pw["w_ple_proj"], tile)


def _layer(h2, p2, pw, bsz, seqlen):
    t = _tiles(bsz, seqlen)
    hm_att = _head_mean_matrix(ATT_Q + ATT_KV, ATT_HEAD_DIM)
    hs_dn = _head_sum_matrix(2 * DN_WIDTH, DN_HEAD_DIM)
    hm_dn = _head_mean_matrix(DN_WIDTH, DN_HEAD_DIM)
    za, zb, zg, glu_in, gates = _in_proj(h2, pw["norm_mix"], pw["w_in"], hm_att, pw["hgain"], t["tm"])
    o_a = _attention(za, pw["sink"], bsz, seqlen)
    y, gb = _dn_prep(zb, gates, pw["dn_conv"], hs_dn, pw["aneg"], pw["dtb"], bsz, seqlen, t["tl"])
    o_f, o_b = _dn_chunk(y, gb, bsz, seqlen, t["ch"])
    o_c = _conformer_conv(glu_in, pw["cv_dw"], pw["cv_dw_bias"], pw["cv_ln_gain"], pw["cv_ln_bias"],
                          bsz, seqlen, t["tcv"])
    h2 = _out_proj(h2, o_a, o_f, o_b, zg, o_c, pw["dn_out_gain"], hm_dn, pw["w_out"], t["tm"])
    return _moe_ple(h2, p2, pw, t)


def _trunk(x, p, layer_weights):
    bsz, seqlen, _ = x.shape
    h2 = x.reshape(bsz * seqlen, D_MODEL)
    for i, pw in enumerate(layer_weights):
        h2 = _layer(h2, p[i].reshape(bsz * seqlen, PLE_DIM), pw, bsz, seqlen)
    return h2.reshape(bsz, seqlen, D_MODEL)


def kernel(x_prompt, x_sample, p_prompt, p_sample, norm_mix, w_in, q_gain, k_gain, sink, dn_conv, dn_a_log,
           dn_dt_bias, dn_out_gain, cv_dw, cv_dw_bias, cv_ln_gain, cv_ln_bias, w_out, norm_ffn, w_router,
           w_gate, w_up, w_down, norm_ple, w_ple_gate, w_ple_proj):
    weights = (norm_mix, w_in, q_gain, k_gain, sink, dn_conv, dn_a_log, dn_dt_bias, dn_out_gain,
               cv_dw, cv_dw_bias, cv_ln_gain, cv_ln_bias, w_out, norm_ffn, w_router, w_gate, w_up, w_down,
               norm_ple, w_ple_gate, w_ple_proj)
    depth = w_in.shape[0]
    layer_weights = [_prep_layer([w[i] for w in weights]) for i in range(depth)]
    return (_trunk(x_prompt, p_prompt, layer_weights), _trunk(x_sample, p_sample, layer_weights))
```

```python
import functools

import numpy as np
import jax
import jax.numpy as jnp
from jax import lax
from jax.experimental import pallas as pl
from jax.experimental.pallas import tpu as pltpu

F32 = jnp.float32
BF16 = jnp.bfloat16

D_MODEL = 1024
ATT_HEADS = 8
ATT_KV_HEADS = 2
ATT_HEAD_DIM = 64
ATT_GROUP = ATT_HEADS // ATT_KV_HEADS
WINDOW = 128
ATT_BLOCK = 128
DN_HEADS = 4
DN_HEAD_DIM = 64
DN_WIDTH = DN_HEADS * DN_HEAD_DIM
DN_CHUNK = 64
CONV_CH = 256
CONV_WIDTH = 31
ATT_Q = ATT_HEADS * ATT_HEAD_DIM
ATT_KV = ATT_KV_HEADS * ATT_HEAD_DIM
N_EXPERTS = 16
CAPACITY_FACTOR = 2
EXPERT_FF = 1024
PLE_DIM = 256
NORM_EPS = 1e-6

LANES = 128
SUBLANES = 8
VMEM_LIMIT = 48 * 1024 * 1024

ZA = ATT_Q + 2 * ATT_KV
ZB = 3 * DN_WIDTH
ZW = ZA + ZB + DN_WIDTH + 2 * CONV_CH + LANES


def _params(sem):
    return pltpu.CompilerParams(dimension_semantics=sem, vmem_limit_bytes=VMEM_LIMIT)


def _head_mean_matrix(width, head):
    idx = np.arange(width) // head
    return jnp.asarray((idx[:, None] == idx[None, :]).astype(np.float32) / head, dtype=BF16)


def _head_sum_matrix(width, head):
    idx = np.arange(width) // head
    return jnp.asarray((idx[:, None] == idx[None, :]).astype(np.float32), dtype=BF16)


def _sigmoid(x):
    return 1.0 / (1.0 + jnp.exp(-x))


def _silu(x):
    return x * _sigmoid(x)


def _in_proj_kernel(x_ref, gain_ref, w_ref, hm_ref, hgain_ref, za_ref, zb_ref, zg_ref, glu_ref, gates_ref):
    x = x_ref[...]
    ms = jnp.mean(x * x, axis=-1, keepdims=True)
    a = (x * lax.rsqrt(ms + NORM_EPS) * gain_ref[...]).astype(BF16)
    z = jnp.dot(a, w_ref[...], preferred_element_type=F32)
    nqk = ATT_Q + ATT_KV
    qk = z[:, :nqk]
    hms = jnp.dot((qk * qk).astype(BF16), hm_ref[...], preferred_element_type=F32)
    za_ref[:, :nqk] = (qk * lax.rsqrt(hms + NORM_EPS) * hgain_ref[...]).astype(BF16)
    za_ref[:, nqk:] = z[:, nqk:ZA].astype(BF16)
    zb_ref[...] = z[:, ZA:ZA + ZB]
    zg_ref[...] = z[:, ZA + ZB:ZA + ZB + DN_WIDTH]
    glu_ref[...] = z[:, ZA + ZB + DN_WIDTH:ZA + ZB + DN_WIDTH + 2 * CONV_CH]
    gates_ref[...] = z[:, ZW - LANES:]


def _in_proj(h2, gain, w_perm, hm, hgain, tm):
    n = h2.shape[0]
    row = lambda i: (i, 0)
    fixed = lambda i: (0, 0)
    return pl.pallas_call(
        _in_proj_kernel,
        grid=(n // tm,),
        in_specs=[pl.BlockSpec((tm, D_MODEL), row), pl.BlockSpec((1, D_MODEL), fixed),
                  pl.BlockSpec((D_MODEL, ZW), fixed), pl.BlockSpec(hm.shape, fixed),
                  pl.BlockSpec(hgain.shape, fixed)],
        out_specs=[pl.BlockSpec((tm, ZA), row), pl.BlockSpec((tm, ZB), row), pl.BlockSpec((tm, DN_WIDTH), row),
                   pl.BlockSpec((tm, 2 * CONV_CH), row), pl.BlockSpec((tm, LANES), row)],
        out_shape=[jax.ShapeDtypeStruct((n, ZA), BF16), jax.ShapeDtypeStruct((n, ZB), F32),
                   jax.ShapeDtypeStruct((n, DN_WIDTH), F32), jax.ShapeDtypeStruct((n, 2 * CONV_CH), F32),
                   jax.ShapeDtypeStruct((n, LANES), F32)],
        compiler_params=_params(("parallel",)),
        name="in_proj",
    )(h2, gain, w_perm, hm, hgain)


ATT_MASKED = -1e30


def _attn_bias_table():
    i = np.arange(ATT_BLOCK)[:, None]
    c = np.arange(3 * ATT_BLOCK)[None, :]
    rel = c - ATT_BLOCK - i
    slopes = 2.0 ** (-8.0 * np.arange(1, ATT_HEADS + 1) / ATT_HEADS)
    table = np.empty((3, ATT_KV_HEADS, ATT_GROUP * ATT_BLOCK, 3 * ATT_BLOCK), np.float32)
    for variant in range(3):
        ok = np.abs(rel) <= WINDOW
        if variant == 0:
            ok = ok & (c >= ATT_BLOCK)
        if variant == 2:
            ok = ok & (c < 2 * ATT_BLOCK)
        for hd in range(ATT_HEADS):
            g, j = divmod(hd, ATT_GROUP)
            table[variant, g, j * ATT_BLOCK:(j + 1) * ATT_BLOCK] = np.where(ok, -slopes[hd] * np.abs(rel), ATT_MASKED)
    return jnp.asarray(table)


def _attn_kernel(sink_ref, q_ref, kvp_ref, kvo_ref, kvn_ref, bias_ref, o_ref):
    kv = jnp.concatenate([kvp_ref[...], kvo_ref[...], kvn_ref[...]], axis=0)
    hd_ = ATT_HEAD_DIM
    groups = range(ATT_KV_HEADS)
    heads = range(ATT_HEADS)
    ks = [kv[:, g * hd_:(g + 1) * hd_].astype(BF16) for g in groups]
    vs = [kv[:, ATT_KV + g * hd_:ATT_KV + (g + 1) * hd_].astype(BF16) for g in groups]
    qs = [jnp.concatenate([q_ref[:, (g * ATT_GROUP + j) * hd_:(g * ATT_GROUP + j + 1) * hd_]
                           for j in range(ATT_GROUP)], axis=0).astype(BF16) for g in groups]
    sg = [lax.dot_general(qs[g], ks[g], (((1,), (1,)), ((), ())), preferred_element_type=F32) + bias_ref[g]
          for g in groups]
    rows = lambda t, hd: t[(hd % ATT_GROUP) * ATT_BLOCK:(hd % ATT_GROUP + 1) * ATT_BLOCK]
    s = [rows(sg[hd // ATT_GROUP], hd) for hd in heads]
    m = [jnp.maximum(jnp.max(s[hd], axis=-1, keepdims=True), sink_ref[hd]) for hd in heads]
    e = [jnp.exp(s[hd] - m[hd]) for hd in heads]
    den = [jnp.sum(e[hd], axis=-1, keepdims=True) + jnp.exp(sink_ref[hd] - m[hd]) for hd in heads]
    eg = [jnp.concatenate([e[g * ATT_GROUP + j].astype(BF16) for j in range(ATT_GROUP)], axis=0) for g in groups]
    og = [jnp.dot(eg[g], vs[g], preferred_element_type=F32) for g in groups]
    for hd in heads:
        o_ref[:, hd * hd_:(hd + 1) * hd_] = rows(og[hd // ATT_GROUP], hd) / den[hd]


def _attention(za, sink, bsz, seqlen):
    nb = seqlen // ATT_BLOCK
    assert nb >= 2
    za3 = za.reshape(bsz, seqlen, ZA)
    kvw = 2 * ATT_KV
    kvc = ATT_Q // kvw
    bias = _attn_bias_table()
    return pl.pallas_call(
        _attn_kernel,
        grid=(bsz, nb),
        in_specs=[pl.BlockSpec(memory_space=pltpu.SMEM),
                  pl.BlockSpec((None, ATT_BLOCK, ATT_Q), lambda b, n: (b, n, 0)),
                  pl.BlockSpec((None, ATT_BLOCK, kvw), lambda b, n: (b, jnp.maximum(n - 1, 0), kvc)),
                  pl.BlockSpec((None, ATT_BLOCK, kvw), lambda b, n: (b, n, kvc)),
                  pl.BlockSpec((None, ATT_BLOCK, kvw), lambda b, n: (b, jnp.minimum(n + 1, nb - 1), kvc)),
                  pl.BlockSpec((None,) + bias.shape[1:],
                               lambda b, n: (jnp.where(n == 0, 0, jnp.where(n == nb - 1, 2, 1)), 0, 0, 0))],
        out_specs=pl.BlockSpec((None, ATT_BLOCK, ATT_Q), lambda b, n: (b, n, 0)),
        out_shape=jax.ShapeDtypeStruct((bsz, seqlen, ATT_Q), F32),
        compiler_params=_params(("parallel", "parallel")),
        name="window_attention",
    )(sink, za3, za3, za3, za3, bias).reshape(bsz * seqlen, ATT_Q)


DN_HALO = SUBLANES


def _dn_prep_kernel(x_ref, xp_ref, xn_ref, cw_ref, hs_ref, g_ref, aneg_ref, dtb_ref, mf_ref, mb_ref,
                    y_ref, gb_ref, buf_ref, *, tl):
    i = pl.program_id(1)
    nt = pl.num_programs(1)
    buf_ref[0:DN_HALO, :] = jnp.where(i > 0, xp_ref[...], 0.0)
    buf_ref[DN_HALO:DN_HALO + tl, :] = x_ref[...]
    buf_ref[DN_HALO + tl:, :] = jnp.where(i < nt - 1, xn_ref[...], 0.0)
    y = (cw_ref[0:1, :] * buf_ref[DN_HALO - 1:DN_HALO - 1 + tl, :]
         + cw_ref[1:2, :] * buf_ref[DN_HALO:DN_HALO + tl, :]
         + cw_ref[2:3, :] * buf_ref[DN_HALO + 1:DN_HALO + 1 + tl, :])
    y = _silu(y)
    qk = y[:, :2 * DN_WIDTH]
    ss = jnp.dot((qk * qk).astype(BF16), hs_ref[...], preferred_element_type=F32)
    lane = lax.broadcasted_iota(jnp.int32, (tl, 2 * DN_WIDTH), 1)
    scale = jnp.where(lane < DN_WIDTH, DN_HEAD_DIM ** -0.5, 1.0)
    y_ref[:, :2 * DN_WIDTH] = qk * lax.rsqrt(ss + NORM_EPS) * scale
    y_ref[:, 2 * DN_WIDTH:] = y[:, 2 * DN_WIDTH:]
    raw = g_ref[...]
    col = lax.broadcasted_iota(jnp.int32, (tl, LANES), 1)
    is_beta = (col & DN_HEADS) == 0
    t = raw + dtb_ref[...]
    softplus = jnp.maximum(t, 0.0) + jnp.log(1.0 + jnp.exp(-jnp.abs(t)))
    vals = jnp.where(is_beta, _sigmoid(raw), aneg_ref[...] * softplus)
    cf = jnp.dot(mf_ref[...], vals, preferred_element_type=F32, precision=lax.Precision.HIGHEST)
    cb = jnp.dot(mb_ref[...], vals, preferred_element_type=F32, precision=lax.Precision.HIGHEST)
    gb_ref[0] = jnp.where(is_beta, vals, cf)
    gb_ref[1] = pltpu.roll(jnp.where(is_beta, vals, cb), LANES - 2 * DN_HEADS, axis=1)


def _dn_prep(zb, gates, conv_w, hs, aneg, dtb, bsz, seqlen, tl):
    zb3 = zb.reshape(bsz, seqlen, ZB)
    g3 = gates.reshape(bsz, seqlen, LANES)
    nt = seqlen // tl
    hb = tl // DN_HALO
    ch = np.arange(tl) // DN_CHUNK
    same = ch[:, None] == ch[None, :]
    pos = np.arange(tl)
    mf = jnp.asarray((same & (pos[None, :] <= pos[:, None])).astype(np.float32))
    mb = jnp.asarray((same & (pos[None, :] >= pos[:, None])).astype(np.float32))
    fixed = lambda b, i: (0, 0)
    y, gb = pl.pallas_call(
        functools.partial(_dn_prep_kernel, tl=tl),
        grid=(bsz, nt),
        in_specs=[pl.BlockSpec((None, tl, ZB), lambda b, i: (b, i, 0)),
                  pl.BlockSpec((None, DN_HALO, ZB), lambda b, i: (b, jnp.maximum(i * hb - 1, 0), 0)),
                  pl.BlockSpec((None, DN_HALO, ZB), lambda b, i: (b, jnp.minimum((i + 1) * hb, nt * hb - 1), 0)),
                  pl.BlockSpec(conv_w.shape, fixed), pl.BlockSpec(hs.shape, fixed),
                  pl.BlockSpec((None, tl, LANES), lambda b, i: (b, i, 0)),
                  pl.BlockSpec((1, LANES), fixed), pl.BlockSpec((1, LANES), fixed),
                  pl.BlockSpec((tl, tl), fixed), pl.BlockSpec((tl, tl), fixed)],
        out_specs=[pl.BlockSpec((None, tl, ZB), lambda b, i: (b, i, 0)),
                   pl.BlockSpec((2, None, tl, LANES), lambda b, i: (0, b, i, 0))],
        out_shape=[jax.ShapeDtypeStruct((bsz, seqlen, ZB), F32),
                   jax.ShapeDtypeStruct((2, bsz, seqlen, LANES), F32)],
        scratch_shapes=[pltpu.VMEM((tl + 2 * DN_HALO, ZB), F32)],
        compiler_params=_params(("parallel", "parallel")),
        name="deltanet_prep",
    )(zb3, zb3, zb3, conv_w, hs, g3, aneg, dtb, mf, mb)
    return y, gb


def _lane_expand(cols, first):
    c = cols.shape[0]
    lane = lax.broadcasted_iota(jnp.int32, (c, LANES), 1)
    halves = []
    for h in range(0, DN_HEADS, 2):
        a = jnp.broadcast_to(cols[:, first + h:first + h + 1], (c, LANES))
        b = jnp.broadcast_to(cols[:, first + h + 1:first + h + 2], (c, LANES))
        halves.append(jnp.where(lane < DN_HEAD_DIM, a, b))
    return jnp.concatenate(halves, axis=1)


def _dn_pair_kernel(xf_ref, xb_ref, gf_ref, gb_ref, of_ref, ob_ref, sf_ref, sb_ref, *, nsub):
    c = DN_CHUNK
    w = DN_WIDTH

    @pl.when(pl.program_id(1) == 0)
    def _():
        sf_ref[...] = jnp.zeros_like(sf_ref)
        sb_ref[...] = jnp.zeros_like(sb_ref)

    r_cat = lax.broadcasted_iota(jnp.int32, (c, w), 0)
    s_cat = lax.broadcasted_iota(jnp.int32, (c, w), 1) & (DN_HEAD_DIM - 1)
    eye_cat = s_cat == r_cat
    rr = lax.broadcasted_iota(jnp.int32, (w, w), 0)
    cc = lax.broadcasted_iota(jnp.int32, (w, w), 1)
    head = (rr >> 6) == (cc >> 6)
    m16 = (rr >> 4) == (cc >> 4)
    m32 = (rr >> 5) == (cc >> 5)
    off16 = m32 & jnp.logical_not(m16)
    off32 = head & jnp.logical_not(m32)
    eye = (rr == cc).astype(F32)

    def tile4(t):
        return jnp.concatenate([t] * DN_HEADS, axis=0)

    def bd(t):
        return jnp.where(head, tile4(t), 0.0)

    def mm(a, b):
        return jnp.dot(a.astype(BF16), b.astype(BF16), preferred_element_type=F32)

    chunks = [(0, i * c) for i in range(nsub)] + [(1, (nsub - 1 - i) * c) for i in range(nsub)]
    xrefs = (xf_ref, xb_ref)
    grefs = (gf_ref, gb_ref)
    orefs = (of_ref, ob_ref)
    srefs = (sf_ref, sb_ref)
    incl = (s_cat <= r_cat, s_cat >= r_cat)
    strict = (s_cat < r_cat, s_cat > r_cat)
    last_row = (c - 1, 0)

    pre = []
    for d, st in chunks:
        x = xrefs[d][st:st + c, :]
        q, k, v = x[:, :w], x[:, w:2 * w], x[:, 2 * w:]
        gbt = grefs[d][st:st + c, :]
        beta = _lane_expand(gbt, 0)
        gc = _lane_expand(gbt, DN_HEADS)
        grow = jnp.sum(jnp.where(eye_cat, gc, 0.0), axis=0, keepdims=True)
        decay = jnp.exp(jnp.where(incl[d], gc - grow, -jnp.inf))
        glast = gc[last_row[d]:last_row[d] + 1, :]
        egc = jnp.exp(gc)
        kb = k * beta
        pre.append(dict(d=d, st=st, q=q, k=k, kb=kb, vb=v * beta, decay=decay, glast=glast, egc=egc,
                        kdec=(k * jnp.exp(glast - gc)).astype(BF16)))

    kks = [lax.dot_general(jnp.concatenate([p["kb"], p["q"]], axis=0).astype(BF16), bd(p["k"]).astype(BF16),
                           (((1,), (1,)), ((), ())), preferred_element_type=F32) for p in pre]
    a4 = [tile4(jnp.where(strict[p["d"]], kk[:c] * p["decay"], 0.0)) for p, kk in zip(pre, kks)]
    intra = [jnp.where(incl[p["d"]], kk[c:] * p["decay"], 0.0).astype(BF16) for p, kk in zip(pre, kks)]
    xm = [jnp.where(m16, -t, 0.0).astype(BF16) for t in a4]
    x2 = [mm(t, t).astype(BF16) for t in xm]
    dinv = [eye + t.astype(F32) for t in xm]
    dinv = [t + mm(t, s2) for t, s2 in zip(dinv, x2)]
    x4 = [mm(t, t).astype(BF16) for t in x2]
    dinv = [t + mm(t, s4) for t, s4 in zip(dinv, x4)]
    x8 = [mm(t, t) for t in x4]
    dinv = [t + mm(t, s8) for t, s8 in zip(dinv, x8)]
    dinv_b = [t.astype(BF16) for t in dinv]
    n32 = [mm(jnp.where(off16, t, 0.0), db) for t, db in zip(a4, dinv_b)]
    t32 = [t - mm(db, n) for t, db, n in zip(dinv, dinv_b, n32)]
    t32_b = [t.astype(BF16) for t in t32]
    n64 = [mm(jnp.where(off32, t, 0.0), tb) for t, tb in zip(a4, t32_b)]
    t64 = [t - mm(tb, n) for t, tb, n in zip(t32, t32_b, n64)]
    t_cat = [(t[0:c] + t[c:2 * c] + t[2 * c:3 * c] + t[3 * c:4 * c]).astype(BF16) for t in t64]
    uw = [mm(tc, jnp.concatenate([bd(p["vb"]), bd(p["kb"] * p["egc"])], axis=1)) for tc, p in zip(t_cat, pre)]
    uw_b = [t.astype(BF16) for t in uw]
    pn = [lax.dot_general(p["kdec"], t, (((0,), (0,)), ((), ())), preferred_element_type=F32)
          for p, t in zip(pre, uw_b)]
    qo = [jnp.dot(it, jnp.concatenate([bd(t[:, :w]), bd(t[:, w:])], axis=1).astype(BF16), preferred_element_type=F32)
          for it, t in zip(intra, uw)]
    lhs = [jnp.concatenate([jnp.where(head, n[:, w:], 0.0), p["q"] * p["egc"] - o[:, w:]], axis=0).astype(BF16)
           for n, o, p in zip(pn, qo, pre)]
    for step in range(nsub):
        for d in range(2):
            i = d * nsub + step
            p = pre[i]
            state = srefs[d][...]
            r = jnp.dot(lhs[i], state.astype(BF16), preferred_element_type=F32)
            orefs[d][p["st"]:p["st"] + c, :] = r[w:] + qo[i][:, :w]
            srefs[d][...] = state * jnp.exp(p["glast"]) - r[:w] + jnp.where(head, pn[i][:, :w], 0.0)


def _dn_chunk(y, gb, bsz, seqlen, ch):
    nsub = ch // DN_CHUNK
    nblk = seqlen // ch
    fwd = lambda b, j: (b, j, 0)
    bwd = lambda b, j: (b, nblk - 1 - j, 0)
    o_f, o_b = pl.pallas_call(
        functools.partial(_dn_pair_kernel, nsub=nsub),
        grid=(bsz, nblk),
        in_specs=[pl.BlockSpec((None, ch, ZB), fwd), pl.BlockSpec((None, ch, ZB), bwd),
                  pl.BlockSpec((None, None, ch, LANES), lambda b, j: (0, b, j, 0)),
                  pl.BlockSpec((None, None, ch, LANES), lambda b, j: (1, b, nblk - 1 - j, 0))],
        out_specs=[pl.BlockSpec((None, ch, DN_WIDTH), fwd), pl.BlockSpec((None, ch, DN_WIDTH), bwd)],
        out_shape=[jax.ShapeDtypeStruct((bsz, seqlen, DN_WIDTH), F32)] * 2,
        scratch_shapes=[pltpu.VMEM((DN_WIDTH, DN_WIDTH), F32)] * 2,
        compiler_params=_params(("parallel", "arbitrary")),
        name="deltanet_chunks",
    )(y, y, gb, gb)
    return o_f.reshape(bsz * seqlen, DN_WIDTH), o_b.reshape(bsz * seqlen, DN_WIDTH)


CV_HALO = 2 * SUBLANES
CV_PAD = (CONV_WIDTH - 1) // 2


def _conv_kernel(x_ref, xp_ref, xn_ref, dw_ref, bias_ref, lng_ref, lnb_ref, o_ref, buf_ref, shift_ref, *, tl):
    i = pl.program_id(1)
    nt = pl.num_programs(1)

    def glu(t):
        return t[:, :CONV_CH] * _sigmoid(t[:, CONV_CH:])

    buf_ref[0:CV_HALO, :] = jnp.where(i > 0, glu(xp_ref[...]), 0.0)
    buf_ref[CV_HALO:CV_HALO + tl, :] = glu(x_ref[...])
    buf_ref[CV_HALO + tl:, :] = jnp.where(i < nt - 1, glu(xn_ref[...]), 0.0)
    acc = jnp.zeros((tl, CONV_CH), F32) + bias_ref[...]
    first = CV_HALO - CV_PAD
    span = -(-(first + CONV_WIDTH) // SUBLANES) * SUBLANES - SUBLANES
    for sub in range(SUBLANES):
        shift_ref[...] = buf_ref[sub:sub + tl + span, :]
        for base in range(0, span + 1, SUBLANES):
            j = base + sub - first
            if 0 <= j < CONV_WIDTH:
                acc = acc + dw_ref[j:j + 1, :] * shift_ref[base:base + tl, :]
    mu = jnp.mean(acc, axis=-1, keepdims=True)
    cen = acc - mu
    var = jnp.mean(cen * cen, axis=-1, keepdims=True)
    o_ref[...] = _silu(cen * lax.rsqrt(var + NORM_EPS) * lng_ref[...] + lnb_ref[...])


def _conformer_conv(glu_in, dw, bias, lng, lnb, bsz, seqlen, tl):
    x3 = glu_in.reshape(bsz, seqlen, 2 * CONV_CH)
    nt = seqlen // tl
    hb = tl // CV_HALO
    fixed = lambda b, i: (0, 0)
    return pl.pallas_call(
        functools.partial(_conv_kernel, tl=tl),
        grid=(bsz, nt),
        in_specs=[pl.BlockSpec((None, tl, 2 * CONV_CH), lambda b, i: (b, i, 0)),
                  pl.BlockSpec((None, CV_HALO, 2 * CONV_CH), lambda b, i: (b, jnp.maximum(i * hb - 1, 0), 0)),
                  pl.BlockSpec((None, CV_HALO, 2 * CONV_CH),
                               lambda b, i: (b, jnp.minimum((i + 1) * hb, nt * hb - 1), 0)),
                  pl.BlockSpec(dw.shape, fixed), pl.BlockSpec((1, CONV_CH), fixed),
                  pl.BlockSpec((1, CONV_CH), fixed), pl.BlockSpec((1, CONV_CH), fixed)],
        out_specs=pl.BlockSpec((None, tl, CONV_CH), lambda b, i: (b, i, 0)),
        out_shape=jax.ShapeDtypeStruct((bsz, seqlen, CONV_CH), F32),
        scratch_shapes=[pltpu.VMEM((tl + 2 * CV_HALO, CONV_CH), F32),
                        pltpu.VMEM((tl + 2 * CV_HALO - SUBLANES, CONV_CH), F32)],
        compiler_params=_params(("parallel", "parallel")),
        name="conformer_conv",
    )(x3, x3, x3, dw, bias, lng, lnb).reshape(bsz * seqlen, CONV_CH)


def _out_proj_kernel(h_ref, oa_ref, of_ref, ob_ref, zg_ref, oc_ref, og_ref, hm_ref, w_ref, out_ref):
    ob = of_ref[...] + ob_ref[...]
    ms = jnp.dot((ob * ob).astype(BF16), hm_ref[...], preferred_element_type=F32)
    obn = ob * lax.rsqrt(ms + NORM_EPS) * og_ref[...]
    ob2 = obn * _silu(zg_ref[...])
    mix = jnp.concatenate([oa_ref[...], ob2, oc_ref[...]], axis=1).astype(BF16)
    out_ref[...] = h_ref[...] + jnp.dot(mix, w_ref[...], preferred_element_type=F32)


def _out_proj(h2, oa, o_f, o_b, zg, oc, og, hm, w, tm):
    n = h2.shape[0]
    row = lambda i: (i, 0)
    fixed = lambda i: (0, 0)
    return pl.pallas_call(
        _out_proj_kernel,
        grid=(n // tm,),
        in_specs=[pl.BlockSpec((tm, D_MODEL), row), pl.BlockSpec((tm, ATT_Q), row),
                  pl.BlockSpec((tm, DN_WIDTH), row), pl.BlockSpec((tm, DN_WIDTH), row),
                  pl.BlockSpec((tm, DN_WIDTH), row),
                  pl.BlockSpec((tm, CONV_CH), row), pl.BlockSpec((1, DN_WIDTH), fixed),
                  pl.BlockSpec(hm.shape, fixed), pl.BlockSpec(w.shape, fixed)],
        out_specs=pl.BlockSpec((tm, D_MODEL), row),
        out_shape=jax.ShapeDtypeStruct((n, D_MODEL), F32),
        compiler_params=_params(("parallel",)),
        name="out_proj",
    )(h2, oa, o_f, o_b, zg, oc, og, hm, w)


def _route_kernel(h_ref, gain_ref, wr_ref, xn_ref, aff_ref):
    x = h_ref[...]
    ms = jnp.mean(x * x, axis=-1, keepdims=True)
    xn = x * lax.rsqrt(ms + NORM_EPS) * gain_ref[...]
    xn_hi = xn.astype(BF16)
    xn_ref[...] = xn_hi
    xn_lo = (xn - xn_hi.astype(F32)).astype(BF16)
    logits = (jnp.dot(xn_hi, wr_ref[0], preferred_element_type=F32)
              + (jnp.dot(xn_lo, wr_ref[0], preferred_element_type=F32)
                 + jnp.dot(xn_hi, wr_ref[1], preferred_element_type=F32)))
    lane = lax.broadcasted_iota(jnp.int32, logits.shape, 1)
    logits = jnp.where(lane < N_EXPERTS, logits, -jnp.inf)
    m = jnp.max(logits, axis=-1, keepdims=True)
    e = jnp.exp(logits - m)
    aff = e / jnp.sum(e, axis=-1, keepdims=True)
    aff_ref[...] = jnp.transpose(aff)[:N_EXPERTS, :]


def _route(h2, gain, wr_pad, tm):
    n = h2.shape[0]
    row = lambda i: (i, 0)
    fixed = lambda i: (0, 0)
    return pl.pallas_call(
        _route_kernel,
        grid=(n // tm,),
        in_specs=[pl.BlockSpec((tm, D_MODEL), row), pl.BlockSpec((1, D_MODEL), fixed),
                  pl.BlockSpec((2, D_MODEL, LANES), lambda i: (0, 0, 0))],
        out_specs=[pl.BlockSpec((tm, D_MODEL), row), pl.BlockSpec((N_EXPERTS, tm), lambda i: (0, i))],
        out_shape=[jax.ShapeDtypeStruct((n, D_MODEL), BF16), jax.ShapeDtypeStruct((N_EXPERTS, n), F32)],
        compiler_params=_params(("parallel",)),
        name="moe_route",
    )(h2, gain, wr_pad)


MOE_TILE = 512
MOE_ALIGN = 2 * SUBLANES
MOE_WIN = 144
MOE_PAD = 1024
MOE_FFN_TILE = 1024
FF_CHUNK = 256
MOE_UNSELECTED = -64.0


def _select_kernel(aff_ref, tri_ref, val_ref, cnt_ref, *, cap, tile):
    ne, n = aff_ref.shape
    nt = n // tile
    capf = float(cap)

    def bits_of(x):
        return lax.bitcast_convert_type(x, jnp.int32)

    def search(i, thr):
        cand = thr | jnp.left_shift(jnp.int32(1), 30 - i)
        cnt = jnp.sum((bits_of(aff_ref[...]) >= cand).astype(F32), axis=1, keepdims=True)
        return jnp.where(cnt >= capf, cand, thr)

    thr = lax.fori_loop(0, 31, search, jnp.zeros((ne, 1), jnp.int32))
    n_gt = jnp.sum((bits_of(aff_ref[...]) > thr).astype(F32), axis=1, keepdims=True)
    need = capf - n_gt
    lane = lax.broadcasted_iota(jnp.int32, (ne, LANES), 1)

    def tile_body(j, carry):
        eq_before, cnt_acc = carry
        off = pl.multiple_of(j * tile, tile)
        b = bits_of(aff_ref[:, pl.ds(off, tile)])
        gt = b > thr
        eqf = (b == thr).astype(F32)
        eq_rank = eq_before + jnp.dot(eqf.astype(BF16), tri_ref[...], preferred_element_type=F32)
        self_ = jnp.where(gt, 1.0, jnp.where(eq_rank <= need, eqf, 0.0))
        rank = jnp.dot(self_.astype(BF16), tri_ref[...], preferred_element_type=F32)
        val_ref[:, pl.ds(off, tile)] = jnp.where(self_ > 0.0, rank, MOE_UNSELECTED)
        cnt = jnp.sum(self_, axis=1, keepdims=True)
        return (eq_before + jnp.sum(eqf, axis=1, keepdims=True), cnt_acc + jnp.where(lane == j, cnt, 0.0))

    init = (jnp.zeros((ne, 1), F32), jnp.zeros((ne, LANES), F32))
    _, cnt_acc = lax.fori_loop(0, nt, tile_body, init)
    cnt_ref[...] = cnt_acc


def _select(aff_t, cap, tile):
    ne, n = aff_t.shape
    assert n // tile <= LANES
    tri = jnp.asarray(np.triu(np.ones((tile, tile), np.float32)), dtype=BF16)
    return pl.pallas_call(
        functools.partial(_select_kernel, cap=cap, tile=tile),
        out_shape=[jax.ShapeDtypeStruct((ne, n), F32), jax.ShapeDtypeStruct((ne, LANES), F32)],
        compiler_params=pltpu.CompilerParams(vmem_limit_bytes=VMEM_LIMIT),
        name="moe_select",
    )(aff_t, tri)


def _moe_plan(cnt, nt):
    c = cnt[:, :nt].astype(jnp.int32).T
    starts = jnp.concatenate([jnp.zeros((1, N_EXPERTS), jnp.int32), jnp.cumsum(c, axis=0)], axis=0)
    head = starts[:-1] & (MOE_ALIGN - 1)
    kmax = jnp.maximum(jnp.max((head + c + MOE_WIN - 1) // MOE_WIN, axis=1), 1).astype(jnp.int32)
    w = jnp.arange(MOE_WIN, dtype=jnp.int32)
    tgt = (w[None, None, :] + 1 - head[:, :, None]).astype(F32).reshape(nt, 1, N_EXPERTS * MOE_WIN)
    return starts.reshape(-1), kmax, tgt


def _expand_matrix():
    e = np.arange(N_EXPERTS * MOE_WIN) // MOE_WIN
    return jnp.asarray((np.arange(N_EXPERTS)[:, None] == e[None, :]).astype(np.float32), dtype=BF16)


def _slot_onehot(val_ref, eexp_ref):
    return lax.dot_general(val_ref[...].astype(BF16), eexp_ref[...], (((0,), (0,)), ((), ())),
                           preferred_element_type=F32)


def _dispatch_kernel(start_ref, kmax_ref, xn_ref, val_ref, tgt_ref, eexp_ref, xe_ref, stage, carry, sem):
    j = pl.program_id(0)
    nt = pl.num_programs(0)
    slot = lax.rem(j, 2)
    ne = N_EXPERTS

    cap = xe_ref.shape[1] - MOE_PAD

    @pl.when(j == 0)
    def _():
        carry[...] = jnp.zeros_like(carry)
        stage[0, 0:MOE_PAD, :] = jnp.zeros((MOE_PAD, D_MODEL), BF16)
        fills = [pltpu.make_async_copy(stage.at[0, pl.ds(0, MOE_PAD)], xe_ref.at[e, pl.ds(cap, MOE_PAD)], sem.at[0])
                 for e in range(ne)]
        for f in fills:
            f.start()
        for f in fills:
            f.wait()

    def window_copy(sl, e, row0):
        return pltpu.make_async_copy(stage.at[sl, pl.ds(e * MOE_WIN, MOE_WIN)],
                                     xe_ref.at[e, pl.ds(row0, MOE_WIN)], sem.at[sl])

    def wait_windows(sl):
        for e in range(ne):
            window_copy(sl, e, 0).wait()

    rep = _slot_onehot(val_ref, eexp_ref)
    xn = xn_ref[...]
    row = lax.broadcasted_iota(jnp.int32, (MOE_ALIGN, D_MODEL), 0)

    def block(k, _):
        @pl.when(k > 0)
        def _():
            wait_windows(slot)

        lo = k * MOE_WIN
        pt = (rep == tgt_ref[...] + lo.astype(F32)).astype(BF16)
        comp = lax.dot_general(pt, xn, (((0,), (0,)), ((), ())), preferred_element_type=F32)
        stage[slot] = comp.astype(BF16)
        for e in range(ne):
            s = start_ref[j * ne + e]
            head = s & (MOE_ALIGN - 1)
            r0 = e * MOE_WIN

            @pl.when(k == 0)
            def _():
                fresh = stage[slot, r0:r0 + MOE_ALIGN, :]
                kept = carry[e * MOE_ALIGN:(e + 1) * MOE_ALIGN, :]
                stage[slot, r0:r0 + MOE_ALIGN, :] = jnp.where(row < head, kept, fresh)

            nxt = (head + start_ref[(j + 1) * ne + e] - s) & (-MOE_ALIGN)

            @pl.when((nxt >= lo) & (nxt < lo + MOE_WIN))
            def _():
                off = pl.multiple_of(nxt - lo, MOE_ALIGN)
                carry[e * MOE_ALIGN:(e + 1) * MOE_ALIGN, :] = stage[slot, pl.ds(r0 + off, MOE_ALIGN), :]

        @pl.when((k == 0) & (j > 0))
        def _():
            wait_windows(1 - slot)

        for e in range(ne):
            base = pl.multiple_of((start_ref[j * ne + e] & (-MOE_ALIGN)) + lo, MOE_ALIGN)
            window_copy(slot, e, base).start()
        return 0

    lax.fori_loop(0, kmax_ref[j], block, 0)

    @pl.when(j == nt - 1)
    def _():
        wait_windows(slot)


def _dispatch(xn, val, starts, kmax, tgt, eexp, cap, tile):
    n = xn.shape[0]
    nt = n // tile
    rows = N_EXPERTS * MOE_WIN
    return pl.pallas_call(
        _dispatch_kernel,
        grid_spec=pltpu.PrefetchScalarGridSpec(
            num_scalar_prefetch=2, grid=(nt,),
            in_specs=[pl.BlockSpec((tile, D_MODEL), lambda j, s, k: (j, 0)),
                      pl.BlockSpec((N_EXPERTS, tile), lambda j, s, k: (0, j)),
                      pl.BlockSpec((None, 1, rows), lambda j, s, k: (j, 0, 0)),
                      pl.BlockSpec((N_EXPERTS, rows), lambda j, s, k: (0, 0))],
            out_specs=pl.BlockSpec(memory_space=pl.ANY),
            scratch_shapes=[pltpu.VMEM((2, rows, D_MODEL), BF16),
                            pltpu.VMEM((N_EXPERTS * MOE_ALIGN, D_MODEL), BF16),
                            pltpu.SemaphoreType.DMA((2,))]),
        out_shape=jax.ShapeDtypeStruct((N_EXPERTS, cap + MOE_PAD, D_MODEL), BF16),
        compiler_params=_params(("arbitrary",)),
        name="moe_dispatch",
    )(starts, kmax, xn, val, tgt, eexp)


def _expert_kernel(x_ref, wr_ref, wg_ref, wu_ref, wd_ref, y_ref, *, ntile):
    e = pl.program_id(0)
    i = pl.program_id(1)

    @pl.when(i < ntile)
    def _():
        x = x_ref[...]
        logits = (jnp.dot(x, wr_ref[0], preferred_element_type=F32)
                  + jnp.dot(x, wr_ref[1], preferred_element_type=F32))
        lane = lax.broadcasted_iota(jnp.int32, logits.shape, 1)
        logits = jnp.where(lane < N_EXPERTS, logits, -jnp.inf)
        ex = jnp.exp(logits - jnp.max(logits, axis=-1, keepdims=True))
        gate = (jnp.sum(jnp.where(lane == e, ex, 0.0), axis=-1, keepdims=True)
                / jnp.sum(ex, axis=-1, keepdims=True))
        hid = []
        for c0 in range(0, EXPERT_FF, FF_CHUNK):
            hg = jnp.dot(x, wg_ref[:, c0:c0 + FF_CHUNK], preferred_element_type=F32)
            hu = jnp.dot(x, wu_ref[:, c0:c0 + FF_CHUNK], preferred_element_type=F32)
            hid.append((_silu(hg) * hu).astype(BF16))
        hid = jnp.concatenate(hid, axis=1)
        y_ref[...] = (jnp.dot(hid, wd_ref[...], preferred_element_type=F32) * gate).astype(BF16)

    @pl.when(i >= ntile)
    def _():
        y_ref[...] = jnp.zeros_like(y_ref)


def _expert_ffn(xe, wr2, wg, wu, wd, cap, tc):
    ne, rows, _ = xe.shape
    wspec = lambda shape: pl.BlockSpec((None,) + shape, lambda e, i: (e, 0, 0))
    return pl.pallas_call(
        functools.partial(_expert_kernel, ntile=cap // tc),
        grid=(ne, rows // tc),
        in_specs=[pl.BlockSpec((None, tc, D_MODEL), lambda e, i: (e, i, 0)),
                  pl.BlockSpec(wr2.shape, lambda e, i: (0, 0, 0)),
                  wspec((D_MODEL, EXPERT_FF)), wspec((D_MODEL, EXPERT_FF)), wspec((EXPERT_FF, D_MODEL))],
        out_specs=pl.BlockSpec((None, tc, D_MODEL), lambda e, i: (e, i, 0)),
        out_shape=jax.ShapeDtypeStruct((ne, rows, D_MODEL), BF16),
        compiler_params=_params(("parallel", "parallel")),
        name="expert_ffn",
    )(xe, wr2, wg, wu, wd)


def _combine_kernel(start_ref, kmax_ref, h_ref, p_ref, val_ref, tgt_ref, eexp_ref, gain_ref, wg_ref, wp_ref,
                    ye_ref, out_ref, stage, sem):
    j = pl.program_id(0)
    nt = pl.num_programs(0)
    slot = lax.rem(j, 2)
    ne = N_EXPERTS

    def window_copy(sl, e, row0):
        return pltpu.make_async_copy(ye_ref.at[e, pl.ds(row0, MOE_WIN)],
                                     stage.at[sl, pl.ds(e * MOE_WIN, MOE_WIN)], sem.at[sl])

    def fetch(sl, tile_idx, lo):
        for e in range(ne):
            base = pl.multiple_of((start_ref[tile_idx * ne + e] & (-MOE_ALIGN)) + lo, MOE_ALIGN)
            window_copy(sl, e, base).start()

    def wait_windows(sl):
        for e in range(ne):
            window_copy(sl, e, 0).wait()

    @pl.when(j == 0)
    def _():
        fetch(slot, j, 0)

    @pl.when(j + 1 < nt)
    def _():
        fetch(1 - slot, j + 1, 0)

    rep = _slot_onehot(val_ref, eexp_ref)
    wait_windows(slot)
    pt = (rep == tgt_ref[...]).astype(BF16)
    acc = h_ref[...] + jnp.dot(pt, stage[slot], preferred_element_type=F32)

    def extra(k, acc):
        lo = k * MOE_WIN
        fetch(slot, j, lo)
        wait_windows(slot)
        pk = (rep == tgt_ref[...] + lo.astype(F32)).astype(BF16)
        return acc + jnp.dot(pk, stage[slot], preferred_element_type=F32)

    x = lax.fori_loop(1, kmax_ref[j], extra, acc)
    ms = jnp.mean(x * x, axis=-1, keepdims=True)
    xn = (x * lax.rsqrt(ms + NORM_EPS) * gain_ref[...]).astype(BF16)
    gate = _sigmoid(jnp.dot(xn, wg_ref[...], preferred_element_type=F32))
    proj = jnp.dot(p_ref[...].astype(BF16), wp_ref[...], preferred_element_type=F32)
    out_ref[...] = x + gate * proj


def _combine_ple(h2, p2, ye, val, starts, kmax, tgt, eexp, gain, wg, wp, tile):
    n = h2.shape[0]
    nt = n // tile
    rows = N_EXPERTS * MOE_WIN
    fixed = lambda j, s, k: (0, 0)
    return pl.pallas_call(
        _combine_kernel,
        grid_spec=pltpu.PrefetchScalarGridSpec(
            num_scalar_prefetch=2, grid=(nt,),
            in_specs=[pl.BlockSpec((tile, D_MODEL), lambda j, s, k: (j, 0)),
                      pl.BlockSpec((tile, PLE_DIM), lambda j, s, k: (j, 0)),
                      pl.BlockSpec((N_EXPERTS, tile), lambda j, s, k: (0, j)),
                      pl.BlockSpec((None, 1, rows), lambda j, s, k: (j, 0, 0)),
                      pl.BlockSpec((N_EXPERTS, rows), fixed),
                      pl.BlockSpec((1, D_MODEL), fixed), pl.BlockSpec(wg.shape, fixed),
                      pl.BlockSpec(wp.shape, fixed),
                      pl.BlockSpec(memory_space=pl.ANY)],
            out_specs=pl.BlockSpec((tile, D_MODEL), lambda j, s, k: (j, 0)),
            scratch_shapes=[pltpu.VMEM((2, rows, D_MODEL), BF16), pltpu.SemaphoreType.DMA((2,))]),
        out_shape=jax.ShapeDtypeStruct((n, D_MODEL), F32),
        compiler_params=_params(("arbitrary",)),
        name="moe_combine_ple",
    )(starts, kmax, h2, p2, val, tgt, eexp, gain, wg, wp, ye)


def _in_perm():
    o_dnz = ZA + ZB
    o_beta = o_dnz + DN_WIDTH
    o_alpha = o_beta + 2 * DN_HEADS
    o_glu = o_alpha + 2 * DN_HEADS
    cols = list(range(0, o_beta)) + list(range(o_glu, o_glu + 2 * CONV_CH))
    for d in range(2):
        cols += [o_beta + d * DN_HEADS + h for h in range(DN_HEADS)]
        cols += [o_alpha + d * DN_HEADS + h for h in range(DN_HEADS)]
    return np.asarray(cols, dtype=np.int32)


def _prep_layer(lw):
    (norm_mix, w_in, q_gain, k_gain, sink, dn_conv, dn_a_log, dn_dt_bias, dn_out_gain,
     cv_dw, cv_dw_bias, cv_ln_gain, cv_ln_bias, w_out, norm_ffn, w_router, w_gate, w_up, w_down,
     norm_ple, w_ple_gate, w_ple_proj) = lw
    perm = _in_perm()
    w_perm = jnp.pad(w_in[:, perm], ((0, 0), (0, ZW - perm.shape[0]))).astype(BF16)
    hgain = jnp.concatenate([jnp.tile(q_gain, ATT_HEADS) * (ATT_HEAD_DIM ** -0.5),
                             jnp.tile(k_gain, ATT_KV_HEADS)]).reshape(1, -1)
    zeros4 = jnp.zeros((DN_HEADS,), F32)
    aneg = -jnp.exp(dn_a_log.astype(F32))
    aneg_row = jnp.concatenate([zeros4, aneg[0], zeros4, aneg[1]])
    dtb_row = jnp.concatenate([zeros4, dn_dt_bias[0], zeros4, dn_dt_bias[1]])
    pad = lambda r: jnp.pad(r, (0, LANES - r.shape[0])).reshape(1, LANES)
    wr = jnp.pad(w_router.astype(F32), ((0, 0), (0, LANES - N_EXPERTS)))
    wr_hi = wr.astype(BF16)
    wr2 = jnp.stack([wr_hi, (wr - wr_hi.astype(F32)).astype(BF16)])
    return dict(
        w_router2=wr2,
        norm_mix=norm_mix.reshape(1, -1), w_in=w_perm, hgain=hgain, sink=sink.astype(F32),
        dn_conv=dn_conv, aneg=pad(aneg_row), dtb=pad(dtb_row),
        dn_out_gain=jnp.tile(dn_out_gain, DN_HEADS).reshape(1, -1),
        cv_dw=cv_dw, cv_dw_bias=cv_dw_bias.reshape(1, -1), cv_ln_gain=cv_ln_gain.reshape(1, -1),
        cv_ln_bias=cv_ln_bias.reshape(1, -1), w_out=w_out.astype(BF16),
        norm_ffn=norm_ffn.reshape(1, -1),
        w_gate=w_gate.astype(BF16), w_up=w_up.astype(BF16), w_down=w_down.astype(BF16),
        norm_ple=norm_ple.reshape(1, -1), w_ple_gate=w_ple_gate.astype(BF16), w_ple_proj=w_ple_proj.astype(BF16))


def _tiles(bsz, seqlen):
    n = bsz * seqlen
    return dict(tm=min(512, n), tl=min(256, seqlen), ch=min(256, seqlen), tcv=min(512, seqlen))


def _moe_ple(h2, p2, pw, t):
    n = h2.shape[0]
    cap = CAPACITY_FACTOR * n // N_EXPERTS
    tile = min(MOE_TILE, n)
    xn, aff_t = _route(h2, pw["norm_ffn"], pw["w_router2"], t["tm"])
    val, cnt = _select(aff_t, cap, tile)
    starts, kmax, tgt = _moe_plan(cnt, n // tile)
    eexp = _expand_matrix()
    xe = _dispatch(xn, val, starts, kmax, tgt, eexp, cap, tile)
    ye = _expert_ffn(xe, pw["w_router2"], pw["w_gate"], pw["w_up"], pw["w_down"], cap, min(MOE_FFN_TILE, cap))
    return _combine_ple(h2, p2, ye, val, starts, kmax, tgt, eexp, pw["norm_ple"], pw["w_ple_gate"],
                        pw["w_ple_proj"], tile)


def _layer(h2, p2, pw, bsz, seqlen):
    t = _tiles(bsz, seqlen)
    hm_att = _head_mean_matrix(ATT_Q + ATT_KV, ATT_HEAD_DIM)
    hs_dn = _head_sum_matrix(2 * DN_WIDTH, DN_HEAD_DIM)
    hm_dn = _head_mean_matrix(DN_WIDTH, DN_HEAD_DIM)
    za, zb, zg, glu_in, gates = _in_proj(h2, pw["norm_mix"], pw["w_in"], hm_att, pw["hgain"], t["tm"])
    o_a = _attention(za, pw["sink"], bsz, seqlen)
    y, gb = _dn_prep(zb, gates, pw["dn_conv"], hs_dn, pw["aneg"], pw["dtb"], bsz, seqlen, t["tl"])
    o_f, o_b = _dn_chunk(y, gb, bsz, seqlen, t["ch"])
    o_c = _conformer_conv(glu_in, pw["cv_dw"], pw["cv_dw_bias"], pw["cv_ln_gain"], pw["cv_ln_bias"],
                          bsz, seqlen, t["tcv"])
    h2 = _out_proj(h2, o_a, o_f, o_b, zg, o_c, pw["dn_out_gain"], hm_dn, pw["w_out"], t["tm"])
    return _moe_ple(h2, p2, pw, t)


def _trunk(x, p, layer_weights):
    bsz, seqlen, _ = x.shape
    h2 = x.reshape(bsz * seqlen, D_MODEL)
    for i, pw in enumerate(layer_weights):
        h2 = _layer(h2, p[i].reshape(bsz * seqlen, PLE_DIM), pw, bsz, seqlen)
    return h2.reshape(bsz, seqlen, D_MODEL)


def kernel(x_prompt, x_sample, p_prompt, p_sample, norm_mix, w_in, q_gain, k_gain, sink, dn_conv, dn_a_log,
           dn_dt_bias, dn_out_gain, cv_dw, cv_dw_bias, cv_ln_gain, cv_ln_bias, w_out, norm_ffn, w_router,
           w_gate, w_up, w_down, norm_ple, w_ple_gate, w_ple_proj):
    weights = (norm_mix, w_in, q_gain, k_gain, sink, dn_conv, dn_a_log, dn_dt_bias, dn_out_gain,
               cv_dw, cv_dw_bias, cv_ln_gain, cv_ln_bias, w_out, norm_ffn, w_router, w_gate, w_up, w_down,
               norm_ple, w_ple_gate, w_ple_proj)
    depth = w_in.shape[0]
    layer_weights = [_prep_layer([w[i] for w in weights]) for i in range(depth)]
    return (_trunk(x_prompt, p_prompt, layer_weights), _trunk(x_sample, p_sample, layer_weights))
```

```python
import functools

import numpy as np
import jax
import jax.numpy as jnp
from jax import lax
from jax.experimental import pallas as pl
from jax.experimental.pallas import tpu as pltpu

F32 = jnp.float32
BF16 = jnp.bfloat16

D_MODEL = 1024
ATT_HEADS = 8
ATT_KV_HEADS = 2
ATT_HEAD_DIM = 64
ATT_GROUP = ATT_HEADS // ATT_KV_HEADS
WINDOW = 128
ATT_BLOCK = 128
DN_HEADS = 4
DN_HEAD_DIM = 64
DN_WIDTH = DN_HEADS * DN_HEAD_DIM
DN_CHUNK = 64
CONV_CH = 256
CONV_WIDTH = 31
ATT_Q = ATT_HEADS * ATT_HEAD_DIM
ATT_KV = ATT_KV_HEADS * ATT_HEAD_DIM
N_EXPERTS = 16
CAPACITY_FACTOR = 2
EXPERT_FF = 1024
PLE_DIM = 256
NORM_EPS = 1e-6

LANES = 128
SUBLANES = 8
VMEM_LIMIT = 48 * 1024 * 1024

ZA = ATT_Q + 2 * ATT_KV
ZB = 3 * DN_WIDTH
ZW = ZA + ZB + DN_WIDTH + 2 * CONV_CH + LANES


def _params(sem):
    return pltpu.CompilerParams(dimension_semantics=sem, vmem_limit_bytes=VMEM_LIMIT)


def _head_mean_matrix(width, head):
    idx = np.arange(width) // head
    return jnp.asarray((idx[:, None] == idx[None, :]).astype(np.float32) / head, dtype=BF16)


def _head_sum_matrix(width, head):
    idx = np.arange(width) // head
    return jnp.asarray((idx[:, None] == idx[None, :]).astype(np.float32), dtype=BF16)


def _sigmoid(x):
    return 1.0 / (1.0 + jnp.exp(-x))


def _silu(x):
    return x * _sigmoid(x)


def _in_proj_kernel(x_ref, gain_ref, w_ref, hm_ref, hgain_ref, za_ref, zb_ref, zg_ref, glu_ref, gates_ref):
    x = x_ref[...]
    ms = jnp.mean(x * x, axis=-1, keepdims=True)
    a = (x * lax.rsqrt(ms + NORM_EPS) * gain_ref[...]).astype(BF16)
    z = jnp.dot(a, w_ref[...], preferred_element_type=F32)
    nqk = ATT_Q + ATT_KV
    qk = z[:, :nqk]
    hms = jnp.dot((qk * qk).astype(BF16), hm_ref[...], preferred_element_type=F32)
    za_ref[:, :nqk] = (qk * lax.rsqrt(hms + NORM_EPS) * hgain_ref[...]).astype(BF16)
    za_ref[:, nqk:] = z[:, nqk:ZA].astype(BF16)
    zb_ref[...] = z[:, ZA:ZA + ZB]
    zg_ref[...] = z[:, ZA + ZB:ZA + ZB + DN_WIDTH]
    glu_ref[...] = z[:, ZA + ZB + DN_WIDTH:ZA + ZB + DN_WIDTH + 2 * CONV_CH]
    gates_ref[...] = z[:, ZW - LANES:]


def _in_proj(h2, gain, w_perm, hm, hgain, tm):
    n = h2.shape[0]
    row = lambda i: (i, 0)
    fixed = lambda i: (0, 0)
    return pl.pallas_call(
        _in_proj_kernel,
        grid=(n // tm,),
        in_specs=[pl.BlockSpec((tm, D_MODEL), row), pl.BlockSpec((1, D_MODEL), fixed),
                  pl.BlockSpec((D_MODEL, ZW), fixed), pl.BlockSpec(hm.shape, fixed),
                  pl.BlockSpec(hgain.shape, fixed)],
        out_specs=[pl.BlockSpec((tm, ZA), row), pl.BlockSpec((tm, ZB), row), pl.BlockSpec((tm, DN_WIDTH), row),
                   pl.BlockSpec((tm, 2 * CONV_CH), row), pl.BlockSpec((tm, LANES), row)],
        out_shape=[jax.ShapeDtypeStruct((n, ZA), BF16), jax.ShapeDtypeStruct((n, ZB), F32),
                   jax.ShapeDtypeStruct((n, DN_WIDTH), F32), jax.ShapeDtypeStruct((n, 2 * CONV_CH), F32),
                   jax.ShapeDtypeStruct((n, LANES), F32)],
        compiler_params=_params(("parallel",)),
        name="in_proj",
    )(h2, gain, w_perm, hm, hgain)


ATT_MASKED = -1e30


def _attn_bias_table():
    i = np.arange(ATT_BLOCK)[:, None]
    c = np.arange(3 * ATT_BLOCK)[None, :]
    rel = c - ATT_BLOCK - i
    slopes = 2.0 ** (-8.0 * np.arange(1, ATT_HEADS + 1) / ATT_HEADS)
    table = np.empty((3, ATT_KV_HEADS, ATT_GROUP * ATT_BLOCK, 3 * ATT_BLOCK), np.float32)
    for variant in range(3):
        ok = np.abs(rel) <= WINDOW
        if variant == 0:
            ok = ok & (c >= ATT_BLOCK)
        if variant == 2:
            ok = ok & (c < 2 * ATT_BLOCK)
        for hd in range(ATT_HEADS):
            g, j = divmod(hd, ATT_GROUP)
            table[variant, g, j * ATT_BLOCK:(j + 1) * ATT_BLOCK] = np.where(ok, -slopes[hd] * np.abs(rel), ATT_MASKED)
    return jnp.asarray(table)


def _attn_kernel(sink_ref, q_ref, kvp_ref, kvo_ref, kvn_ref, bias_ref, o_ref):
    kv = jnp.concatenate([kvp_ref[...], kvo_ref[...], kvn_ref[...]], axis=0)
    hd_ = ATT_HEAD_DIM
    groups = range(ATT_KV_HEADS)
    heads = range(ATT_HEADS)
    ks = [kv[:, g * hd_:(g + 1) * hd_].astype(BF16) for g in groups]
    vs = [kv[:, ATT_KV + g * hd_:ATT_KV + (g + 1) * hd_].astype(BF16) for g in groups]
    qs = [jnp.concatenate([q_ref[:, (g * ATT_GROUP + j) * hd_:(g * ATT_GROUP + j + 1) * hd_]
                           for j in range(ATT_GROUP)], axis=0).astype(BF16) for g in groups]
    sg = [lax.dot_general(qs[g], ks[g], (((1,), (1,)), ((), ())), preferred_element_type=F32) + bias_ref[g]
          for g in groups]
    rows = lambda t, hd: t[(hd % ATT_GROUP) * ATT_BLOCK:(hd % ATT_GROUP + 1) * ATT_BLOCK]
    s = [rows(sg[hd // ATT_GROUP], hd) for hd in heads]
    m = [jnp.maximum(jnp.max(s[hd], axis=-1, keepdims=True), sink_ref[hd]) for hd in heads]
    e = [jnp.exp(s[hd] - m[hd]) for hd in heads]
    den = [jnp.sum(e[hd], axis=-1, keepdims=True) + jnp.exp(sink_ref[hd] - m[hd]) for hd in heads]
    eg = [jnp.concatenate([e[g * ATT_GROUP + j].astype(BF16) for j in range(ATT_GROUP)], axis=0) for g in groups]
    og = [jnp.dot(eg[g], vs[g], preferred_element_type=F32) for g in groups]
    for hd in heads:
        o_ref[:, hd * hd_:(hd + 1) * hd_] = rows(og[hd // ATT_GROUP], hd) / den[hd]


def _attention(za, sink, bsz, seqlen):
    nb = seqlen // ATT_BLOCK
    assert nb >= 2
    za3 = za.reshape(bsz, seqlen, ZA)
    kvw = 2 * ATT_KV
    kvc = ATT_Q // kvw
    bias = _attn_bias_table()
    return pl.pallas_call(
        _attn_kernel,
        grid=(bsz, nb),
        in_specs=[pl.BlockSpec(memory_space=pltpu.SMEM),
                  pl.BlockSpec((None, ATT_BLOCK, ATT_Q), lambda b, n: (b, n, 0)),
                  pl.BlockSpec((None, ATT_BLOCK, kvw), lambda b, n: (b, jnp.maximum(n - 1, 0), kvc)),
                  pl.BlockSpec((None, ATT_BLOCK, kvw), lambda b, n: (b, n, kvc)),
                  pl.BlockSpec((None, ATT_BLOCK, kvw), lambda b, n: (b, jnp.minimum(n + 1, nb - 1), kvc)),
                  pl.BlockSpec((None,) + bias.shape[1:],
                               lambda b, n: (jnp.where(n == 0, 0, jnp.where(n == nb - 1, 2, 1)), 0, 0, 0))],
        out_specs=pl.BlockSpec((None, ATT_BLOCK, ATT_Q), lambda b, n: (b, n, 0)),
        out_shape=jax.ShapeDtypeStruct((bsz, seqlen, ATT_Q), F32),
        compiler_params=_params(("parallel", "parallel")),
        name="window_attention",
    )(sink, za3, za3, za3, za3, bias).reshape(bsz * seqlen, ATT_Q)


DN_HALO = SUBLANES


def _dn_prep_kernel(x_ref, xp_ref, xn_ref, cw_ref, hs_ref, g_ref, aneg_ref, dtb_ref, mf_ref, mb_ref,
                    y_ref, gb_ref, buf_ref, *, tl):
    i = pl.program_id(1)
    nt = pl.num_programs(1)
    buf_ref[0:DN_HALO, :] = jnp.where(i > 0, xp_ref[...], 0.0)
    buf_ref[DN_HALO:DN_HALO + tl, :] = x_ref[...]
    buf_ref[DN_HALO + tl:, :] = jnp.where(i < nt - 1, xn_ref[...], 0.0)
    y = (cw_ref[0:1, :] * buf_ref[DN_HALO - 1:DN_HALO - 1 + tl, :]
         + cw_ref[1:2, :] * buf_ref[DN_HALO:DN_HALO + tl, :]
         + cw_ref[2:3, :] * buf_ref[DN_HALO + 1:DN_HALO + 1 + tl, :])
    y = _silu(y)
    qk = y[:, :2 * DN_WIDTH]
    ss = jnp.dot((qk * qk).astype(BF16), hs_ref[...], preferred_element_type=F32)
    lane = lax.broadcasted_iota(jnp.int32, (tl, 2 * DN_WIDTH), 1)
    scale = jnp.where(lane < DN_WIDTH, DN_HEAD_DIM ** -0.5, 1.0)
    y_ref[:, :2 * DN_WIDTH] = qk * lax.rsqrt(ss + NORM_EPS) * scale
    y_ref[:, 2 * DN_WIDTH:] = y[:, 2 * DN_WIDTH:]
    raw = g_ref[...]
    col = lax.broadcasted_iota(jnp.int32, (tl, LANES), 1)
    is_beta = (col & DN_HEADS) == 0
    t = raw + dtb_ref[...]
    softplus = jnp.maximum(t, 0.0) + jnp.log(1.0 + jnp.exp(-jnp.abs(t)))
    vals = jnp.where(is_beta, _sigmoid(raw), aneg_ref[...] * softplus)
    cf = jnp.dot(mf_ref[...], vals, preferred_element_type=F32, precision=lax.Precision.HIGHEST)
    cb = jnp.dot(mb_ref[...], vals, preferred_element_type=F32, precision=lax.Precision.HIGHEST)
    gb_ref[0] = jnp.where(is_beta, vals, cf)
    gb_ref[1] = pltpu.roll(jnp.where(is_beta, vals, cb), LANES - 2 * DN_HEADS, axis=1)


def _dn_prep(zb, gates, conv_w, hs, aneg, dtb, bsz, seqlen, tl):
    zb3 = zb.reshape(bsz, seqlen, ZB)
    g3 = gates.reshape(bsz, seqlen, LANES)
    nt = seqlen // tl
    hb = tl // DN_HALO
    ch = np.arange(tl) // DN_CHUNK
    same = ch[:, None] == ch[None, :]
    pos = np.arange(tl)
    mf = jnp.asarray((same & (pos[None, :] <= pos[:, None])).astype(np.float32))
    mb = jnp.asarray((same & (pos[None, :] >= pos[:, None])).astype(np.float32))
    fixed = lambda b, i: (0, 0)
    y, gb = pl.pallas_call(
        functools.partial(_dn_prep_kernel, tl=tl),
        grid=(bsz, nt),
        in_specs=[pl.BlockSpec((None, tl, ZB), lambda b, i: (b, i, 0)),
                  pl.BlockSpec((None, DN_HALO, ZB), lambda b, i: (b, jnp.maximum(i * hb - 1, 0), 0)),
                  pl.BlockSpec((None, DN_HALO, ZB), lambda b, i: (b, jnp.minimum((i + 1) * hb, nt * hb - 1), 0)),
                  pl.BlockSpec(conv_w.shape, fixed), pl.BlockSpec(hs.shape, fixed),
                  pl.BlockSpec((None, tl, LANES), lambda b, i: (b, i, 0)),
                  pl.BlockSpec((1, LANES), fixed), pl.BlockSpec((1, LANES), fixed),
                  pl.BlockSpec((tl, tl), fixed), pl.BlockSpec((tl, tl), fixed)],
        out_specs=[pl.BlockSpec((None, tl, ZB), lambda b, i: (b, i, 0)),
                   pl.BlockSpec((2, None, tl, LANES), lambda b, i: (0, b, i, 0))],
        out_shape=[jax.ShapeDtypeStruct((bsz, seqlen, ZB), F32),
                   jax.ShapeDtypeStruct((2, bsz, seqlen, LANES), F32)],
        scratch_shapes=[pltpu.VMEM((tl + 2 * DN_HALO, ZB), F32)],
        compiler_params=_params(("parallel", "parallel")),
        name="deltanet_prep",
    )(zb3, zb3, zb3, conv_w, hs, g3, aneg, dtb, mf, mb)
    return y, gb


def _lane_expand(cols, first):
    c = cols.shape[0]
    lane = lax.broadcasted_iota(jnp.int32, (c, LANES), 1)
    halves = []
    for h in range(0, DN_HEADS, 2):
        a = jnp.broadcast_to(cols[:, first + h:first + h + 1], (c, LANES))
        b = jnp.broadcast_to(cols[:, first + h + 1:first + h + 2], (c, LANES))
        halves.append(jnp.where(lane < DN_HEAD_DIM, a, b))
    return jnp.concatenate(halves, axis=1)


def _dn_pair_kernel(xf_ref, xb_ref, gf_ref, gb_ref, of_ref, ob_ref, sf_ref, sb_ref, *, nsub):
    c = DN_CHUNK
    w = DN_WIDTH

    @pl.when(pl.program_id(1) == 0)
    def _():
        sf_ref[...] = jnp.zeros_like(sf_ref)
        sb_ref[...] = jnp.zeros_like(sb_ref)

    r_cat = lax.broadcasted_iota(jnp.int32, (c, w), 0)
    s_cat = lax.broadcasted_iota(jnp.int32, (c, w), 1) & (DN_HEAD_DIM - 1)
    eye_cat = s_cat == r_cat
    rr = lax.broadcasted_iota(jnp.int32, (w, w), 0)
    cc = lax.broadcasted_iota(jnp.int32, (w, w), 1)
    head = (rr >> 6) == (cc >> 6)
    head_b = head.astype(BF16)
    m16 = (s_cat >> 4) == (r_cat >> 4)
    m32 = (s_cat >> 5) == (r_cat >> 5)
    off16 = m32 & jnp.logical_not(m16)
    off32 = jnp.logical_not(m32)
    eye_f = eye_cat.astype(F32)

    def bd(t):
        return jnp.concatenate([t] * DN_HEADS, axis=0) * head_b

    def mm(a, b):
        return jnp.dot(a, b, preferred_element_type=F32)

    chunks = [(0, i * c) for i in range(nsub)] + [(1, (nsub - 1 - i) * c) for i in range(nsub)]
    xrefs = (xf_ref, xb_ref)
    grefs = (gf_ref, gb_ref)
    orefs = (of_ref, ob_ref)
    srefs = (sf_ref, sb_ref)
    incl = (s_cat <= r_cat, s_cat >= r_cat)
    strict = (s_cat < r_cat, s_cat > r_cat)
    last_row = (c - 1, 0)

    pre = []
    for d, st in chunks:
        x = xrefs[d][st:st + c, :]
        q, k, v = x[:, :w], x[:, w:2 * w], x[:, 2 * w:]
        gbt = grefs[d][st:st + c, :]
        beta = _lane_expand(gbt, 0)
        gc = _lane_expand(gbt, DN_HEADS)
        grow = jnp.sum(jnp.where(eye_cat, gc, 0.0), axis=0, keepdims=True)
        decay = jnp.exp(jnp.where(incl[d], gc - grow, -jnp.inf))
        glast = gc[last_row[d]:last_row[d] + 1, :]
        egc = jnp.exp(gc)
        kb = k * beta
        pre.append(dict(d=d, st=st, q=q, k=k, kb=kb, vb=v * beta, decay=decay, glast=glast, egc=egc,
                        kdec=(k * jnp.exp(glast - gc)).astype(BF16)))

    kks = [lax.dot_general(jnp.concatenate([p["kb"], p["q"]], axis=0).astype(BF16), bd(p["k"].astype(BF16)),
                           (((1,), (1,)), ((), ())), preferred_element_type=F32) for p in pre]
    a = [jnp.where(strict[p["d"]], kk[:c] * p["decay"], 0.0) for p, kk in zip(pre, kks)]
    intra = [jnp.where(incl[p["d"]], kk[c:] * p["decay"], 0.0).astype(BF16) for p, kk in zip(pre, kks)]
    xm = [jnp.where(m16, -t, 0.0) for t in a]
    xm_b = [t.astype(BF16) for t in xm]
    x2_b = [mm(t, bd(t)).astype(BF16) for t in xm_b]
    x2_d = [bd(t) for t in x2_b]
    dinv = [eye_f + t for t in xm]
    dinv = [t + mm(t.astype(BF16), s2) for t, s2 in zip(dinv, x2_d)]
    x4_b = [mm(t, s2).astype(BF16) for t, s2 in zip(x2_b, x2_d)]
    x4_d = [bd(t) for t in x4_b]
    dinv = [t + mm(t.astype(BF16), s4) for t, s4 in zip(dinv, x4_d)]
    x8_d = [bd(mm(t, s4).astype(BF16)) for t, s4 in zip(x4_b, x4_d)]
    dinv = [t + mm(t.astype(BF16), s8) for t, s8 in zip(dinv, x8_d)]
    dinv_b = [t.astype(BF16) for t in dinv]
    n32 = [bd(mm(jnp.where(off16, t, 0.0).astype(BF16), bd(db)).astype(BF16)) for t, db in zip(a, dinv_b)]
    t32 = [t - mm(db, n) for t, db, n in zip(dinv, dinv_b, n32)]
    t32_b = [t.astype(BF16) for t in t32]
    n64 = [bd(mm(jnp.where(off32, t, 0.0).astype(BF16), bd(tb)).astype(BF16)) for t, tb in zip(a, t32_b)]
    t_cat = [(t - mm(tb, n)).astype(BF16) for t, tb, n in zip(t32, t32_b, n64)]
    uw = [mm(tc, jnp.concatenate([bd(p["vb"].astype(BF16)), bd((p["kb"] * p["egc"]).astype(BF16))], axis=1))
          for tc, p in zip(t_cat, pre)]
    uw_b = [t.astype(BF16) for t in uw]
    pn = [lax.dot_general(p["kdec"], t, (((0,), (0,)), ((), ())), preferred_element_type=F32)
          for p, t in zip(pre, uw_b)]
    qo = [mm(it, jnp.concatenate([bd(t[:, :w]), bd(t[:, w:])], axis=1)) for it, t in zip(intra, uw_b)]
    lhs = [jnp.concatenate([n[:, w:].astype(BF16) * head_b, (p["q"] * p["egc"] - o[:, w:]).astype(BF16)], axis=0)
           for n, o, p in zip(pn, qo, pre)]
    for step in range(nsub):
        for d in range(2):
            i = d * nsub + step
            p = pre[i]
            state = srefs[d][...]
            r = jnp.dot(lhs[i], state.astype(BF16), preferred_element_type=F32)
            orefs[d][p["st"]:p["st"] + c, :] = r[w:] + qo[i][:, :w]
            srefs[d][...] = state * jnp.exp(p["glast"]) - r[:w] + jnp.where(head, pn[i][:, :w], 0.0)


def _dn_chunk(y, gb, bsz, seqlen, ch):
    nsub = ch // DN_CHUNK
    nblk = seqlen // ch
    fwd = lambda b, j: (b, j, 0)
    bwd = lambda b, j: (b, nblk - 1 - j, 0)
    o_f, o_b = pl.pallas_call(
        functools.partial(_dn_pair_kernel, nsub=nsub),
        grid=(bsz, nblk),
        in_specs=[pl.BlockSpec((None, ch, ZB), fwd), pl.BlockSpec((None, ch, ZB), bwd),
                  pl.BlockSpec((None, None, ch, LANES), lambda b, j: (0, b, j, 0)),
                  pl.BlockSpec((None, None, ch, LANES), lambda b, j: (1, b, nblk - 1 - j, 0))],
        out_specs=[pl.BlockSpec((None, ch, DN_WIDTH), fwd), pl.BlockSpec((None, ch, DN_WIDTH), bwd)],
        out_shape=[jax.ShapeDtypeStruct((bsz, seqlen, DN_WIDTH), F32)] * 2,
        scratch_shapes=[pltpu.VMEM((DN_WIDTH, DN_WIDTH), F32)] * 2,
        compiler_params=_params(("parallel", "arbitrary")),
        name="deltanet_chunks",
    )(y, y, gb, gb)
    return o_f.reshape(bsz * seqlen, DN_WIDTH), o_b.reshape(bsz * seqlen, DN_WIDTH)


CV_HALO = 2 * SUBLANES
CV_PAD = (CONV_WIDTH - 1) // 2


def _conv_kernel(x_ref, xp_ref, xn_ref, dw_ref, bias_ref, lng_ref, lnb_ref, o_ref, buf_ref, shift_ref, *, tl):
    i = pl.program_id(1)
    nt = pl.num_programs(1)

    def glu(t):
        return t[:, :CONV_CH] * _sigmoid(t[:, CONV_CH:])

    buf_ref[0:CV_HALO, :] = jnp.where(i > 0, glu(xp_ref[...]), 0.0)
    buf_ref[CV_HALO:CV_HALO + tl, :] = glu(x_ref[...])
    buf_ref[CV_HALO + tl:, :] = jnp.where(i < nt - 1, glu(xn_ref[...]), 0.0)
    acc = jnp.zeros((tl, CONV_CH), F32) + bias_ref[...]
    first = CV_HALO - CV_PAD
    span = -(-(first + CONV_WIDTH) // SUBLANES) * SUBLANES - SUBLANES
    for sub in range(SUBLANES):
        shift_ref[...] = buf_ref[sub:sub + tl + span, :]
        for base in range(0, span + 1, SUBLANES):
            j = base + sub - first
            if 0 <= j < CONV_WIDTH:
                acc = acc + dw_ref[j:j + 1, :] * shift_ref[base:base + tl, :]
    mu = jnp.mean(acc, axis=-1, keepdims=True)
    cen = acc - mu
    var = jnp.mean(cen * cen, axis=-1, keepdims=True)
    o_ref[...] = _silu(cen * lax.rsqrt(var + NORM_EPS) * lng_ref[...] + lnb_ref[...])


def _conformer_conv(glu_in, dw, bias, lng, lnb, bsz, seqlen, tl):
    x3 = glu_in.reshape(bsz, seqlen, 2 * CONV_CH)
    nt = seqlen // tl
    hb = tl // CV_HALO
    fixed = lambda b, i: (0, 0)
    return pl.pallas_call(
        functools.partial(_conv_kernel, tl=tl),
        grid=(bsz, nt),
        in_specs=[pl.BlockSpec((None, tl, 2 * CONV_CH), lambda b, i: (b, i, 0)),
                  pl.BlockSpec((None, CV_HALO, 2 * CONV_CH), lambda b, i: (b, jnp.maximum(i * hb - 1, 0), 0)),
                  pl.BlockSpec((None, CV_HALO, 2 * CONV_CH),
                               lambda b, i: (b, jnp.minimum((i + 1) * hb, nt * hb - 1), 0)),
                  pl.BlockSpec(dw.shape, fixed), pl.BlockSpec((1, CONV_CH), fixed),
                  pl.BlockSpec((1, CONV_CH), fixed), pl.BlockSpec((1, CONV_CH), fixed)],
        out_specs=pl.BlockSpec((None, tl, CONV_CH), lambda b, i: (b, i, 0)),
        out_shape=jax.ShapeDtypeStruct((bsz, seqlen, CONV_CH), F32),
        scratch_shapes=[pltpu.VMEM((tl + 2 * CV_HALO, CONV_CH), F32),
                        pltpu.VMEM((tl + 2 * CV_HALO - SUBLANES, CONV_CH), F32)],
        compiler_params=_params(("parallel", "parallel")),
        name="conformer_conv",
    )(x3, x3, x3, dw, bias, lng, lnb).reshape(bsz * seqlen, CONV_CH)


def _out_proj_kernel(h_ref, oa_ref, of_ref, ob_ref, zg_ref, oc_ref, og_ref, hm_ref, w_ref, out_ref):
    ob = of_ref[...] + ob_ref[...]
    ms = jnp.dot((ob * ob).astype(BF16), hm_ref[...], preferred_element_type=F32)
    obn = ob * lax.rsqrt(ms + NORM_EPS) * og_ref[...]
    ob2 = obn * _silu(zg_ref[...])
    mix = jnp.concatenate([oa_ref[...], ob2, oc_ref[...]], axis=1).astype(BF16)
    out_ref[...] = h_ref[...] + jnp.dot(mix, w_ref[...], preferred_element_type=F32)


def _out_proj(h2, oa, o_f, o_b, zg, oc, og, hm, w, tm):
    n = h2.shape[0]
    row = lambda i: (i, 0)
    fixed = lambda i: (0, 0)
    return pl.pallas_call(
        _out_proj_kernel,
        grid=(n // tm,),
        in_specs=[pl.BlockSpec((tm, D_MODEL), row), pl.BlockSpec((tm, ATT_Q), row),
                  pl.BlockSpec((tm, DN_WIDTH), row), pl.BlockSpec((tm, DN_WIDTH), row),
                  pl.BlockSpec((tm, DN_WIDTH), row),
                  pl.BlockSpec((tm, CONV_CH), row), pl.BlockSpec((1, DN_WIDTH), fixed),
                  pl.BlockSpec(hm.shape, fixed), pl.BlockSpec(w.shape, fixed)],
        out_specs=pl.BlockSpec((tm, D_MODEL), row),
        out_shape=jax.ShapeDtypeStruct((n, D_MODEL), F32),
        compiler_params=_params(("parallel",)),
        name="out_proj",
    )(h2, oa, o_f, o_b, zg, oc, og, hm, w)


def _route_kernel(h_ref, gain_ref, wr_ref, xn_ref, aff_ref):
    x = h_ref[...]
    ms = jnp.mean(x * x, axis=-1, keepdims=True)
    xn = x * lax.rsqrt(ms + NORM_EPS) * gain_ref[...]
    xn_hi = xn.astype(BF16)
    xn_ref[...] = xn_hi
    xn_lo = (xn - xn_hi.astype(F32)).astype(BF16)
    logits = (jnp.dot(xn_hi, wr_ref[0], preferred_element_type=F32)
              + (jnp.dot(xn_lo, wr_ref[0], preferred_element_type=F32)
                 + jnp.dot(xn_hi, wr_ref[1], preferred_element_type=F32)))
    lane = lax.broadcasted_iota(jnp.int32, logits.shape, 1)
    logits = jnp.where(lane < N_EXPERTS, logits, -jnp.inf)
    m = jnp.max(logits, axis=-1, keepdims=True)
    e = jnp.exp(logits - m)
    aff = e / jnp.sum(e, axis=-1, keepdims=True)
    aff_ref[...] = jnp.transpose(aff)[:N_EXPERTS, :]


def _route(h2, gain, wr_pad, tm):
    n = h2.shape[0]
    row = lambda i: (i, 0)
    fixed = lambda i: (0, 0)
    return pl.pallas_call(
        _route_kernel,
        grid=(n // tm,),
        in_specs=[pl.BlockSpec((tm, D_MODEL), row), pl.BlockSpec((1, D_MODEL), fixed),
                  pl.BlockSpec((2, D_MODEL, LANES), lambda i: (0, 0, 0))],
        out_specs=[pl.BlockSpec((tm, D_MODEL), row), pl.BlockSpec((N_EXPERTS, tm), lambda i: (0, i))],
        out_shape=[jax.ShapeDtypeStruct((n, D_MODEL), BF16), jax.ShapeDtypeStruct((N_EXPERTS, n), F32)],
        compiler_params=_params(("parallel",)),
        name="moe_route",
    )(h2, gain, wr_pad)


MOE_TILE = 256
MOE_ALIGN = 2 * SUBLANES
MOE_WIN = 80
MOE_PAD = 1024
MOE_FFN_TILE = 1024
FF_CHUNK = 256
MOE_UNSELECTED = -64.0


def _select_kernel(aff_ref, tri_ref, val_ref, cnt_ref, *, cap, tile):
    ne, n = aff_ref.shape
    nt = n // tile
    capf = float(cap)

    def bits_of(x):
        return lax.bitcast_convert_type(x, jnp.int32)

    def search(i, thr):
        cand = thr | jnp.left_shift(jnp.int32(1), 30 - i)
        cnt = jnp.sum((bits_of(aff_ref[...]) >= cand).astype(F32), axis=1, keepdims=True)
        return jnp.where(cnt >= capf, cand, thr)

    thr = lax.fori_loop(0, 31, search, jnp.zeros((ne, 1), jnp.int32))
    n_gt = jnp.sum((bits_of(aff_ref[...]) > thr).astype(F32), axis=1, keepdims=True)
    need = capf - n_gt
    lane = lax.broadcasted_iota(jnp.int32, (ne, LANES), 1)

    def tile_body(j, carry):
        eq_before, cnt_acc = carry
        off = pl.multiple_of(j * tile, tile)
        b = bits_of(aff_ref[:, pl.ds(off, tile)])
        gt = b > thr
        eqf = (b == thr).astype(F32)
        eq_rank = eq_before + jnp.dot(eqf.astype(BF16), tri_ref[...], preferred_element_type=F32)
        self_ = jnp.where(gt, 1.0, jnp.where(eq_rank <= need, eqf, 0.0))
        rank = jnp.dot(self_.astype(BF16), tri_ref[...], preferred_element_type=F32)
        val_ref[:, pl.ds(off, tile)] = jnp.where(self_ > 0.0, rank, MOE_UNSELECTED)
        cnt = jnp.sum(self_, axis=1, keepdims=True)
        return (eq_before + jnp.sum(eqf, axis=1, keepdims=True), cnt_acc + jnp.where(lane == j, cnt, 0.0))

    init = (jnp.zeros((ne, 1), F32), jnp.zeros((ne, LANES), F32))
    _, cnt_acc = lax.fori_loop(0, nt, tile_body, init)
    cnt_ref[...] = cnt_acc


def _select(aff_t, cap, tile):
    ne, n = aff_t.shape
    assert n // tile <= LANES
    tri = jnp.asarray(np.triu(np.ones((tile, tile), np.float32)), dtype=BF16)
    return pl.pallas_call(
        functools.partial(_select_kernel, cap=cap, tile=tile),
        out_shape=[jax.ShapeDtypeStruct((ne, n), F32), jax.ShapeDtypeStruct((ne, LANES), F32)],
        compiler_params=pltpu.CompilerParams(vmem_limit_bytes=VMEM_LIMIT),
        name="moe_select",
    )(aff_t, tri)


def _moe_plan(cnt, nt):
    c = cnt[:, :nt].astype(jnp.int32).T
    starts = jnp.concatenate([jnp.zeros((1, N_EXPERTS), jnp.int32), jnp.cumsum(c, axis=0)], axis=0)
    head = starts[:-1] & (MOE_ALIGN - 1)
    kmax = jnp.maximum(jnp.max((head + c + MOE_WIN - 1) // MOE_WIN, axis=1), 1).astype(jnp.int32)
    w = jnp.arange(MOE_WIN, dtype=jnp.int32)
    tgt = (w[None, None, :] + 1 - head[:, :, None]).astype(F32).reshape(nt, 1, N_EXPERTS * MOE_WIN)
    return starts.reshape(-1), kmax, tgt


def _expand_matrix():
    e = np.arange(N_EXPERTS * MOE_WIN) // MOE_WIN
    return jnp.asarray((np.arange(N_EXPERTS)[:, None] == e[None, :]).astype(np.float32), dtype=BF16)


def _slot_onehot(val_ref, eexp_ref):
    return lax.dot_general(val_ref[...].astype(BF16), eexp_ref[...], (((0,), (0,)), ((), ())),
                           preferred_element_type=F32)


def _dispatch_kernel(start_ref, kmax_ref, xn_ref, val_ref, tgt_ref, eexp_ref, xe_ref, stage, carry, sem):
    j = pl.program_id(0)
    nt = pl.num_programs(0)
    slot = lax.rem(j, 2)
    ne = N_EXPERTS

    cap = xe_ref.shape[1] - MOE_PAD

    @pl.when(j == 0)
    def _():
        carry[...] = jnp.zeros_like(carry)
        stage[0, 0:MOE_PAD, :] = jnp.zeros((MOE_PAD, D_MODEL), BF16)
        fills = [pltpu.make_async_copy(stage.at[0, pl.ds(0, MOE_PAD)], xe_ref.at[e, pl.ds(cap, MOE_PAD)], sem.at[0])
                 for e in range(ne)]
        for f in fills:
            f.start()
        for f in fills:
            f.wait()

    def window_copy(sl, e, row0):
        return pltpu.make_async_copy(stage.at[sl, pl.ds(e * MOE_WIN, MOE_WIN)],
                                     xe_ref.at[e, pl.ds(row0, MOE_WIN)], sem.at[sl])

    def wait_windows(sl):
        for e in range(ne):
            window_copy(sl, e, 0).wait()

    rep = _slot_onehot(val_ref, eexp_ref)
    xn = xn_ref[...]
    row = lax.broadcasted_iota(jnp.int32, (MOE_ALIGN, D_MODEL), 0)

    def block(k, _):
        @pl.when(k > 0)
        def _():
            wait_windows(slot)

        lo = k * MOE_WIN
        pt = (rep == tgt_ref[...] + lo.astype(F32)).astype(BF16)
        comp = lax.dot_general(pt, xn, (((0,), (0,)), ((), ())), preferred_element_type=F32)
        stage[slot] = comp.astype(BF16)
        for e in range(ne):
            s = start_ref[j * ne + e]
            head = s & (MOE_ALIGN - 1)
            r0 = e * MOE_WIN

            @pl.when(k == 0)
            def _():
                fresh = stage[slot, r0:r0 + MOE_ALIGN, :]
                kept = carry[e * MOE_ALIGN:(e + 1) * MOE_ALIGN, :]
                stage[slot, r0:r0 + MOE_ALIGN, :] = jnp.where(row < head, kept, fresh)

            nxt = (head + start_ref[(j + 1) * ne + e] - s) & (-MOE_ALIGN)

            @pl.when((nxt >= lo) & (nxt < lo + MOE_WIN))
            def _():
                off = pl.multiple_of(nxt - lo, MOE_ALIGN)
                carry[e * MOE_ALIGN:(e + 1) * MOE_ALIGN, :] = stage[slot, pl.ds(r0 + off, MOE_ALIGN), :]

        @pl.when((k == 0) & (j > 0))
        def _():
            wait_windows(1 - slot)

        for e in range(ne):
            base = pl.multiple_of((start_ref[j * ne + e] & (-MOE_ALIGN)) + lo, MOE_ALIGN)
            window_copy(slot, e, base).start()
        return 0

    lax.fori_loop(0, kmax_ref[j], block, 0)

    @pl.when(j == nt - 1)
    def _():
        wait_windows(slot)


def _dispatch(xn, val, starts, kmax, tgt, eexp, cap, tile):
    n = xn.shape[0]
    nt = n // tile
    rows = N_EXPERTS * MOE_WIN
    return pl.pallas_call(
        _dispatch_kernel,
        grid_spec=pltpu.PrefetchScalarGridSpec(
            num_scalar_prefetch=2, grid=(nt,),
            in_specs=[pl.BlockSpec((tile, D_MODEL), lambda j, s, k: (j, 0)),
                      pl.BlockSpec((N_EXPERTS, tile), lambda j, s, k: (0, j)),
                      pl.BlockSpec((None, 1, rows), lambda j, s, k: (j, 0, 0)),
                      pl.BlockSpec((N_EXPERTS, rows), lambda j, s, k: (0, 0))],
            out_specs=pl.BlockSpec(memory_space=pl.ANY),
            scratch_shapes=[pltpu.VMEM((2, rows, D_MODEL), BF16),
                            pltpu.VMEM((N_EXPERTS * MOE_ALIGN, D_MODEL), BF16),
                            pltpu.SemaphoreType.DMA((2,))]),
        out_shape=jax.ShapeDtypeStruct((N_EXPERTS, cap + MOE_PAD, D_MODEL), BF16),
        compiler_params=_params(("arbitrary",)),
        name="moe_dispatch",
    )(starts, kmax, xn, val, tgt, eexp)


def _expert_kernel(x_ref, wr_ref, wg_ref, wu_ref, wd_ref, y_ref, *, ntile):
    e = pl.program_id(0)
    i = pl.program_id(1)

    @pl.when(i < ntile)
    def _():
        x = x_ref[...]
        logits = (jnp.dot(x, wr_ref[0], preferred_element_type=F32)
                  + jnp.dot(x, wr_ref[1], preferred_element_type=F32))
        lane = lax.broadcasted_iota(jnp.int32, logits.shape, 1)
        logits = jnp.where(lane < N_EXPERTS, logits, -jnp.inf)
        ex = jnp.exp(logits - jnp.max(logits, axis=-1, keepdims=True))
        gate = (jnp.sum(jnp.where(lane == e, ex, 0.0), axis=-1, keepdims=True)
                / jnp.sum(ex, axis=-1, keepdims=True))
        hid = []
        for c0 in range(0, EXPERT_FF, FF_CHUNK):
            hg = jnp.dot(x, wg_ref[:, c0:c0 + FF_CHUNK], preferred_element_type=F32)
            hu = jnp.dot(x, wu_ref[:, c0:c0 + FF_CHUNK], preferred_element_type=F32)
            hid.append((_silu(hg) * hu).astype(BF16))
        hid = jnp.concatenate(hid, axis=1)
        y_ref[...] = (jnp.dot(hid, wd_ref[...], preferred_element_type=F32) * gate).astype(BF16)

    @pl.when(i >= ntile)
    def _():
        y_ref[...] = jnp.zeros_like(y_ref)


def _expert_ffn(xe, wr2, wg, wu, wd, cap, tc):
    ne, rows, _ = xe.shape
    wspec = lambda shape: pl.BlockSpec((None,) + shape, lambda e, i: (e, 0, 0))
    return pl.pallas_call(
        functools.partial(_expert_kernel, ntile=cap // tc),
        grid=(ne, rows // tc),
        in_specs=[pl.BlockSpec((None, tc, D_MODEL), lambda e, i: (e, i, 0)),
                  pl.BlockSpec(wr2.shape, lambda e, i: (0, 0, 0)),
                  wspec((D_MODEL, EXPERT_FF)), wspec((D_MODEL, EXPERT_FF)), wspec((EXPERT_FF, D_MODEL))],
        out_specs=pl.BlockSpec((None, tc, D_MODEL), lambda e, i: (e, i, 0)),
        out_shape=jax.ShapeDtypeStruct((ne, rows, D_MODEL), BF16),
        compiler_params=_params(("parallel", "parallel")),
        name="expert_ffn",
    )(xe, wr2, wg, wu, wd)


def _combine_kernel(start_ref, kmax_ref, h_ref, p_ref, val_ref, tgt_ref, eexp_ref, gain_ref, wg_ref, wp_ref,
                    ye_ref, out_ref, stage, sem):
    j = pl.program_id(0)
    nt = pl.num_programs(0)
    slot = lax.rem(j, 2)
    ne = N_EXPERTS

    def window_copy(sl, e, row0):
        return pltpu.make_async_copy(ye_ref.at[e, pl.ds(row0, MOE_WIN)],
                                     stage.at[sl, pl.ds(e * MOE_WIN, MOE_WIN)], sem.at[sl])

    def fetch(sl, tile_idx, lo):
        for e in range(ne):
            base = pl.multiple_of((start_ref[tile_idx * ne + e] & (-MOE_ALIGN)) + lo, MOE_ALIGN)
            window_copy(sl, e, base).start()

    def wait_windows(sl):
        for e in range(ne):
            window_copy(sl, e, 0).wait()

    @pl.when(j == 0)
    def _():
        fetch(slot, j, 0)

    @pl.when(j + 1 < nt)
    def _():
        fetch(1 - slot, j + 1, 0)

    rep = _slot_onehot(val_ref, eexp_ref)
    wait_windows(slot)
    pt = (rep == tgt_ref[...]).astype(BF16)
    acc = h_ref[...] + jnp.dot(pt, stage[slot], preferred_element_type=F32)

    def extra(k, acc):
        lo = k * MOE_WIN
        fetch(slot, j, lo)
        wait_windows(slot)
        pk = (rep == tgt_ref[...] + lo.astype(F32)).astype(BF16)
        return acc + jnp.dot(pk, stage[slot], preferred_element_type=F32)

    x = lax.fori_loop(1, kmax_ref[j], extra, acc)
    ms = jnp.mean(x * x, axis=-1, keepdims=True)
    xn = (x * lax.rsqrt(ms + NORM_EPS) * gain_ref[...]).astype(BF16)
    gate = _sigmoid(jnp.dot(xn, wg_ref[...], preferred_element_type=F32))
    proj = jnp.dot(p_ref[...].astype(BF16), wp_ref[...], preferred_element_type=F32)
    out_ref[...] = x + gate * proj


def _combine_ple(h2, p2, ye, val, starts, kmax, tgt, eexp, gain, wg, wp, tile):
    n = h2.shape[0]
    nt = n // tile
    rows = N_EXPERTS * MOE_WIN
    fixed = lambda j, s, k: (0, 0)
    return pl.pallas_call(
        _combine_kernel,
        grid_spec=pltpu.PrefetchScalarGridSpec(
            num_scalar_prefetch=2, grid=(nt,),
            in_specs=[pl.BlockSpec((tile, D_MODEL), lambda j, s, k: (j, 0)),
                      pl.BlockSpec((tile, PLE_DIM), lambda j, s, k: (j, 0)),
                      pl.BlockSpec((N_EXPERTS, tile), lambda j, s, k: (0, j)),
                      pl.BlockSpec((None, 1, rows), lambda j, s, k: (j, 0, 0)),
                      pl.BlockSpec((N_EXPERTS, rows), fixed),
                      pl.BlockSpec((1, D_MODEL), fixed), pl.BlockSpec(wg.shape, fixed),
                      pl.BlockSpec(wp.shape, fixed),
                      pl.BlockSpec(memory_space=pl.ANY)],
            out_specs=pl.BlockSpec((tile, D_MODEL), lambda j, s, k: (j, 0)),
            scratch_shapes=[pltpu.VMEM((2, rows, D_MODEL), BF16), pltpu.SemaphoreType.DMA((2,))]),
        out_shape=jax.ShapeDtypeStruct((n, D_MODEL), F32),
        compiler_params=_params(("arbitrary",)),
        name="moe_combine_ple",
    )(starts, kmax, h2, p2, val, tgt, eexp, gain, wg, wp, ye)


def _in_perm():
    o_dnz = ZA + ZB
    o_beta = o_dnz + DN_WIDTH
    o_alpha = o_beta + 2 * DN_HEADS
    o_glu = o_alpha + 2 * DN_HEADS
    cols = list(range(0, o_beta)) + list(range(o_glu, o_glu + 2 * CONV_CH))
    for d in range(2):
        cols += [o_beta + d * DN_HEADS + h for h in range(DN_HEADS)]
        cols += [o_alpha + d * DN_HEADS + h for h in range(DN_HEADS)]
    return np.asarray(cols, dtype=np.int32)


def _prep_layer(lw):
    (norm_mix, w_in, q_gain, k_gain, sink, dn_conv, dn_a_log, dn_dt_bias, dn_out_gain,
     cv_dw, cv_dw_bias, cv_ln_gain, cv_ln_bias, w_out, norm_ffn, w_router, w_gate, w_up, w_down,
     norm_ple, w_ple_gate, w_ple_proj) = lw
    perm = _in_perm()
    w_perm = jnp.pad(w_in[:, perm], ((0, 0), (0, ZW - perm.shape[0]))).astype(BF16)
    hgain = jnp.concatenate([jnp.tile(q_gain, ATT_HEADS) * (ATT_HEAD_DIM ** -0.5),
                             jnp.tile(k_gain, ATT_KV_HEADS)]).reshape(1, -1)
    zeros4 = jnp.zeros((DN_HEADS,), F32)
    aneg = -jnp.exp(dn_a_log.astype(F32))
    aneg_row = jnp.concatenate([zeros4, aneg[0], zeros4, aneg[1]])
    dtb_row = jnp.concatenate([zeros4, dn_dt_bias[0], zeros4, dn_dt_bias[1]])
    pad = lambda r: jnp.pad(r, (0, LANES - r.shape[0])).reshape(1, LANES)
    wr = jnp.pad(w_router.astype(F32), ((0, 0), (0, LANES - N_EXPERTS)))
    wr_hi = wr.astype(BF16)
    wr2 = jnp.stack([wr_hi, (wr - wr_hi.astype(F32)).astype(BF16)])
    return dict(
        w_router2=wr2,
        norm_mix=norm_mix.reshape(1, -1), w_in=w_perm, hgain=hgain, sink=sink.astype(F32),
        dn_conv=dn_conv, aneg=pad(aneg_row), dtb=pad(dtb_row),
        dn_out_gain=jnp.tile(dn_out_gain, DN_HEADS).reshape(1, -1),
        cv_dw=cv_dw, cv_dw_bias=cv_dw_bias.reshape(1, -1), cv_ln_gain=cv_ln_gain.reshape(1, -1),
        cv_ln_bias=cv_ln_bias.reshape(1, -1), w_out=w_out.astype(BF16),
        norm_ffn=norm_ffn.reshape(1, -1),
        w_gate=w_gate.astype(BF16), w_up=w_up.astype(BF16), w_down=w_down.astype(BF16),
        norm_ple=norm_ple.reshape(1, -1), w_ple_gate=w_ple_gate.astype(BF16), w_ple_proj=w_ple_proj.astype(BF16))


def _tiles(bsz, seqlen):
    n = bsz * seqlen
    return dict(tm=min(512, n), tl=min(256, seqlen), ch=min(256, seqlen), tcv=min(512, seqlen))


def _moe_ple(h2, p2, pw, t):
    n = h2.shape[0]
    cap = CAPACITY_FACTOR * n // N_EXPERTS
    tile = min(MOE_TILE, n)
    xn, aff_t = _route(h2, pw["norm_ffn"], pw["w_router2"], t["tm"])
    val, cnt = _select(aff_t, cap, tile)
    starts, kmax, tgt = _moe_plan(cnt, n // tile)
    eexp = _expand_matrix()
    xe = _dispatch(xn, val, starts, kmax, tgt, eexp, cap, tile)
    ye = _expert_ffn(xe, pw["w_router2"], pw["w_gate"], pw["w_up"], pw["w_down"], cap, min(MOE_FFN_TILE, cap))
    return _combine_ple(h2, p2, ye, val, starts, kmax, tgt, eexp, pw["norm_ple"], pw["w_ple_gate"],
                        pw["w_ple_proj"], tile)


def _layer(h2, p2, pw, bsz, seqlen):
    t = _tiles(bsz, seqlen)
    hm_att = _head_mean_matrix(ATT_Q + ATT_KV, ATT_HEAD_DIM)
    hs_dn = _head_sum_matrix(2 * DN_WIDTH, DN_HEAD_DIM)
    hm_dn = _head_mean_matrix(DN_WIDTH, DN_HEAD_DIM)
    za, zb, zg, glu_in, gates = _in_proj(h2, pw["norm_mix"], pw["w_in"], hm_att, pw["hgain"], t["tm"])
    o_a = _attention(za, pw["sink"], bsz, seqlen)
    y, gb = _dn_prep(zb, gates, pw["dn_conv"], hs_dn, pw["aneg"], pw["dtb"], bsz, seqlen, t["tl"])
    o_f, o_b = _dn_chunk(y, gb, bsz, seqlen, t["ch"])
    o_c = _conformer_conv(glu_in, pw["cv_dw"], pw["cv_dw_bias"], pw["cv_ln_gain"], pw["cv_ln_bias"],
                          bsz, seqlen, t["tcv"])
    h2 = _out_proj(h2, o_a, o_f, o_b, zg, o_c, pw["dn_out_gain"], hm_dn, pw["w_out"], t["tm"])
    return _moe_ple(h2, p2, pw, t)


def _trunk(x, p, layer_weights):
    bsz, seqlen, _ = x.shape
    h2 = x.reshape(bsz * seqlen, D_MODEL)
    for i, pw in enumerate(layer_weights):
        h2 = _layer(h2, p[i].reshape(bsz * seqlen, PLE_DIM), pw, bsz, seqlen)
    return h2.reshape(bsz, seqlen, D_MODEL)


def kernel(x_prompt, x_sample, p_prompt, p_sample, norm_mix, w_in, q_gain, k_gain, sink, dn_conv, dn_a_log,
           dn_dt_bias, dn_out_gain, cv_dw, cv_dw_bias, cv_ln_gain, cv_ln_bias, w_out, norm_ffn, w_router,
           w_gate, w_up, w_down, norm_ple, w_ple_gate, w_ple_proj):
    weights = (norm_mix, w_in, q_gain, k_gain, sink, dn_conv, dn_a_log, dn_dt_bias, dn_out_gain,
               cv_dw, cv_dw_bias, cv_ln_gain, cv_ln_bias, w_out, norm_ffn, w_router, w_gate, w_up, w_down,
               norm_ple, w_ple_gate, w_ple_proj)
    depth = w_in.shape[0]
    layer_weights = [_prep_layer([w[i] for w in weights]) for i in range(depth)]
    return (_trunk(x_prompt, p_prompt, layer_weights), _trunk(x_sample, p_sample, layer_weights))
```

```python
import functools

import numpy as np
import jax
import jax.numpy as jnp
from jax import lax
from jax.experimental import pallas as pl
from jax.experimental.pallas import tpu as pltpu

F32 = jnp.float32
BF16 = jnp.bfloat16

D_MODEL = 1024
ATT_HEADS = 8
ATT_KV_HEADS = 2
ATT_HEAD_DIM = 64
ATT_GROUP = ATT_HEADS // ATT_KV_HEADS
WINDOW = 128
ATT_BLOCK = 128
DN_HEADS = 4
DN_HEAD_DIM = 64
DN_WIDTH = DN_HEADS * DN_HEAD_DIM
DN_CHUNK = 64
CONV_CH = 256
CONV_WIDTH = 31
ATT_Q = ATT_HEADS * ATT_HEAD_DIM
ATT_KV = ATT_KV_HEADS * ATT_HEAD_DIM
N_EXPERTS = 16
CAPACITY_FACTOR = 2
EXPERT_FF = 1024
PLE_DIM = 256
NORM_EPS = 1e-6

LANES = 128
SUBLANES = 8
VMEM_LIMIT = 48 * 1024 * 1024

ZA = ATT_Q + 2 * ATT_KV
ZB = 3 * DN_WIDTH
ZW = ZA + ZB + DN_WIDTH + 2 * CONV_CH + LANES


def _params(sem):
    return pltpu.CompilerParams(dimension_semantics=sem, vmem_limit_bytes=VMEM_LIMIT)


def _head_mean_matrix(width, head):
    idx = np.arange(width) // head
    return jnp.asarray((idx[:, None] == idx[None, :]).astype(np.float32) / head, dtype=BF16)


def _head_sum_matrix(width, head):
    idx = np.arange(width) // head
    return jnp.asarray((idx[:, None] == idx[None, :]).astype(np.float32), dtype=BF16)


def _sigmoid(x):
    return 1.0 / (1.0 + jnp.exp(-x))


def _silu(x):
    return x * _sigmoid(x)


def _in_proj_kernel(x_ref, gain_ref, w_ref, hm_ref, hgain_ref, za_ref, zb_ref, zg_ref, glu_ref, gates_ref):
    x = x_ref[...]
    ms = jnp.mean(x * x, axis=-1, keepdims=True)
    a = (x * lax.rsqrt(ms + NORM_EPS) * gain_ref[...]).astype(BF16)
    z = jnp.dot(a, w_ref[...], preferred_element_type=F32)
    nqk = ATT_Q + ATT_KV
    qk = z[:, :nqk]
    hms = jnp.dot((qk * qk).astype(BF16), hm_ref[...], preferred_element_type=F32)
    za_ref[:, :nqk] = (qk * lax.rsqrt(hms + NORM_EPS) * hgain_ref[...]).astype(BF16)
    za_ref[:, nqk:] = z[:, nqk:ZA].astype(BF16)
    zb_ref[...] = z[:, ZA:ZA + ZB]
    zg_ref[...] = z[:, ZA + ZB:ZA + ZB + DN_WIDTH]
    glu_ref[...] = z[:, ZA + ZB + DN_WIDTH:ZA + ZB + DN_WIDTH + 2 * CONV_CH]
    gates_ref[...] = z[:, ZW - LANES:]


def _in_proj(h2, gain, w_perm, hm, hgain, tm):
    n = h2.shape[0]
    row = lambda i: (i, 0)
    fixed = lambda i: (0, 0)
    return pl.pallas_call(
        _in_proj_kernel,
        grid=(n // tm,),
        in_specs=[pl.BlockSpec((tm, D_MODEL), row), pl.BlockSpec((1, D_MODEL), fixed),
                  pl.BlockSpec((D_MODEL, ZW), fixed), pl.BlockSpec(hm.shape, fixed),
                  pl.BlockSpec(hgain.shape, fixed)],
        out_specs=[pl.BlockSpec((tm, ZA), row), pl.BlockSpec((tm, ZB), row), pl.BlockSpec((tm, DN_WIDTH), row),
                   pl.BlockSpec((tm, 2 * CONV_CH), row), pl.BlockSpec((tm, LANES), row)],
        out_shape=[jax.ShapeDtypeStruct((n, ZA), BF16), jax.ShapeDtypeStruct((n, ZB), F32),
                   jax.ShapeDtypeStruct((n, DN_WIDTH), F32), jax.ShapeDtypeStruct((n, 2 * CONV_CH), F32),
                   jax.ShapeDtypeStruct((n, LANES), F32)],
        compiler_params=_params(("parallel",)),
        name="in_proj",
    )(h2, gain, w_perm, hm, hgain)


ATT_MASKED = -1e30


def _attn_bias_table():
    i = np.arange(ATT_BLOCK)[:, None]
    c = np.arange(3 * ATT_BLOCK)[None, :]
    rel = c - ATT_BLOCK - i
    slopes = 2.0 ** (-8.0 * np.arange(1, ATT_HEADS + 1) / ATT_HEADS)
    table = np.empty((3, ATT_KV_HEADS, ATT_GROUP * ATT_BLOCK, 3 * ATT_BLOCK), np.float32)
    for variant in range(3):
        ok = np.abs(rel) <= WINDOW
        if variant == 0:
            ok = ok & (c >= ATT_BLOCK)
        if variant == 2:
            ok = ok & (c < 2 * ATT_BLOCK)
        for hd in range(ATT_HEADS):
            g, j = divmod(hd, ATT_GROUP)
            table[variant, g, j * ATT_BLOCK:(j + 1) * ATT_BLOCK] = np.where(ok, -slopes[hd] * np.abs(rel), ATT_MASKED)
    return jnp.asarray(table)


def _attn_kernel(sink_ref, q_ref, kvp_ref, kvo_ref, kvn_ref, bias_ref, o_ref):
    kv = jnp.concatenate([kvp_ref[...], kvo_ref[...], kvn_ref[...]], axis=0)
    hd_ = ATT_HEAD_DIM
    groups = range(ATT_KV_HEADS)
    heads = range(ATT_HEADS)
    ks = [kv[:, g * hd_:(g + 1) * hd_].astype(BF16) for g in groups]
    vs = [kv[:, ATT_KV + g * hd_:ATT_KV + (g + 1) * hd_].astype(BF16) for g in groups]
    qs = [jnp.concatenate([q_ref[:, (g * ATT_GROUP + j) * hd_:(g * ATT_GROUP + j + 1) * hd_]
                           for j in range(ATT_GROUP)], axis=0).astype(BF16) for g in groups]
    sg = [lax.dot_general(qs[g], ks[g], (((1,), (1,)), ((), ())), preferred_element_type=F32) + bias_ref[g]
          for g in groups]
    rows = lambda t, hd: t[(hd % ATT_GROUP) * ATT_BLOCK:(hd % ATT_GROUP + 1) * ATT_BLOCK]
    s = [rows(sg[hd // ATT_GROUP], hd) for hd in heads]
    m = [jnp.maximum(jnp.max(s[hd], axis=-1, keepdims=True), sink_ref[hd]) for hd in heads]
    e = [jnp.exp(s[hd] - m[hd]) for hd in heads]
    den = [jnp.sum(e[hd], axis=-1, keepdims=True) + jnp.exp(sink_ref[hd] - m[hd]) for hd in heads]
    eg = [jnp.concatenate([e[g * ATT_GROUP + j].astype(BF16) for j in range(ATT_GROUP)], axis=0) for g in groups]
    og = [jnp.dot(eg[g], vs[g], preferred_element_type=F32) for g in groups]
    for hd in heads:
        o_ref[:, hd * hd_:(hd + 1) * hd_] = (rows(og[hd // ATT_GROUP], hd) / den[hd]).astype(BF16)


def _attention(za, sink, bsz, seqlen):
    nb = seqlen // ATT_BLOCK
    assert nb >= 2
    za3 = za.reshape(bsz, seqlen, ZA)
    kvw = 2 * ATT_KV
    kvc = ATT_Q // kvw
    bias = _attn_bias_table()
    return pl.pallas_call(
        _attn_kernel,
        grid=(bsz, nb),
        in_specs=[pl.BlockSpec(memory_space=pltpu.SMEM),
                  pl.BlockSpec((None, ATT_BLOCK, ATT_Q), lambda b, n: (b, n, 0)),
                  pl.BlockSpec((None, ATT_BLOCK, kvw), lambda b, n: (b, jnp.maximum(n - 1, 0), kvc)),
                  pl.BlockSpec((None, ATT_BLOCK, kvw), lambda b, n: (b, n, kvc)),
                  pl.BlockSpec((None, ATT_BLOCK, kvw), lambda b, n: (b, jnp.minimum(n + 1, nb - 1), kvc)),
                  pl.BlockSpec((None,) + bias.shape[1:],
                               lambda b, n: (jnp.where(n == 0, 0, jnp.where(n == nb - 1, 2, 1)), 0, 0, 0))],
        out_specs=pl.BlockSpec((None, ATT_BLOCK, ATT_Q), lambda b, n: (b, n, 0)),
        out_shape=jax.ShapeDtypeStruct((bsz, seqlen, ATT_Q), BF16),
        compiler_params=_params(("parallel", "parallel")),
        name="window_attention",
    )(sink, za3, za3, za3, za3, bias).reshape(bsz * seqlen, ATT_Q)


DN_HALO = SUBLANES


def _dn_prep_kernel(x_ref, xp_ref, xn_ref, cw_ref, hs_ref, g_ref, aneg_ref, dtb_ref, mf_ref, mb_ref,
                    y_ref, gb_ref, buf_ref, *, tl):
    i = pl.program_id(1)
    nt = pl.num_programs(1)
    buf_ref[0:DN_HALO, :] = jnp.where(i > 0, xp_ref[...], 0.0)
    buf_ref[DN_HALO:DN_HALO + tl, :] = x_ref[...]
    buf_ref[DN_HALO + tl:, :] = jnp.where(i < nt - 1, xn_ref[...], 0.0)
    y = (cw_ref[0:1, :] * buf_ref[DN_HALO - 1:DN_HALO - 1 + tl, :]
         + cw_ref[1:2, :] * buf_ref[DN_HALO:DN_HALO + tl, :]
         + cw_ref[2:3, :] * buf_ref[DN_HALO + 1:DN_HALO + 1 + tl, :])
    y = _silu(y)
    qk = y[:, :2 * DN_WIDTH]
    ss = jnp.dot((qk * qk).astype(BF16), hs_ref[...], preferred_element_type=F32)
    lane = lax.broadcasted_iota(jnp.int32, (tl, 2 * DN_WIDTH), 1)
    scale = jnp.where(lane < DN_WIDTH, DN_HEAD_DIM ** -0.5, 1.0)
    y_ref[:, :2 * DN_WIDTH] = qk * lax.rsqrt(ss + NORM_EPS) * scale
    y_ref[:, 2 * DN_WIDTH:] = y[:, 2 * DN_WIDTH:]
    raw = g_ref[...]
    col = lax.broadcasted_iota(jnp.int32, (tl, LANES), 1)
    is_beta = (col & DN_HEADS) == 0
    t = raw + dtb_ref[...]
    softplus = jnp.maximum(t, 0.0) + jnp.log(1.0 + jnp.exp(-jnp.abs(t)))
    vals = jnp.where(is_beta, _sigmoid(raw), aneg_ref[...] * softplus)
    v_hi = vals.astype(BF16)
    r1 = vals - v_hi.astype(F32)
    v_mid = r1.astype(BF16)
    v_lo = (r1 - v_mid.astype(F32)).astype(BF16)
    terms = jnp.concatenate([v_hi, v_mid, v_lo], axis=1)
    cf3 = jnp.dot(mf_ref[...], terms, preferred_element_type=F32)
    cb3 = jnp.dot(mb_ref[...], terms, preferred_element_type=F32)
    cf = cf3[:, :LANES] + (cf3[:, LANES:2 * LANES] + cf3[:, 2 * LANES:])
    cb = cb3[:, :LANES] + (cb3[:, LANES:2 * LANES] + cb3[:, 2 * LANES:])
    gb_ref[0] = jnp.where(is_beta, vals, cf)
    gb_ref[1] = pltpu.roll(jnp.where(is_beta, vals, cb), LANES - 2 * DN_HEADS, axis=1)


def _dn_prep(zb, gates, conv_w, hs, aneg, dtb, bsz, seqlen, tl):
    zb3 = zb.reshape(bsz, seqlen, ZB)
    g3 = gates.reshape(bsz, seqlen, LANES)
    nt = seqlen // tl
    hb = tl // DN_HALO
    ch = np.arange(tl) // DN_CHUNK
    same = ch[:, None] == ch[None, :]
    pos = np.arange(tl)
    mf = jnp.asarray((same & (pos[None, :] <= pos[:, None])).astype(np.float32), dtype=BF16)
    mb = jnp.asarray((same & (pos[None, :] >= pos[:, None])).astype(np.float32), dtype=BF16)
    fixed = lambda b, i: (0, 0)
    y, gb = pl.pallas_call(
        functools.partial(_dn_prep_kernel, tl=tl),
        grid=(bsz, nt),
        in_specs=[pl.BlockSpec((None, tl, ZB), lambda b, i: (b, i, 0)),
                  pl.BlockSpec((None, DN_HALO, ZB), lambda b, i: (b, jnp.maximum(i * hb - 1, 0), 0)),
                  pl.BlockSpec((None, DN_HALO, ZB), lambda b, i: (b, jnp.minimum((i + 1) * hb, nt * hb - 1), 0)),
                  pl.BlockSpec(conv_w.shape, fixed), pl.BlockSpec(hs.shape, fixed),
                  pl.BlockSpec((None, tl, LANES), lambda b, i: (b, i, 0)),
                  pl.BlockSpec((1, LANES), fixed), pl.BlockSpec((1, LANES), fixed),
                  pl.BlockSpec((tl, tl), fixed), pl.BlockSpec((tl, tl), fixed)],
        out_specs=[pl.BlockSpec((None, tl, ZB), lambda b, i: (b, i, 0)),
                   pl.BlockSpec((2, None, tl, LANES), lambda b, i: (0, b, i, 0))],
        out_shape=[jax.ShapeDtypeStruct((bsz, seqlen, ZB), F32),
                   jax.ShapeDtypeStruct((2, bsz, seqlen, LANES), F32)],
        scratch_shapes=[pltpu.VMEM((tl + 2 * DN_HALO, ZB), F32)],
        compiler_params=_params(("parallel", "parallel")),
        name="deltanet_prep",
    )(zb3, zb3, zb3, conv_w, hs, g3, aneg, dtb, mf, mb)
    return y, gb


def _lane_expand(cols, first):
    c = cols.shape[0]
    lane = lax.broadcasted_iota(jnp.int32, (c, LANES), 1)
    halves = []
    for h in range(0, DN_HEADS, 2):
        a = jnp.broadcast_to(cols[:, first + h:first + h + 1], (c, LANES))
        b = jnp.broadcast_to(cols[:, first + h + 1:first + h + 2], (c, LANES))
        halves.append(jnp.where(lane < DN_HEAD_DIM, a, b))
    return jnp.concatenate(halves, axis=1)


def _dn_pair_kernel(xf_ref, xb_ref, gf_ref, gb_ref, of_ref, ob_ref, sf_ref, sb_ref, *, nsub):
    c = DN_CHUNK
    w = DN_WIDTH

    @pl.when(pl.program_id(1) == 0)
    def _():
        sf_ref[...] = jnp.zeros_like(sf_ref)
        sb_ref[...] = jnp.zeros_like(sb_ref)

    r_cat = lax.broadcasted_iota(jnp.int32, (c, w), 0)
    s_cat = lax.broadcasted_iota(jnp.int32, (c, w), 1) & (DN_HEAD_DIM - 1)
    eye_cat = s_cat == r_cat
    rr = lax.broadcasted_iota(jnp.int32, (w, w), 0)
    cc = lax.broadcasted_iota(jnp.int32, (w, w), 1)
    head = (rr >> 6) == (cc >> 6)
    head_b = head.astype(BF16)
    m16 = (s_cat >> 4) == (r_cat >> 4)
    m32 = (s_cat >> 5) == (r_cat >> 5)
    off16 = m32 & jnp.logical_not(m16)
    off32 = jnp.logical_not(m32)
    eye_f = eye_cat.astype(F32)

    def bd(t):
        return jnp.concatenate([t] * DN_HEADS, axis=0) * head_b

    def mm(a, b):
        return jnp.dot(a, b, preferred_element_type=F32)

    chunks = [(0, i * c) for i in range(nsub)] + [(1, (nsub - 1 - i) * c) for i in range(nsub)]
    xrefs = (xf_ref, xb_ref)
    grefs = (gf_ref, gb_ref)
    orefs = (of_ref, ob_ref)
    srefs = (sf_ref, sb_ref)
    incl = (s_cat <= r_cat, s_cat >= r_cat)
    strict = (s_cat < r_cat, s_cat > r_cat)
    last_row = (c - 1, 0)

    pre = []
    for d, st in chunks:
        x = xrefs[d][st:st + c, :]
        q, k, v = x[:, :w], x[:, w:2 * w], x[:, 2 * w:]
        gbt = grefs[d][st:st + c, :]
        beta = _lane_expand(gbt, 0)
        gc = _lane_expand(gbt, DN_HEADS)
        grow = jnp.sum(jnp.where(eye_cat, gc, 0.0), axis=0, keepdims=True)
        decay = jnp.exp(jnp.where(incl[d], gc - grow, -jnp.inf))
        glast = gc[last_row[d]:last_row[d] + 1, :]
        egc = jnp.exp(gc)
        kb = k * beta
        pre.append(dict(d=d, st=st, q=q, k=k, kb=kb, vb=v * beta, decay=decay, glast=glast, egc=egc,
                        kdec=(k * jnp.exp(glast - gc)).astype(BF16)))

    kks = [lax.dot_general(jnp.concatenate([p["kb"], p["q"]], axis=0).astype(BF16), bd(p["k"].astype(BF16)),
                           (((1,), (1,)), ((), ())), preferred_element_type=F32) for p in pre]
    a = [jnp.where(strict[p["d"]], kk[:c] * p["decay"], 0.0) for p, kk in zip(pre, kks)]
    intra = [jnp.where(incl[p["d"]], kk[c:] * p["decay"], 0.0).astype(BF16) for p, kk in zip(pre, kks)]
    xm = [jnp.where(m16, -t, 0.0) for t in a]
    xm_b = [t.astype(BF16) for t in xm]
    x2_b = [mm(t, bd(t)).astype(BF16) for t in xm_b]
    x2_d = [bd(t) for t in x2_b]
    dinv = [eye_f + t for t in xm]
    r2 = [mm(jnp.concatenate([t.astype(BF16), p2], axis=0), s2) for t, p2, s2 in zip(dinv, x2_b, x2_d)]
    dinv = [t + r[:c] for t, r in zip(dinv, r2)]
    x4_b = [r[c:].astype(BF16) for r in r2]
    x4_d = [bd(t) for t in x4_b]
    r4 = [mm(jnp.concatenate([t.astype(BF16), p4], axis=0), s4) for t, p4, s4 in zip(dinv, x4_b, x4_d)]
    dinv = [t + r[:c] for t, r in zip(dinv, r4)]
    x8_d = [bd(r[c:].astype(BF16)) for r in r4]
    dinv = [t + mm(t.astype(BF16), s8) for t, s8 in zip(dinv, x8_d)]
    dinv_b = [t.astype(BF16) for t in dinv]
    n32 = [bd(mm(jnp.where(off16, t, 0.0).astype(BF16), bd(db)).astype(BF16)) for t, db in zip(a, dinv_b)]
    t32 = [t - mm(db, n) for t, db, n in zip(dinv, dinv_b, n32)]
    t32_b = [t.astype(BF16) for t in t32]
    n64 = [bd(mm(jnp.where(off32, t, 0.0).astype(BF16), bd(tb)).astype(BF16)) for t, tb in zip(a, t32_b)]
    t_cat = [(t - mm(tb, n)).astype(BF16) for t, tb, n in zip(t32, t32_b, n64)]
    uw = [mm(tc, jnp.concatenate([bd(p["vb"].astype(BF16)), bd((p["kb"] * p["egc"]).astype(BF16))], axis=1))
          for tc, p in zip(t_cat, pre)]
    uw_b = [t.astype(BF16) for t in uw]
    pn = [lax.dot_general(p["kdec"], t, (((0,), (0,)), ((), ())), preferred_element_type=F32)
          for p, t in zip(pre, uw_b)]
    qo = [mm(it, jnp.concatenate([bd(t[:, :w]), bd(t[:, w:])], axis=1)) for it, t in zip(intra, uw_b)]
    lhs = [jnp.concatenate([n[:, w:].astype(BF16) * head_b, (p["q"] * p["egc"] - o[:, w:]).astype(BF16)], axis=0)
           for n, o, p in zip(pn, qo, pre)]
    for step in range(nsub):
        for d in range(2):
            i = d * nsub + step
            p = pre[i]
            state = srefs[d][...]
            r = jnp.dot(lhs[i], state.astype(BF16), preferred_element_type=F32)
            orefs[d][p["st"]:p["st"] + c, :] = r[w:] + qo[i][:, :w]
            srefs[d][...] = state * jnp.exp(p["glast"]) - r[:w] + jnp.where(head, pn[i][:, :w], 0.0)


def _dn_chunk(y, gb, bsz, seqlen, ch):
    nsub = ch // DN_CHUNK
    nblk = seqlen // ch
    fwd = lambda b, j: (b, j, 0)
    bwd = lambda b, j: (b, nblk - 1 - j, 0)
    o_f, o_b = pl.pallas_call(
        functools.partial(_dn_pair_kernel, nsub=nsub),
        grid=(bsz, nblk),
        in_specs=[pl.BlockSpec((None, ch, ZB), fwd), pl.BlockSpec((None, ch, ZB), bwd),
                  pl.BlockSpec((None, None, ch, LANES), lambda b, j: (0, b, j, 0)),
                  pl.BlockSpec((None, None, ch, LANES), lambda b, j: (1, b, nblk - 1 - j, 0))],
        out_specs=[pl.BlockSpec((None, ch, DN_WIDTH), fwd), pl.BlockSpec((None, ch, DN_WIDTH), bwd)],
        out_shape=[jax.ShapeDtypeStruct((bsz, seqlen, DN_WIDTH), F32)] * 2,
        scratch_shapes=[pltpu.VMEM((DN_WIDTH, DN_WIDTH), F32)] * 2,
        compiler_params=_params(("parallel", "arbitrary")),
        name="deltanet_chunks",
    )(y, y, gb, gb)
    return o_f.reshape(bsz * seqlen, DN_WIDTH), o_b.reshape(bsz * seqlen, DN_WIDTH)


CV_HALO = 2 * SUBLANES
CV_PAD = (CONV_WIDTH - 1) // 2


def _conv_kernel(x_ref, xp_ref, xn_ref, dw_ref, bias_ref, lng_ref, lnb_ref, o_ref, buf_ref, shift_ref, *, tl):
    i = pl.program_id(1)
    nt = pl.num_programs(1)

    def glu(t):
        return t[:, :CONV_CH] * _sigmoid(t[:, CONV_CH:])

    buf_ref[0:CV_HALO, :] = jnp.where(i > 0, glu(xp_ref[...]), 0.0)
    buf_ref[CV_HALO:CV_HALO + tl, :] = glu(x_ref[...])
    buf_ref[CV_HALO + tl:, :] = jnp.where(i < nt - 1, glu(xn_ref[...]), 0.0)
    acc = jnp.zeros((tl, CONV_CH), F32) + bias_ref[...]
    first = CV_HALO - CV_PAD
    span = -(-(first + CONV_WIDTH) // SUBLANES) * SUBLANES - SUBLANES
    for sub in range(SUBLANES):
        shift_ref[...] = buf_ref[sub:sub + tl + span, :]
        for base in range(0, span + 1, SUBLANES):
            j = base + sub - first
            if 0 <= j < CONV_WIDTH:
                acc = acc + dw_ref[j:j + 1, :] * shift_ref[base:base + tl, :]
    mu = jnp.mean(acc, axis=-1, keepdims=True)
    cen = acc - mu
    var = jnp.mean(cen * cen, axis=-1, keepdims=True)
    o_ref[...] = _silu(cen * lax.rsqrt(var + NORM_EPS) * lng_ref[...] + lnb_ref[...]).astype(BF16)


def _conformer_conv(glu_in, dw, bias, lng, lnb, bsz, seqlen, tl):
    x3 = glu_in.reshape(bsz, seqlen, 2 * CONV_CH)
    nt = seqlen // tl
    hb = tl // CV_HALO
    fixed = lambda b, i: (0, 0)
    return pl.pallas_call(
        functools.partial(_conv_kernel, tl=tl),
        grid=(bsz, nt),
        in_specs=[pl.BlockSpec((None, tl, 2 * CONV_CH), lambda b, i: (b, i, 0)),
                  pl.BlockSpec((None, CV_HALO, 2 * CONV_CH), lambda b, i: (b, jnp.maximum(i * hb - 1, 0), 0)),
                  pl.BlockSpec((None, CV_HALO, 2 * CONV_CH),
                               lambda b, i: (b, jnp.minimum((i + 1) * hb, nt * hb - 1), 0)),
                  pl.BlockSpec(dw.shape, fixed), pl.BlockSpec((1, CONV_CH), fixed),
                  pl.BlockSpec((1, CONV_CH), fixed), pl.BlockSpec((1, CONV_CH), fixed)],
        out_specs=pl.BlockSpec((None, tl, CONV_CH), lambda b, i: (b, i, 0)),
        out_shape=jax.ShapeDtypeStruct((bsz, seqlen, CONV_CH), BF16),
        scratch_shapes=[pltpu.VMEM((tl + 2 * CV_HALO, CONV_CH), F32),
                        pltpu.VMEM((tl + 2 * CV_HALO - SUBLANES, CONV_CH), F32)],
        compiler_params=_params(("parallel", "parallel")),
        name="conformer_conv",
    )(x3, x3, x3, dw, bias, lng, lnb).reshape(bsz * seqlen, CONV_CH)


def _out_proj_kernel(h_ref, oa_ref, of_ref, ob_ref, zg_ref, oc_ref, og_ref, hm_ref, w_ref, gain_ref, wr_ref,
                     out_ref, xn_ref, aff_ref):
    ob = of_ref[...] + ob_ref[...]
    ms = jnp.dot((ob * ob).astype(BF16), hm_ref[...], preferred_element_type=F32)
    obn = ob * lax.rsqrt(ms + NORM_EPS) * og_ref[...]
    ob2 = obn * _silu(zg_ref[...])
    mix = jnp.concatenate([oa_ref[...], ob2.astype(BF16), oc_ref[...]], axis=1)
    x = h_ref[...] + jnp.dot(mix, w_ref[...], preferred_element_type=F32)
    out_ref[...] = x
    ms = jnp.mean(x * x, axis=-1, keepdims=True)
    xn = x * lax.rsqrt(ms + NORM_EPS) * gain_ref[...]
    xn_hi = xn.astype(BF16)
    xn_ref[...] = xn_hi
    xn_lo = (xn - xn_hi.astype(F32)).astype(BF16)
    logits = (jnp.dot(xn_hi, wr_ref[0], preferred_element_type=F32)
              + (jnp.dot(xn_lo, wr_ref[0], preferred_element_type=F32)
                 + jnp.dot(xn_hi, wr_ref[1], preferred_element_type=F32)))
    lane = lax.broadcasted_iota(jnp.int32, logits.shape, 1)
    logits = jnp.where(lane < N_EXPERTS, logits, -jnp.inf)
    m = jnp.max(logits, axis=-1, keepdims=True)
    e = jnp.exp(logits - m)
    aff = e / jnp.sum(e, axis=-1, keepdims=True)
    aff_ref[...] = jnp.transpose(aff)[:N_EXPERTS, :]


def _out_proj_route(h2, oa, o_f, o_b, zg, oc, og, hm, w, gain, wr2, tm):
    n = h2.shape[0]
    row = lambda i: (i, 0)
    fixed = lambda i: (0, 0)
    return pl.pallas_call(
        _out_proj_kernel,
        grid=(n // tm,),
        in_specs=[pl.BlockSpec((tm, D_MODEL), row), pl.BlockSpec((tm, ATT_Q), row),
                  pl.BlockSpec((tm, DN_WIDTH), row), pl.BlockSpec((tm, DN_WIDTH), row),
                  pl.BlockSpec((tm, DN_WIDTH), row),
                  pl.BlockSpec((tm, CONV_CH), row), pl.BlockSpec((1, DN_WIDTH), fixed),
                  pl.BlockSpec(hm.shape, fixed), pl.BlockSpec(w.shape, fixed),
                  pl.BlockSpec((1, D_MODEL), fixed), pl.BlockSpec((2, D_MODEL, LANES), lambda i: (0, 0, 0))],
        out_specs=[pl.BlockSpec((tm, D_MODEL), row), pl.BlockSpec((tm, D_MODEL), row),
                   pl.BlockSpec((N_EXPERTS, tm), lambda i: (0, i))],
        out_shape=[jax.ShapeDtypeStruct((n, D_MODEL), F32), jax.ShapeDtypeStruct((n, D_MODEL), BF16),
                   jax.ShapeDtypeStruct((N_EXPERTS, n), F32)],
        compiler_params=_params(("parallel",)),
        name="out_proj_route",
    )(h2, oa, o_f, o_b, zg, oc, og, hm, w, gain, wr2)


MOE_TILE = 256
MOE_ALIGN = 2 * SUBLANES
MOE_WIN = 80
MOE_PAD = 1024
MOE_FFN_TILE = 1024
FF_CHUNK = 256
MOE_UNSELECTED = -64.0


def _select_kernel(aff_ref, tri_ref, val_ref, cnt_ref, *, cap, tile):
    ne, n = aff_ref.shape
    nt = n // tile
    capf = float(cap)

    def bits_of(x):
        return lax.bitcast_convert_type(x, jnp.int32)

    def search(i, thr):
        cand = thr | jnp.left_shift(jnp.int32(1), 30 - i)
        cnt = jnp.sum((bits_of(aff_ref[...]) >= cand).astype(F32), axis=1, keepdims=True)
        return jnp.where(cnt >= capf, cand, thr)

    thr = lax.fori_loop(0, 31, search, jnp.zeros((ne, 1), jnp.int32))
    n_gt = jnp.sum((bits_of(aff_ref[...]) > thr).astype(F32), axis=1, keepdims=True)
    need = capf - n_gt
    lane = lax.broadcasted_iota(jnp.int32, (ne, LANES), 1)

    def tile_body(j, carry):
        eq_before, cnt_acc = carry
        off = pl.multiple_of(j * tile, tile)
        b = bits_of(aff_ref[:, pl.ds(off, tile)])
        gt = b > thr
        eqf = (b == thr).astype(F32)
        eq_rank = eq_before + jnp.dot(eqf.astype(BF16), tri_ref[...], preferred_element_type=F32)
        self_ = jnp.where(gt, 1.0, jnp.where(eq_rank <= need, eqf, 0.0))
        rank = jnp.dot(self_.astype(BF16), tri_ref[...], preferred_element_type=F32)
        val_ref[:, pl.ds(off, tile)] = jnp.where(self_ > 0.0, rank, MOE_UNSELECTED)
        cnt = jnp.sum(self_, axis=1, keepdims=True)
        return (eq_before + jnp.sum(eqf, axis=1, keepdims=True), cnt_acc + jnp.where(lane == j, cnt, 0.0))

    init = (jnp.zeros((ne, 1), F32), jnp.zeros((ne, LANES), F32))
    _, cnt_acc = lax.fori_loop(0, nt, tile_body, init)
    cnt_ref[...] = cnt_acc


def _select(aff_t, cap, tile):
    ne, n = aff_t.shape
    assert n // tile <= LANES
    tri = jnp.asarray(np.triu(np.ones((tile, tile), np.float32)), dtype=BF16)
    return pl.pallas_call(
        functools.partial(_select_kernel, cap=cap, tile=tile),
        out_shape=[jax.ShapeDtypeStruct((ne, n), F32), jax.ShapeDtypeStruct((ne, LANES), F32)],
        compiler_params=pltpu.CompilerParams(vmem_limit_bytes=VMEM_LIMIT),
        name="moe_select",
    )(aff_t, tri)


def _moe_plan(cnt, nt):
    c = cnt[:, :nt].astype(jnp.int32).T
    starts = jnp.concatenate([jnp.zeros((1, N_EXPERTS), jnp.int32), jnp.cumsum(c, axis=0)], axis=0)
    head = starts[:-1] & (MOE_ALIGN - 1)
    kmax = jnp.maximum(jnp.max((head + c + MOE_WIN - 1) // MOE_WIN, axis=1), 1).astype(jnp.int32)
    w = jnp.arange(MOE_WIN, dtype=jnp.int32)
    tgt = (w[None, None, :] + 1 - head[:, :, None]).astype(F32).reshape(nt, 1, N_EXPERTS * MOE_WIN)
    return starts.reshape(-1), kmax, tgt


def _expand_matrix():
    e = np.arange(N_EXPERTS * MOE_WIN) // MOE_WIN
    return jnp.asarray((np.arange(N_EXPERTS)[:, None] == e[None, :]).astype(np.float32), dtype=BF16)


def _slot_onehot(val_ref, eexp_ref):
    return lax.dot_general(val_ref[...].astype(BF16), eexp_ref[...], (((0,), (0,)), ((), ())),
                           preferred_element_type=F32)


def _dispatch_kernel(start_ref, kmax_ref, xn_ref, val_ref, tgt_ref, eexp_ref, xe_ref, stage, carry, sem):
    j = pl.program_id(0)
    nt = pl.num_programs(0)
    slot = lax.rem(j, 2)
    ne = N_EXPERTS

    cap = xe_ref.shape[1] - MOE_PAD

    @pl.when(j == 0)
    def _():
        carry[...] = jnp.zeros_like(carry)
        stage[0, 0:MOE_PAD, :] = jnp.zeros((MOE_PAD, D_MODEL), BF16)
        fills = [pltpu.make_async_copy(stage.at[0, pl.ds(0, MOE_PAD)], xe_ref.at[e, pl.ds(cap, MOE_PAD)], sem.at[0])
                 for e in range(ne)]
        for f in fills:
            f.start()
        for f in fills:
            f.wait()

    def window_copy(sl, e, row0):
        return pltpu.make_async_copy(stage.at[sl, pl.ds(e * MOE_WIN, MOE_WIN)],
                                     xe_ref.at[e, pl.ds(row0, MOE_WIN)], sem.at[sl])

    def wait_windows(sl):
        for e in range(ne):
            window_copy(sl, e, 0).wait()

    rep = _slot_onehot(val_ref, eexp_ref)
    xn = xn_ref[...]
    row = lax.broadcasted_iota(jnp.int32, (MOE_ALIGN, D_MODEL), 0)

    def block(k, _):
        @pl.when(k > 0)
        def _():
            wait_windows(slot)

        lo = k * MOE_WIN
        pt = (rep == tgt_ref[...] + lo.astype(F32)).astype(BF16)
        comp = lax.dot_general(pt, xn, (((0,), (0,)), ((), ())), preferred_element_type=F32)
        stage[slot] = comp.astype(BF16)
        for e in range(ne):
            s = start_ref[j * ne + e]
            head = s & (MOE_ALIGN - 1)
            r0 = e * MOE_WIN

            @pl.when(k == 0)
            def _():
                fresh = stage[slot, r0:r0 + MOE_ALIGN, :]
                kept = carry[e * MOE_ALIGN:(e + 1) * MOE_ALIGN, :]
                stage[slot, r0:r0 + MOE_ALIGN, :] = jnp.where(row < head, kept, fresh)

            nxt = (head + start_ref[(j + 1) * ne + e] - s) & (-MOE_ALIGN)

            @pl.when((nxt >= lo) & (nxt < lo + MOE_WIN))
            def _():
                off = pl.multiple_of(nxt - lo, MOE_ALIGN)
                carry[e * MOE_ALIGN:(e + 1) * MOE_ALIGN, :] = stage[slot, pl.ds(r0 + off, MOE_ALIGN), :]

        @pl.when((k == 0) & (j > 0))
        def _():
            wait_windows(1 - slot)

        for e in range(ne):
            base = pl.multiple_of((start_ref[j * ne + e] & (-MOE_ALIGN)) + lo, MOE_ALIGN)
            window_copy(slot, e, base).start()
        return 0

    lax.fori_loop(0, kmax_ref[j], block, 0)

    @pl.when(j == nt - 1)
    def _():
        wait_windows(slot)


def _dispatch(xn, val, starts, kmax, tgt, eexp, cap, tile):
    n = xn.shape[0]
    nt = n // tile
    rows = N_EXPERTS * MOE_WIN
    return pl.pallas_call(
        _dispatch_kernel,
        grid_spec=pltpu.PrefetchScalarGridSpec(
            num_scalar_prefetch=2, grid=(nt,),
            in_specs=[pl.BlockSpec((tile, D_MODEL), lambda j, s, k: (j, 0)),
                      pl.BlockSpec((N_EXPERTS, tile), lambda j, s, k: (0, j)),
                      pl.BlockSpec((None, 1, rows), lambda j, s, k: (j, 0, 0)),
                      pl.BlockSpec((N_EXPERTS, rows), lambda j, s, k: (0, 0))],
            out_specs=pl.BlockSpec(memory_space=pl.ANY),
            scratch_shapes=[pltpu.VMEM((2, rows, D_MODEL), BF16),
                            pltpu.VMEM((N_EXPERTS * MOE_ALIGN, D_MODEL), BF16),
                            pltpu.SemaphoreType.DMA((2,))]),
        out_shape=jax.ShapeDtypeStruct((N_EXPERTS, cap + MOE_PAD, D_MODEL), BF16),
        compiler_params=_params(("arbitrary",)),
        name="moe_dispatch",
    )(starts, kmax, xn, val, tgt, eexp)


def _expert_kernel(x_ref, wr_ref, wg_ref, wu_ref, wd_ref, y_ref, *, ntile):
    e = pl.program_id(0)
    i = pl.program_id(1)

    @pl.when(i < ntile)
    def _():
        x = x_ref[...]
        logits = (jnp.dot(x, wr_ref[0], preferred_element_type=F32)
                  + jnp.dot(x, wr_ref[1], preferred_element_type=F32))
        lane = lax.broadcasted_iota(jnp.int32, logits.shape, 1)
        logits = jnp.where(lane < N_EXPERTS, logits, -jnp.inf)
        ex = jnp.exp(logits - jnp.max(logits, axis=-1, keepdims=True))
        gate = (jnp.sum(jnp.where(lane == e, ex, 0.0), axis=-1, keepdims=True)
                / jnp.sum(ex, axis=-1, keepdims=True))
        hid = []
        for c0 in range(0, EXPERT_FF, FF_CHUNK):
            hg = jnp.dot(x, wg_ref[:, c0:c0 + FF_CHUNK], preferred_element_type=F32)
            hu = jnp.dot(x, wu_ref[:, c0:c0 + FF_CHUNK], preferred_element_type=F32)
            hid.append((_silu(hg) * hu).astype(BF16))
        hid = jnp.concatenate(hid, axis=1)
        y_ref[...] = (jnp.dot(hid, wd_ref[...], preferred_element_type=F32) * gate).astype(BF16)

    @pl.when(i >= ntile)
    def _():
        y_ref[...] = jnp.zeros_like(y_ref)


def _expert_ffn(xe, wr2, wg, wu, wd, cap, tc):
    ne, rows, _ = xe.shape
    wspec = lambda shape: pl.BlockSpec((None,) + shape, lambda e, i: (e, 0, 0))
    return pl.pallas_call(
        functools.partial(_expert_kernel, ntile=cap // tc),
        grid=(ne, rows // tc),
        in_specs=[pl.BlockSpec((None, tc, D_MODEL), lambda e, i: (e, i, 0)),
                  pl.BlockSpec(wr2.shape, lambda e, i: (0, 0, 0)),
                  wspec((D_MODEL, EXPERT_FF)), wspec((D_MODEL, EXPERT_FF)), wspec((EXPERT_FF, D_MODEL))],
        out_specs=pl.BlockSpec((None, tc, D_MODEL), lambda e, i: (e, i, 0)),
        out_shape=jax.ShapeDtypeStruct((ne, rows, D_MODEL), BF16),
        compiler_params=_params(("parallel", "parallel")),
        name="expert_ffn",
    )(xe, wr2, wg, wu, wd)


def _combine_kernel(start_ref, kmax_ref, h_ref, p_ref, val_ref, tgt_ref, eexp_ref, gain_ref, wg_ref, wp_ref,
                    ye_ref, out_ref, stage, sem):
    j = pl.program_id(0)
    nt = pl.num_programs(0)
    slot = lax.rem(j, 2)
    ne = N_EXPERTS

    def window_copy(sl, e, row0):
        return pltpu.make_async_copy(ye_ref.at[e, pl.ds(row0, MOE_WIN)],
                                     stage.at[sl, pl.ds(e * MOE_WIN, MOE_WIN)], sem.at[sl])

    def fetch(sl, tile_idx, lo):
        for e in range(ne):
            base = pl.multiple_of((start_ref[tile_idx * ne + e] & (-MOE_ALIGN)) + lo, MOE_ALIGN)
            window_copy(sl, e, base).start()

    def wait_windows(sl):
        for e in range(ne):
            window_copy(sl, e, 0).wait()

    @pl.when(j == 0)
    def _():
        fetch(slot, j, 0)

    @pl.when(j + 1 < nt)
    def _():
        fetch(1 - slot, j + 1, 0)

    rep = _slot_onehot(val_ref, eexp_ref)
    wait_windows(slot)
    pt = (rep == tgt_ref[...]).astype(BF16)
    acc = h_ref[...] + jnp.dot(pt, stage[slot], preferred_element_type=F32)

    def extra(k, acc):
        lo = k * MOE_WIN
        fetch(slot, j, lo)
        wait_windows(slot)
        pk = (rep == tgt_ref[...] + lo.astype(F32)).astype(BF16)
        return acc + jnp.dot(pk, stage[slot], preferred_element_type=F32)

    x = lax.fori_loop(1, kmax_ref[j], extra, acc)
    ms = jnp.mean(x * x, axis=-1, keepdims=True)
    xn = (x * lax.rsqrt(ms + NORM_EPS) * gain_ref[...]).astype(BF16)
    gate = _sigmoid(jnp.dot(xn, wg_ref[...], preferred_element_type=F32))
    proj = jnp.dot(p_ref[...].astype(BF16), wp_ref[...], preferred_element_type=F32)
    out_ref[...] = x + gate * proj


def _combine_ple(h2, p2, ye, val, starts, kmax, tgt, eexp, gain, wg, wp, tile):
    n = h2.shape[0]
    nt = n // tile
    rows = N_EXPERTS * MOE_WIN
    fixed = lambda j, s, k: (0, 0)
    return pl.pallas_call(
        _combine_kernel,
        grid_spec=pltpu.PrefetchScalarGridSpec(
            num_scalar_prefetch=2, grid=(nt,),
            in_specs=[pl.BlockSpec((tile, D_MODEL), lambda j, s, k: (j, 0)),
                      pl.BlockSpec((tile, PLE_DIM), lambda j, s, k: (j, 0)),
                      pl.BlockSpec((N_EXPERTS, tile), lambda j, s, k: (0, j)),
                      pl.BlockSpec((None, 1, rows), lambda j, s, k: (j, 0, 0)),
                      pl.BlockSpec((N_EXPERTS, rows), fixed),
                      pl.BlockSpec((1, D_MODEL), fixed), pl.BlockSpec(wg.shape, fixed),
                      pl.BlockSpec(wp.shape, fixed),
                      pl.BlockSpec(memory_space=pl.ANY)],
            out_specs=pl.BlockSpec((tile, D_MODEL), lambda j, s, k: (j, 0)),
            scratch_shapes=[pltpu.VMEM((2, rows, D_MODEL), BF16), pltpu.SemaphoreType.DMA((2,))]),
        out_shape=jax.ShapeDtypeStruct((n, D_MODEL), F32),
        compiler_params=_params(("arbitrary",)),
        name="moe_combine_ple",
    )(starts, kmax, h2, p2, val, tgt, eexp, gain, wg, wp, ye)


def _in_perm():
    o_dnz = ZA + ZB
    o_beta = o_dnz + DN_WIDTH
    o_alpha = o_beta + 2 * DN_HEADS
    o_glu = o_alpha + 2 * DN_HEADS
    cols = list(range(0, o_beta)) + list(range(o_glu, o_glu + 2 * CONV_CH))
    for d in range(2):
        cols += [o_beta + d * DN_HEADS + h for h in range(DN_HEADS)]
        cols += [o_alpha + d * DN_HEADS + h for h in range(DN_HEADS)]
    return np.asarray(cols, dtype=np.int32)


def _prep_layer(lw):
    (norm_mix, w_in, q_gain, k_gain, sink, dn_conv, dn_a_log, dn_dt_bias, dn_out_gain,
     cv_dw, cv_dw_bias, cv_ln_gain, cv_ln_bias, w_out, norm_ffn, w_router, w_gate, w_up, w_down,
     norm_ple, w_ple_gate, w_ple_proj) = lw
    perm = _in_perm()
    w_perm = jnp.pad(w_in[:, perm], ((0, 0), (0, ZW - perm.shape[0]))).astype(BF16)
    hgain = jnp.concatenate([jnp.tile(q_gain, ATT_HEADS) * (ATT_HEAD_DIM ** -0.5),
                             jnp.tile(k_gain, ATT_KV_HEADS)]).reshape(1, -1)
    zeros4 = jnp.zeros((DN_HEADS,), F32)
    aneg = -jnp.exp(dn_a_log.astype(F32))
    aneg_row = jnp.concatenate([zeros4, aneg[0], zeros4, aneg[1]])
    dtb_row = jnp.concatenate([zeros4, dn_dt_bias[0], zeros4, dn_dt_bias[1]])
    pad = lambda r: jnp.pad(r, (0, LANES - r.shape[0])).reshape(1, LANES)
    wr = jnp.pad(w_router.astype(F32), ((0, 0), (0, LANES - N_EXPERTS)))
    wr_hi = wr.astype(BF16)
    wr2 = jnp.stack([wr_hi, (wr - wr_hi.astype(F32)).astype(BF16)])
    return dict(
        w_router2=wr2,
        norm_mix=norm_mix.reshape(1, -1), w_in=w_perm, hgain=hgain, sink=sink.astype(F32),
        dn_conv=dn_conv, aneg=pad(aneg_row), dtb=pad(dtb_row),
        dn_out_gain=jnp.tile(dn_out_gain, DN_HEADS).reshape(1, -1),
        cv_dw=cv_dw, cv_dw_bias=cv_dw_bias.reshape(1, -1), cv_ln_gain=cv_ln_gain.reshape(1, -1),
        cv_ln_bias=cv_ln_bias.reshape(1, -1), w_out=w_out.astype(BF16),
        norm_ffn=norm_ffn.reshape(1, -1),
        w_gate=w_gate.astype(BF16), w_up=w_up.astype(BF16), w_down=w_down.astype(BF16),
        norm_ple=norm_ple.reshape(1, -1), w_ple_gate=w_ple_gate.astype(BF16), w_ple_proj=w_ple_proj.astype(BF16))


def _tiles(bsz, seqlen):
    n = bsz * seqlen
    return dict(tm=min(512, n), tl=min(256, seqlen), ch=min(256, seqlen), tcv=min(512, seqlen))


def _moe_ple(h2, xn, aff_t, p2, pw):
    n = h2.shape[0]
    cap = CAPACITY_FACTOR * n // N_EXPERTS
    tile = min(MOE_TILE, n)
    val, cnt = _select(aff_t, cap, tile)
    starts, kmax, tgt = _moe_plan(cnt, n // tile)
    eexp = _expand_matrix()
    xe = _dispatch(xn, val, starts, kmax, tgt, eexp, cap, tile)
    ye = _expert_ffn(xe, pw["w_router2"], pw["w_gate"], pw["w_up"], pw["w_down"], cap, min(MOE_FFN_TILE, cap))
    return _combine_ple(h2, p2, ye, val, starts, kmax, tgt, eexp, pw["norm_ple"], pw["w_ple_gate"],
                        pw["w_ple_proj"], tile)


def _layer(h2, p2, pw, bsz, seqlen):
    t = _tiles(bsz, seqlen)
    hm_att = _head_mean_matrix(ATT_Q + ATT_KV, ATT_HEAD_DIM)
    hs_dn = _head_sum_matrix(2 * DN_WIDTH, DN_HEAD_DIM)
    hm_dn = _head_mean_matrix(DN_WIDTH, DN_HEAD_DIM)
    za, zb, zg, glu_in, gates = _in_proj(h2, pw["norm_mix"], pw["w_in"], hm_att, pw["hgain"], t["tm"])
    o_a = _attention(za, pw["sink"], bsz, seqlen)
    y, gb = _dn_prep(zb, gates, pw["dn_conv"], hs_dn, pw["aneg"], pw["dtb"], bsz, seqlen, t["tl"])
    o_f, o_b = _dn_chunk(y, gb, bsz, seqlen, t["ch"])
    o_c = _conformer_conv(glu_in, pw["cv_dw"], pw["cv_dw_bias"], pw["cv_ln_gain"], pw["cv_ln_bias"],
                          bsz, seqlen, t["tcv"])
    h2, xn, aff_t = _out_proj_route(h2, o_a, o_f, o_b, zg, o_c, pw["dn_out_gain"], hm_dn, pw["w_out"],
                                    pw["norm_ffn"], pw["w_router2"], t["tm"])
    return _moe_ple(h2, xn, aff_t, p2, pw)


def _trunk(x, p, layer_weights):
    bsz, seqlen, _ = x.shape
    h2 = x.reshape(bsz * seqlen, D_MODEL)
    for i, pw in enumerate(layer_weights):
        h2 = _layer(h2, p[i].reshape(bsz * seqlen, PLE_DIM), pw, bsz, seqlen)
    return h2.reshape(bsz, seqlen, D_MODEL)


def kernel(x_prompt, x_sample, p_prompt, p_sample, norm_mix, w_in, q_gain, k_gain, sink, dn_conv, dn_a_log,
           dn_dt_bias, dn_out_gain, cv_dw, cv_dw_bias, cv_ln_gain, cv_ln_bias, w_out, norm_ffn, w_router,
           w_gate, w_up, w_down, norm_ple, w_ple_gate, w_ple_proj):
    weights = (norm_mix, w_in, q_gain, k_gain, sink, dn_conv, dn_a_log, dn_dt_bias, dn_out_gain,
               cv_dw, cv_dw_bias, cv_ln_gain, cv_ln_bias, w_out, norm_ffn, w_router, w_gate, w_up, w_down,
               norm_ple, w_ple_gate, w_ple_proj)
    depth = w_in.shape[0]
    layer_weights = [_prep_layer([w[i] for w in weights]) for i in range(depth)]
    return (_trunk(x_prompt, p_prompt, layer_weights), _trunk(x_sample, p_sample, layer_weights))
```

```python
import functools

import numpy as np
import jax
import jax.numpy as jnp
from jax import lax
from jax.experimental import pallas as pl
from jax.experimental.pallas import tpu as pltpu

F32 = jnp.float32
BF16 = jnp.bfloat16

D_MODEL = 1024
ATT_HEADS = 8
ATT_KV_HEADS = 2
ATT_HEAD_DIM = 64
ATT_GROUP = ATT_HEADS // ATT_KV_HEADS
WINDOW = 128
ATT_BLOCK = 128
DN_HEADS = 4
DN_HEAD_DIM = 64
DN_WIDTH = DN_HEADS * DN_HEAD_DIM
DN_CHUNK = 64
CONV_CH = 256
CONV_WIDTH = 31
ATT_Q = ATT_HEADS * ATT_HEAD_DIM
ATT_KV = ATT_KV_HEADS * ATT_HEAD_DIM
N_EXPERTS = 16
CAPACITY_FACTOR = 2
EXPERT_FF = 1024
PLE_DIM = 256
NORM_EPS = 1e-6

LANES = 128
SUBLANES = 8
VMEM_LIMIT = 48 * 1024 * 1024

ZA = ATT_Q + 2 * ATT_KV
ZB = 3 * DN_WIDTH
ZW = ZA + ZB + DN_WIDTH + 2 * CONV_CH + LANES


def _params(sem):
    return pltpu.CompilerParams(dimension_semantics=sem, vmem_limit_bytes=VMEM_LIMIT)


def _head_mean_matrix(width, head):
    idx = np.arange(width) // head
    return jnp.asarray((idx[:, None] == idx[None, :]).astype(np.float32) / head, dtype=BF16)


def _head_sum_matrix(width, head):
    idx = np.arange(width) // head
    return jnp.asarray((idx[:, None] == idx[None, :]).astype(np.float32), dtype=BF16)


def _sigmoid(x):
    return 1.0 / (1.0 + jnp.exp(-x))


def _silu(x):
    return x * _sigmoid(x)


def _in_proj_kernel(x_ref, gain_ref, w_ref, hm_ref, hgain_ref, za_ref, zb_ref, zg_ref, glu_ref, gates_ref):
    x = x_ref[...]
    ms = jnp.mean(x * x, axis=-1, keepdims=True)
    a = (x * lax.rsqrt(ms + NORM_EPS) * gain_ref[...]).astype(BF16)
    z = jnp.dot(a, w_ref[...], preferred_element_type=F32)
    nqk = ATT_Q + ATT_KV
    qk = z[:, :nqk]
    hms = jnp.dot((qk * qk).astype(BF16), hm_ref[...], preferred_element_type=F32)
    za_ref[:, :nqk] = (qk * lax.rsqrt(hms + NORM_EPS) * hgain_ref[...]).astype(BF16)
    za_ref[:, nqk:] = z[:, nqk:ZA].astype(BF16)
    zb_ref[...] = z[:, ZA:ZA + ZB]
    zg_ref[...] = z[:, ZA + ZB:ZA + ZB + DN_WIDTH]
    glu_ref[...] = z[:, ZA + ZB + DN_WIDTH:ZA + ZB + DN_WIDTH + 2 * CONV_CH]
    gates_ref[...] = z[:, ZW - LANES:]


def _in_proj(h2, gain, w_perm, hm, hgain, tm):
    n = h2.shape[0]
    row = lambda i: (i, 0)
    fixed = lambda i: (0, 0)
    return pl.pallas_call(
        _in_proj_kernel,
        grid=(n // tm,),
        in_specs=[pl.BlockSpec((tm, D_MODEL), row), pl.BlockSpec((1, D_MODEL), fixed),
                  pl.BlockSpec((D_MODEL, ZW), fixed), pl.BlockSpec(hm.shape, fixed),
                  pl.BlockSpec(hgain.shape, fixed)],
        out_specs=[pl.BlockSpec((tm, ZA), row), pl.BlockSpec((tm, ZB), row), pl.BlockSpec((tm, DN_WIDTH), row),
                   pl.BlockSpec((tm, 2 * CONV_CH), row), pl.BlockSpec((tm, LANES), row)],
        out_shape=[jax.ShapeDtypeStruct((n, ZA), BF16), jax.ShapeDtypeStruct((n, ZB), F32),
                   jax.ShapeDtypeStruct((n, DN_WIDTH), F32), jax.ShapeDtypeStruct((n, 2 * CONV_CH), F32),
                   jax.ShapeDtypeStruct((n, LANES), F32)],
        compiler_params=_params(("parallel",)),
        name="in_proj",
    )(h2, gain, w_perm, hm, hgain)


ATT_MASKED = -1e30


def _attn_bias_table():
    i = np.arange(ATT_BLOCK)[:, None]
    c = np.arange(3 * ATT_BLOCK)[None, :]
    rel = c - ATT_BLOCK - i
    slopes = 2.0 ** (-8.0 * np.arange(1, ATT_HEADS + 1) / ATT_HEADS)
    table = np.empty((3, ATT_KV_HEADS, ATT_GROUP * ATT_BLOCK, 3 * ATT_BLOCK), np.float32)
    for variant in range(3):
        ok = np.abs(rel) <= WINDOW
        if variant == 0:
            ok = ok & (c >= ATT_BLOCK)
        if variant == 2:
            ok = ok & (c < 2 * ATT_BLOCK)
        for hd in range(ATT_HEADS):
            g, j = divmod(hd, ATT_GROUP)
            table[variant, g, j * ATT_BLOCK:(j + 1) * ATT_BLOCK] = np.where(ok, -slopes[hd] * np.abs(rel), ATT_MASKED)
    return jnp.asarray(table)


def _attn_kernel(sink_ref, q_ref, kvp_ref, kvo_ref, kvn_ref, bias_ref, o_ref):
    kv = jnp.concatenate([kvp_ref[...], kvo_ref[...], kvn_ref[...]], axis=0)
    hd_ = ATT_HEAD_DIM
    groups = range(ATT_KV_HEADS)
    heads = range(ATT_HEADS)
    ks = [kv[:, g * hd_:(g + 1) * hd_].astype(BF16) for g in groups]
    vs = [kv[:, ATT_KV + g * hd_:ATT_KV + (g + 1) * hd_].astype(BF16) for g in groups]
    qs = [jnp.concatenate([q_ref[:, (g * ATT_GROUP + j) * hd_:(g * ATT_GROUP + j + 1) * hd_]
                           for j in range(ATT_GROUP)], axis=0).astype(BF16) for g in groups]
    sg = [lax.dot_general(qs[g], ks[g], (((1,), (1,)), ((), ())), preferred_element_type=F32) + bias_ref[g]
          for g in groups]
    rows = lambda t, hd: t[(hd % ATT_GROUP) * ATT_BLOCK:(hd % ATT_GROUP + 1) * ATT_BLOCK]
    s = [rows(sg[hd // ATT_GROUP], hd) for hd in heads]
    m = [jnp.maximum(jnp.max(s[hd], axis=-1, keepdims=True), sink_ref[hd]) for hd in heads]
    e = [jnp.exp(s[hd] - m[hd]) for hd in heads]
    den = [jnp.sum(e[hd], axis=-1, keepdims=True) + jnp.exp(sink_ref[hd] - m[hd]) for hd in heads]
    eg = [jnp.concatenate([e[g * ATT_GROUP + j].astype(BF16) for j in range(ATT_GROUP)], axis=0) for g in groups]
    og = [jnp.dot(eg[g], vs[g], preferred_element_type=F32) for g in groups]
    for hd in heads:
        o_ref[:, hd * hd_:(hd + 1) * hd_] = (rows(og[hd // ATT_GROUP], hd) / den[hd]).astype(BF16)


def _attention(za, sink, bsz, seqlen):
    nb = seqlen // ATT_BLOCK
    assert nb >= 2
    za3 = za.reshape(bsz, seqlen, ZA)
    kvw = 2 * ATT_KV
    kvc = ATT_Q // kvw
    bias = _attn_bias_table()
    return pl.pallas_call(
        _attn_kernel,
        grid=(bsz, nb),
        in_specs=[pl.BlockSpec(memory_space=pltpu.SMEM),
                  pl.BlockSpec((None, ATT_BLOCK, ATT_Q), lambda b, n: (b, n, 0)),
                  pl.BlockSpec((None, ATT_BLOCK, kvw), lambda b, n: (b, jnp.maximum(n - 1, 0), kvc)),
                  pl.BlockSpec((None, ATT_BLOCK, kvw), lambda b, n: (b, n, kvc)),
                  pl.BlockSpec((None, ATT_BLOCK, kvw), lambda b, n: (b, jnp.minimum(n + 1, nb - 1), kvc)),
                  pl.BlockSpec((None,) + bias.shape[1:],
                               lambda b, n: (jnp.where(n == 0, 0, jnp.where(n == nb - 1, 2, 1)), 0, 0, 0))],
        out_specs=pl.BlockSpec((None, ATT_BLOCK, ATT_Q), lambda b, n: (b, n, 0)),
        out_shape=jax.ShapeDtypeStruct((bsz, seqlen, ATT_Q), BF16),
        compiler_params=_params(("parallel", "parallel")),
        name="window_attention",
    )(sink, za3, za3, za3, za3, bias).reshape(bsz * seqlen, ATT_Q)


DN_HALO = SUBLANES


def _dn_prep_kernel(x_ref, xp_ref, xn_ref, cw_ref, hs_ref, g_ref, aneg_ref, dtb_ref, mf_ref, mb_ref,
                    y_ref, gb_ref, buf_ref, *, tl):
    i = pl.program_id(1)
    nt = pl.num_programs(1)
    buf_ref[0:DN_HALO, :] = jnp.where(i > 0, xp_ref[...], 0.0)
    buf_ref[DN_HALO:DN_HALO + tl, :] = x_ref[...]
    buf_ref[DN_HALO + tl:, :] = jnp.where(i < nt - 1, xn_ref[...], 0.0)
    y = (cw_ref[0:1, :] * buf_ref[DN_HALO - 1:DN_HALO - 1 + tl, :]
         + cw_ref[1:2, :] * buf_ref[DN_HALO:DN_HALO + tl, :]
         + cw_ref[2:3, :] * buf_ref[DN_HALO + 1:DN_HALO + 1 + tl, :])
    y = _silu(y)
    qk = y[:, :2 * DN_WIDTH]
    ss = jnp.dot((qk * qk).astype(BF16), hs_ref[...], preferred_element_type=F32)
    lane = lax.broadcasted_iota(jnp.int32, (tl, 2 * DN_WIDTH), 1)
    scale = jnp.where(lane < DN_WIDTH, DN_HEAD_DIM ** -0.5, 1.0)
    y_ref[:, :2 * DN_WIDTH] = qk * lax.rsqrt(ss + NORM_EPS) * scale
    y_ref[:, 2 * DN_WIDTH:] = y[:, 2 * DN_WIDTH:]
    raw = g_ref[...]
    col = lax.broadcasted_iota(jnp.int32, (tl, LANES), 1)
    is_beta = (col & DN_HEADS) == 0
    t = raw + dtb_ref[...]
    softplus = jnp.maximum(t, 0.0) + jnp.log(1.0 + jnp.exp(-jnp.abs(t)))
    vals = jnp.where(is_beta, _sigmoid(raw), aneg_ref[...] * softplus)
    v_hi = vals.astype(BF16)
    r1 = vals - v_hi.astype(F32)
    v_mid = r1.astype(BF16)
    v_lo = (r1 - v_mid.astype(F32)).astype(BF16)
    terms = jnp.concatenate([v_hi, v_mid, v_lo], axis=1)
    cf3 = jnp.dot(mf_ref[...], terms, preferred_element_type=F32)
    cb3 = jnp.dot(mb_ref[...], terms, preferred_element_type=F32)
    cf = cf3[:, :LANES] + (cf3[:, LANES:2 * LANES] + cf3[:, 2 * LANES:])
    cb = cb3[:, :LANES] + (cb3[:, LANES:2 * LANES] + cb3[:, 2 * LANES:])
    gb_ref[0] = jnp.where(is_beta, vals, cf)
    gb_ref[1] = pltpu.roll(jnp.where(is_beta, vals, cb), LANES - 2 * DN_HEADS, axis=1)


def _dn_prep(zb, gates, conv_w, hs, aneg, dtb, bsz, seqlen, tl):
    zb3 = zb.reshape(bsz, seqlen, ZB)
    g3 = gates.reshape(bsz, seqlen, LANES)
    nt = seqlen // tl
    hb = tl // DN_HALO
    ch = np.arange(tl) // DN_CHUNK
    same = ch[:, None] == ch[None, :]
    pos = np.arange(tl)
    mf = jnp.asarray((same & (pos[None, :] <= pos[:, None])).astype(np.float32), dtype=BF16)
    mb = jnp.asarray((same & (pos[None, :] >= pos[:, None])).astype(np.float32), dtype=BF16)
    fixed = lambda b, i: (0, 0)
    y, gb = pl.pallas_call(
        functools.partial(_dn_prep_kernel, tl=tl),
        grid=(bsz, nt),
        in_specs=[pl.BlockSpec((None, tl, ZB), lambda b, i: (b, i, 0)),
                  pl.BlockSpec((None, DN_HALO, ZB), lambda b, i: (b, jnp.maximum(i * hb - 1, 0), 0)),
                  pl.BlockSpec((None, DN_HALO, ZB), lambda b, i: (b, jnp.minimum((i + 1) * hb, nt * hb - 1), 0)),
                  pl.BlockSpec(conv_w.shape, fixed), pl.BlockSpec(hs.shape, fixed),
                  pl.BlockSpec((None, tl, LANES), lambda b, i: (b, i, 0)),
                  pl.BlockSpec((1, LANES), fixed), pl.BlockSpec((1, LANES), fixed),
                  pl.BlockSpec((tl, tl), fixed), pl.BlockSpec((tl, tl), fixed)],
        out_specs=[pl.BlockSpec((None, tl, ZB), lambda b, i: (b, i, 0)),
                   pl.BlockSpec((2, None, tl, LANES), lambda b, i: (0, b, i, 0))],
        out_shape=[jax.ShapeDtypeStruct((bsz, seqlen, ZB), F32),
                   jax.ShapeDtypeStruct((2, bsz, seqlen, LANES), F32)],
        scratch_shapes=[pltpu.VMEM((tl + 2 * DN_HALO, ZB), F32)],
        compiler_params=_params(("parallel", "parallel")),
        name="deltanet_prep",
    )(zb3, zb3, zb3, conv_w, hs, g3, aneg, dtb, mf, mb)
    return y, gb


def _lane_expand(cols, first):
    c = cols.shape[0]
    lane = lax.broadcasted_iota(jnp.int32, (c, LANES), 1)
    halves = []
    for h in range(0, DN_HEADS, 2):
        a = jnp.broadcast_to(cols[:, first + h:first + h + 1], (c, LANES))
        b = jnp.broadcast_to(cols[:, first + h + 1:first + h + 2], (c, LANES))
        halves.append(jnp.where(lane < DN_HEAD_DIM, a, b))
    return jnp.concatenate(halves, axis=1)


def _dn_pair_kernel(xf_ref, xb_ref, gf_ref, gb_ref, of_ref, ob_ref, sf_ref, sb_ref, *, nsub):
    c = DN_CHUNK
    w = DN_WIDTH

    @pl.when(pl.program_id(1) == 0)
    def _():
        sf_ref[...] = jnp.zeros_like(sf_ref)
        sb_ref[...] = jnp.zeros_like(sb_ref)

    r_cat = lax.broadcasted_iota(jnp.int32, (c, w), 0)
    s_cat = lax.broadcasted_iota(jnp.int32, (c, w), 1) & (DN_HEAD_DIM - 1)
    eye_cat = s_cat == r_cat
    rr = lax.broadcasted_iota(jnp.int32, (w, w), 0)
    cc = lax.broadcasted_iota(jnp.int32, (w, w), 1)
    head = (rr >> 6) == (cc >> 6)
    head_b = head.astype(BF16)
    m16 = (s_cat >> 4) == (r_cat >> 4)
    m32 = (s_cat >> 5) == (r_cat >> 5)
    off16 = m32 & jnp.logical_not(m16)
    off32 = jnp.logical_not(m32)
    eye_f = eye_cat.astype(F32)

    def bd(t):
        return jnp.concatenate([t] * DN_HEADS, axis=0) * head_b

    def mm(a, b):
        return jnp.dot(a, b, preferred_element_type=F32)

    chunks = [(0, i * c) for i in range(nsub)] + [(1, (nsub - 1 - i) * c) for i in range(nsub)]
    xrefs = (xf_ref, xb_ref)
    grefs = (gf_ref, gb_ref)
    orefs = (of_ref, ob_ref)
    srefs = (sf_ref, sb_ref)
    incl = (s_cat <= r_cat, s_cat >= r_cat)
    strict = (s_cat < r_cat, s_cat > r_cat)
    last_row = (c - 1, 0)

    pre = []
    for d, st in chunks:
        x = xrefs[d][st:st + c, :]
        q, k, v = x[:, :w], x[:, w:2 * w], x[:, 2 * w:]
        gbt = grefs[d][st:st + c, :]
        beta = _lane_expand(gbt, 0)
        gc = _lane_expand(gbt, DN_HEADS)
        grow = jnp.sum(jnp.where(eye_cat, gc, 0.0), axis=0, keepdims=True)
        decay = jnp.exp(jnp.where(incl[d], gc - grow, -jnp.inf))
        glast = gc[last_row[d]:last_row[d] + 1, :]
        egc = jnp.exp(gc)
        kb = k * beta
        pre.append(dict(d=d, st=st, q=q, k=k, kb=kb, vb=v * beta, decay=decay, glast=glast, egc=egc,
                        kdec=(k * jnp.exp(glast - gc)).astype(BF16)))

    kks = [lax.dot_general(jnp.concatenate([p["kb"], p["q"]], axis=0).astype(BF16), bd(p["k"].astype(BF16)),
                           (((1,), (1,)), ((), ())), preferred_element_type=F32) for p in pre]
    a = [jnp.where(strict[p["d"]], kk[:c] * p["decay"], 0.0) for p, kk in zip(pre, kks)]
    intra = [jnp.where(incl[p["d"]], kk[c:] * p["decay"], 0.0).astype(BF16) for p, kk in zip(pre, kks)]
    xm = [jnp.where(m16, -t, 0.0) for t in a]
    xm_b = [t.astype(BF16) for t in xm]
    x2_b = [mm(t, bd(t)).astype(BF16) for t in xm_b]
    x2_d = [bd(t) for t in x2_b]
    dinv = [eye_f + t for t in xm]
    r2 = [mm(jnp.concatenate([t.astype(BF16), p2], axis=0), s2) for t, p2, s2 in zip(dinv, x2_b, x2_d)]
    dinv = [t + r[:c] for t, r in zip(dinv, r2)]
    x4_b = [r[c:].astype(BF16) for r in r2]
    x4_d = [bd(t) for t in x4_b]
    r4 = [mm(jnp.concatenate([t.astype(BF16), p4], axis=0), s4) for t, p4, s4 in zip(dinv, x4_b, x4_d)]
    dinv = [t + r[:c] for t, r in zip(dinv, r4)]
    x8_d = [bd(r[c:].astype(BF16)) for r in r4]
    dinv = [t + mm(t.astype(BF16), s8) for t, s8 in zip(dinv, x8_d)]
    dinv_b = [t.astype(BF16) for t in dinv]
    n32 = [bd(mm(jnp.where(off16, t, 0.0).astype(BF16), bd(db)).astype(BF16)) for t, db in zip(a, dinv_b)]
    t32 = [t - mm(db, n) for t, db, n in zip(dinv, dinv_b, n32)]
    t32_b = [t.astype(BF16) for t in t32]
    n64 = [bd(mm(jnp.where(off32, t, 0.0).astype(BF16), bd(tb)).astype(BF16)) for t, tb in zip(a, t32_b)]
    t_cat = [(t - mm(tb, n)).astype(BF16) for t, tb, n in zip(t32, t32_b, n64)]
    uw = [mm(tc, jnp.concatenate([bd(p["vb"].astype(BF16)), bd((p["kb"] * p["egc"]).astype(BF16))], axis=1))
          for tc, p in zip(t_cat, pre)]
    uw_b = [t.astype(BF16) for t in uw]
    pn = [lax.dot_general(p["kdec"], t, (((0,), (0,)), ((), ())), preferred_element_type=F32)
          for p, t in zip(pre, uw_b)]
    qo = [mm(it, jnp.concatenate([bd(t[:, :w]), bd(t[:, w:])], axis=1)) for it, t in zip(intra, uw_b)]
    lhs = [jnp.concatenate([n[:, w:].astype(BF16) * head_b, (p["q"] * p["egc"] - o[:, w:]).astype(BF16)], axis=0)
           for n, o, p in zip(pn, qo, pre)]
    for step in range(nsub):
        for d in range(2):
            i = d * nsub + step
            p = pre[i]
            state = srefs[d][...]
            r = jnp.dot(lhs[i], state.astype(BF16), preferred_element_type=F32)
            orefs[d][p["st"]:p["st"] + c, :] = r[w:] + qo[i][:, :w]
            srefs[d][...] = state * jnp.exp(p["glast"]) - r[:w] + jnp.where(head, pn[i][:, :w], 0.0)


def _dn_chunk(y, gb, bsz, seqlen, ch):
    nsub = ch // DN_CHUNK
    nblk = seqlen // ch
    fwd = lambda b, j: (b, j, 0)
    bwd = lambda b, j: (b, nblk - 1 - j, 0)
    o_f, o_b = pl.pallas_call(
        functools.partial(_dn_pair_kernel, nsub=nsub),
        grid=(bsz, nblk),
        in_specs=[pl.BlockSpec((None, ch, ZB), fwd), pl.BlockSpec((None, ch, ZB), bwd),
                  pl.BlockSpec((None, None, ch, LANES), lambda b, j: (0, b, j, 0)),
                  pl.BlockSpec((None, None, ch, LANES), lambda b, j: (1, b, nblk - 1 - j, 0))],
        out_specs=[pl.BlockSpec((None, ch, DN_WIDTH), fwd), pl.BlockSpec((None, ch, DN_WIDTH), bwd)],
        out_shape=[jax.ShapeDtypeStruct((bsz, seqlen, DN_WIDTH), F32)] * 2,
        scratch_shapes=[pltpu.VMEM((DN_WIDTH, DN_WIDTH), F32)] * 2,
        compiler_params=_params(("parallel", "arbitrary")),
        name="deltanet_chunks",
    )(y, y, gb, gb)
    return o_f.reshape(bsz * seqlen, DN_WIDTH), o_b.reshape(bsz * seqlen, DN_WIDTH)


CV_HALO = 2 * SUBLANES
CV_PAD = (CONV_WIDTH - 1) // 2


def _conv_kernel(x_ref, xp_ref, xn_ref, dw_ref, bias_ref, lng_ref, lnb_ref, o_ref, buf_ref, shift_ref, *, tl):
    i = pl.program_id(1)
    nt = pl.num_programs(1)

    def glu(t):
        return t[:, :CONV_CH] * _sigmoid(t[:, CONV_CH:])

    buf_ref[0:CV_HALO, :] = jnp.where(i > 0, glu(xp_ref[...]), 0.0)
    buf_ref[CV_HALO:CV_HALO + tl, :] = glu(x_ref[...])
    buf_ref[CV_HALO + tl:, :] = jnp.where(i < nt - 1, glu(xn_ref[...]), 0.0)
    acc = jnp.zeros((tl, CONV_CH), F32) + bias_ref[...]
    first = CV_HALO - CV_PAD
    span = -(-(first + CONV_WIDTH) // SUBLANES) * SUBLANES - SUBLANES
    for sub in range(SUBLANES):
        shift_ref[...] = buf_ref[sub:sub + tl + span, :]
        for base in range(0, span + 1, SUBLANES):
            j = base + sub - first
            if 0 <= j < CONV_WIDTH:
                acc = acc + dw_ref[j:j + 1, :] * shift_ref[base:base + tl, :]
    mu = jnp.mean(acc, axis=-1, keepdims=True)
    cen = acc - mu
    var = jnp.mean(cen * cen, axis=-1, keepdims=True)
    o_ref[...] = _silu(cen * lax.rsqrt(var + NORM_EPS) * lng_ref[...] + lnb_ref[...]).astype(BF16)


def _conformer_conv(glu_in, dw, bias, lng, lnb, bsz, seqlen, tl):
    x3 = glu_in.reshape(bsz, seqlen, 2 * CONV_CH)
    nt = seqlen // tl
    hb = tl // CV_HALO
    fixed = lambda b, i: (0, 0)
    return pl.pallas_call(
        functools.partial(_conv_kernel, tl=tl),
        grid=(bsz, nt),
        in_specs=[pl.BlockSpec((None, tl, 2 * CONV_CH), lambda b, i: (b, i, 0)),
                  pl.BlockSpec((None, CV_HALO, 2 * CONV_CH), lambda b, i: (b, jnp.maximum(i * hb - 1, 0), 0)),
                  pl.BlockSpec((None, CV_HALO, 2 * CONV_CH),
                               lambda b, i: (b, jnp.minimum((i + 1) * hb, nt * hb - 1), 0)),
                  pl.BlockSpec(dw.shape, fixed), pl.BlockSpec((1, CONV_CH), fixed),
                  pl.BlockSpec((1, CONV_CH), fixed), pl.BlockSpec((1, CONV_CH), fixed)],
        out_specs=pl.BlockSpec((None, tl, CONV_CH), lambda b, i: (b, i, 0)),
        out_shape=jax.ShapeDtypeStruct((bsz, seqlen, CONV_CH), BF16),
        scratch_shapes=[pltpu.VMEM((tl + 2 * CV_HALO, CONV_CH), F32),
                        pltpu.VMEM((tl + 2 * CV_HALO - SUBLANES, CONV_CH), F32)],
        compiler_params=_params(("parallel", "parallel")),
        name="conformer_conv",
    )(x3, x3, x3, dw, bias, lng, lnb).reshape(bsz * seqlen, CONV_CH)


def _out_proj_kernel(h_ref, oa_ref, of_ref, ob_ref, zg_ref, oc_ref, og_ref, hm_ref, w_ref, gain_ref, wr_ref,
                     out_ref, xn_ref, aff_ref):
    ob = of_ref[...] + ob_ref[...]
    ms = jnp.dot((ob * ob).astype(BF16), hm_ref[...], preferred_element_type=F32)
    obn = ob * lax.rsqrt(ms + NORM_EPS) * og_ref[...]
    ob2 = obn * _silu(zg_ref[...])
    mix = jnp.concatenate([oa_ref[...], ob2.astype(BF16), oc_ref[...]], axis=1)
    x = h_ref[...] + jnp.dot(mix, w_ref[...], preferred_element_type=F32)
    out_ref[...] = x
    ms = jnp.mean(x * x, axis=-1, keepdims=True)
    xn = x * lax.rsqrt(ms + NORM_EPS) * gain_ref[...]
    xn_hi = xn.astype(BF16)
    xn_ref[...] = xn_hi
    xn_lo = (xn - xn_hi.astype(F32)).astype(BF16)
    logits = (jnp.dot(xn_hi, wr_ref[0], preferred_element_type=F32)
              + (jnp.dot(xn_lo, wr_ref[0], preferred_element_type=F32)
                 + jnp.dot(xn_hi, wr_ref[1], preferred_element_type=F32)))
    lane = lax.broadcasted_iota(jnp.int32, logits.shape, 1)
    logits = jnp.where(lane < N_EXPERTS, logits, -jnp.inf)
    m = jnp.max(logits, axis=-1, keepdims=True)
    e = jnp.exp(logits - m)
    aff = e / jnp.sum(e, axis=-1, keepdims=True)
    aff_ref[...] = jnp.transpose(aff)[:N_EXPERTS, :]


def _out_proj_route(h2, oa, o_f, o_b, zg, oc, og, hm, w, gain, wr2, tm):
    n = h2.shape[0]
    row = lambda i: (i, 0)
    fixed = lambda i: (0, 0)
    return pl.pallas_call(
        _out_proj_kernel,
        grid=(n // tm,),
        in_specs=[pl.BlockSpec((tm, D_MODEL), row), pl.BlockSpec((tm, ATT_Q), row),
                  pl.BlockSpec((tm, DN_WIDTH), row), pl.BlockSpec((tm, DN_WIDTH), row),
                  pl.BlockSpec((tm, DN_WIDTH), row),
                  pl.BlockSpec((tm, CONV_CH), row), pl.BlockSpec((1, DN_WIDTH), fixed),
                  pl.BlockSpec(hm.shape, fixed), pl.BlockSpec(w.shape, fixed),
                  pl.BlockSpec((1, D_MODEL), fixed), pl.BlockSpec((2, D_MODEL, LANES), lambda i: (0, 0, 0))],
        out_specs=[pl.BlockSpec((tm, D_MODEL), row), pl.BlockSpec((tm, D_MODEL), row),
                   pl.BlockSpec((N_EXPERTS, tm), lambda i: (0, i))],
        out_shape=[jax.ShapeDtypeStruct((n, D_MODEL), F32), jax.ShapeDtypeStruct((n, D_MODEL), BF16),
                   jax.ShapeDtypeStruct((N_EXPERTS, n), F32)],
        compiler_params=_params(("parallel",)),
        name="out_proj_route",
    )(h2, oa, o_f, o_b, zg, oc, og, hm, w, gain, wr2)


MOE_TILE = 256
MOE_ALIGN = 2 * SUBLANES
MOE_WIN = 64
MOE_PAD = 1024
MOE_FFN_TILE = 1024
FF_CHUNK = 256
MOE_UNSELECTED = -64.0


def _select_kernel(aff_ref, tri_ref, val_ref, cnt_ref, *, cap, tile):
    ne, n = aff_ref.shape
    nt = n // tile
    capf = float(cap)

    def bits_of(x):
        return lax.bitcast_convert_type(x, jnp.int32)

    def search(i, thr):
        cand = thr | jnp.left_shift(jnp.int32(1), 30 - i)
        cnt = jnp.sum((bits_of(aff_ref[...]) >= cand).astype(F32), axis=1, keepdims=True)
        return jnp.where(cnt >= capf, cand, thr)

    thr = lax.fori_loop(0, 31, search, jnp.zeros((ne, 1), jnp.int32))
    n_gt = jnp.sum((bits_of(aff_ref[...]) > thr).astype(F32), axis=1, keepdims=True)
    need = capf - n_gt
    lane = lax.broadcasted_iota(jnp.int32, (ne, LANES), 1)

    def tile_body(j, carry):
        eq_before, cnt_acc = carry
        off = pl.multiple_of(j * tile, tile)
        b = bits_of(aff_ref[:, pl.ds(off, tile)])
        gt = b > thr
        eqf = (b == thr).astype(F32)
        eq_rank = eq_before + jnp.dot(eqf.astype(BF16), tri_ref[...], preferred_element_type=F32)
        self_ = jnp.where(gt, 1.0, jnp.where(eq_rank <= need, eqf, 0.0))
        rank = jnp.dot(self_.astype(BF16), tri_ref[...], preferred_element_type=F32)
        val_ref[:, pl.ds(off, tile)] = jnp.where(self_ > 0.0, rank, MOE_UNSELECTED)
        cnt = jnp.sum(self_, axis=1, keepdims=True)
        return (eq_before + jnp.sum(eqf, axis=1, keepdims=True), cnt_acc + jnp.where(lane == j, cnt, 0.0))

    init = (jnp.zeros((ne, 1), F32), jnp.zeros((ne, LANES), F32))
    _, cnt_acc = lax.fori_loop(0, nt, tile_body, init)
    cnt_ref[...] = cnt_acc


def _select(aff_t, cap, tile):
    ne, n = aff_t.shape
    assert n // tile <= LANES
    tri = jnp.asarray(np.triu(np.ones((tile, tile), np.float32)), dtype=BF16)
    return pl.pallas_call(
        functools.partial(_select_kernel, cap=cap, tile=tile),
        out_shape=[jax.ShapeDtypeStruct((ne, n), F32), jax.ShapeDtypeStruct((ne, LANES), F32)],
        compiler_params=pltpu.CompilerParams(vmem_limit_bytes=VMEM_LIMIT),
        name="moe_select",
    )(aff_t, tri)


def _moe_plan(cnt, nt):
    c = cnt[:, :nt].astype(jnp.int32).T
    starts = jnp.concatenate([jnp.zeros((1, N_EXPERTS), jnp.int32), jnp.cumsum(c, axis=0)], axis=0)
    head = starts[:-1] & (MOE_ALIGN - 1)
    kmax = jnp.maximum(jnp.max((head + c + MOE_WIN - 1) // MOE_WIN, axis=1), 1).astype(jnp.int32)
    w = jnp.arange(MOE_WIN, dtype=jnp.int32)
    tgt = (w[None, None, :] + 1 - head[:, :, None]).astype(F32).reshape(nt, 1, N_EXPERTS * MOE_WIN)
    return starts.reshape(-1), kmax, tgt


def _expand_matrix():
    e = np.arange(N_EXPERTS * MOE_WIN) // MOE_WIN
    return jnp.asarray((np.arange(N_EXPERTS)[:, None] == e[None, :]).astype(np.float32), dtype=BF16)


def _slot_onehot(val_ref, eexp_ref):
    return lax.dot_general(val_ref[...].astype(BF16), eexp_ref[...], (((0,), (0,)), ((), ())),
                           preferred_element_type=F32)


def _dispatch_kernel(start_ref, kmax_ref, xn_ref, val_ref, tgt_ref, eexp_ref, xe_ref, stage, carry, sem):
    j = pl.program_id(0)
    nt = pl.num_programs(0)
    slot = lax.rem(j, 2)
    ne = N_EXPERTS

    cap = xe_ref.shape[1] - MOE_PAD

    @pl.when(j == 0)
    def _():
        carry[...] = jnp.zeros_like(carry)
        stage[0, 0:MOE_PAD, :] = jnp.zeros((MOE_PAD, D_MODEL), BF16)
        fills = [pltpu.make_async_copy(stage.at[0, pl.ds(0, MOE_PAD)], xe_ref.at[e, pl.ds(cap, MOE_PAD)], sem.at[0])
                 for e in range(ne)]
        for f in fills:
            f.start()
        for f in fills:
            f.wait()

    def window_copy(sl, e, row0):
        return pltpu.make_async_copy(stage.at[sl, pl.ds(e * MOE_WIN, MOE_WIN)],
                                     xe_ref.at[e, pl.ds(row0, MOE_WIN)], sem.at[sl])

    def wait_windows(sl):
        for e in range(ne):
            window_copy(sl, e, 0).wait()

    rep = _slot_onehot(val_ref, eexp_ref)
    xn = xn_ref[...]
    row = lax.broadcasted_iota(jnp.int32, (MOE_ALIGN, D_MODEL), 0)

    def block(k, _):
        @pl.when(k > 0)
        def _():
            wait_windows(slot)

        lo = k * MOE_WIN
        pt = (rep == tgt_ref[...] + lo.astype(F32)).astype(BF16)
        comp = lax.dot_general(pt, xn, (((0,), (0,)), ((), ())), preferred_element_type=F32)
        stage[slot] = comp.astype(BF16)
        for e in range(ne):
            s = start_ref[j * ne + e]
            head = s & (MOE_ALIGN - 1)
            r0 = e * MOE_WIN

            @pl.when(k == 0)
            def _():
                fresh = stage[slot, r0:r0 + MOE_ALIGN, :]
                kept = carry[e * MOE_ALIGN:(e + 1) * MOE_ALIGN, :]
                stage[slot, r0:r0 + MOE_ALIGN, :] = jnp.where(row < head, kept, fresh)

            nxt = (head + start_ref[(j + 1) * ne + e] - s) & (-MOE_ALIGN)

            @pl.when((nxt >= lo) & (nxt < lo + MOE_WIN))
            def _():
                off = pl.multiple_of(nxt - lo, MOE_ALIGN)
                carry[e * MOE_ALIGN:(e + 1) * MOE_ALIGN, :] = stage[slot, pl.ds(r0 + off, MOE_ALIGN), :]

        @pl.when((k == 0) & (j > 0))
        def _():
            wait_windows(1 - slot)

        for e in range(ne):
            base = pl.multiple_of((start_ref[j * ne + e] & (-MOE_ALIGN)) + lo, MOE_ALIGN)
            window_copy(slot, e, base).start()
        return 0

    lax.fori_loop(0, kmax_ref[j], block, 0)

    @pl.when(j == nt - 1)
    def _():
        wait_windows(slot)


def _dispatch(xn, val, starts, kmax, tgt, eexp, cap, tile):
    n = xn.shape[0]
    nt = n // tile
    rows = N_EXPERTS * MOE_WIN
    return pl.pallas_call(
        _dispatch_kernel,
        grid_spec=pltpu.PrefetchScalarGridSpec(
            num_scalar_prefetch=2, grid=(nt,),
            in_specs=[pl.BlockSpec((tile, D_MODEL), lambda j, s, k: (j, 0)),
                      pl.BlockSpec((N_EXPERTS, tile), lambda j, s, k: (0, j)),
                      pl.BlockSpec((None, 1, rows), lambda j, s, k: (j, 0, 0)),
                      pl.BlockSpec((N_EXPERTS, rows), lambda j, s, k: (0, 0))],
            out_specs=pl.BlockSpec(memory_space=pl.ANY),
            scratch_shapes=[pltpu.VMEM((2, rows, D_MODEL), BF16),
                            pltpu.VMEM((N_EXPERTS * MOE_ALIGN, D_MODEL), BF16),
                            pltpu.SemaphoreType.DMA((2,))]),
        out_shape=jax.ShapeDtypeStruct((N_EXPERTS, cap + MOE_PAD, D_MODEL), BF16),
        compiler_params=_params(("arbitrary",)),
        name="moe_dispatch",
    )(starts, kmax, xn, val, tgt, eexp)


def _expert_kernel(x_ref, wr_ref, wg_ref, wu_ref, wd_ref, y_ref, *, ntile):
    e = pl.program_id(0)
    i = pl.program_id(1)

    @pl.when(i < ntile)
    def _():
        x = x_ref[...]
        logits = (jnp.dot(x, wr_ref[0], preferred_element_type=F32)
                  + jnp.dot(x, wr_ref[1], preferred_element_type=F32))
        lane = lax.broadcasted_iota(jnp.int32, logits.shape, 1)
        logits = jnp.where(lane < N_EXPERTS, logits, -jnp.inf)
        ex = jnp.exp(logits - jnp.max(logits, axis=-1, keepdims=True))
        gate = (jnp.sum(jnp.where(lane == e, ex, 0.0), axis=-1, keepdims=True)
                / jnp.sum(ex, axis=-1, keepdims=True))
        hid = []
        for c0 in range(0, EXPERT_FF, FF_CHUNK):
            hg = jnp.dot(x, wg_ref[:, c0:c0 + FF_CHUNK], preferred_element_type=F32)
            hu = jnp.dot(x, wu_ref[:, c0:c0 + FF_CHUNK], preferred_element_type=F32)
            hid.append((_silu(hg) * hu).astype(BF16))
        hid = jnp.concatenate(hid, axis=1)
        y_ref[...] = (jnp.dot(hid, wd_ref[...], preferred_element_type=F32) * gate).astype(BF16)

    @pl.when(i >= ntile)
    def _():
        y_ref[...] = jnp.zeros_like(y_ref)


def _expert_ffn(xe, wr2, wg, wu, wd, cap, tc):
    ne, rows, _ = xe.shape
    wspec = lambda shape: pl.BlockSpec((None,) + shape, lambda e, i: (e, 0, 0))
    return pl.pallas_call(
        functools.partial(_expert_kernel, ntile=cap // tc),
        grid=(ne, rows // tc),
        in_specs=[pl.BlockSpec((None, tc, D_MODEL), lambda e, i: (e, i, 0)),
                  pl.BlockSpec(wr2.shape, lambda e, i: (0, 0, 0)),
                  wspec((D_MODEL, EXPERT_FF)), wspec((D_MODEL, EXPERT_FF)), wspec((EXPERT_FF, D_MODEL))],
        out_specs=pl.BlockSpec((None, tc, D_MODEL), lambda e, i: (e, i, 0)),
        out_shape=jax.ShapeDtypeStruct((ne, rows, D_MODEL), BF16),
        compiler_params=_params(("parallel", "parallel")),
        name="expert_ffn",
    )(xe, wr2, wg, wu, wd)


COMBINE_SUB = 2


def _combine_kernel(start_ref, kmax_ref, h_ref, p_ref, val_ref, tgt_ref, eexp_ref, gain_ref, wg_ref, wp_ref,
                    ye_ref, out_ref, stage, sem, *, tile):
    j = pl.program_id(0)
    nstep = pl.num_programs(0)
    slot = lax.rem(j, 2)
    ne = N_EXPERTS
    subs = range(COMBINE_SUB)

    def window_copy(sl, sub, e, row0):
        buf = sl * COMBINE_SUB + sub
        return pltpu.make_async_copy(ye_ref.at[e, pl.ds(row0, MOE_WIN)],
                                     stage.at[buf, pl.ds(e * MOE_WIN, MOE_WIN)], sem.at[buf])

    def fetch(sl, sub, tile_idx, lo):
        for e in range(ne):
            base = pl.multiple_of((start_ref[tile_idx * ne + e] & (-MOE_ALIGN)) + lo, MOE_ALIGN)
            window_copy(sl, sub, e, base).start()

    def wait_windows(sl, sub):
        for e in range(ne):
            window_copy(sl, sub, e, 0).wait()

    @pl.when(j == 0)
    def _():
        for sub in subs:
            fetch(slot, sub, sub, 0)

    @pl.when(j + 1 < nstep)
    def _():
        for sub in subs:
            fetch(1 - slot, sub, (j + 1) * COMBINE_SUB + sub, 0)

    reps = [lax.dot_general(val_ref[:, sub * tile:(sub + 1) * tile].astype(BF16), eexp_ref[...],
                            (((0,), (0,)), ((), ())), preferred_element_type=F32) for sub in subs]
    pts = [(reps[sub] == tgt_ref[sub]).astype(BF16) for sub in subs]
    for sub in subs:
        wait_windows(slot, sub)
    accs = [h_ref[sub * tile:(sub + 1) * tile, :]
            + jnp.dot(pts[sub], stage[slot * COMBINE_SUB + sub], preferred_element_type=F32) for sub in subs]
    for sub in subs:
        t = j * COMBINE_SUB + sub

        def extra(k, acc, sub=sub, t=t):
            lo = k * MOE_WIN
            fetch(slot, sub, t, lo)
            wait_windows(slot, sub)
            pk = (reps[sub] == tgt_ref[sub] + lo.astype(F32)).astype(BF16)
            return acc + jnp.dot(pk, stage[slot * COMBINE_SUB + sub], preferred_element_type=F32)

        accs[sub] = lax.fori_loop(1, kmax_ref[t], extra, accs[sub])
    x = jnp.concatenate(accs, axis=0)
    ms = jnp.mean(x * x, axis=-1, keepdims=True)
    xn = (x * lax.rsqrt(ms + NORM_EPS) * gain_ref[...]).astype(BF16)
    gate = _sigmoid(jnp.dot(xn, wg_ref[...], preferred_element_type=F32))
    proj = jnp.dot(p_ref[...].astype(BF16), wp_ref[...], preferred_element_type=F32)
    out_ref[...] = x + gate * proj


def _combine_ple(h2, p3, layer, ye, val, starts, kmax, tgt, eexp, gain, wg, wp, tile):
    n = h2.shape[0]
    step = COMBINE_SUB * tile
    assert n % step == 0
    rows = N_EXPERTS * MOE_WIN
    fixed = lambda j, s, k: (0, 0)
    return pl.pallas_call(
        functools.partial(_combine_kernel, tile=tile),
        grid_spec=pltpu.PrefetchScalarGridSpec(
            num_scalar_prefetch=2, grid=(n // step,),
            in_specs=[pl.BlockSpec((step, D_MODEL), lambda j, s, k: (j, 0)),
                      pl.BlockSpec((None, step, PLE_DIM), lambda j, s, k: (layer, j, 0)),
                      pl.BlockSpec((N_EXPERTS, step), lambda j, s, k: (0, j)),
                      pl.BlockSpec((COMBINE_SUB, 1, rows), lambda j, s, k: (j, 0, 0)),
                      pl.BlockSpec((N_EXPERTS, rows), fixed),
                      pl.BlockSpec((1, D_MODEL), fixed), pl.BlockSpec(wg.shape, fixed),
                      pl.BlockSpec(wp.shape, fixed),
                      pl.BlockSpec(memory_space=pl.ANY)],
            out_specs=pl.BlockSpec((step, D_MODEL), lambda j, s, k: (j, 0)),
            scratch_shapes=[pltpu.VMEM((2 * COMBINE_SUB, rows, D_MODEL), BF16),
                            pltpu.SemaphoreType.DMA((2 * COMBINE_SUB,))]),
        out_shape=jax.ShapeDtypeStruct((n, D_MODEL), F32),
        compiler_params=_params(("arbitrary",)),
        name="moe_combine_ple",
    )(starts, kmax, h2, p3, val, tgt, eexp, gain, wg, wp, ye)


def _relayout_w_in(w_in):
    o_beta = ZA + ZB + DN_WIDTH
    o_alpha = o_beta + 2 * DN_HEADS
    o_glu = o_alpha + 2 * DN_HEADS
    pieces = [w_in[:, :o_beta], w_in[:, o_glu:o_glu + 2 * CONV_CH]]
    for d in range(2):
        pieces.append(w_in[:, o_beta + d * DN_HEADS:o_beta + (d + 1) * DN_HEADS])
        pieces.append(w_in[:, o_alpha + d * DN_HEADS:o_alpha + (d + 1) * DN_HEADS])
    pieces.append(jnp.zeros((w_in.shape[0], LANES - 4 * DN_HEADS), w_in.dtype))
    return jnp.concatenate(pieces, axis=1).astype(BF16)


def _prep_layer(lw):
    (norm_mix, w_in, q_gain, k_gain, sink, dn_conv, dn_a_log, dn_dt_bias, dn_out_gain,
     cv_dw, cv_dw_bias, cv_ln_gain, cv_ln_bias, w_out, norm_ffn, w_router, w_gate, w_up, w_down,
     norm_ple, w_ple_gate, w_ple_proj) = lw
    w_perm = _relayout_w_in(w_in)
    hgain = jnp.concatenate([jnp.tile(q_gain, ATT_HEADS) * (ATT_HEAD_DIM ** -0.5),
                             jnp.tile(k_gain, ATT_KV_HEADS)]).reshape(1, -1)
    zeros4 = jnp.zeros((DN_HEADS,), F32)
    aneg = -jnp.exp(dn_a_log.astype(F32))
    aneg_row = jnp.concatenate([zeros4, aneg[0], zeros4, aneg[1]])
    dtb_row = jnp.concatenate([zeros4, dn_dt_bias[0], zeros4, dn_dt_bias[1]])
    pad = lambda r: jnp.pad(r, (0, LANES - r.shape[0])).reshape(1, LANES)
    wr = jnp.pad(w_router.astype(F32), ((0, 0), (0, LANES - N_EXPERTS)))
    wr_hi = wr.astype(BF16)
    wr2 = jnp.stack([wr_hi, (wr - wr_hi.astype(F32)).astype(BF16)])
    return dict(
        w_router2=wr2,
        norm_mix=norm_mix.reshape(1, -1), w_in=w_perm, hgain=hgain, sink=sink.astype(F32),
        dn_conv=dn_conv, aneg=pad(aneg_row), dtb=pad(dtb_row),
        dn_out_gain=jnp.tile(dn_out_gain, DN_HEADS).reshape(1, -1),
        cv_dw=cv_dw, cv_dw_bias=cv_dw_bias.reshape(1, -1), cv_ln_gain=cv_ln_gain.reshape(1, -1),
        cv_ln_bias=cv_ln_bias.reshape(1, -1), w_out=w_out.astype(BF16),
        norm_ffn=norm_ffn.reshape(1, -1),
        w_gate=w_gate.astype(BF16), w_up=w_up.astype(BF16), w_down=w_down.astype(BF16),
        norm_ple=norm_ple.reshape(1, -1), w_ple_gate=w_ple_gate.astype(BF16), w_ple_proj=w_ple_proj.astype(BF16))


def _tiles(bsz, seqlen):
    n = bsz * seqlen
    return dict(tm=min(512, n), tl=min(256, seqlen), ch=min(256, seqlen), tcv=min(512, seqlen))


def _moe_ple(h2, xn, aff_t, p3, layer, pw):
    n = h2.shape[0]
    cap = CAPACITY_FACTOR * n // N_EXPERTS
    tile = min(MOE_TILE, n)
    val, cnt = _select(aff_t, cap, tile)
    starts, kmax, tgt = _moe_plan(cnt, n // tile)
    eexp = _expand_matrix()
    xe = _dispatch(xn, val, starts, kmax, tgt, eexp, cap, tile)
    ye = _expert_ffn(xe, pw["w_router2"], pw["w_gate"], pw["w_up"], pw["w_down"], cap, min(MOE_FFN_TILE, cap))
    return _combine_ple(h2, p3, layer, ye, val, starts, kmax, tgt, eexp, pw["norm_ple"], pw["w_ple_gate"],
                        pw["w_ple_proj"], tile)


def _layer(h2, p3, layer, pw, bsz, seqlen):
    t = _tiles(bsz, seqlen)
    hm_att = _head_mean_matrix(ATT_Q + ATT_KV, ATT_HEAD_DIM)
    hs_dn = _head_sum_matrix(2 * DN_WIDTH, DN_HEAD_DIM)
    hm_dn = _head_mean_matrix(DN_WIDTH, DN_HEAD_DIM)
    za, zb, zg, glu_in, gates = _in_proj(h2, pw["norm_mix"], pw["w_in"], hm_att, pw["hgain"], t["tm"])
    o_a = _attention(za, pw["sink"], bsz, seqlen)
    y, gb = _dn_prep(zb, gates, pw["dn_conv"], hs_dn, pw["aneg"], pw["dtb"], bsz, seqlen, t["tl"])
    o_f, o_b = _dn_chunk(y, gb, bsz, seqlen, t["ch"])
    o_c = _conformer_conv(glu_in, pw["cv_dw"], pw["cv_dw_bias"], pw["cv_ln_gain"], pw["cv_ln_bias"],
                          bsz, seqlen, t["tcv"])
    h2, xn, aff_t = _out_proj_route(h2, o_a, o_f, o_b, zg, o_c, pw["dn_out_gain"], hm_dn, pw["w_out"],
                                    pw["norm_ffn"], pw["w_router2"], t["tm"])
    return _moe_ple(h2, xn, aff_t, p3, layer, pw)


def _trunk(x, p, layer_weights):
    bsz, seqlen, _ = x.shape
    h2 = x.reshape(bsz * seqlen, D_MODEL)
    p3 = p.reshape(p.shape[0], bsz * seqlen, PLE_DIM)
    for i, pw in enumerate(layer_weights):
        h2 = _layer(h2, p3, i, pw, bsz, seqlen)
    return h2.reshape(bsz, seqlen, D_MODEL)


def kernel(x_prompt, x_sample, p_prompt, p_sample, norm_mix, w_in, q_gain, k_gain, sink, dn_conv, dn_a_log,
           dn_dt_bias, dn_out_gain, cv_dw, cv_dw_bias, cv_ln_gain, cv_ln_bias, w_out, norm_ffn, w_router,
           w_gate, w_up, w_down, norm_ple, w_ple_gate, w_ple_proj):
    weights = (norm_mix, w_in, q_gain, k_gain, sink, dn_conv, dn_a_log, dn_dt_bias, dn_out_gain,
               cv_dw, cv_dw_bias, cv_ln_gain, cv_ln_bias, w_out, norm_ffn, w_router, w_gate, w_up, w_down,
               norm_ple, w_ple_gate, w_ple_proj)
    depth = w_in.shape[0]
    layer_weights = [_prep_layer([w[i] for w in weights]) for i in range(depth)]
    return (_trunk(x_prompt, p_prompt, layer_weights), _trunk(x_sample, p_sample, layer_weights))
```

```python
import functools

import numpy as np
import jax
import jax.numpy as jnp
from jax import lax
from jax.experimental import pallas as pl
from jax.experimental.pallas import tpu as pltpu

F32 = jnp.float32
BF16 = jnp.bfloat16

D_MODEL = 1024
ATT_HEADS = 8
ATT_KV_HEADS = 2
ATT_HEAD_DIM = 64
ATT_GROUP = ATT_HEADS // ATT_KV_HEADS
WINDOW = 128
ATT_BLOCK = 128
DN_HEADS = 4
DN_HEAD_DIM = 64
DN_WIDTH = DN_HEADS * DN_HEAD_DIM
DN_CHUNK = 64
CONV_CH = 256
CONV_WIDTH = 31
ATT_Q = ATT_HEADS * ATT_HEAD_DIM
ATT_KV = ATT_KV_HEADS * ATT_HEAD_DIM
N_EXPERTS = 16
CAPACITY_FACTOR = 2
EXPERT_FF = 1024
PLE_DIM = 256
NORM_EPS = 1e-6

LANES = 128
SUBLANES = 8
VMEM_LIMIT = 48 * 1024 * 1024

ZA = ATT_Q + 2 * ATT_KV
ZB = 3 * DN_WIDTH
ZW = ZA + ZB + DN_WIDTH + 2 * CONV_CH + LANES


def _params(sem):
    return pltpu.CompilerParams(dimension_semantics=sem, vmem_limit_bytes=VMEM_LIMIT)


def _head_mean_matrix(width, head):
    idx = np.arange(width) // head
    return jnp.asarray((idx[:, None] == idx[None, :]).astype(np.float32) / head, dtype=BF16)


def _head_sum_matrix(width, head):
    idx = np.arange(width) // head
    return jnp.asarray((idx[:, None] == idx[None, :]).astype(np.float32), dtype=BF16)


def _sigmoid(x):
    return 1.0 / (1.0 + jnp.exp(-x))


def _silu(x):
    return x * _sigmoid(x)


def _in_proj_kernel(x_ref, gain_ref, w_ref, hm_ref, hgain_ref, za_ref, zb_ref, zg_ref, glu_ref, gates_ref):
    x = x_ref[...]
    ms = jnp.mean(x * x, axis=-1, keepdims=True)
    a = (x * lax.rsqrt(ms + NORM_EPS) * gain_ref[...]).astype(BF16)
    proj = lambda lo, hi: jnp.dot(a, w_ref[:, lo:hi], preferred_element_type=F32)
    nqk = ATT_Q + ATT_KV
    z = proj(0, ZA)
    qk = z[:, :nqk]
    hms = jnp.dot((qk * qk).astype(BF16), hm_ref[...], preferred_element_type=F32)
    za_ref[:, :nqk] = (qk * lax.rsqrt(hms + NORM_EPS) * hgain_ref[...]).astype(BF16)
    za_ref[:, nqk:] = z[:, nqk:].astype(BF16)
    zb_ref[...] = proj(ZA, ZA + ZB)
    z = proj(ZA + ZB, ZW)
    zg_ref[...] = z[:, :DN_WIDTH]
    glu_ref[...] = z[:, DN_WIDTH:DN_WIDTH + 2 * CONV_CH]
    gates_ref[...] = z[:, DN_WIDTH + 2 * CONV_CH:]


def _in_proj(h2, gain, w_perm, hm, hgain, tm):
    n = h2.shape[0]
    row = lambda i: (i, 0)
    fixed = lambda i: (0, 0)
    return pl.pallas_call(
        _in_proj_kernel,
        grid=(n // tm,),
        in_specs=[pl.BlockSpec((tm, D_MODEL), row), pl.BlockSpec((1, D_MODEL), fixed),
                  pl.BlockSpec((D_MODEL, ZW), fixed), pl.BlockSpec(hm.shape, fixed),
                  pl.BlockSpec(hgain.shape, fixed)],
        out_specs=[pl.BlockSpec((tm, ZA), row), pl.BlockSpec((tm, ZB), row), pl.BlockSpec((tm, DN_WIDTH), row),
                   pl.BlockSpec((tm, 2 * CONV_CH), row), pl.BlockSpec((tm, LANES), row)],
        out_shape=[jax.ShapeDtypeStruct((n, ZA), BF16), jax.ShapeDtypeStruct((n, ZB), F32),
                   jax.ShapeDtypeStruct((n, DN_WIDTH), F32), jax.ShapeDtypeStruct((n, 2 * CONV_CH), F32),
                   jax.ShapeDtypeStruct((n, LANES), F32)],
        compiler_params=_params(("parallel",)),
        name="in_proj",
    )(h2, gain, w_perm, hm, hgain)


ATT_MASKED = -1e30


def _attn_bias_table():
    i = np.arange(ATT_BLOCK)[:, None]
    c = np.arange(3 * ATT_BLOCK)[None, :]
    rel = c - ATT_BLOCK - i
    slopes = 2.0 ** (-8.0 * np.arange(1, ATT_HEADS + 1) / ATT_HEADS)
    table = np.empty((3, ATT_KV_HEADS, ATT_GROUP * ATT_BLOCK, 3 * ATT_BLOCK), np.float32)
    for variant in range(3):
        ok = np.abs(rel) <= WINDOW
        if variant == 0:
            ok = ok & (c >= ATT_BLOCK)
        if variant == 2:
            ok = ok & (c < 2 * ATT_BLOCK)
        for hd in range(ATT_HEADS):
            g, j = divmod(hd, ATT_GROUP)
            table[variant, g, j * ATT_BLOCK:(j + 1) * ATT_BLOCK] = np.where(ok, -slopes[hd] * np.abs(rel), ATT_MASKED)
    return jnp.asarray(table)


def _attn_kernel(sink_ref, q_ref, kvp_ref, kvo_ref, kvn_ref, bias_a_ref, bias_b_ref, o_ref):
    kv = jnp.concatenate([kvp_ref[...], kvo_ref[...], kvn_ref[...]], axis=0)
    hd_ = ATT_HEAD_DIM
    groups = range(ATT_KV_HEADS)
    heads = range(ATT_HEADS)
    rows = lambda t, hd: t[(hd % ATT_GROUP) * ATT_BLOCK:(hd % ATT_GROUP + 1) * ATT_BLOCK]
    work = []
    for blk, bias_ref in enumerate((bias_a_ref, bias_b_ref)):
        keys = kv[blk * ATT_BLOCK:(blk + 3) * ATT_BLOCK]
        ks = [keys[:, g * hd_:(g + 1) * hd_] for g in groups]
        vs = [keys[:, ATT_KV + g * hd_:ATT_KV + (g + 1) * hd_] for g in groups]
        q = q_ref[blk * ATT_BLOCK:(blk + 1) * ATT_BLOCK, :]
        qs = [jnp.concatenate([q[:, (g * ATT_GROUP + j) * hd_:(g * ATT_GROUP + j + 1) * hd_]
                               for j in range(ATT_GROUP)], axis=0) for g in groups]
        sg = [lax.dot_general(qs[g], ks[g], (((1,), (1,)), ((), ())), preferred_element_type=F32) + bias_ref[g]
              for g in groups]
        work.append((vs, [rows(sg[hd // ATT_GROUP], hd) for hd in heads]))
    m = [[jnp.maximum(jnp.max(s[hd], axis=-1, keepdims=True), sink_ref[hd]) for hd in heads] for _, s in work]
    e = [[jnp.exp(s[hd] - mb[hd]) for hd in heads] for (_, s), mb in zip(work, m)]
    den = [[jnp.sum(eb[hd], axis=-1, keepdims=True) + jnp.exp(sink_ref[hd] - mb[hd]) for hd in heads]
           for eb, mb in zip(e, m)]
    for blk, ((vs, _), eb, db) in enumerate(zip(work, e, den)):
        eg = [jnp.concatenate([eb[g * ATT_GROUP + j].astype(BF16) for j in range(ATT_GROUP)], axis=0)
              for g in groups]
        og = [jnp.dot(eg[g], vs[g], preferred_element_type=F32) for g in groups]
        for hd in heads:
            o_ref[blk * ATT_BLOCK:(blk + 1) * ATT_BLOCK, hd * hd_:(hd + 1) * hd_] = (
                rows(og[hd // ATT_GROUP], hd) / db[hd]).astype(BF16)


def _attention(za, sink, bsz, seqlen):
    nb = seqlen // ATT_BLOCK
    assert nb >= 2 and nb % 2 == 0
    npair = nb // 2
    za3 = za.reshape(bsz, seqlen, ZA)
    kvw = 2 * ATT_KV
    kvc = ATT_Q // kvw
    bias = _attn_bias_table()
    bias_spec = lambda pick: pl.BlockSpec((None,) + bias.shape[1:], lambda b, n: (pick(n), 0, 0, 0))
    return pl.pallas_call(
        _attn_kernel,
        grid=(bsz, npair),
        in_specs=[pl.BlockSpec(memory_space=pltpu.SMEM),
                  pl.BlockSpec((None, 2 * ATT_BLOCK, ATT_Q), lambda b, n: (b, n, 0)),
                  pl.BlockSpec((None, ATT_BLOCK, kvw), lambda b, n: (b, jnp.maximum(2 * n - 1, 0), kvc)),
                  pl.BlockSpec((None, 2 * ATT_BLOCK, kvw), lambda b, n: (b, n, kvc)),
                  pl.BlockSpec((None, ATT_BLOCK, kvw), lambda b, n: (b, jnp.minimum(2 * n + 2, nb - 1), kvc)),
                  bias_spec(lambda n: jnp.where(n == 0, 0, 1)),
                  bias_spec(lambda n: jnp.where(n == npair - 1, 2, 1))],
        out_specs=pl.BlockSpec((None, 2 * ATT_BLOCK, ATT_Q), lambda b, n: (b, n, 0)),
        out_shape=jax.ShapeDtypeStruct((bsz, seqlen, ATT_Q), BF16),
        compiler_params=_params(("parallel", "parallel")),
        name="window_attention",
    )(sink, za3, za3, za3, za3, bias, bias).reshape(bsz * seqlen, ATT_Q)


DN_HALO = SUBLANES


def _dn_prep_kernel(x_ref, xp_ref, xn_ref, cw_ref, hs_ref, g_ref, aneg_ref, dtb_ref, mf_ref, mb_ref,
                    y_ref, gb_ref, buf_ref, *, tl):
    i = pl.program_id(1)
    nt = pl.num_programs(1)
    buf_ref[0:DN_HALO, :] = jnp.where(i > 0, xp_ref[...], 0.0)
    buf_ref[DN_HALO:DN_HALO + tl, :] = x_ref[...]
    buf_ref[DN_HALO + tl:, :] = jnp.where(i < nt - 1, xn_ref[...], 0.0)
    y = (cw_ref[0:1, :] * buf_ref[DN_HALO - 1:DN_HALO - 1 + tl, :]
         + cw_ref[1:2, :] * buf_ref[DN_HALO:DN_HALO + tl, :]
         + cw_ref[2:3, :] * buf_ref[DN_HALO + 1:DN_HALO + 1 + tl, :])
    y = _silu(y)
    qk = y[:, :2 * DN_WIDTH]
    ss = jnp.dot((qk * qk).astype(BF16), hs_ref[...], preferred_element_type=F32)
    lane = lax.broadcasted_iota(jnp.int32, (tl, 2 * DN_WIDTH), 1)
    scale = jnp.where(lane < DN_WIDTH, DN_HEAD_DIM ** -0.5, 1.0)
    y_ref[:, :2 * DN_WIDTH] = qk * lax.rsqrt(ss + NORM_EPS) * scale
    y_ref[:, 2 * DN_WIDTH:] = y[:, 2 * DN_WIDTH:]
    raw = g_ref[...]
    col = lax.broadcasted_iota(jnp.int32, (tl, LANES), 1)
    is_beta = (col & DN_HEADS) == 0
    t = raw + dtb_ref[...]
    softplus = jnp.maximum(t, 0.0) + jnp.log(1.0 + jnp.exp(-jnp.abs(t)))
    vals = jnp.where(is_beta, _sigmoid(raw), aneg_ref[...] * softplus)
    v_hi = vals.astype(BF16)
    r1 = vals - v_hi.astype(F32)
    v_mid = r1.astype(BF16)
    v_lo = (r1 - v_mid.astype(F32)).astype(BF16)
    terms = jnp.concatenate([v_hi, v_mid, v_lo], axis=1)
    cf3 = jnp.dot(mf_ref[...], terms, preferred_element_type=F32)
    cb3 = jnp.dot(mb_ref[...], terms, preferred_element_type=F32)
    cf = cf3[:, :LANES] + (cf3[:, LANES:2 * LANES] + cf3[:, 2 * LANES:])
    cb = cb3[:, :LANES] + (cb3[:, LANES:2 * LANES] + cb3[:, 2 * LANES:])
    gb_ref[0] = jnp.where(is_beta, vals, cf)
    gb_ref[1] = pltpu.roll(jnp.where(is_beta, vals, cb), LANES - 2 * DN_HEADS, axis=1)


def _dn_prep(zb, gates, conv_w, hs, aneg, dtb, bsz, seqlen, tl):
    zb3 = zb.reshape(bsz, seqlen, ZB)
    g3 = gates.reshape(bsz, seqlen, LANES)
    nt = seqlen // tl
    hb = tl // DN_HALO
    ch = np.arange(tl) // DN_CHUNK
    same = ch[:, None] == ch[None, :]
    pos = np.arange(tl)
    mf = jnp.asarray((same & (pos[None, :] <= pos[:, None])).astype(np.float32), dtype=BF16)
    mb = jnp.asarray((same & (pos[None, :] >= pos[:, None])).astype(np.float32), dtype=BF16)
    fixed = lambda b, i: (0, 0)
    y, gb = pl.pallas_call(
        functools.partial(_dn_prep_kernel, tl=tl),
        grid=(bsz, nt),
        in_specs=[pl.BlockSpec((None, tl, ZB), lambda b, i: (b, i, 0)),
                  pl.BlockSpec((None, DN_HALO, ZB), lambda b, i: (b, jnp.maximum(i * hb - 1, 0), 0)),
                  pl.BlockSpec((None, DN_HALO, ZB), lambda b, i: (b, jnp.minimum((i + 1) * hb, nt * hb - 1), 0)),
                  pl.BlockSpec(conv_w.shape, fixed), pl.BlockSpec(hs.shape, fixed),
                  pl.BlockSpec((None, tl, LANES), lambda b, i: (b, i, 0)),
                  pl.BlockSpec((1, LANES), fixed), pl.BlockSpec((1, LANES), fixed),
                  pl.BlockSpec((tl, tl), fixed), pl.BlockSpec((tl, tl), fixed)],
        out_specs=[pl.BlockSpec((None, tl, ZB), lambda b, i: (b, i, 0)),
                   pl.BlockSpec((2, None, tl, LANES), lambda b, i: (0, b, i, 0))],
        out_shape=[jax.ShapeDtypeStruct((bsz, seqlen, ZB), F32),
                   jax.ShapeDtypeStruct((2, bsz, seqlen, LANES), F32)],
        scratch_shapes=[pltpu.VMEM((tl + 2 * DN_HALO, ZB), F32)],
        compiler_params=_params(("parallel", "parallel")),
        name="deltanet_prep",
    )(zb3, zb3, zb3, conv_w, hs, g3, aneg, dtb, mf, mb)
    return y, gb


def _lane_expand(cols, first):
    c = cols.shape[0]
    lane = lax.broadcasted_iota(jnp.int32, (c, LANES), 1)
    halves = []
    for h in range(0, DN_HEADS, 2):
        a = jnp.broadcast_to(cols[:, first + h:first + h + 1], (c, LANES))
        b = jnp.broadcast_to(cols[:, first + h + 1:first + h + 2], (c, LANES))
        halves.append(jnp.where(lane < DN_HEAD_DIM, a, b))
    return jnp.concatenate(halves, axis=1)


def _dn_pair_kernel(xf_ref, xb_ref, gf_ref, gb_ref, of_ref, ob_ref, sf_ref, sb_ref, *, nsub):
    c = DN_CHUNK
    w = DN_WIDTH

    @pl.when(pl.program_id(1) == 0)
    def _():
        sf_ref[...] = jnp.zeros_like(sf_ref)
        sb_ref[...] = jnp.zeros_like(sb_ref)

    r_cat = lax.broadcasted_iota(jnp.int32, (c, w), 0)
    s_cat = lax.broadcasted_iota(jnp.int32, (c, w), 1) & (DN_HEAD_DIM - 1)
    eye_cat = s_cat == r_cat
    rr = lax.broadcasted_iota(jnp.int32, (w, w), 0)
    cc = lax.broadcasted_iota(jnp.int32, (w, w), 1)
    head = (rr >> 6) == (cc >> 6)
    head_b = head.astype(BF16)
    m16 = (s_cat >> 4) == (r_cat >> 4)
    m32 = (s_cat >> 5) == (r_cat >> 5)
    off16 = m32 & jnp.logical_not(m16)
    off32 = jnp.logical_not(m32)
    eye_f = eye_cat.astype(F32)

    def bd(t):
        return jnp.concatenate([t] * DN_HEADS, axis=0) * head_b

    def mm(a, b):
        return jnp.dot(a, b, preferred_element_type=F32)

    chunks = [(0, i * c) for i in range(nsub)] + [(1, (nsub - 1 - i) * c) for i in range(nsub)]
    xrefs = (xf_ref, xb_ref)
    grefs = (gf_ref, gb_ref)
    orefs = (of_ref, ob_ref)
    srefs = (sf_ref, sb_ref)
    incl = (s_cat <= r_cat, s_cat >= r_cat)
    strict = (s_cat < r_cat, s_cat > r_cat)
    last_row = (c - 1, 0)

    pre = []
    for d, st in chunks:
        x = xrefs[d][st:st + c, :]
        q, k, v = x[:, :w], x[:, w:2 * w], x[:, 2 * w:]
        gbt = grefs[d][st:st + c, :]
        beta = _lane_expand(gbt, 0)
        gc = _lane_expand(gbt, DN_HEADS)
        grow = jnp.sum(jnp.where(eye_cat, gc, 0.0), axis=0, keepdims=True)
        decay = jnp.exp(jnp.where(incl[d], gc - grow, -jnp.inf))
        glast = gc[last_row[d]:last_row[d] + 1, :]
        egc = jnp.exp(gc)
        kb = k * beta
        pre.append(dict(d=d, st=st, q=q, k=k, kb=kb, vb=v * beta, decay=decay, glast=glast, egc=egc,
                        kdec=(k * jnp.exp(glast - gc)).astype(BF16)))

    kks = [lax.dot_general(jnp.concatenate([p["kb"], p["q"]], axis=0).astype(BF16), bd(p["k"].astype(BF16)),
                           (((1,), (1,)), ((), ())), preferred_element_type=F32) for p in pre]
    a = [jnp.where(strict[p["d"]], kk[:c] * p["decay"], 0.0) for p, kk in zip(pre, kks)]
    intra = [jnp.where(incl[p["d"]], kk[c:] * p["decay"], 0.0).astype(BF16) for p, kk in zip(pre, kks)]
    xm = [jnp.where(m16, -t, 0.0) for t in a]
    xm_b = [t.astype(BF16) for t in xm]
    x2_b = [mm(t, bd(t)).astype(BF16) for t in xm_b]
    x2_d = [bd(t) for t in x2_b]
    dinv = [eye_f + t for t in xm]
    r2 = [mm(jnp.concatenate([t.astype(BF16), p2], axis=0), s2) for t, p2, s2 in zip(dinv, x2_b, x2_d)]
    dinv = [t + r[:c] for t, r in zip(dinv, r2)]
    x4_b = [r[c:].astype(BF16) for r in r2]
    x4_d = [bd(t) for t in x4_b]
    r4 = [mm(jnp.concatenate([t.astype(BF16), p4], axis=0), s4) for t, p4, s4 in zip(dinv, x4_b, x4_d)]
    dinv = [t + r[:c] for t, r in zip(dinv, r4)]
    x8_d = [bd(r[c:].astype(BF16)) for r in r4]
    dinv = [t + mm(t.astype(BF16), s8) for t, s8 in zip(dinv, x8_d)]
    dinv_b = [t.astype(BF16) for t in dinv]
    n32 = [bd(mm(jnp.where(off16, t, 0.0).astype(BF16), bd(db)).astype(BF16)) for t, db in zip(a, dinv_b)]
    t32 = [t - mm(db, n) for t, db, n in zip(dinv, dinv_b, n32)]
    t32_b = [t.astype(BF16) for t in t32]
    n64 = [bd(mm(jnp.where(off32, t, 0.0).astype(BF16), bd(tb)).astype(BF16)) for t, tb in zip(a, t32_b)]
    t_cat = [(t - mm(tb, n)).astype(BF16) for t, tb, n in zip(t32, t32_b, n64)]
    uw = [mm(tc, jnp.concatenate([bd(p["vb"].astype(BF16)), bd((p["kb"] * p["egc"]).astype(BF16))], axis=1))
          for tc, p in zip(t_cat, pre)]
    uw_b = [t.astype(BF16) for t in uw]
    pn = [lax.dot_general(p["kdec"], t, (((0,), (0,)), ((), ())), preferred_element_type=F32)
          for p, t in zip(pre, uw_b)]
    qo = [mm(it, jnp.concatenate([bd(t[:, :w]), bd(t[:, w:])], axis=1)) for it, t in zip(intra, uw_b)]
    lhs = [jnp.concatenate([n[:, w:].astype(BF16) * head_b, (p["q"] * p["egc"] - o[:, w:]).astype(BF16)], axis=0)
           for n, o, p in zip(pn, qo, pre)]
    for step in range(nsub):
        for d in range(2):
            i = d * nsub + step
            p = pre[i]
            state = srefs[d][...]
            r = jnp.dot(lhs[i], state.astype(BF16), preferred_element_type=F32)
            orefs[d][p["st"]:p["st"] + c, :] = r[w:] + qo[i][:, :w]
            srefs[d][...] = state * jnp.exp(p["glast"]) - r[:w] + jnp.where(head, pn[i][:, :w], 0.0)


def _dn_chunk(y, gb, bsz, seqlen, ch):
    nsub = ch // DN_CHUNK
    nblk = seqlen // ch
    fwd = lambda b, j: (b, j, 0)
    bwd = lambda b, j: (b, nblk - 1 - j, 0)
    o_f, o_b = pl.pallas_call(
        functools.partial(_dn_pair_kernel, nsub=nsub),
        grid=(bsz, nblk),
        in_specs=[pl.BlockSpec((None, ch, ZB), fwd), pl.BlockSpec((None, ch, ZB), bwd),
                  pl.BlockSpec((None, None, ch, LANES), lambda b, j: (0, b, j, 0)),
                  pl.BlockSpec((None, None, ch, LANES), lambda b, j: (1, b, nblk - 1 - j, 0))],
        out_specs=[pl.BlockSpec((None, ch, DN_WIDTH), fwd), pl.BlockSpec((None, ch, DN_WIDTH), bwd)],
        out_shape=[jax.ShapeDtypeStruct((bsz, seqlen, DN_WIDTH), F32)] * 2,
        scratch_shapes=[pltpu.VMEM((DN_WIDTH, DN_WIDTH), F32)] * 2,
        compiler_params=_params(("parallel", "arbitrary")),
        name="deltanet_chunks",
    )(y, y, gb, gb)
    return o_f.reshape(bsz * seqlen, DN_WIDTH), o_b.reshape(bsz * seqlen, DN_WIDTH)


CV_HALO = 2 * SUBLANES
CV_PAD = (CONV_WIDTH - 1) // 2


def _conv_kernel(x_ref, xp_ref, xn_ref, dw_ref, bias_ref, lng_ref, lnb_ref, o_ref, buf_ref, shift_ref, *, tl):
    i = pl.program_id(1)
    nt = pl.num_programs(1)

    def glu(t):
        return t[:, :CONV_CH] * _sigmoid(t[:, CONV_CH:])

    buf_ref[0:CV_HALO, :] = jnp.where(i > 0, glu(xp_ref[...]), 0.0)
    buf_ref[CV_HALO:CV_HALO + tl, :] = glu(x_ref[...])
    buf_ref[CV_HALO + tl:, :] = jnp.where(i < nt - 1, glu(xn_ref[...]), 0.0)
    acc = jnp.zeros((tl, CONV_CH), F32) + bias_ref[...]
    first = CV_HALO - CV_PAD
    span = -(-(first + CONV_WIDTH) // SUBLANES) * SUBLANES - SUBLANES
    for sub in range(SUBLANES):
        shift_ref[...] = buf_ref[sub:sub + tl + span, :]
        for base in range(0, span + 1, SUBLANES):
            j = base + sub - first
            if 0 <= j < CONV_WIDTH:
                acc = acc + dw_ref[j:j + 1, :] * shift_ref[base:base + tl, :]
    mu = jnp.mean(acc, axis=-1, keepdims=True)
    cen = acc - mu
    var = jnp.mean(cen * cen, axis=-1, keepdims=True)
    o_ref[...] = _silu(cen * lax.rsqrt(var + NORM_EPS) * lng_ref[...] + lnb_ref[...]).astype(BF16)


def _conformer_conv(glu_in, dw, bias, lng, lnb, bsz, seqlen, tl):
    x3 = glu_in.reshape(bsz, seqlen, 2 * CONV_CH)
    nt = seqlen // tl
    hb = tl // CV_HALO
    fixed = lambda b, i: (0, 0)
    return pl.pallas_call(
        functools.partial(_conv_kernel, tl=tl),
        grid=(bsz, nt),
        in_specs=[pl.BlockSpec((None, tl, 2 * CONV_CH), lambda b, i: (b, i, 0)),
                  pl.BlockSpec((None, CV_HALO, 2 * CONV_CH), lambda b, i: (b, jnp.maximum(i * hb - 1, 0), 0)),
                  pl.BlockSpec((None, CV_HALO, 2 * CONV_CH),
                               lambda b, i: (b, jnp.minimum((i + 1) * hb, nt * hb - 1), 0)),
                  pl.BlockSpec(dw.shape, fixed), pl.BlockSpec((1, CONV_CH), fixed),
                  pl.BlockSpec((1, CONV_CH), fixed), pl.BlockSpec((1, CONV_CH), fixed)],
        out_specs=pl.BlockSpec((None, tl, CONV_CH), lambda b, i: (b, i, 0)),
        out_shape=jax.ShapeDtypeStruct((bsz, seqlen, CONV_CH), BF16),
        scratch_shapes=[pltpu.VMEM((tl + 2 * CV_HALO, CONV_CH), F32),
                        pltpu.VMEM((tl + 2 * CV_HALO - SUBLANES, CONV_CH), F32)],
        compiler_params=_params(("parallel", "parallel")),
        name="conformer_conv",
    )(x3, x3, x3, dw, bias, lng, lnb).reshape(bsz * seqlen, CONV_CH)


def _out_proj_kernel(h_ref, oa_ref, of_ref, ob_ref, zg_ref, oc_ref, og_ref, hm_ref, w_ref, gain_ref, wr_ref,
                     out_ref, xn_ref, aff_ref):
    half = h_ref.shape[0] // 2
    for r0 in (0, half):
        rs = slice(r0, r0 + half)
        ob = of_ref[rs, :] + ob_ref[rs, :]
        ms = jnp.dot((ob * ob).astype(BF16), hm_ref[...], preferred_element_type=F32)
        obn = ob * lax.rsqrt(ms + NORM_EPS) * og_ref[...]
        ob2 = obn * _silu(zg_ref[rs, :])
        mix = jnp.concatenate([oa_ref[rs, :], ob2.astype(BF16), oc_ref[rs, :]], axis=1)
        x = h_ref[rs, :] + jnp.dot(mix, w_ref[...], preferred_element_type=F32)
        out_ref[rs, :] = x
        ms = jnp.mean(x * x, axis=-1, keepdims=True)
        xn = x * lax.rsqrt(ms + NORM_EPS) * gain_ref[...]
        xn_hi = xn.astype(BF16)
        xn_ref[rs, :] = xn_hi
        xn_lo = (xn - xn_hi.astype(F32)).astype(BF16)
        logits = (jnp.dot(xn_hi, wr_ref[0], preferred_element_type=F32)
                  + (jnp.dot(xn_lo, wr_ref[0], preferred_element_type=F32)
                     + jnp.dot(xn_hi, wr_ref[1], preferred_element_type=F32)))
        lane = lax.broadcasted_iota(jnp.int32, logits.shape, 1)
        logits = jnp.where(lane < N_EXPERTS, logits, -jnp.inf)
        m = jnp.max(logits, axis=-1, keepdims=True)
        e = jnp.exp(logits - m)
        aff = e / jnp.sum(e, axis=-1, keepdims=True)
        aff_ref[:, rs] = jnp.transpose(aff)[:N_EXPERTS, :]


def _out_proj_route(h2, oa, o_f, o_b, zg, oc, og, hm, w, gain, wr2, tm):
    n = h2.shape[0]
    row = lambda i: (i, 0)
    fixed = lambda i: (0, 0)
    return pl.pallas_call(
        _out_proj_kernel,
        grid=(n // tm,),
        in_specs=[pl.BlockSpec((tm, D_MODEL), row), pl.BlockSpec((tm, ATT_Q), row),
                  pl.BlockSpec((tm, DN_WIDTH), row), pl.BlockSpec((tm, DN_WIDTH), row),
                  pl.BlockSpec((tm, DN_WIDTH), row),
                  pl.BlockSpec((tm, CONV_CH), row), pl.BlockSpec((1, DN_WIDTH), fixed),
                  pl.BlockSpec(hm.shape, fixed), pl.BlockSpec(w.shape, fixed),
                  pl.BlockSpec((1, D_MODEL), fixed), pl.BlockSpec((2, D_MODEL, LANES), lambda i: (0, 0, 0))],
        out_specs=[pl.BlockSpec((tm, D_MODEL), row), pl.BlockSpec((tm, D_MODEL), row),
                   pl.BlockSpec((N_EXPERTS, tm), lambda i: (0, i))],
        out_shape=[jax.ShapeDtypeStruct((n, D_MODEL), F32), jax.ShapeDtypeStruct((n, D_MODEL), BF16),
                   jax.ShapeDtypeStruct((N_EXPERTS, n), F32)],
        compiler_params=_params(("parallel",)),
        name="out_proj_route",
    )(h2, oa, o_f, o_b, zg, oc, og, hm, w, gain, wr2)


MOE_TILE = 256
MOE_ALIGN = 2 * SUBLANES
MOE_WIN = 64
MOE_PAD = 1024
MOE_FFN_TILE = 1024
FF_CHUNK = 256
MOE_UNSELECTED = -64.0


def _select_kernel(aff_ref, tri_ref, val_ref, cnt_ref, *, cap, tile):
    ne, n = aff_ref.shape
    nt = n // tile
    capf = float(cap)

    def bits_of(x):
        return lax.bitcast_convert_type(x, jnp.int32)

    def search(i, thr):
        cand = thr | jnp.left_shift(jnp.int32(1), 30 - i)
        cnt = jnp.sum((bits_of(aff_ref[...]) >= cand).astype(F32), axis=1, keepdims=True)
        return jnp.where(cnt >= capf, cand, thr)

    thr = lax.fori_loop(0, 31, search, jnp.zeros((ne, 1), jnp.int32))
    n_gt = jnp.sum((bits_of(aff_ref[...]) > thr).astype(F32), axis=1, keepdims=True)
    need = capf - n_gt
    lane = lax.broadcasted_iota(jnp.int32, (ne, LANES), 1)

    def tile_body(j, carry):
        eq_before, cnt_acc = carry
        off = pl.multiple_of(j * tile, tile)
        b = bits_of(aff_ref[:, pl.ds(off, tile)])
        gt = b > thr
        eqf = (b == thr).astype(F32)
        eq_rank = eq_before + jnp.dot(eqf.astype(BF16), tri_ref[...], preferred_element_type=F32)
        self_ = jnp.where(gt, 1.0, jnp.where(eq_rank <= need, eqf, 0.0))
        rank = jnp.dot(self_.astype(BF16), tri_ref[...], preferred_element_type=F32)
        val_ref[:, pl.ds(off, tile)] = jnp.where(self_ > 0.0, rank, MOE_UNSELECTED)
        cnt = jnp.sum(self_, axis=1, keepdims=True)
        return (eq_before + jnp.sum(eqf, axis=1, keepdims=True), cnt_acc + jnp.where(lane == j, cnt, 0.0))

    init = (jnp.zeros((ne, 1), F32), jnp.zeros((ne, LANES), F32))
    _, cnt_acc = lax.fori_loop(0, nt, tile_body, init)
    cnt_ref[...] = cnt_acc


def _select(aff_t, cap, tile):
    ne, n = aff_t.shape
    assert n // tile <= LANES
    tri = jnp.asarray(np.triu(np.ones((tile, tile), np.float32)), dtype=BF16)
    return pl.pallas_call(
        functools.partial(_select_kernel, cap=cap, tile=tile),
        out_shape=[jax.ShapeDtypeStruct((ne, n), F32), jax.ShapeDtypeStruct((ne, LANES), F32)],
        compiler_params=pltpu.CompilerParams(vmem_limit_bytes=VMEM_LIMIT),
        name="moe_select",
    )(aff_t, tri)


def _moe_plan(cnt, nt):
    c = cnt[:, :nt].astype(jnp.int32).T
    starts = jnp.concatenate([jnp.zeros((1, N_EXPERTS), jnp.int32), jnp.cumsum(c, axis=0)], axis=0)
    head = starts[:-1] & (MOE_ALIGN - 1)
    kmax = jnp.maximum(jnp.max((head + c + MOE_WIN - 1) // MOE_WIN, axis=1), 1).astype(jnp.int32)
    w = jnp.arange(MOE_WIN, dtype=jnp.int32)
    tgt = (w[None, None, :] + 1 - head[:, :, None]).astype(F32).reshape(nt, 1, N_EXPERTS * MOE_WIN)
    return starts.reshape(-1), kmax, tgt


def _expand_matrix():
    e = np.arange(N_EXPERTS * MOE_WIN) // MOE_WIN
    return jnp.asarray((np.arange(N_EXPERTS)[:, None] == e[None, :]).astype(np.float32), dtype=BF16)


def _slot_onehot(val_ref, eexp_ref):
    return lax.dot_general(val_ref[...].astype(BF16), eexp_ref[...], (((0,), (0,)), ((), ())),
                           preferred_element_type=F32)


def _dispatch_kernel(start_ref, kmax_ref, xn_ref, val_ref, tgt_ref, eexp_ref, xe_ref, stage, carry, sem):
    j = pl.program_id(0)
    nt = pl.num_programs(0)
    slot = lax.rem(j, 2)
    ne = N_EXPERTS

    cap = xe_ref.shape[1] - MOE_PAD

    @pl.when(j == 0)
    def _():
        carry[...] = jnp.zeros_like(carry)
        stage[0, 0:MOE_PAD, :] = jnp.zeros((MOE_PAD, D_MODEL), BF16)
        fills = [pltpu.make_async_copy(stage.at[0, pl.ds(0, MOE_PAD)], xe_ref.at[e, pl.ds(cap, MOE_PAD)], sem.at[0])
                 for e in range(ne)]
        for f in fills:
            f.start()
        for f in fills:
            f.wait()

    def window_copy(sl, e, row0):
        return pltpu.make_async_copy(stage.at[sl, pl.ds(e * MOE_WIN, MOE_WIN)],
                                     xe_ref.at[e, pl.ds(row0, MOE_WIN)], sem.at[sl])

    def wait_windows(sl):
        for e in range(ne):
            window_copy(sl, e, 0).wait()

    rep = _slot_onehot(val_ref, eexp_ref)
    xn = xn_ref[...]
    row = lax.broadcasted_iota(jnp.int32, (MOE_ALIGN, D_MODEL), 0)

    def block(k, _):
        @pl.when(k > 0)
        def _():
            wait_windows(slot)

        lo = k * MOE_WIN
        pt = (rep == tgt_ref[...] + lo.astype(F32)).astype(BF16)
        comp = lax.dot_general(pt, xn, (((0,), (0,)), ((), ())), preferred_element_type=F32)
        stage[slot] = comp.astype(BF16)
        for e in range(ne):
            s = start_ref[j * ne + e]
            head = s & (MOE_ALIGN - 1)
            r0 = e * MOE_WIN

            @pl.when(k == 0)
            def _():
                fresh = stage[slot, r0:r0 + MOE_ALIGN, :]
                kept = carry[e * MOE_ALIGN:(e + 1) * MOE_ALIGN, :]
                stage[slot, r0:r0 + MOE_ALIGN, :] = jnp.where(row < head, kept, fresh)

            nxt = (head + start_ref[(j + 1) * ne + e] - s) & (-MOE_ALIGN)

            @pl.when((nxt >= lo) & (nxt < lo + MOE_WIN))
            def _():
                off = pl.multiple_of(nxt - lo, MOE_ALIGN)
                carry[e * MOE_ALIGN:(e + 1) * MOE_ALIGN, :] = stage[slot, pl.ds(r0 + off, MOE_ALIGN), :]

        @pl.when((k == 0) & (j > 0))
        def _():
            wait_windows(1 - slot)

        for e in range(ne):
            base = pl.multiple_of((start_ref[j * ne + e] & (-MOE_ALIGN)) + lo, MOE_ALIGN)
            window_copy(slot, e, base).start()
        return 0

    lax.fori_loop(0, kmax_ref[j], block, 0)

    @pl.when(j == nt - 1)
    def _():
        wait_windows(slot)


def _dispatch(xn, val, starts, kmax, tgt, eexp, cap, tile):
    n = xn.shape[0]
    nt = n // tile
    rows = N_EXPERTS * MOE_WIN
    return pl.pallas_call(
        _dispatch_kernel,
        grid_spec=pltpu.PrefetchScalarGridSpec(
            num_scalar_prefetch=2, grid=(nt,),
            in_specs=[pl.BlockSpec((tile, D_MODEL), lambda j, s, k: (j, 0)),
                      pl.BlockSpec((N_EXPERTS, tile), lambda j, s, k: (0, j)),
                      pl.BlockSpec((None, 1, rows), lambda j, s, k: (j, 0, 0)),
                      pl.BlockSpec((N_EXPERTS, rows), lambda j, s, k: (0, 0))],
            out_specs=pl.BlockSpec(memory_space=pl.ANY),
            scratch_shapes=[pltpu.VMEM((2, rows, D_MODEL), BF16),
                            pltpu.VMEM((N_EXPERTS * MOE_ALIGN, D_MODEL), BF16),
                            pltpu.SemaphoreType.DMA((2,))]),
        out_shape=jax.ShapeDtypeStruct((N_EXPERTS, cap + MOE_PAD, D_MODEL), BF16),
        compiler_params=_params(("arbitrary",)),
        name="moe_dispatch",
    )(starts, kmax, xn, val, tgt, eexp)


def _expert_kernel(x_ref, wr_ref, wg_ref, wu_ref, wd_ref, y_ref, *, ntile):
    e = pl.program_id(0)
    i = pl.program_id(1)

    @pl.when(i < ntile)
    def _():
        x = x_ref[...]
        logits = (jnp.dot(x, wr_ref[0], preferred_element_type=F32)
                  + jnp.dot(x, wr_ref[1], preferred_element_type=F32))
        lane = lax.broadcasted_iota(jnp.int32, logits.shape, 1)
        logits = jnp.where(lane < N_EXPERTS, logits, -jnp.inf)
        ex = jnp.exp(logits - jnp.max(logits, axis=-1, keepdims=True))
        gate = (jnp.sum(jnp.where(lane == e, ex, 0.0), axis=-1, keepdims=True)
                / jnp.sum(ex, axis=-1, keepdims=True))
        hid = []
        for c0 in range(0, EXPERT_FF, FF_CHUNK):
            hg = jnp.dot(x, wg_ref[:, c0:c0 + FF_CHUNK], preferred_element_type=F32)
            hu = jnp.dot(x, wu_ref[:, c0:c0 + FF_CHUNK], preferred_element_type=F32)
            hid.append((_silu(hg) * hu).astype(BF16))
        hid = jnp.concatenate(hid, axis=1)
        y_ref[...] = (jnp.dot(hid, wd_ref[...], preferred_element_type=F32) * gate).astype(BF16)

    @pl.when(i >= ntile)
    def _():
        y_ref[...] = jnp.zeros_like(y_ref)


def _expert_ffn(xe, wr2, wg, wu, wd, cap, tc):
    ne, rows, _ = xe.shape
    wspec = lambda shape: pl.BlockSpec((None,) + shape, lambda e, i: (e, 0, 0))
    return pl.pallas_call(
        functools.partial(_expert_kernel, ntile=cap // tc),
        grid=(ne, rows // tc),
        in_specs=[pl.BlockSpec((None, tc, D_MODEL), lambda e, i: (e, i, 0)),
                  pl.BlockSpec(wr2.shape, lambda e, i: (0, 0, 0)),
                  wspec((D_MODEL, EXPERT_FF)), wspec((D_MODEL, EXPERT_FF)), wspec((EXPERT_FF, D_MODEL))],
        out_specs=pl.BlockSpec((None, tc, D_MODEL), lambda e, i: (e, i, 0)),
        out_shape=jax.ShapeDtypeStruct((ne, rows, D_MODEL), BF16),
        compiler_params=_params(("parallel", "parallel")),
        name="expert_ffn",
    )(xe, wr2, wg, wu, wd)


COMBINE_SUB = 2


def _combine_kernel(start_ref, kmax_ref, h_ref, p_ref, val_ref, tgt_ref, eexp_ref, gain_ref, wg_ref, wp_ref,
                    ye_ref, out_ref, stage, sem, *, tile):
    j = pl.program_id(0)
    nstep = pl.num_programs(0)
    slot = lax.rem(j, 2)
    ne = N_EXPERTS
    subs = range(COMBINE_SUB)

    def window_copy(sl, sub, e, row0):
        buf = sl * COMBINE_SUB + sub
        return pltpu.make_async_copy(ye_ref.at[e, pl.ds(row0, MOE_WIN)],
                                     stage.at[buf, pl.ds(e * MOE_WIN, MOE_WIN)], sem.at[buf])

    def fetch(sl, sub, tile_idx, lo):
        for e in range(ne):
            base = pl.multiple_of((start_ref[tile_idx * ne + e] & (-MOE_ALIGN)) + lo, MOE_ALIGN)
            window_copy(sl, sub, e, base).start()

    def wait_windows(sl, sub):
        for e in range(ne):
            window_copy(sl, sub, e, 0).wait()

    @pl.when(j == 0)
    def _():
        for sub in subs:
            fetch(slot, sub, sub, 0)

    @pl.when(j + 1 < nstep)
    def _():
        for sub in subs:
            fetch(1 - slot, sub, (j + 1) * COMBINE_SUB + sub, 0)

    reps = [lax.dot_general(val_ref[:, sub * tile:(sub + 1) * tile].astype(BF16), eexp_ref[...],
                            (((0,), (0,)), ((), ())), preferred_element_type=F32) for sub in subs]
    pts = [(reps[sub] == tgt_ref[sub]).astype(BF16) for sub in subs]
    for sub in subs:
        wait_windows(slot, sub)
    accs = [h_ref[sub * tile:(sub + 1) * tile, :]
            + jnp.dot(pts[sub], stage[slot * COMBINE_SUB + sub], preferred_element_type=F32) for sub in subs]
    for sub in subs:
        t = j * COMBINE_SUB + sub

        def extra(k, acc, sub=sub, t=t):
            lo = k * MOE_WIN
            fetch(slot, sub, t, lo)
            wait_windows(slot, sub)
            pk = (reps[sub] == tgt_ref[sub] + lo.astype(F32)).astype(BF16)
            return acc + jnp.dot(pk, stage[slot * COMBINE_SUB + sub], preferred_element_type=F32)

        accs[sub] = lax.fori_loop(1, kmax_ref[t], extra, accs[sub])
    x = jnp.concatenate(accs, axis=0)
    ms = jnp.mean(x * x, axis=-1, keepdims=True)
    xn = (x * lax.rsqrt(ms + NORM_EPS) * gain_ref[...]).astype(BF16)
    gate = _sigmoid(jnp.dot(xn, wg_ref[...], preferred_element_type=F32))
    proj = jnp.dot(p_ref[...].astype(BF16), wp_ref[...], preferred_element_type=F32)
    out_ref[...] = x + gate * proj


def _combine_ple(h2, p3, layer, ye, val, starts, kmax, tgt, eexp, gain, wg, wp, tile):
    n = h2.shape[0]
    step = COMBINE_SUB * tile
    assert n % step == 0
    rows = N_EXPERTS * MOE_WIN
    fixed = lambda j, s, k: (0, 0)
    return pl.pallas_call(
        functools.partial(_combine_kernel, tile=tile),
        grid_spec=pltpu.PrefetchScalarGridSpec(
            num_scalar_prefetch=2, grid=(n // step,),
            in_specs=[pl.BlockSpec((step, D_MODEL), lambda j, s, k: (j, 0)),
                      pl.BlockSpec((None, step, PLE_DIM), lambda j, s, k: (layer, j, 0)),
                      pl.BlockSpec((N_EXPERTS, step), lambda j, s, k: (0, j)),
                      pl.BlockSpec((COMBINE_SUB, 1, rows), lambda j, s, k: (j, 0, 0)),
                      pl.BlockSpec((N_EXPERTS, rows), fixed),
                      pl.BlockSpec((1, D_MODEL), fixed), pl.BlockSpec(wg.shape, fixed),
                      pl.BlockSpec(wp.shape, fixed),
                      pl.BlockSpec(memory_space=pl.ANY)],
            out_specs=pl.BlockSpec((step, D_MODEL), lambda j, s, k: (j, 0)),
            scratch_shapes=[pltpu.VMEM((2 * COMBINE_SUB, rows, D_MODEL), BF16),
                            pltpu.SemaphoreType.DMA((2 * COMBINE_SUB,))]),
        out_shape=jax.ShapeDtypeStruct((n, D_MODEL), F32),
        compiler_params=_params(("arbitrary",)),
        name="moe_combine_ple",
    )(starts, kmax, h2, p3, val, tgt, eexp, gain, wg, wp, ye)


def _relayout_w_in(w_in):
    o_beta = ZA + ZB + DN_WIDTH
    o_alpha = o_beta + 2 * DN_HEADS
    o_glu = o_alpha + 2 * DN_HEADS
    pieces = [w_in[:, :o_beta], w_in[:, o_glu:o_glu + 2 * CONV_CH]]
    for d in range(2):
        pieces.append(w_in[:, o_beta + d * DN_HEADS:o_beta + (d + 1) * DN_HEADS])
        pieces.append(w_in[:, o_alpha + d * DN_HEADS:o_alpha + (d + 1) * DN_HEADS])
    pieces.append(jnp.zeros((w_in.shape[0], LANES - 4 * DN_HEADS), w_in.dtype))
    return jnp.concatenate(pieces, axis=1).astype(BF16)


def _prep_layer(lw):
    (norm_mix, w_in, q_gain, k_gain, sink, dn_conv, dn_a_log, dn_dt_bias, dn_out_gain,
     cv_dw, cv_dw_bias, cv_ln_gain, cv_ln_bias, w_out, norm_ffn, w_router, w_gate, w_up, w_down,
     norm_ple, w_ple_gate, w_ple_proj) = lw
    w_perm = _relayout_w_in(w_in)
    hgain = jnp.concatenate([jnp.tile(q_gain, ATT_HEADS) * (ATT_HEAD_DIM ** -0.5),
                             jnp.tile(k_gain, ATT_KV_HEADS)]).reshape(1, -1)
    zeros4 = jnp.zeros((DN_HEADS,), F32)
    aneg = -jnp.exp(dn_a_log.astype(F32))
    aneg_row = jnp.concatenate([zeros4, aneg[0], zeros4, aneg[1]])
    dtb_row = jnp.concatenate([zeros4, dn_dt_bias[0], zeros4, dn_dt_bias[1]])
    pad = lambda r: jnp.pad(r, (0, LANES - r.shape[0])).reshape(1, LANES)
    wr = jnp.pad(w_router.astype(F32), ((0, 0), (0, LANES - N_EXPERTS)))
    wr_hi = wr.astype(BF16)
    wr2 = jnp.stack([wr_hi, (wr - wr_hi.astype(F32)).astype(BF16)])
    return dict(
        w_router2=wr2,
        norm_mix=norm_mix.reshape(1, -1), w_in=w_perm, hgain=hgain, sink=sink.astype(F32),
        dn_conv=dn_conv, aneg=pad(aneg_row), dtb=pad(dtb_row),
        dn_out_gain=jnp.tile(dn_out_gain, DN_HEADS).reshape(1, -1),
        cv_dw=cv_dw, cv_dw_bias=cv_dw_bias.reshape(1, -1), cv_ln_gain=cv_ln_gain.reshape(1, -1),
        cv_ln_bias=cv_ln_bias.reshape(1, -1), w_out=w_out.astype(BF16),
        norm_ffn=norm_ffn.reshape(1, -1),
        w_gate=w_gate.astype(BF16), w_up=w_up.astype(BF16), w_down=w_down.astype(BF16),
        norm_ple=norm_ple.reshape(1, -1), w_ple_gate=w_ple_gate.astype(BF16), w_ple_proj=w_ple_proj.astype(BF16))


def _tiles(bsz, seqlen):
    n = bsz * seqlen
    return dict(tm=min(512, n), tmo=min(1024, n), tl=min(256, seqlen), ch=min(256, seqlen), tcv=min(512, seqlen))


def _moe_ple(h2, xn, aff_t, p3, layer, pw):
    n = h2.shape[0]
    cap = CAPACITY_FACTOR * n // N_EXPERTS
    tile = min(MOE_TILE, n)
    val, cnt = _select(aff_t, cap, tile)
    starts, kmax, tgt = _moe_plan(cnt, n // tile)
    eexp = _expand_matrix()
    xe = _dispatch(xn, val, starts, kmax, tgt, eexp, cap, tile)
    ye = _expert_ffn(xe, pw["w_router2"], pw["w_gate"], pw["w_up"], pw["w_down"], cap, min(MOE_FFN_TILE, cap))
    return _combine_ple(h2, p3, layer, ye, val, starts, kmax, tgt, eexp, pw["norm_ple"], pw["w_ple_gate"],
                        pw["w_ple_proj"], tile)


def _layer(h2, p3, layer, pw, bsz, seqlen):
    t = _tiles(bsz, seqlen)
    hm_att = _head_mean_matrix(ATT_Q + ATT_KV, ATT_HEAD_DIM)
    hs_dn = _head_sum_matrix(2 * DN_WIDTH, DN_HEAD_DIM)
    hm_dn = _head_mean_matrix(DN_WIDTH, DN_HEAD_DIM)
    za, zb, zg, glu_in, gates = _in_proj(h2, pw["norm_mix"], pw["w_in"], hm_att, pw["hgain"], t["tm"])
    o_a = _attention(za, pw["sink"], bsz, seqlen)
    y, gb = _dn_prep(zb, gates, pw["dn_conv"], hs_dn, pw["aneg"], pw["dtb"], bsz, seqlen, t["tl"])
    o_f, o_b = _dn_chunk(y, gb, bsz, seqlen, t["ch"])
    o_c = _conformer_conv(glu_in, pw["cv_dw"], pw["cv_dw_bias"], pw["cv_ln_gain"], pw["cv_ln_bias"],
                          bsz, seqlen, t["tcv"])
    h2, xn, aff_t = _out_proj_route(h2, o_a, o_f, o_b, zg, o_c, pw["dn_out_gain"], hm_dn, pw["w_out"],
                                    pw["norm_ffn"], pw["w_router2"], t["tmo"])
    return _moe_ple(h2, xn, aff_t, p3, layer, pw)


def _trunk(x, p, layer_weights):
    bsz, seqlen, _ = x.shape
    h2 = x.reshape(bsz * seqlen, D_MODEL)
    p3 = p.reshape(p.shape[0], bsz * seqlen, PLE_DIM)
    for i, pw in enumerate(layer_weights):
        h2 = _layer(h2, p3, i, pw, bsz, seqlen)
    return h2.reshape(bsz, seqlen, D_MODEL)


def kernel(x_prompt, x_sample, p_prompt, p_sample, norm_mix, w_in, q_gain, k_gain, sink, dn_conv, dn_a_log,
           dn_dt_bias, dn_out_gain, cv_dw, cv_dw_bias, cv_ln_gain, cv_ln_bias, w_out, norm_ffn, w_router,
           w_gate, w_up, w_down, norm_ple, w_ple_gate, w_ple_proj):
    weights = (norm_mix, w_in, q_gain, k_gain, sink, dn_conv, dn_a_log, dn_dt_bias, dn_out_gain,
               cv_dw, cv_dw_bias, cv_ln_gain, cv_ln_bias, w_out, norm_ffn, w_router, w_gate, w_up, w_down,
               norm_ple, w_ple_gate, w_ple_proj)
    depth = w_in.shape[0]
    layer_weights = [_prep_layer([w[i] for w in weights]) for i in range(depth)]
    return (_trunk(x_prompt, p_prompt, layer_weights), _trunk(x_sample, p_sample, layer_weights))
```

```python
import functools

import numpy as np
import jax
import jax.numpy as jnp
from jax import lax
from jax.experimental import pallas as pl
from jax.experimental.pallas import tpu as pltpu

F32 = jnp.float32
BF16 = jnp.bfloat16

D_MODEL = 1024
ATT_HEADS = 8
ATT_KV_HEADS = 2
ATT_HEAD_DIM = 64
ATT_GROUP = ATT_HEADS // ATT_KV_HEADS
WINDOW = 128
ATT_BLOCK = 128
DN_HEADS = 4
DN_HEAD_DIM = 64
DN_WIDTH = DN_HEADS * DN_HEAD_DIM
DN_CHUNK = 64
CONV_CH = 256
CONV_WIDTH = 31
ATT_Q = ATT_HEADS * ATT_HEAD_DIM
ATT_KV = ATT_KV_HEADS * ATT_HEAD_DIM
N_EXPERTS = 16
CAPACITY_FACTOR = 2
EXPERT_FF = 1024
PLE_DIM = 256
NORM_EPS = 1e-6

LANES = 128
SUBLANES = 8
VMEM_LIMIT = 48 * 1024 * 1024

ZA = ATT_Q + 2 * ATT_KV
ZB = 3 * DN_WIDTH
ZW = ZA + ZB + DN_WIDTH + 2 * CONV_CH + LANES


def _params(sem):
    return pltpu.CompilerParams(dimension_semantics=sem, vmem_limit_bytes=VMEM_LIMIT)


def _head_mean_matrix(width, head):
    idx = np.arange(width) // head
    return jnp.asarray((idx[:, None] == idx[None, :]).astype(np.float32) / head, dtype=BF16)


def _head_sum_matrix(width, head):
    idx = np.arange(width) // head
    return jnp.asarray((idx[:, None] == idx[None, :]).astype(np.float32), dtype=BF16)


def _sigmoid(x):
    return 1.0 / (1.0 + jnp.exp(-x))


def _silu(x):
    return x * _sigmoid(x)


def _in_proj_kernel(x_ref, gain_ref, w_ref, hm_ref, hgain_ref, za_ref, zb_ref, zg_ref, glu_ref, gates_ref):
    x = x_ref[...]
    ms = jnp.mean(x * x, axis=-1, keepdims=True)
    a = (x * lax.rsqrt(ms + NORM_EPS) * gain_ref[...]).astype(BF16)
    z = jnp.dot(a, w_ref[...], preferred_element_type=F32)
    nqk = ATT_Q + ATT_KV
    qk = z[:, :nqk]
    hms = jnp.dot((qk * qk).astype(BF16), hm_ref[...], preferred_element_type=F32)
    za_ref[:, :nqk] = (qk * lax.rsqrt(hms + NORM_EPS) * hgain_ref[...]).astype(BF16)
    za_ref[:, nqk:] = z[:, nqk:ZA].astype(BF16)
    zb_ref[...] = z[:, ZA:ZA + ZB]
    zg_ref[...] = z[:, ZA + ZB:ZA + ZB + DN_WIDTH]
    glu_ref[...] = z[:, ZA + ZB + DN_WIDTH:ZA + ZB + DN_WIDTH + 2 * CONV_CH]
    gates_ref[...] = z[:, ZW - LANES:]


def _in_proj(h2, gain, w_perm, hm, hgain, tm):
    n = h2.shape[0]
    row = lambda i: (i, 0)
    fixed = lambda i: (0, 0)
    return pl.pallas_call(
        _in_proj_kernel,
        grid=(n // tm,),
        in_specs=[pl.BlockSpec((tm, D_MODEL), row), pl.BlockSpec((1, D_MODEL), fixed),
                  pl.BlockSpec((D_MODEL, ZW), fixed), pl.BlockSpec(hm.shape, fixed),
                  pl.BlockSpec(hgain.shape, fixed)],
        out_specs=[pl.BlockSpec((tm, ZA), row), pl.BlockSpec((tm, ZB), row), pl.BlockSpec((tm, DN_WIDTH), row),
                   pl.BlockSpec((tm, 2 * CONV_CH), row), pl.BlockSpec((tm, LANES), row)],
        out_shape=[jax.ShapeDtypeStruct((n, ZA), BF16), jax.ShapeDtypeStruct((n, ZB), F32),
                   jax.ShapeDtypeStruct((n, DN_WIDTH), F32), jax.ShapeDtypeStruct((n, 2 * CONV_CH), F32),
                   jax.ShapeDtypeStruct((n, LANES), F32)],
        compiler_params=_params(("parallel",)),
        name="in_proj",
    )(h2, gain, w_perm, hm, hgain)


ATT_MASKED = -1e30


def _attn_bias_table():
    i = np.arange(ATT_BLOCK)[:, None]
    c = np.arange(3 * ATT_BLOCK)[None, :]
    rel = c - ATT_BLOCK - i
    slopes = 2.0 ** (-8.0 * np.arange(1, ATT_HEADS + 1) / ATT_HEADS)
    table = np.empty((3, ATT_KV_HEADS, ATT_GROUP * ATT_BLOCK, 3 * ATT_BLOCK), np.float32)
    for variant in range(3):
        ok = np.abs(rel) <= WINDOW
        if variant == 0:
            ok = ok & (c >= ATT_BLOCK)
        if variant == 2:
            ok = ok & (c < 2 * ATT_BLOCK)
        for hd in range(ATT_HEADS):
            g, j = divmod(hd, ATT_GROUP)
            table[variant, g, j * ATT_BLOCK:(j + 1) * ATT_BLOCK] = np.where(ok, -slopes[hd] * np.abs(rel), ATT_MASKED)
    return jnp.asarray(table)


def _attn_kernel(sink_ref, q_ref, kvp_ref, kvo_ref, kvn_ref, bias_a_ref, bias_b_ref, o_ref):
    kv = jnp.concatenate([kvp_ref[...], kvo_ref[...], kvn_ref[...]], axis=0)
    hd_ = ATT_HEAD_DIM
    groups = range(ATT_KV_HEADS)
    heads = range(ATT_HEADS)
    rows = lambda t, hd: t[(hd % ATT_GROUP) * ATT_BLOCK:(hd % ATT_GROUP + 1) * ATT_BLOCK]
    work = []
    for blk, bias_ref in enumerate((bias_a_ref, bias_b_ref)):
        keys = kv[blk * ATT_BLOCK:(blk + 3) * ATT_BLOCK]
        ks = [keys[:, g * hd_:(g + 1) * hd_] for g in groups]
        vs = [keys[:, ATT_KV + g * hd_:ATT_KV + (g + 1) * hd_] for g in groups]
        q = q_ref[blk * ATT_BLOCK:(blk + 1) * ATT_BLOCK, :]
        qs = [jnp.concatenate([q[:, (g * ATT_GROUP + j) * hd_:(g * ATT_GROUP + j + 1) * hd_]
                               for j in range(ATT_GROUP)], axis=0) for g in groups]
        sg = [lax.dot_general(qs[g], ks[g], (((1,), (1,)), ((), ())), preferred_element_type=F32) + bias_ref[g]
              for g in groups]
        work.append((vs, [rows(sg[hd // ATT_GROUP], hd) for hd in heads]))
    m = [[jnp.maximum(jnp.max(s[hd], axis=-1, keepdims=True), sink_ref[hd]) for hd in heads] for _, s in work]
    e = [[jnp.exp(s[hd] - mb[hd]) for hd in heads] for (_, s), mb in zip(work, m)]
    den = [[jnp.sum(eb[hd], axis=-1, keepdims=True) + jnp.exp(sink_ref[hd] - mb[hd]) for hd in heads]
           for eb, mb in zip(e, m)]
    for blk, ((vs, _), eb, db) in enumerate(zip(work, e, den)):
        eg = [jnp.concatenate([eb[g * ATT_GROUP + j].astype(BF16) for j in range(ATT_GROUP)], axis=0)
              for g in groups]
        og = [jnp.dot(eg[g], vs[g], preferred_element_type=F32) for g in groups]
        for hd in heads:
            o_ref[blk * ATT_BLOCK:(blk + 1) * ATT_BLOCK, hd * hd_:(hd + 1) * hd_] = (
                rows(og[hd // ATT_GROUP], hd) / db[hd]).astype(BF16)


def _attention(za, sink, bsz, seqlen):
    nb = seqlen // ATT_BLOCK
    assert nb >= 2 and nb % 2 == 0
    npair = nb // 2
    za3 = za.reshape(bsz, seqlen, ZA)
    kvw = 2 * ATT_KV
    kvc = ATT_Q // kvw
    bias = _attn_bias_table()
    bias_spec = lambda pick: pl.BlockSpec((None,) + bias.shape[1:], lambda b, n: (pick(n), 0, 0, 0))
    return pl.pallas_call(
        _attn_kernel,
        grid=(bsz, npair),
        in_specs=[pl.BlockSpec(memory_space=pltpu.SMEM),
                  pl.BlockSpec((None, 2 * ATT_BLOCK, ATT_Q), lambda b, n: (b, n, 0)),
                  pl.BlockSpec((None, ATT_BLOCK, kvw), lambda b, n: (b, jnp.maximum(2 * n - 1, 0), kvc)),
                  pl.BlockSpec((None, 2 * ATT_BLOCK, kvw), lambda b, n: (b, n, kvc)),
                  pl.BlockSpec((None, ATT_BLOCK, kvw), lambda b, n: (b, jnp.minimum(2 * n + 2, nb - 1), kvc)),
                  bias_spec(lambda n: jnp.where(n == 0, 0, 1)),
                  bias_spec(lambda n: jnp.where(n == npair - 1, 2, 1))],
        out_specs=pl.BlockSpec((None, 2 * ATT_BLOCK, ATT_Q), lambda b, n: (b, n, 0)),
        out_shape=jax.ShapeDtypeStruct((bsz, seqlen, ATT_Q), BF16),
        compiler_params=_params(("parallel", "parallel")),
        name="window_attention",
    )(sink, za3, za3, za3, za3, bias, bias).reshape(bsz * seqlen, ATT_Q)


DN_HALO = SUBLANES


def _dn_prep_kernel(x_ref, xp_ref, xn_ref, cw_ref, hs_ref, g_ref, aneg_ref, dtb_ref, mf_ref, mb_ref,
                    y_ref, gb_ref, buf_ref, *, tl):
    i = pl.program_id(1)
    nt = pl.num_programs(1)
    buf_ref[0:DN_HALO, :] = jnp.where(i > 0, xp_ref[...], 0.0)
    buf_ref[DN_HALO:DN_HALO + tl, :] = x_ref[...]
    buf_ref[DN_HALO + tl:, :] = jnp.where(i < nt - 1, xn_ref[...], 0.0)
    y = (cw_ref[0:1, :] * buf_ref[DN_HALO - 1:DN_HALO - 1 + tl, :]
         + cw_ref[1:2, :] * buf_ref[DN_HALO:DN_HALO + tl, :]
         + cw_ref[2:3, :] * buf_ref[DN_HALO + 1:DN_HALO + 1 + tl, :])
    y = _silu(y)
    qk = y[:, :2 * DN_WIDTH]
    ss = jnp.dot((qk * qk).astype(BF16), hs_ref[...], preferred_element_type=F32)
    lane = lax.broadcasted_iota(jnp.int32, (tl, 2 * DN_WIDTH), 1)
    scale = jnp.where(lane < DN_WIDTH, DN_HEAD_DIM ** -0.5, 1.0)
    y_ref[:, :2 * DN_WIDTH] = qk * lax.rsqrt(ss + NORM_EPS) * scale
    y_ref[:, 2 * DN_WIDTH:] = y[:, 2 * DN_WIDTH:]
    raw = g_ref[...]
    col = lax.broadcasted_iota(jnp.int32, (tl, LANES), 1)
    is_beta = (col & DN_HEADS) == 0
    t = raw + dtb_ref[...]
    softplus = jnp.maximum(t, 0.0) + jnp.log(1.0 + jnp.exp(-jnp.abs(t)))
    vals = jnp.where(is_beta, _sigmoid(raw), aneg_ref[...] * softplus)
    v_hi = vals.astype(BF16)
    r1 = vals - v_hi.astype(F32)
    v_mid = r1.astype(BF16)
    v_lo = (r1 - v_mid.astype(F32)).astype(BF16)
    terms = jnp.concatenate([v_hi, v_mid, v_lo], axis=1)
    cf3 = jnp.dot(mf_ref[...], terms, preferred_element_type=F32)
    cb3 = jnp.dot(mb_ref[...], terms, preferred_element_type=F32)
    cf = cf3[:, :LANES] + (cf3[:, LANES:2 * LANES] + cf3[:, 2 * LANES:])
    cb = cb3[:, :LANES] + (cb3[:, LANES:2 * LANES] + cb3[:, 2 * LANES:])
    gb_ref[0] = jnp.where(is_beta, vals, cf)
    gb_ref[1] = pltpu.roll(jnp.where(is_beta, vals, cb), LANES - 2 * DN_HEADS, axis=1)


def _dn_prep(zb, gates, conv_w, hs, aneg, dtb, bsz, seqlen, tl):
    zb3 = zb.reshape(bsz, seqlen, ZB)
    g3 = gates.reshape(bsz, seqlen, LANES)
    nt = seqlen // tl
    hb = tl // DN_HALO
    ch = np.arange(tl) // DN_CHUNK
    same = ch[:, None] == ch[None, :]
    pos = np.arange(tl)
    mf = jnp.asarray((same & (pos[None, :] <= pos[:, None])).astype(np.float32), dtype=BF16)
    mb = jnp.asarray((same & (pos[None, :] >= pos[:, None])).astype(np.float32), dtype=BF16)
    fixed = lambda b, i: (0, 0)
    y, gb = pl.pallas_call(
        functools.partial(_dn_prep_kernel, tl=tl),
        grid=(bsz, nt),
        in_specs=[pl.BlockSpec((None, tl, ZB), lambda b, i: (b, i, 0)),
                  pl.BlockSpec((None, DN_HALO, ZB), lambda b, i: (b, jnp.maximum(i * hb - 1, 0), 0)),
                  pl.BlockSpec((None, DN_HALO, ZB), lambda b, i: (b, jnp.minimum((i + 1) * hb, nt * hb - 1), 0)),
                  pl.BlockSpec(conv_w.shape, fixed), pl.BlockSpec(hs.shape, fixed),
                  pl.BlockSpec((None, tl, LANES), lambda b, i: (b, i, 0)),
                  pl.BlockSpec((1, LANES), fixed), pl.BlockSpec((1, LANES), fixed),
                  pl.BlockSpec((tl, tl), fixed), pl.BlockSpec((tl, tl), fixed)],
        out_specs=[pl.BlockSpec((None, tl, ZB), lambda b, i: (b, i, 0)),
                   pl.BlockSpec((2, None, tl, LANES), lambda b, i: (0, b, i, 0))],
        out_shape=[jax.ShapeDtypeStruct((bsz, seqlen, ZB), F32),
                   jax.ShapeDtypeStruct((2, bsz, seqlen, LANES), F32)],
        scratch_shapes=[pltpu.VMEM((tl + 2 * DN_HALO, ZB), F32)],
        compiler_params=_params(("parallel", "parallel")),
        name="deltanet_prep",
    )(zb3, zb3, zb3, conv_w, hs, g3, aneg, dtb, mf, mb)
    return y, gb


def _lane_expand(cols, first):
    c = cols.shape[0]
    lane = lax.broadcasted_iota(jnp.int32, (c, LANES), 1)
    halves = []
    for h in range(0, DN_HEADS, 2):
        a = jnp.broadcast_to(cols[:, first + h:first + h + 1], (c, LANES))
        b = jnp.broadcast_to(cols[:, first + h + 1:first + h + 2], (c, LANES))
        halves.append(jnp.where(lane < DN_HEAD_DIM, a, b))
    return jnp.concatenate(halves, axis=1)


def _dn_pair_kernel(xf_ref, xb_ref, gf_ref, gb_ref, of_ref, ob_ref, sf_ref, sb_ref, *, nsub):
    c = DN_CHUNK
    w = DN_WIDTH

    @pl.when(pl.program_id(1) == 0)
    def _():
        sf_ref[...] = jnp.zeros_like(sf_ref)
        sb_ref[...] = jnp.zeros_like(sb_ref)

    r_cat = lax.broadcasted_iota(jnp.int32, (c, w), 0)
    s_cat = lax.broadcasted_iota(jnp.int32, (c, w), 1) & (DN_HEAD_DIM - 1)
    eye_cat = s_cat == r_cat
    rr = lax.broadcasted_iota(jnp.int32, (w, w), 0)
    cc = lax.broadcasted_iota(jnp.int32, (w, w), 1)
    head = (rr >> 6) == (cc >> 6)
    head_b = head.astype(BF16)
    m16 = (s_cat >> 4) == (r_cat >> 4)
    m32 = (s_cat >> 5) == (r_cat >> 5)
    off16 = m32 & jnp.logical_not(m16)
    off32 = jnp.logical_not(m32)
    eye_f = eye_cat.astype(F32)

    def bd(t):
        return jnp.concatenate([t] * DN_HEADS, axis=0) * head_b

    def mm(a, b):
        return jnp.dot(a, b, preferred_element_type=F32)

    chunks = [(0, i * c) for i in range(nsub)] + [(1, (nsub - 1 - i) * c) for i in range(nsub)]
    xrefs = (xf_ref, xb_ref)
    grefs = (gf_ref, gb_ref)
    orefs = (of_ref, ob_ref)
    srefs = (sf_ref, sb_ref)
    incl = (s_cat <= r_cat, s_cat >= r_cat)
    strict = (s_cat < r_cat, s_cat > r_cat)
    last_row = (c - 1, 0)

    pre = []
    for d, st in chunks:
        x = xrefs[d][st:st + c, :]
        q, k, v = x[:, :w], x[:, w:2 * w], x[:, 2 * w:]
        gbt = grefs[d][st:st + c, :]
        beta = _lane_expand(gbt, 0)
        gc = _lane_expand(gbt, DN_HEADS)
        grow = jnp.sum(jnp.where(eye_cat, gc, 0.0), axis=0, keepdims=True)
        decay = jnp.exp(jnp.where(incl[d], gc - grow, -jnp.inf))
        glast = gc[last_row[d]:last_row[d] + 1, :]
        egc = jnp.exp(gc)
        kb = k * beta
        pre.append(dict(d=d, st=st, q=q, k=k, kb=kb, vb=v * beta, decay=decay, glast=glast, egc=egc,
                        kdec=(k * jnp.exp(glast - gc)).astype(BF16)))

    kks = [lax.dot_general(jnp.concatenate([p["kb"], p["q"]], axis=0).astype(BF16), bd(p["k"].astype(BF16)),
                           (((1,), (1,)), ((), ())), preferred_element_type=F32) for p in pre]
    a = [jnp.where(strict[p["d"]], kk[:c] * p["decay"], 0.0) for p, kk in zip(pre, kks)]
    intra = [jnp.where(incl[p["d"]], kk[c:] * p["decay"], 0.0).astype(BF16) for p, kk in zip(pre, kks)]
    xm = [jnp.where(m16, -t, 0.0) for t in a]
    xm_b = [t.astype(BF16) for t in xm]
    x2_b = [mm(t, bd(t)).astype(BF16) for t in xm_b]
    x2_d = [bd(t) for t in x2_b]
    dinv = [eye_f + t for t in xm]
    r2 = [mm(jnp.concatenate([t.astype(BF16), p2], axis=0), s2) for t, p2, s2 in zip(dinv, x2_b, x2_d)]
    dinv = [t + r[:c] for t, r in zip(dinv, r2)]
    x4_b = [r[c:].astype(BF16) for r in r2]
    x4_d = [bd(t) for t in x4_b]
    r4 = [mm(jnp.concatenate([t.astype(BF16), p4], axis=0), s4) for t, p4, s4 in zip(dinv, x4_b, x4_d)]
    dinv = [t + r[:c] for t, r in zip(dinv, r4)]
    x8_d = [bd(r[c:].astype(BF16)) for r in r4]
    dinv = [t + mm(t.astype(BF16), s8) for t, s8 in zip(dinv, x8_d)]
    dinv_b = [t.astype(BF16) for t in dinv]
    n32 = [bd(mm(jnp.where(off16, t, 0.0).astype(BF16), bd(db)).astype(BF16)) for t, db in zip(a, dinv_b)]
    t32 = [t - mm(db, n) for t, db, n in zip(dinv, dinv_b, n32)]
    t32_b = [t.astype(BF16) for t in t32]
    n64 = [bd(mm(jnp.where(off32, t, 0.0).astype(BF16), bd(tb)).astype(BF16)) for t, tb in zip(a, t32_b)]
    t_cat = [(t - mm(tb, n)).astype(BF16) for t, tb, n in zip(t32, t32_b, n64)]
    uw = [mm(tc, jnp.concatenate([bd(p["vb"].astype(BF16)), bd((p["kb"] * p["egc"]).astype(BF16))], axis=1))
          for tc, p in zip(t_cat, pre)]
    uw_b = [t.astype(BF16) for t in uw]
    pn = [lax.dot_general(p["kdec"], t, (((0,), (0,)), ((), ())), preferred_element_type=F32)
          for p, t in zip(pre, uw_b)]
    qo = [mm(it, jnp.concatenate([bd(t[:, :w]), bd(t[:, w:])], axis=1)) for it, t in zip(intra, uw_b)]
    lhs = [jnp.concatenate([n[:, w:].astype(BF16) * head_b, (p["q"] * p["egc"] - o[:, w:]).astype(BF16)], axis=0)
           for n, o, p in zip(pn, qo, pre)]
    for step in range(nsub):
        for d in range(2):
            i = d * nsub + step
            p = pre[i]
            state = srefs[d][...]
            r = jnp.dot(lhs[i], state.astype(BF16), preferred_element_type=F32)
            orefs[d][p["st"]:p["st"] + c, :] = r[w:] + qo[i][:, :w]
            srefs[d][...] = state * jnp.exp(p["glast"]) - r[:w] + jnp.where(head, pn[i][:, :w], 0.0)


def _dn_chunk(y, gb, bsz, seqlen, ch):
    nsub = ch // DN_CHUNK
    nblk = seqlen // ch
    fwd = lambda b, j: (b, j, 0)
    bwd = lambda b, j: (b, nblk - 1 - j, 0)
    o_f, o_b = pl.pallas_call(
        functools.partial(_dn_pair_kernel, nsub=nsub),
        grid=(bsz, nblk),
        in_specs=[pl.BlockSpec((None, ch, ZB), fwd), pl.BlockSpec((None, ch, ZB), bwd),
                  pl.BlockSpec((None, None, ch, LANES), lambda b, j: (0, b, j, 0)),
                  pl.BlockSpec((None, None, ch, LANES), lambda b, j: (1, b, nblk - 1 - j, 0))],
        out_specs=[pl.BlockSpec((None, ch, DN_WIDTH), fwd), pl.BlockSpec((None, ch, DN_WIDTH), bwd)],
        out_shape=[jax.ShapeDtypeStruct((bsz, seqlen, DN_WIDTH), F32)] * 2,
        scratch_shapes=[pltpu.VMEM((DN_WIDTH, DN_WIDTH), F32)] * 2,
        compiler_params=_params(("parallel", "arbitrary")),
        name="deltanet_chunks",
    )(y, y, gb, gb)
    return o_f.reshape(bsz * seqlen, DN_WIDTH), o_b.reshape(bsz * seqlen, DN_WIDTH)


CV_HALO = 2 * SUBLANES
CV_PAD = (CONV_WIDTH - 1) // 2


def _conv_kernel(x_ref, xp_ref, xn_ref, dw_ref, bias_ref, lng_ref, lnb_ref, o_ref, buf_ref, shift_ref, *, tl):
    i = pl.program_id(1)
    nt = pl.num_programs(1)

    def glu(t):
        return t[:, :CONV_CH] * _sigmoid(t[:, CONV_CH:])

    buf_ref[0:CV_HALO, :] = jnp.where(i > 0, glu(xp_ref[...]), 0.0)
    buf_ref[CV_HALO:CV_HALO + tl, :] = glu(x_ref[...])
    buf_ref[CV_HALO + tl:, :] = jnp.where(i < nt - 1, glu(xn_ref[...]), 0.0)
    acc = jnp.zeros((tl, CONV_CH), F32) + bias_ref[...]
    first = CV_HALO - CV_PAD
    span = -(-(first + CONV_WIDTH) // SUBLANES) * SUBLANES - SUBLANES
    for sub in range(SUBLANES):
        shift_ref[...] = buf_ref[sub:sub + tl + span, :]
        for base in range(0, span + 1, SUBLANES):
            j = base + sub - first
            if 0 <= j < CONV_WIDTH:
                acc = acc + dw_ref[j:j + 1, :] * shift_ref[base:base + tl, :]
    mu = jnp.mean(acc, axis=-1, keepdims=True)
    cen = acc - mu
    var = jnp.mean(cen * cen, axis=-1, keepdims=True)
    o_ref[...] = _silu(cen * lax.rsqrt(var + NORM_EPS) * lng_ref[...] + lnb_ref[...]).astype(BF16)


def _conformer_conv(glu_in, dw, bias, lng, lnb, bsz, seqlen, tl):
    x3 = glu_in.reshape(bsz, seqlen, 2 * CONV_CH)
    nt = seqlen // tl
    hb = tl // CV_HALO
    fixed = lambda b, i: (0, 0)
    return pl.pallas_call(
        functools.partial(_conv_kernel, tl=tl),
        grid=(bsz, nt),
        in_specs=[pl.BlockSpec((None, tl, 2 * CONV_CH), lambda b, i: (b, i, 0)),
                  pl.BlockSpec((None, CV_HALO, 2 * CONV_CH), lambda b, i: (b, jnp.maximum(i * hb - 1, 0), 0)),
                  pl.BlockSpec((None, CV_HALO, 2 * CONV_CH),
                               lambda b, i: (b, jnp.minimum((i + 1) * hb, nt * hb - 1), 0)),
                  pl.BlockSpec(dw.shape, fixed), pl.BlockSpec((1, CONV_CH), fixed),
                  pl.BlockSpec((1, CONV_CH), fixed), pl.BlockSpec((1, CONV_CH), fixed)],
        out_specs=pl.BlockSpec((None, tl, CONV_CH), lambda b, i: (b, i, 0)),
        out_shape=jax.ShapeDtypeStruct((bsz, seqlen, CONV_CH), BF16),
        scratch_shapes=[pltpu.VMEM((tl + 2 * CV_HALO, CONV_CH), F32),
                        pltpu.VMEM((tl + 2 * CV_HALO - SUBLANES, CONV_CH), F32)],
        compiler_params=_params(("parallel", "parallel")),
        name="conformer_conv",
    )(x3, x3, x3, dw, bias, lng, lnb).reshape(bsz * seqlen, CONV_CH)


def _out_proj_kernel(h_ref, oa_ref, of_ref, ob_ref, zg_ref, oc_ref, og_ref, hm_ref, w_ref, gain_ref, wr_ref,
                     out_ref, xn_ref, aff_ref):
    half = h_ref.shape[0] // 2
    for r0 in (0, half):
        rs = slice(r0, r0 + half)
        ob = of_ref[rs, :] + ob_ref[rs, :]
        ms = jnp.dot((ob * ob).astype(BF16), hm_ref[...], preferred_element_type=F32)
        obn = ob * lax.rsqrt(ms + NORM_EPS) * og_ref[...]
        ob2 = obn * _silu(zg_ref[rs, :])
        mix = jnp.concatenate([oa_ref[rs, :], ob2.astype(BF16), oc_ref[rs, :]], axis=1)
        x = h_ref[rs, :] + jnp.dot(mix, w_ref[...], preferred_element_type=F32)
        out_ref[rs, :] = x
        ms = jnp.mean(x * x, axis=-1, keepdims=True)
        xn = x * lax.rsqrt(ms + NORM_EPS) * gain_ref[...]
        xn_hi = xn.astype(BF16)
        xn_ref[rs, :] = xn_hi
        xn_lo = (xn - xn_hi.astype(F32)).astype(BF16)
        logits = (jnp.dot(xn_hi, wr_ref[0], preferred_element_type=F32)
                  + (jnp.dot(xn_lo, wr_ref[0], preferred_element_type=F32)
                     + jnp.dot(xn_hi, wr_ref[1], preferred_element_type=F32)))
        lane = lax.broadcasted_iota(jnp.int32, logits.shape, 1)
        logits = jnp.where(lane < N_EXPERTS, logits, -jnp.inf)
        m = jnp.max(logits, axis=-1, keepdims=True)
        e = jnp.exp(logits - m)
        aff = e / jnp.sum(e, axis=-1, keepdims=True)
        aff_ref[:, rs] = jnp.transpose(aff)[:N_EXPERTS, :]


def _out_proj_route(h2, oa, o_f, o_b, zg, oc, og, hm, w, gain, wr2, tm):
    n = h2.shape[0]
    row = lambda i: (i, 0)
    fixed = lambda i: (0, 0)
    return pl.pallas_call(
        _out_proj_kernel,
        grid=(n // tm,),
        in_specs=[pl.BlockSpec((tm, D_MODEL), row), pl.BlockSpec((tm, ATT_Q), row),
                  pl.BlockSpec((tm, DN_WIDTH), row), pl.BlockSpec((tm, DN_WIDTH), row),
                  pl.BlockSpec((tm, DN_WIDTH), row),
                  pl.BlockSpec((tm, CONV_CH), row), pl.BlockSpec((1, DN_WIDTH), fixed),
                  pl.BlockSpec(hm.shape, fixed), pl.BlockSpec(w.shape, fixed),
                  pl.BlockSpec((1, D_MODEL), fixed), pl.BlockSpec((2, D_MODEL, LANES), lambda i: (0, 0, 0))],
        out_specs=[pl.BlockSpec((tm, D_MODEL), row), pl.BlockSpec((tm, D_MODEL), row),
                   pl.BlockSpec((N_EXPERTS, tm), lambda i: (0, i))],
        out_shape=[jax.ShapeDtypeStruct((n, D_MODEL), F32), jax.ShapeDtypeStruct((n, D_MODEL), BF16),
                   jax.ShapeDtypeStruct((N_EXPERTS, n), F32)],
        compiler_params=_params(("parallel",)),
        name="out_proj_route",
    )(h2, oa, o_f, o_b, zg, oc, og, hm, w, gain, wr2)


MOE_TILE = 256
MOE_ALIGN = 2 * SUBLANES
MOE_WIN = 64
MOE_PAD = 1024
MOE_FFN_TILE = 1024
FF_CHUNK = 256
MOE_UNSELECTED = -64.0


def _select_kernel(aff_ref, tri_ref, val_ref, cnt_ref, *, cap, tile):
    ne, n = aff_ref.shape
    nt = n // tile
    capf = float(cap)

    def bits_of(x):
        return lax.bitcast_convert_type(x, jnp.int32)

    def search(i, thr):
        cand = thr | jnp.left_shift(jnp.int32(1), 30 - i)
        cnt = jnp.sum((bits_of(aff_ref[...]) >= cand).astype(F32), axis=1, keepdims=True)
        return jnp.where(cnt >= capf, cand, thr)

    thr = lax.fori_loop(0, 31, search, jnp.zeros((ne, 1), jnp.int32))
    n_gt = jnp.sum((bits_of(aff_ref[...]) > thr).astype(F32), axis=1, keepdims=True)
    need = capf - n_gt
    lane = lax.broadcasted_iota(jnp.int32, (ne, LANES), 1)

    def tile_body(j, carry):
        eq_before, cnt_acc = carry
        off = pl.multiple_of(j * tile, tile)
        b = bits_of(aff_ref[:, pl.ds(off, tile)])
        gt = b > thr
        eqf = (b == thr).astype(F32)
        eq_rank = eq_before + jnp.dot(eqf.astype(BF16), tri_ref[...], preferred_element_type=F32)
        self_ = jnp.where(gt, 1.0, jnp.where(eq_rank <= need, eqf, 0.0))
        rank = jnp.dot(self_.astype(BF16), tri_ref[...], preferred_element_type=F32)
        val_ref[:, pl.ds(off, tile)] = jnp.where(self_ > 0.0, rank, MOE_UNSELECTED)
        cnt = jnp.sum(self_, axis=1, keepdims=True)
        return (eq_before + jnp.sum(eqf, axis=1, keepdims=True), cnt_acc + jnp.where(lane == j, cnt, 0.0))

    init = (jnp.zeros((ne, 1), F32), jnp.zeros((ne, LANES), F32))
    _, cnt_acc = lax.fori_loop(0, nt, tile_body, init)
    cnt_ref[...] = cnt_acc


def _select(aff_t, cap, tile):
    ne, n = aff_t.shape
    assert n // tile <= LANES
    tri = jnp.asarray(np.triu(np.ones((tile, tile), np.float32)), dtype=BF16)
    return pl.pallas_call(
        functools.partial(_select_kernel, cap=cap, tile=tile),
        out_shape=[jax.ShapeDtypeStruct((ne, n), F32), jax.ShapeDtypeStruct((ne, LANES), F32)],
        compiler_params=pltpu.CompilerParams(vmem_limit_bytes=VMEM_LIMIT),
        name="moe_select",
    )(aff_t, tri)


def _moe_plan(cnt, nt):
    c = cnt[:, :nt].astype(jnp.int32).T
    starts = jnp.concatenate([jnp.zeros((1, N_EXPERTS), jnp.int32), jnp.cumsum(c, axis=0)], axis=0)
    head = starts[:-1] & (MOE_ALIGN - 1)
    kmax = jnp.maximum(jnp.max((head + c + MOE_WIN - 1) // MOE_WIN, axis=1), 1).astype(jnp.int32)
    w = jnp.arange(MOE_WIN, dtype=jnp.int32)
    tgt = (w[None, None, :] + 1 - head[:, :, None]).astype(F32).reshape(nt, 1, N_EXPERTS * MOE_WIN)
    return starts.reshape(-1), kmax, tgt


def _expand_matrix():
    e = np.arange(N_EXPERTS * MOE_WIN) // MOE_WIN
    return jnp.asarray((np.arange(N_EXPERTS)[:, None] == e[None, :]).astype(np.float32), dtype=BF16)


def _slot_onehot(val_ref, eexp_ref):
    return lax.dot_general(val_ref[...].astype(BF16), eexp_ref[...], (((0,), (0,)), ((), ())),
                           preferred_element_type=F32)


def _dispatch_kernel(start_ref, kmax_ref, xn_ref, val_ref, tgt_ref, eexp_ref, xe_ref, stage, carry, sem):
    j = pl.program_id(0)
    nt = pl.num_programs(0)
    slot = lax.rem(j, 2)
    ne = N_EXPERTS

    cap = xe_ref.shape[1] - MOE_PAD

    @pl.when(j == 0)
    def _():
        carry[...] = jnp.zeros_like(carry)
        stage[0, 0:MOE_PAD, :] = jnp.zeros((MOE_PAD, D_MODEL), BF16)
        fills = [pltpu.make_async_copy(stage.at[0, pl.ds(0, MOE_PAD)], xe_ref.at[e, pl.ds(cap, MOE_PAD)], sem.at[0])
                 for e in range(ne)]
        for f in fills:
            f.start()
        for f in fills:
            f.wait()

    def window_copy(sl, e, row0):
        return pltpu.make_async_copy(stage.at[sl, pl.ds(e * MOE_WIN, MOE_WIN)],
                                     xe_ref.at[e, pl.ds(row0, MOE_WIN)], sem.at[sl])

    def wait_windows(sl):
        for e in range(ne):
            window_copy(sl, e, 0).wait()

    rep = _slot_onehot(val_ref, eexp_ref)
    xn = xn_ref[...]
    row = lax.broadcasted_iota(jnp.int32, (MOE_ALIGN, D_MODEL), 0)

    def block(k, _):
        @pl.when(k > 0)
        def _():
            wait_windows(slot)

        lo = k * MOE_WIN
        pt = (rep == tgt_ref[...] + lo.astype(F32)).astype(BF16)
        comp = lax.dot_general(pt, xn, (((0,), (0,)), ((), ())), preferred_element_type=F32)
        stage[slot] = comp.astype(BF16)
        for e in range(ne):
            s = start_ref[j * ne + e]
            head = s & (MOE_ALIGN - 1)
            r0 = e * MOE_WIN

            @pl.when(k == 0)
            def _():
                fresh = stage[slot, r0:r0 + MOE_ALIGN, :]
                kept = carry[e * MOE_ALIGN:(e + 1) * MOE_ALIGN, :]
                stage[slot, r0:r0 + MOE_ALIGN, :] = jnp.where(row < head, kept, fresh)

            nxt = (head + start_ref[(j + 1) * ne + e] - s) & (-MOE_ALIGN)

            @pl.when((nxt >= lo) & (nxt < lo + MOE_WIN))
            def _():
                off = pl.multiple_of(nxt - lo, MOE_ALIGN)
                carry[e * MOE_ALIGN:(e + 1) * MOE_ALIGN, :] = stage[slot, pl.ds(r0 + off, MOE_ALIGN), :]

        @pl.when((k == 0) & (j > 0))
        def _():
            wait_windows(1 - slot)

        for e in range(ne):
            base = pl.multiple_of((start_ref[j * ne + e] & (-MOE_ALIGN)) + lo, MOE_ALIGN)
            window_copy(slot, e, base).start()
        return 0

    lax.fori_loop(0, kmax_ref[j], block, 0)

    @pl.when(j == nt - 1)
    def _():
        wait_windows(slot)


def _dispatch(xn, val, starts, kmax, tgt, eexp, cap, tile):
    n = xn.shape[0]
    nt = n // tile
    rows = N_EXPERTS * MOE_WIN
    return pl.pallas_call(
        _dispatch_kernel,
        grid_spec=pltpu.PrefetchScalarGridSpec(
            num_scalar_prefetch=2, grid=(nt,),
            in_specs=[pl.BlockSpec((tile, D_MODEL), lambda j, s, k: (j, 0)),
                      pl.BlockSpec((N_EXPERTS, tile), lambda j, s, k: (0, j)),
                      pl.BlockSpec((None, 1, rows), lambda j, s, k: (j, 0, 0)),
                      pl.BlockSpec((N_EXPERTS, rows), lambda j, s, k: (0, 0))],
            out_specs=pl.BlockSpec(memory_space=pl.ANY),
            scratch_shapes=[pltpu.VMEM((2, rows, D_MODEL), BF16),
                            pltpu.VMEM((N_EXPERTS * MOE_ALIGN, D_MODEL), BF16),
                            pltpu.SemaphoreType.DMA((2,))]),
        out_shape=jax.ShapeDtypeStruct((N_EXPERTS, cap + MOE_PAD, D_MODEL), BF16),
        compiler_params=_params(("arbitrary",)),
        name="moe_dispatch",
    )(starts, kmax, xn, val, tgt, eexp)


def _expert_kernel(x_ref, wr_ref, wg_ref, wu_ref, wd_ref, y_ref, *, ntile):
    e = pl.program_id(0)
    i = pl.program_id(1)

    @pl.when(i < ntile)
    def _():
        x = x_ref[...]
        logits = (jnp.dot(x, wr_ref[0], preferred_element_type=F32)
                  + jnp.dot(x, wr_ref[1], preferred_element_type=F32))
        lane = lax.broadcasted_iota(jnp.int32, logits.shape, 1)
        logits = jnp.where(lane < N_EXPERTS, logits, -jnp.inf)
        ex = jnp.exp(logits - jnp.max(logits, axis=-1, keepdims=True))
        gate = (jnp.sum(jnp.where(lane == e, ex, 0.0), axis=-1, keepdims=True)
                / jnp.sum(ex, axis=-1, keepdims=True))
        hid = []
        for c0 in range(0, EXPERT_FF, FF_CHUNK):
            hg = jnp.dot(x, wg_ref[:, c0:c0 + FF_CHUNK], preferred_element_type=F32)
            hu = jnp.dot(x, wu_ref[:, c0:c0 + FF_CHUNK], preferred_element_type=F32)
            hid.append((_silu(hg) * hu).astype(BF16))
        hid = jnp.concatenate(hid, axis=1)
        y_ref[...] = (jnp.dot(hid, wd_ref[...], preferred_element_type=F32) * gate).astype(BF16)

    @pl.when(i >= ntile)
    def _():
        y_ref[...] = jnp.zeros_like(y_ref)


def _expert_ffn(xe, wr2, wg, wu, wd, layer, cap, tc):
    ne, rows, _ = xe.shape
    wspec = lambda shape: pl.BlockSpec((None, None) + shape, lambda e, i: (layer, e, 0, 0))
    return pl.pallas_call(
        functools.partial(_expert_kernel, ntile=cap // tc),
        grid=(ne, rows // tc),
        in_specs=[pl.BlockSpec((None, tc, D_MODEL), lambda e, i: (e, i, 0)),
                  pl.BlockSpec(wr2.shape, lambda e, i: (0, 0, 0)),
                  wspec((D_MODEL, EXPERT_FF)), wspec((D_MODEL, EXPERT_FF)), wspec((EXPERT_FF, D_MODEL))],
        out_specs=pl.BlockSpec((None, tc, D_MODEL), lambda e, i: (e, i, 0)),
        out_shape=jax.ShapeDtypeStruct((ne, rows, D_MODEL), BF16),
        compiler_params=_params(("parallel", "parallel")),
        name="expert_ffn",
    )(xe, wr2, wg, wu, wd)


COMBINE_SUB = 2


def _combine_kernel(start_ref, kmax_ref, h_ref, p_ref, val_ref, tgt_ref, eexp_ref, gain_ref, wg_ref, wp_ref,
                    ye_ref, out_ref, stage, sem, *, tile):
    j = pl.program_id(0)
    nstep = pl.num_programs(0)
    slot = lax.rem(j, 2)
    ne = N_EXPERTS
    subs = range(COMBINE_SUB)

    def window_copy(sl, sub, e, row0):
        buf = sl * COMBINE_SUB + sub
        return pltpu.make_async_copy(ye_ref.at[e, pl.ds(row0, MOE_WIN)],
                                     stage.at[buf, pl.ds(e * MOE_WIN, MOE_WIN)], sem.at[buf])

    def fetch(sl, sub, tile_idx, lo):
        for e in range(ne):
            base = pl.multiple_of((start_ref[tile_idx * ne + e] & (-MOE_ALIGN)) + lo, MOE_ALIGN)
            window_copy(sl, sub, e, base).start()

    def wait_windows(sl, sub):
        for e in range(ne):
            window_copy(sl, sub, e, 0).wait()

    @pl.when(j == 0)
    def _():
        for sub in subs:
            fetch(slot, sub, sub, 0)

    @pl.when(j + 1 < nstep)
    def _():
        for sub in subs:
            fetch(1 - slot, sub, (j + 1) * COMBINE_SUB + sub, 0)

    reps = [lax.dot_general(val_ref[:, sub * tile:(sub + 1) * tile].astype(BF16), eexp_ref[...],
                            (((0,), (0,)), ((), ())), preferred_element_type=F32) for sub in subs]
    pts = [(reps[sub] == tgt_ref[sub]).astype(BF16) for sub in subs]
    for sub in subs:
        wait_windows(slot, sub)
    accs = [h_ref[sub * tile:(sub + 1) * tile, :]
            + jnp.dot(pts[sub], stage[slot * COMBINE_SUB + sub], preferred_element_type=F32) for sub in subs]
    for sub in subs:
        t = j * COMBINE_SUB + sub

        def extra(k, acc, sub=sub, t=t):
            lo = k * MOE_WIN
            fetch(slot, sub, t, lo)
            wait_windows(slot, sub)
            pk = (reps[sub] == tgt_ref[sub] + lo.astype(F32)).astype(BF16)
            return acc + jnp.dot(pk, stage[slot * COMBINE_SUB + sub], preferred_element_type=F32)

        accs[sub] = lax.fori_loop(1, kmax_ref[t], extra, accs[sub])
    x = jnp.concatenate(accs, axis=0)
    ms = jnp.mean(x * x, axis=-1, keepdims=True)
    xn = (x * lax.rsqrt(ms + NORM_EPS) * gain_ref[...]).astype(BF16)
    gate = _sigmoid(jnp.dot(xn, wg_ref[...], preferred_element_type=F32))
    proj = jnp.dot(p_ref[...].astype(BF16), wp_ref[...], preferred_element_type=F32)
    out_ref[...] = x + gate * proj


def _combine_ple(h2, p3, layer, ye, val, starts, kmax, tgt, eexp, gain, wg, wp, tile):
    n = h2.shape[0]
    step = COMBINE_SUB * tile
    assert n % step == 0
    rows = N_EXPERTS * MOE_WIN
    fixed = lambda j, s, k: (0, 0)
    return pl.pallas_call(
        functools.partial(_combine_kernel, tile=tile),
        grid_spec=pltpu.PrefetchScalarGridSpec(
            num_scalar_prefetch=2, grid=(n // step,),
            in_specs=[pl.BlockSpec((step, D_MODEL), lambda j, s, k: (j, 0)),
                      pl.BlockSpec((None, step, PLE_DIM), lambda j, s, k: (layer, j, 0)),
                      pl.BlockSpec((N_EXPERTS, step), lambda j, s, k: (0, j)),
                      pl.BlockSpec((COMBINE_SUB, 1, rows), lambda j, s, k: (j, 0, 0)),
                      pl.BlockSpec((N_EXPERTS, rows), fixed),
                      pl.BlockSpec((1, D_MODEL), fixed), pl.BlockSpec(wg.shape, fixed),
                      pl.BlockSpec(wp.shape, fixed),
                      pl.BlockSpec(memory_space=pl.ANY)],
            out_specs=pl.BlockSpec((step, D_MODEL), lambda j, s, k: (j, 0)),
            scratch_shapes=[pltpu.VMEM((2 * COMBINE_SUB, rows, D_MODEL), BF16),
                            pltpu.SemaphoreType.DMA((2 * COMBINE_SUB,))]),
        out_shape=jax.ShapeDtypeStruct((n, D_MODEL), F32),
        compiler_params=_params(("arbitrary",)),
        name="moe_combine_ple",
    )(starts, kmax, h2, p3, val, tgt, eexp, gain, wg, wp, ye)


def _relayout_w_in(w_in):
    o_beta = ZA + ZB + DN_WIDTH
    o_alpha = o_beta + 2 * DN_HEADS
    o_glu = o_alpha + 2 * DN_HEADS
    pieces = [w_in[:, :o_beta], w_in[:, o_glu:o_glu + 2 * CONV_CH]]
    for d in range(2):
        pieces.append(w_in[:, o_beta + d * DN_HEADS:o_beta + (d + 1) * DN_HEADS])
        pieces.append(w_in[:, o_alpha + d * DN_HEADS:o_alpha + (d + 1) * DN_HEADS])
    pieces.append(jnp.zeros((w_in.shape[0], LANES - 4 * DN_HEADS), w_in.dtype))
    return jnp.concatenate(pieces, axis=1).astype(BF16)


def _prep_layer(lw):
    (norm_mix, w_in, q_gain, k_gain, sink, dn_conv, dn_a_log, dn_dt_bias, dn_out_gain,
     cv_dw, cv_dw_bias, cv_ln_gain, cv_ln_bias, w_out, norm_ffn, w_router, w_gate, w_up, w_down,
     norm_ple, w_ple_gate, w_ple_proj) = lw
    w_perm = _relayout_w_in(w_in)
    hgain = jnp.concatenate([jnp.tile(q_gain, ATT_HEADS) * (ATT_HEAD_DIM ** -0.5),
                             jnp.tile(k_gain, ATT_KV_HEADS)]).reshape(1, -1)
    zeros4 = jnp.zeros((DN_HEADS,), F32)
    aneg = -jnp.exp(dn_a_log.astype(F32))
    aneg_row = jnp.concatenate([zeros4, aneg[0], zeros4, aneg[1]])
    dtb_row = jnp.concatenate([zeros4, dn_dt_bias[0], zeros4, dn_dt_bias[1]])
    pad = lambda r: jnp.pad(r, (0, LANES - r.shape[0])).reshape(1, LANES)
    wr = jnp.pad(w_router.astype(F32), ((0, 0), (0, LANES - N_EXPERTS)))
    wr_hi = wr.astype(BF16)
    wr2 = jnp.stack([wr_hi, (wr - wr_hi.astype(F32)).astype(BF16)])
    return dict(
        w_router2=wr2,
        norm_mix=norm_mix.reshape(1, -1), w_in=w_perm, hgain=hgain, sink=sink.astype(F32),
        dn_conv=dn_conv, aneg=pad(aneg_row), dtb=pad(dtb_row),
        dn_out_gain=jnp.tile(dn_out_gain, DN_HEADS).reshape(1, -1),
        cv_dw=cv_dw, cv_dw_bias=cv_dw_bias.reshape(1, -1), cv_ln_gain=cv_ln_gain.reshape(1, -1),
        cv_ln_bias=cv_ln_bias.reshape(1, -1), w_out=w_out.astype(BF16),
        norm_ffn=norm_ffn.reshape(1, -1),
        norm_ple=norm_ple.reshape(1, -1), w_ple_gate=w_ple_gate.astype(BF16), w_ple_proj=w_ple_proj.astype(BF16))


def _tiles(bsz, seqlen):
    n = bsz * seqlen
    return dict(tm=min(512, n), tmo=min(1024, n), tl=min(256, seqlen), ch=min(256, seqlen), tcv=min(1024, seqlen))


def _moe_ple(h2, xn, aff_t, p3, layer, pw):
    n = h2.shape[0]
    cap = CAPACITY_FACTOR * n // N_EXPERTS
    tile = min(MOE_TILE, n)
    val, cnt = _select(aff_t, cap, tile)
    starts, kmax, tgt = _moe_plan(cnt, n // tile)
    eexp = _expand_matrix()
    xe = _dispatch(xn, val, starts, kmax, tgt, eexp, cap, tile)
    ye = _expert_ffn(xe, pw["w_router2"], pw["w_gate"], pw["w_up"], pw["w_down"], layer, cap,
                     min(MOE_FFN_TILE, cap))
    return _combine_ple(h2, p3, layer, ye, val, starts, kmax, tgt, eexp, pw["norm_ple"], pw["w_ple_gate"],
                        pw["w_ple_proj"], tile)


def _layer(h2, p3, layer, pw, bsz, seqlen):
    t = _tiles(bsz, seqlen)
    hm_att = _head_mean_matrix(ATT_Q + ATT_KV, ATT_HEAD_DIM)
    hs_dn = _head_sum_matrix(2 * DN_WIDTH, DN_HEAD_DIM)
    hm_dn = _head_mean_matrix(DN_WIDTH, DN_HEAD_DIM)
    za, zb, zg, glu_in, gates = _in_proj(h2, pw["norm_mix"], pw["w_in"], hm_att, pw["hgain"], t["tm"])
    o_a = _attention(za, pw["sink"], bsz, seqlen)
    y, gb = _dn_prep(zb, gates, pw["dn_conv"], hs_dn, pw["aneg"], pw["dtb"], bsz, seqlen, t["tl"])
    o_f, o_b = _dn_chunk(y, gb, bsz, seqlen, t["ch"])
    o_c = _conformer_conv(glu_in, pw["cv_dw"], pw["cv_dw_bias"], pw["cv_ln_gain"], pw["cv_ln_bias"],
                          bsz, seqlen, t["tcv"])
    h2, xn, aff_t = _out_proj_route(h2, o_a, o_f, o_b, zg, o_c, pw["dn_out_gain"], hm_dn, pw["w_out"],
                                    pw["norm_ffn"], pw["w_router2"], t["tmo"])
    return _moe_ple(h2, xn, aff_t, p3, layer, pw)


def _trunk(x, p, layer_weights):
    bsz, seqlen, _ = x.shape
    h2 = x.reshape(bsz * seqlen, D_MODEL)
    p3 = p.reshape(p.shape[0], bsz * seqlen, PLE_DIM)
    for i, pw in enumerate(layer_weights):
        h2 = _layer(h2, p3, i, pw, bsz, seqlen)
    return h2.reshape(bsz, seqlen, D_MODEL)


def kernel(x_prompt, x_sample, p_prompt, p_sample, norm_mix, w_in, q_gain, k_gain, sink, dn_conv, dn_a_log,
           dn_dt_bias, dn_out_gain, cv_dw, cv_dw_bias, cv_ln_gain, cv_ln_bias, w_out, norm_ffn, w_router,
           w_gate, w_up, w_down, norm_ple, w_ple_gate, w_ple_proj):
    weights = (norm_mix, w_in, q_gain, k_gain, sink, dn_conv, dn_a_log, dn_dt_bias, dn_out_gain,
               cv_dw, cv_dw_bias, cv_ln_gain, cv_ln_bias, w_out, norm_ffn, w_router, w_gate, w_up, w_down,
               norm_ple, w_ple_gate, w_ple_proj)
    depth = w_in.shape[0]
    experts = dict(w_gate=w_gate.astype(BF16), w_up=w_up.astype(BF16), w_down=w_down.astype(BF16))
    layer_weights = [dict(_prep_layer([w[i] for w in weights]), **experts) for i in range(depth)]
    return (_trunk(x_prompt, p_prompt, layer_weights), _trunk(x_sample, p_sample, layer_weights))
```

```python
import functools
import math

import numpy as np
import jax
import jax.numpy as jnp
from jax import lax
from jax.experimental import pallas as pl
from jax.experimental.pallas import tpu as pltpu

F32 = jnp.float32
BF16 = jnp.bfloat16

D_MODEL = 1024
ATT_HEADS = 8
ATT_KV_HEADS = 2
ATT_HEAD_DIM = 64
ATT_GROUP = ATT_HEADS // ATT_KV_HEADS
WINDOW = 128
ATT_BLOCK = 128
DN_HEADS = 4
DN_HEAD_DIM = 64
DN_WIDTH = DN_HEADS * DN_HEAD_DIM
DN_CHUNK = 64
CONV_CH = 256
CONV_WIDTH = 31
ATT_Q = ATT_HEADS * ATT_HEAD_DIM
ATT_KV = ATT_KV_HEADS * ATT_HEAD_DIM
N_EXPERTS = 16
CAPACITY_FACTOR = 2
EXPERT_FF = 1024
PLE_DIM = 256
NORM_EPS = 1e-6

LANES = 128
SUBLANES = 8
VMEM_LIMIT = 48 * 1024 * 1024

ZA = ATT_Q + 2 * ATT_KV
ZB = 3 * DN_WIDTH
ZW = ZA + ZB + DN_WIDTH + 2 * CONV_CH + LANES


def _params(sem):
    return pltpu.CompilerParams(dimension_semantics=sem, vmem_limit_bytes=VMEM_LIMIT)


def _head_mean_matrix(width, head):
    idx = np.arange(width) // head
    return jnp.asarray((idx[:, None] == idx[None, :]).astype(np.float32) / head, dtype=BF16)


def _head_sum_matrix(width, head):
    idx = np.arange(width) // head
    return jnp.asarray((idx[:, None] == idx[None, :]).astype(np.float32), dtype=BF16)


def _sigmoid(x):
    return 1.0 / (1.0 + jnp.exp(-x))


def _silu(x):
    return x * _sigmoid(x)


def _in_proj_kernel(x_ref, gain_ref, w_ref, hm_ref, hgain_ref, za_ref, zb_ref, zg_ref, glu_ref, gates_ref):
    x = x_ref[...]
    ms = jnp.mean(x * x, axis=-1, keepdims=True)
    a = (x * lax.rsqrt(ms + NORM_EPS) * gain_ref[...]).astype(BF16)
    z = jnp.dot(a, w_ref[...], preferred_element_type=F32)
    nqk = ATT_Q + ATT_KV
    qk = z[:, :nqk]
    hms = jnp.dot((qk * qk).astype(BF16), hm_ref[...], preferred_element_type=F32)
    za_ref[:, :nqk] = (qk * lax.rsqrt(hms + NORM_EPS) * hgain_ref[...]).astype(BF16)
    za_ref[:, nqk:] = z[:, nqk:ZA].astype(BF16)
    zb_ref[...] = z[:, ZA:ZA + ZB]
    zg_ref[...] = z[:, ZA + ZB:ZA + ZB + DN_WIDTH]
    glu_ref[...] = z[:, ZA + ZB + DN_WIDTH:ZA + ZB + DN_WIDTH + 2 * CONV_CH]
    gates_ref[...] = z[:, ZW - LANES:]


def _in_proj(h2, gain, w_perm, hm, hgain, tm):
    n = h2.shape[0]
    row = lambda i: (i, 0)
    fixed = lambda i: (0, 0)
    return pl.pallas_call(
        _in_proj_kernel,
        grid=(n // tm,),
        in_specs=[pl.BlockSpec((tm, D_MODEL), row), pl.BlockSpec((1, D_MODEL), fixed),
                  pl.BlockSpec((D_MODEL, ZW), fixed), pl.BlockSpec(hm.shape, fixed),
                  pl.BlockSpec(hgain.shape, fixed)],
        out_specs=[pl.BlockSpec((tm, ZA), row), pl.BlockSpec((tm, ZB), row), pl.BlockSpec((tm, DN_WIDTH), row),
                   pl.BlockSpec((tm, 2 * CONV_CH), row), pl.BlockSpec((tm, LANES), row)],
        out_shape=[jax.ShapeDtypeStruct((n, ZA), BF16), jax.ShapeDtypeStruct((n, ZB), F32),
                   jax.ShapeDtypeStruct((n, DN_WIDTH), F32), jax.ShapeDtypeStruct((n, 2 * CONV_CH), F32),
                   jax.ShapeDtypeStruct((n, LANES), F32)],
        compiler_params=_params(("parallel",)),
        name="in_proj",
    )(h2, gain, w_perm, hm, hgain)


ATT_MASKED = -1e30


def _attn_bias_table():
    i = np.arange(ATT_BLOCK)[:, None]
    c = np.arange(3 * ATT_BLOCK)[None, :]
    rel = c - ATT_BLOCK - i
    slopes = 2.0 ** (-8.0 * np.arange(1, ATT_HEADS + 1) / ATT_HEADS)
    table = np.empty((3, ATT_KV_HEADS, ATT_GROUP * ATT_BLOCK, 3 * ATT_BLOCK), np.float32)
    for variant in range(3):
        ok = np.abs(rel) <= WINDOW
        if variant == 0:
            ok = ok & (c >= ATT_BLOCK)
        if variant == 2:
            ok = ok & (c < 2 * ATT_BLOCK)
        for hd in range(ATT_HEADS):
            g, j = divmod(hd, ATT_GROUP)
            table[variant, g, j * ATT_BLOCK:(j + 1) * ATT_BLOCK] = np.where(ok, -slopes[hd] * np.abs(rel), ATT_MASKED)
    return jnp.asarray(table)


def _attn_kernel(sink_ref, q_ref, kvp_ref, kvo_ref, kvn_ref, bias_a_ref, bias_b_ref, o_ref):
    kv = jnp.concatenate([kvp_ref[...], kvo_ref[...], kvn_ref[...]], axis=0)
    hd_ = ATT_HEAD_DIM
    groups = range(ATT_KV_HEADS)
    heads = range(ATT_HEADS)
    rows = lambda t, hd: t[(hd % ATT_GROUP) * ATT_BLOCK:(hd % ATT_GROUP + 1) * ATT_BLOCK]
    work = []
    for blk, bias_ref in enumerate((bias_a_ref, bias_b_ref)):
        keys = kv[blk * ATT_BLOCK:(blk + 3) * ATT_BLOCK]
        ks = [keys[:, g * hd_:(g + 1) * hd_] for g in groups]
        vs = [keys[:, ATT_KV + g * hd_:ATT_KV + (g + 1) * hd_] for g in groups]
        q = q_ref[blk * ATT_BLOCK:(blk + 1) * ATT_BLOCK, :]
        qs = [jnp.concatenate([q[:, (g * ATT_GROUP + j) * hd_:(g * ATT_GROUP + j + 1) * hd_]
                               for j in range(ATT_GROUP)], axis=0) for g in groups]
        sg = [lax.dot_general(qs[g], ks[g], (((1,), (1,)), ((), ())), preferred_element_type=F32) + bias_ref[g]
              for g in groups]
        work.append((vs, [rows(sg[hd // ATT_GROUP], hd) for hd in heads]))
    m = [[jnp.maximum(jnp.max(s[hd], axis=-1, keepdims=True), sink_ref[hd]) for hd in heads] for _, s in work]
    e = [[jnp.exp(s[hd] - mb[hd]) for hd in heads] for (_, s), mb in zip(work, m)]
    den = [[jnp.sum(eb[hd], axis=-1, keepdims=True) + jnp.exp(sink_ref[hd] - mb[hd]) for hd in heads]
           for eb, mb in zip(e, m)]
    for blk, ((vs, _), eb, db) in enumerate(zip(work, e, den)):
        eg = [jnp.concatenate([eb[g * ATT_GROUP + j].astype(BF16) for j in range(ATT_GROUP)], axis=0)
              for g in groups]
        og = [jnp.dot(eg[g], vs[g], preferred_element_type=F32) for g in groups]
        for hd in heads:
            o_ref[blk * ATT_BLOCK:(blk + 1) * ATT_BLOCK, hd * hd_:(hd + 1) * hd_] = (
                rows(og[hd // ATT_GROUP], hd) / db[hd]).astype(BF16)


def _attention(za, sink, bsz, seqlen):
    nb = seqlen // ATT_BLOCK
    assert nb >= 2 and nb % 2 == 0
    npair = nb // 2
    za3 = za.reshape(bsz, seqlen, ZA)
    kvw = 2 * ATT_KV
    kvc = ATT_Q // kvw
    bias = _attn_bias_table()
    bias_spec = lambda pick: pl.BlockSpec((None,) + bias.shape[1:], lambda b, n: (pick(n), 0, 0, 0))
    return pl.pallas_call(
        _attn_kernel,
        grid=(bsz, npair),
        in_specs=[pl.BlockSpec(memory_space=pltpu.SMEM),
                  pl.BlockSpec((None, 2 * ATT_BLOCK, ATT_Q), lambda b, n: (b, n, 0)),
                  pl.BlockSpec((None, ATT_BLOCK, kvw), lambda b, n: (b, jnp.maximum(2 * n - 1, 0), kvc)),
                  pl.BlockSpec((None, 2 * ATT_BLOCK, kvw), lambda b, n: (b, n, kvc)),
                  pl.BlockSpec((None, ATT_BLOCK, kvw), lambda b, n: (b, jnp.minimum(2 * n + 2, nb - 1), kvc)),
                  bias_spec(lambda n: jnp.where(n == 0, 0, 1)),
                  bias_spec(lambda n: jnp.where(n == npair - 1, 2, 1))],
        out_specs=pl.BlockSpec((None, 2 * ATT_BLOCK, ATT_Q), lambda b, n: (b, n, 0)),
        out_shape=jax.ShapeDtypeStruct((bsz, seqlen, ATT_Q), BF16),
        compiler_params=_params(("parallel", "parallel")),
        name="window_attention",
    )(sink, za3, za3, za3, za3, bias, bias).reshape(bsz * seqlen, ATT_Q)


DN_HALO = SUBLANES


def _dn_prep_kernel(x_ref, xp_ref, xn_ref, cw_ref, hs_ref, g_ref, aneg_ref, dtb_ref, mf_ref, mb_ref,
                    y_ref, gb_ref, buf_ref, *, tl):
    i = pl.program_id(1)
    nt = pl.num_programs(1)
    buf_ref[0:DN_HALO, :] = jnp.where(i > 0, xp_ref[...], 0.0)
    buf_ref[DN_HALO:DN_HALO + tl, :] = x_ref[...]
    buf_ref[DN_HALO + tl:, :] = jnp.where(i < nt - 1, xn_ref[...], 0.0)
    y = (cw_ref[0:1, :] * buf_ref[DN_HALO - 1:DN_HALO - 1 + tl, :]
         + cw_ref[1:2, :] * buf_ref[DN_HALO:DN_HALO + tl, :]
         + cw_ref[2:3, :] * buf_ref[DN_HALO + 1:DN_HALO + 1 + tl, :])
    y = _silu(y)
    qk = y[:, :2 * DN_WIDTH]
    ss = jnp.dot((qk * qk).astype(BF16), hs_ref[...], preferred_element_type=F32)
    lane = lax.broadcasted_iota(jnp.int32, (tl, 2 * DN_WIDTH), 1)
    scale = jnp.where(lane < DN_WIDTH, DN_HEAD_DIM ** -0.5, 1.0)
    y_ref[:, :2 * DN_WIDTH] = qk * lax.rsqrt(ss + NORM_EPS) * scale
    y_ref[:, 2 * DN_WIDTH:] = y[:, 2 * DN_WIDTH:]
    raw = g_ref[...]
    col = lax.broadcasted_iota(jnp.int32, (tl, LANES), 1)
    is_beta = (col & DN_HEADS) == 0
    t = raw + dtb_ref[...]
    softplus = jnp.maximum(t, 0.0) + jnp.log(1.0 + jnp.exp(-jnp.abs(t)))
    vals = jnp.where(is_beta, _sigmoid(raw), aneg_ref[...] * softplus)
    v_hi = vals.astype(BF16)
    r1 = vals - v_hi.astype(F32)
    v_mid = r1.astype(BF16)
    v_lo = (r1 - v_mid.astype(F32)).astype(BF16)
    terms = jnp.concatenate([v_hi, v_mid, v_lo], axis=1)
    cf3 = jnp.dot(mf_ref[...], terms, preferred_element_type=F32)
    cb3 = jnp.dot(mb_ref[...], terms, preferred_element_type=F32)
    cf = cf3[:, :LANES] + (cf3[:, LANES:2 * LANES] + cf3[:, 2 * LANES:])
    cb = cb3[:, :LANES] + (cb3[:, LANES:2 * LANES] + cb3[:, 2 * LANES:])
    gb_ref[0] = jnp.where(is_beta, vals, cf)
    gb_ref[1] = pltpu.roll(jnp.where(is_beta, vals, cb), LANES - 2 * DN_HEADS, axis=1)


def _dn_prep(zb, gates, conv_w, hs, aneg, dtb, bsz, seqlen, tl):
    zb3 = zb.reshape(bsz, seqlen, ZB)
    g3 = gates.reshape(bsz, seqlen, LANES)
    nt = seqlen // tl
    hb = tl // DN_HALO
    ch = np.arange(tl) // DN_CHUNK
    same = ch[:, None] == ch[None, :]
    pos = np.arange(tl)
    mf = jnp.asarray((same & (pos[None, :] <= pos[:, None])).astype(np.float32), dtype=BF16)
    mb = jnp.asarray((same & (pos[None, :] >= pos[:, None])).astype(np.float32), dtype=BF16)
    fixed = lambda b, i: (0, 0)
    y, gb = pl.pallas_call(
        functools.partial(_dn_prep_kernel, tl=tl),
        grid=(bsz, nt),
        in_specs=[pl.BlockSpec((None, tl, ZB), lambda b, i: (b, i, 0)),
                  pl.BlockSpec((None, DN_HALO, ZB), lambda b, i: (b, jnp.maximum(i * hb - 1, 0), 0)),
                  pl.BlockSpec((None, DN_HALO, ZB), lambda b, i: (b, jnp.minimum((i + 1) * hb, nt * hb - 1), 0)),
                  pl.BlockSpec(conv_w.shape, fixed), pl.BlockSpec(hs.shape, fixed),
                  pl.BlockSpec((None, tl, LANES), lambda b, i: (b, i, 0)),
                  pl.BlockSpec((1, LANES), fixed), pl.BlockSpec((1, LANES), fixed),
                  pl.BlockSpec((tl, tl), fixed), pl.BlockSpec((tl, tl), fixed)],
        out_specs=[pl.BlockSpec((None, tl, ZB), lambda b, i: (b, i, 0)),
                   pl.BlockSpec((2, None, tl, LANES), lambda b, i: (0, b, i, 0))],
        out_shape=[jax.ShapeDtypeStruct((bsz, seqlen, ZB), F32),
                   jax.ShapeDtypeStruct((2, bsz, seqlen, LANES), F32)],
        scratch_shapes=[pltpu.VMEM((tl + 2 * DN_HALO, ZB), F32)],
        compiler_params=_params(("parallel", "parallel")),
        name="deltanet_prep",
    )(zb3, zb3, zb3, conv_w, hs, g3, aneg, dtb, mf, mb)
    return y, gb


def _lane_expand(cols, first):
    c = cols.shape[0]
    lane = lax.broadcasted_iota(jnp.int32, (c, LANES), 1)
    halves = []
    for h in range(0, DN_HEADS, 2):
        a = jnp.broadcast_to(cols[:, first + h:first + h + 1], (c, LANES))
        b = jnp.broadcast_to(cols[:, first + h + 1:first + h + 2], (c, LANES))
        halves.append(jnp.where(lane < DN_HEAD_DIM, a, b))
    return jnp.concatenate(halves, axis=1)


def _dn_pair_kernel(xf_ref, xb_ref, gf_ref, gb_ref, of_ref, ob_ref, sf_ref, sb_ref, *, nsub):
    c = DN_CHUNK
    w = DN_WIDTH

    @pl.when(pl.program_id(1) == 0)
    def _():
        sf_ref[...] = jnp.zeros_like(sf_ref)
        sb_ref[...] = jnp.zeros_like(sb_ref)

    r_cat = lax.broadcasted_iota(jnp.int32, (c, w), 0)
    s_cat = lax.broadcasted_iota(jnp.int32, (c, w), 1) & (DN_HEAD_DIM - 1)
    eye_cat = s_cat == r_cat
    rr = lax.broadcasted_iota(jnp.int32, (w, w), 0)
    cc = lax.broadcasted_iota(jnp.int32, (w, w), 1)
    head = (rr >> 6) == (cc >> 6)
    head_b = head.astype(BF16)
    m16 = (s_cat >> 4) == (r_cat >> 4)
    m32 = (s_cat >> 5) == (r_cat >> 5)
    off16 = m32 & jnp.logical_not(m16)
    off32 = jnp.logical_not(m32)
    eye_f = eye_cat.astype(F32)

    def bd(t):
        return jnp.concatenate([t] * DN_HEADS, axis=0) * head_b

    def mm(a, b):
        return jnp.dot(a, b, preferred_element_type=F32)

    chunks = [(0, i * c) for i in range(nsub)] + [(1, (nsub - 1 - i) * c) for i in range(nsub)]
    xrefs = (xf_ref, xb_ref)
    grefs = (gf_ref, gb_ref)
    orefs = (of_ref, ob_ref)
    srefs = (sf_ref, sb_ref)
    incl = (s_cat <= r_cat, s_cat >= r_cat)
    strict = (s_cat < r_cat, s_cat > r_cat)
    last_row = (c - 1, 0)

    pre = []
    for d, st in chunks:
        x = xrefs[d][st:st + c, :]
        q, k, v = x[:, :w], x[:, w:2 * w], x[:, 2 * w:]
        gbt = grefs[d][st:st + c, :]
        beta = _lane_expand(gbt, 0)
        gc = _lane_expand(gbt, DN_HEADS)
        grow = jnp.sum(jnp.where(eye_cat, gc, 0.0), axis=0, keepdims=True)
        decay = jnp.exp(jnp.where(incl[d], gc - grow, -jnp.inf))
        glast = gc[last_row[d]:last_row[d] + 1, :]
        egc = jnp.exp(gc)
        kb = k * beta
        pre.append(dict(d=d, st=st, q=q, k=k, kb=kb, vb=v * beta, decay=decay, glast=glast, egc=egc,
                        kdec=(k * jnp.exp(glast - gc)).astype(BF16)))

    kks = [lax.dot_general(jnp.concatenate([p["kb"], p["q"]], axis=0).astype(BF16), bd(p["k"].astype(BF16)),
                           (((1,), (1,)), ((), ())), preferred_element_type=F32) for p in pre]
    a = [jnp.where(strict[p["d"]], kk[:c] * p["decay"], 0.0) for p, kk in zip(pre, kks)]
    intra = [jnp.where(incl[p["d"]], kk[c:] * p["decay"], 0.0).astype(BF16) for p, kk in zip(pre, kks)]
    xm = [jnp.where(m16, -t, 0.0) for t in a]
    xm_b = [t.astype(BF16) for t in xm]
    x2_b = [mm(t, bd(t)).astype(BF16) for t in xm_b]
    x2_d = [bd(t) for t in x2_b]
    dinv = [eye_f + t for t in xm]
    r2 = [mm(jnp.concatenate([t.astype(BF16), p2], axis=0), s2) for t, p2, s2 in zip(dinv, x2_b, x2_d)]
    dinv = [t + r[:c] for t, r in zip(dinv, r2)]
    x4_b = [r[c:].astype(BF16) for r in r2]
    x4_d = [bd(t) for t in x4_b]
    r4 = [mm(jnp.concatenate([t.astype(BF16), p4], axis=0), s4) for t, p4, s4 in zip(dinv, x4_b, x4_d)]
    dinv = [t + r[:c] for t, r in zip(dinv, r4)]
    x8_d = [bd(r[c:].astype(BF16)) for r in r4]
    dinv = [t + mm(t.astype(BF16), s8) for t, s8 in zip(dinv, x8_d)]
    dinv_b = [t.astype(BF16) for t in dinv]
    n32 = [bd(mm(jnp.where(off16, t, 0.0).astype(BF16), bd(db)).astype(BF16)) for t, db in zip(a, dinv_b)]
    t32 = [t - mm(db, n) for t, db, n in zip(dinv, dinv_b, n32)]
    t32_b = [t.astype(BF16) for t in t32]
    n64 = [bd(mm(jnp.where(off32, t, 0.0).astype(BF16), bd(tb)).astype(BF16)) for t, tb in zip(a, t32_b)]
    t_cat = [(t - mm(tb, n)).astype(BF16) for t, tb, n in zip(t32, t32_b, n64)]
    uw = [mm(tc, jnp.concatenate([bd(p["vb"].astype(BF16)), bd((p["kb"] * p["egc"]).astype(BF16))], axis=1))
          for tc, p in zip(t_cat, pre)]
    uw_b = [t.astype(BF16) for t in uw]
    pn = [lax.dot_general(p["kdec"], t, (((0,), (0,)), ((), ())), preferred_element_type=F32)
          for p, t in zip(pre, uw_b)]
    qo = [mm(it, jnp.concatenate([bd(t[:, :w]), bd(t[:, w:])], axis=1)) for it, t in zip(intra, uw_b)]
    lhs = [jnp.concatenate([n[:, w:].astype(BF16) * head_b, (p["q"] * p["egc"] - o[:, w:]).astype(BF16)], axis=0)
           for n, o, p in zip(pn, qo, pre)]
    for step in range(nsub):
        for d in range(2):
            i = d * nsub + step
            p = pre[i]
            state = srefs[d][...]
            r = jnp.dot(lhs[i], state.astype(BF16), preferred_element_type=F32)
            orefs[d][p["st"]:p["st"] + c, :] = r[w:] + qo[i][:, :w]
            srefs[d][...] = state * jnp.exp(p["glast"]) - r[:w] + jnp.where(head, pn[i][:, :w], 0.0)


def _dn_chunk(y, gb, bsz, seqlen, ch):
    nsub = ch // DN_CHUNK
    nblk = seqlen // ch
    fwd = lambda b, j: (b, j, 0)
    bwd = lambda b, j: (b, nblk - 1 - j, 0)
    o_f, o_b = pl.pallas_call(
        functools.partial(_dn_pair_kernel, nsub=nsub),
        grid=(bsz, nblk),
        in_specs=[pl.BlockSpec((None, ch, ZB), fwd), pl.BlockSpec((None, ch, ZB), bwd),
                  pl.BlockSpec((None, None, ch, LANES), lambda b, j: (0, b, j, 0)),
                  pl.BlockSpec((None, None, ch, LANES), lambda b, j: (1, b, nblk - 1 - j, 0))],
        out_specs=[pl.BlockSpec((None, ch, DN_WIDTH), fwd), pl.BlockSpec((None, ch, DN_WIDTH), bwd)],
        out_shape=[jax.ShapeDtypeStruct((bsz, seqlen, DN_WIDTH), F32)] * 2,
        scratch_shapes=[pltpu.VMEM((DN_WIDTH, DN_WIDTH), F32)] * 2,
        compiler_params=_params(("parallel", "arbitrary")),
        name="deltanet_chunks",
    )(y, y, gb, gb)
    return o_f.reshape(bsz * seqlen, DN_WIDTH), o_b.reshape(bsz * seqlen, DN_WIDTH)


CV_HALO = 2 * SUBLANES
CV_PAD = (CONV_WIDTH - 1) // 2


def _conv_kernel(x_ref, xp_ref, xn_ref, dw_ref, bias_ref, lng_ref, lnb_ref, o_ref, buf_ref, shift_ref, *, tl):
    i = pl.program_id(1)
    nt = pl.num_programs(1)

    def glu(t):
        return t[:, :CONV_CH] * _sigmoid(t[:, CONV_CH:])

    buf_ref[0:CV_HALO, :] = jnp.where(i > 0, glu(xp_ref[...]), 0.0)
    buf_ref[CV_HALO:CV_HALO + tl, :] = glu(x_ref[...])
    buf_ref[CV_HALO + tl:, :] = jnp.where(i < nt - 1, glu(xn_ref[...]), 0.0)
    acc = jnp.zeros((tl, CONV_CH), F32) + bias_ref[...]
    first = CV_HALO - CV_PAD
    span = -(-(first + CONV_WIDTH) // SUBLANES) * SUBLANES - SUBLANES
    for sub in range(SUBLANES):
        shift_ref[...] = buf_ref[sub:sub + tl + span, :]
        for base in range(0, span + 1, SUBLANES):
            j = base + sub - first
            if 0 <= j < CONV_WIDTH:
                acc = acc + dw_ref[j:j + 1, :] * shift_ref[base:base + tl, :]
    mu = jnp.mean(acc, axis=-1, keepdims=True)
    cen = acc - mu
    var = jnp.mean(cen * cen, axis=-1, keepdims=True)
    o_ref[...] = _silu(cen * lax.rsqrt(var + NORM_EPS) * lng_ref[...] + lnb_ref[...]).astype(BF16)


def _conformer_conv(glu_in, dw, bias, lng, lnb, bsz, seqlen, tl):
    x3 = glu_in.reshape(bsz, seqlen, 2 * CONV_CH)
    nt = seqlen // tl
    hb = tl // CV_HALO
    fixed = lambda b, i: (0, 0)
    return pl.pallas_call(
        functools.partial(_conv_kernel, tl=tl),
        grid=(bsz, nt),
        in_specs=[pl.BlockSpec((None, tl, 2 * CONV_CH), lambda b, i: (b, i, 0)),
                  pl.BlockSpec((None, CV_HALO, 2 * CONV_CH), lambda b, i: (b, jnp.maximum(i * hb - 1, 0), 0)),
                  pl.BlockSpec((None, CV_HALO, 2 * CONV_CH),
                               lambda b, i: (b, jnp.minimum((i + 1) * hb, nt * hb - 1), 0)),
                  pl.BlockSpec(dw.shape, fixed), pl.BlockSpec((1, CONV_CH), fixed),
                  pl.BlockSpec((1, CONV_CH), fixed), pl.BlockSpec((1, CONV_CH), fixed)],
        out_specs=pl.BlockSpec((None, tl, CONV_CH), lambda b, i: (b, i, 0)),
        out_shape=jax.ShapeDtypeStruct((bsz, seqlen, CONV_CH), BF16),
        scratch_shapes=[pltpu.VMEM((tl + 2 * CV_HALO, CONV_CH), F32),
                        pltpu.VMEM((tl + 2 * CV_HALO - SUBLANES, CONV_CH), F32)],
        compiler_params=_params(("parallel", "parallel")),
        name="conformer_conv",
    )(x3, x3, x3, dw, bias, lng, lnb).reshape(bsz * seqlen, CONV_CH)


def _out_proj_kernel(h_ref, oa_ref, of_ref, ob_ref, zg_ref, oc_ref, og_ref, hm_ref, w_ref, gain_ref, wr_ref,
                     out_ref, xn_ref, aff_ref):
    half = h_ref.shape[0] // 2
    for r0 in (0, half):
        rs = slice(r0, r0 + half)
        ob = of_ref[rs, :] + ob_ref[rs, :]
        ms = jnp.dot((ob * ob).astype(BF16), hm_ref[...], preferred_element_type=F32)
        obn = ob * lax.rsqrt(ms + NORM_EPS) * og_ref[...]
        ob2 = obn * _silu(zg_ref[rs, :])
        mix = jnp.concatenate([oa_ref[rs, :], ob2.astype(BF16), oc_ref[rs, :]], axis=1)
        x = h_ref[rs, :] + jnp.dot(mix, w_ref[...], preferred_element_type=F32)
        out_ref[rs, :] = x
        ms = jnp.mean(x * x, axis=-1, keepdims=True)
        xn = x * lax.rsqrt(ms + NORM_EPS) * gain_ref[...]
        xn_hi = xn.astype(BF16)
        xn_ref[rs, :] = xn_hi
        xn_lo = (xn - xn_hi.astype(F32)).astype(BF16)
        logits = (jnp.dot(xn_hi, wr_ref[0], preferred_element_type=F32)
                  + (jnp.dot(xn_lo, wr_ref[0], preferred_element_type=F32)
                     + jnp.dot(xn_hi, wr_ref[1], preferred_element_type=F32)))
        lane = lax.broadcasted_iota(jnp.int32, logits.shape, 1)
        logits = jnp.where(lane < N_EXPERTS, logits, -jnp.inf)
        m = jnp.max(logits, axis=-1, keepdims=True)
        e = jnp.exp(logits - m)
        aff = e / jnp.sum(e, axis=-1, keepdims=True)
        aff_ref[:, rs] = jnp.transpose(aff)[:N_EXPERTS, :]


def _out_proj_route(h2, oa, o_f, o_b, zg, oc, og, hm, w, gain, wr2, tm):
    n = h2.shape[0]
    row = lambda i: (i, 0)
    fixed = lambda i: (0, 0)
    return pl.pallas_call(
        _out_proj_kernel,
        grid=(n // tm,),
        in_specs=[pl.BlockSpec((tm, D_MODEL), row), pl.BlockSpec((tm, ATT_Q), row),
                  pl.BlockSpec((tm, DN_WIDTH), row), pl.BlockSpec((tm, DN_WIDTH), row),
                  pl.BlockSpec((tm, DN_WIDTH), row),
                  pl.BlockSpec((tm, CONV_CH), row), pl.BlockSpec((1, DN_WIDTH), fixed),
                  pl.BlockSpec(hm.shape, fixed), pl.BlockSpec(w.shape, fixed),
                  pl.BlockSpec((1, D_MODEL), fixed), pl.BlockSpec((2, D_MODEL, LANES), lambda i: (0, 0, 0))],
        out_specs=[pl.BlockSpec((tm, D_MODEL), row), pl.BlockSpec((tm, D_MODEL), row),
                   pl.BlockSpec((N_EXPERTS, tm), lambda i: (0, i))],
        out_shape=[jax.ShapeDtypeStruct((n, D_MODEL), F32), jax.ShapeDtypeStruct((n, D_MODEL), BF16),
                   jax.ShapeDtypeStruct((N_EXPERTS, n), F32)],
        compiler_params=_params(("parallel",)),
        name="out_proj_route",
    )(h2, oa, o_f, o_b, zg, oc, og, hm, w, gain, wr2)


MOE_TILE = 256
MOE_ALIGN = 2 * SUBLANES
MOE_WIN = 64
MOE_PAD = 1024
MOE_FFN_TILE = 1024
FF_CHUNK = 256
MOE_UNSELECTED = -64.0


def _select_kernel(aff_ref, tri_ref, val_ref, cnt_ref, *, cap, tile):
    ne, n = aff_ref.shape
    nt = n // tile
    capf = float(cap)

    def bits_of(x):
        return lax.bitcast_convert_type(x, jnp.int32)

    def search(i, thr):
        cand = thr | jnp.left_shift(jnp.int32(1), 30 - i)
        cnt = jnp.sum((bits_of(aff_ref[...]) >= cand).astype(F32), axis=1, keepdims=True)
        return jnp.where(cnt >= capf, cand, thr)

    thr = lax.fori_loop(0, 31, search, jnp.zeros((ne, 1), jnp.int32))
    n_gt = jnp.sum((bits_of(aff_ref[...]) > thr).astype(F32), axis=1, keepdims=True)
    need = capf - n_gt
    lane = lax.broadcasted_iota(jnp.int32, (ne, LANES), 1)

    def tile_body(j, carry):
        eq_before, cnt_acc = carry
        off = pl.multiple_of(j * tile, tile)
        b = bits_of(aff_ref[:, pl.ds(off, tile)])
        gt = b > thr
        eqf = (b == thr).astype(F32)
        eq_rank = eq_before + jnp.dot(eqf.astype(BF16), tri_ref[...], preferred_element_type=F32)
        self_ = jnp.where(gt, 1.0, jnp.where(eq_rank <= need, eqf, 0.0))
        rank = jnp.dot(self_.astype(BF16), tri_ref[...], preferred_element_type=F32)
        val_ref[:, pl.ds(off, tile)] = jnp.where(self_ > 0.0, rank, MOE_UNSELECTED)
        cnt = jnp.sum(self_, axis=1, keepdims=True)
        return (eq_before + jnp.sum(eqf, axis=1, keepdims=True), cnt_acc + jnp.where(lane == j, cnt, 0.0))

    init = (jnp.zeros((ne, 1), F32), jnp.zeros((ne, LANES), F32))
    _, cnt_acc = lax.fori_loop(0, nt, tile_body, init, unroll=math.gcd(nt, 4))
    cnt_ref[...] = cnt_acc


def _select(aff_t, cap, tile):
    ne, n = aff_t.shape
    assert n // tile <= LANES
    tri = jnp.asarray(np.triu(np.ones((tile, tile), np.float32)), dtype=BF16)
    return pl.pallas_call(
        functools.partial(_select_kernel, cap=cap, tile=tile),
        out_shape=[jax.ShapeDtypeStruct((ne, n), F32), jax.ShapeDtypeStruct((ne, LANES), F32)],
        compiler_params=pltpu.CompilerParams(vmem_limit_bytes=VMEM_LIMIT),
        name="moe_select",
    )(aff_t, tri)


def _moe_plan(cnt, nt):
    c = cnt[:, :nt].astype(jnp.int32).T
    starts = jnp.concatenate([jnp.zeros((1, N_EXPERTS), jnp.int32), jnp.cumsum(c, axis=0)], axis=0)
    head = starts[:-1] & (MOE_ALIGN - 1)
    kmax = jnp.maximum(jnp.max((head + c + MOE_WIN - 1) // MOE_WIN, axis=1), 1).astype(jnp.int32)
    w = jnp.arange(MOE_WIN, dtype=jnp.int32)
    tgt = (w[None, None, :] + 1 - head[:, :, None]).astype(F32).reshape(nt, 1, N_EXPERTS * MOE_WIN)
    return starts.reshape(-1), kmax, tgt


def _expand_matrix():
    e = np.arange(N_EXPERTS * MOE_WIN) // MOE_WIN
    return jnp.asarray((np.arange(N_EXPERTS)[:, None] == e[None, :]).astype(np.float32), dtype=BF16)


def _slot_onehot(val_ref, eexp_ref):
    return lax.dot_general(val_ref[...].astype(BF16), eexp_ref[...], (((0,), (0,)), ((), ())),
                           preferred_element_type=F32)


def _dispatch_kernel(start_ref, kmax_ref, xn_ref, val_ref, tgt_ref, eexp_ref, xe_ref, stage, carry, sem):
    j = pl.program_id(0)
    nt = pl.num_programs(0)
    slot = lax.rem(j, 2)
    ne = N_EXPERTS

    cap = xe_ref.shape[1] - MOE_PAD

    @pl.when(j == 0)
    def _():
        carry[...] = jnp.zeros_like(carry)
        stage[0, 0:MOE_PAD, :] = jnp.zeros((MOE_PAD, D_MODEL), BF16)
        fills = [pltpu.make_async_copy(stage.at[0, pl.ds(0, MOE_PAD)], xe_ref.at[e, pl.ds(cap, MOE_PAD)], sem.at[0])
                 for e in range(ne)]
        for f in fills:
            f.start()
        for f in fills:
            f.wait()

    def window_copy(sl, e, row0):
        return pltpu.make_async_copy(stage.at[sl, pl.ds(e * MOE_WIN, MOE_WIN)],
                                     xe_ref.at[e, pl.ds(row0, MOE_WIN)], sem.at[sl])

    def wait_windows(sl):
        for e in range(ne):
            window_copy(sl, e, 0).wait()

    rep = _slot_onehot(val_ref, eexp_ref)
    xn = xn_ref[...]
    row = lax.broadcasted_iota(jnp.int32, (MOE_ALIGN, D_MODEL), 0)

    def block(k, _):
        @pl.when(k > 0)
        def _():
            wait_windows(slot)

        lo = k * MOE_WIN
        pt = (rep == tgt_ref[...] + lo.astype(F32)).astype(BF16)
        comp = lax.dot_general(pt, xn, (((0,), (0,)), ((), ())), preferred_element_type=F32)
        stage[slot] = comp.astype(BF16)
        for e in range(ne):
            s = start_ref[j * ne + e]
            head = s & (MOE_ALIGN - 1)
            r0 = e * MOE_WIN

            @pl.when(k == 0)
            def _():
                fresh = stage[slot, r0:r0 + MOE_ALIGN, :]
                kept = carry[e * MOE_ALIGN:(e + 1) * MOE_ALIGN, :]
                stage[slot, r0:r0 + MOE_ALIGN, :] = jnp.where(row < head, kept, fresh)

            nxt = (head + start_ref[(j + 1) * ne + e] - s) & (-MOE_ALIGN)

            @pl.when((nxt >= lo) & (nxt < lo + MOE_WIN))
            def _():
                off = pl.multiple_of(nxt - lo, MOE_ALIGN)
                carry[e * MOE_ALIGN:(e + 1) * MOE_ALIGN, :] = stage[slot, pl.ds(r0 + off, MOE_ALIGN), :]

        @pl.when((k == 0) & (j > 0))
        def _():
            wait_windows(1 - slot)

        for e in range(ne):
            base = pl.multiple_of((start_ref[j * ne + e] & (-MOE_ALIGN)) + lo, MOE_ALIGN)
            window_copy(slot, e, base).start()
        return 0

    lax.fori_loop(0, kmax_ref[j], block, 0)

    @pl.when(j == nt - 1)
    def _():
        wait_windows(slot)


def _dispatch(xn, val, starts, kmax, tgt, eexp, cap, tile):
    n = xn.shape[0]
    nt = n // tile
    rows = N_EXPERTS * MOE_WIN
    return pl.pallas_call(
        _dispatch_kernel,
        grid_spec=pltpu.PrefetchScalarGridSpec(
            num_scalar_prefetch=2, grid=(nt,),
            in_specs=[pl.BlockSpec((tile, D_MODEL), lambda j, s, k: (j, 0)),
                      pl.BlockSpec((N_EXPERTS, tile), lambda j, s, k: (0, j)),
                      pl.BlockSpec((None, 1, rows), lambda j, s, k: (j, 0, 0)),
                      pl.BlockSpec((N_EXPERTS, rows), lambda j, s, k: (0, 0))],
            out_specs=pl.BlockSpec(memory_space=pl.ANY),
            scratch_shapes=[pltpu.VMEM((2, rows, D_MODEL), BF16),
                            pltpu.VMEM((N_EXPERTS * MOE_ALIGN, D_MODEL), BF16),
                            pltpu.SemaphoreType.DMA((2,))]),
        out_shape=jax.ShapeDtypeStruct((N_EXPERTS, cap + MOE_PAD, D_MODEL), BF16),
        compiler_params=_params(("arbitrary",)),
        name="moe_dispatch",
    )(starts, kmax, xn, val, tgt, eexp)


def _expert_kernel(x_ref, wr_ref, wg_ref, wu_ref, wd_ref, y_ref, *, ntile):
    e = pl.program_id(0)
    i = pl.program_id(1)

    @pl.when(i < ntile)
    def _():
        x = x_ref[...]
        logits = (jnp.dot(x, wr_ref[0], preferred_element_type=F32)
                  + jnp.dot(x, wr_ref[1], preferred_element_type=F32))
        lane = lax.broadcasted_iota(jnp.int32, logits.shape, 1)
        logits = jnp.where(lane < N_EXPERTS, logits, -jnp.inf)
        ex = jnp.exp(logits - jnp.max(logits, axis=-1, keepdims=True))
        gate = (jnp.sum(jnp.where(lane == e, ex, 0.0), axis=-1, keepdims=True)
                / jnp.sum(ex, axis=-1, keepdims=True))
        hid = []
        for c0 in range(0, EXPERT_FF, FF_CHUNK):
            hg = jnp.dot(x, wg_ref[:, c0:c0 + FF_CHUNK], preferred_element_type=F32)
            hu = jnp.dot(x, wu_ref[:, c0:c0 + FF_CHUNK], preferred_element_type=F32)
            hid.append((_silu(hg) * hu).astype(BF16))
        hid = jnp.concatenate(hid, axis=1)
        y_ref[...] = (jnp.dot(hid, wd_ref[...], preferred_element_type=F32) * gate).astype(BF16)

    @pl.when(i >= ntile)
    def _():
        y_ref[...] = jnp.zeros_like(y_ref)


def _expert_ffn(xe, wr2, wg, wu, wd, layer, cap, tc):
    ne, rows, _ = xe.shape
    wspec = lambda shape: pl.BlockSpec((None, None) + shape, lambda e, i: (layer, e, 0, 0))
    return pl.pallas_call(
        functools.partial(_expert_kernel, ntile=cap // tc),
        grid=(ne, rows // tc),
        in_specs=[pl.BlockSpec((None, tc, D_MODEL), lambda e, i: (e, i, 0)),
                  pl.BlockSpec(wr2.shape, lambda e, i: (0, 0, 0)),
                  wspec((D_MODEL, EXPERT_FF)), wspec((D_MODEL, EXPERT_FF)), wspec((EXPERT_FF, D_MODEL))],
        out_specs=pl.BlockSpec((None, tc, D_MODEL), lambda e, i: (e, i, 0)),
        out_shape=jax.ShapeDtypeStruct((ne, rows, D_MODEL), BF16),
        compiler_params=_params(("parallel", "parallel")),
        name="expert_ffn",
    )(xe, wr2, wg, wu, wd)


COMBINE_SUB = 2


def _combine_kernel(start_ref, kmax_ref, h_ref, p_ref, val_ref, tgt_ref, eexp_ref, gain_ref, wg_ref, wp_ref,
                    ye_ref, out_ref, stage, sem, *, tile):
    j = pl.program_id(0)
    nstep = pl.num_programs(0)
    slot = lax.rem(j, 2)
    ne = N_EXPERTS
    subs = range(COMBINE_SUB)

    def window_copy(sl, sub, e, row0):
        buf = sl * COMBINE_SUB + sub
        return pltpu.make_async_copy(ye_ref.at[e, pl.ds(row0, MOE_WIN)],
                                     stage.at[buf, pl.ds(e * MOE_WIN, MOE_WIN)], sem.at[buf])

    def fetch(sl, sub, tile_idx, lo):
        for e in range(ne):
            base = pl.multiple_of((start_ref[tile_idx * ne + e] & (-MOE_ALIGN)) + lo, MOE_ALIGN)
            window_copy(sl, sub, e, base).start()

    def wait_windows(sl, sub):
        for e in range(ne):
            window_copy(sl, sub, e, 0).wait()

    @pl.when(j == 0)
    def _():
        for sub in subs:
            fetch(slot, sub, sub, 0)

    @pl.when(j + 1 < nstep)
    def _():
        for sub in subs:
            fetch(1 - slot, sub, (j + 1) * COMBINE_SUB + sub, 0)

    reps = [lax.dot_general(val_ref[:, sub * tile:(sub + 1) * tile].astype(BF16), eexp_ref[...],
                            (((0,), (0,)), ((), ())), preferred_element_type=F32) for sub in subs]
    pts = [(reps[sub] == tgt_ref[sub]).astype(BF16) for sub in subs]
    for sub in subs:
        wait_windows(slot, sub)
    accs = [h_ref[sub * tile:(sub + 1) * tile, :]
            + jnp.dot(pts[sub], stage[slot * COMBINE_SUB + sub], preferred_element_type=F32) for sub in subs]
    for sub in subs:
        t = j * COMBINE_SUB + sub

        def extra(k, acc, sub=sub, t=t):
            lo = k * MOE_WIN
            fetch(slot, sub, t, lo)
            wait_windows(slot, sub)
            pk = (reps[sub] == tgt_ref[sub] + lo.astype(F32)).astype(BF16)
            return acc + jnp.dot(pk, stage[slot * COMBINE_SUB + sub], preferred_element_type=F32)

        accs[sub] = lax.fori_loop(1, kmax_ref[t], extra, accs[sub])
    x = jnp.concatenate(accs, axis=0)
    ms = jnp.mean(x * x, axis=-1, keepdims=True)
    xn = (x * lax.rsqrt(ms + NORM_EPS) * gain_ref[...]).astype(BF16)
    gate = _sigmoid(jnp.dot(xn, wg_ref[...], preferred_element_type=F32))
    proj = jnp.dot(p_ref[...].astype(BF16), wp_ref[...], preferred_element_type=F32)
    out_ref[...] = x + gate * proj


def _combine_ple(h2, p3, layer, ye, val, starts, kmax, tgt, eexp, gain, wg, wp, tile):
    n = h2.shape[0]
    step = COMBINE_SUB * tile
    assert n % step == 0
    rows = N_EXPERTS * MOE_WIN
    fixed = lambda j, s, k: (0, 0)
    return pl.pallas_call(
        functools.partial(_combine_kernel, tile=tile),
        grid_spec=pltpu.PrefetchScalarGridSpec(
            num_scalar_prefetch=2, grid=(n // step,),
            in_specs=[pl.BlockSpec((step, D_MODEL), lambda j, s, k: (j, 0)),
                      pl.BlockSpec((None, step, PLE_DIM), lambda j, s, k: (layer, j, 0)),
                      pl.BlockSpec((N_EXPERTS, step), lambda j, s, k: (0, j)),
                      pl.BlockSpec((COMBINE_SUB, 1, rows), lambda j, s, k: (j, 0, 0)),
                      pl.BlockSpec((N_EXPERTS, rows), fixed),
                      pl.BlockSpec((1, D_MODEL), fixed), pl.BlockSpec(wg.shape, fixed),
                      pl.BlockSpec(wp.shape, fixed),
                      pl.BlockSpec(memory_space=pl.ANY)],
            out_specs=pl.BlockSpec((step, D_MODEL), lambda j, s, k: (j, 0)),
            scratch_shapes=[pltpu.VMEM((2 * COMBINE_SUB, rows, D_MODEL), BF16),
                            pltpu.SemaphoreType.DMA((2 * COMBINE_SUB,))]),
        out_shape=jax.ShapeDtypeStruct((n, D_MODEL), F32),
        compiler_params=_params(("arbitrary",)),
        name="moe_combine_ple",
    )(starts, kmax, h2, p3, val, tgt, eexp, gain, wg, wp, ye)


def _relayout_w_in(w_in):
    o_beta = ZA + ZB + DN_WIDTH
    o_alpha = o_beta + 2 * DN_HEADS
    o_glu = o_alpha + 2 * DN_HEADS
    pieces = [w_in[:, :o_beta], w_in[:, o_glu:o_glu + 2 * CONV_CH]]
    for d in range(2):
        pieces.append(w_in[:, o_beta + d * DN_HEADS:o_beta + (d + 1) * DN_HEADS])
        pieces.append(w_in[:, o_alpha + d * DN_HEADS:o_alpha + (d + 1) * DN_HEADS])
    pieces.append(jnp.zeros((w_in.shape[0], LANES - 4 * DN_HEADS), w_in.dtype))
    return jnp.concatenate(pieces, axis=1).astype(BF16)


def _prep_layer(lw):
    (norm_mix, w_in, q_gain, k_gain, sink, dn_conv, dn_a_log, dn_dt_bias, dn_out_gain,
     cv_dw, cv_dw_bias, cv_ln_gain, cv_ln_bias, w_out, norm_ffn, w_router, w_gate, w_up, w_down,
     norm_ple, w_ple_gate, w_ple_proj) = lw
    w_perm = _relayout_w_in(w_in)
    hgain = jnp.concatenate([jnp.tile(q_gain, ATT_HEADS) * (ATT_HEAD_DIM ** -0.5),
                             jnp.tile(k_gain, ATT_KV_HEADS)]).reshape(1, -1)
    zeros4 = jnp.zeros((DN_HEADS,), F32)
    aneg = -jnp.exp(dn_a_log.astype(F32))
    aneg_row = jnp.concatenate([zeros4, aneg[0], zeros4, aneg[1]])
    dtb_row = jnp.concatenate([zeros4, dn_dt_bias[0], zeros4, dn_dt_bias[1]])
    pad = lambda r: jnp.pad(r, (0, LANES - r.shape[0])).reshape(1, LANES)
    wr = jnp.pad(w_router.astype(F32), ((0, 0), (0, LANES - N_EXPERTS)))
    wr_hi = wr.astype(BF16)
    wr2 = jnp.stack([wr_hi, (wr - wr_hi.astype(F32)).astype(BF16)])
    return dict(
        w_router2=wr2,
        norm_mix=norm_mix.reshape(1, -1), w_in=w_perm, hgain=hgain, sink=sink.astype(F32),
        dn_conv=dn_conv, aneg=pad(aneg_row), dtb=pad(dtb_row),
        dn_out_gain=jnp.tile(dn_out_gain, DN_HEADS).reshape(1, -1),
        cv_dw=cv_dw, cv_dw_bias=cv_dw_bias.reshape(1, -1), cv_ln_gain=cv_ln_gain.reshape(1, -1),
        cv_ln_bias=cv_ln_bias.reshape(1, -1), w_out=w_out.astype(BF16),
        norm_ffn=norm_ffn.reshape(1, -1),
        norm_ple=norm_ple.reshape(1, -1), w_ple_gate=w_ple_gate.astype(BF16), w_ple_proj=w_ple_proj.astype(BF16))


def _tiles(bsz, seqlen):
    n = bsz * seqlen
    return dict(tm=min(1024, n), tl=min(512, seqlen), ch=min(256, seqlen), tcv=min(1024, seqlen))


def _moe_ple(h2, xn, aff_t, p3, layer, pw):
    n = h2.shape[0]
    cap = CAPACITY_FACTOR * n // N_EXPERTS
    tile = min(MOE_TILE, n)
    val, cnt = _select(aff_t, cap, tile)
    starts, kmax, tgt = _moe_plan(cnt, n // tile)
    eexp = _expand_matrix()
    xe = _dispatch(xn, val, starts, kmax, tgt, eexp, cap, tile)
    ye = _expert_ffn(xe, pw["w_router2"], pw["w_gate"], pw["w_up"], pw["w_down"], layer, cap,
                     min(MOE_FFN_TILE, cap))
    return _combine_ple(h2, p3, layer, ye, val, starts, kmax, tgt, eexp, pw["norm_ple"], pw["w_ple_gate"],
                        pw["w_ple_proj"], tile)


def _layer(h2, p3, layer, pw, bsz, seqlen):
    t = _tiles(bsz, seqlen)
    hm_att = _head_mean_matrix(ATT_Q + ATT_KV, ATT_HEAD_DIM)
    hs_dn = _head_sum_matrix(2 * DN_WIDTH, DN_HEAD_DIM)
    hm_dn = _head_mean_matrix(DN_WIDTH, DN_HEAD_DIM)
    za, zb, zg, glu_in, gates = _in_proj(h2, pw["norm_mix"], pw["w_in"], hm_att, pw["hgain"], t["tm"])
    o_a = _attention(za, pw["sink"], bsz, seqlen)
    y, gb = _dn_prep(zb, gates, pw["dn_conv"], hs_dn, pw["aneg"], pw["dtb"], bsz, seqlen, t["tl"])
    o_f, o_b = _dn_chunk(y, gb, bsz, seqlen, t["ch"])
    o_c = _conformer_conv(glu_in, pw["cv_dw"], pw["cv_dw_bias"], pw["cv_ln_gain"], pw["cv_ln_bias"],
                          bsz, seqlen, t["tcv"])
    h2, xn, aff_t = _out_proj_route(h2, o_a, o_f, o_b, zg, o_c, pw["dn_out_gain"], hm_dn, pw["w_out"],
                                    pw["norm_ffn"], pw["w_router2"], t["tm"])
    return _moe_ple(h2, xn, aff_t, p3, layer, pw)


def _trunk(x, p, layer_weights):
    bsz, seqlen, _ = x.shape
    h2 = x.reshape(bsz * seqlen, D_MODEL)
    p3 = p.reshape(p.shape[0], bsz * seqlen, PLE_DIM)
    for i, pw in enumerate(layer_weights):
        h2 = _layer(h2, p3, i, pw, bsz, seqlen)
    return h2.reshape(bsz, seqlen, D_MODEL)


def kernel(x_prompt, x_sample, p_prompt, p_sample, norm_mix, w_in, q_gain, k_gain, sink, dn_conv, dn_a_log,
           dn_dt_bias, dn_out_gain, cv_dw, cv_dw_bias, cv_ln_gain, cv_ln_bias, w_out, norm_ffn, w_router,
           w_gate, w_up, w_down, norm_ple, w_ple_gate, w_ple_proj):
    weights = (norm_mix, w_in, q_gain, k_gain, sink, dn_conv, dn_a_log, dn_dt_bias, dn_out_gain,
               cv_dw, cv_dw_bias, cv_ln_gain, cv_ln_bias, w_out, norm_ffn, w_router, w_gate, w_up, w_down,
               norm_ple, w_ple_gate, w_ple_proj)
    depth = w_in.shape[0]
    experts = dict(w_gate=w_gate.astype(BF16), w_up=w_up.astype(BF16), w_down=w_down.astype(BF16))
    layer_weights = [dict(_prep_layer([w[i] for w in weights]), **experts) for i in range(depth)]
    return (_trunk(x_prompt, p_prompt, layer_weights), _trunk(x_sample, p_sample, layer_weights))
```

```python
import functools
import math

import numpy as np
import jax
import jax.numpy as jnp
from jax import lax
from jax.experimental import pallas as pl
from jax.experimental.pallas import tpu as pltpu

F32 = jnp.float32
BF16 = jnp.bfloat16

D_MODEL = 1024
ATT_HEADS = 8
ATT_KV_HEADS = 2
ATT_HEAD_DIM = 64
ATT_GROUP = ATT_HEADS // ATT_KV_HEADS
WINDOW = 128
ATT_BLOCK = 128
DN_HEADS = 4
DN_HEAD_DIM = 64
DN_WIDTH = DN_HEADS * DN_HEAD_DIM
DN_CHUNK = 64
CONV_CH = 256
CONV_WIDTH = 31
ATT_Q = ATT_HEADS * ATT_HEAD_DIM
ATT_KV = ATT_KV_HEADS * ATT_HEAD_DIM
N_EXPERTS = 16
CAPACITY_FACTOR = 2
EXPERT_FF = 1024
PLE_DIM = 256
NORM_EPS = 1e-6

LANES = 128
SUBLANES = 8
VMEM_LIMIT = 48 * 1024 * 1024

ZA = ATT_Q + 2 * ATT_KV
ZB = 3 * DN_WIDTH
ZW = ZA + ZB + DN_WIDTH + 2 * CONV_CH + LANES


def _params(sem):
    return pltpu.CompilerParams(dimension_semantics=sem, vmem_limit_bytes=VMEM_LIMIT)


def _head_mean_matrix(width, head):
    idx = np.arange(width) // head
    return jnp.asarray((idx[:, None] == idx[None, :]).astype(np.float32) / head, dtype=BF16)


def _head_sum_matrix(width, head):
    idx = np.arange(width) // head
    return jnp.asarray((idx[:, None] == idx[None, :]).astype(np.float32), dtype=BF16)


def _sigmoid(x):
    return 1.0 / (1.0 + jnp.exp(-x))


def _silu(x):
    return x * _sigmoid(x)


def _in_proj_kernel(x_ref, gain_ref, w_ref, hm_ref, hgain_ref, za_ref, zb_ref, zg_ref, glu_ref, gates_ref):
    x = x_ref[...]
    ms = jnp.mean(x * x, axis=-1, keepdims=True)
    a = (x * lax.rsqrt(ms + NORM_EPS) * gain_ref[...]).astype(BF16)
    z = jnp.dot(a, w_ref[...], preferred_element_type=F32)
    nqk = ATT_Q + ATT_KV
    qk = z[:, :nqk]
    hms = jnp.dot((qk * qk).astype(BF16), hm_ref[...], preferred_element_type=F32)
    za_ref[:, :nqk] = (qk * lax.rsqrt(hms + NORM_EPS) * hgain_ref[...]).astype(BF16)
    za_ref[:, nqk:] = z[:, nqk:ZA].astype(BF16)
    zb_ref[...] = z[:, ZA:ZA + ZB]
    zg_ref[...] = z[:, ZA + ZB:ZA + ZB + DN_WIDTH]
    glu_ref[...] = z[:, ZA + ZB + DN_WIDTH:ZA + ZB + DN_WIDTH + 2 * CONV_CH]
    gates_ref[...] = z[:, ZW - LANES:]


def _in_proj(h2, gain, w_perm, hm, hgain, tm):
    n = h2.shape[0]
    row = lambda i: (i, 0)
    fixed = lambda i: (0, 0)
    return pl.pallas_call(
        _in_proj_kernel,
        grid=(n // tm,),
        in_specs=[pl.BlockSpec((tm, D_MODEL), row), pl.BlockSpec((1, D_MODEL), fixed),
                  pl.BlockSpec((D_MODEL, ZW), fixed), pl.BlockSpec(hm.shape, fixed),
                  pl.BlockSpec(hgain.shape, fixed)],
        out_specs=[pl.BlockSpec((tm, ZA), row), pl.BlockSpec((tm, ZB), row), pl.BlockSpec((tm, DN_WIDTH), row),
                   pl.BlockSpec((tm, 2 * CONV_CH), row), pl.BlockSpec((tm, LANES), row)],
        out_shape=[jax.ShapeDtypeStruct((n, ZA), BF16), jax.ShapeDtypeStruct((n, ZB), F32),
                   jax.ShapeDtypeStruct((n, DN_WIDTH), F32), jax.ShapeDtypeStruct((n, 2 * CONV_CH), F32),
                   jax.ShapeDtypeStruct((n, LANES), F32)],
        compiler_params=_params(("parallel",)),
        name="in_proj",
    )(h2, gain, w_perm, hm, hgain)


ATT_MASKED = -1e30


def _attn_bias_table():
    i = np.arange(ATT_BLOCK)[:, None]
    c = np.arange(3 * ATT_BLOCK)[None, :]
    rel = c - ATT_BLOCK - i
    slopes = 2.0 ** (-8.0 * np.arange(1, ATT_HEADS + 1) / ATT_HEADS)
    table = np.empty((3, ATT_KV_HEADS, ATT_GROUP * ATT_BLOCK, 3 * ATT_BLOCK), np.float32)
    for variant in range(3):
        ok = np.abs(rel) <= WINDOW
        if variant == 0:
            ok = ok & (c >= ATT_BLOCK)
        if variant == 2:
            ok = ok & (c < 2 * ATT_BLOCK)
        for hd in range(ATT_HEADS):
            g, j = divmod(hd, ATT_GROUP)
            table[variant, g, j * ATT_BLOCK:(j + 1) * ATT_BLOCK] = np.where(ok, -slopes[hd] * np.abs(rel), ATT_MASKED)
    return jnp.asarray(table)


def _attn_kernel(sink_ref, q_ref, kvp_ref, kvo_ref, kvn_ref, bias_a_ref, bias_b_ref, o_ref):
    kv = jnp.concatenate([kvp_ref[...], kvo_ref[...], kvn_ref[...]], axis=0)
    hd_ = ATT_HEAD_DIM
    groups = range(ATT_KV_HEADS)
    heads = range(ATT_HEADS)
    rows = lambda t, hd: t[(hd % ATT_GROUP) * ATT_BLOCK:(hd % ATT_GROUP + 1) * ATT_BLOCK]
    work = []
    for blk, bias_ref in enumerate((bias_a_ref, bias_b_ref)):
        keys = kv[blk * ATT_BLOCK:(blk + 3) * ATT_BLOCK]
        ks = [keys[:, g * hd_:(g + 1) * hd_] for g in groups]
        vs = [keys[:, ATT_KV + g * hd_:ATT_KV + (g + 1) * hd_] for g in groups]
        q = q_ref[blk * ATT_BLOCK:(blk + 1) * ATT_BLOCK, :]
        qs = [jnp.concatenate([q[:, (g * ATT_GROUP + j) * hd_:(g * ATT_GROUP + j + 1) * hd_]
                               for j in range(ATT_GROUP)], axis=0) for g in groups]
        sg = [lax.dot_general(qs[g], ks[g], (((1,), (1,)), ((), ())), preferred_element_type=F32) + bias_ref[g]
              for g in groups]
        work.append((vs, [rows(sg[hd // ATT_GROUP], hd) for hd in heads]))
    m = [[jnp.maximum(jnp.max(s[hd], axis=-1, keepdims=True), sink_ref[hd]) for hd in heads] for _, s in work]
    e = [[jnp.exp(s[hd] - mb[hd]) for hd in heads] for (_, s), mb in zip(work, m)]
    den = [[jnp.sum(eb[hd], axis=-1, keepdims=True) + jnp.exp(sink_ref[hd] - mb[hd]) for hd in heads]
           for eb, mb in zip(e, m)]
    for blk, ((vs, _), eb, db) in enumerate(zip(work, e, den)):
        eg = [jnp.concatenate([eb[g * ATT_GROUP + j].astype(BF16) for j in range(ATT_GROUP)], axis=0)
              for g in groups]
        og = [jnp.dot(eg[g], vs[g], preferred_element_type=F32) for g in groups]
        for hd in heads:
            o_ref[blk * ATT_BLOCK:(blk + 1) * ATT_BLOCK, hd * hd_:(hd + 1) * hd_] = (
                rows(og[hd // ATT_GROUP], hd) / db[hd]).astype(BF16)


def _attention(za, sink, bsz, seqlen):
    nb = seqlen // ATT_BLOCK
    assert nb >= 2 and nb % 2 == 0
    npair = nb // 2
    za3 = za.reshape(bsz, seqlen, ZA)
    kvw = 2 * ATT_KV
    kvc = ATT_Q // kvw
    bias = _attn_bias_table()
    bias_spec = lambda pick: pl.BlockSpec((None,) + bias.shape[1:], lambda b, n: (pick(n), 0, 0, 0))
    return pl.pallas_call(
        _attn_kernel,
        grid=(bsz, npair),
        in_specs=[pl.BlockSpec(memory_space=pltpu.SMEM),
                  pl.BlockSpec((None, 2 * ATT_BLOCK, ATT_Q), lambda b, n: (b, n, 0)),
                  pl.BlockSpec((None, ATT_BLOCK, kvw), lambda b, n: (b, jnp.maximum(2 * n - 1, 0), kvc)),
                  pl.BlockSpec((None, 2 * ATT_BLOCK, kvw), lambda b, n: (b, n, kvc)),
                  pl.BlockSpec((None, ATT_BLOCK, kvw), lambda b, n: (b, jnp.minimum(2 * n + 2, nb - 1), kvc)),
                  bias_spec(lambda n: jnp.where(n == 0, 0, 1)),
                  bias_spec(lambda n: jnp.where(n == npair - 1, 2, 1))],
        out_specs=pl.BlockSpec((None, 2 * ATT_BLOCK, ATT_Q), lambda b, n: (b, n, 0)),
        out_shape=jax.ShapeDtypeStruct((bsz, seqlen, ATT_Q), BF16),
        compiler_params=_params(("parallel", "parallel")),
        name="window_attention",
    )(sink, za3, za3, za3, za3, bias, bias).reshape(bsz * seqlen, ATT_Q)


DN_HALO = SUBLANES


def _dn_prep_kernel(x_ref, xp_ref, xn_ref, cw_ref, hs_ref, g_ref, aneg_ref, dtb_ref, mf_ref, mb_ref,
                    y_ref, gb_ref, buf_ref, *, tl):
    i = pl.program_id(1)
    nt = pl.num_programs(1)
    buf_ref[0:DN_HALO, :] = jnp.where(i > 0, xp_ref[...], 0.0)
    buf_ref[DN_HALO:DN_HALO + tl, :] = x_ref[...]
    buf_ref[DN_HALO + tl:, :] = jnp.where(i < nt - 1, xn_ref[...], 0.0)
    y = (cw_ref[0:1, :] * buf_ref[DN_HALO - 1:DN_HALO - 1 + tl, :]
         + cw_ref[1:2, :] * buf_ref[DN_HALO:DN_HALO + tl, :]
         + cw_ref[2:3, :] * buf_ref[DN_HALO + 1:DN_HALO + 1 + tl, :])
    y = _silu(y)
    qk = y[:, :2 * DN_WIDTH]
    ss = jnp.dot((qk * qk).astype(BF16), hs_ref[...], preferred_element_type=F32)
    lane = lax.broadcasted_iota(jnp.int32, (tl, 2 * DN_WIDTH), 1)
    scale = jnp.where(lane < DN_WIDTH, DN_HEAD_DIM ** -0.5, 1.0)
    y_ref[:, :2 * DN_WIDTH] = qk * lax.rsqrt(ss + NORM_EPS) * scale
    y_ref[:, 2 * DN_WIDTH:] = y[:, 2 * DN_WIDTH:]
    raw = g_ref[...]
    col = lax.broadcasted_iota(jnp.int32, (tl, LANES), 1)
    is_beta = (col & DN_HEADS) == 0
    t = raw + dtb_ref[...]
    softplus = jnp.maximum(t, 0.0) + jnp.log(1.0 + jnp.exp(-jnp.abs(t)))
    vals = jnp.where(is_beta, _sigmoid(raw), aneg_ref[...] * softplus)
    v_hi = vals.astype(BF16)
    r1 = vals - v_hi.astype(F32)
    v_mid = r1.astype(BF16)
    v_lo = (r1 - v_mid.astype(F32)).astype(BF16)
    terms = jnp.concatenate([v_hi, v_mid, v_lo], axis=1)
    cf3 = jnp.dot(mf_ref[...], terms, preferred_element_type=F32)
    cb3 = jnp.dot(mb_ref[...], terms, preferred_element_type=F32)
    cf = cf3[:, :LANES] + (cf3[:, LANES:2 * LANES] + cf3[:, 2 * LANES:])
    cb = cb3[:, :LANES] + (cb3[:, LANES:2 * LANES] + cb3[:, 2 * LANES:])
    gb_ref[0] = jnp.where(is_beta, vals, cf)
    gb_ref[1] = pltpu.roll(jnp.where(is_beta, vals, cb), LANES - 2 * DN_HEADS, axis=1)


def _dn_prep(zb, gates, conv_w, hs, aneg, dtb, bsz, seqlen, tl):
    zb3 = zb.reshape(bsz, seqlen, ZB)
    g3 = gates.reshape(bsz, seqlen, LANES)
    nt = seqlen // tl
    hb = tl // DN_HALO
    ch = np.arange(tl) // DN_CHUNK
    same = ch[:, None] == ch[None, :]
    pos = np.arange(tl)
    mf = jnp.asarray((same & (pos[None, :] <= pos[:, None])).astype(np.float32), dtype=BF16)
    mb = jnp.asarray((same & (pos[None, :] >= pos[:, None])).astype(np.float32), dtype=BF16)
    fixed = lambda b, i: (0, 0)
    y, gb = pl.pallas_call(
        functools.partial(_dn_prep_kernel, tl=tl),
        grid=(bsz, nt),
        in_specs=[pl.BlockSpec((None, tl, ZB), lambda b, i: (b, i, 0)),
                  pl.BlockSpec((None, DN_HALO, ZB), lambda b, i: (b, jnp.maximum(i * hb - 1, 0), 0)),
                  pl.BlockSpec((None, DN_HALO, ZB), lambda b, i: (b, jnp.minimum((i + 1) * hb, nt * hb - 1), 0)),
                  pl.BlockSpec(conv_w.shape, fixed), pl.BlockSpec(hs.shape, fixed),
                  pl.BlockSpec((None, tl, LANES), lambda b, i: (b, i, 0)),
                  pl.BlockSpec((1, LANES), fixed), pl.BlockSpec((1, LANES), fixed),
                  pl.BlockSpec((tl, tl), fixed), pl.BlockSpec((tl, tl), fixed)],
        out_specs=[pl.BlockSpec((None, tl, ZB), lambda b, i: (b, i, 0)),
                   pl.BlockSpec((2, None, tl, LANES), lambda b, i: (0, b, i, 0))],
        out_shape=[jax.ShapeDtypeStruct((bsz, seqlen, ZB), F32),
                   jax.ShapeDtypeStruct((2, bsz, seqlen, LANES), F32)],
        scratch_shapes=[pltpu.VMEM((tl + 2 * DN_HALO, ZB), F32)],
        compiler_params=_params(("parallel", "parallel")),
        name="deltanet_prep",
    )(zb3, zb3, zb3, conv_w, hs, g3, aneg, dtb, mf, mb)
    return y, gb


def _lane_expand(cols, first):
    c = cols.shape[0]
    lane = lax.broadcasted_iota(jnp.int32, (c, LANES), 1)
    halves = []
    for h in range(0, DN_HEADS, 2):
        a = jnp.broadcast_to(cols[:, first + h:first + h + 1], (c, LANES))
        b = jnp.broadcast_to(cols[:, first + h + 1:first + h + 2], (c, LANES))
        halves.append(jnp.where(lane < DN_HEAD_DIM, a, b))
    return jnp.concatenate(halves, axis=1)


def _dn_pair_kernel(xf_ref, xb_ref, gf_ref, gb_ref, of_ref, ob_ref, sf_ref, sb_ref, *, nsub):
    c = DN_CHUNK
    w = DN_WIDTH

    @pl.when(pl.program_id(1) == 0)
    def _():
        sf_ref[...] = jnp.zeros_like(sf_ref)
        sb_ref[...] = jnp.zeros_like(sb_ref)

    r_cat = lax.broadcasted_iota(jnp.int32, (c, w), 0)
    s_cat = lax.broadcasted_iota(jnp.int32, (c, w), 1) & (DN_HEAD_DIM - 1)
    eye_cat = s_cat == r_cat
    rr = lax.broadcasted_iota(jnp.int32, (w, w), 0)
    cc = lax.broadcasted_iota(jnp.int32, (w, w), 1)
    head = (rr >> 6) == (cc >> 6)
    head_b = head.astype(BF16)
    m16 = (s_cat >> 4) == (r_cat >> 4)
    m32 = (s_cat >> 5) == (r_cat >> 5)
    off16 = m32 & jnp.logical_not(m16)
    off32 = jnp.logical_not(m32)
    eye_f = eye_cat.astype(F32)

    def bd(t):
        return jnp.concatenate([t] * DN_HEADS, axis=0) * head_b

    def mm(a, b):
        return jnp.dot(a, b, preferred_element_type=F32)

    chunks = [(0, i * c) for i in range(nsub)] + [(1, (nsub - 1 - i) * c) for i in range(nsub)]
    xrefs = (xf_ref, xb_ref)
    grefs = (gf_ref, gb_ref)
    orefs = (of_ref, ob_ref)
    srefs = (sf_ref, sb_ref)
    incl = (s_cat <= r_cat, s_cat >= r_cat)
    strict = (s_cat < r_cat, s_cat > r_cat)
    last_row = (c - 1, 0)

    pre = []
    for d, st in chunks:
        x = xrefs[d][st:st + c, :]
        q, k, v = x[:, :w], x[:, w:2 * w], x[:, 2 * w:]
        gbt = grefs[d][st:st + c, :]
        beta = _lane_expand(gbt, 0)
        gc = _lane_expand(gbt, DN_HEADS)
        grow = jnp.sum(jnp.where(eye_cat, gc, 0.0), axis=0, keepdims=True)
        decay = jnp.exp(jnp.where(incl[d], gc - grow, -jnp.inf))
        glast = gc[last_row[d]:last_row[d] + 1, :]
        egc = jnp.exp(gc)
        kb = k * beta
        pre.append(dict(d=d, st=st, q=q, k=k, kb=kb, vb=v * beta, decay=decay, glast=glast, egc=egc,
                        kdec=(k * jnp.exp(glast - gc)).astype(BF16)))

    kks = [lax.dot_general(jnp.concatenate([p["kb"], p["q"]], axis=0).astype(BF16), bd(p["k"].astype(BF16)),
                           (((1,), (1,)), ((), ())), preferred_element_type=F32) for p in pre]
    a = [jnp.where(strict[p["d"]], kk[:c] * p["decay"], 0.0) for p, kk in zip(pre, kks)]
    intra = [jnp.where(incl[p["d"]], kk[c:] * p["decay"], 0.0).astype(BF16) for p, kk in zip(pre, kks)]
    xm = [jnp.where(m16, -t, 0.0) for t in a]
    xm_b = [t.astype(BF16) for t in xm]
    x2_b = [mm(t, bd(t)).astype(BF16) for t in xm_b]
    x2_d = [bd(t) for t in x2_b]
    dinv = [eye_f + t for t in xm]
    r2 = [mm(jnp.concatenate([t.astype(BF16), p2], axis=0), s2) for t, p2, s2 in zip(dinv, x2_b, x2_d)]
    dinv = [t + r[:c] for t, r in zip(dinv, r2)]
    x4_b = [r[c:].astype(BF16) for r in r2]
    x4_d = [bd(t) for t in x4_b]
    r4 = [mm(jnp.concatenate([t.astype(BF16), p4], axis=0), s4) for t, p4, s4 in zip(dinv, x4_b, x4_d)]
    dinv = [t + r[:c] for t, r in zip(dinv, r4)]
    x8_d = [bd(r[c:].astype(BF16)) for r in r4]
    dinv = [t + mm(t.astype(BF16), s8) for t, s8 in zip(dinv, x8_d)]
    dinv_b = [t.astype(BF16) for t in dinv]
    n32 = [bd(mm(jnp.where(off16, t, 0.0).astype(BF16), bd(db)).astype(BF16)) for t, db in zip(a, dinv_b)]
    t32 = [t - mm(db, n) for t, db, n in zip(dinv, dinv_b, n32)]
    t32_b = [t.astype(BF16) for t in t32]
    n64 = [bd(mm(jnp.where(off32, t, 0.0).astype(BF16), bd(tb)).astype(BF16)) for t, tb in zip(a, t32_b)]
    t_cat = [(t - mm(tb, n)).astype(BF16) for t, tb, n in zip(t32, t32_b, n64)]
    uw = [mm(tc, jnp.concatenate([bd(p["vb"].astype(BF16)), bd((p["kb"] * p["egc"]).astype(BF16))], axis=1))
          for tc, p in zip(t_cat, pre)]
    uw_b = [t.astype(BF16) for t in uw]
    pn = [lax.dot_general(p["kdec"], t, (((0,), (0,)), ((), ())), preferred_element_type=F32)
          for p, t in zip(pre, uw_b)]
    qo = [mm(it, jnp.concatenate([bd(t[:, :w]), bd(t[:, w:])], axis=1)) for it, t in zip(intra, uw_b)]
    lhs = [jnp.concatenate([n[:, w:].astype(BF16) * head_b, (p["q"] * p["egc"] - o[:, w:]).astype(BF16)], axis=0)
           for n, o, p in zip(pn, qo, pre)]
    for step in range(nsub):
        for d in range(2):
            i = d * nsub + step
            p = pre[i]
            state = srefs[d][...]
            r = jnp.dot(lhs[i], state.astype(BF16), preferred_element_type=F32)
            orefs[d][p["st"]:p["st"] + c, :] = r[w:] + qo[i][:, :w]
            srefs[d][...] = state * jnp.exp(p["glast"]) - r[:w] + jnp.where(head, pn[i][:, :w], 0.0)


def _dn_chunk(y, gb, bsz, seqlen, ch):
    nsub = ch // DN_CHUNK
    nblk = seqlen // ch
    fwd = lambda b, j: (b, j, 0)
    bwd = lambda b, j: (b, nblk - 1 - j, 0)
    o_f, o_b = pl.pallas_call(
        functools.partial(_dn_pair_kernel, nsub=nsub),
        grid=(bsz, nblk),
        in_specs=[pl.BlockSpec((None, ch, ZB), fwd), pl.BlockSpec((None, ch, ZB), bwd),
                  pl.BlockSpec((None, None, ch, LANES), lambda b, j: (0, b, j, 0)),
                  pl.BlockSpec((None, None, ch, LANES), lambda b, j: (1, b, nblk - 1 - j, 0))],
        out_specs=[pl.BlockSpec((None, ch, DN_WIDTH), fwd), pl.BlockSpec((None, ch, DN_WIDTH), bwd)],
        out_shape=[jax.ShapeDtypeStruct((bsz, seqlen, DN_WIDTH), F32)] * 2,
        scratch_shapes=[pltpu.VMEM((DN_WIDTH, DN_WIDTH), F32)] * 2,
        compiler_params=_params(("parallel", "arbitrary")),
        name="deltanet_chunks",
    )(y, y, gb, gb)
    return o_f.reshape(bsz * seqlen, DN_WIDTH), o_b.reshape(bsz * seqlen, DN_WIDTH)


CV_HALO = 2 * SUBLANES
CV_PAD = (CONV_WIDTH - 1) // 2


def _conv_kernel(x_ref, xp_ref, xn_ref, dw_ref, bias_ref, lng_ref, lnb_ref, o_ref, buf_ref, shift_ref, *, tl):
    i = pl.program_id(1)
    nt = pl.num_programs(1)

    def glu(t):
        return t[:, :CONV_CH] * _sigmoid(t[:, CONV_CH:])

    buf_ref[0:CV_HALO, :] = jnp.where(i > 0, glu(xp_ref[...]), 0.0)
    buf_ref[CV_HALO:CV_HALO + tl, :] = glu(x_ref[...])
    buf_ref[CV_HALO + tl:, :] = jnp.where(i < nt - 1, glu(xn_ref[...]), 0.0)
    acc = jnp.zeros((tl, CONV_CH), F32) + bias_ref[...]
    first = CV_HALO - CV_PAD
    span = -(-(first + CONV_WIDTH) // SUBLANES) * SUBLANES - SUBLANES
    for sub in range(SUBLANES):
        shift_ref[...] = buf_ref[sub:sub + tl + span, :]
        for base in range(0, span + 1, SUBLANES):
            j = base + sub - first
            if 0 <= j < CONV_WIDTH:
                acc = acc + dw_ref[j:j + 1, :] * shift_ref[base:base + tl, :]
    mu = jnp.mean(acc, axis=-1, keepdims=True)
    cen = acc - mu
    var = jnp.mean(cen * cen, axis=-1, keepdims=True)
    o_ref[...] = _silu(cen * lax.rsqrt(var + NORM_EPS) * lng_ref[...] + lnb_ref[...]).astype(BF16)


def _conformer_conv(glu_in, dw, bias, lng, lnb, bsz, seqlen, tl):
    x3 = glu_in.reshape(bsz, seqlen, 2 * CONV_CH)
    nt = seqlen // tl
    hb = tl // CV_HALO
    fixed = lambda b, i: (0, 0)
    return pl.pallas_call(
        functools.partial(_conv_kernel, tl=tl),
        grid=(bsz, nt),
        in_specs=[pl.BlockSpec((None, tl, 2 * CONV_CH), lambda b, i: (b, i, 0)),
                  pl.BlockSpec((None, CV_HALO, 2 * CONV_CH), lambda b, i: (b, jnp.maximum(i * hb - 1, 0), 0)),
                  pl.BlockSpec((None, CV_HALO, 2 * CONV_CH),
                               lambda b, i: (b, jnp.minimum((i + 1) * hb, nt * hb - 1), 0)),
                  pl.BlockSpec(dw.shape, fixed), pl.BlockSpec((1, CONV_CH), fixed),
                  pl.BlockSpec((1, CONV_CH), fixed), pl.BlockSpec((1, CONV_CH), fixed)],
        out_specs=pl.BlockSpec((None, tl, CONV_CH), lambda b, i: (b, i, 0)),
        out_shape=jax.ShapeDtypeStruct((bsz, seqlen, CONV_CH), BF16),
        scratch_shapes=[pltpu.VMEM((tl + 2 * CV_HALO, CONV_CH), F32),
                        pltpu.VMEM((tl + 2 * CV_HALO - SUBLANES, CONV_CH), F32)],
        compiler_params=_params(("parallel", "parallel")),
        name="conformer_conv",
    )(x3, x3, x3, dw, bias, lng, lnb).reshape(bsz * seqlen, CONV_CH)


def _out_proj_kernel(h_ref, oa_ref, of_ref, ob_ref, zg_ref, oc_ref, og_ref, hm_ref, w_ref, gain_ref, wr_ref,
                     out_ref, xn_ref, aff_ref):
    half = h_ref.shape[0] // 2
    for r0 in (0, half):
        rs = slice(r0, r0 + half)
        ob = of_ref[rs, :] + ob_ref[rs, :]
        ms = jnp.dot((ob * ob).astype(BF16), hm_ref[...], preferred_element_type=F32)
        obn = ob * lax.rsqrt(ms + NORM_EPS) * og_ref[...]
        ob2 = obn * _silu(zg_ref[rs, :])
        mix = jnp.concatenate([oa_ref[rs, :], ob2.astype(BF16), oc_ref[rs, :]], axis=1)
        x = h_ref[rs, :] + jnp.dot(mix, w_ref[...], preferred_element_type=F32)
        out_ref[rs, :] = x
        ms = jnp.mean(x * x, axis=-1, keepdims=True)
        xn = x * lax.rsqrt(ms + NORM_EPS) * gain_ref[...]
        xn_hi = xn.astype(BF16)
        xn_ref[rs, :] = xn_hi
        xn_lo = (xn - xn_hi.astype(F32)).astype(BF16)
        logits = (jnp.dot(xn_hi, wr_ref[0], preferred_element_type=F32)
                  + (jnp.dot(xn_lo, wr_ref[0], preferred_element_type=F32)
                     + jnp.dot(xn_hi, wr_ref[1], preferred_element_type=F32)))
        lane = lax.broadcasted_iota(jnp.int32, logits.shape, 1)
        logits = jnp.where(lane < N_EXPERTS, logits, -jnp.inf)
        m = jnp.max(logits, axis=-1, keepdims=True)
        e = jnp.exp(logits - m)
        aff = e / jnp.sum(e, axis=-1, keepdims=True)
        aff_ref[:, rs] = jnp.transpose(aff)[:N_EXPERTS, :]


def _out_proj_route(h2, oa, o_f, o_b, zg, oc, og, hm, w, gain, wr2, tm):
    n = h2.shape[0]
    row = lambda i: (i, 0)
    fixed = lambda i: (0, 0)
    return pl.pallas_call(
        _out_proj_kernel,
        grid=(n // tm,),
        in_specs=[pl.BlockSpec((tm, D_MODEL), row), pl.BlockSpec((tm, ATT_Q), row),
                  pl.BlockSpec((tm, DN_WIDTH), row), pl.BlockSpec((tm, DN_WIDTH), row),
                  pl.BlockSpec((tm, DN_WIDTH), row),
                  pl.BlockSpec((tm, CONV_CH), row), pl.BlockSpec((1, DN_WIDTH), fixed),
                  pl.BlockSpec(hm.shape, fixed), pl.BlockSpec(w.shape, fixed),
                  pl.BlockSpec((1, D_MODEL), fixed), pl.BlockSpec((2, D_MODEL, LANES), lambda i: (0, 0, 0))],
        out_specs=[pl.BlockSpec((tm, D_MODEL), row), pl.BlockSpec((tm, D_MODEL), row),
                   pl.BlockSpec((N_EXPERTS, tm), lambda i: (0, i))],
        out_shape=[jax.ShapeDtypeStruct((n, D_MODEL), F32), jax.ShapeDtypeStruct((n, D_MODEL), BF16),
                   jax.ShapeDtypeStruct((N_EXPERTS, n), F32)],
        compiler_params=_params(("parallel",)),
        name="out_proj_route",
    )(h2, oa, o_f, o_b, zg, oc, og, hm, w, gain, wr2)


MOE_TILE = 256
MOE_ALIGN = 2 * SUBLANES
MOE_WIN = 64
MOE_PAD = 1024
MOE_FFN_TILE = 1024
FF_CHUNK = 256
MOE_UNSELECTED = -64.0


def _select_kernel(aff_ref, tri_ref, val_ref, cnt_ref, *, cap, tile):
    ne, n = aff_ref.shape
    nt = n // tile
    capf = float(cap)

    def bits_of(x):
        return lax.bitcast_convert_type(x, jnp.int32)

    def search(i, thr):
        cand = thr | jnp.left_shift(jnp.int32(1), 30 - i)
        cnt = jnp.sum((bits_of(aff_ref[...]) >= cand).astype(F32), axis=1, keepdims=True)
        return jnp.where(cnt >= capf, cand, thr)

    thr = lax.fori_loop(0, 31, search, jnp.zeros((ne, 1), jnp.int32))
    n_gt = jnp.sum((bits_of(aff_ref[...]) > thr).astype(F32), axis=1, keepdims=True)
    need = capf - n_gt
    lane = lax.broadcasted_iota(jnp.int32, (ne, LANES), 1)

    def tile_body(j, carry):
        eq_before, cnt_acc = carry
        off = pl.multiple_of(j * tile, tile)
        b = bits_of(aff_ref[:, pl.ds(off, tile)])
        gt = b > thr
        eqf = (b == thr).astype(F32)
        eq_rank = eq_before + jnp.dot(eqf.astype(BF16), tri_ref[...], preferred_element_type=F32)
        self_ = jnp.where(gt, 1.0, jnp.where(eq_rank <= need, eqf, 0.0))
        rank = jnp.dot(self_.astype(BF16), tri_ref[...], preferred_element_type=F32)
        val_ref[:, pl.ds(off, tile)] = jnp.where(self_ > 0.0, rank, MOE_UNSELECTED)
        cnt = jnp.sum(self_, axis=1, keepdims=True)
        return (eq_before + jnp.sum(eqf, axis=1, keepdims=True), cnt_acc + jnp.where(lane == j, cnt, 0.0))

    init = (jnp.zeros((ne, 1), F32), jnp.zeros((ne, LANES), F32))
    _, cnt_acc = lax.fori_loop(0, nt, tile_body, init, unroll=math.gcd(nt, 4))
    cnt_ref[...] = cnt_acc


def _select(aff_t, cap, tile):
    ne, n = aff_t.shape
    assert n // tile <= LANES
    tri = jnp.asarray(np.triu(np.ones((tile, tile), np.float32)), dtype=BF16)
    return pl.pallas_call(
        functools.partial(_select_kernel, cap=cap, tile=tile),
        out_shape=[jax.ShapeDtypeStruct((ne, n), F32), jax.ShapeDtypeStruct((ne, LANES), F32)],
        compiler_params=pltpu.CompilerParams(vmem_limit_bytes=VMEM_LIMIT),
        name="moe_select",
    )(aff_t, tri)


def _moe_plan(cnt, nt):
    c = cnt[:, :nt].astype(jnp.int32).T
    starts = jnp.concatenate([jnp.zeros((1, N_EXPERTS), jnp.int32), jnp.cumsum(c, axis=0)], axis=0)
    head = starts[:-1] & (MOE_ALIGN - 1)
    kmax = jnp.maximum(jnp.max((head + c + MOE_WIN - 1) // MOE_WIN, axis=1), 1).astype(jnp.int32)
    w = jnp.arange(MOE_WIN, dtype=jnp.int32)
    tgt = (w[None, None, :] + 1 - head[:, :, None]).astype(F32).reshape(nt, 1, N_EXPERTS * MOE_WIN)
    return starts.reshape(-1), kmax, tgt


def _expand_matrix():
    e = np.arange(N_EXPERTS * MOE_WIN) // MOE_WIN
    return jnp.asarray((np.arange(N_EXPERTS)[:, None] == e[None, :]).astype(np.float32), dtype=BF16)


def _slot_onehot(val_ref, eexp_ref):
    return lax.dot_general(val_ref[...].astype(BF16), eexp_ref[...], (((0,), (0,)), ((), ())),
                           preferred_element_type=F32)


def _dispatch_kernel(start_ref, kmax_ref, xn_ref, val_ref, tgt_ref, eexp_ref, xe_ref, stage, carry, sem):
    j = pl.program_id(0)
    nt = pl.num_programs(0)
    slot = lax.rem(j, 2)
    ne = N_EXPERTS

    cap = xe_ref.shape[1] - MOE_PAD

    @pl.when(j == 0)
    def _():
        carry[...] = jnp.zeros_like(carry)
        stage[0, 0:MOE_PAD, :] = jnp.zeros((MOE_PAD, D_MODEL), BF16)
        fills = [pltpu.make_async_copy(stage.at[0, pl.ds(0, MOE_PAD)], xe_ref.at[e, pl.ds(cap, MOE_PAD)], sem.at[0])
                 for e in range(ne)]
        for f in fills:
            f.start()
        for f in fills:
            f.wait()

    def window_copy(sl, e, row0):
        return pltpu.make_async_copy(stage.at[sl, pl.ds(e * MOE_WIN, MOE_WIN)],
                                     xe_ref.at[e, pl.ds(row0, MOE_WIN)], sem.at[sl])

    def wait_windows(sl):
        for e in range(ne):
            window_copy(sl, e, 0).wait()

    rep = _slot_onehot(val_ref, eexp_ref)
    xn = xn_ref[...]
    row = lax.broadcasted_iota(jnp.int32, (MOE_ALIGN, D_MODEL), 0)

    def block(k, _):
        @pl.when(k > 0)
        def _():
            wait_windows(slot)

        lo = k * MOE_WIN
        pt = (rep == tgt_ref[...] + lo.astype(F32)).astype(BF16)
        comp = lax.dot_general(pt, xn, (((0,), (0,)), ((), ())), preferred_element_type=F32)
        stage[slot] = comp.astype(BF16)
        for e in range(ne):
            s = start_ref[j * ne + e]
            head = s & (MOE_ALIGN - 1)
            r0 = e * MOE_WIN

            @pl.when(k == 0)
            def _():
                fresh = stage[slot, r0:r0 + MOE_ALIGN, :]
                kept = carry[e * MOE_ALIGN:(e + 1) * MOE_ALIGN, :]
                stage[slot, r0:r0 + MOE_ALIGN, :] = jnp.where(row < head, kept, fresh)

            nxt = (head + start_ref[(j + 1) * ne + e] - s) & (-MOE_ALIGN)

            @pl.when((nxt >= lo) & (nxt < lo + MOE_WIN))
            def _():
                off = pl.multiple_of(nxt - lo, MOE_ALIGN)
                carry[e * MOE_ALIGN:(e + 1) * MOE_ALIGN, :] = stage[slot, pl.ds(r0 + off, MOE_ALIGN), :]

        @pl.when((k == 0) & (j > 0))
        def _():
            wait_windows(1 - slot)

        for e in range(ne):
            base = pl.multiple_of((start_ref[j * ne + e] & (-MOE_ALIGN)) + lo, MOE_ALIGN)
            window_copy(slot, e, base).start()
        return 0

    lax.fori_loop(0, kmax_ref[j], block, 0)

    @pl.when(j == nt - 1)
    def _():
        wait_windows(slot)


def _dispatch(xn, val, starts, kmax, tgt, eexp, cap, tile):
    n = xn.shape[0]
    nt = n // tile
    rows = N_EXPERTS * MOE_WIN
    return pl.pallas_call(
        _dispatch_kernel,
        grid_spec=pltpu.PrefetchScalarGridSpec(
            num_scalar_prefetch=2, grid=(nt,),
            in_specs=[pl.BlockSpec((tile, D_MODEL), lambda j, s, k: (j, 0)),
                      pl.BlockSpec((N_EXPERTS, tile), lambda j, s, k: (0, j)),
                      pl.BlockSpec((None, 1, rows), lambda j, s, k: (j, 0, 0)),
                      pl.BlockSpec((N_EXPERTS, rows), lambda j, s, k: (0, 0))],
            out_specs=pl.BlockSpec(memory_space=pl.ANY),
            scratch_shapes=[pltpu.VMEM((2, rows, D_MODEL), BF16),
                            pltpu.VMEM((N_EXPERTS * MOE_ALIGN, D_MODEL), BF16),
                            pltpu.SemaphoreType.DMA((2,))]),
        out_shape=jax.ShapeDtypeStruct((N_EXPERTS, cap + MOE_PAD, D_MODEL), BF16),
        compiler_params=_params(("arbitrary",)),
        name="moe_dispatch",
    )(starts, kmax, xn, val, tgt, eexp)


def _expert_kernel(x_ref, wr_ref, wg_ref, wu_ref, wd_ref, y_ref, *, npad):
    e = pl.program_id(0)
    i = pl.program_id(1)

    @pl.when(i >= npad)
    def _():
        x = x_ref[...]
        logits = (jnp.dot(x, wr_ref[0], preferred_element_type=F32)
                  + jnp.dot(x, wr_ref[1], preferred_element_type=F32))
        lane = lax.broadcasted_iota(jnp.int32, logits.shape, 1)
        logits = jnp.where(lane < N_EXPERTS, logits, -jnp.inf)
        ex = jnp.exp(logits - jnp.max(logits, axis=-1, keepdims=True))
        gate = (jnp.sum(jnp.where(lane == e, ex, 0.0), axis=-1, keepdims=True)
                / jnp.sum(ex, axis=-1, keepdims=True))
        hid = []
        for c0 in range(0, EXPERT_FF, FF_CHUNK):
            hg = jnp.dot(x, wg_ref[:, c0:c0 + FF_CHUNK], preferred_element_type=F32)
            hu = jnp.dot(x, wu_ref[:, c0:c0 + FF_CHUNK], preferred_element_type=F32)
            hid.append((_silu(hg) * hu).astype(BF16))
        hid = jnp.concatenate(hid, axis=1)
        y_ref[...] = (jnp.dot(hid, wd_ref[...], preferred_element_type=F32) * gate).astype(BF16)

    @pl.when(i < npad)
    def _():
        y_ref[...] = jnp.zeros_like(y_ref)


def _expert_ffn(xe, wr2, wg, wu, wd, layer, cap, tc):
    ne, rows, _ = xe.shape
    ntile = cap // tc
    npad = rows // tc - ntile
    wspec = lambda shape: pl.BlockSpec((None, None) + shape, lambda e, i: (layer, e, 0, 0))
    return pl.pallas_call(
        functools.partial(_expert_kernel, npad=npad),
        grid=(ne, rows // tc),
        in_specs=[pl.BlockSpec((None, tc, D_MODEL), lambda e, i: (e, jnp.maximum(i - npad, 0), 0)),
                  pl.BlockSpec(wr2.shape, lambda e, i: (0, 0, 0)),
                  wspec((D_MODEL, EXPERT_FF)), wspec((D_MODEL, EXPERT_FF)), wspec((EXPERT_FF, D_MODEL))],
        out_specs=pl.BlockSpec((None, tc, D_MODEL),
                               lambda e, i: (e, jnp.where(i < npad, ntile + i, i - npad), 0)),
        out_shape=jax.ShapeDtypeStruct((ne, rows, D_MODEL), BF16),
        compiler_params=_params(("parallel", "parallel")),
        name="expert_ffn",
    )(xe, wr2, wg, wu, wd)


COMBINE_SUB = 2


def _combine_kernel(start_ref, kmax_ref, h_ref, p_ref, val_ref, tgt_ref, eexp_ref, gain_ref, wg_ref, wp_ref,
                    ye_ref, out_ref, stage, sem, *, tile):
    j = pl.program_id(0)
    nstep = pl.num_programs(0)
    slot = lax.rem(j, 2)
    ne = N_EXPERTS
    subs = range(COMBINE_SUB)

    def window_copy(sl, sub, e, row0):
        buf = sl * COMBINE_SUB + sub
        return pltpu.make_async_copy(ye_ref.at[e, pl.ds(row0, MOE_WIN)],
                                     stage.at[buf, pl.ds(e * MOE_WIN, MOE_WIN)], sem.at[buf])

    def fetch(sl, sub, tile_idx, lo):
        for e in range(ne):
            base = pl.multiple_of((start_ref[tile_idx * ne + e] & (-MOE_ALIGN)) + lo, MOE_ALIGN)
            window_copy(sl, sub, e, base).start()

    def wait_windows(sl, sub):
        for e in range(ne):
            window_copy(sl, sub, e, 0).wait()

    @pl.when(j == 0)
    def _():
        for sub in subs:
            fetch(slot, sub, sub, 0)

    @pl.when(j + 1 < nstep)
    def _():
        for sub in subs:
            fetch(1 - slot, sub, (j + 1) * COMBINE_SUB + sub, 0)

    reps = [lax.dot_general(val_ref[:, sub * tile:(sub + 1) * tile].astype(BF16), eexp_ref[...],
                            (((0,), (0,)), ((), ())), preferred_element_type=F32) for sub in subs]
    pts = [(reps[sub] == tgt_ref[sub]).astype(BF16) for sub in subs]
    for sub in subs:
        wait_windows(slot, sub)
    accs = [h_ref[sub * tile:(sub + 1) * tile, :]
            + jnp.dot(pts[sub], stage[slot * COMBINE_SUB + sub], preferred_element_type=F32) for sub in subs]
    for sub in subs:
        t = j * COMBINE_SUB + sub

        def extra(k, acc, sub=sub, t=t):
            lo = k * MOE_WIN
            fetch(slot, sub, t, lo)
            wait_windows(slot, sub)
            pk = (reps[sub] == tgt_ref[sub] + lo.astype(F32)).astype(BF16)
            return acc + jnp.dot(pk, stage[slot * COMBINE_SUB + sub], preferred_element_type=F32)

        accs[sub] = lax.fori_loop(1, kmax_ref[t], extra, accs[sub])
    x = jnp.concatenate(accs, axis=0)
    ms = jnp.mean(x * x, axis=-1, keepdims=True)
    xn = (x * lax.rsqrt(ms + NORM_EPS) * gain_ref[...]).astype(BF16)
    gate = _sigmoid(jnp.dot(xn, wg_ref[...], preferred_element_type=F32))
    proj = jnp.dot(p_ref[...].astype(BF16), wp_ref[...], preferred_element_type=F32)
    out_ref[...] = x + gate * proj


def _combine_ple(h2, p3, layer, ye, val, starts, kmax, tgt, eexp, gain, wg, wp, tile):
    n = h2.shape[0]
    step = COMBINE_SUB * tile
    assert n % step == 0
    rows = N_EXPERTS * MOE_WIN
    fixed = lambda j, s, k: (0, 0)
    return pl.pallas_call(
        functools.partial(_combine_kernel, tile=tile),
        grid_spec=pltpu.PrefetchScalarGridSpec(
            num_scalar_prefetch=2, grid=(n // step,),
            in_specs=[pl.BlockSpec((step, D_MODEL), lambda j, s, k: (j, 0)),
                      pl.BlockSpec((None, step, PLE_DIM), lambda j, s, k: (layer, j, 0)),
                      pl.BlockSpec((N_EXPERTS, step), lambda j, s, k: (0, j)),
                      pl.BlockSpec((COMBINE_SUB, 1, rows), lambda j, s, k: (j, 0, 0)),
                      pl.BlockSpec((N_EXPERTS, rows), fixed),
                      pl.BlockSpec((1, D_MODEL), fixed), pl.BlockSpec(wg.shape, fixed),
                      pl.BlockSpec(wp.shape, fixed),
                      pl.BlockSpec(memory_space=pl.ANY)],
            out_specs=pl.BlockSpec((step, D_MODEL), lambda j, s, k: (j, 0)),
            scratch_shapes=[pltpu.VMEM((2 * COMBINE_SUB, rows, D_MODEL), BF16),
                            pltpu.SemaphoreType.DMA((2 * COMBINE_SUB,))]),
        out_shape=jax.ShapeDtypeStruct((n, D_MODEL), F32),
        compiler_params=_params(("arbitrary",)),
        name="moe_combine_ple",
    )(starts, kmax, h2, p3, val, tgt, eexp, gain, wg, wp, ye)


def _relayout_w_in(w_in):
    o_beta = ZA + ZB + DN_WIDTH
    o_alpha = o_beta + 2 * DN_HEADS
    o_glu = o_alpha + 2 * DN_HEADS
    pieces = [w_in[:, :o_beta], w_in[:, o_glu:o_glu + 2 * CONV_CH]]
    for d in range(2):
        pieces.append(w_in[:, o_beta + d * DN_HEADS:o_beta + (d + 1) * DN_HEADS])
        pieces.append(w_in[:, o_alpha + d * DN_HEADS:o_alpha + (d + 1) * DN_HEADS])
    pieces.append(jnp.zeros((w_in.shape[0], LANES - 4 * DN_HEADS), w_in.dtype))
    return jnp.concatenate(pieces, axis=1).astype(BF16)


def _prep_layer(lw):
    (norm_mix, w_in, q_gain, k_gain, sink, dn_conv, dn_a_log, dn_dt_bias, dn_out_gain,
     cv_dw, cv_dw_bias, cv_ln_gain, cv_ln_bias, w_out, norm_ffn, w_router, w_gate, w_up, w_down,
     norm_ple, w_ple_gate, w_ple_proj) = lw
    w_perm = _relayout_w_in(w_in)
    hgain = jnp.concatenate([jnp.tile(q_gain, ATT_HEADS) * (ATT_HEAD_DIM ** -0.5),
                             jnp.tile(k_gain, ATT_KV_HEADS)]).reshape(1, -1)
    zeros4 = jnp.zeros((DN_HEADS,), F32)
    aneg = -jnp.exp(dn_a_log.astype(F32))
    aneg_row = jnp.concatenate([zeros4, aneg[0], zeros4, aneg[1]])
    dtb_row = jnp.concatenate([zeros4, dn_dt_bias[0], zeros4, dn_dt_bias[1]])
    pad = lambda r: jnp.pad(r, (0, LANES - r.shape[0])).reshape(1, LANES)
    wr = jnp.pad(w_router.astype(F32), ((0, 0), (0, LANES - N_EXPERTS)))
    wr_hi = wr.astype(BF16)
    wr2 = jnp.stack([wr_hi, (wr - wr_hi.astype(F32)).astype(BF16)])
    return dict(
        w_router2=wr2,
        norm_mix=norm_mix.reshape(1, -1), w_in=w_perm, hgain=hgain, sink=sink.astype(F32),
        dn_conv=dn_conv, aneg=pad(aneg_row), dtb=pad(dtb_row),
        dn_out_gain=jnp.tile(dn_out_gain, DN_HEADS).reshape(1, -1),
        cv_dw=cv_dw, cv_dw_bias=cv_dw_bias.reshape(1, -1), cv_ln_gain=cv_ln_gain.reshape(1, -1),
        cv_ln_bias=cv_ln_bias.reshape(1, -1), w_out=w_out.astype(BF16),
        norm_ffn=norm_ffn.reshape(1, -1),
        norm_ple=norm_ple.reshape(1, -1), w_ple_gate=w_ple_gate.astype(BF16), w_ple_proj=w_ple_proj.astype(BF16))


def _tiles(bsz, seqlen):
    n = bsz * seqlen
    return dict(tm=min(1024, n), tl=min(512, seqlen), ch=min(256, seqlen), tcv=min(1024, seqlen))


def _moe_ple(h2, xn, aff_t, p3, layer, pw):
    n = h2.shape[0]
    cap = CAPACITY_FACTOR * n // N_EXPERTS
    tile = min(MOE_TILE, n)
    val, cnt = _select(aff_t, cap, tile)
    starts, kmax, tgt = _moe_plan(cnt, n // tile)
    eexp = _expand_matrix()
    xe = _dispatch(xn, val, starts, kmax, tgt, eexp, cap, tile)
    ye = _expert_ffn(xe, pw["w_router2"], pw["w_gate"], pw["w_up"], pw["w_down"], layer, cap,
                     min(MOE_FFN_TILE, cap))
    return _combine_ple(h2, p3, layer, ye, val, starts, kmax, tgt, eexp, pw["norm_ple"], pw["w_ple_gate"],
                        pw["w_ple_proj"], tile)


def _layer(h2, p3, layer, pw, bsz, seqlen):
    t = _tiles(bsz, seqlen)
    hm_att = _head_mean_matrix(ATT_Q + ATT_KV, ATT_HEAD_DIM)
    hs_dn = _head_sum_matrix(2 * DN_WIDTH, DN_HEAD_DIM)
    hm_dn = _head_mean_matrix(DN_WIDTH, DN_HEAD_DIM)
    za, zb, zg, glu_in, gates = _in_proj(h2, pw["norm_mix"], pw["w_in"], hm_att, pw["hgain"], t["tm"])
    o_a = _attention(za, pw["sink"], bsz, seqlen)
    y, gb = _dn_prep(zb, gates, pw["dn_conv"], hs_dn, pw["aneg"], pw["dtb"], bsz, seqlen, t["tl"])
    o_f, o_b = _dn_chunk(y, gb, bsz, seqlen, t["ch"])
    o_c = _conformer_conv(glu_in, pw["cv_dw"], pw["cv_dw_bias"], pw["cv_ln_gain"], pw["cv_ln_bias"],
                          bsz, seqlen, t["tcv"])
    h2, xn, aff_t = _out_proj_route(h2, o_a, o_f, o_b, zg, o_c, pw["dn_out_gain"], hm_dn, pw["w_out"],
                                    pw["norm_ffn"], pw["w_router2"], t["tm"])
    return _moe_ple(h2, xn, aff_t, p3, layer, pw)


def _trunk(x, p, layer_weights):
    bsz, seqlen, _ = x.shape
    h2 = x.reshape(bsz * seqlen, D_MODEL)
    p3 = p.reshape(p.shape[0], bsz * seqlen, PLE_DIM)
    for i, pw in enumerate(layer_weights):
        h2 = _layer(h2, p3, i, pw, bsz, seqlen)
    return h2.reshape(bsz, seqlen, D_MODEL)


def kernel(x_prompt, x_sample, p_prompt, p_sample, norm_mix, w_in, q_gain, k_gain, sink, dn_conv, dn_a_log,
           dn_dt_bias, dn_out_gain, cv_dw, cv_dw_bias, cv_ln_gain, cv_ln_bias, w_out, norm_ffn, w_router,
           w_gate, w_up, w_down, norm_ple, w_ple_gate, w_ple_proj):
    weights = (norm_mix, w_in, q_gain, k_gain, sink, dn_conv, dn_a_log, dn_dt_bias, dn_out_gain,
               cv_dw, cv_dw_bias, cv_ln_gain, cv_ln_bias, w_out, norm_ffn, w_router, w_gate, w_up, w_down,
               norm_ple, w_ple_gate, w_ple_proj)
    depth = w_in.shape[0]
    experts = dict(w_gate=w_gate.astype(BF16), w_up=w_up.astype(BF16), w_down=w_down.astype(BF16))
    layer_weights = [dict(_prep_layer([w[i] for w in weights]), **experts) for i in range(depth)]
    return (_trunk(x_prompt, p_prompt, layer_weights), _trunk(x_sample, p_sample, layer_weights))
```

```python
import functools
import math

import numpy as np
import jax
import jax.numpy as jnp
from jax import lax
from jax.experimental import pallas as pl
from jax.experimental.pallas import tpu as pltpu

F32 = jnp.float32
BF16 = jnp.bfloat16

D_MODEL = 1024
ATT_HEADS = 8
ATT_KV_HEADS = 2
ATT_HEAD_DIM = 64
ATT_GROUP = ATT_HEADS // ATT_KV_HEADS
WINDOW = 128
ATT_BLOCK = 128
DN_HEADS = 4
DN_HEAD_DIM = 64
DN_WIDTH = DN_HEADS * DN_HEAD_DIM
DN_CHUNK = 64
CONV_CH = 256
CONV_WIDTH = 31
ATT_Q = ATT_HEADS * ATT_HEAD_DIM
ATT_KV = ATT_KV_HEADS * ATT_HEAD_DIM
N_EXPERTS = 16
CAPACITY_FACTOR = 2
EXPERT_FF = 1024
PLE_DIM = 256
NORM_EPS = 1e-6

LANES = 128
SUBLANES = 8
VMEM_LIMIT = 48 * 1024 * 1024

ZA = ATT_Q + 2 * ATT_KV
ZB = 3 * DN_WIDTH
ZW = ZA + ZB + DN_WIDTH + 2 * CONV_CH + LANES


def _params(sem):
    return pltpu.CompilerParams(dimension_semantics=sem, vmem_limit_bytes=VMEM_LIMIT)


def _head_mean_matrix(width, head):
    idx = np.arange(width) // head
    return jnp.asarray((idx[:, None] == idx[None, :]).astype(np.float32) / head, dtype=BF16)


def _head_sum_matrix(width, head):
    idx = np.arange(width) // head
    return jnp.asarray((idx[:, None] == idx[None, :]).astype(np.float32), dtype=BF16)


def _sigmoid(x):
    return 1.0 / (1.0 + jnp.exp(-x))


def _silu(x):
    return x * _sigmoid(x)


def _in_proj_kernel(x_ref, gain_ref, w_ref, hm_ref, hgain_ref, za_ref, zb_ref, zg_ref, glu_ref, gates_ref):
    x = x_ref[...]
    ms = jnp.mean(x * x, axis=-1, keepdims=True)
    a = (x * lax.rsqrt(ms + NORM_EPS) * gain_ref[...]).astype(BF16)
    z = jnp.dot(a, w_ref[...], preferred_element_type=F32)
    nqk = ATT_Q + ATT_KV
    qk = z[:, :nqk]
    hms = jnp.dot((qk * qk).astype(BF16), hm_ref[...], preferred_element_type=F32)
    za_ref[:, :nqk] = (qk * lax.rsqrt(hms + NORM_EPS) * hgain_ref[...]).astype(BF16)
    za_ref[:, nqk:] = z[:, nqk:ZA].astype(BF16)
    zb_ref[...] = z[:, ZA:ZA + ZB]
    zg_ref[...] = z[:, ZA + ZB:ZA + ZB + DN_WIDTH]
    glu_ref[...] = z[:, ZA + ZB + DN_WIDTH:ZA + ZB + DN_WIDTH + 2 * CONV_CH]
    gates_ref[...] = z[:, ZW - LANES:]


def _in_proj(h2, gain, w_perm, hm, hgain, tm):
    n = h2.shape[0]
    row = lambda i: (i, 0)
    fixed = lambda i: (0, 0)
    return pl.pallas_call(
        _in_proj_kernel,
        grid=(n // tm,),
        in_specs=[pl.BlockSpec((tm, D_MODEL), row), pl.BlockSpec((1, D_MODEL), fixed),
                  pl.BlockSpec((D_MODEL, ZW), fixed), pl.BlockSpec(hm.shape, fixed),
                  pl.BlockSpec(hgain.shape, fixed)],
        out_specs=[pl.BlockSpec((tm, ZA), row), pl.BlockSpec((tm, ZB), row), pl.BlockSpec((tm, DN_WIDTH), row),
                   pl.BlockSpec((tm, 2 * CONV_CH), row), pl.BlockSpec((tm, LANES), row)],
        out_shape=[jax.ShapeDtypeStruct((n, ZA), BF16), jax.ShapeDtypeStruct((n, ZB), F32),
                   jax.ShapeDtypeStruct((n, DN_WIDTH), F32), jax.ShapeDtypeStruct((n, 2 * CONV_CH), F32),
                   jax.ShapeDtypeStruct((n, LANES), F32)],
        compiler_params=_params(("parallel",)),
        name="in_proj",
    )(h2, gain, w_perm, hm, hgain)


ATT_MASKED = -1e30


def _attn_bias_table():
    i = np.arange(ATT_BLOCK)[:, None]
    c = np.arange(3 * ATT_BLOCK)[None, :]
    rel = c - ATT_BLOCK - i
    slopes = 2.0 ** (-8.0 * np.arange(1, ATT_HEADS + 1) / ATT_HEADS)
    table = np.empty((3, ATT_KV_HEADS, ATT_GROUP * ATT_BLOCK, 3 * ATT_BLOCK), np.float32)
    for variant in range(3):
        ok = np.abs(rel) <= WINDOW
        if variant == 0:
            ok = ok & (c >= ATT_BLOCK)
        if variant == 2:
            ok = ok & (c < 2 * ATT_BLOCK)
        for hd in range(ATT_HEADS):
            g, j = divmod(hd, ATT_GROUP)
            table[variant, g, j * ATT_BLOCK:(j + 1) * ATT_BLOCK] = np.where(ok, -slopes[hd] * np.abs(rel), ATT_MASKED)
    return jnp.asarray(table)


def _attn_kernel(sink_ref, q_ref, kvp_ref, kvo_ref, kvn_ref, bias_a_ref, bias_b_ref, o_ref):
    kv = jnp.concatenate([kvp_ref[...], kvo_ref[...], kvn_ref[...]], axis=0)
    hd_ = ATT_HEAD_DIM
    groups = range(ATT_KV_HEADS)
    heads = range(ATT_HEADS)
    rows = lambda t, hd: t[(hd % ATT_GROUP) * ATT_BLOCK:(hd % ATT_GROUP + 1) * ATT_BLOCK]
    work = []
    for blk, bias_ref in enumerate((bias_a_ref, bias_b_ref)):
        keys = kv[blk * ATT_BLOCK:(blk + 3) * ATT_BLOCK]
        ks = [keys[:, g * hd_:(g + 1) * hd_] for g in groups]
        vs = [keys[:, ATT_KV + g * hd_:ATT_KV + (g + 1) * hd_] for g in groups]
        q = q_ref[blk * ATT_BLOCK:(blk + 1) * ATT_BLOCK, :]
        qs = [jnp.concatenate([q[:, (g * ATT_GROUP + j) * hd_:(g * ATT_GROUP + j + 1) * hd_]
                               for j in range(ATT_GROUP)], axis=0) for g in groups]
        sg = [lax.dot_general(qs[g], ks[g], (((1,), (1,)), ((), ())), preferred_element_type=F32) + bias_ref[g]
              for g in groups]
        work.append((vs, [rows(sg[hd // ATT_GROUP], hd) for hd in heads]))
    m = [[jnp.maximum(jnp.max(s[hd], axis=-1, keepdims=True), sink_ref[hd]) for hd in heads] for _, s in work]
    e = [[jnp.exp(s[hd] - mb[hd]) for hd in heads] for (_, s), mb in zip(work, m)]
    den = [[jnp.sum(eb[hd], axis=-1, keepdims=True) + jnp.exp(sink_ref[hd] - mb[hd]) for hd in heads]
           for eb, mb in zip(e, m)]
    for blk, ((vs, _), eb, db) in enumerate(zip(work, e, den)):
        eg = [jnp.concatenate([eb[g * ATT_GROUP + j].astype(BF16) for j in range(ATT_GROUP)], axis=0)
              for g in groups]
        og = [jnp.dot(eg[g], vs[g], preferred_element_type=F32) for g in groups]
        for hd in heads:
            o_ref[blk * ATT_BLOCK:(blk + 1) * ATT_BLOCK, hd * hd_:(hd + 1) * hd_] = (
                rows(og[hd // ATT_GROUP], hd) / db[hd]).astype(BF16)


def _attention(za, sink, bsz, seqlen):
    nb = seqlen // ATT_BLOCK
    assert nb >= 2 and nb % 2 == 0
    npair = nb // 2
    za3 = za.reshape(bsz, seqlen, ZA)
    kvw = 2 * ATT_KV
    kvc = ATT_Q // kvw
    bias = _attn_bias_table()
    bias_spec = lambda pick: pl.BlockSpec((None,) + bias.shape[1:], lambda b, n: (pick(n), 0, 0, 0))
    return pl.pallas_call(
        _attn_kernel,
        grid=(bsz, npair),
        in_specs=[pl.BlockSpec(memory_space=pltpu.SMEM),
                  pl.BlockSpec((None, 2 * ATT_BLOCK, ATT_Q), lambda b, n: (b, n, 0)),
                  pl.BlockSpec((None, ATT_BLOCK, kvw), lambda b, n: (b, jnp.maximum(2 * n - 1, 0), kvc)),
                  pl.BlockSpec((None, 2 * ATT_BLOCK, kvw), lambda b, n: (b, n, kvc)),
                  pl.BlockSpec((None, ATT_BLOCK, kvw), lambda b, n: (b, jnp.minimum(2 * n + 2, nb - 1), kvc)),
                  bias_spec(lambda n: jnp.where(n == 0, 0, 1)),
                  bias_spec(lambda n: jnp.where(n == npair - 1, 2, 1))],
        out_specs=pl.BlockSpec((None, 2 * ATT_BLOCK, ATT_Q), lambda b, n: (b, n, 0)),
        out_shape=jax.ShapeDtypeStruct((bsz, seqlen, ATT_Q), BF16),
        compiler_params=_params(("parallel", "parallel")),
        name="window_attention",
    )(sink, za3, za3, za3, za3, bias, bias).reshape(bsz * seqlen, ATT_Q)


DN_HALO = SUBLANES


def _dn_prep_kernel(x_ref, xp_ref, xn_ref, cw_ref, hs_ref, g_ref, aneg_ref, dtb_ref, mf_ref, mb_ref,
                    y_ref, gb_ref, buf_ref, *, tl):
    i = pl.program_id(1)
    nt = pl.num_programs(1)
    buf_ref[0:DN_HALO, :] = jnp.where(i > 0, xp_ref[...], 0.0)
    buf_ref[DN_HALO:DN_HALO + tl, :] = x_ref[...]
    buf_ref[DN_HALO + tl:, :] = jnp.where(i < nt - 1, xn_ref[...], 0.0)
    y = (cw_ref[0:1, :] * buf_ref[DN_HALO - 1:DN_HALO - 1 + tl, :]
         + cw_ref[1:2, :] * buf_ref[DN_HALO:DN_HALO + tl, :]
         + cw_ref[2:3, :] * buf_ref[DN_HALO + 1:DN_HALO + 1 + tl, :])
    y = _silu(y)
    qk = y[:, :2 * DN_WIDTH]
    ss = jnp.dot((qk * qk).astype(BF16), hs_ref[...], preferred_element_type=F32)
    lane = lax.broadcasted_iota(jnp.int32, (tl, 2 * DN_WIDTH), 1)
    scale = jnp.where(lane < DN_WIDTH, DN_HEAD_DIM ** -0.5, 1.0)
    y_ref[:, :2 * DN_WIDTH] = qk * lax.rsqrt(ss + NORM_EPS) * scale
    y_ref[:, 2 * DN_WIDTH:] = y[:, 2 * DN_WIDTH:]
    raw = g_ref[...]
    col = lax.broadcasted_iota(jnp.int32, (tl, LANES), 1)
    is_beta = (col & DN_HEADS) == 0
    t = raw + dtb_ref[...]
    softplus = jnp.maximum(t, 0.0) + jnp.log(1.0 + jnp.exp(-jnp.abs(t)))
    vals = jnp.where(is_beta, _sigmoid(raw), aneg_ref[...] * softplus)
    v_hi = vals.astype(BF16)
    r1 = vals - v_hi.astype(F32)
    v_mid = r1.astype(BF16)
    v_lo = (r1 - v_mid.astype(F32)).astype(BF16)
    terms = jnp.concatenate([v_hi, v_mid, v_lo], axis=1)
    cf3 = jnp.dot(mf_ref[...], terms, preferred_element_type=F32)
    cb3 = jnp.dot(mb_ref[...], terms, preferred_element_type=F32)
    cf = cf3[:, :LANES] + (cf3[:, LANES:2 * LANES] + cf3[:, 2 * LANES:])
    cb = cb3[:, :LANES] + (cb3[:, LANES:2 * LANES] + cb3[:, 2 * LANES:])
    gb_ref[0] = jnp.where(is_beta, vals, cf)
    gb_ref[1] = pltpu.roll(jnp.where(is_beta, vals, cb), LANES - 2 * DN_HEADS, axis=1)


def _dn_prep(zb, gates, conv_w, hs, aneg, dtb, bsz, seqlen, tl):
    zb3 = zb.reshape(bsz, seqlen, ZB)
    g3 = gates.reshape(bsz, seqlen, LANES)
    nt = seqlen // tl
    hb = tl // DN_HALO
    ch = np.arange(tl) // DN_CHUNK
    same = ch[:, None] == ch[None, :]
    pos = np.arange(tl)
    mf = jnp.asarray((same & (pos[None, :] <= pos[:, None])).astype(np.float32), dtype=BF16)
    mb = jnp.asarray((same & (pos[None, :] >= pos[:, None])).astype(np.float32), dtype=BF16)
    fixed = lambda b, i: (0, 0)
    y, gb = pl.pallas_call(
        functools.partial(_dn_prep_kernel, tl=tl),
        grid=(bsz, nt),
        in_specs=[pl.BlockSpec((None, tl, ZB), lambda b, i: (b, i, 0)),
                  pl.BlockSpec((None, DN_HALO, ZB), lambda b, i: (b, jnp.maximum(i * hb - 1, 0), 0)),
                  pl.BlockSpec((None, DN_HALO, ZB), lambda b, i: (b, jnp.minimum((i + 1) * hb, nt * hb - 1), 0)),
                  pl.BlockSpec(conv_w.shape, fixed), pl.BlockSpec(hs.shape, fixed),
                  pl.BlockSpec((None, tl, LANES), lambda b, i: (b, i, 0)),
                  pl.BlockSpec((1, LANES), fixed), pl.BlockSpec((1, LANES), fixed),
                  pl.BlockSpec((tl, tl), fixed), pl.BlockSpec((tl, tl), fixed)],
        out_specs=[pl.BlockSpec((None, tl, ZB), lambda b, i: (b, i, 0)),
                   pl.BlockSpec((2, None, tl, LANES), lambda b, i: (0, b, i, 0))],
        out_shape=[jax.ShapeDtypeStruct((bsz, seqlen, ZB), F32),
                   jax.ShapeDtypeStruct((2, bsz, seqlen, LANES), F32)],
        scratch_shapes=[pltpu.VMEM((tl + 2 * DN_HALO, ZB), F32)],
        compiler_params=_params(("parallel", "parallel")),
        name="deltanet_prep",
    )(zb3, zb3, zb3, conv_w, hs, g3, aneg, dtb, mf, mb)
    return y, gb


def _lane_expand(cols, first):
    c = cols.shape[0]
    lane = lax.broadcasted_iota(jnp.int32, (c, LANES), 1)
    halves = []
    for h in range(0, DN_HEADS, 2):
        a = jnp.broadcast_to(cols[:, first + h:first + h + 1], (c, LANES))
        b = jnp.broadcast_to(cols[:, first + h + 1:first + h + 2], (c, LANES))
        halves.append(jnp.where(lane < DN_HEAD_DIM, a, b))
    return jnp.concatenate(halves, axis=1)


def _dn_pair_kernel(xf_ref, xb_ref, gf_ref, gb_ref, of_ref, ob_ref, sf_ref, sb_ref, *, nsub):
    c = DN_CHUNK
    w = DN_WIDTH

    @pl.when(pl.program_id(1) == 0)
    def _():
        sf_ref[...] = jnp.zeros_like(sf_ref)
        sb_ref[...] = jnp.zeros_like(sb_ref)

    r_cat = lax.broadcasted_iota(jnp.int32, (c, w), 0)
    s_cat = lax.broadcasted_iota(jnp.int32, (c, w), 1) & (DN_HEAD_DIM - 1)
    eye_cat = s_cat == r_cat
    rr = lax.broadcasted_iota(jnp.int32, (w, w), 0)
    cc = lax.broadcasted_iota(jnp.int32, (w, w), 1)
    head = (rr >> 6) == (cc >> 6)
    head_b = head.astype(BF16)
    m16 = (s_cat >> 4) == (r_cat >> 4)
    m32 = (s_cat >> 5) == (r_cat >> 5)
    off16 = m32 & jnp.logical_not(m16)
    off32 = jnp.logical_not(m32)
    eye_f = eye_cat.astype(F32)

    def bd(t):
        return jnp.concatenate([t] * DN_HEADS, axis=0) * head_b

    def mm(a, b):
        return jnp.dot(a, b, preferred_element_type=F32)

    chunks = [(0, i * c) for i in range(nsub)] + [(1, (nsub - 1 - i) * c) for i in range(nsub)]
    xrefs = (xf_ref, xb_ref)
    grefs = (gf_ref, gb_ref)
    orefs = (of_ref, ob_ref)
    srefs = (sf_ref, sb_ref)
    incl = (s_cat <= r_cat, s_cat >= r_cat)
    strict = (s_cat < r_cat, s_cat > r_cat)
    last_row = (c - 1, 0)

    pre = []
    for d, st in chunks:
        x = xrefs[d][st:st + c, :]
        q, k, v = x[:, :w], x[:, w:2 * w], x[:, 2 * w:]
        gbt = grefs[d][st:st + c, :]
        beta = _lane_expand(gbt, 0)
        gc = _lane_expand(gbt, DN_HEADS)
        grow = jnp.sum(jnp.where(eye_cat, gc, 0.0), axis=0, keepdims=True)
        decay = jnp.exp(jnp.where(incl[d], gc - grow, -jnp.inf))
        glast = gc[last_row[d]:last_row[d] + 1, :]
        egc = jnp.exp(gc)
        kb = k * beta
        pre.append(dict(d=d, st=st, q=q, k=k, kb=kb, vb=v * beta, decay=decay, glast=glast, egc=egc,
                        kdec=(k * jnp.exp(glast - gc)).astype(BF16)))

    kks = [lax.dot_general(jnp.concatenate([p["kb"], p["q"]], axis=0).astype(BF16), bd(p["k"].astype(BF16)),
                           (((1,), (1,)), ((), ())), preferred_element_type=F32) for p in pre]
    a = [jnp.where(strict[p["d"]], kk[:c] * p["decay"], 0.0) for p, kk in zip(pre, kks)]
    intra = [jnp.where(incl[p["d"]], kk[c:] * p["decay"], 0.0).astype(BF16) for p, kk in zip(pre, kks)]
    xm = [jnp.where(m16, -t, 0.0) for t in a]
    xm_b = [t.astype(BF16) for t in xm]
    x2_b = [mm(t, bd(t)).astype(BF16) for t in xm_b]
    x2_d = [bd(t) for t in x2_b]
    dinv = [eye_f + t for t in xm]
    r2 = [mm(jnp.concatenate([t.astype(BF16), p2], axis=0), s2) for t, p2, s2 in zip(dinv, x2_b, x2_d)]
    dinv = [t + r[:c] for t, r in zip(dinv, r2)]
    x4_b = [r[c:].astype(BF16) for r in r2]
    x4_d = [bd(t) for t in x4_b]
    r4 = [mm(jnp.concatenate([t.astype(BF16), p4], axis=0), s4) for t, p4, s4 in zip(dinv, x4_b, x4_d)]
    dinv = [t + r[:c] for t, r in zip(dinv, r4)]
    x8_d = [bd(r[c:].astype(BF16)) for r in r4]
    dinv = [t + mm(t.astype(BF16), s8) for t, s8 in zip(dinv, x8_d)]
    dinv_b = [t.astype(BF16) for t in dinv]
    n32 = [bd(mm(jnp.where(off16, t, 0.0).astype(BF16), bd(db)).astype(BF16)) for t, db in zip(a, dinv_b)]
    t32 = [t - mm(db, n) for t, db, n in zip(dinv, dinv_b, n32)]
    t32_b = [t.astype(BF16) for t in t32]
    n64 = [bd(mm(jnp.where(off32, t, 0.0).astype(BF16), bd(tb)).astype(BF16)) for t, tb in zip(a, t32_b)]
    t_cat = [(t - mm(tb, n)).astype(BF16) for t, tb, n in zip(t32, t32_b, n64)]
    uw = [mm(tc, jnp.concatenate([bd(p["vb"].astype(BF16)), bd((p["kb"] * p["egc"]).astype(BF16))], axis=1))
          for tc, p in zip(t_cat, pre)]
    uw_b = [t.astype(BF16) for t in uw]
    pn = [lax.dot_general(p["kdec"], t, (((0,), (0,)), ((), ())), preferred_element_type=F32)
          for p, t in zip(pre, uw_b)]
    qo = [mm(it, jnp.concatenate([bd(t[:, :w]), bd(t[:, w:])], axis=1)) for it, t in zip(intra, uw_b)]
    lhs = [jnp.concatenate([n[:, w:].astype(BF16) * head_b, (p["q"] * p["egc"] - o[:, w:]).astype(BF16)], axis=0)
           for n, o, p in zip(pn, qo, pre)]
    for step in range(nsub):
        for d in range(2):
            i = d * nsub + step
            p = pre[i]
            state = srefs[d][...]
            r = jnp.dot(lhs[i], state.astype(BF16), preferred_element_type=F32)
            orefs[d][p["st"]:p["st"] + c, :] = r[w:] + qo[i][:, :w]
            srefs[d][...] = state * jnp.exp(p["glast"]) - r[:w] + jnp.where(head, pn[i][:, :w], 0.0)


def _dn_chunk(y, gb, bsz, seqlen, ch):
    nsub = ch // DN_CHUNK
    nblk = seqlen // ch
    fwd = lambda b, j: (b, j, 0)
    bwd = lambda b, j: (b, nblk - 1 - j, 0)
    o_f, o_b = pl.pallas_call(
        functools.partial(_dn_pair_kernel, nsub=nsub),
        grid=(bsz, nblk),
        in_specs=[pl.BlockSpec((None, ch, ZB), fwd), pl.BlockSpec((None, ch, ZB), bwd),
                  pl.BlockSpec((None, None, ch, LANES), lambda b, j: (0, b, j, 0)),
                  pl.BlockSpec((None, None, ch, LANES), lambda b, j: (1, b, nblk - 1 - j, 0))],
        out_specs=[pl.BlockSpec((None, ch, DN_WIDTH), fwd), pl.BlockSpec((None, ch, DN_WIDTH), bwd)],
        out_shape=[jax.ShapeDtypeStruct((bsz, seqlen, DN_WIDTH), F32)] * 2,
        scratch_shapes=[pltpu.VMEM((DN_WIDTH, DN_WIDTH), F32)] * 2,
        compiler_params=_params(("parallel", "arbitrary")),
        name="deltanet_chunks",
    )(y, y, gb, gb)
    return o_f.reshape(bsz * seqlen, DN_WIDTH), o_b.reshape(bsz * seqlen, DN_WIDTH)


CV_HALO = 2 * SUBLANES
CV_PAD = (CONV_WIDTH - 1) // 2


def _conv_kernel(x_ref, xp_ref, xn_ref, dw_ref, bias_ref, lng_ref, lnb_ref, o_ref, buf_ref, shift_ref, *, tl):
    i = pl.program_id(1)
    nt = pl.num_programs(1)

    def glu(t):
        return t[:, :CONV_CH] * _sigmoid(t[:, CONV_CH:])

    buf_ref[0:CV_HALO, :] = jnp.where(i > 0, glu(xp_ref[...]), 0.0)
    buf_ref[CV_HALO:CV_HALO + tl, :] = glu(x_ref[...])
    buf_ref[CV_HALO + tl:, :] = jnp.where(i < nt - 1, glu(xn_ref[...]), 0.0)
    acc = jnp.zeros((tl, CONV_CH), F32) + bias_ref[...]
    first = CV_HALO - CV_PAD
    span = -(-(first + CONV_WIDTH) // SUBLANES) * SUBLANES - SUBLANES
    for sub in range(SUBLANES):
        shift_ref[...] = buf_ref[sub:sub + tl + span, :]
        for base in range(0, span + 1, SUBLANES):
            j = base + sub - first
            if 0 <= j < CONV_WIDTH:
                acc = acc + dw_ref[j:j + 1, :] * shift_ref[base:base + tl, :]
    mu = jnp.mean(acc, axis=-1, keepdims=True)
    cen = acc - mu
    var = jnp.mean(cen * cen, axis=-1, keepdims=True)
    o_ref[...] = _silu(cen * lax.rsqrt(var + NORM_EPS) * lng_ref[...] + lnb_ref[...]).astype(BF16)


def _conformer_conv(glu_in, dw, bias, lng, lnb, bsz, seqlen, tl):
    x3 = glu_in.reshape(bsz, seqlen, 2 * CONV_CH)
    nt = seqlen // tl
    hb = tl // CV_HALO
    fixed = lambda b, i: (0, 0)
    return pl.pallas_call(
        functools.partial(_conv_kernel, tl=tl),
        grid=(bsz, nt),
        in_specs=[pl.BlockSpec((None, tl, 2 * CONV_CH), lambda b, i: (b, i, 0)),
                  pl.BlockSpec((None, CV_HALO, 2 * CONV_CH), lambda b, i: (b, jnp.maximum(i * hb - 1, 0), 0)),
                  pl.BlockSpec((None, CV_HALO, 2 * CONV_CH),
                               lambda b, i: (b, jnp.minimum((i + 1) * hb, nt * hb - 1), 0)),
                  pl.BlockSpec(dw.shape, fixed), pl.BlockSpec((1, CONV_CH), fixed),
                  pl.BlockSpec((1, CONV_CH), fixed), pl.BlockSpec((1, CONV_CH), fixed)],
        out_specs=pl.BlockSpec((None, tl, CONV_CH), lambda b, i: (b, i, 0)),
        out_shape=jax.ShapeDtypeStruct((bsz, seqlen, CONV_CH), BF16),
        scratch_shapes=[pltpu.VMEM((tl + 2 * CV_HALO, CONV_CH), F32),
                        pltpu.VMEM((tl + 2 * CV_HALO - SUBLANES, CONV_CH), F32)],
        compiler_params=_params(("parallel", "parallel")),
        name="conformer_conv",
    )(x3, x3, x3, dw, bias, lng, lnb).reshape(bsz * seqlen, CONV_CH)


def _out_proj_kernel(h_ref, oa_ref, of_ref, ob_ref, zg_ref, oc_ref, og_ref, hm_ref, w_ref, gain_ref, wr_ref,
                     out_ref, xn_ref, aff_ref):
    half = h_ref.shape[0] // 2
    for r0 in (0, half):
        rs = slice(r0, r0 + half)
        ob = of_ref[rs, :] + ob_ref[rs, :]
        ms = jnp.dot((ob * ob).astype(BF16), hm_ref[...], preferred_element_type=F32)
        obn = ob * lax.rsqrt(ms + NORM_EPS) * og_ref[...]
        ob2 = obn * _silu(zg_ref[rs, :])
        mix = jnp.concatenate([oa_ref[rs, :], ob2.astype(BF16), oc_ref[rs, :]], axis=1)
        x = h_ref[rs, :] + jnp.dot(mix, w_ref[...], preferred_element_type=F32)
        out_ref[rs, :] = x
        ms = jnp.mean(x * x, axis=-1, keepdims=True)
        xn = x * lax.rsqrt(ms + NORM_EPS) * gain_ref[...]
        xn_hi = xn.astype(BF16)
        xn_ref[rs, :] = xn_hi
        xn_lo = (xn - xn_hi.astype(F32)).astype(BF16)
        logits = (jnp.dot(xn_hi, wr_ref[0], preferred_element_type=F32)
                  + (jnp.dot(xn_lo, wr_ref[0], preferred_element_type=F32)
                     + jnp.dot(xn_hi, wr_ref[1], preferred_element_type=F32)))
        lane = lax.broadcasted_iota(jnp.int32, logits.shape, 1)
        logits = jnp.where(lane < N_EXPERTS, logits, -jnp.inf)
        m = jnp.max(logits, axis=-1, keepdims=True)
        e = jnp.exp(logits - m)
        aff = e / jnp.sum(e, axis=-1, keepdims=True)
        aff_ref[:, rs] = jnp.transpose(aff)[:N_EXPERTS, :]


def _out_proj_route(h2, oa, o_f, o_b, zg, oc, og, hm, w, gain, wr2, tm):
    n = h2.shape[0]
    row = lambda i: (i, 0)
    fixed = lambda i: (0, 0)
    return pl.pallas_call(
        _out_proj_kernel,
        grid=(n // tm,),
        in_specs=[pl.BlockSpec((tm, D_MODEL), row), pl.BlockSpec((tm, ATT_Q), row),
                  pl.BlockSpec((tm, DN_WIDTH), row), pl.BlockSpec((tm, DN_WIDTH), row),
                  pl.BlockSpec((tm, DN_WIDTH), row),
                  pl.BlockSpec((tm, CONV_CH), row), pl.BlockSpec((1, DN_WIDTH), fixed),
                  pl.BlockSpec(hm.shape, fixed), pl.BlockSpec(w.shape, fixed),
                  pl.BlockSpec((1, D_MODEL), fixed), pl.BlockSpec((2, D_MODEL, LANES), lambda i: (0, 0, 0))],
        out_specs=[pl.BlockSpec((tm, D_MODEL), row), pl.BlockSpec((tm, D_MODEL), row),
                   pl.BlockSpec((N_EXPERTS, tm), lambda i: (0, i))],
        out_shape=[jax.ShapeDtypeStruct((n, D_MODEL), F32), jax.ShapeDtypeStruct((n, D_MODEL), BF16),
                   jax.ShapeDtypeStruct((N_EXPERTS, n), F32)],
        compiler_params=_params(("parallel",)),
        name="out_proj_route",
    )(h2, oa, o_f, o_b, zg, oc, og, hm, w, gain, wr2)


MOE_TILE = 256
MOE_ALIGN = 2 * SUBLANES
MOE_WIN = 64
MOE_PAD = 1024
MOE_FFN_TILE = 1024
FF_CHUNK = 256
MOE_UNSELECTED = -64.0


def _select_kernel(aff_ref, tri_ref, val_ref, cnt_ref, *, cap, tile):
    ne, n = aff_ref.shape
    nt = n // tile
    capf = float(cap)

    def bits_of(x):
        return lax.bitcast_convert_type(x, jnp.int32)

    def search(i, thr):
        cand = thr | jnp.left_shift(jnp.int32(1), 30 - i)
        cnt = jnp.sum((bits_of(aff_ref[...]) >= cand).astype(F32), axis=1, keepdims=True)
        return jnp.where(cnt >= capf, cand, thr)

    thr = lax.fori_loop(0, 31, search, jnp.zeros((ne, 1), jnp.int32))
    n_gt = jnp.sum((bits_of(aff_ref[...]) > thr).astype(F32), axis=1, keepdims=True)
    need = capf - n_gt
    lane = lax.broadcasted_iota(jnp.int32, (ne, LANES), 1)

    def tile_body(j, carry):
        eq_before, cnt_acc = carry
        off = pl.multiple_of(j * tile, tile)
        b = bits_of(aff_ref[:, pl.ds(off, tile)])
        gt = b > thr
        eqf = (b == thr).astype(F32)
        eq_rank = eq_before + jnp.dot(eqf.astype(BF16), tri_ref[...], preferred_element_type=F32)
        self_ = jnp.where(gt, 1.0, jnp.where(eq_rank <= need, eqf, 0.0))
        rank = jnp.dot(self_.astype(BF16), tri_ref[...], preferred_element_type=F32)
        val_ref[:, pl.ds(off, tile)] = jnp.where(self_ > 0.0, rank, MOE_UNSELECTED)
        cnt = jnp.sum(self_, axis=1, keepdims=True)
        return (eq_before + jnp.sum(eqf, axis=1, keepdims=True), cnt_acc + jnp.where(lane == j, cnt, 0.0))

    init = (jnp.zeros((ne, 1), F32), jnp.zeros((ne, LANES), F32))
    _, cnt_acc = lax.fori_loop(0, nt, tile_body, init, unroll=math.gcd(nt, 4))
    cnt_ref[...] = cnt_acc


def _select(aff_t, cap, tile):
    ne, n = aff_t.shape
    assert n // tile <= LANES
    tri = jnp.asarray(np.triu(np.ones((tile, tile), np.float32)), dtype=BF16)
    return pl.pallas_call(
        functools.partial(_select_kernel, cap=cap, tile=tile),
        out_shape=[jax.ShapeDtypeStruct((ne, n), F32), jax.ShapeDtypeStruct((ne, LANES), F32)],
        compiler_params=pltpu.CompilerParams(vmem_limit_bytes=VMEM_LIMIT),
        name="moe_select",
    )(aff_t, tri)


def _moe_plan(cnt, nt):
    c = cnt[:, :nt].astype(jnp.int32).T
    starts = jnp.concatenate([jnp.zeros((1, N_EXPERTS), jnp.int32), jnp.cumsum(c, axis=0)], axis=0)
    head = starts[:-1] & (MOE_ALIGN - 1)
    kmax = jnp.maximum(jnp.max((head + c + MOE_WIN - 1) // MOE_WIN, axis=1), 1).astype(jnp.int32)
    w = jnp.arange(MOE_WIN, dtype=jnp.int32)
    tgt = (w[None, None, :] + 1 - head[:, :, None]).astype(F32).reshape(nt, 1, N_EXPERTS * MOE_WIN)
    return starts.reshape(-1), kmax, tgt


def _expand_matrix():
    e = np.arange(N_EXPERTS * MOE_WIN) // MOE_WIN
    return jnp.asarray((np.arange(N_EXPERTS)[:, None] == e[None, :]).astype(np.float32), dtype=BF16)


def _slot_onehot(val_ref, eexp_ref):
    return lax.dot_general(val_ref[...].astype(BF16), eexp_ref[...], (((0,), (0,)), ((), ())),
                           preferred_element_type=F32)


def _dispatch_kernel(start_ref, kmax_ref, xn_ref, val_ref, tgt_ref, eexp_ref, xe_ref, stage, carry, sem):
    j = pl.program_id(0)
    nt = pl.num_programs(0)
    slot = lax.rem(j, 2)
    ne = N_EXPERTS

    cap = xe_ref.shape[1] - MOE_PAD

    @pl.when(j == 0)
    def _():
        carry[...] = jnp.zeros_like(carry)
        stage[0, 0:MOE_PAD, :] = jnp.zeros((MOE_PAD, D_MODEL), BF16)
        fills = [pltpu.make_async_copy(stage.at[0, pl.ds(0, MOE_PAD)], xe_ref.at[e, pl.ds(cap, MOE_PAD)], sem.at[0])
                 for e in range(ne)]
        for f in fills:
            f.start()
        for f in fills:
            f.wait()

    def window_copy(sl, e, row0):
        return pltpu.make_async_copy(stage.at[sl, pl.ds(e * MOE_WIN, MOE_WIN)],
                                     xe_ref.at[e, pl.ds(row0, MOE_WIN)], sem.at[sl])

    def wait_windows(sl):
        for e in range(ne):
            window_copy(sl, e, 0).wait()

    rep = _slot_onehot(val_ref, eexp_ref)
    xn = xn_ref[...]
    row = lax.broadcasted_iota(jnp.int32, (MOE_ALIGN, D_MODEL), 0)

    def block(k, _):
        @pl.when(k > 0)
        def _():
            wait_windows(slot)

        lo = k * MOE_WIN
        pt = (rep == tgt_ref[...] + lo.astype(F32)).astype(BF16)
        comp = lax.dot_general(pt, xn, (((0,), (0,)), ((), ())), preferred_element_type=F32)
        stage[slot] = comp.astype(BF16)
        for e in range(ne):
            s = start_ref[j * ne + e]
            head = s & (MOE_ALIGN - 1)
            r0 = e * MOE_WIN
            kept = carry[e * MOE_ALIGN:(e + 1) * MOE_ALIGN, :]
            owned = row < jnp.where(k == 0, head, 0)
            stage[slot, r0:r0 + MOE_ALIGN, :] = jnp.where(owned, kept, stage[slot, r0:r0 + MOE_ALIGN, :])
            nxt = (head + start_ref[(j + 1) * ne + e] - s) & (-MOE_ALIGN)
            here = (nxt >= lo) & (nxt < lo + MOE_WIN)
            off = pl.multiple_of(jnp.clip(nxt - lo, 0, MOE_WIN - MOE_ALIGN), MOE_ALIGN)
            cand = stage[slot, pl.ds(r0 + off, MOE_ALIGN), :]
            carry[e * MOE_ALIGN:(e + 1) * MOE_ALIGN, :] = jnp.where(here, cand, kept)

        @pl.when((k == 0) & (j > 0))
        def _():
            wait_windows(1 - slot)

        for e in range(ne):
            base = pl.multiple_of((start_ref[j * ne + e] & (-MOE_ALIGN)) + lo, MOE_ALIGN)
            window_copy(slot, e, base).start()
        return 0

    lax.fori_loop(0, kmax_ref[j], block, 0)

    @pl.when(j == nt - 1)
    def _():
        wait_windows(slot)


def _dispatch(xn, val, starts, kmax, tgt, eexp, cap, tile):
    n = xn.shape[0]
    nt = n // tile
    rows = N_EXPERTS * MOE_WIN
    return pl.pallas_call(
        _dispatch_kernel,
        grid_spec=pltpu.PrefetchScalarGridSpec(
            num_scalar_prefetch=2, grid=(nt,),
            in_specs=[pl.BlockSpec((tile, D_MODEL), lambda j, s, k: (j, 0)),
                      pl.BlockSpec((N_EXPERTS, tile), lambda j, s, k: (0, j)),
                      pl.BlockSpec((None, 1, rows), lambda j, s, k: (j, 0, 0)),
                      pl.BlockSpec((N_EXPERTS, rows), lambda j, s, k: (0, 0))],
            out_specs=pl.BlockSpec(memory_space=pl.ANY),
            scratch_shapes=[pltpu.VMEM((2, rows, D_MODEL), BF16),
                            pltpu.VMEM((N_EXPERTS * MOE_ALIGN, D_MODEL), BF16),
                            pltpu.SemaphoreType.DMA((2,))]),
        out_shape=jax.ShapeDtypeStruct((N_EXPERTS, cap + MOE_PAD, D_MODEL), BF16),
        compiler_params=_params(("arbitrary",)),
        name="moe_dispatch",
    )(starts, kmax, xn, val, tgt, eexp)


def _expert_kernel(x_ref, wr_ref, wg_ref, wu_ref, wd_ref, y_ref, *, npad):
    e = pl.program_id(0)
    i = pl.program_id(1)

    @pl.when(i >= npad)
    def _():
        x = x_ref[...]
        logits = (jnp.dot(x, wr_ref[0], preferred_element_type=F32)
                  + jnp.dot(x, wr_ref[1], preferred_element_type=F32))
        lane = lax.broadcasted_iota(jnp.int32, logits.shape, 1)
        logits = jnp.where(lane < N_EXPERTS, logits, -jnp.inf)
        ex = jnp.exp(logits - jnp.max(logits, axis=-1, keepdims=True))
        gate = (jnp.sum(jnp.where(lane == e, ex, 0.0), axis=-1, keepdims=True)
                / jnp.sum(ex, axis=-1, keepdims=True))
        hid = []
        for c0 in range(0, EXPERT_FF, FF_CHUNK):
            hg = jnp.dot(x, wg_ref[:, c0:c0 + FF_CHUNK], preferred_element_type=F32)
            hu = jnp.dot(x, wu_ref[:, c0:c0 + FF_CHUNK], preferred_element_type=F32)
            hid.append((_silu(hg) * hu).astype(BF16))
        hid = jnp.concatenate(hid, axis=1)
        y_ref[...] = (jnp.dot(hid, wd_ref[...], preferred_element_type=F32) * gate).astype(BF16)

    @pl.when(i < npad)
    def _():
        y_ref[...] = jnp.zeros_like(y_ref)


def _expert_ffn(xe, wr2, wg, wu, wd, layer, cap, tc):
    ne, rows, _ = xe.shape
    ntile = cap // tc
    npad = rows // tc - ntile
    wspec = lambda shape: pl.BlockSpec((None, None) + shape, lambda e, i: (layer, e, 0, 0))
    return pl.pallas_call(
        functools.partial(_expert_kernel, npad=npad),
        grid=(ne, rows // tc),
        in_specs=[pl.BlockSpec((None, tc, D_MODEL), lambda e, i: (e, jnp.maximum(i - npad, 0), 0)),
                  pl.BlockSpec(wr2.shape, lambda e, i: (0, 0, 0)),
                  wspec((D_MODEL, EXPERT_FF)), wspec((D_MODEL, EXPERT_FF)), wspec((EXPERT_FF, D_MODEL))],
        out_specs=pl.BlockSpec((None, tc, D_MODEL),
                               lambda e, i: (e, jnp.where(i < npad, ntile + i, i - npad), 0)),
        out_shape=jax.ShapeDtypeStruct((ne, rows, D_MODEL), BF16),
        compiler_params=_params(("parallel", "parallel")),
        name="expert_ffn",
    )(xe, wr2, wg, wu, wd)


COMBINE_SUB = 2


def _combine_kernel(start_ref, kmax_ref, h_ref, p_ref, val_ref, tgt_ref, eexp_ref, gain_ref, wg_ref, wp_ref,
                    ye_ref, out_ref, stage, sem, *, tile):
    j = pl.program_id(0)
    nstep = pl.num_programs(0)
    slot = lax.rem(j, 2)
    ne = N_EXPERTS
    subs = range(COMBINE_SUB)

    def window_copy(sl, sub, e, row0):
        buf = sl * COMBINE_SUB + sub
        return pltpu.make_async_copy(ye_ref.at[e, pl.ds(row0, MOE_WIN)],
                                     stage.at[buf, pl.ds(e * MOE_WIN, MOE_WIN)], sem.at[buf])

    def fetch(sl, sub, tile_idx, lo):
        for e in range(ne):
            base = pl.multiple_of((start_ref[tile_idx * ne + e] & (-MOE_ALIGN)) + lo, MOE_ALIGN)
            window_copy(sl, sub, e, base).start()

    def wait_windows(sl, sub):
        for e in range(ne):
            window_copy(sl, sub, e, 0).wait()

    @pl.when(j == 0)
    def _():
        for sub in subs:
            fetch(slot, sub, sub, 0)

    @pl.when(j + 1 < nstep)
    def _():
        for sub in subs:
            fetch(1 - slot, sub, (j + 1) * COMBINE_SUB + sub, 0)

    reps = [lax.dot_general(val_ref[:, sub * tile:(sub + 1) * tile].astype(BF16), eexp_ref[...],
                            (((0,), (0,)), ((), ())), preferred_element_type=F32) for sub in subs]
    pts = [(reps[sub] == tgt_ref[sub]).astype(BF16) for sub in subs]
    for sub in subs:
        wait_windows(slot, sub)
    accs = [h_ref[sub * tile:(sub + 1) * tile, :]
            + jnp.dot(pts[sub], stage[slot * COMBINE_SUB + sub], preferred_element_type=F32) for sub in subs]
    for sub in subs:
        t = j * COMBINE_SUB + sub

        def extra(k, acc, sub=sub, t=t):
            lo = k * MOE_WIN
            fetch(slot, sub, t, lo)
            wait_windows(slot, sub)
            pk = (reps[sub] == tgt_ref[sub] + lo.astype(F32)).astype(BF16)
            return acc + jnp.dot(pk, stage[slot * COMBINE_SUB + sub], preferred_element_type=F32)

        accs[sub] = lax.fori_loop(1, kmax_ref[t], extra, accs[sub])
    x = jnp.concatenate(accs, axis=0)
    ms = jnp.mean(x * x, axis=-1, keepdims=True)
    xn = (x * lax.rsqrt(ms + NORM_EPS) * gain_ref[...]).astype(BF16)
    gate = _sigmoid(jnp.dot(xn, wg_ref[...], preferred_element_type=F32))
    proj = jnp.dot(p_ref[...].astype(BF16), wp_ref[...], preferred_element_type=F32)
    out_ref[...] = x + gate * proj


def _combine_ple(h2, p3, layer, ye, val, starts, kmax, tgt, eexp, gain, wg, wp, tile):
    n = h2.shape[0]
    step = COMBINE_SUB * tile
    assert n % step == 0
    rows = N_EXPERTS * MOE_WIN
    fixed = lambda j, s, k: (0, 0)
    return pl.pallas_call(
        functools.partial(_combine_kernel, tile=tile),
        grid_spec=pltpu.PrefetchScalarGridSpec(
            num_scalar_prefetch=2, grid=(n // step,),
            in_specs=[pl.BlockSpec((step, D_MODEL), lambda j, s, k: (j, 0)),
                      pl.BlockSpec((None, step, PLE_DIM), lambda j, s, k: (layer, j, 0)),
                      pl.BlockSpec((N_EXPERTS, step), lambda j, s, k: (0, j)),
                      pl.BlockSpec((COMBINE_SUB, 1, rows), lambda j, s, k: (j, 0, 0)),
                      pl.BlockSpec((N_EXPERTS, rows), fixed),
                      pl.BlockSpec((1, D_MODEL), fixed), pl.BlockSpec(wg.shape, fixed),
                      pl.BlockSpec(wp.shape, fixed),
                      pl.BlockSpec(memory_space=pl.ANY)],
            out_specs=pl.BlockSpec((step, D_MODEL), lambda j, s, k: (j, 0)),
            scratch_shapes=[pltpu.VMEM((2 * COMBINE_SUB, rows, D_MODEL), BF16),
                            pltpu.SemaphoreType.DMA((2 * COMBINE_SUB,))]),
        out_shape=jax.ShapeDtypeStruct((n, D_MODEL), F32),
        compiler_params=_params(("arbitrary",)),
        name="moe_combine_ple",
    )(starts, kmax, h2, p3, val, tgt, eexp, gain, wg, wp, ye)


def _relayout_w_in(w_in):
    o_beta = ZA + ZB + DN_WIDTH
    o_alpha = o_beta + 2 * DN_HEADS
    o_glu = o_alpha + 2 * DN_HEADS
    pieces = [w_in[:, :o_beta], w_in[:, o_glu:o_glu + 2 * CONV_CH]]
    for d in range(2):
        pieces.append(w_in[:, o_beta + d * DN_HEADS:o_beta + (d + 1) * DN_HEADS])
        pieces.append(w_in[:, o_alpha + d * DN_HEADS:o_alpha + (d + 1) * DN_HEADS])
    pieces.append(jnp.zeros((w_in.shape[0], LANES - 4 * DN_HEADS), w_in.dtype))
    return jnp.concatenate(pieces, axis=1).astype(BF16)


def _prep_layer(lw):
    (norm_mix, w_in, q_gain, k_gain, sink, dn_conv, dn_a_log, dn_dt_bias, dn_out_gain,
     cv_dw, cv_dw_bias, cv_ln_gain, cv_ln_bias, w_out, norm_ffn, w_router, w_gate, w_up, w_down,
     norm_ple, w_ple_gate, w_ple_proj) = lw
    w_perm = _relayout_w_in(w_in)
    hgain = jnp.concatenate([jnp.tile(q_gain, ATT_HEADS) * (ATT_HEAD_DIM ** -0.5),
                             jnp.tile(k_gain, ATT_KV_HEADS)]).reshape(1, -1)
    zeros4 = jnp.zeros((DN_HEADS,), F32)
    aneg = -jnp.exp(dn_a_log.astype(F32))
    aneg_row = jnp.concatenate([zeros4, aneg[0], zeros4, aneg[1]])
    dtb_row = jnp.concatenate([zeros4, dn_dt_bias[0], zeros4, dn_dt_bias[1]])
    pad = lambda r: jnp.pad(r, (0, LANES - r.shape[0])).reshape(1, LANES)
    wr = jnp.pad(w_router.astype(F32), ((0, 0), (0, LANES - N_EXPERTS)))
    wr_hi = wr.astype(BF16)
    wr2 = jnp.stack([wr_hi, (wr - wr_hi.astype(F32)).astype(BF16)])
    return dict(
        w_router2=wr2,
        norm_mix=norm_mix.reshape(1, -1), w_in=w_perm, hgain=hgain, sink=sink.astype(F32),
        dn_conv=dn_conv, aneg=pad(aneg_row), dtb=pad(dtb_row),
        dn_out_gain=jnp.tile(dn_out_gain, DN_HEADS).reshape(1, -1),
        cv_dw=cv_dw, cv_dw_bias=cv_dw_bias.reshape(1, -1), cv_ln_gain=cv_ln_gain.reshape(1, -1),
        cv_ln_bias=cv_ln_bias.reshape(1, -1), w_out=w_out.astype(BF16),
        norm_ffn=norm_ffn.reshape(1, -1),
        norm_ple=norm_ple.reshape(1, -1), w_ple_gate=w_ple_gate.astype(BF16), w_ple_proj=w_ple_proj.astype(BF16))


def _tiles(bsz, seqlen):
    n = bsz * seqlen
    return dict(tm=min(1024, n), tl=min(512, seqlen), ch=min(256, seqlen), tcv=min(1024, seqlen))


def _moe_ple(h2, xn, aff_t, p3, layer, pw):
    n = h2.shape[0]
    cap = CAPACITY_FACTOR * n // N_EXPERTS
    tile = min(MOE_TILE, n)
    val, cnt = _select(aff_t, cap, tile)
    starts, kmax, tgt = _moe_plan(cnt, n // tile)
    eexp = _expand_matrix()
    xe = _dispatch(xn, val, starts, kmax, tgt, eexp, cap, tile)
    ye = _expert_ffn(xe, pw["w_router2"], pw["w_gate"], pw["w_up"], pw["w_down"], layer, cap,
                     min(MOE_FFN_TILE, cap))
    return _combine_ple(h2, p3, layer, ye, val, starts, kmax, tgt, eexp, pw["norm_ple"], pw["w_ple_gate"],
                        pw["w_ple_proj"], tile)


def _layer(h2, p3, layer, pw, bsz, seqlen):
    t = _tiles(bsz, seqlen)
    hm_att = _head_mean_matrix(ATT_Q + ATT_KV, ATT_HEAD_DIM)
    hs_dn = _head_sum_matrix(2 * DN_WIDTH, DN_HEAD_DIM)
    hm_dn = _head_mean_matrix(DN_WIDTH, DN_HEAD_DIM)
    za, zb, zg, glu_in, gates = _in_proj(h2, pw["norm_mix"], pw["w_in"], hm_att, pw["hgain"], t["tm"])
    o_a = _attention(za, pw["sink"], bsz, seqlen)
    y, gb = _dn_prep(zb, gates, pw["dn_conv"], hs_dn, pw["aneg"], pw["dtb"], bsz, seqlen, t["tl"])
    o_f, o_b = _dn_chunk(y, gb, bsz, seqlen, t["ch"])
    o_c = _conformer_conv(glu_in, pw["cv_dw"], pw["cv_dw_bias"], pw["cv_ln_gain"], pw["cv_ln_bias"],
                          bsz, seqlen, t["tcv"])
    h2, xn, aff_t = _out_proj_route(h2, o_a, o_f, o_b, zg, o_c, pw["dn_out_gain"], hm_dn, pw["w_out"],
                                    pw["norm_ffn"], pw["w_router2"], t["tm"])
    return _moe_ple(h2, xn, aff_t, p3, layer, pw)


def _trunk(x, p, layer_weights):
    bsz, seqlen, _ = x.shape
    h2 = x.reshape(bsz * seqlen, D_MODEL)
    p3 = p.reshape(p.shape[0], bsz * seqlen, PLE_DIM)
    for i, pw in enumerate(layer_weights):
        h2 = _layer(h2, p3, i, pw, bsz, seqlen)
    return h2.reshape(bsz, seqlen, D_MODEL)


def kernel(x_prompt, x_sample, p_prompt, p_sample, norm_mix, w_in, q_gain, k_gain, sink, dn_conv, dn_a_log,
           dn_dt_bias, dn_out_gain, cv_dw, cv_dw_bias, cv_ln_gain, cv_ln_bias, w_out, norm_ffn, w_router,
           w_gate, w_up, w_down, norm_ple, w_ple_gate, w_ple_proj):
    weights = (norm_mix, w_in, q_gain, k_gain, sink, dn_conv, dn_a_log, dn_dt_bias, dn_out_gain,
               cv_dw, cv_dw_bias, cv_ln_gain, cv_ln_bias, w_out, norm_ffn, w_router, w_gate, w_up, w_down,
               norm_ple, w_ple_gate, w_ple_proj)
    depth = w_in.shape[0]
    experts = dict(w_gate=w_gate.astype(BF16), w_up=w_up.astype(BF16), w_down=w_down.astype(BF16))
    layer_weights = [dict(_prep_layer([w[i] for w in weights]), **experts) for i in range(depth)]
    return (_trunk(x_prompt, p_prompt, layer_weights), _trunk(x_sample, p_sample, layer_weights))
```

```python
import functools
import math

import numpy as np
import jax
import jax.numpy as jnp
from jax import lax
from jax.experimental import pallas as pl
from jax.experimental.pallas import tpu as pltpu

F32 = jnp.float32
BF16 = jnp.bfloat16

D_MODEL = 1024
ATT_HEADS = 8
ATT_KV_HEADS = 2
ATT_HEAD_DIM = 64
ATT_GROUP = ATT_HEADS // ATT_KV_HEADS
WINDOW = 128
ATT_BLOCK = 128
DN_HEADS = 4
DN_HEAD_DIM = 64
DN_WIDTH = DN_HEADS * DN_HEAD_DIM
DN_CHUNK = 64
CONV_CH = 256
CONV_WIDTH = 31
ATT_Q = ATT_HEADS * ATT_HEAD_DIM
ATT_KV = ATT_KV_HEADS * ATT_HEAD_DIM
N_EXPERTS = 16
CAPACITY_FACTOR = 2
EXPERT_FF = 1024
PLE_DIM = 256
NORM_EPS = 1e-6

LANES = 128
SUBLANES = 8
VMEM_LIMIT = 48 * 1024 * 1024
FFN_VMEM_LIMIT = 56 * 1024 * 1024

ZA = ATT_Q + 2 * ATT_KV
ZB = 3 * DN_WIDTH
ZW = ZA + ZB + DN_WIDTH + 2 * CONV_CH + LANES


def _params(sem):
    return pltpu.CompilerParams(dimension_semantics=sem, vmem_limit_bytes=VMEM_LIMIT)


def _head_mean_matrix(width, head):
    idx = np.arange(width) // head
    return jnp.asarray((idx[:, None] == idx[None, :]).astype(np.float32) / head, dtype=BF16)


def _head_sum_matrix(width, head):
    idx = np.arange(width) // head
    return jnp.asarray((idx[:, None] == idx[None, :]).astype(np.float32), dtype=BF16)


def _sigmoid(x):
    return 1.0 / (1.0 + jnp.exp(-x))


def _silu(x):
    return x * _sigmoid(x)


def _in_proj_kernel(x_ref, gain_ref, w_ref, hm_ref, hgain_ref, za_ref, zb_ref, zg_ref, glu_ref, gates_ref):
    x = x_ref[...]
    ms = jnp.mean(x * x, axis=-1, keepdims=True)
    a = (x * lax.rsqrt(ms + NORM_EPS) * gain_ref[...]).astype(BF16)
    z = jnp.dot(a, w_ref[...], preferred_element_type=F32)
    nqk = ATT_Q + ATT_KV
    qk = z[:, :nqk]
    hms = jnp.dot((qk * qk).astype(BF16), hm_ref[...], preferred_element_type=F32)
    za_ref[:, :nqk] = (qk * lax.rsqrt(hms + NORM_EPS) * hgain_ref[...]).astype(BF16)
    za_ref[:, nqk:] = z[:, nqk:ZA].astype(BF16)
    zb_ref[...] = z[:, ZA:ZA + ZB]
    zg_ref[...] = z[:, ZA + ZB:ZA + ZB + DN_WIDTH]
    glu_ref[...] = z[:, ZA + ZB + DN_WIDTH:ZA + ZB + DN_WIDTH + 2 * CONV_CH]
    gates_ref[...] = z[:, ZW - LANES:]


def _in_proj(h2, gain, w_perm, hm, hgain, tm):
    n = h2.shape[0]
    row = lambda i: (i, 0)
    fixed = lambda i: (0, 0)
    return pl.pallas_call(
        _in_proj_kernel,
        grid=(n // tm,),
        in_specs=[pl.BlockSpec((tm, D_MODEL), row), pl.BlockSpec((1, D_MODEL), fixed),
                  pl.BlockSpec((D_MODEL, ZW), fixed), pl.BlockSpec(hm.shape, fixed),
                  pl.BlockSpec(hgain.shape, fixed)],
        out_specs=[pl.BlockSpec((tm, ZA), row), pl.BlockSpec((tm, ZB), row), pl.BlockSpec((tm, DN_WIDTH), row),
                   pl.BlockSpec((tm, 2 * CONV_CH), row), pl.BlockSpec((tm, LANES), row)],
        out_shape=[jax.ShapeDtypeStruct((n, ZA), BF16), jax.ShapeDtypeStruct((n, ZB), F32),
                   jax.ShapeDtypeStruct((n, DN_WIDTH), F32), jax.ShapeDtypeStruct((n, 2 * CONV_CH), F32),
                   jax.ShapeDtypeStruct((n, LANES), F32)],
        compiler_params=_params(("parallel",)),
        name="in_proj",
    )(h2, gain, w_perm, hm, hgain)


ATT_MASKED = -1e30


def _attn_bias_table():
    i = np.arange(ATT_BLOCK)[:, None]
    c = np.arange(3 * ATT_BLOCK)[None, :]
    rel = c - ATT_BLOCK - i
    slopes = 2.0 ** (-8.0 * np.arange(1, ATT_HEADS + 1) / ATT_HEADS)
    table = np.empty((3, ATT_KV_HEADS, ATT_GROUP * ATT_BLOCK, 3 * ATT_BLOCK), np.float32)
    for variant in range(3):
        ok = np.abs(rel) <= WINDOW
        if variant == 0:
            ok = ok & (c >= ATT_BLOCK)
        if variant == 2:
            ok = ok & (c < 2 * ATT_BLOCK)
        for hd in range(ATT_HEADS):
            g, j = divmod(hd, ATT_GROUP)
            table[variant, g, j * ATT_BLOCK:(j + 1) * ATT_BLOCK] = np.where(ok, -slopes[hd] * np.abs(rel), ATT_MASKED)
    return jnp.asarray(table)


def _attn_kernel(sink_ref, q_ref, kvp_ref, kvo_ref, kvn_ref, bias_a_ref, bias_b_ref, o_ref):
    kv = jnp.concatenate([kvp_ref[...], kvo_ref[...], kvn_ref[...]], axis=0)
    hd_ = ATT_HEAD_DIM
    groups = range(ATT_KV_HEADS)
    heads = range(ATT_HEADS)
    rows = lambda t, hd: t[(hd % ATT_GROUP) * ATT_BLOCK:(hd % ATT_GROUP + 1) * ATT_BLOCK]
    work = []
    for blk, bias_ref in enumerate((bias_a_ref, bias_b_ref)):
        keys = kv[blk * ATT_BLOCK:(blk + 3) * ATT_BLOCK]
        ks = [keys[:, g * hd_:(g + 1) * hd_] for g in groups]
        vs = [keys[:, ATT_KV + g * hd_:ATT_KV + (g + 1) * hd_] for g in groups]
        q = q_ref[blk * ATT_BLOCK:(blk + 1) * ATT_BLOCK, :]
        qs = [jnp.concatenate([q[:, (g * ATT_GROUP + j) * hd_:(g * ATT_GROUP + j + 1) * hd_]
                               for j in range(ATT_GROUP)], axis=0) for g in groups]
        sg = [lax.dot_general(qs[g], ks[g], (((1,), (1,)), ((), ())), preferred_element_type=F32) + bias_ref[g]
              for g in groups]
        work.append((vs, [rows(sg[hd // ATT_GROUP], hd) for hd in heads]))
    m = [[jnp.maximum(jnp.max(s[hd], axis=-1, keepdims=True), sink_ref[hd]) for hd in heads] for _, s in work]
    e = [[jnp.exp(s[hd] - mb[hd]) for hd in heads] for (_, s), mb in zip(work, m)]
    den = [[jnp.sum(eb[hd], axis=-1, keepdims=True) + jnp.exp(sink_ref[hd] - mb[hd]) for hd in heads]
           for eb, mb in zip(e, m)]
    for blk, ((vs, _), eb, db) in enumerate(zip(work, e, den)):
        eg = [jnp.concatenate([eb[g * ATT_GROUP + j].astype(BF16) for j in range(ATT_GROUP)], axis=0)
              for g in groups]
        og = [jnp.dot(eg[g], vs[g], preferred_element_type=F32) for g in groups]
        for hd in heads:
            o_ref[blk * ATT_BLOCK:(blk + 1) * ATT_BLOCK, hd * hd_:(hd + 1) * hd_] = (
                rows(og[hd // ATT_GROUP], hd) / db[hd]).astype(BF16)


def _attention(za, sink, bsz, seqlen):
    nb = seqlen // ATT_BLOCK
    assert nb >= 2 and nb % 2 == 0
    npair = nb // 2
    za3 = za.reshape(bsz, seqlen, ZA)
    kvw = 2 * ATT_KV
    kvc = ATT_Q // kvw
    bias = _attn_bias_table()
    bias_spec = lambda pick: pl.BlockSpec((None,) + bias.shape[1:], lambda b, n: (pick(n), 0, 0, 0))
    return pl.pallas_call(
        _attn_kernel,
        grid=(bsz, npair),
        in_specs=[pl.BlockSpec(memory_space=pltpu.SMEM),
                  pl.BlockSpec((None, 2 * ATT_BLOCK, ATT_Q), lambda b, n: (b, n, 0)),
                  pl.BlockSpec((None, ATT_BLOCK, kvw), lambda b, n: (b, jnp.maximum(2 * n - 1, 0), kvc)),
                  pl.BlockSpec((None, 2 * ATT_BLOCK, kvw), lambda b, n: (b, n, kvc)),
                  pl.BlockSpec((None, ATT_BLOCK, kvw), lambda b, n: (b, jnp.minimum(2 * n + 2, nb - 1), kvc)),
                  bias_spec(lambda n: jnp.where(n == 0, 0, 1)),
                  bias_spec(lambda n: jnp.where(n == npair - 1, 2, 1))],
        out_specs=pl.BlockSpec((None, 2 * ATT_BLOCK, ATT_Q), lambda b, n: (b, n, 0)),
        out_shape=jax.ShapeDtypeStruct((bsz, seqlen, ATT_Q), BF16),
        compiler_params=_params(("parallel", "parallel")),
        name="window_attention",
    )(sink, za3, za3, za3, za3, bias, bias).reshape(bsz * seqlen, ATT_Q)


DN_HALO = SUBLANES


def _dn_prep_kernel(x_ref, xp_ref, xn_ref, cw_ref, hs_ref, g_ref, aneg_ref, dtb_ref, mf_ref, mb_ref,
                    y_ref, gb_ref, buf_ref, *, tl):
    i = pl.program_id(1)
    nt = pl.num_programs(1)
    buf_ref[0:DN_HALO, :] = jnp.where(i > 0, xp_ref[...], 0.0)
    buf_ref[DN_HALO:DN_HALO + tl, :] = x_ref[...]
    buf_ref[DN_HALO + tl:, :] = jnp.where(i < nt - 1, xn_ref[...], 0.0)
    y = (cw_ref[0:1, :] * buf_ref[DN_HALO - 1:DN_HALO - 1 + tl, :]
         + cw_ref[1:2, :] * buf_ref[DN_HALO:DN_HALO + tl, :]
         + cw_ref[2:3, :] * buf_ref[DN_HALO + 1:DN_HALO + 1 + tl, :])
    y = _silu(y)
    qk = y[:, :2 * DN_WIDTH]
    ss = jnp.dot((qk * qk).astype(BF16), hs_ref[...], preferred_element_type=F32)
    lane = lax.broadcasted_iota(jnp.int32, (tl, 2 * DN_WIDTH), 1)
    scale = jnp.where(lane < DN_WIDTH, DN_HEAD_DIM ** -0.5, 1.0)
    y_ref[:, :2 * DN_WIDTH] = qk * lax.rsqrt(ss + NORM_EPS) * scale
    y_ref[:, 2 * DN_WIDTH:] = y[:, 2 * DN_WIDTH:]
    raw = g_ref[...]
    col = lax.broadcasted_iota(jnp.int32, (tl, LANES), 1)
    is_beta = (col & DN_HEADS) == 0
    t = raw + dtb_ref[...]
    softplus = jnp.maximum(t, 0.0) + jnp.log(1.0 + jnp.exp(-jnp.abs(t)))
    vals = jnp.where(is_beta, _sigmoid(raw), aneg_ref[...] * softplus)
    v_hi = vals.astype(BF16)
    r1 = vals - v_hi.astype(F32)
    v_mid = r1.astype(BF16)
    v_lo = (r1 - v_mid.astype(F32)).astype(BF16)
    terms = jnp.concatenate([v_hi, v_mid, v_lo], axis=1)
    cf3 = jnp.dot(mf_ref[...], terms, preferred_element_type=F32)
    cb3 = jnp.dot(mb_ref[...], terms, preferred_element_type=F32)
    cf = cf3[:, :LANES] + (cf3[:, LANES:2 * LANES] + cf3[:, 2 * LANES:])
    cb = cb3[:, :LANES] + (cb3[:, LANES:2 * LANES] + cb3[:, 2 * LANES:])
    gb_ref[0] = jnp.where(is_beta, vals, cf)
    gb_ref[1] = pltpu.roll(jnp.where(is_beta, vals, cb), LANES - 2 * DN_HEADS, axis=1)


def _dn_prep(zb, gates, conv_w, hs, aneg, dtb, bsz, seqlen, tl):
    zb3 = zb.reshape(bsz, seqlen, ZB)
    g3 = gates.reshape(bsz, seqlen, LANES)
    nt = seqlen // tl
    hb = tl // DN_HALO
    ch = np.arange(tl) // DN_CHUNK
    same = ch[:, None] == ch[None, :]
    pos = np.arange(tl)
    mf = jnp.asarray((same & (pos[None, :] <= pos[:, None])).astype(np.float32), dtype=BF16)
    mb = jnp.asarray((same & (pos[None, :] >= pos[:, None])).astype(np.float32), dtype=BF16)
    fixed = lambda b, i: (0, 0)
    y, gb = pl.pallas_call(
        functools.partial(_dn_prep_kernel, tl=tl),
        grid=(bsz, nt),
        in_specs=[pl.BlockSpec((None, tl, ZB), lambda b, i: (b, i, 0)),
                  pl.BlockSpec((None, DN_HALO, ZB), lambda b, i: (b, jnp.maximum(i * hb - 1, 0), 0)),
                  pl.BlockSpec((None, DN_HALO, ZB), lambda b, i: (b, jnp.minimum((i + 1) * hb, nt * hb - 1), 0)),
                  pl.BlockSpec(conv_w.shape, fixed), pl.BlockSpec(hs.shape, fixed),
                  pl.BlockSpec((None, tl, LANES), lambda b, i: (b, i, 0)),
                  pl.BlockSpec((1, LANES), fixed), pl.BlockSpec((1, LANES), fixed),
                  pl.BlockSpec((tl, tl), fixed), pl.BlockSpec((tl, tl), fixed)],
        out_specs=[pl.BlockSpec((None, tl, ZB), lambda b, i: (b, i, 0)),
                   pl.BlockSpec((2, None, tl, LANES), lambda b, i: (0, b, i, 0))],
        out_shape=[jax.ShapeDtypeStruct((bsz, seqlen, ZB), F32),
                   jax.ShapeDtypeStruct((2, bsz, seqlen, LANES), F32)],
        scratch_shapes=[pltpu.VMEM((tl + 2 * DN_HALO, ZB), F32)],
        compiler_params=_params(("parallel", "parallel")),
        name="deltanet_prep",
    )(zb3, zb3, zb3, conv_w, hs, g3, aneg, dtb, mf, mb)
    return y, gb


def _lane_expand(cols, first):
    c = cols.shape[0]
    lane = lax.broadcasted_iota(jnp.int32, (c, LANES), 1)
    halves = []
    for h in range(0, DN_HEADS, 2):
        a = jnp.broadcast_to(cols[:, first + h:first + h + 1], (c, LANES))
        b = jnp.broadcast_to(cols[:, first + h + 1:first + h + 2], (c, LANES))
        halves.append(jnp.where(lane < DN_HEAD_DIM, a, b))
    return jnp.concatenate(halves, axis=1)


def _dn_pair_kernel(xf_ref, xb_ref, gf_ref, gb_ref, of_ref, ob_ref, sf_ref, sb_ref, *, nsub):
    c = DN_CHUNK
    w = DN_WIDTH

    @pl.when(pl.program_id(1) == 0)
    def _():
        sf_ref[...] = jnp.zeros_like(sf_ref)
        sb_ref[...] = jnp.zeros_like(sb_ref)

    r_cat = lax.broadcasted_iota(jnp.int32, (c, w), 0)
    s_cat = lax.broadcasted_iota(jnp.int32, (c, w), 1) & (DN_HEAD_DIM - 1)
    eye_cat = s_cat == r_cat
    rr = lax.broadcasted_iota(jnp.int32, (w, w), 0)
    cc = lax.broadcasted_iota(jnp.int32, (w, w), 1)
    head = (rr >> 6) == (cc >> 6)
    head_b = head.astype(BF16)
    m16 = (s_cat >> 4) == (r_cat >> 4)
    m32 = (s_cat >> 5) == (r_cat >> 5)
    off16 = m32 & jnp.logical_not(m16)
    off32 = jnp.logical_not(m32)
    eye_f = eye_cat.astype(F32)

    def bd(t):
        return jnp.concatenate([t] * DN_HEADS, axis=0) * head_b

    def mm(a, b):
        return jnp.dot(a, b, preferred_element_type=F32)

    chunks = [(0, i * c) for i in range(nsub)] + [(1, (nsub - 1 - i) * c) for i in range(nsub)]
    xrefs = (xf_ref, xb_ref)
    grefs = (gf_ref, gb_ref)
    orefs = (of_ref, ob_ref)
    srefs = (sf_ref, sb_ref)
    incl = (s_cat <= r_cat, s_cat >= r_cat)
    strict = (s_cat < r_cat, s_cat > r_cat)
    last_row = (c - 1, 0)

    pre = []
    for d, st in chunks:
        x = xrefs[d][st:st + c, :]
        q, k, v = x[:, :w], x[:, w:2 * w], x[:, 2 * w:]
        gbt = grefs[d][st:st + c, :]
        beta = _lane_expand(gbt, 0)
        gc = _lane_expand(gbt, DN_HEADS)
        grow = jnp.sum(jnp.where(eye_cat, gc, 0.0), axis=0, keepdims=True)
        decay = jnp.exp(jnp.where(incl[d], gc - grow, -jnp.inf))
        glast = gc[last_row[d]:last_row[d] + 1, :]
        egc = jnp.exp(gc)
        kb = k * beta
        pre.append(dict(d=d, st=st, q=q, k=k, kb=kb, vb=v * beta, decay=decay, glast=glast, egc=egc,
                        kdec=(k * jnp.exp(glast - gc)).astype(BF16)))

    kks = [lax.dot_general(jnp.concatenate([p["kb"], p["q"]], axis=0).astype(BF16), bd(p["k"].astype(BF16)),
                           (((1,), (1,)), ((), ())), preferred_element_type=F32) for p in pre]
    a = [jnp.where(strict[p["d"]], kk[:c] * p["decay"], 0.0) for p, kk in zip(pre, kks)]
    intra = [jnp.where(incl[p["d"]], kk[c:] * p["decay"], 0.0).astype(BF16) for p, kk in zip(pre, kks)]
    xm = [jnp.where(m16, -t, 0.0) for t in a]
    xm_b = [t.astype(BF16) for t in xm]
    x2_b = [mm(t, bd(t)).astype(BF16) for t in xm_b]
    x2_d = [bd(t) for t in x2_b]
    dinv = [eye_f + t for t in xm]
    r2 = [mm(jnp.concatenate([t.astype(BF16), p2], axis=0), s2) for t, p2, s2 in zip(dinv, x2_b, x2_d)]
    dinv = [t + r[:c] for t, r in zip(dinv, r2)]
    x4_b = [r[c:].astype(BF16) for r in r2]
    x4_d = [bd(t) for t in x4_b]
    r4 = [mm(jnp.concatenate([t.astype(BF16), p4], axis=0), s4) for t, p4, s4 in zip(dinv, x4_b, x4_d)]
    dinv = [t + r[:c] for t, r in zip(dinv, r4)]
    x8_d = [bd(r[c:].astype(BF16)) for r in r4]
    dinv = [t + mm(t.astype(BF16), s8) for t, s8 in zip(dinv, x8_d)]
    dinv_b = [t.astype(BF16) for t in dinv]
    n32 = [bd(mm(jnp.where(off16, t, 0.0).astype(BF16), bd(db)).astype(BF16)) for t, db in zip(a, dinv_b)]
    t32 = [t - mm(db, n) for t, db, n in zip(dinv, dinv_b, n32)]
    t32_b = [t.astype(BF16) for t in t32]
    n64 = [bd(mm(jnp.where(off32, t, 0.0).astype(BF16), bd(tb)).astype(BF16)) for t, tb in zip(a, t32_b)]
    t_cat = [(t - mm(tb, n)).astype(BF16) for t, tb, n in zip(t32, t32_b, n64)]
    uw = [mm(tc, jnp.concatenate([bd(p["vb"].astype(BF16)), bd((p["kb"] * p["egc"]).astype(BF16))], axis=1))
          for tc, p in zip(t_cat, pre)]
    uw_b = [t.astype(BF16) for t in uw]
    pn = [lax.dot_general(p["kdec"], t, (((0,), (0,)), ((), ())), preferred_element_type=F32)
          for p, t in zip(pre, uw_b)]
    qo = [mm(it, jnp.concatenate([bd(t[:, :w]), bd(t[:, w:])], axis=1)) for it, t in zip(intra, uw_b)]
    lhs = [jnp.concatenate([n[:, w:].astype(BF16) * head_b, (p["q"] * p["egc"] - o[:, w:]).astype(BF16)], axis=0)
           for n, o, p in zip(pn, qo, pre)]
    for step in range(nsub):
        for d in range(2):
            i = d * nsub + step
            p = pre[i]
            state = srefs[d][...]
            r = jnp.dot(lhs[i], state.astype(BF16), preferred_element_type=F32)
            orefs[d][p["st"]:p["st"] + c, :] = r[w:] + qo[i][:, :w]
            srefs[d][...] = state * jnp.exp(p["glast"]) - r[:w] + jnp.where(head, pn[i][:, :w], 0.0)


def _dn_chunk(y, gb, bsz, seqlen, ch):
    nsub = ch // DN_CHUNK
    nblk = seqlen // ch
    fwd = lambda b, j: (b, j, 0)
    bwd = lambda b, j: (b, nblk - 1 - j, 0)
    o_f, o_b = pl.pallas_call(
        functools.partial(_dn_pair_kernel, nsub=nsub),
        grid=(bsz, nblk),
        in_specs=[pl.BlockSpec((None, ch, ZB), fwd), pl.BlockSpec((None, ch, ZB), bwd),
                  pl.BlockSpec((None, None, ch, LANES), lambda b, j: (0, b, j, 0)),
                  pl.BlockSpec((None, None, ch, LANES), lambda b, j: (1, b, nblk - 1 - j, 0))],
        out_specs=[pl.BlockSpec((None, ch, DN_WIDTH), fwd), pl.BlockSpec((None, ch, DN_WIDTH), bwd)],
        out_shape=[jax.ShapeDtypeStruct((bsz, seqlen, DN_WIDTH), F32)] * 2,
        scratch_shapes=[pltpu.VMEM((DN_WIDTH, DN_WIDTH), F32)] * 2,
        compiler_params=_params(("parallel", "arbitrary")),
        name="deltanet_chunks",
    )(y, y, gb, gb)
    return o_f.reshape(bsz * seqlen, DN_WIDTH), o_b.reshape(bsz * seqlen, DN_WIDTH)


CV_HALO = 2 * SUBLANES
CV_PAD = (CONV_WIDTH - 1) // 2


def _conv_kernel(x_ref, xp_ref, xn_ref, dw_ref, bias_ref, lng_ref, lnb_ref, o_ref, buf_ref, shift_ref, *, tl):
    i = pl.program_id(1)
    nt = pl.num_programs(1)

    def glu(t):
        return t[:, :CONV_CH] * _sigmoid(t[:, CONV_CH:])

    buf_ref[0:CV_HALO, :] = jnp.where(i > 0, glu(xp_ref[...]), 0.0)
    buf_ref[CV_HALO:CV_HALO + tl, :] = glu(x_ref[...])
    buf_ref[CV_HALO + tl:, :] = jnp.where(i < nt - 1, glu(xn_ref[...]), 0.0)
    acc = jnp.zeros((tl, CONV_CH), F32) + bias_ref[...]
    first = CV_HALO - CV_PAD
    span = -(-(first + CONV_WIDTH) // SUBLANES) * SUBLANES - SUBLANES
    for sub in range(SUBLANES):
        shift_ref[...] = buf_ref[sub:sub + tl + span, :]
        for base in range(0, span + 1, SUBLANES):
            j = base + sub - first
            if 0 <= j < CONV_WIDTH:
                acc = acc + dw_ref[j:j + 1, :] * shift_ref[base:base + tl, :]
    mu = jnp.mean(acc, axis=-1, keepdims=True)
    cen = acc - mu
    var = jnp.mean(cen * cen, axis=-1, keepdims=True)
    o_ref[...] = _silu(cen * lax.rsqrt(var + NORM_EPS) * lng_ref[...] + lnb_ref[...]).astype(BF16)


def _conformer_conv(glu_in, dw, bias, lng, lnb, bsz, seqlen, tl):
    x3 = glu_in.reshape(bsz, seqlen, 2 * CONV_CH)
    nt = seqlen // tl
    hb = tl // CV_HALO
    fixed = lambda b, i: (0, 0)
    return pl.pallas_call(
        functools.partial(_conv_kernel, tl=tl),
        grid=(bsz, nt),
        in_specs=[pl.BlockSpec((None, tl, 2 * CONV_CH), lambda b, i: (b, i, 0)),
                  pl.BlockSpec((None, CV_HALO, 2 * CONV_CH), lambda b, i: (b, jnp.maximum(i * hb - 1, 0), 0)),
                  pl.BlockSpec((None, CV_HALO, 2 * CONV_CH),
                               lambda b, i: (b, jnp.minimum((i + 1) * hb, nt * hb - 1), 0)),
                  pl.BlockSpec(dw.shape, fixed), pl.BlockSpec((1, CONV_CH), fixed),
                  pl.BlockSpec((1, CONV_CH), fixed), pl.BlockSpec((1, CONV_CH), fixed)],
        out_specs=pl.BlockSpec((None, tl, CONV_CH), lambda b, i: (b, i, 0)),
        out_shape=jax.ShapeDtypeStruct((bsz, seqlen, CONV_CH), BF16),
        scratch_shapes=[pltpu.VMEM((tl + 2 * CV_HALO, CONV_CH), F32),
                        pltpu.VMEM((tl + 2 * CV_HALO - SUBLANES, CONV_CH), F32)],
        compiler_params=_params(("parallel", "parallel")),
        name="conformer_conv",
    )(x3, x3, x3, dw, bias, lng, lnb).reshape(bsz * seqlen, CONV_CH)


def _out_proj_kernel(h_ref, oa_ref, of_ref, ob_ref, zg_ref, oc_ref, og_ref, hm_ref, w_ref, gain_ref, wr_ref,
                     out_ref, xn_ref, aff_ref):
    half = h_ref.shape[0] // 2
    for r0 in (0, half):
        rs = slice(r0, r0 + half)
        ob = of_ref[rs, :] + ob_ref[rs, :]
        ms = jnp.dot((ob * ob).astype(BF16), hm_ref[...], preferred_element_type=F32)
        obn = ob * lax.rsqrt(ms + NORM_EPS) * og_ref[...]
        ob2 = obn * _silu(zg_ref[rs, :])
        mix = jnp.concatenate([oa_ref[rs, :], ob2.astype(BF16), oc_ref[rs, :]], axis=1)
        x = h_ref[rs, :] + jnp.dot(mix, w_ref[...], preferred_element_type=F32)
        out_ref[rs, :] = x
        ms = jnp.mean(x * x, axis=-1, keepdims=True)
        xn = x * lax.rsqrt(ms + NORM_EPS) * gain_ref[...]
        xn_hi = xn.astype(BF16)
        xn_ref[rs, :] = xn_hi
        xn_lo = (xn - xn_hi.astype(F32)).astype(BF16)
        logits = (jnp.dot(xn_hi, wr_ref[0], preferred_element_type=F32)
                  + (jnp.dot(xn_lo, wr_ref[0], preferred_element_type=F32)
                     + jnp.dot(xn_hi, wr_ref[1], preferred_element_type=F32)))
        lane = lax.broadcasted_iota(jnp.int32, logits.shape, 1)
        logits = jnp.where(lane < N_EXPERTS, logits, -jnp.inf)
        m = jnp.max(logits, axis=-1, keepdims=True)
        e = jnp.exp(logits - m)
        aff = e / jnp.sum(e, axis=-1, keepdims=True)
        aff_ref[:, rs] = jnp.transpose(aff)[:N_EXPERTS, :]


def _out_proj_route(h2, oa, o_f, o_b, zg, oc, og, hm, w, gain, wr2, tm):
    n = h2.shape[0]
    row = lambda i: (i, 0)
    fixed = lambda i: (0, 0)
    return pl.pallas_call(
        _out_proj_kernel,
        grid=(n // tm,),
        in_specs=[pl.BlockSpec((tm, D_MODEL), row), pl.BlockSpec((tm, ATT_Q), row),
                  pl.BlockSpec((tm, DN_WIDTH), row), pl.BlockSpec((tm, DN_WIDTH), row),
                  pl.BlockSpec((tm, DN_WIDTH), row),
                  pl.BlockSpec((tm, CONV_CH), row), pl.BlockSpec((1, DN_WIDTH), fixed),
                  pl.BlockSpec(hm.shape, fixed), pl.BlockSpec(w.shape, fixed),
                  pl.BlockSpec((1, D_MODEL), fixed), pl.BlockSpec((2, D_MODEL, LANES), lambda i: (0, 0, 0))],
        out_specs=[pl.BlockSpec((tm, D_MODEL), row), pl.BlockSpec((tm, D_MODEL), row),
                   pl.BlockSpec((N_EXPERTS, tm), lambda i: (0, i))],
        out_shape=[jax.ShapeDtypeStruct((n, D_MODEL), F32), jax.ShapeDtypeStruct((n, D_MODEL), BF16),
                   jax.ShapeDtypeStruct((N_EXPERTS, n), F32)],
        compiler_params=_params(("parallel",)),
        name="out_proj_route",
    )(h2, oa, o_f, o_b, zg, oc, og, hm, w, gain, wr2)


MOE_TILE = 256
MOE_ALIGN = 2 * SUBLANES
MOE_WIN = 64
MOE_PAD = 1024
MOE_FFN_TILE = 1024
FF_CHUNK = 256
MOE_UNSELECTED = -64.0


def _select_kernel(aff_ref, tri_ref, val_ref, cnt_ref, *, cap, tile):
    ne, n = aff_ref.shape
    nt = n // tile
    capf = float(cap)

    def bits_of(x):
        return lax.bitcast_convert_type(x, jnp.int32)

    def search(i, thr):
        cand = thr | jnp.left_shift(jnp.int32(1), 30 - i)
        cnt = jnp.sum((bits_of(aff_ref[...]) >= cand).astype(F32), axis=1, keepdims=True)
        return jnp.where(cnt >= capf, cand, thr)

    thr = lax.fori_loop(0, 31, search, jnp.zeros((ne, 1), jnp.int32))
    n_gt = jnp.sum((bits_of(aff_ref[...]) > thr).astype(F32), axis=1, keepdims=True)
    need = capf - n_gt
    lane = lax.broadcasted_iota(jnp.int32, (ne, LANES), 1)

    def tile_body(j, carry):
        eq_before, cnt_acc = carry
        off = pl.multiple_of(j * tile, tile)
        b = bits_of(aff_ref[:, pl.ds(off, tile)])
        gt = b > thr
        eqf = (b == thr).astype(F32)
        eq_rank = eq_before + jnp.dot(eqf.astype(BF16), tri_ref[...], preferred_element_type=F32)
        self_ = jnp.where(gt, 1.0, jnp.where(eq_rank <= need, eqf, 0.0))
        rank = jnp.dot(self_.astype(BF16), tri_ref[...], preferred_element_type=F32)
        val_ref[:, pl.ds(off, tile)] = jnp.where(self_ > 0.0, rank, MOE_UNSELECTED)
        cnt = jnp.sum(self_, axis=1, keepdims=True)
        return (eq_before + jnp.sum(eqf, axis=1, keepdims=True), cnt_acc + jnp.where(lane == j, cnt, 0.0))

    init = (jnp.zeros((ne, 1), F32), jnp.zeros((ne, LANES), F32))
    _, cnt_acc = lax.fori_loop(0, nt, tile_body, init, unroll=math.gcd(nt, 4))
    cnt_ref[...] = cnt_acc


def _select(aff_t, cap, tile):
    ne, n = aff_t.shape
    assert n // tile <= LANES
    tri = jnp.asarray(np.triu(np.ones((tile, tile), np.float32)), dtype=BF16)
    return pl.pallas_call(
        functools.partial(_select_kernel, cap=cap, tile=tile),
        out_shape=[jax.ShapeDtypeStruct((ne, n), F32), jax.ShapeDtypeStruct((ne, LANES), F32)],
        compiler_params=pltpu.CompilerParams(vmem_limit_bytes=VMEM_LIMIT),
        name="moe_select",
    )(aff_t, tri)


def _moe_plan(cnt, nt):
    c = cnt[:, :nt].astype(jnp.int32).T
    starts = jnp.concatenate([jnp.zeros((1, N_EXPERTS), jnp.int32), jnp.cumsum(c, axis=0)], axis=0)
    head = starts[:-1] & (MOE_ALIGN - 1)
    kmax = jnp.maximum(jnp.max((head + c + MOE_WIN - 1) // MOE_WIN, axis=1), 1).astype(jnp.int32)
    w = jnp.arange(MOE_WIN, dtype=jnp.int32)
    tgt = (w[None, None, :] + 1 - head[:, :, None]).astype(F32).reshape(nt, 1, N_EXPERTS * MOE_WIN)
    return starts.reshape(-1), kmax, tgt


def _expand_matrix():
    e = np.arange(N_EXPERTS * MOE_WIN) // MOE_WIN
    return jnp.asarray((np.arange(N_EXPERTS)[:, None] == e[None, :]).astype(np.float32), dtype=BF16)


def _slot_onehot(val_ref, eexp_ref):
    return lax.dot_general(val_ref[...].astype(BF16), eexp_ref[...], (((0,), (0,)), ((), ())),
                           preferred_element_type=F32)


def _dispatch_kernel(start_ref, kmax_ref, xn_ref, val_ref, tgt_ref, eexp_ref, xe_ref, stage, carry, sem):
    j = pl.program_id(0)
    nt = pl.num_programs(0)
    slot = lax.rem(j, 2)
    ne = N_EXPERTS

    cap = xe_ref.shape[1] - MOE_PAD

    @pl.when(j == 0)
    def _():
        carry[...] = jnp.zeros_like(carry)
        stage[0, 0:MOE_PAD, :] = jnp.zeros((MOE_PAD, D_MODEL), BF16)
        fills = [pltpu.make_async_copy(stage.at[0, pl.ds(0, MOE_PAD)], xe_ref.at[e, pl.ds(cap, MOE_PAD)], sem.at[0])
                 for e in range(ne)]
        for f in fills:
            f.start()
        for f in fills:
            f.wait()

    def window_copy(sl, e, row0):
        return pltpu.make_async_copy(stage.at[sl, pl.ds(e * MOE_WIN, MOE_WIN)],
                                     xe_ref.at[e, pl.ds(row0, MOE_WIN)], sem.at[sl])

    def wait_windows(sl):
        for e in range(ne):
            window_copy(sl, e, 0).wait()

    rep = _slot_onehot(val_ref, eexp_ref)
    xn = xn_ref[...]
    row = lax.broadcasted_iota(jnp.int32, (MOE_ALIGN, D_MODEL), 0)

    def block(k, _):
        @pl.when(k > 0)
        def _():
            wait_windows(slot)

        lo = k * MOE_WIN
        pt = (rep == tgt_ref[...] + lo.astype(F32)).astype(BF16)
        comp = lax.dot_general(pt, xn, (((0,), (0,)), ((), ())), preferred_element_type=F32)
        stage[slot] = comp.astype(BF16)
        for e in range(ne):
            s = start_ref[j * ne + e]
            head = s & (MOE_ALIGN - 1)
            r0 = e * MOE_WIN
            kept = carry[e * MOE_ALIGN:(e + 1) * MOE_ALIGN, :]
            owned = row < jnp.where(k == 0, head, 0)
            stage[slot, r0:r0 + MOE_ALIGN, :] = jnp.where(owned, kept, stage[slot, r0:r0 + MOE_ALIGN, :])
            nxt = (head + start_ref[(j + 1) * ne + e] - s) & (-MOE_ALIGN)
            here = (nxt >= lo) & (nxt < lo + MOE_WIN)
            off = pl.multiple_of(jnp.clip(nxt - lo, 0, MOE_WIN - MOE_ALIGN), MOE_ALIGN)
            cand = stage[slot, pl.ds(r0 + off, MOE_ALIGN), :]
            carry[e * MOE_ALIGN:(e + 1) * MOE_ALIGN, :] = jnp.where(here, cand, kept)

        @pl.when((k == 0) & (j > 0))
        def _():
            wait_windows(1 - slot)

        for e in range(ne):
            base = pl.multiple_of((start_ref[j * ne + e] & (-MOE_ALIGN)) + lo, MOE_ALIGN)
            window_copy(slot, e, base).start()
        return 0

    lax.fori_loop(0, kmax_ref[j], block, 0)

    @pl.when(j == nt - 1)
    def _():
        wait_windows(slot)


def _dispatch(xn, val, starts, kmax, tgt, eexp, cap, tile):
    n = xn.shape[0]
    nt = n // tile
    rows = N_EXPERTS * MOE_WIN
    return pl.pallas_call(
        _dispatch_kernel,
        grid_spec=pltpu.PrefetchScalarGridSpec(
            num_scalar_prefetch=2, grid=(nt,),
            in_specs=[pl.BlockSpec((tile, D_MODEL), lambda j, s, k: (j, 0)),
                      pl.BlockSpec((N_EXPERTS, tile), lambda j, s, k: (0, j)),
                      pl.BlockSpec((None, 1, rows), lambda j, s, k: (j, 0, 0)),
                      pl.BlockSpec((N_EXPERTS, rows), lambda j, s, k: (0, 0))],
            out_specs=pl.BlockSpec(memory_space=pl.ANY),
            scratch_shapes=[pltpu.VMEM((2, rows, D_MODEL), BF16),
                            pltpu.VMEM((N_EXPERTS * MOE_ALIGN, D_MODEL), BF16),
                            pltpu.SemaphoreType.DMA((2,))]),
        out_shape=jax.ShapeDtypeStruct((N_EXPERTS, cap + MOE_PAD, D_MODEL), BF16),
        compiler_params=_params(("arbitrary",)),
        name="moe_dispatch",
    )(starts, kmax, xn, val, tgt, eexp)


def _expert_kernel(x_ref, wr_ref, wg32_ref, wu32_ref, wd32_ref, y_ref, wg_ref, wu_ref, wd_ref, *, npad):
    e = pl.program_id(0)
    i = pl.program_id(1)

    @pl.when(i == npad)
    def _():
        wg_ref[...] = wg32_ref[...].astype(BF16)
        wu_ref[...] = wu32_ref[...].astype(BF16)
        wd_ref[...] = wd32_ref[...].astype(BF16)

    @pl.when(i >= npad)
    def _():
        x = x_ref[...]
        logits = (jnp.dot(x, wr_ref[0], preferred_element_type=F32)
                  + jnp.dot(x, wr_ref[1], preferred_element_type=F32))
        lane = lax.broadcasted_iota(jnp.int32, logits.shape, 1)
        logits = jnp.where(lane < N_EXPERTS, logits, -jnp.inf)
        ex = jnp.exp(logits - jnp.max(logits, axis=-1, keepdims=True))
        gate = (jnp.sum(jnp.where(lane == e, ex, 0.0), axis=-1, keepdims=True)
                / jnp.sum(ex, axis=-1, keepdims=True))
        hid = []
        for c0 in range(0, EXPERT_FF, FF_CHUNK):
            hg = jnp.dot(x, wg_ref[:, c0:c0 + FF_CHUNK], preferred_element_type=F32)
            hu = jnp.dot(x, wu_ref[:, c0:c0 + FF_CHUNK], preferred_element_type=F32)
            hid.append((_silu(hg) * hu).astype(BF16))
        hid = jnp.concatenate(hid, axis=1)
        y_ref[...] = (jnp.dot(hid, wd_ref[...], preferred_element_type=F32) * gate).astype(BF16)

    @pl.when(i < npad)
    def _():
        y_ref[...] = jnp.zeros_like(y_ref)


def _expert_ffn(xe, wr2, wg, wu, wd, layer, cap, tc):
    ne, rows, _ = xe.shape
    ntile = cap // tc
    npad = rows // tc - ntile
    wspec = lambda shape: pl.BlockSpec((None, None) + shape, lambda e, i: (layer, e, 0, 0))
    return pl.pallas_call(
        functools.partial(_expert_kernel, npad=npad),
        grid=(ne, rows // tc),
        in_specs=[pl.BlockSpec((None, tc, D_MODEL), lambda e, i: (e, jnp.maximum(i - npad, 0), 0)),
                  pl.BlockSpec(wr2.shape, lambda e, i: (0, 0, 0)),
                  wspec((D_MODEL, EXPERT_FF)), wspec((D_MODEL, EXPERT_FF)), wspec((EXPERT_FF, D_MODEL))],
        out_specs=pl.BlockSpec((None, tc, D_MODEL),
                               lambda e, i: (e, jnp.where(i < npad, ntile + i, i - npad), 0)),
        out_shape=jax.ShapeDtypeStruct((ne, rows, D_MODEL), BF16),
        scratch_shapes=[pltpu.VMEM((D_MODEL, EXPERT_FF), BF16), pltpu.VMEM((D_MODEL, EXPERT_FF), BF16),
                        pltpu.VMEM((EXPERT_FF, D_MODEL), BF16)],
        compiler_params=pltpu.CompilerParams(dimension_semantics=("parallel", "arbitrary"),
                                             vmem_limit_bytes=FFN_VMEM_LIMIT),
        name="expert_ffn",
    )(xe, wr2, wg, wu, wd)


COMBINE_SUB = 2


def _combine_kernel(start_ref, kmax_ref, h_ref, p_ref, val_ref, tgt_ref, eexp_ref, gain_ref, wg_ref, wp_ref,
                    ye_ref, out_ref, stage, sem, *, tile):
    j = pl.program_id(0)
    nstep = pl.num_programs(0)
    slot = lax.rem(j, 2)
    ne = N_EXPERTS
    subs = range(COMBINE_SUB)

    def window_copy(sl, sub, e, row0):
        buf = sl * COMBINE_SUB + sub
        return pltpu.make_async_copy(ye_ref.at[e, pl.ds(row0, MOE_WIN)],
                                     stage.at[buf, pl.ds(e * MOE_WIN, MOE_WIN)], sem.at[buf])

    def fetch(sl, sub, tile_idx, lo):
        for e in range(ne):
            base = pl.multiple_of((start_ref[tile_idx * ne + e] & (-MOE_ALIGN)) + lo, MOE_ALIGN)
            window_copy(sl, sub, e, base).start()

    def wait_windows(sl, sub):
        for e in range(ne):
            window_copy(sl, sub, e, 0).wait()

    @pl.when(j == 0)
    def _():
        for sub in subs:
            fetch(slot, sub, sub, 0)

    @pl.when(j + 1 < nstep)
    def _():
        for sub in subs:
            fetch(1 - slot, sub, (j + 1) * COMBINE_SUB + sub, 0)

    reps = [lax.dot_general(val_ref[:, sub * tile:(sub + 1) * tile].astype(BF16), eexp_ref[...],
                            (((0,), (0,)), ((), ())), preferred_element_type=F32) for sub in subs]
    pts = [(reps[sub] == tgt_ref[sub]).astype(BF16) for sub in subs]
    for sub in subs:
        wait_windows(slot, sub)
    accs = [h_ref[sub * tile:(sub + 1) * tile, :]
            + jnp.dot(pts[sub], stage[slot * COMBINE_SUB + sub], preferred_element_type=F32) for sub in subs]
    for sub in subs:
        t = j * COMBINE_SUB + sub

        def extra(k, acc, sub=sub, t=t):
            lo = k * MOE_WIN
            fetch(slot, sub, t, lo)
            wait_windows(slot, sub)
            pk = (reps[sub] == tgt_ref[sub] + lo.astype(F32)).astype(BF16)
            return acc + jnp.dot(pk, stage[slot * COMBINE_SUB + sub], preferred_element_type=F32)

        accs[sub] = lax.fori_loop(1, kmax_ref[t], extra, accs[sub])
    x = jnp.concatenate(accs, axis=0)
    ms = jnp.mean(x * x, axis=-1, keepdims=True)
    xn = (x * lax.rsqrt(ms + NORM_EPS) * gain_ref[...]).astype(BF16)
    gate = _sigmoid(jnp.dot(xn, wg_ref[...], preferred_element_type=F32))
    proj = jnp.dot(p_ref[...].astype(BF16), wp_ref[...], preferred_element_type=F32)
    out_ref[...] = x + gate * proj


def _combine_ple(h2, p3, layer, ye, val, starts, kmax, tgt, eexp, gain, wg, wp, tile):
    n = h2.shape[0]
    step = COMBINE_SUB * tile
    assert n % step == 0
    rows = N_EXPERTS * MOE_WIN
    fixed = lambda j, s, k: (0, 0)
    return pl.pallas_call(
        functools.partial(_combine_kernel, tile=tile),
        grid_spec=pltpu.PrefetchScalarGridSpec(
            num_scalar_prefetch=2, grid=(n // step,),
            in_specs=[pl.BlockSpec((step, D_MODEL), lambda j, s, k: (j, 0)),
                      pl.BlockSpec((None, step, PLE_DIM), lambda j, s, k: (layer, j, 0)),
                      pl.BlockSpec((N_EXPERTS, step), lambda j, s, k: (0, j)),
                      pl.BlockSpec((COMBINE_SUB, 1, rows), lambda j, s, k: (j, 0, 0)),
                      pl.BlockSpec((N_EXPERTS, rows), fixed),
                      pl.BlockSpec((1, D_MODEL), fixed), pl.BlockSpec(wg.shape, fixed),
                      pl.BlockSpec(wp.shape, fixed),
                      pl.BlockSpec(memory_space=pl.ANY)],
            out_specs=pl.BlockSpec((step, D_MODEL), lambda j, s, k: (j, 0)),
            scratch_shapes=[pltpu.VMEM((2 * COMBINE_SUB, rows, D_MODEL), BF16),
                            pltpu.SemaphoreType.DMA((2 * COMBINE_SUB,))]),
        out_shape=jax.ShapeDtypeStruct((n, D_MODEL), F32),
        compiler_params=_params(("arbitrary",)),
        name="moe_combine_ple",
    )(starts, kmax, h2, p3, val, tgt, eexp, gain, wg, wp, ye)


def _relayout_w_in(w_in):
    o_beta = ZA + ZB + DN_WIDTH
    o_alpha = o_beta + 2 * DN_HEADS
    o_glu = o_alpha + 2 * DN_HEADS
    pieces = [w_in[:, :o_beta], w_in[:, o_glu:o_glu + 2 * CONV_CH]]
    for d in range(2):
        pieces.append(w_in[:, o_beta + d * DN_HEADS:o_beta + (d + 1) * DN_HEADS])
        pieces.append(w_in[:, o_alpha + d * DN_HEADS:o_alpha + (d + 1) * DN_HEADS])
    pieces.append(jnp.zeros((w_in.shape[0], LANES - 4 * DN_HEADS), w_in.dtype))
    return jnp.concatenate(pieces, axis=1).astype(BF16)


def _prep_layer(lw):
    (norm_mix, w_in, q_gain, k_gain, sink, dn_conv, dn_a_log, dn_dt_bias, dn_out_gain,
     cv_dw, cv_dw_bias, cv_ln_gain, cv_ln_bias, w_out, norm_ffn, w_router, w_gate, w_up, w_down,
     norm_ple, w_ple_gate, w_ple_proj) = lw
    w_perm = _relayout_w_in(w_in)
    hgain = jnp.concatenate([jnp.tile(q_gain, ATT_HEADS) * (ATT_HEAD_DIM ** -0.5),
                             jnp.tile(k_gain, ATT_KV_HEADS)]).reshape(1, -1)
    zeros4 = jnp.zeros((DN_HEADS,), F32)
    aneg = -jnp.exp(dn_a_log.astype(F32))
    aneg_row = jnp.concatenate([zeros4, aneg[0], zeros4, aneg[1]])
    dtb_row = jnp.concatenate([zeros4, dn_dt_bias[0], zeros4, dn_dt_bias[1]])
    pad = lambda r: jnp.pad(r, (0, LANES - r.shape[0])).reshape(1, LANES)
    wr = jnp.pad(w_router.astype(F32), ((0, 0), (0, LANES - N_EXPERTS)))
    wr_hi = wr.astype(BF16)
    wr2 = jnp.stack([wr_hi, (wr - wr_hi.astype(F32)).astype(BF16)])
    return dict(
        w_router2=wr2,
        norm_mix=norm_mix.reshape(1, -1), w_in=w_perm, hgain=hgain, sink=sink.astype(F32),
        dn_conv=dn_conv, aneg=pad(aneg_row), dtb=pad(dtb_row),
        dn_out_gain=jnp.tile(dn_out_gain, DN_HEADS).reshape(1, -1),
        cv_dw=cv_dw, cv_dw_bias=cv_dw_bias.reshape(1, -1), cv_ln_gain=cv_ln_gain.reshape(1, -1),
        cv_ln_bias=cv_ln_bias.reshape(1, -1), w_out=w_out.astype(BF16),
        norm_ffn=norm_ffn.reshape(1, -1),
        norm_ple=norm_ple.reshape(1, -1), w_ple_gate=w_ple_gate.astype(BF16), w_ple_proj=w_ple_proj.astype(BF16))


def _tiles(bsz, seqlen):
    n = bsz * seqlen
    return dict(tm=min(1024, n), tl=min(512, seqlen), ch=min(256, seqlen), tcv=min(1024, seqlen))


def _moe_ple(h2, xn, aff_t, p3, layer, pw):
    n = h2.shape[0]
    cap = CAPACITY_FACTOR * n // N_EXPERTS
    tile = min(MOE_TILE, n)
    val, cnt = _select(aff_t, cap, tile)
    starts, kmax, tgt = _moe_plan(cnt, n // tile)
    eexp = _expand_matrix()
    xe = _dispatch(xn, val, starts, kmax, tgt, eexp, cap, tile)
    ye = _expert_ffn(xe, pw["w_router2"], pw["w_gate"], pw["w_up"], pw["w_down"], layer, cap,
                     min(MOE_FFN_TILE, cap))
    return _combine_ple(h2, p3, layer, ye, val, starts, kmax, tgt, eexp, pw["norm_ple"], pw["w_ple_gate"],
                        pw["w_ple_proj"], tile)


def _layer(h2, p3, layer, pw, bsz, seqlen):
    t = _tiles(bsz, seqlen)
    hm_att = _head_mean_matrix(ATT_Q + ATT_KV, ATT_HEAD_DIM)
    hs_dn = _head_sum_matrix(2 * DN_WIDTH, DN_HEAD_DIM)
    hm_dn = _head_mean_matrix(DN_WIDTH, DN_HEAD_DIM)
    za, zb, zg, glu_in, gates = _in_proj(h2, pw["norm_mix"], pw["w_in"], hm_att, pw["hgain"], t["tm"])
    o_a = _attention(za, pw["sink"], bsz, seqlen)
    y, gb = _dn_prep(zb, gates, pw["dn_conv"], hs_dn, pw["aneg"], pw["dtb"], bsz, seqlen, t["tl"])
    o_f, o_b = _dn_chunk(y, gb, bsz, seqlen, t["ch"])
    o_c = _conformer_conv(glu_in, pw["cv_dw"], pw["cv_dw_bias"], pw["cv_ln_gain"], pw["cv_ln_bias"],
                          bsz, seqlen, t["tcv"])
    h2, xn, aff_t = _out_proj_route(h2, o_a, o_f, o_b, zg, o_c, pw["dn_out_gain"], hm_dn, pw["w_out"],
                                    pw["norm_ffn"], pw["w_router2"], t["tm"])
    return _moe_ple(h2, xn, aff_t, p3, layer, pw)


def _trunk(x, p, layer_weights):
    bsz, seqlen, _ = x.shape
    h2 = x.reshape(bsz * seqlen, D_MODEL)
    p3 = p.reshape(p.shape[0], bsz * seqlen, PLE_DIM)
    for i, pw in enumerate(layer_weights):
        h2 = _layer(h2, p3, i, pw, bsz, seqlen)
    return h2.reshape(bsz, seqlen, D_MODEL)


def kernel(x_prompt, x_sample, p_prompt, p_sample, norm_mix, w_in, q_gain, k_gain, sink, dn_conv, dn_a_log,
           dn_dt_bias, dn_out_gain, cv_dw, cv_dw_bias, cv_ln_gain, cv_ln_bias, w_out, norm_ffn, w_router,
           w_gate, w_up, w_down, norm_ple, w_ple_gate, w_ple_proj):
    weights = (norm_mix, w_in, q_gain, k_gain, sink, dn_conv, dn_a_log, dn_dt_bias, dn_out_gain,
               cv_dw, cv_dw_bias, cv_ln_gain, cv_ln_bias, w_out, norm_ffn, w_router, w_gate, w_up, w_down,
               norm_ple, w_ple_gate, w_ple_proj)
    depth = w_in.shape[0]
    experts = dict(w_gate=w_gate, w_up=w_up, w_down=w_down)
    layer_weights = [dict(_prep_layer([w[i] for w in weights]), **experts) for i in range(depth)]
    return (_trunk(x_prompt, p_prompt, layer_weights), _trunk(x_sample, p_sample, layer_weights))
```

```python
import functools
import math

import numpy as np
import jax
import jax.numpy as jnp
from jax import lax
from jax.experimental import pallas as pl
from jax.experimental.pallas import tpu as pltpu

F32 = jnp.float32
BF16 = jnp.bfloat16

D_MODEL = 1024
ATT_HEADS = 8
ATT_KV_HEADS = 2
ATT_HEAD_DIM = 64
ATT_GROUP = ATT_HEADS // ATT_KV_HEADS
WINDOW = 128
ATT_BLOCK = 128
DN_HEADS = 4
DN_HEAD_DIM = 64
DN_WIDTH = DN_HEADS * DN_HEAD_DIM
DN_CHUNK = 64
CONV_CH = 256
CONV_WIDTH = 31
ATT_Q = ATT_HEADS * ATT_HEAD_DIM
ATT_KV = ATT_KV_HEADS * ATT_HEAD_DIM
N_EXPERTS = 16
CAPACITY_FACTOR = 2
EXPERT_FF = 1024
PLE_DIM = 256
NORM_EPS = 1e-6

LANES = 128
SUBLANES = 8
VMEM_LIMIT = 48 * 1024 * 1024
FFN_VMEM_LIMIT = 56 * 1024 * 1024

ZA = ATT_Q + 2 * ATT_KV
ZB = 3 * DN_WIDTH
ZW = ZA + ZB + DN_WIDTH + 2 * CONV_CH + LANES


def _params(sem):
    return pltpu.CompilerParams(dimension_semantics=sem, vmem_limit_bytes=VMEM_LIMIT)


def _head_mean_matrix(width, head):
    idx = np.arange(width) // head
    return jnp.asarray((idx[:, None] == idx[None, :]).astype(np.float32) / head, dtype=BF16)


def _head_sum_matrix(width, head):
    idx = np.arange(width) // head
    return jnp.asarray((idx[:, None] == idx[None, :]).astype(np.float32), dtype=BF16)


def _sigmoid(x):
    return 1.0 / (1.0 + jnp.exp(-x))


def _silu(x):
    return x * _sigmoid(x)


def _in_proj_kernel(x_ref, gain_ref, w_ref, hm_ref, hgain_ref, za_ref, zb_ref, zg_ref, glu_ref, gates_ref):
    x = x_ref[...]
    ms = jnp.mean(x * x, axis=-1, keepdims=True)
    a = (x * lax.rsqrt(ms + NORM_EPS) * gain_ref[...]).astype(BF16)
    z = jnp.dot(a, w_ref[...], preferred_element_type=F32)
    nqk = ATT_Q + ATT_KV
    qk = z[:, :nqk]
    hms = jnp.dot((qk * qk).astype(BF16), hm_ref[...], preferred_element_type=F32)
    za_ref[:, :nqk] = (qk * lax.rsqrt(hms + NORM_EPS) * hgain_ref[...]).astype(BF16)
    za_ref[:, nqk:] = z[:, nqk:ZA].astype(BF16)
    zb_ref[...] = z[:, ZA:ZA + ZB]
    zg_ref[...] = z[:, ZA + ZB:ZA + ZB + DN_WIDTH]
    glu_ref[...] = z[:, ZA + ZB + DN_WIDTH:ZA + ZB + DN_WIDTH + 2 * CONV_CH]
    gates_ref[...] = z[:, ZW - LANES:]


def _in_proj(h2, gain, w_perm, hm, hgain, tm):
    n = h2.shape[0]
    row = lambda i: (i, 0)
    fixed = lambda i: (0, 0)
    return pl.pallas_call(
        _in_proj_kernel,
        grid=(n // tm,),
        in_specs=[pl.BlockSpec((tm, D_MODEL), row), pl.BlockSpec((1, D_MODEL), fixed),
                  pl.BlockSpec((D_MODEL, ZW), fixed), pl.BlockSpec(hm.shape, fixed),
                  pl.BlockSpec(hgain.shape, fixed)],
        out_specs=[pl.BlockSpec((tm, ZA), row), pl.BlockSpec((tm, ZB), row), pl.BlockSpec((tm, DN_WIDTH), row),
                   pl.BlockSpec((tm, 2 * CONV_CH), row), pl.BlockSpec((tm, LANES), row)],
        out_shape=[jax.ShapeDtypeStruct((n, ZA), BF16), jax.ShapeDtypeStruct((n, ZB), F32),
                   jax.ShapeDtypeStruct((n, DN_WIDTH), F32), jax.ShapeDtypeStruct((n, 2 * CONV_CH), F32),
                   jax.ShapeDtypeStruct((n, LANES), F32)],
        compiler_params=_params(("parallel",)),
        name="in_proj",
    )(h2, gain, w_perm, hm, hgain)


ATT_MASKED = -1e30


def _attn_bias_table():
    i = np.arange(ATT_BLOCK)[:, None]
    c = np.arange(3 * ATT_BLOCK)[None, :]
    rel = c - ATT_BLOCK - i
    slopes = 2.0 ** (-8.0 * np.arange(1, ATT_HEADS + 1) / ATT_HEADS)
    table = np.empty((3, ATT_KV_HEADS, ATT_GROUP * ATT_BLOCK, 3 * ATT_BLOCK), np.float32)
    for variant in range(3):
        ok = np.abs(rel) <= WINDOW
        if variant == 0:
            ok = ok & (c >= ATT_BLOCK)
        if variant == 2:
            ok = ok & (c < 2 * ATT_BLOCK)
        for hd in range(ATT_HEADS):
            g, j = divmod(hd, ATT_GROUP)
            table[variant, g, j * ATT_BLOCK:(j + 1) * ATT_BLOCK] = np.where(ok, -slopes[hd] * np.abs(rel), ATT_MASKED)
    return jnp.asarray(table)


def _attn_kernel(sink_ref, q_ref, kvp_ref, kvo_ref, kvn_ref, bias_a_ref, bias_b_ref, o_ref):
    kv = jnp.concatenate([kvp_ref[...], kvo_ref[...], kvn_ref[...]], axis=0)
    hd_ = ATT_HEAD_DIM
    groups = range(ATT_KV_HEADS)
    heads = range(ATT_HEADS)
    rows = lambda t, hd: t[(hd % ATT_GROUP) * ATT_BLOCK:(hd % ATT_GROUP + 1) * ATT_BLOCK]
    work = []
    for blk, bias_ref in enumerate((bias_a_ref, bias_b_ref)):
        keys = kv[blk * ATT_BLOCK:(blk + 3) * ATT_BLOCK]
        ks = [keys[:, g * hd_:(g + 1) * hd_] for g in groups]
        vs = [keys[:, ATT_KV + g * hd_:ATT_KV + (g + 1) * hd_] for g in groups]
        q = q_ref[blk * ATT_BLOCK:(blk + 1) * ATT_BLOCK, :]
        qs = [jnp.concatenate([q[:, (g * ATT_GROUP + j) * hd_:(g * ATT_GROUP + j + 1) * hd_]
                               for j in range(ATT_GROUP)], axis=0) for g in groups]
        sg = [lax.dot_general(qs[g], ks[g], (((1,), (1,)), ((), ())), preferred_element_type=F32) + bias_ref[g]
              for g in groups]
        work.append((vs, [rows(sg[hd // ATT_GROUP], hd) for hd in heads]))
    m = [[jnp.maximum(jnp.max(s[hd], axis=-1, keepdims=True), sink_ref[hd]) for hd in heads] for _, s in work]
    e = [[jnp.exp(s[hd] - mb[hd]) for hd in heads] for (_, s), mb in zip(work, m)]
    den = [[jnp.sum(eb[hd], axis=-1, keepdims=True) + jnp.exp(sink_ref[hd] - mb[hd]) for hd in heads]
           for eb, mb in zip(e, m)]
    for blk, ((vs, _), eb, db) in enumerate(zip(work, e, den)):
        eg = [jnp.concatenate([eb[g * ATT_GROUP + j].astype(BF16) for j in range(ATT_GROUP)], axis=0)
              for g in groups]
        og = [jnp.dot(eg[g], vs[g], preferred_element_type=F32) for g in groups]
        for hd in heads:
            o_ref[blk * ATT_BLOCK:(blk + 1) * ATT_BLOCK, hd * hd_:(hd + 1) * hd_] = (
                rows(og[hd // ATT_GROUP], hd) / db[hd]).astype(BF16)


def _attention(za, sink, bsz, seqlen):
    nb = seqlen // ATT_BLOCK
    assert nb >= 2 and nb % 2 == 0
    npair = nb // 2
    za3 = za.reshape(bsz, seqlen, ZA)
    kvw = 2 * ATT_KV
    kvc = ATT_Q // kvw
    bias = _attn_bias_table()
    bias_spec = lambda pick: pl.BlockSpec((None,) + bias.shape[1:], lambda b, n: (pick(n), 0, 0, 0))
    return pl.pallas_call(
        _attn_kernel,
        grid=(bsz, npair),
        in_specs=[pl.BlockSpec(memory_space=pltpu.SMEM),
                  pl.BlockSpec((None, 2 * ATT_BLOCK, ATT_Q), lambda b, n: (b, n, 0)),
                  pl.BlockSpec((None, ATT_BLOCK, kvw), lambda b, n: (b, jnp.maximum(2 * n - 1, 0), kvc)),
                  pl.BlockSpec((None, 2 * ATT_BLOCK, kvw), lambda b, n: (b, n, kvc)),
                  pl.BlockSpec((None, ATT_BLOCK, kvw), lambda b, n: (b, jnp.minimum(2 * n + 2, nb - 1), kvc)),
                  bias_spec(lambda n: jnp.where(n == 0, 0, 1)),
                  bias_spec(lambda n: jnp.where(n == npair - 1, 2, 1))],
        out_specs=pl.BlockSpec((None, 2 * ATT_BLOCK, ATT_Q), lambda b, n: (b, n, 0)),
        out_shape=jax.ShapeDtypeStruct((bsz, seqlen, ATT_Q), BF16),
        compiler_params=_params(("parallel", "parallel")),
        name="window_attention",
    )(sink, za3, za3, za3, za3, bias, bias).reshape(bsz * seqlen, ATT_Q)


DN_HALO = SUBLANES


def _dn_prep_kernel(x_ref, xp_ref, xn_ref, cw_ref, hs_ref, g_ref, aneg_ref, dtb_ref, mf_ref, mb_ref,
                    y_ref, gb_ref, buf_ref, *, tl):
    i = pl.program_id(1)
    nt = pl.num_programs(1)
    buf_ref[0:DN_HALO, :] = jnp.where(i > 0, xp_ref[...], 0.0)
    buf_ref[DN_HALO:DN_HALO + tl, :] = x_ref[...]
    buf_ref[DN_HALO + tl:, :] = jnp.where(i < nt - 1, xn_ref[...], 0.0)
    y = (cw_ref[0:1, :] * buf_ref[DN_HALO - 1:DN_HALO - 1 + tl, :]
         + cw_ref[1:2, :] * buf_ref[DN_HALO:DN_HALO + tl, :]
         + cw_ref[2:3, :] * buf_ref[DN_HALO + 1:DN_HALO + 1 + tl, :])
    y = _silu(y)
    qk = y[:, :2 * DN_WIDTH]
    ss = jnp.dot((qk * qk).astype(BF16), hs_ref[...], preferred_element_type=F32)
    lane = lax.broadcasted_iota(jnp.int32, (tl, 2 * DN_WIDTH), 1)
    scale = jnp.where(lane < DN_WIDTH, DN_HEAD_DIM ** -0.5, 1.0)
    y_ref[:, :2 * DN_WIDTH] = qk * lax.rsqrt(ss + NORM_EPS) * scale
    y_ref[:, 2 * DN_WIDTH:] = y[:, 2 * DN_WIDTH:]
    raw = g_ref[...]
    col = lax.broadcasted_iota(jnp.int32, (tl, LANES), 1)
    is_beta = (col & DN_HEADS) == 0
    t = raw + dtb_ref[...]
    softplus = jnp.maximum(t, 0.0) + jnp.log(1.0 + jnp.exp(-jnp.abs(t)))
    vals = jnp.where(is_beta, _sigmoid(raw), aneg_ref[...] * softplus)
    v_hi = vals.astype(BF16)
    r1 = vals - v_hi.astype(F32)
    v_mid = r1.astype(BF16)
    v_lo = (r1 - v_mid.astype(F32)).astype(BF16)
    terms = jnp.concatenate([v_hi, v_mid, v_lo], axis=1)
    cf3 = jnp.dot(mf_ref[...], terms, preferred_element_type=F32)
    cb3 = jnp.dot(mb_ref[...], terms, preferred_element_type=F32)
    cf = cf3[:, :LANES] + (cf3[:, LANES:2 * LANES] + cf3[:, 2 * LANES:])
    cb = cb3[:, :LANES] + (cb3[:, LANES:2 * LANES] + cb3[:, 2 * LANES:])
    gb_ref[0] = jnp.where(is_beta, vals, cf)
    gb_ref[1] = pltpu.roll(jnp.where(is_beta, vals, cb), LANES - 2 * DN_HEADS, axis=1)


def _dn_prep(zb, gates, conv_w, hs, aneg, dtb, bsz, seqlen, tl):
    zb3 = zb.reshape(bsz, seqlen, ZB)
    g3 = gates.reshape(bsz, seqlen, LANES)
    nt = seqlen // tl
    hb = tl // DN_HALO
    ch = np.arange(tl) // DN_CHUNK
    same = ch[:, None] == ch[None, :]
    pos = np.arange(tl)
    mf = jnp.asarray((same & (pos[None, :] <= pos[:, None])).astype(np.float32), dtype=BF16)
    mb = jnp.asarray((same & (pos[None, :] >= pos[:, None])).astype(np.float32), dtype=BF16)
    fixed = lambda b, i: (0, 0)
    y, gb = pl.pallas_call(
        functools.partial(_dn_prep_kernel, tl=tl),
        grid=(bsz, nt),
        in_specs=[pl.BlockSpec((None, tl, ZB), lambda b, i: (b, i, 0)),
                  pl.BlockSpec((None, DN_HALO, ZB), lambda b, i: (b, jnp.maximum(i * hb - 1, 0), 0)),
                  pl.BlockSpec((None, DN_HALO, ZB), lambda b, i: (b, jnp.minimum((i + 1) * hb, nt * hb - 1), 0)),
                  pl.BlockSpec(conv_w.shape, fixed), pl.BlockSpec(hs.shape, fixed),
                  pl.BlockSpec((None, tl, LANES), lambda b, i: (b, i, 0)),
                  pl.BlockSpec((1, LANES), fixed), pl.BlockSpec((1, LANES), fixed),
                  pl.BlockSpec((tl, tl), fixed), pl.BlockSpec((tl, tl), fixed)],
        out_specs=[pl.BlockSpec((None, tl, ZB), lambda b, i: (b, i, 0)),
                   pl.BlockSpec((2, None, tl, LANES), lambda b, i: (0, b, i, 0))],
        out_shape=[jax.ShapeDtypeStruct((bsz, seqlen, ZB), F32),
                   jax.ShapeDtypeStruct((2, bsz, seqlen, LANES), F32)],
        scratch_shapes=[pltpu.VMEM((tl + 2 * DN_HALO, ZB), F32)],
        compiler_params=_params(("parallel", "parallel")),
        name="deltanet_prep",
    )(zb3, zb3, zb3, conv_w, hs, g3, aneg, dtb, mf, mb)
    return y, gb


def _lane_expand(cols, first):
    c = cols.shape[0]
    lane = lax.broadcasted_iota(jnp.int32, (c, LANES), 1)
    halves = []
    for h in range(0, DN_HEADS, 2):
        a = jnp.broadcast_to(cols[:, first + h:first + h + 1], (c, LANES))
        b = jnp.broadcast_to(cols[:, first + h + 1:first + h + 2], (c, LANES))
        halves.append(jnp.where(lane < DN_HEAD_DIM, a, b))
    return jnp.concatenate(halves, axis=1)


def _dn_pair_kernel(xf_ref, xb_ref, gf_ref, gb_ref, of_ref, ob_ref, sf_ref, sb_ref, *, nsub):
    c = DN_CHUNK
    w = DN_WIDTH

    @pl.when(pl.program_id(1) == 0)
    def _():
        sf_ref[...] = jnp.zeros_like(sf_ref)
        sb_ref[...] = jnp.zeros_like(sb_ref)

    r_cat = lax.broadcasted_iota(jnp.int32, (c, w), 0)
    s_cat = lax.broadcasted_iota(jnp.int32, (c, w), 1) & (DN_HEAD_DIM - 1)
    eye_cat = s_cat == r_cat
    rr = lax.broadcasted_iota(jnp.int32, (w, w), 0)
    cc = lax.broadcasted_iota(jnp.int32, (w, w), 1)
    head = (rr >> 6) == (cc >> 6)
    head_b = head.astype(BF16)
    m16 = (s_cat >> 4) == (r_cat >> 4)
    m32 = (s_cat >> 5) == (r_cat >> 5)
    off16 = m32 & jnp.logical_not(m16)
    off32 = jnp.logical_not(m32)
    eye_f = eye_cat.astype(F32)

    def bd(t):
        return jnp.concatenate([t] * DN_HEADS, axis=0) * head_b

    def mm(a, b):
        return jnp.dot(a, b, preferred_element_type=F32)

    chunks = [(0, i * c) for i in range(nsub)] + [(1, (nsub - 1 - i) * c) for i in range(nsub)]
    xrefs = (xf_ref, xb_ref)
    grefs = (gf_ref, gb_ref)
    orefs = (of_ref, ob_ref)
    srefs = (sf_ref, sb_ref)
    incl = (s_cat <= r_cat, s_cat >= r_cat)
    strict = (s_cat < r_cat, s_cat > r_cat)
    last_row = (c - 1, 0)

    pre = []
    for d, st in chunks:
        x = xrefs[d][st:st + c, :]
        q, k, v = x[:, :w], x[:, w:2 * w], x[:, 2 * w:]
        gbt = grefs[d][st:st + c, :]
        beta = _lane_expand(gbt, 0)
        gc = _lane_expand(gbt, DN_HEADS)
        grow = jnp.sum(jnp.where(eye_cat, gc, 0.0), axis=0, keepdims=True)
        decay = jnp.exp(jnp.where(incl[d], gc - grow, -jnp.inf))
        glast = gc[last_row[d]:last_row[d] + 1, :]
        egc = jnp.exp(gc)
        kb = k * beta
        pre.append(dict(d=d, st=st, q=q, k=k, kb=kb, vb=v * beta, decay=decay, glast=glast, egc=egc,
                        kdec=(k * jnp.exp(glast - gc)).astype(BF16)))

    kks = [lax.dot_general(jnp.concatenate([p["kb"], p["q"]], axis=0).astype(BF16), bd(p["k"].astype(BF16)),
                           (((1,), (1,)), ((), ())), preferred_element_type=F32) for p in pre]
    a = [jnp.where(strict[p["d"]], kk[:c] * p["decay"], 0.0) for p, kk in zip(pre, kks)]
    intra = [jnp.where(incl[p["d"]], kk[c:] * p["decay"], 0.0).astype(BF16) for p, kk in zip(pre, kks)]
    xm = [jnp.where(m16, -t, 0.0) for t in a]
    xm_b = [t.astype(BF16) for t in xm]
    x2_b = [mm(t, bd(t)).astype(BF16) for t in xm_b]
    x2_d = [bd(t) for t in x2_b]
    dinv = [eye_f + t for t in xm]
    r2 = [mm(jnp.concatenate([t.astype(BF16), p2], axis=0), s2) for t, p2, s2 in zip(dinv, x2_b, x2_d)]
    dinv = [t + r[:c] for t, r in zip(dinv, r2)]
    x4_b = [r[c:].astype(BF16) for r in r2]
    x4_d = [bd(t) for t in x4_b]
    r4 = [mm(jnp.concatenate([t.astype(BF16), p4], axis=0), s4) for t, p4, s4 in zip(dinv, x4_b, x4_d)]
    dinv = [t + r[:c] for t, r in zip(dinv, r4)]
    x8_d = [bd(r[c:].astype(BF16)) for r in r4]
    dinv = [t + mm(t.astype(BF16), s8) for t, s8 in zip(dinv, x8_d)]
    dinv_b = [t.astype(BF16) for t in dinv]
    n32 = [bd(mm(jnp.where(off16, t, 0.0).astype(BF16), bd(db)).astype(BF16)) for t, db in zip(a, dinv_b)]
    t32 = [t - mm(db, n) for t, db, n in zip(dinv, dinv_b, n32)]
    t32_b = [t.astype(BF16) for t in t32]
    n64 = [bd(mm(jnp.where(off32, t, 0.0).astype(BF16), bd(tb)).astype(BF16)) for t, tb in zip(a, t32_b)]
    t_cat = [(t - mm(tb, n)).astype(BF16) for t, tb, n in zip(t32, t32_b, n64)]
    uw = [mm(tc, jnp.concatenate([bd(p["vb"].astype(BF16)), bd((p["kb"] * p["egc"]).astype(BF16))], axis=1))
          for tc, p in zip(t_cat, pre)]
    uw_b = [t.astype(BF16) for t in uw]
    pn = [lax.dot_general(p["kdec"], t, (((0,), (0,)), ((), ())), preferred_element_type=F32)
          for p, t in zip(pre, uw_b)]
    qo = [mm(it, jnp.concatenate([bd(t[:, :w]), bd(t[:, w:])], axis=1)) for it, t in zip(intra, uw_b)]
    lhs = [jnp.concatenate([n[:, w:].astype(BF16) * head_b, (p["q"] * p["egc"] - o[:, w:]).astype(BF16)], axis=0)
           for n, o, p in zip(pn, qo, pre)]
    for step in range(nsub):
        for d in range(2):
            i = d * nsub + step
            p = pre[i]
            state = srefs[d][...]
            r = jnp.dot(lhs[i], state.astype(BF16), preferred_element_type=F32)
            orefs[d][p["st"]:p["st"] + c, :] = r[w:] + qo[i][:, :w]
            srefs[d][...] = state * jnp.exp(p["glast"]) - r[:w] + jnp.where(head, pn[i][:, :w], 0.0)


def _dn_chunk(y, gb, bsz, seqlen, ch):
    nsub = ch // DN_CHUNK
    nblk = seqlen // ch
    fwd = lambda b, j: (b, j, 0)
    bwd = lambda b, j: (b, nblk - 1 - j, 0)
    o_f, o_b = pl.pallas_call(
        functools.partial(_dn_pair_kernel, nsub=nsub),
        grid=(bsz, nblk),
        in_specs=[pl.BlockSpec((None, ch, ZB), fwd), pl.BlockSpec((None, ch, ZB), bwd),
                  pl.BlockSpec((None, None, ch, LANES), lambda b, j: (0, b, j, 0)),
                  pl.BlockSpec((None, None, ch, LANES), lambda b, j: (1, b, nblk - 1 - j, 0))],
        out_specs=[pl.BlockSpec((None, ch, DN_WIDTH), fwd), pl.BlockSpec((None, ch, DN_WIDTH), bwd)],
        out_shape=[jax.ShapeDtypeStruct((bsz, seqlen, DN_WIDTH), F32)] * 2,
        scratch_shapes=[pltpu.VMEM((DN_WIDTH, DN_WIDTH), F32)] * 2,
        compiler_params=_params(("parallel", "arbitrary")),
        name="deltanet_chunks",
    )(y, y, gb, gb)
    return o_f.reshape(bsz * seqlen, DN_WIDTH), o_b.reshape(bsz * seqlen, DN_WIDTH)


CV_HALO = 2 * SUBLANES
CV_PAD = (CONV_WIDTH - 1) // 2


def _conv_kernel(x_ref, xp_ref, xn_ref, dw_ref, bias_ref, lng_ref, lnb_ref, o_ref, buf_ref, shift_ref, *, tl):
    i = pl.program_id(1)
    nt = pl.num_programs(1)

    def glu(t):
        return t[:, :CONV_CH] * _sigmoid(t[:, CONV_CH:])

    buf_ref[0:CV_HALO, :] = jnp.where(i > 0, glu(xp_ref[...]), 0.0)
    buf_ref[CV_HALO:CV_HALO + tl, :] = glu(x_ref[...])
    buf_ref[CV_HALO + tl:, :] = jnp.where(i < nt - 1, glu(xn_ref[...]), 0.0)
    acc = jnp.zeros((tl, CONV_CH), F32) + bias_ref[...]
    first = CV_HALO - CV_PAD
    span = -(-(first + CONV_WIDTH) // SUBLANES) * SUBLANES - SUBLANES
    for sub in range(SUBLANES):
        shift_ref[...] = buf_ref[sub:sub + tl + span, :]
        for base in range(0, span + 1, SUBLANES):
            j = base + sub - first
            if 0 <= j < CONV_WIDTH:
                acc = acc + dw_ref[j:j + 1, :] * shift_ref[base:base + tl, :]
    mu = jnp.mean(acc, axis=-1, keepdims=True)
    cen = acc - mu
    var = jnp.mean(cen * cen, axis=-1, keepdims=True)
    o_ref[...] = _silu(cen * lax.rsqrt(var + NORM_EPS) * lng_ref[...] + lnb_ref[...]).astype(BF16)


def _conformer_conv(glu_in, dw, bias, lng, lnb, bsz, seqlen, tl):
    x3 = glu_in.reshape(bsz, seqlen, 2 * CONV_CH)
    nt = seqlen // tl
    hb = tl // CV_HALO
    fixed = lambda b, i: (0, 0)
    return pl.pallas_call(
        functools.partial(_conv_kernel, tl=tl),
        grid=(bsz, nt),
        in_specs=[pl.BlockSpec((None, tl, 2 * CONV_CH), lambda b, i: (b, i, 0)),
                  pl.BlockSpec((None, CV_HALO, 2 * CONV_CH), lambda b, i: (b, jnp.maximum(i * hb - 1, 0), 0)),
                  pl.BlockSpec((None, CV_HALO, 2 * CONV_CH),
                               lambda b, i: (b, jnp.minimum((i + 1) * hb, nt * hb - 1), 0)),
                  pl.BlockSpec(dw.shape, fixed), pl.BlockSpec((1, CONV_CH), fixed),
                  pl.BlockSpec((1, CONV_CH), fixed), pl.BlockSpec((1, CONV_CH), fixed)],
        out_specs=pl.BlockSpec((None, tl, CONV_CH), lambda b, i: (b, i, 0)),
        out_shape=jax.ShapeDtypeStruct((bsz, seqlen, CONV_CH), BF16),
        scratch_shapes=[pltpu.VMEM((tl + 2 * CV_HALO, CONV_CH), F32),
                        pltpu.VMEM((tl + 2 * CV_HALO - SUBLANES, CONV_CH), F32)],
        compiler_params=_params(("parallel", "parallel")),
        name="conformer_conv",
    )(x3, x3, x3, dw, bias, lng, lnb).reshape(bsz * seqlen, CONV_CH)


def _out_proj_kernel(h_ref, oa_ref, of_ref, ob_ref, zg_ref, oc_ref, og_ref, hm_ref, w_ref, gain_ref, wr_ref,
                     out_ref, xn_ref, aff_ref):
    half = h_ref.shape[0] // 2
    for r0 in (0, half):
        rs = slice(r0, r0 + half)
        ob = of_ref[rs, :] + ob_ref[rs, :]
        ms = jnp.dot((ob * ob).astype(BF16), hm_ref[...], preferred_element_type=F32)
        obn = ob * lax.rsqrt(ms + NORM_EPS) * og_ref[...]
        ob2 = obn * _silu(zg_ref[rs, :])
        mix = jnp.concatenate([oa_ref[rs, :], ob2.astype(BF16), oc_ref[rs, :]], axis=1)
        x = h_ref[rs, :] + jnp.dot(mix, w_ref[...], preferred_element_type=F32)
        out_ref[rs, :] = x
        ms = jnp.mean(x * x, axis=-1, keepdims=True)
        xn = x * lax.rsqrt(ms + NORM_EPS) * gain_ref[...]
        xn_hi = xn.astype(BF16)
        xn_ref[rs, :] = xn_hi
        xn_lo = (xn - xn_hi.astype(F32)).astype(BF16)
        logits = (jnp.dot(xn_hi, wr_ref[0], preferred_element_type=F32)
                  + (jnp.dot(xn_lo, wr_ref[0], preferred_element_type=F32)
                     + jnp.dot(xn_hi, wr_ref[1], preferred_element_type=F32)))
        lane = lax.broadcasted_iota(jnp.int32, logits.shape, 1)
        logits = jnp.where(lane < N_EXPERTS, logits, -jnp.inf)
        m = jnp.max(logits, axis=-1, keepdims=True)
        e = jnp.exp(logits - m)
        aff = e / jnp.sum(e, axis=-1, keepdims=True)
        aff_ref[:, rs] = jnp.transpose(aff)[:N_EXPERTS, :]


def _out_proj_route(h2, oa, o_f, o_b, zg, oc, og, hm, w, gain, wr2, tm):
    n = h2.shape[0]
    row = lambda i: (i, 0)
    fixed = lambda i: (0, 0)
    return pl.pallas_call(
        _out_proj_kernel,
        grid=(n // tm,),
        in_specs=[pl.BlockSpec((tm, D_MODEL), row), pl.BlockSpec((tm, ATT_Q), row),
                  pl.BlockSpec((tm, DN_WIDTH), row), pl.BlockSpec((tm, DN_WIDTH), row),
                  pl.BlockSpec((tm, DN_WIDTH), row),
                  pl.BlockSpec((tm, CONV_CH), row), pl.BlockSpec((1, DN_WIDTH), fixed),
                  pl.BlockSpec(hm.shape, fixed), pl.BlockSpec(w.shape, fixed),
                  pl.BlockSpec((1, D_MODEL), fixed), pl.BlockSpec((2, D_MODEL, LANES), lambda i: (0, 0, 0))],
        out_specs=[pl.BlockSpec((tm, D_MODEL), row), pl.BlockSpec((tm, D_MODEL), row),
                   pl.BlockSpec((N_EXPERTS, tm), lambda i: (0, i))],
        out_shape=[jax.ShapeDtypeStruct((n, D_MODEL), F32), jax.ShapeDtypeStruct((n, D_MODEL), BF16),
                   jax.ShapeDtypeStruct((N_EXPERTS, n), F32)],
        compiler_params=_params(("parallel",)),
        name="out_proj_route",
    )(h2, oa, o_f, o_b, zg, oc, og, hm, w, gain, wr2)


MOE_TILE = 256
MOE_ALIGN = 2 * SUBLANES
MOE_WIN = 64
MOE_PAD = 1024
MOE_FFN_TILE = 1024
FF_CHUNK = 256
MOE_UNSELECTED = -64.0


def _select_kernel(aff_ref, tri_ref, val_ref, cnt_ref, *, cap, tile):
    ne, n = aff_ref.shape
    nt = n // tile
    capf = float(cap)

    def bits_of(x):
        return lax.bitcast_convert_type(x, jnp.int32)

    def search(i, thr):
        cand = thr | jnp.left_shift(jnp.int32(1), 30 - i)
        cnt = jnp.sum((bits_of(aff_ref[...]) >= cand).astype(F32), axis=1, keepdims=True)
        return jnp.where(cnt >= capf, cand, thr)

    thr = lax.fori_loop(0, 31, search, jnp.zeros((ne, 1), jnp.int32))
    n_gt = jnp.sum((bits_of(aff_ref[...]) > thr).astype(F32), axis=1, keepdims=True)
    need = capf - n_gt
    lane = lax.broadcasted_iota(jnp.int32, (ne, LANES), 1)

    def tile_body(j, carry):
        eq_before, cnt_acc = carry
        off = pl.multiple_of(j * tile, tile)
        b = bits_of(aff_ref[:, pl.ds(off, tile)])
        gt = b > thr
        eqf = (b == thr).astype(F32)
        eq_rank = eq_before + jnp.dot(eqf.astype(BF16), tri_ref[...], preferred_element_type=F32)
        self_ = jnp.where(gt, 1.0, jnp.where(eq_rank <= need, eqf, 0.0))
        rank = jnp.dot(self_.astype(BF16), tri_ref[...], preferred_element_type=F32)
        val_ref[:, pl.ds(off, tile)] = jnp.where(self_ > 0.0, rank, MOE_UNSELECTED)
        cnt = jnp.sum(self_, axis=1, keepdims=True)
        return (eq_before + jnp.sum(eqf, axis=1, keepdims=True), cnt_acc + jnp.where(lane == j, cnt, 0.0))

    init = (jnp.zeros((ne, 1), F32), jnp.zeros((ne, LANES), F32))
    _, cnt_acc = lax.fori_loop(0, nt, tile_body, init, unroll=math.gcd(nt, 4))
    cnt_ref[...] = cnt_acc


def _select(aff_t, cap, tile):
    ne, n = aff_t.shape
    assert n // tile <= LANES
    tri = jnp.asarray(np.triu(np.ones((tile, tile), np.float32)), dtype=BF16)
    return pl.pallas_call(
        functools.partial(_select_kernel, cap=cap, tile=tile),
        out_shape=[jax.ShapeDtypeStruct((ne, n), F32), jax.ShapeDtypeStruct((ne, LANES), F32)],
        compiler_params=pltpu.CompilerParams(vmem_limit_bytes=VMEM_LIMIT),
        name="moe_select",
    )(aff_t, tri)


def _moe_plan(cnt, nt):
    c = cnt[:, :nt].astype(jnp.int32).T
    starts = jnp.concatenate([jnp.zeros((1, N_EXPERTS), jnp.int32), jnp.cumsum(c, axis=0)], axis=0)
    head = starts[:-1] & (MOE_ALIGN - 1)
    kmax = jnp.maximum(jnp.max((head + c + MOE_WIN - 1) // MOE_WIN, axis=1), 1).astype(jnp.int32)
    w = jnp.arange(MOE_WIN, dtype=jnp.int32)
    tgt = (w[None, None, :] + 1 - head[:, :, None]).astype(F32).reshape(nt, 1, N_EXPERTS * MOE_WIN)
    return starts.reshape(-1), kmax, tgt


def _expand_matrix():
    e = np.arange(N_EXPERTS * MOE_WIN) // MOE_WIN
    return jnp.asarray((np.arange(N_EXPERTS)[:, None] == e[None, :]).astype(np.float32), dtype=BF16)


def _slot_onehot(val_ref, eexp_ref):
    return lax.dot_general(val_ref[...].astype(BF16), eexp_ref[...], (((0,), (0,)), ((), ())),
                           preferred_element_type=F32)


def _dispatch_kernel(start_ref, kmax_ref, xn_ref, val_ref, tgt_ref, eexp_ref, xe_ref, stage, carry, sem):
    j = pl.program_id(0)
    nt = pl.num_programs(0)
    slot = lax.rem(j, 2)
    ne = N_EXPERTS

    cap = xe_ref.shape[1] - MOE_PAD

    @pl.when(j == 0)
    def _():
        carry[...] = jnp.zeros_like(carry)
        stage[0, 0:MOE_PAD, :] = jnp.zeros((MOE_PAD, D_MODEL), BF16)
        fills = [pltpu.make_async_copy(stage.at[0, pl.ds(0, MOE_PAD)], xe_ref.at[e, pl.ds(cap, MOE_PAD)], sem.at[0])
                 for e in range(ne)]
        for f in fills:
            f.start()
        for f in fills:
            f.wait()

    def window_copy(sl, e, row0):
        return pltpu.make_async_copy(stage.at[sl, pl.ds(e * MOE_WIN, MOE_WIN)],
                                     xe_ref.at[e, pl.ds(row0, MOE_WIN)], sem.at[sl])

    def wait_windows(sl):
        for e in range(ne):
            window_copy(sl, e, 0).wait()

    rep = _slot_onehot(val_ref, eexp_ref)
    xn = xn_ref[...]
    row = lax.broadcasted_iota(jnp.int32, (MOE_ALIGN, D_MODEL), 0)

    def block(k, _):
        @pl.when(k > 0)
        def _():
            wait_windows(slot)

        lo = k * MOE_WIN
        pt = (rep == tgt_ref[...] + lo.astype(F32)).astype(BF16)
        comp = lax.dot_general(pt, xn, (((0,), (0,)), ((), ())), preferred_element_type=F32)
        stage[slot] = comp.astype(BF16)
        for e in range(ne):
            s = start_ref[j * ne + e]
            head = s & (MOE_ALIGN - 1)
            r0 = e * MOE_WIN
            kept = carry[e * MOE_ALIGN:(e + 1) * MOE_ALIGN, :]
            owned = row < jnp.where(k == 0, head, 0)
            stage[slot, r0:r0 + MOE_ALIGN, :] = jnp.where(owned, kept, stage[slot, r0:r0 + MOE_ALIGN, :])
            nxt = (head + start_ref[(j + 1) * ne + e] - s) & (-MOE_ALIGN)
            here = (nxt >= lo) & (nxt < lo + MOE_WIN)
            off = pl.multiple_of(jnp.clip(nxt - lo, 0, MOE_WIN - MOE_ALIGN), MOE_ALIGN)
            cand = stage[slot, pl.ds(r0 + off, MOE_ALIGN), :]
            carry[e * MOE_ALIGN:(e + 1) * MOE_ALIGN, :] = jnp.where(here, cand, kept)

        @pl.when((k == 0) & (j > 0))
        def _():
            wait_windows(1 - slot)

        for e in range(ne):
            base = pl.multiple_of((start_ref[j * ne + e] & (-MOE_ALIGN)) + lo, MOE_ALIGN)
            window_copy(slot, e, base).start()
        return 0

    lax.fori_loop(0, kmax_ref[j], block, 0)

    @pl.when(j == nt - 1)
    def _():
        wait_windows(slot)


def _dispatch(xn, val, starts, kmax, tgt, eexp, cap, tile):
    n = xn.shape[0]
    nt = n // tile
    rows = N_EXPERTS * MOE_WIN
    return pl.pallas_call(
        _dispatch_kernel,
        grid_spec=pltpu.PrefetchScalarGridSpec(
            num_scalar_prefetch=2, grid=(nt,),
            in_specs=[pl.BlockSpec((tile, D_MODEL), lambda j, s, k: (j, 0)),
                      pl.BlockSpec((N_EXPERTS, tile), lambda j, s, k: (0, j)),
                      pl.BlockSpec((None, 1, rows), lambda j, s, k: (j, 0, 0)),
                      pl.BlockSpec((N_EXPERTS, rows), lambda j, s, k: (0, 0))],
            out_specs=pl.BlockSpec(memory_space=pl.ANY),
            scratch_shapes=[pltpu.VMEM((2, rows, D_MODEL), BF16),
                            pltpu.VMEM((N_EXPERTS * MOE_ALIGN, D_MODEL), BF16),
                            pltpu.SemaphoreType.DMA((2,))]),
        out_shape=jax.ShapeDtypeStruct((N_EXPERTS, cap + MOE_PAD, D_MODEL), BF16),
        compiler_params=_params(("arbitrary",)),
        name="moe_dispatch",
    )(starts, kmax, xn, val, tgt, eexp)


def _expert_kernel(x_ref, wr_ref, wg32_ref, wu32_ref, wd32_ref, y_ref, wg_ref, wu_ref, wd_ref, *, npad):
    e = pl.program_id(0)
    i = pl.program_id(1)

    @pl.when(i == npad)
    def _():
        wg_ref[...] = wg32_ref[...].astype(BF16)
        wu_ref[...] = wu32_ref[...].astype(BF16)
        wd_ref[...] = wd32_ref[...].astype(BF16)

    @pl.when(i >= npad)
    def _():
        x = x_ref[...]
        logits = (jnp.dot(x, wr_ref[0], preferred_element_type=F32)
                  + jnp.dot(x, wr_ref[1], preferred_element_type=F32))
        lane = lax.broadcasted_iota(jnp.int32, logits.shape, 1)
        logits = jnp.where(lane < N_EXPERTS, logits, -jnp.inf)
        ex = jnp.exp(logits - jnp.max(logits, axis=-1, keepdims=True))
        gate = (jnp.sum(jnp.where(lane == e, ex, 0.0), axis=-1, keepdims=True)
                / jnp.sum(ex, axis=-1, keepdims=True))
        hid = []
        for c0 in range(0, EXPERT_FF, FF_CHUNK):
            hg = jnp.dot(x, wg_ref[:, c0:c0 + FF_CHUNK], preferred_element_type=F32)
            hu = jnp.dot(x, wu_ref[:, c0:c0 + FF_CHUNK], preferred_element_type=F32)
            hid.append((_silu(hg) * hu).astype(BF16))
        hid = jnp.concatenate(hid, axis=1)
        y_ref[...] = (jnp.dot(hid, wd_ref[...], preferred_element_type=F32) * gate).astype(BF16)

    @pl.when(i < npad)
    def _():
        y_ref[...] = jnp.zeros_like(y_ref)


def _expert_ffn(xe, wr2, wg, wu, wd, layer, cap, tc):
    ne, rows, _ = xe.shape
    ntile = cap // tc
    npad = rows // tc - ntile
    wspec = lambda shape: pl.BlockSpec((None, None) + shape, lambda e, i: (layer, e, 0, 0))
    return pl.pallas_call(
        functools.partial(_expert_kernel, npad=npad),
        grid=(ne, rows // tc),
        in_specs=[pl.BlockSpec((None, tc, D_MODEL), lambda e, i: (e, jnp.maximum(i - npad, 0), 0)),
                  pl.BlockSpec(wr2.shape, lambda e, i: (0, 0, 0)),
                  wspec((D_MODEL, EXPERT_FF)), wspec((D_MODEL, EXPERT_FF)), wspec((EXPERT_FF, D_MODEL))],
        out_specs=pl.BlockSpec((None, tc, D_MODEL),
                               lambda e, i: (e, jnp.where(i < npad, ntile + i, i - npad), 0)),
        out_shape=jax.ShapeDtypeStruct((ne, rows, D_MODEL), BF16),
        scratch_shapes=[pltpu.VMEM((D_MODEL, EXPERT_FF), BF16), pltpu.VMEM((D_MODEL, EXPERT_FF), BF16),
                        pltpu.VMEM((EXPERT_FF, D_MODEL), BF16)],
        compiler_params=pltpu.CompilerParams(dimension_semantics=("parallel", "arbitrary"),
                                             vmem_limit_bytes=FFN_VMEM_LIMIT),
        name="expert_ffn",
    )(xe, wr2, wg, wu, wd)


COMBINE_SUB = 2


def _combine_kernel(start_ref, kmax_ref, h_ref, p_ref, val_ref, tgt_ref, eexp_ref, gain_ref, wg_ref, wp_ref,
                    ye_ref, out_ref, stage, sem, *, tile):
    j = pl.program_id(0)
    nstep = pl.num_programs(0)
    slot = lax.rem(j, 2)
    ne = N_EXPERTS
    subs = range(COMBINE_SUB)

    def window_copy(sl, sub, e, row0):
        buf = sl * COMBINE_SUB + sub
        return pltpu.make_async_copy(ye_ref.at[e, pl.ds(row0, MOE_WIN)],
                                     stage.at[buf, pl.ds(e * MOE_WIN, MOE_WIN)], sem.at[buf])

    def fetch(sl, sub, tile_idx, lo):
        for e in range(ne):
            base = pl.multiple_of((start_ref[tile_idx * ne + e] & (-MOE_ALIGN)) + lo, MOE_ALIGN)
            window_copy(sl, sub, e, base).start()

    def wait_windows(sl, sub):
        for e in range(ne):
            window_copy(sl, sub, e, 0).wait()

    @pl.when(j == 0)
    def _():
        for sub in subs:
            fetch(slot, sub, sub, 0)

    @pl.when(j + 1 < nstep)
    def _():
        for sub in subs:
            fetch(1 - slot, sub, (j + 1) * COMBINE_SUB + sub, 0)

    reps = [lax.dot_general(val_ref[:, sub * tile:(sub + 1) * tile].astype(BF16), eexp_ref[...],
                            (((0,), (0,)), ((), ())), preferred_element_type=F32) for sub in subs]
    pts = [(reps[sub] == tgt_ref[sub]).astype(BF16) for sub in subs]
    for sub in subs:
        wait_windows(slot, sub)
    accs = [h_ref[sub * tile:(sub + 1) * tile, :]
            + jnp.dot(pts[sub], stage[slot * COMBINE_SUB + sub], preferred_element_type=F32) for sub in subs]
    for sub in subs:
        t = j * COMBINE_SUB + sub

        def extra(k, acc, sub=sub, t=t):
            lo = k * MOE_WIN
            fetch(slot, sub, t, lo)
            wait_windows(slot, sub)
            pk = (reps[sub] == tgt_ref[sub] + lo.astype(F32)).astype(BF16)
            return acc + jnp.dot(pk, stage[slot * COMBINE_SUB + sub], preferred_element_type=F32)

        accs[sub] = lax.fori_loop(1, kmax_ref[t], extra, accs[sub])
    x = jnp.concatenate(accs, axis=0)
    ms = jnp.mean(x * x, axis=-1, keepdims=True)
    xn = (x * lax.rsqrt(ms + NORM_EPS) * gain_ref[...]).astype(BF16)
    gate = _sigmoid(jnp.dot(xn, wg_ref[...], preferred_element_type=F32))
    proj = jnp.dot(p_ref[...].astype(BF16), wp_ref[...], preferred_element_type=F32)
    out_ref[...] = x + gate * proj


def _combine_ple(h2, p3, layer, ye, val, starts, kmax, tgt, eexp, gain, wg, wp, tile):
    n = h2.shape[0]
    step = COMBINE_SUB * tile
    assert n % step == 0
    rows = N_EXPERTS * MOE_WIN
    fixed = lambda j, s, k: (0, 0)
    return pl.pallas_call(
        functools.partial(_combine_kernel, tile=tile),
        grid_spec=pltpu.PrefetchScalarGridSpec(
            num_scalar_prefetch=2, grid=(n // step,),
            in_specs=[pl.BlockSpec((step, D_MODEL), lambda j, s, k: (j, 0)),
                      pl.BlockSpec((None, step, PLE_DIM), lambda j, s, k: (layer, j, 0)),
                      pl.BlockSpec((N_EXPERTS, step), lambda j, s, k: (0, j)),
                      pl.BlockSpec((COMBINE_SUB, 1, rows), lambda j, s, k: (j, 0, 0)),
                      pl.BlockSpec((N_EXPERTS, rows), fixed),
                      pl.BlockSpec((1, D_MODEL), fixed), pl.BlockSpec(wg.shape, fixed),
                      pl.BlockSpec(wp.shape, fixed),
                      pl.BlockSpec(memory_space=pl.ANY)],
            out_specs=pl.BlockSpec((step, D_MODEL), lambda j, s, k: (j, 0)),
            scratch_shapes=[pltpu.VMEM((2 * COMBINE_SUB, rows, D_MODEL), BF16),
                            pltpu.SemaphoreType.DMA((2 * COMBINE_SUB,))]),
        out_shape=jax.ShapeDtypeStruct((n, D_MODEL), F32),
        compiler_params=_params(("arbitrary",)),
        name="moe_combine_ple",
    )(starts, kmax, h2, p3, val, tgt, eexp, gain, wg, wp, ye)


def _relayout_w_in(w_in):
    o_beta = ZA + ZB + DN_WIDTH
    o_alpha = o_beta + 2 * DN_HEADS
    o_glu = o_alpha + 2 * DN_HEADS
    pieces = [w_in[:, :o_beta], w_in[:, o_glu:o_glu + 2 * CONV_CH]]
    for d in range(2):
        pieces.append(w_in[:, o_beta + d * DN_HEADS:o_beta + (d + 1) * DN_HEADS])
        pieces.append(w_in[:, o_alpha + d * DN_HEADS:o_alpha + (d + 1) * DN_HEADS])
    pieces.append(jnp.zeros((w_in.shape[0], LANES - 4 * DN_HEADS), w_in.dtype))
    return jnp.concatenate(pieces, axis=1).astype(BF16)


def _prep_layer(lw):
    (norm_mix, w_in, q_gain, k_gain, sink, dn_conv, dn_a_log, dn_dt_bias, dn_out_gain,
     cv_dw, cv_dw_bias, cv_ln_gain, cv_ln_bias, w_out, norm_ffn, w_router, w_gate, w_up, w_down,
     norm_ple, w_ple_gate, w_ple_proj) = lw
    w_perm = _relayout_w_in(w_in)
    hgain = jnp.concatenate([jnp.tile(q_gain, ATT_HEADS) * (ATT_HEAD_DIM ** -0.5),
                             jnp.tile(k_gain, ATT_KV_HEADS)]).reshape(1, -1)
    zeros4 = jnp.zeros((DN_HEADS,), F32)
    aneg = -jnp.exp(dn_a_log.astype(F32))
    aneg_row = jnp.concatenate([zeros4, aneg[0], zeros4, aneg[1]])
    dtb_row = jnp.concatenate([zeros4, dn_dt_bias[0], zeros4, dn_dt_bias[1]])
    pad = lambda r: jnp.pad(r, (0, LANES - r.shape[0])).reshape(1, LANES)
    wr = jnp.pad(w_router.astype(F32), ((0, 0), (0, LANES - N_EXPERTS)))
    wr_hi = wr.astype(BF16)
    wr2 = jnp.stack([wr_hi, (wr - wr_hi.astype(F32)).astype(BF16)])
    return dict(
        w_router2=wr2,
        norm_mix=norm_mix.reshape(1, -1), w_in=w_perm, hgain=hgain, sink=sink.astype(F32),
        dn_conv=dn_conv, aneg=pad(aneg_row), dtb=pad(dtb_row),
        dn_out_gain=jnp.tile(dn_out_gain, DN_HEADS).reshape(1, -1),
        cv_dw=cv_dw, cv_dw_bias=cv_dw_bias.reshape(1, -1), cv_ln_gain=cv_ln_gain.reshape(1, -1),
        cv_ln_bias=cv_ln_bias.reshape(1, -1), w_out=w_out.astype(BF16),
        norm_ffn=norm_ffn.reshape(1, -1),
        norm_ple=norm_ple.reshape(1, -1), w_ple_gate=w_ple_gate.astype(BF16), w_ple_proj=w_ple_proj.astype(BF16))


def _tiles(bsz, seqlen):
    n = bsz * seqlen
    return dict(tm=min(1024, n), tl=min(512, seqlen), ch=min(512, seqlen), tcv=min(1024, seqlen))


def _moe_ple(h2, xn, aff_t, p3, layer, pw):
    n = h2.shape[0]
    cap = CAPACITY_FACTOR * n // N_EXPERTS
    tile = min(MOE_TILE, n)
    val, cnt = _select(aff_t, cap, tile)
    starts, kmax, tgt = _moe_plan(cnt, n // tile)
    eexp = _expand_matrix()
    xe = _dispatch(xn, val, starts, kmax, tgt, eexp, cap, tile)
    ye = _expert_ffn(xe, pw["w_router2"], pw["w_gate"], pw["w_up"], pw["w_down"], layer, cap,
                     min(MOE_FFN_TILE, cap))
    return _combine_ple(h2, p3, layer, ye, val, starts, kmax, tgt, eexp, pw["norm_ple"], pw["w_ple_gate"],
                        pw["w_ple_proj"], tile)


def _layer(h2, p3, layer, pw, bsz, seqlen):
    t = _tiles(bsz, seqlen)
    hm_att = _head_mean_matrix(ATT_Q + ATT_KV, ATT_HEAD_DIM)
    hs_dn = _head_sum_matrix(2 * DN_WIDTH, DN_HEAD_DIM)
    hm_dn = _head_mean_matrix(DN_WIDTH, DN_HEAD_DIM)
    za, zb, zg, glu_in, gates = _in_proj(h2, pw["norm_mix"], pw["w_in"], hm_att, pw["hgain"], t["tm"])
    o_a = _attention(za, pw["sink"], bsz, seqlen)
    y, gb = _dn_prep(zb, gates, pw["dn_conv"], hs_dn, pw["aneg"], pw["dtb"], bsz, seqlen, t["tl"])
    o_f, o_b = _dn_chunk(y, gb, bsz, seqlen, t["ch"])
    o_c = _conformer_conv(glu_in, pw["cv_dw"], pw["cv_dw_bias"], pw["cv_ln_gain"], pw["cv_ln_bias"],
                          bsz, seqlen, t["tcv"])
    h2, xn, aff_t = _out_proj_route(h2, o_a, o_f, o_b, zg, o_c, pw["dn_out_gain"], hm_dn, pw["w_out"],
                                    pw["norm_ffn"], pw["w_router2"], t["tm"])
    return _moe_ple(h2, xn, aff_t, p3, layer, pw)


def _trunk(x, p, layer_weights):
    bsz, seqlen, _ = x.shape
    h2 = x.reshape(bsz * seqlen, D_MODEL)
    p3 = p.reshape(p.shape[0], bsz * seqlen, PLE_DIM)
    for i, pw in enumerate(layer_weights):
        h2 = _layer(h2, p3, i, pw, bsz, seqlen)
    return h2.reshape(bsz, seqlen, D_MODEL)


def kernel(x_prompt, x_sample, p_prompt, p_sample, norm_mix, w_in, q_gain, k_gain, sink, dn_conv, dn_a_log,
           dn_dt_bias, dn_out_gain, cv_dw, cv_dw_bias, cv_ln_gain, cv_ln_bias, w_out, norm_ffn, w_router,
           w_gate, w_up, w_down, norm_ple, w_ple_gate, w_ple_proj):
    weights = (norm_mix, w_in, q_gain, k_gain, sink, dn_conv, dn_a_log, dn_dt_bias, dn_out_gain,
               cv_dw, cv_dw_bias, cv_ln_gain, cv_ln_bias, w_out, norm_ffn, w_router, w_gate, w_up, w_down,
               norm_ple, w_ple_gate, w_ple_proj)
    depth = w_in.shape[0]
    experts = dict(w_gate=w_gate, w_up=w_up, w_down=w_down)
    layer_weights = [dict(_prep_layer([w[i] for w in weights]), **experts) for i in range(depth)]
    return (_trunk(x_prompt, p_prompt, layer_weights), _trunk(x_sample, p_sample, layer_weights))
```

```python
import functools
import math

import numpy as np
import jax
import jax.numpy as jnp
from jax import lax
from jax.experimental import pallas as pl
from jax.experimental.pallas import tpu as pltpu

F32 = jnp.float32
BF16 = jnp.bfloat16

D_MODEL = 1024
ATT_HEADS = 8
ATT_KV_HEADS = 2
ATT_HEAD_DIM = 64
ATT_GROUP = ATT_HEADS // ATT_KV_HEADS
WINDOW = 128
ATT_BLOCK = 128
DN_HEADS = 4
DN_HEAD_DIM = 64
DN_WIDTH = DN_HEADS * DN_HEAD_DIM
DN_CHUNK = 64
CONV_CH = 256
CONV_WIDTH = 31
ATT_Q = ATT_HEADS * ATT_HEAD_DIM
ATT_KV = ATT_KV_HEADS * ATT_HEAD_DIM
N_EXPERTS = 16
CAPACITY_FACTOR = 2
EXPERT_FF = 1024
PLE_DIM = 256
NORM_EPS = 1e-6

LANES = 128
SUBLANES = 8
VMEM_LIMIT = 48 * 1024 * 1024
FFN_VMEM_LIMIT = 56 * 1024 * 1024

ZA = ATT_Q + 2 * ATT_KV
ZB = 3 * DN_WIDTH
ZW = ZA + ZB + DN_WIDTH + 2 * CONV_CH + LANES


def _params(sem):
    return pltpu.CompilerParams(dimension_semantics=sem, vmem_limit_bytes=VMEM_LIMIT)


def _head_mean_matrix(width, head):
    idx = np.arange(width) // head
    return jnp.asarray((idx[:, None] == idx[None, :]).astype(np.float32) / head, dtype=BF16)


def _head_sum_matrix(width, head):
    idx = np.arange(width) // head
    return jnp.asarray((idx[:, None] == idx[None, :]).astype(np.float32), dtype=BF16)


def _sigmoid(x):
    return 1.0 / (1.0 + jnp.exp(-x))


def _silu(x):
    return x * _sigmoid(x)


def _in_proj_kernel(x_ref, gain_ref, w_ref, hm_ref, hgain_ref, za_ref, zb_ref, zg_ref, glu_ref, gates_ref):
    x = x_ref[...]
    ms = jnp.mean(x * x, axis=-1, keepdims=True)
    a = (x * lax.rsqrt(ms + NORM_EPS) * gain_ref[...]).astype(BF16)
    z = jnp.dot(a, w_ref[...], preferred_element_type=F32)
    nqk = ATT_Q + ATT_KV
    qk = z[:, :nqk]
    hms = jnp.dot((qk * qk).astype(BF16), hm_ref[...], preferred_element_type=F32)
    za_ref[:, :nqk] = (qk * lax.rsqrt(hms + NORM_EPS) * hgain_ref[...]).astype(BF16)
    za_ref[:, nqk:] = z[:, nqk:ZA].astype(BF16)
    zb_ref[...] = z[:, ZA:ZA + ZB]
    zg_ref[...] = z[:, ZA + ZB:ZA + ZB + DN_WIDTH]
    glu_ref[...] = z[:, ZA + ZB + DN_WIDTH:ZA + ZB + DN_WIDTH + 2 * CONV_CH]
    gates_ref[...] = z[:, ZW - LANES:]


def _in_proj(h2, gain, w_perm, hm, hgain, tm):
    n = h2.shape[0]
    row = lambda i: (i, 0)
    fixed = lambda i: (0, 0)
    return pl.pallas_call(
        _in_proj_kernel,
        grid=(n // tm,),
        in_specs=[pl.BlockSpec((tm, D_MODEL), row), pl.BlockSpec((1, D_MODEL), fixed),
                  pl.BlockSpec((D_MODEL, ZW), fixed), pl.BlockSpec(hm.shape, fixed),
                  pl.BlockSpec(hgain.shape, fixed)],
        out_specs=[pl.BlockSpec((tm, ZA), row), pl.BlockSpec((tm, ZB), row), pl.BlockSpec((tm, DN_WIDTH), row),
                   pl.BlockSpec((tm, 2 * CONV_CH), row), pl.BlockSpec((tm, LANES), row)],
        out_shape=[jax.ShapeDtypeStruct((n, ZA), BF16), jax.ShapeDtypeStruct((n, ZB), F32),
                   jax.ShapeDtypeStruct((n, DN_WIDTH), F32), jax.ShapeDtypeStruct((n, 2 * CONV_CH), F32),
                   jax.ShapeDtypeStruct((n, LANES), F32)],
        compiler_params=_params(("parallel",)),
        name="in_proj",
    )(h2, gain, w_perm, hm, hgain)


ATT_MASKED = -1e30


def _attn_bias_table():
    i = np.arange(ATT_BLOCK)[:, None]
    c = np.arange(3 * ATT_BLOCK)[None, :]
    rel = c - ATT_BLOCK - i
    slopes = 2.0 ** (-8.0 * np.arange(1, ATT_HEADS + 1) / ATT_HEADS)
    table = np.empty((3, ATT_KV_HEADS, ATT_GROUP * ATT_BLOCK, 3 * ATT_BLOCK), np.float32)
    for variant in range(3):
        ok = np.abs(rel) <= WINDOW
        if variant == 0:
            ok = ok & (c >= ATT_BLOCK)
        if variant == 2:
            ok = ok & (c < 2 * ATT_BLOCK)
        for hd in range(ATT_HEADS):
            g, j = divmod(hd, ATT_GROUP)
            table[variant, g, j * ATT_BLOCK:(j + 1) * ATT_BLOCK] = np.where(ok, -slopes[hd] * np.abs(rel), ATT_MASKED)
    return jnp.asarray(table)


def _attn_kernel(sink_ref, q_ref, kvp_ref, kvo_ref, kvn_ref, bias_a_ref, bias_b_ref, o_ref):
    kv = jnp.concatenate([kvp_ref[...], kvo_ref[...], kvn_ref[...]], axis=0)
    hd_ = ATT_HEAD_DIM
    groups = range(ATT_KV_HEADS)
    heads = range(ATT_HEADS)
    rows = lambda t, hd: t[(hd % ATT_GROUP) * ATT_BLOCK:(hd % ATT_GROUP + 1) * ATT_BLOCK]
    work = []
    for blk, bias_ref in enumerate((bias_a_ref, bias_b_ref)):
        keys = kv[blk * ATT_BLOCK:(blk + 3) * ATT_BLOCK]
        ks = [keys[:, g * hd_:(g + 1) * hd_] for g in groups]
        vs = [keys[:, ATT_KV + g * hd_:ATT_KV + (g + 1) * hd_] for g in groups]
        q = q_ref[blk * ATT_BLOCK:(blk + 1) * ATT_BLOCK, :]
        qs = [jnp.concatenate([q[:, (g * ATT_GROUP + j) * hd_:(g * ATT_GROUP + j + 1) * hd_]
                               for j in range(ATT_GROUP)], axis=0) for g in groups]
        sg = [lax.dot_general(qs[g], ks[g], (((1,), (1,)), ((), ())), preferred_element_type=F32) + bias_ref[g]
              for g in groups]
        work.append((vs, [rows(sg[hd // ATT_GROUP], hd) for hd in heads]))
    m = [[jnp.maximum(jnp.max(s[hd], axis=-1, keepdims=True), sink_ref[hd]) for hd in heads] for _, s in work]
    e = [[jnp.exp(s[hd] - mb[hd]) for hd in heads] for (_, s), mb in zip(work, m)]
    den = [[jnp.sum(eb[hd], axis=-1, keepdims=True) + jnp.exp(sink_ref[hd] - mb[hd]) for hd in heads]
           for eb, mb in zip(e, m)]
    for blk, ((vs, _), eb, db) in enumerate(zip(work, e, den)):
        eg = [jnp.concatenate([eb[g * ATT_GROUP + j].astype(BF16) for j in range(ATT_GROUP)], axis=0)
              for g in groups]
        og = [jnp.dot(eg[g], vs[g], preferred_element_type=F32) for g in groups]
        for hd in heads:
            o_ref[blk * ATT_BLOCK:(blk + 1) * ATT_BLOCK, hd * hd_:(hd + 1) * hd_] = (
                rows(og[hd // ATT_GROUP], hd) / db[hd]).astype(BF16)


def _attention(za, sink, bsz, seqlen):
    nb = seqlen // ATT_BLOCK
    assert nb >= 2 and nb % 2 == 0
    npair = nb // 2
    za3 = za.reshape(bsz, seqlen, ZA)
    kvw = 2 * ATT_KV
    kvc = ATT_Q // kvw
    bias = _attn_bias_table()
    bias_spec = lambda pick: pl.BlockSpec((None,) + bias.shape[1:], lambda b, n: (pick(n), 0, 0, 0))
    return pl.pallas_call(
        _attn_kernel,
        grid=(bsz, npair),
        in_specs=[pl.BlockSpec(memory_space=pltpu.SMEM),
                  pl.BlockSpec((None, 2 * ATT_BLOCK, ATT_Q), lambda b, n: (b, n, 0)),
                  pl.BlockSpec((None, ATT_BLOCK, kvw), lambda b, n: (b, jnp.maximum(2 * n - 1, 0), kvc)),
                  pl.BlockSpec((None, 2 * ATT_BLOCK, kvw), lambda b, n: (b, n, kvc)),
                  pl.BlockSpec((None, ATT_BLOCK, kvw), lambda b, n: (b, jnp.minimum(2 * n + 2, nb - 1), kvc)),
                  bias_spec(lambda n: jnp.where(n == 0, 0, 1)),
                  bias_spec(lambda n: jnp.where(n == npair - 1, 2, 1))],
        out_specs=pl.BlockSpec((None, 2 * ATT_BLOCK, ATT_Q), lambda b, n: (b, n, 0)),
        out_shape=jax.ShapeDtypeStruct((bsz, seqlen, ATT_Q), BF16),
        compiler_params=_params(("parallel", "parallel")),
        name="window_attention",
    )(sink, za3, za3, za3, za3, bias, bias).reshape(bsz * seqlen, ATT_Q)


DN_HALO = SUBLANES


def _dn_prep_kernel(x_ref, xp_ref, xn_ref, cw_ref, hs_ref, g_ref, aneg_ref, dtb_ref, mf_ref, mb_ref,
                    y_ref, gb_ref, buf_ref, *, tl):
    i = pl.program_id(1)
    nt = pl.num_programs(1)
    buf_ref[0:DN_HALO, :] = jnp.where(i > 0, xp_ref[...], 0.0)
    buf_ref[DN_HALO:DN_HALO + tl, :] = x_ref[...]
    buf_ref[DN_HALO + tl:, :] = jnp.where(i < nt - 1, xn_ref[...], 0.0)
    y = (cw_ref[0:1, :] * buf_ref[DN_HALO - 1:DN_HALO - 1 + tl, :]
         + cw_ref[1:2, :] * buf_ref[DN_HALO:DN_HALO + tl, :]
         + cw_ref[2:3, :] * buf_ref[DN_HALO + 1:DN_HALO + 1 + tl, :])
    y = _silu(y)
    qk = y[:, :2 * DN_WIDTH]
    ss = jnp.dot((qk * qk).astype(BF16), hs_ref[...], preferred_element_type=F32)
    lane = lax.broadcasted_iota(jnp.int32, (tl, 2 * DN_WIDTH), 1)
    scale = jnp.where(lane < DN_WIDTH, DN_HEAD_DIM ** -0.5, 1.0)
    y_ref[:, :2 * DN_WIDTH] = qk * lax.rsqrt(ss + NORM_EPS) * scale
    y_ref[:, 2 * DN_WIDTH:] = y[:, 2 * DN_WIDTH:]
    raw = g_ref[...]
    col = lax.broadcasted_iota(jnp.int32, (tl, LANES), 1)
    is_beta = (col & DN_HEADS) == 0
    t = raw + dtb_ref[...]
    softplus = jnp.maximum(t, 0.0) + jnp.log(1.0 + jnp.exp(-jnp.abs(t)))
    vals = jnp.where(is_beta, _sigmoid(raw), aneg_ref[...] * softplus)
    v_hi = vals.astype(BF16)
    r1 = vals - v_hi.astype(F32)
    v_mid = r1.astype(BF16)
    v_lo = (r1 - v_mid.astype(F32)).astype(BF16)
    terms = jnp.concatenate([v_hi, v_mid, v_lo], axis=1)
    cf3 = jnp.dot(mf_ref[...], terms, preferred_element_type=F32)
    cb3 = jnp.dot(mb_ref[...], terms, preferred_element_type=F32)
    cf = cf3[:, :LANES] + (cf3[:, LANES:2 * LANES] + cf3[:, 2 * LANES:])
    cb = cb3[:, :LANES] + (cb3[:, LANES:2 * LANES] + cb3[:, 2 * LANES:])
    gb_ref[0] = jnp.where(is_beta, vals, cf)
    gb_ref[1] = pltpu.roll(jnp.where(is_beta, vals, cb), LANES - 2 * DN_HEADS, axis=1)


def _dn_prep(zb, gates, conv_w, hs, aneg, dtb, bsz, seqlen, tl):
    zb3 = zb.reshape(bsz, seqlen, ZB)
    g3 = gates.reshape(bsz, seqlen, LANES)
    nt = seqlen // tl
    hb = tl // DN_HALO
    ch = np.arange(tl) // DN_CHUNK
    same = ch[:, None] == ch[None, :]
    pos = np.arange(tl)
    mf = jnp.asarray((same & (pos[None, :] <= pos[:, None])).astype(np.float32), dtype=BF16)
    mb = jnp.asarray((same & (pos[None, :] >= pos[:, None])).astype(np.float32), dtype=BF16)
    fixed = lambda b, i: (0, 0)
    y, gb = pl.pallas_call(
        functools.partial(_dn_prep_kernel, tl=tl),
        grid=(bsz, nt),
        in_specs=[pl.BlockSpec((None, tl, ZB), lambda b, i: (b, i, 0)),
                  pl.BlockSpec((None, DN_HALO, ZB), lambda b, i: (b, jnp.maximum(i * hb - 1, 0), 0)),
                  pl.BlockSpec((None, DN_HALO, ZB), lambda b, i: (b, jnp.minimum((i + 1) * hb, nt * hb - 1), 0)),
                  pl.BlockSpec(conv_w.shape, fixed), pl.BlockSpec(hs.shape, fixed),
                  pl.BlockSpec((None, tl, LANES), lambda b, i: (b, i, 0)),
                  pl.BlockSpec((1, LANES), fixed), pl.BlockSpec((1, LANES), fixed),
                  pl.BlockSpec((tl, tl), fixed), pl.BlockSpec((tl, tl), fixed)],
        out_specs=[pl.BlockSpec((None, tl, ZB), lambda b, i: (b, i, 0)),
                   pl.BlockSpec((2, None, tl, LANES), lambda b, i: (0, b, i, 0))],
        out_shape=[jax.ShapeDtypeStruct((bsz, seqlen, ZB), F32),
                   jax.ShapeDtypeStruct((2, bsz, seqlen, LANES), F32)],
        scratch_shapes=[pltpu.VMEM((tl + 2 * DN_HALO, ZB), F32)],
        compiler_params=_params(("parallel", "parallel")),
        name="deltanet_prep",
    )(zb3, zb3, zb3, conv_w, hs, g3, aneg, dtb, mf, mb)
    return y, gb


def _lane_expand(cols, first):
    c = cols.shape[0]
    lane = lax.broadcasted_iota(jnp.int32, (c, LANES), 1)
    halves = []
    for h in range(0, DN_HEADS, 2):
        a = jnp.broadcast_to(cols[:, first + h:first + h + 1], (c, LANES))
        b = jnp.broadcast_to(cols[:, first + h + 1:first + h + 2], (c, LANES))
        halves.append(jnp.where(lane < DN_HEAD_DIM, a, b))
    return jnp.concatenate(halves, axis=1)


def _dn_pair_kernel(xf_ref, xb_ref, gf_ref, gb_ref, of_ref, ob_ref, sf_ref, sb_ref, *, nsub):
    c = DN_CHUNK
    w = DN_WIDTH

    @pl.when(pl.program_id(1) == 0)
    def _():
        sf_ref[...] = jnp.zeros_like(sf_ref)
        sb_ref[...] = jnp.zeros_like(sb_ref)

    r_cat = lax.broadcasted_iota(jnp.int32, (c, w), 0)
    s_cat = lax.broadcasted_iota(jnp.int32, (c, w), 1) & (DN_HEAD_DIM - 1)
    eye_cat = s_cat == r_cat
    rr = lax.broadcasted_iota(jnp.int32, (w, w), 0)
    cc = lax.broadcasted_iota(jnp.int32, (w, w), 1)
    head = (rr >> 6) == (cc >> 6)
    head_b = head.astype(BF16)
    m16 = (s_cat >> 4) == (r_cat >> 4)
    m32 = (s_cat >> 5) == (r_cat >> 5)
    off16 = m32 & jnp.logical_not(m16)
    off32 = jnp.logical_not(m32)
    eye_f = eye_cat.astype(F32)

    def bd(t):
        return jnp.concatenate([t] * DN_HEADS, axis=0) * head_b

    def mm(a, b):
        return jnp.dot(a, b, preferred_element_type=F32)

    chunks = [(0, i * c) for i in range(nsub)] + [(1, (nsub - 1 - i) * c) for i in range(nsub)]
    xrefs = (xf_ref, xb_ref)
    grefs = (gf_ref, gb_ref)
    orefs = (of_ref, ob_ref)
    srefs = (sf_ref, sb_ref)
    incl = (s_cat <= r_cat, s_cat >= r_cat)
    strict = (s_cat < r_cat, s_cat > r_cat)
    last_row = (c - 1, 0)

    pre = []
    for d, st in chunks:
        x = xrefs[d][st:st + c, :]
        q, k, v = x[:, :w], x[:, w:2 * w], x[:, 2 * w:]
        gbt = grefs[d][st:st + c, :]
        beta = _lane_expand(gbt, 0)
        gc = _lane_expand(gbt, DN_HEADS)
        grow = jnp.sum(jnp.where(eye_cat, gc, 0.0), axis=0, keepdims=True)
        decay = jnp.exp(jnp.where(incl[d], gc - grow, -jnp.inf))
        glast = gc[last_row[d]:last_row[d] + 1, :]
        egc = jnp.exp(gc)
        kb = k * beta
        pre.append(dict(d=d, st=st, q=q, k=k, kb=kb, vb=v * beta, decay=decay, glast=glast, egc=egc,
                        kdec=(k * jnp.exp(glast - gc)).astype(BF16)))

    kks = [lax.dot_general(jnp.concatenate([p["kb"], p["q"]], axis=0).astype(BF16), bd(p["k"].astype(BF16)),
                           (((1,), (1,)), ((), ())), preferred_element_type=F32) for p in pre]
    a = [jnp.where(strict[p["d"]], kk[:c] * p["decay"], 0.0) for p, kk in zip(pre, kks)]
    intra = [jnp.where(incl[p["d"]], kk[c:] * p["decay"], 0.0).astype(BF16) for p, kk in zip(pre, kks)]
    xm = [jnp.where(m16, -t, 0.0) for t in a]
    xm_b = [t.astype(BF16) for t in xm]
    x2_b = [mm(t, bd(t)).astype(BF16) for t in xm_b]
    x2_d = [bd(t) for t in x2_b]
    dinv = [eye_f + t for t in xm]
    r2 = [mm(jnp.concatenate([t.astype(BF16), p2], axis=0), s2) for t, p2, s2 in zip(dinv, x2_b, x2_d)]
    dinv = [t + r[:c] for t, r in zip(dinv, r2)]
    x4_b = [r[c:].astype(BF16) for r in r2]
    x4_d = [bd(t) for t in x4_b]
    r4 = [mm(jnp.concatenate([t.astype(BF16), p4], axis=0), s4) for t, p4, s4 in zip(dinv, x4_b, x4_d)]
    dinv = [t + r[:c] for t, r in zip(dinv, r4)]
    x8_d = [bd(r[c:].astype(BF16)) for r in r4]
    dinv = [t + mm(t.astype(BF16), s8) for t, s8 in zip(dinv, x8_d)]
    dinv_b = [t.astype(BF16) for t in dinv]
    n32 = [bd(mm(jnp.where(off16, t, 0.0).astype(BF16), bd(db)).astype(BF16)) for t, db in zip(a, dinv_b)]
    t32 = [t - mm(db, n) for t, db, n in zip(dinv, dinv_b, n32)]
    t32_b = [t.astype(BF16) for t in t32]
    n64 = [bd(mm(jnp.where(off32, t, 0.0).astype(BF16), bd(tb)).astype(BF16)) for t, tb in zip(a, t32_b)]
    t_cat = [(t - mm(tb, n)).astype(BF16) for t, tb, n in zip(t32, t32_b, n64)]
    uw = [mm(tc, jnp.concatenate([bd(p["vb"].astype(BF16)), bd((p["kb"] * p["egc"]).astype(BF16))], axis=1))
          for tc, p in zip(t_cat, pre)]
    uw_b = [t.astype(BF16) for t in uw]
    pn = [lax.dot_general(p["kdec"], t, (((0,), (0,)), ((), ())), preferred_element_type=F32)
          for p, t in zip(pre, uw_b)]
    qo = [mm(it, jnp.concatenate([bd(t[:, :w]), bd(t[:, w:])], axis=1)) for it, t in zip(intra, uw_b)]
    lhs = [jnp.concatenate([n[:, w:].astype(BF16) * head_b, (p["q"] * p["egc"] - o[:, w:]).astype(BF16)], axis=0)
           for n, o, p in zip(pn, qo, pre)]
    for step in range(nsub):
        for d in range(2):
            i = d * nsub + step
            p = pre[i]
            state = srefs[d][...]
            r = jnp.dot(lhs[i], state.astype(BF16), preferred_element_type=F32)
            orefs[d][p["st"]:p["st"] + c, :] = r[w:] + qo[i][:, :w]
            srefs[d][...] = state * jnp.exp(p["glast"]) - r[:w] + jnp.where(head, pn[i][:, :w], 0.0)


def _dn_chunk(y, gb, bsz, seqlen, ch):
    nsub = ch // DN_CHUNK
    nblk = seqlen // ch
    fwd = lambda b, j: (b, j, 0)
    bwd = lambda b, j: (b, nblk - 1 - j, 0)
    o_f, o_b = pl.pallas_call(
        functools.partial(_dn_pair_kernel, nsub=nsub),
        grid=(bsz, nblk),
        in_specs=[pl.BlockSpec((None, ch, ZB), fwd), pl.BlockSpec((None, ch, ZB), bwd),
                  pl.BlockSpec((None, None, ch, LANES), lambda b, j: (0, b, j, 0)),
                  pl.BlockSpec((None, None, ch, LANES), lambda b, j: (1, b, nblk - 1 - j, 0))],
        out_specs=[pl.BlockSpec((None, ch, DN_WIDTH), fwd), pl.BlockSpec((None, ch, DN_WIDTH), bwd)],
        out_shape=[jax.ShapeDtypeStruct((bsz, seqlen, DN_WIDTH), F32)] * 2,
        scratch_shapes=[pltpu.VMEM((DN_WIDTH, DN_WIDTH), F32)] * 2,
        compiler_params=_params(("parallel", "arbitrary")),
        name="deltanet_chunks",
    )(y, y, gb, gb)
    return o_f.reshape(bsz * seqlen, DN_WIDTH), o_b.reshape(bsz * seqlen, DN_WIDTH)


CV_HALO = 2 * SUBLANES
CV_PAD = (CONV_WIDTH - 1) // 2


def _conv_kernel(x_ref, xp_ref, xn_ref, dw_ref, bias_ref, lng_ref, lnb_ref, o_ref, buf_ref, shift_ref, *, tl):
    i = pl.program_id(1)
    nt = pl.num_programs(1)

    def glu(t):
        return t[:, :CONV_CH] * _sigmoid(t[:, CONV_CH:])

    buf_ref[0:CV_HALO, :] = jnp.where(i > 0, glu(xp_ref[...]), 0.0)
    buf_ref[CV_HALO:CV_HALO + tl, :] = glu(x_ref[...])
    buf_ref[CV_HALO + tl:, :] = jnp.where(i < nt - 1, glu(xn_ref[...]), 0.0)
    acc = jnp.zeros((tl, CONV_CH), F32) + bias_ref[...]
    first = CV_HALO - CV_PAD
    span = -(-(first + CONV_WIDTH) // SUBLANES) * SUBLANES - SUBLANES
    for sub in range(SUBLANES):
        shift_ref[...] = buf_ref[sub:sub + tl + span, :]
        for base in range(0, span + 1, SUBLANES):
            j = base + sub - first
            if 0 <= j < CONV_WIDTH:
                acc = acc + dw_ref[j:j + 1, :] * shift_ref[base:base + tl, :]
    mu = jnp.mean(acc, axis=-1, keepdims=True)
    cen = acc - mu
    var = jnp.mean(cen * cen, axis=-1, keepdims=True)
    o_ref[...] = _silu(cen * lax.rsqrt(var + NORM_EPS) * lng_ref[...] + lnb_ref[...]).astype(BF16)


def _conformer_conv(glu_in, dw, bias, lng, lnb, bsz, seqlen, tl):
    x3 = glu_in.reshape(bsz, seqlen, 2 * CONV_CH)
    nt = seqlen // tl
    hb = tl // CV_HALO
    fixed = lambda b, i: (0, 0)
    return pl.pallas_call(
        functools.partial(_conv_kernel, tl=tl),
        grid=(bsz, nt),
        in_specs=[pl.BlockSpec((None, tl, 2 * CONV_CH), lambda b, i: (b, i, 0)),
                  pl.BlockSpec((None, CV_HALO, 2 * CONV_CH), lambda b, i: (b, jnp.maximum(i * hb - 1, 0), 0)),
                  pl.BlockSpec((None, CV_HALO, 2 * CONV_CH),
                               lambda b, i: (b, jnp.minimum((i + 1) * hb, nt * hb - 1), 0)),
                  pl.BlockSpec(dw.shape, fixed), pl.BlockSpec((1, CONV_CH), fixed),
                  pl.BlockSpec((1, CONV_CH), fixed), pl.BlockSpec((1, CONV_CH), fixed)],
        out_specs=pl.BlockSpec((None, tl, CONV_CH), lambda b, i: (b, i, 0)),
        out_shape=jax.ShapeDtypeStruct((bsz, seqlen, CONV_CH), BF16),
        scratch_shapes=[pltpu.VMEM((tl + 2 * CV_HALO, CONV_CH), F32),
                        pltpu.VMEM((tl + 2 * CV_HALO - SUBLANES, CONV_CH), F32)],
        compiler_params=_params(("parallel", "parallel")),
        name="conformer_conv",
    )(x3, x3, x3, dw, bias, lng, lnb).reshape(bsz * seqlen, CONV_CH)


def _out_proj_kernel(h_ref, oa_ref, of_ref, ob_ref, zg_ref, oc_ref, og_ref, hm_ref, w_ref, gain_ref, wr_ref,
                     out_ref, xn_ref, aff_ref):
    half = h_ref.shape[0] // 2
    for r0 in (0, half):
        rs = slice(r0, r0 + half)
        ob = of_ref[rs, :] + ob_ref[rs, :]
        ms = jnp.dot((ob * ob).astype(BF16), hm_ref[...], preferred_element_type=F32)
        obn = ob * lax.rsqrt(ms + NORM_EPS) * og_ref[...]
        ob2 = obn * _silu(zg_ref[rs, :])
        mix = jnp.concatenate([oa_ref[rs, :], ob2.astype(BF16), oc_ref[rs, :]], axis=1)
        x = h_ref[rs, :] + jnp.dot(mix, w_ref[...], preferred_element_type=F32)
        out_ref[rs, :] = x
        ms = jnp.mean(x * x, axis=-1, keepdims=True)
        xn = x * lax.rsqrt(ms + NORM_EPS) * gain_ref[...]
        xn_hi = xn.astype(BF16)
        xn_ref[rs, :] = xn_hi
        xn_lo = (xn - xn_hi.astype(F32)).astype(BF16)
        logits = (jnp.dot(xn_hi, wr_ref[0], preferred_element_type=F32)
                  + (jnp.dot(xn_lo, wr_ref[0], preferred_element_type=F32)
                     + jnp.dot(xn_hi, wr_ref[1], preferred_element_type=F32)))
        lane = lax.broadcasted_iota(jnp.int32, logits.shape, 1)
        logits = jnp.where(lane < N_EXPERTS, logits, -jnp.inf)
        m = jnp.max(logits, axis=-1, keepdims=True)
        e = jnp.exp(logits - m)
        aff = e / jnp.sum(e, axis=-1, keepdims=True)
        aff_ref[:, rs] = jnp.transpose(aff)[:N_EXPERTS, :]


def _out_proj_route(h2, oa, o_f, o_b, zg, oc, og, hm, w, gain, wr2, tm):
    n = h2.shape[0]
    row = lambda i: (i, 0)
    fixed = lambda i: (0, 0)
    return pl.pallas_call(
        _out_proj_kernel,
        grid=(n // tm,),
        in_specs=[pl.BlockSpec((tm, D_MODEL), row), pl.BlockSpec((tm, ATT_Q), row),
                  pl.BlockSpec((tm, DN_WIDTH), row), pl.BlockSpec((tm, DN_WIDTH), row),
                  pl.BlockSpec((tm, DN_WIDTH), row),
                  pl.BlockSpec((tm, CONV_CH), row), pl.BlockSpec((1, DN_WIDTH), fixed),
                  pl.BlockSpec(hm.shape, fixed), pl.BlockSpec(w.shape, fixed),
                  pl.BlockSpec((1, D_MODEL), fixed), pl.BlockSpec((2, D_MODEL, LANES), lambda i: (0, 0, 0))],
        out_specs=[pl.BlockSpec((tm, D_MODEL), row), pl.BlockSpec((tm, D_MODEL), row),
                   pl.BlockSpec((N_EXPERTS, tm), lambda i: (0, i))],
        out_shape=[jax.ShapeDtypeStruct((n, D_MODEL), F32), jax.ShapeDtypeStruct((n, D_MODEL), BF16),
                   jax.ShapeDtypeStruct((N_EXPERTS, n), F32)],
        compiler_params=_params(("parallel",)),
        name="out_proj_route",
    )(h2, oa, o_f, o_b, zg, oc, og, hm, w, gain, wr2)


MOE_TILE = 256
MOE_ALIGN = 2 * SUBLANES
MOE_WIN = 64
MOE_PAD = 1024
MOE_FFN_TILE = 1024
FF_CHUNK = 256
MOE_UNSELECTED = -64.0


def _select_kernel(aff_ref, tri_ref, val_ref, cnt_ref, *, cap, tile):
    ne, n = aff_ref.shape
    nt = n // tile
    capf = float(cap)

    def bits_of(x):
        return lax.bitcast_convert_type(x, jnp.int32)

    def search(i, thr):
        cand = thr | jnp.left_shift(jnp.int32(1), 30 - i)
        cnt = jnp.sum((bits_of(aff_ref[...]) >= cand).astype(F32), axis=1, keepdims=True)
        return jnp.where(cnt >= capf, cand, thr)

    thr = lax.fori_loop(0, 31, search, jnp.zeros((ne, 1), jnp.int32))
    n_gt = jnp.sum((bits_of(aff_ref[...]) > thr).astype(F32), axis=1, keepdims=True)
    need = capf - n_gt
    lane = lax.broadcasted_iota(jnp.int32, (ne, LANES), 1)

    def tile_body(j, carry):
        eq_before, cnt_acc = carry
        off = pl.multiple_of(j * tile, tile)
        b = bits_of(aff_ref[:, pl.ds(off, tile)])
        gt = b > thr
        eqf = (b == thr).astype(F32)
        eq_rank = eq_before + jnp.dot(eqf.astype(BF16), tri_ref[...], preferred_element_type=F32)
        self_ = jnp.where(gt, 1.0, jnp.where(eq_rank <= need, eqf, 0.0))
        rank = jnp.dot(self_.astype(BF16), tri_ref[...], preferred_element_type=F32)
        val_ref[:, pl.ds(off, tile)] = jnp.where(self_ > 0.0, rank, MOE_UNSELECTED)
        cnt = jnp.sum(self_, axis=1, keepdims=True)
        return (eq_before + jnp.sum(eqf, axis=1, keepdims=True), cnt_acc + jnp.where(lane == j, cnt, 0.0))

    init = (jnp.zeros((ne, 1), F32), jnp.zeros((ne, LANES), F32))
    _, cnt_acc = lax.fori_loop(0, nt, tile_body, init, unroll=math.gcd(nt, 4))
    cnt_ref[...] = cnt_acc


def _select(aff_t, cap, tile):
    ne, n = aff_t.shape
    assert n // tile <= LANES
    tri = jnp.asarray(np.triu(np.ones((tile, tile), np.float32)), dtype=BF16)
    return pl.pallas_call(
        functools.partial(_select_kernel, cap=cap, tile=tile),
        out_shape=[jax.ShapeDtypeStruct((ne, n), F32), jax.ShapeDtypeStruct((ne, LANES), F32)],
        compiler_params=pltpu.CompilerParams(vmem_limit_bytes=VMEM_LIMIT),
        name="moe_select",
    )(aff_t, tri)


def _moe_plan(cnt, nt):
    c = cnt[:, :nt].astype(jnp.int32).T
    starts = jnp.concatenate([jnp.zeros((1, N_EXPERTS), jnp.int32), jnp.cumsum(c, axis=0)], axis=0)
    head = starts[:-1] & (MOE_ALIGN - 1)
    kmax = jnp.maximum(jnp.max((head + c + MOE_WIN - 1) // MOE_WIN, axis=1), 1).astype(jnp.int32)
    w = jnp.arange(MOE_WIN, dtype=jnp.int32)
    tgt = (w[None, None, :] + 1 - head[:, :, None]).astype(F32).reshape(nt, 1, N_EXPERTS * MOE_WIN)
    return starts.reshape(-1), kmax, tgt


def _expand_matrix():
    e = np.arange(N_EXPERTS * MOE_WIN) // MOE_WIN
    return jnp.asarray((np.arange(N_EXPERTS)[:, None] == e[None, :]).astype(np.float32), dtype=BF16)


def _slot_onehot(val_ref, eexp_ref):
    return lax.dot_general(val_ref[...].astype(BF16), eexp_ref[...], (((0,), (0,)), ((), ())),
                           preferred_element_type=F32)


def _dispatch_kernel(start_ref, kmax_ref, xn_ref, val_ref, tgt_ref, eexp_ref, xe_ref, stage, carry, sem):
    j = pl.program_id(0)
    nt = pl.num_programs(0)
    slot = lax.rem(j, 2)
    ne = N_EXPERTS

    cap = xe_ref.shape[1] - MOE_PAD

    @pl.when(j == 0)
    def _():
        carry[...] = jnp.zeros_like(carry)
        stage[0, 0:MOE_PAD, :] = jnp.zeros((MOE_PAD, D_MODEL), BF16)
        fills = [pltpu.make_async_copy(stage.at[0, pl.ds(0, MOE_PAD)], xe_ref.at[e, pl.ds(cap, MOE_PAD)], sem.at[0])
                 for e in range(ne)]
        for f in fills:
            f.start()
        for f in fills:
            f.wait()

    def window_copy(sl, e, row0):
        return pltpu.make_async_copy(stage.at[sl, pl.ds(e * MOE_WIN, MOE_WIN)],
                                     xe_ref.at[e, pl.ds(row0, MOE_WIN)], sem.at[sl])

    def wait_windows(sl):
        for e in range(ne):
            window_copy(sl, e, 0).wait()

    rep = _slot_onehot(val_ref, eexp_ref)
    xn = xn_ref[...]
    row = lax.broadcasted_iota(jnp.int32, (MOE_ALIGN, D_MODEL), 0)

    def block(k, _):
        @pl.when(k > 0)
        def _():
            wait_windows(slot)

        lo = k * MOE_WIN
        pt = (rep == tgt_ref[...] + lo.astype(F32)).astype(BF16)
        comp = lax.dot_general(pt, xn, (((0,), (0,)), ((), ())), preferred_element_type=F32)
        stage[slot] = comp.astype(BF16)
        for e in range(ne):
            s = start_ref[j * ne + e]
            head = s & (MOE_ALIGN - 1)
            r0 = e * MOE_WIN
            kept = carry[e * MOE_ALIGN:(e + 1) * MOE_ALIGN, :]
            owned = row < jnp.where(k == 0, head, 0)
            stage[slot, r0:r0 + MOE_ALIGN, :] = jnp.where(owned, kept, stage[slot, r0:r0 + MOE_ALIGN, :])
            nxt = (head + start_ref[(j + 1) * ne + e] - s) & (-MOE_ALIGN)
            here = (nxt >= lo) & (nxt < lo + MOE_WIN)
            off = pl.multiple_of(jnp.clip(nxt - lo, 0, MOE_WIN - MOE_ALIGN), MOE_ALIGN)
            cand = stage[slot, pl.ds(r0 + off, MOE_ALIGN), :]
            carry[e * MOE_ALIGN:(e + 1) * MOE_ALIGN, :] = jnp.where(here, cand, kept)

        @pl.when((k == 0) & (j > 0))
        def _():
            wait_windows(1 - slot)

        for e in range(ne):
            base = pl.multiple_of((start_ref[j * ne + e] & (-MOE_ALIGN)) + lo, MOE_ALIGN)
            window_copy(slot, e, base).start()
        return 0

    lax.fori_loop(0, kmax_ref[j], block, 0)

    @pl.when(j == nt - 1)
    def _():
        wait_windows(slot)


def _dispatch(xn, val, starts, kmax, tgt, eexp, cap, tile):
    n = xn.shape[0]
    nt = n // tile
    rows = N_EXPERTS * MOE_WIN
    return pl.pallas_call(
        _dispatch_kernel,
        grid_spec=pltpu.PrefetchScalarGridSpec(
            num_scalar_prefetch=2, grid=(nt,),
            in_specs=[pl.BlockSpec((tile, D_MODEL), lambda j, s, k: (j, 0)),
                      pl.BlockSpec((N_EXPERTS, tile), lambda j, s, k: (0, j)),
                      pl.BlockSpec((None, 1, rows), lambda j, s, k: (j, 0, 0)),
                      pl.BlockSpec((N_EXPERTS, rows), lambda j, s, k: (0, 0))],
            out_specs=pl.BlockSpec(memory_space=pl.ANY),
            scratch_shapes=[pltpu.VMEM((2, rows, D_MODEL), BF16),
                            pltpu.VMEM((N_EXPERTS * MOE_ALIGN, D_MODEL), BF16),
                            pltpu.SemaphoreType.DMA((2,))]),
        out_shape=jax.ShapeDtypeStruct((N_EXPERTS, cap + MOE_PAD, D_MODEL), BF16),
        compiler_params=_params(("arbitrary",)),
        name="moe_dispatch",
    )(starts, kmax, xn, val, tgt, eexp)


def _expert_kernel(x_ref, wr_ref, wg32_ref, wu32_ref, wd32_ref, y_ref, wg_ref, wu_ref, wd_ref, *, npad):
    e = pl.program_id(0)
    i = pl.program_id(1)

    @pl.when(i == npad)
    def _():
        wg_ref[...] = wg32_ref[...].astype(BF16)
        wu_ref[...] = wu32_ref[...].astype(BF16)
        wd_ref[...] = wd32_ref[...].astype(BF16)

    @pl.when(i >= npad)
    def _():
        x = x_ref[...]
        logits = (jnp.dot(x, wr_ref[0], preferred_element_type=F32)
                  + jnp.dot(x, wr_ref[1], preferred_element_type=F32))
        lane = lax.broadcasted_iota(jnp.int32, logits.shape, 1)
        logits = jnp.where(lane < N_EXPERTS, logits, -jnp.inf)
        ex = jnp.exp(logits - jnp.max(logits, axis=-1, keepdims=True))
        gate = (jnp.sum(jnp.where(lane == e, ex, 0.0), axis=-1, keepdims=True)
                / jnp.sum(ex, axis=-1, keepdims=True))
        hid = []
        for c0 in range(0, EXPERT_FF, FF_CHUNK):
            hg = jnp.dot(x, wg_ref[:, c0:c0 + FF_CHUNK], preferred_element_type=F32)
            hu = jnp.dot(x, wu_ref[:, c0:c0 + FF_CHUNK], preferred_element_type=F32)
            hid.append((_silu(hg) * hu).astype(BF16))
        hid = jnp.concatenate(hid, axis=1)
        y_ref[...] = (jnp.dot(hid, wd_ref[...], preferred_element_type=F32) * gate).astype(BF16)

    @pl.when(i < npad)
    def _():
        y_ref[...] = jnp.zeros_like(y_ref)


def _expert_ffn(xe, wr2, wg, wu, wd, layer, cap, tc):
    ne, rows, _ = xe.shape
    ntile = cap // tc
    npad = rows // tc - ntile
    wspec = lambda shape: pl.BlockSpec((None, None) + shape, lambda e, i: (layer, e, 0, 0))
    return pl.pallas_call(
        functools.partial(_expert_kernel, npad=npad),
        grid=(ne, rows // tc),
        in_specs=[pl.BlockSpec((None, tc, D_MODEL), lambda e, i: (e, jnp.maximum(i - npad, 0), 0)),
                  pl.BlockSpec(wr2.shape, lambda e, i: (0, 0, 0)),
                  wspec((D_MODEL, EXPERT_FF)), wspec((D_MODEL, EXPERT_FF)), wspec((EXPERT_FF, D_MODEL))],
        out_specs=pl.BlockSpec((None, tc, D_MODEL),
                               lambda e, i: (e, jnp.where(i < npad, ntile + i, i - npad), 0)),
        out_shape=jax.ShapeDtypeStruct((ne, rows, D_MODEL), BF16),
        scratch_shapes=[pltpu.VMEM((D_MODEL, EXPERT_FF), BF16), pltpu.VMEM((D_MODEL, EXPERT_FF), BF16),
                        pltpu.VMEM((EXPERT_FF, D_MODEL), BF16)],
        compiler_params=pltpu.CompilerParams(dimension_semantics=("parallel", "arbitrary"),
                                             vmem_limit_bytes=FFN_VMEM_LIMIT),
        name="expert_ffn",
    )(xe, wr2, wg, wu, wd)


COMBINE_SUB = 2


def _combine_kernel(start_ref, kmax_ref, h_ref, p_ref, val_ref, tgt_ref, eexp_ref, gain_ref, wg_ref, wp_ref,
                    ye_ref, out_ref, stage, sem, *, tile):
    j = pl.program_id(0)
    nstep = pl.num_programs(0)
    slot = lax.rem(j, 2)
    ne = N_EXPERTS
    subs = range(COMBINE_SUB)

    def window_copy(sl, sub, e, row0):
        buf = sl * COMBINE_SUB + sub
        return pltpu.make_async_copy(ye_ref.at[e, pl.ds(row0, MOE_WIN)],
                                     stage.at[buf, pl.ds(e * MOE_WIN, MOE_WIN)], sem.at[buf])

    def fetch(sl, sub, tile_idx, lo):
        for e in range(ne):
            base = pl.multiple_of((start_ref[tile_idx * ne + e] & (-MOE_ALIGN)) + lo, MOE_ALIGN)
            window_copy(sl, sub, e, base).start()

    def wait_windows(sl, sub):
        for e in range(ne):
            window_copy(sl, sub, e, 0).wait()

    @pl.when(j == 0)
    def _():
        for sub in subs:
            fetch(slot, sub, sub, 0)

    @pl.when(j + 1 < nstep)
    def _():
        for sub in subs:
            fetch(1 - slot, sub, (j + 1) * COMBINE_SUB + sub, 0)

    reps = [lax.dot_general(val_ref[:, sub * tile:(sub + 1) * tile].astype(BF16), eexp_ref[...],
                            (((0,), (0,)), ((), ())), preferred_element_type=F32) for sub in subs]
    pts = [(reps[sub] == tgt_ref[sub]).astype(BF16) for sub in subs]
    for sub in subs:
        wait_windows(slot, sub)
    accs = [h_ref[sub * tile:(sub + 1) * tile, :]
            + jnp.dot(pts[sub], stage[slot * COMBINE_SUB + sub], preferred_element_type=F32) for sub in subs]
    for sub in subs:
        t = j * COMBINE_SUB + sub

        def extra(k, acc, sub=sub, t=t):
            lo = k * MOE_WIN
            fetch(slot, sub, t, lo)
            wait_windows(slot, sub)
            pk = (reps[sub] == tgt_ref[sub] + lo.astype(F32)).astype(BF16)
            return acc + jnp.dot(pk, stage[slot * COMBINE_SUB + sub], preferred_element_type=F32)

        accs[sub] = lax.fori_loop(1, kmax_ref[t], extra, accs[sub])
    x = jnp.concatenate(accs, axis=0)
    ms = jnp.mean(x * x, axis=-1, keepdims=True)
    xn = (x * lax.rsqrt(ms + NORM_EPS) * gain_ref[...]).astype(BF16)
    gate = _sigmoid(jnp.dot(xn, wg_ref[...], preferred_element_type=F32))
    proj = jnp.dot(p_ref[...].astype(BF16), wp_ref[...], preferred_element_type=F32)
    out_ref[...] = x + gate * proj


def _combine_ple(h2, p3, layer, ye, val, starts, kmax, tgt, eexp, gain, wg, wp, tile):
    n = h2.shape[0]
    step = COMBINE_SUB * tile
    assert n % step == 0
    rows = N_EXPERTS * MOE_WIN
    fixed = lambda j, s, k: (0, 0)
    return pl.pallas_call(
        functools.partial(_combine_kernel, tile=tile),
        grid_spec=pltpu.PrefetchScalarGridSpec(
            num_scalar_prefetch=2, grid=(n // step,),
            in_specs=[pl.BlockSpec((step, D_MODEL), lambda j, s, k: (j, 0)),
                      pl.BlockSpec((None, step, PLE_DIM), lambda j, s, k: (layer, j, 0)),
                      pl.BlockSpec((N_EXPERTS, step), lambda j, s, k: (0, j)),
                      pl.BlockSpec((COMBINE_SUB, 1, rows), lambda j, s, k: (j, 0, 0)),
                      pl.BlockSpec((N_EXPERTS, rows), fixed),
                      pl.BlockSpec((1, D_MODEL), fixed), pl.BlockSpec(wg.shape, fixed),
                      pl.BlockSpec(wp.shape, fixed),
                      pl.BlockSpec(memory_space=pl.ANY)],
            out_specs=pl.BlockSpec((step, D_MODEL), lambda j, s, k: (j, 0)),
            scratch_shapes=[pltpu.VMEM((2 * COMBINE_SUB, rows, D_MODEL), BF16),
                            pltpu.SemaphoreType.DMA((2 * COMBINE_SUB,))]),
        out_shape=jax.ShapeDtypeStruct((n, D_MODEL), F32),
        compiler_params=_params(("arbitrary",)),
        name="moe_combine_ple",
    )(starts, kmax, h2, p3, val, tgt, eexp, gain, wg, wp, ye)


def _relayout_w_in(w_in):
    o_beta = ZA + ZB + DN_WIDTH
    o_alpha = o_beta + 2 * DN_HEADS
    o_glu = o_alpha + 2 * DN_HEADS
    pieces = [w_in[:, :o_beta], w_in[:, o_glu:o_glu + 2 * CONV_CH]]
    for d in range(2):
        pieces.append(w_in[:, o_beta + d * DN_HEADS:o_beta + (d + 1) * DN_HEADS])
        pieces.append(w_in[:, o_alpha + d * DN_HEADS:o_alpha + (d + 1) * DN_HEADS])
    pieces.append(jnp.zeros((w_in.shape[0], LANES - 4 * DN_HEADS), w_in.dtype))
    return jnp.concatenate(pieces, axis=1).astype(BF16)


def _prep_layer(lw):
    (norm_mix, w_in, q_gain, k_gain, sink, dn_conv, dn_a_log, dn_dt_bias, dn_out_gain,
     cv_dw, cv_dw_bias, cv_ln_gain, cv_ln_bias, w_out, norm_ffn, w_router, w_gate, w_up, w_down,
     norm_ple, w_ple_gate, w_ple_proj) = lw
    w_perm = _relayout_w_in(w_in)
    hgain = jnp.concatenate([jnp.tile(q_gain, ATT_HEADS) * (ATT_HEAD_DIM ** -0.5),
                             jnp.tile(k_gain, ATT_KV_HEADS)]).reshape(1, -1)
    zeros4 = jnp.zeros((DN_HEADS,), F32)
    aneg = -jnp.exp(dn_a_log.astype(F32))
    aneg_row = jnp.concatenate([zeros4, aneg[0], zeros4, aneg[1]])
    dtb_row = jnp.concatenate([zeros4, dn_dt_bias[0], zeros4, dn_dt_bias[1]])
    pad = lambda r: jnp.pad(r, (0, LANES - r.shape[0])).reshape(1, LANES)
    wr = jnp.pad(w_router.astype(F32), ((0, 0), (0, LANES - N_EXPERTS)))
    wr_hi = wr.astype(BF16)
    wr2 = jnp.stack([wr_hi, (wr - wr_hi.astype(F32)).astype(BF16)])
    return dict(
        w_router2=wr2,
        norm_mix=norm_mix.reshape(1, -1), w_in=w_perm, hgain=hgain, sink=sink.astype(F32),
        dn_conv=dn_conv, aneg=pad(aneg_row), dtb=pad(dtb_row),
        dn_out_gain=jnp.tile(dn_out_gain, DN_HEADS).reshape(1, -1),
        cv_dw=cv_dw, cv_dw_bias=cv_dw_bias.reshape(1, -1), cv_ln_gain=cv_ln_gain.reshape(1, -1),
        cv_ln_bias=cv_ln_bias.reshape(1, -1), w_out=w_out.astype(BF16),
        norm_ffn=norm_ffn.reshape(1, -1),
        norm_ple=norm_ple.reshape(1, -1), w_ple_gate=w_ple_gate.astype(BF16), w_ple_proj=w_ple_proj.astype(BF16))


def _tiles(bsz, seqlen):
    n = bsz * seqlen
    return dict(tm=min(1024, n), tl=min(512, seqlen), ch=min(1024, seqlen), tcv=min(1024, seqlen))


def _moe_ple(h2, xn, aff_t, p3, layer, pw):
    n = h2.shape[0]
    cap = CAPACITY_FACTOR * n // N_EXPERTS
    tile = min(MOE_TILE, n)
    val, cnt = _select(aff_t, cap, tile)
    starts, kmax, tgt = _moe_plan(cnt, n // tile)
    eexp = _expand_matrix()
    xe = _dispatch(xn, val, starts, kmax, tgt, eexp, cap, tile)
    ye = _expert_ffn(xe, pw["w_router2"], pw["w_gate"], pw["w_up"], pw["w_down"], layer, cap,
                     min(MOE_FFN_TILE, cap))
    return _combine_ple(h2, p3, layer, ye, val, starts, kmax, tgt, eexp, pw["norm_ple"], pw["w_ple_gate"],
                        pw["w_ple_proj"], tile)


def _layer(h2, p3, layer, pw, bsz, seqlen):
    t = _tiles(bsz, seqlen)
    hm_att = _head_mean_matrix(ATT_Q + ATT_KV, ATT_HEAD_DIM)
    hs_dn = _head_sum_matrix(2 * DN_WIDTH, DN_HEAD_DIM)
    hm_dn = _head_mean_matrix(DN_WIDTH, DN_HEAD_DIM)
    za, zb, zg, glu_in, gates = _in_proj(h2, pw["norm_mix"], pw["w_in"], hm_att, pw["hgain"], t["tm"])
    o_a = _attention(za, pw["sink"], bsz, seqlen)
    y, gb = _dn_prep(zb, gates, pw["dn_conv"], hs_dn, pw["aneg"], pw["dtb"], bsz, seqlen, t["tl"])
    o_f, o_b = _dn_chunk(y, gb, bsz, seqlen, t["ch"])
    o_c = _conformer_conv(glu_in, pw["cv_dw"], pw["cv_dw_bias"], pw["cv_ln_gain"], pw["cv_ln_bias"],
                          bsz, seqlen, t["tcv"])
    h2, xn, aff_t = _out_proj_route(h2, o_a, o_f, o_b, zg, o_c, pw["dn_out_gain"], hm_dn, pw["w_out"],
                                    pw["norm_ffn"], pw["w_router2"], t["tm"])
    return _moe_ple(h2, xn, aff_t, p3, layer, pw)


def _trunk(x, p, layer_weights):
    bsz, seqlen, _ = x.shape
    h2 = x.reshape(bsz * seqlen, D_MODEL)
    p3 = p.reshape(p.shape[0], bsz * seqlen, PLE_DIM)
    for i, pw in enumerate(layer_weights):
        h2 = _layer(h2, p3, i, pw, bsz, seqlen)
    return h2.reshape(bsz, seqlen, D_MODEL)


def kernel(x_prompt, x_sample, p_prompt, p_sample, norm_mix, w_in, q_gain, k_gain, sink, dn_conv, dn_a_log,
           dn_dt_bias, dn_out_gain, cv_dw, cv_dw_bias, cv_ln_gain, cv_ln_bias, w_out, norm_ffn, w_router,
           w_gate, w_up, w_down, norm_ple, w_ple_gate, w_ple_proj):
    weights = (norm_mix, w_in, q_gain, k_gain, sink, dn_conv, dn_a_log, dn_dt_bias, dn_out_gain,
               cv_dw, cv_dw_bias, cv_ln_gain, cv_ln_bias, w_out, norm_ffn, w_router, w_gate, w_up, w_down,
               norm_ple, w_ple_gate, w_ple_proj)
    depth = w_in.shape[0]
    experts = dict(w_gate=w_gate, w_up=w_up, w_down=w_down)
    layer_weights = [dict(_prep_layer([w[i] for w in weights]), **experts) for i in range(depth)]
    return (_trunk(x_prompt, p_prompt, layer_weights), _trunk(x_sample, p_sample, layer_weights))
```

```python
import functools
import math

import numpy as np
import jax
import jax.numpy as jnp
from jax import lax
from jax.experimental import pallas as pl
from jax.experimental.pallas import tpu as pltpu

F32 = jnp.float32
BF16 = jnp.bfloat16

D_MODEL = 1024
ATT_HEADS = 8
ATT_KV_HEADS = 2
ATT_HEAD_DIM = 64
ATT_GROUP = ATT_HEADS // ATT_KV_HEADS
WINDOW = 128
ATT_BLOCK = 128
DN_HEADS = 4
DN_HEAD_DIM = 64
DN_WIDTH = DN_HEADS * DN_HEAD_DIM
DN_CHUNK = 64
CONV_CH = 256
CONV_WIDTH = 31
ATT_Q = ATT_HEADS * ATT_HEAD_DIM
ATT_KV = ATT_KV_HEADS * ATT_HEAD_DIM
N_EXPERTS = 16
CAPACITY_FACTOR = 2
EXPERT_FF = 1024
PLE_DIM = 256
NORM_EPS = 1e-6

LANES = 128
SUBLANES = 8
VMEM_LIMIT = 48 * 1024 * 1024

ZA = ATT_Q + 2 * ATT_KV
ZB = 3 * DN_WIDTH
ZW = ZA + ZB + DN_WIDTH + 2 * CONV_CH + LANES


def _params(sem):
    return pltpu.CompilerParams(dimension_semantics=sem, vmem_limit_bytes=VMEM_LIMIT)


def _head_mean_matrix(width, head):
    idx = np.arange(width) // head
    return jnp.asarray((idx[:, None] == idx[None, :]).astype(np.float32) / head, dtype=BF16)


def _head_sum_matrix(width, head):
    idx = np.arange(width) // head
    return jnp.asarray((idx[:, None] == idx[None, :]).astype(np.float32), dtype=BF16)


def _sigmoid(x):
    return 1.0 / (1.0 + jnp.exp(-x))


def _silu(x):
    return x * _sigmoid(x)


def _in_proj_kernel(x_ref, gain_ref, w_ref, hm_ref, hgain_ref, za_ref, zb_ref, zg_ref, glu_ref, gates_ref):
    x = x_ref[...]
    ms = jnp.mean(x * x, axis=-1, keepdims=True)
    a = (x * lax.rsqrt(ms + NORM_EPS) * gain_ref[...]).astype(BF16)
    z = jnp.dot(a, w_ref[...], preferred_element_type=F32)
    nqk = ATT_Q + ATT_KV
    qk = z[:, :nqk]
    hms = jnp.dot((qk * qk).astype(BF16), hm_ref[...], preferred_element_type=F32)
    za_ref[:, :nqk] = (qk * lax.rsqrt(hms + NORM_EPS) * hgain_ref[...]).astype(BF16)
    za_ref[:, nqk:] = z[:, nqk:ZA].astype(BF16)
    zb_ref[...] = z[:, ZA:ZA + ZB]
    zg_ref[...] = z[:, ZA + ZB:ZA + ZB + DN_WIDTH]
    glu_ref[...] = z[:, ZA + ZB + DN_WIDTH:ZA + ZB + DN_WIDTH + 2 * CONV_CH]
    gates_ref[...] = z[:, ZW - LANES:]


def _in_proj(h2, gain, w_perm, hm, hgain, tm):
    n = h2.shape[0]
    row = lambda i: (i, 0)
    fixed = lambda i: (0, 0)
    return pl.pallas_call(
        _in_proj_kernel,
        grid=(n // tm,),
        in_specs=[pl.BlockSpec((tm, D_MODEL), row), pl.BlockSpec((1, D_MODEL), fixed),
                  pl.BlockSpec((D_MODEL, ZW), fixed), pl.BlockSpec(hm.shape, fixed),
                  pl.BlockSpec(hgain.shape, fixed)],
        out_specs=[pl.BlockSpec((tm, ZA), row), pl.BlockSpec((tm, ZB), row), pl.BlockSpec((tm, DN_WIDTH), row),
                   pl.BlockSpec((tm, 2 * CONV_CH), row), pl.BlockSpec((tm, LANES), row)],
        out_shape=[jax.ShapeDtypeStruct((n, ZA), BF16), jax.ShapeDtypeStruct((n, ZB), F32),
                   jax.ShapeDtypeStruct((n, DN_WIDTH), F32), jax.ShapeDtypeStruct((n, 2 * CONV_CH), F32),
                   jax.ShapeDtypeStruct((n, LANES), F32)],
        compiler_params=_params(("parallel",)),
        name="in_proj",
    )(h2, gain, w_perm, hm, hgain)


ATT_MASKED = -1e30


def _attn_bias_table():
    i = np.arange(ATT_BLOCK)[:, None]
    c = np.arange(3 * ATT_BLOCK)[None, :]
    rel = c - ATT_BLOCK - i
    slopes = 2.0 ** (-8.0 * np.arange(1, ATT_HEADS + 1) / ATT_HEADS)
    table = np.empty((3, ATT_KV_HEADS, ATT_GROUP * ATT_BLOCK, 3 * ATT_BLOCK), np.float32)
    for variant in range(3):
        ok = np.abs(rel) <= WINDOW
        if variant == 0:
            ok = ok & (c >= ATT_BLOCK)
        if variant == 2:
            ok = ok & (c < 2 * ATT_BLOCK)
        for hd in range(ATT_HEADS):
            g, j = divmod(hd, ATT_GROUP)
            table[variant, g, j * ATT_BLOCK:(j + 1) * ATT_BLOCK] = np.where(ok, -slopes[hd] * np.abs(rel), ATT_MASKED)
    return jnp.asarray(table)


def _attn_kernel(sink_ref, q_ref, kvp_ref, kvo_ref, kvn_ref, bias_a_ref, bias_b_ref, o_ref):
    kv = jnp.concatenate([kvp_ref[...], kvo_ref[...], kvn_ref[...]], axis=0)
    hd_ = ATT_HEAD_DIM
    groups = range(ATT_KV_HEADS)
    heads = range(ATT_HEADS)
    rows = lambda t, hd: t[(hd % ATT_GROUP) * ATT_BLOCK:(hd % ATT_GROUP + 1) * ATT_BLOCK]
    work = []
    for blk, bias_ref in enumerate((bias_a_ref, bias_b_ref)):
        keys = kv[blk * ATT_BLOCK:(blk + 3) * ATT_BLOCK]
        ks = [keys[:, g * hd_:(g + 1) * hd_] for g in groups]
        vs = [keys[:, ATT_KV + g * hd_:ATT_KV + (g + 1) * hd_] for g in groups]
        q = q_ref[blk * ATT_BLOCK:(blk + 1) * ATT_BLOCK, :]
        qs = [jnp.concatenate([q[:, (g * ATT_GROUP + j) * hd_:(g * ATT_GROUP + j + 1) * hd_]
                               for j in range(ATT_GROUP)], axis=0) for g in groups]
        sg = [lax.dot_general(qs[g], ks[g], (((1,), (1,)), ((), ())), preferred_element_type=F32) + bias_ref[g]
              for g in groups]
        work.append((vs, [rows(sg[hd // ATT_GROUP], hd) for hd in heads]))
    m = [[jnp.maximum(jnp.max(s[hd], axis=-1, keepdims=True), sink_ref[hd]) for hd in heads] for _, s in work]
    e = [[jnp.exp(s[hd] - mb[hd]) for hd in heads] for (_, s), mb in zip(work, m)]
    den = [[jnp.sum(eb[hd], axis=-1, keepdims=True) + jnp.exp(sink_ref[hd] - mb[hd]) for hd in heads]
           for eb, mb in zip(e, m)]
    for blk, ((vs, _), eb, db) in enumerate(zip(work, e, den)):
        eg = [jnp.concatenate([eb[g * ATT_GROUP + j].astype(BF16) for j in range(ATT_GROUP)], axis=0)
              for g in groups]
        og = [jnp.dot(eg[g], vs[g], preferred_element_type=F32) for g in groups]
        for hd in heads:
            o_ref[blk * ATT_BLOCK:(blk + 1) * ATT_BLOCK, hd * hd_:(hd + 1) * hd_] = (
                rows(og[hd // ATT_GROUP], hd) / db[hd]).astype(BF16)


def _attention(za, sink, bsz, seqlen):
    nb = seqlen // ATT_BLOCK
    assert nb >= 2 and nb % 2 == 0
    npair = nb // 2
    za3 = za.reshape(bsz, seqlen, ZA)
    kvw = 2 * ATT_KV
    kvc = ATT_Q // kvw
    bias = _attn_bias_table()
    bias_spec = lambda pick: pl.BlockSpec((None,) + bias.shape[1:], lambda b, n: (pick(n), 0, 0, 0))
    return pl.pallas_call(
        _attn_kernel,
        grid=(bsz, npair),
        in_specs=[pl.BlockSpec(memory_space=pltpu.SMEM),
                  pl.BlockSpec((None, 2 * ATT_BLOCK, ATT_Q), lambda b, n: (b, n, 0)),
                  pl.BlockSpec((None, ATT_BLOCK, kvw), lambda b, n: (b, jnp.maximum(2 * n - 1, 0), kvc)),
                  pl.BlockSpec((None, 2 * ATT_BLOCK, kvw), lambda b, n: (b, n, kvc)),
                  pl.BlockSpec((None, ATT_BLOCK, kvw), lambda b, n: (b, jnp.minimum(2 * n + 2, nb - 1), kvc)),
                  bias_spec(lambda n: jnp.where(n == 0, 0, 1)),
                  bias_spec(lambda n: jnp.where(n == npair - 1, 2, 1))],
        out_specs=pl.BlockSpec((None, 2 * ATT_BLOCK, ATT_Q), lambda b, n: (b, n, 0)),
        out_shape=jax.ShapeDtypeStruct((bsz, seqlen, ATT_Q), BF16),
        compiler_params=_params(("parallel", "parallel")),
        name="window_attention",
    )(sink, za3, za3, za3, za3, bias, bias).reshape(bsz * seqlen, ATT_Q)


DN_HALO = SUBLANES


def _dn_prep_kernel(x_ref, xp_ref, xn_ref, cw_ref, hs_ref, g_ref, aneg_ref, dtb_ref, mf_ref, mb_ref,
                    y_ref, gb_ref, buf_ref, *, tl):
    i = pl.program_id(1)
    nt = pl.num_programs(1)
    buf_ref[0:DN_HALO, :] = jnp.where(i > 0, xp_ref[...], 0.0)
    buf_ref[DN_HALO:DN_HALO + tl, :] = x_ref[...]
    buf_ref[DN_HALO + tl:, :] = jnp.where(i < nt - 1, xn_ref[...], 0.0)
    y = (cw_ref[0:1, :] * buf_ref[DN_HALO - 1:DN_HALO - 1 + tl, :]
         + cw_ref[1:2, :] * buf_ref[DN_HALO:DN_HALO + tl, :]
         + cw_ref[2:3, :] * buf_ref[DN_HALO + 1:DN_HALO + 1 + tl, :])
    y = _silu(y)
    qk = y[:, :2 * DN_WIDTH]
    ss = jnp.dot((qk * qk).astype(BF16), hs_ref[...], preferred_element_type=F32)
    lane = lax.broadcasted_iota(jnp.int32, (tl, 2 * DN_WIDTH), 1)
    scale = jnp.where(lane < DN_WIDTH, DN_HEAD_DIM ** -0.5, 1.0)
    y_ref[:, :2 * DN_WIDTH] = qk * lax.rsqrt(ss + NORM_EPS) * scale
    y_ref[:, 2 * DN_WIDTH:] = y[:, 2 * DN_WIDTH:]
    raw = g_ref[...]
    col = lax.broadcasted_iota(jnp.int32, (tl, LANES), 1)
    is_beta = (col & DN_HEADS) == 0
    t = raw + dtb_ref[...]
    softplus = jnp.maximum(t, 0.0) + jnp.log(1.0 + jnp.exp(-jnp.abs(t)))
    vals = jnp.where(is_beta, _sigmoid(raw), aneg_ref[...] * softplus)
    v_hi = vals.astype(BF16)
    r1 = vals - v_hi.astype(F32)
    v_mid = r1.astype(BF16)
    v_lo = (r1 - v_mid.astype(F32)).astype(BF16)
    terms = jnp.concatenate([v_hi, v_mid, v_lo], axis=1)
    cf3 = jnp.dot(mf_ref[...], terms, preferred_element_type=F32)
    cb3 = jnp.dot(mb_ref[...], terms, preferred_element_type=F32)
    cf = cf3[:, :LANES] + (cf3[:, LANES:2 * LANES] + cf3[:, 2 * LANES:])
    cb = cb3[:, :LANES] + (cb3[:, LANES:2 * LANES] + cb3[:, 2 * LANES:])
    gb_ref[0] = jnp.where(is_beta, vals, cf)
    gb_ref[1] = pltpu.roll(jnp.where(is_beta, vals, cb), LANES - 2 * DN_HEADS, axis=1)


def _dn_prep(zb, gates, conv_w, hs, aneg, dtb, bsz, seqlen, tl):
    zb3 = zb.reshape(bsz, seqlen, ZB)
    g3 = gates.reshape(bsz, seqlen, LANES)
    nt = seqlen // tl
    hb = tl // DN_HALO
    ch = np.arange(tl) // DN_CHUNK
    same = ch[:, None] == ch[None, :]
    pos = np.arange(tl)
    mf = jnp.asarray((same & (pos[None, :] <= pos[:, None])).astype(np.float32), dtype=BF16)
    mb = jnp.asarray((same & (pos[None, :] >= pos[:, None])).astype(np.float32), dtype=BF16)
    fixed = lambda b, i: (0, 0)
    y, gb = pl.pallas_call(
        functools.partial(_dn_prep_kernel, tl=tl),
        grid=(bsz, nt),
        in_specs=[pl.BlockSpec((None, tl, ZB), lambda b, i: (b, i, 0)),
                  pl.BlockSpec((None, DN_HALO, ZB), lambda b, i: (b, jnp.maximum(i * hb - 1, 0), 0)),
                  pl.BlockSpec((None, DN_HALO, ZB), lambda b, i: (b, jnp.minimum((i + 1) * hb, nt * hb - 1), 0)),
                  pl.BlockSpec(conv_w.shape, fixed), pl.BlockSpec(hs.shape, fixed),
                  pl.BlockSpec((None, tl, LANES), lambda b, i: (b, i, 0)),
                  pl.BlockSpec((1, LANES), fixed), pl.BlockSpec((1, LANES), fixed),
                  pl.BlockSpec((tl, tl), fixed), pl.BlockSpec((tl, tl), fixed)],
        out_specs=[pl.BlockSpec((None, tl, ZB), lambda b, i: (b, i, 0)),
                   pl.BlockSpec((2, None, tl, LANES), lambda b, i: (0, b, i, 0))],
        out_shape=[jax.ShapeDtypeStruct((bsz, seqlen, ZB), F32),
                   jax.ShapeDtypeStruct((2, bsz, seqlen, LANES), F32)],
        scratch_shapes=[pltpu.VMEM((tl + 2 * DN_HALO, ZB), F32)],
        compiler_params=_params(("parallel", "parallel")),
        name="deltanet_prep",
    )(zb3, zb3, zb3, conv_w, hs, g3, aneg, dtb, mf, mb)
    return y, gb


def _lane_expand(cols, first):
    c = cols.shape[0]
    lane = lax.broadcasted_iota(jnp.int32, (c, LANES), 1)
    halves = []
    for h in range(0, DN_HEADS, 2):
        a = jnp.broadcast_to(cols[:, first + h:first + h + 1], (c, LANES))
        b = jnp.broadcast_to(cols[:, first + h + 1:first + h + 2], (c, LANES))
        halves.append(jnp.where(lane < DN_HEAD_DIM, a, b))
    return jnp.concatenate(halves, axis=1)


def _dn_pair_kernel(xf_ref, xb_ref, gf_ref, gb_ref, of_ref, ob_ref, sf_ref, sb_ref, *, nsub):
    c = DN_CHUNK
    w = DN_WIDTH

    @pl.when(pl.program_id(1) == 0)
    def _():
        sf_ref[...] = jnp.zeros_like(sf_ref)
        sb_ref[...] = jnp.zeros_like(sb_ref)

    r_cat = lax.broadcasted_iota(jnp.int32, (c, w), 0)
    s_cat = lax.broadcasted_iota(jnp.int32, (c, w), 1) & (DN_HEAD_DIM - 1)
    eye_cat = s_cat == r_cat
    rr = lax.broadcasted_iota(jnp.int32, (w, w), 0)
    cc = lax.broadcasted_iota(jnp.int32, (w, w), 1)
    head = (rr >> 6) == (cc >> 6)
    head_b = head.astype(BF16)
    m16 = (s_cat >> 4) == (r_cat >> 4)
    m32 = (s_cat >> 5) == (r_cat >> 5)
    off16 = m32 & jnp.logical_not(m16)
    off32 = jnp.logical_not(m32)
    eye_f = eye_cat.astype(F32)

    def bd(t):
        return jnp.concatenate([t] * DN_HEADS, axis=0) * head_b

    def mm(a, b):
        return jnp.dot(a, b, preferred_element_type=F32)

    chunks = [(0, i * c) for i in range(nsub)] + [(1, (nsub - 1 - i) * c) for i in range(nsub)]
    xrefs = (xf_ref, xb_ref)
    grefs = (gf_ref, gb_ref)
    orefs = (of_ref, ob_ref)
    srefs = (sf_ref, sb_ref)
    incl = (s_cat <= r_cat, s_cat >= r_cat)
    strict = (s_cat < r_cat, s_cat > r_cat)
    last_row = (c - 1, 0)

    pre = []
    for d, st in chunks:
        x = xrefs[d][st:st + c, :]
        q, k, v = x[:, :w], x[:, w:2 * w], x[:, 2 * w:]
        gbt = grefs[d][st:st + c, :]
        beta = _lane_expand(gbt, 0)
        gc = _lane_expand(gbt, DN_HEADS)
        grow = jnp.sum(jnp.where(eye_cat, gc, 0.0), axis=0, keepdims=True)
        decay = jnp.exp(jnp.where(incl[d], gc - grow, -jnp.inf))
        glast = gc[last_row[d]:last_row[d] + 1, :]
        egc = jnp.exp(gc)
        kb = k * beta
        pre.append(dict(d=d, st=st, q=q, k=k, kb=kb, vb=v * beta, decay=decay, glast=glast, egc=egc,
                        kdec=(k * jnp.exp(glast - gc)).astype(BF16)))

    kks = [lax.dot_general(jnp.concatenate([p["kb"], p["q"]], axis=0).astype(BF16), bd(p["k"].astype(BF16)),
                           (((1,), (1,)), ((), ())), preferred_element_type=F32) for p in pre]
    a = [jnp.where(strict[p["d"]], kk[:c] * p["decay"], 0.0) for p, kk in zip(pre, kks)]
    intra = [jnp.where(incl[p["d"]], kk[c:] * p["decay"], 0.0).astype(BF16) for p, kk in zip(pre, kks)]
    xm = [jnp.where(m16, -t, 0.0) for t in a]
    xm_b = [t.astype(BF16) for t in xm]
    x2_b = [mm(t, bd(t)).astype(BF16) for t in xm_b]
    x2_d = [bd(t) for t in x2_b]
    dinv = [eye_f + t for t in xm]
    r2 = [mm(jnp.concatenate([t.astype(BF16), p2], axis=0), s2) for t, p2, s2 in zip(dinv, x2_b, x2_d)]
    dinv = [t + r[:c] for t, r in zip(dinv, r2)]
    x4_b = [r[c:].astype(BF16) for r in r2]
    x4_d = [bd(t) for t in x4_b]
    r4 = [mm(jnp.concatenate([t.astype(BF16), p4], axis=0), s4) for t, p4, s4 in zip(dinv, x4_b, x4_d)]
    dinv = [t + r[:c] for t, r in zip(dinv, r4)]
    x8_d = [bd(r[c:].astype(BF16)) for r in r4]
    dinv = [t + mm(t.astype(BF16), s8) for t, s8 in zip(dinv, x8_d)]
    dinv_b = [t.astype(BF16) for t in dinv]
    n32 = [bd(mm(jnp.where(off16, t, 0.0).astype(BF16), bd(db)).astype(BF16)) for t, db in zip(a, dinv_b)]
    t32 = [t - mm(db, n) for t, db, n in zip(dinv, dinv_b, n32)]
    t32_b = [t.astype(BF16) for t in t32]
    n64 = [bd(mm(jnp.where(off32, t, 0.0).astype(BF16), bd(tb)).astype(BF16)) for t, tb in zip(a, t32_b)]
    t_cat = [(t - mm(tb, n)).astype(BF16) for t, tb, n in zip(t32, t32_b, n64)]
    uw = [mm(tc, jnp.concatenate([bd(p["vb"].astype(BF16)), bd((p["kb"] * p["egc"]).astype(BF16))], axis=1))
          for tc, p in zip(t_cat, pre)]
    uw_b = [t.astype(BF16) for t in uw]
    pn = [lax.dot_general(p["kdec"], t, (((0,), (0,)), ((), ())), preferred_element_type=F32)
          for p, t in zip(pre, uw_b)]
    qo = [mm(it, jnp.concatenate([bd(t[:, :w]), bd(t[:, w:])], axis=1)) for it, t in zip(intra, uw_b)]
    lhs = [jnp.concatenate([n[:, w:].astype(BF16) * head_b, (p["q"] * p["egc"] - o[:, w:]).astype(BF16)], axis=0)
           for n, o, p in zip(pn, qo, pre)]
    for step in range(nsub):
        for d in range(2):
            i = d * nsub + step
            p = pre[i]
            state = srefs[d][...]
            r = jnp.dot(lhs[i], state.astype(BF16), preferred_element_type=F32)
            orefs[d][p["st"]:p["st"] + c, :] = r[w:] + qo[i][:, :w]
            srefs[d][...] = state * jnp.exp(p["glast"]) - r[:w] + jnp.where(head, pn[i][:, :w], 0.0)


def _dn_chunk(y, gb, bsz, seqlen, ch):
    nsub = ch // DN_CHUNK
    nblk = seqlen // ch
    fwd = lambda b, j: (b, j, 0)
    bwd = lambda b, j: (b, nblk - 1 - j, 0)
    o_f, o_b = pl.pallas_call(
        functools.partial(_dn_pair_kernel, nsub=nsub),
        grid=(bsz, nblk),
        in_specs=[pl.BlockSpec((None, ch, ZB), fwd), pl.BlockSpec((None, ch, ZB), bwd),
                  pl.BlockSpec((None, None, ch, LANES), lambda b, j: (0, b, j, 0)),
                  pl.BlockSpec((None, None, ch, LANES), lambda b, j: (1, b, nblk - 1 - j, 0))],
        out_specs=[pl.BlockSpec((None, ch, DN_WIDTH), fwd), pl.BlockSpec((None, ch, DN_WIDTH), bwd)],
        out_shape=[jax.ShapeDtypeStruct((bsz, seqlen, DN_WIDTH), F32)] * 2,
        scratch_shapes=[pltpu.VMEM((DN_WIDTH, DN_WIDTH), F32)] * 2,
        compiler_params=_params(("parallel", "arbitrary")),
        name="deltanet_chunks",
    )(y, y, gb, gb)
    return o_f.reshape(bsz * seqlen, DN_WIDTH), o_b.reshape(bsz * seqlen, DN_WIDTH)


CV_HALO = 2 * SUBLANES
CV_PAD = (CONV_WIDTH - 1) // 2


def _conv_kernel(x_ref, xp_ref, xn_ref, dw_ref, bias_ref, lng_ref, lnb_ref, o_ref, buf_ref, shift_ref, *, tl):
    i = pl.program_id(1)
    nt = pl.num_programs(1)

    def glu(t):
        return t[:, :CONV_CH] * _sigmoid(t[:, CONV_CH:])

    buf_ref[0:CV_HALO, :] = jnp.where(i > 0, glu(xp_ref[...]), 0.0)
    buf_ref[CV_HALO:CV_HALO + tl, :] = glu(x_ref[...])
    buf_ref[CV_HALO + tl:, :] = jnp.where(i < nt - 1, glu(xn_ref[...]), 0.0)
    acc = jnp.zeros((tl, CONV_CH), F32) + bias_ref[...]
    first = CV_HALO - CV_PAD
    span = -(-(first + CONV_WIDTH) // SUBLANES) * SUBLANES - SUBLANES
    for sub in range(SUBLANES):
        shift_ref[...] = buf_ref[sub:sub + tl + span, :]
        for base in range(0, span + 1, SUBLANES):
            j = base + sub - first
            if 0 <= j < CONV_WIDTH:
                acc = acc + dw_ref[j:j + 1, :] * shift_ref[base:base + tl, :]
    mu = jnp.mean(acc, axis=-1, keepdims=True)
    cen = acc - mu
    var = jnp.mean(cen * cen, axis=-1, keepdims=True)
    o_ref[...] = _silu(cen * lax.rsqrt(var + NORM_EPS) * lng_ref[...] + lnb_ref[...]).astype(BF16)


def _conformer_conv(glu_in, dw, bias, lng, lnb, bsz, seqlen, tl):
    x3 = glu_in.reshape(bsz, seqlen, 2 * CONV_CH)
    nt = seqlen // tl
    hb = tl // CV_HALO
    fixed = lambda b, i: (0, 0)
    return pl.pallas_call(
        functools.partial(_conv_kernel, tl=tl),
        grid=(bsz, nt),
        in_specs=[pl.BlockSpec((None, tl, 2 * CONV_CH), lambda b, i: (b, i, 0)),
                  pl.BlockSpec((None, CV_HALO, 2 * CONV_CH), lambda b, i: (b, jnp.maximum(i * hb - 1, 0), 0)),
                  pl.BlockSpec((None, CV_HALO, 2 * CONV_CH),
                               lambda b, i: (b, jnp.minimum((i + 1) * hb, nt * hb - 1), 0)),
                  pl.BlockSpec(dw.shape, fixed), pl.BlockSpec((1, CONV_CH), fixed),
                  pl.BlockSpec((1, CONV_CH), fixed), pl.BlockSpec((1, CONV_CH), fixed)],
        out_specs=pl.BlockSpec((None, tl, CONV_CH), lambda b, i: (b, i, 0)),
        out_shape=jax.ShapeDtypeStruct((bsz, seqlen, CONV_CH), BF16),
        scratch_shapes=[pltpu.VMEM((tl + 2 * CV_HALO, CONV_CH), F32),
                        pltpu.VMEM((tl + 2 * CV_HALO - SUBLANES, CONV_CH), F32)],
        compiler_params=_params(("parallel", "parallel")),
        name="conformer_conv",
    )(x3, x3, x3, dw, bias, lng, lnb).reshape(bsz * seqlen, CONV_CH)


def _out_proj_kernel(h_ref, oa_ref, of_ref, ob_ref, zg_ref, oc_ref, og_ref, hm_ref, w_ref, gain_ref, wr_ref,
                     out_ref, xn_ref, aff_ref):
    half = h_ref.shape[0] // 2
    for r0 in (0, half):
        rs = slice(r0, r0 + half)
        ob = of_ref[rs, :] + ob_ref[rs, :]
        ms = jnp.dot((ob * ob).astype(BF16), hm_ref[...], preferred_element_type=F32)
        obn = ob * lax.rsqrt(ms + NORM_EPS) * og_ref[...]
        ob2 = obn * _silu(zg_ref[rs, :])
        mix = jnp.concatenate([oa_ref[rs, :], ob2.astype(BF16), oc_ref[rs, :]], axis=1)
        x = h_ref[rs, :] + jnp.dot(mix, w_ref[...], preferred_element_type=F32)
        out_ref[rs, :] = x
        ms = jnp.mean(x * x, axis=-1, keepdims=True)
        xn = x * lax.rsqrt(ms + NORM_EPS) * gain_ref[...]
        xn_hi = xn.astype(BF16)
        xn_ref[rs, :] = xn_hi
        xn_lo = (xn - xn_hi.astype(F32)).astype(BF16)
        logits = (jnp.dot(xn_hi, wr_ref[0], preferred_element_type=F32)
                  + (jnp.dot(xn_lo, wr_ref[0], preferred_element_type=F32)
                     + jnp.dot(xn_hi, wr_ref[1], preferred_element_type=F32)))
        lane = lax.broadcasted_iota(jnp.int32, logits.shape, 1)
        logits = jnp.where(lane < N_EXPERTS, logits, -jnp.inf)
        m = jnp.max(logits, axis=-1, keepdims=True)
        e = jnp.exp(logits - m)
        aff = e / jnp.sum(e, axis=-1, keepdims=True)
        aff_ref[:, rs] = jnp.transpose(aff)[:N_EXPERTS, :]


def _out_proj_route(h2, oa, o_f, o_b, zg, oc, og, hm, w, gain, wr2, tm):
    n = h2.shape[0]
    row = lambda i: (i, 0)
    fixed = lambda i: (0, 0)
    return pl.pallas_call(
        _out_proj_kernel,
        grid=(n // tm,),
        in_specs=[pl.BlockSpec((tm, D_MODEL), row), pl.BlockSpec((tm, ATT_Q), row),
                  pl.BlockSpec((tm, DN_WIDTH), row), pl.BlockSpec((tm, DN_WIDTH), row),
                  pl.BlockSpec((tm, DN_WIDTH), row),
                  pl.BlockSpec((tm, CONV_CH), row), pl.BlockSpec((1, DN_WIDTH), fixed),
                  pl.BlockSpec(hm.shape, fixed), pl.BlockSpec(w.shape, fixed),
                  pl.BlockSpec((1, D_MODEL), fixed), pl.BlockSpec((2, D_MODEL, LANES), lambda i: (0, 0, 0))],
        out_specs=[pl.BlockSpec((tm, D_MODEL), row), pl.BlockSpec((tm, D_MODEL), row),
                   pl.BlockSpec((N_EXPERTS, tm), lambda i: (0, i))],
        out_shape=[jax.ShapeDtypeStruct((n, D_MODEL), F32), jax.ShapeDtypeStruct((n, D_MODEL), BF16),
                   jax.ShapeDtypeStruct((N_EXPERTS, n), F32)],
        compiler_params=_params(("parallel",)),
        name="out_proj_route",
    )(h2, oa, o_f, o_b, zg, oc, og, hm, w, gain, wr2)


MOE_TILE = 256
MOE_ALIGN = 2 * SUBLANES
MOE_WIN = 64
MOE_PAD = 1024
MOE_FFN_TILE = 1024
FF_CHUNK = 256
MOE_UNSELECTED = -64.0


def _select_kernel(aff_ref, tri_ref, val_ref, cnt_ref, *, cap, tile):
    ne, n = aff_ref.shape
    nt = n // tile
    capf = float(cap)

    def bits_of(x):
        return lax.bitcast_convert_type(x, jnp.int32)

    def search(i, thr):
        cand = thr | jnp.left_shift(jnp.int32(1), 30 - i)
        cnt = jnp.sum((bits_of(aff_ref[...]) >= cand).astype(F32), axis=1, keepdims=True)
        return jnp.where(cnt >= capf, cand, thr)

    thr = lax.fori_loop(0, 31, search, jnp.zeros((ne, 1), jnp.int32))
    n_gt = jnp.sum((bits_of(aff_ref[...]) > thr).astype(F32), axis=1, keepdims=True)
    need = capf - n_gt
    lane = lax.broadcasted_iota(jnp.int32, (ne, LANES), 1)

    def tile_body(j, carry):
        eq_before, cnt_acc = carry
        off = pl.multiple_of(j * tile, tile)
        b = bits_of(aff_ref[:, pl.ds(off, tile)])
        gt = b > thr
        eqf = (b == thr).astype(F32)
        eq_rank = eq_before + jnp.dot(eqf.astype(BF16), tri_ref[...], preferred_element_type=F32)
        self_ = jnp.where(gt, 1.0, jnp.where(eq_rank <= need, eqf, 0.0))
        rank = jnp.dot(self_.astype(BF16), tri_ref[...], preferred_element_type=F32)
        val_ref[:, pl.ds(off, tile)] = jnp.where(self_ > 0.0, rank, MOE_UNSELECTED)
        cnt = jnp.sum(self_, axis=1, keepdims=True)
        return (eq_before + jnp.sum(eqf, axis=1, keepdims=True), cnt_acc + jnp.where(lane == j, cnt, 0.0))

    init = (jnp.zeros((ne, 1), F32), jnp.zeros((ne, LANES), F32))
    _, cnt_acc = lax.fori_loop(0, nt, tile_body, init, unroll=math.gcd(nt, 4))
    cnt_ref[...] = cnt_acc


def _select(aff_t, cap, tile):
    ne, n = aff_t.shape
    assert n // tile <= LANES
    tri = jnp.asarray(np.triu(np.ones((tile, tile), np.float32)), dtype=BF16)
    return pl.pallas_call(
        functools.partial(_select_kernel, cap=cap, tile=tile),
        out_shape=[jax.ShapeDtypeStruct((ne, n), F32), jax.ShapeDtypeStruct((ne, LANES), F32)],
        compiler_params=pltpu.CompilerParams(vmem_limit_bytes=VMEM_LIMIT),
        name="moe_select",
    )(aff_t, tri)


def _moe_plan(cnt, nt):
    c = cnt[:, :nt].astype(jnp.int32).T
    starts = jnp.concatenate([jnp.zeros((1, N_EXPERTS), jnp.int32), jnp.cumsum(c, axis=0)], axis=0)
    head = starts[:-1] & (MOE_ALIGN - 1)
    kmax = jnp.maximum(jnp.max((head + c + MOE_WIN - 1) // MOE_WIN, axis=1), 1).astype(jnp.int32)
    w = jnp.arange(MOE_WIN, dtype=jnp.int32)
    tgt = (w[None, None, :] + 1 - head[:, :, None]).astype(F32).reshape(nt, 1, N_EXPERTS * MOE_WIN)
    return starts.reshape(-1), kmax, tgt


def _expand_matrix():
    e = np.arange(N_EXPERTS * MOE_WIN) // MOE_WIN
    return jnp.asarray((np.arange(N_EXPERTS)[:, None] == e[None, :]).astype(np.float32), dtype=BF16)


def _slot_onehot(val_ref, eexp_ref):
    return lax.dot_general(val_ref[...].astype(BF16), eexp_ref[...], (((0,), (0,)), ((), ())),
                           preferred_element_type=F32)


def _dispatch_kernel(start_ref, kmax_ref, xn_ref, val_ref, tgt_ref, eexp_ref, xe_ref, stage, carry, sem):
    j = pl.program_id(0)
    nt = pl.num_programs(0)
    slot = lax.rem(j, 2)
    ne = N_EXPERTS

    cap = xe_ref.shape[1] - MOE_PAD

    @pl.when(j == 0)
    def _():
        carry[...] = jnp.zeros_like(carry)
        stage[0, 0:MOE_PAD, :] = jnp.zeros((MOE_PAD, D_MODEL), BF16)
        fills = [pltpu.make_async_copy(stage.at[0, pl.ds(0, MOE_PAD)], xe_ref.at[e, pl.ds(cap, MOE_PAD)], sem.at[0])
                 for e in range(ne)]
        for f in fills:
            f.start()
        for f in fills:
            f.wait()

    def window_copy(sl, e, row0):
        return pltpu.make_async_copy(stage.at[sl, pl.ds(e * MOE_WIN, MOE_WIN)],
                                     xe_ref.at[e, pl.ds(row0, MOE_WIN)], sem.at[sl])

    def wait_windows(sl):
        for e in range(ne):
            window_copy(sl, e, 0).wait()

    rep = _slot_onehot(val_ref, eexp_ref)
    xn = xn_ref[...]
    row = lax.broadcasted_iota(jnp.int32, (MOE_ALIGN, D_MODEL), 0)

    def block(k, _):
        @pl.when(k > 0)
        def _():
            wait_windows(slot)

        lo = k * MOE_WIN
        pt = (rep == tgt_ref[...] + lo.astype(F32)).astype(BF16)
        comp = lax.dot_general(pt, xn, (((0,), (0,)), ((), ())), preferred_element_type=F32)
        stage[slot] = comp.astype(BF16)
        for e in range(ne):
            s = start_ref[j * ne + e]
            head = s & (MOE_ALIGN - 1)
            r0 = e * MOE_WIN
            kept = carry[e * MOE_ALIGN:(e + 1) * MOE_ALIGN, :]
            owned = row < jnp.where(k == 0, head, 0)
            stage[slot, r0:r0 + MOE_ALIGN, :] = jnp.where(owned, kept, stage[slot, r0:r0 + MOE_ALIGN, :])
            nxt = (head + start_ref[(j + 1) * ne + e] - s) & (-MOE_ALIGN)
            here = (nxt >= lo) & (nxt < lo + MOE_WIN)
            off = pl.multiple_of(jnp.clip(nxt - lo, 0, MOE_WIN - MOE_ALIGN), MOE_ALIGN)
            cand = stage[slot, pl.ds(r0 + off, MOE_ALIGN), :]
            carry[e * MOE_ALIGN:(e + 1) * MOE_ALIGN, :] = jnp.where(here, cand, kept)

        @pl.when((k == 0) & (j > 0))
        def _():
            wait_windows(1 - slot)

        for e in range(ne):
            base = pl.multiple_of((start_ref[j * ne + e] & (-MOE_ALIGN)) + lo, MOE_ALIGN)
            window_copy(slot, e, base).start()
        return 0

    lax.fori_loop(0, kmax_ref[j], block, 0)

    @pl.when(j == nt - 1)
    def _():
        wait_windows(slot)


def _dispatch(xn, val, starts, kmax, tgt, eexp, cap, tile):
    n = xn.shape[0]
    nt = n // tile
    rows = N_EXPERTS * MOE_WIN
    return pl.pallas_call(
        _dispatch_kernel,
        grid_spec=pltpu.PrefetchScalarGridSpec(
            num_scalar_prefetch=2, grid=(nt,),
            in_specs=[pl.BlockSpec((tile, D_MODEL), lambda j, s, k: (j, 0)),
                      pl.BlockSpec((N_EXPERTS, tile), lambda j, s, k: (0, j)),
                      pl.BlockSpec((None, 1, rows), lambda j, s, k: (j, 0, 0)),
                      pl.BlockSpec((N_EXPERTS, rows), lambda j, s, k: (0, 0))],
            out_specs=pl.BlockSpec(memory_space=pl.ANY),
            scratch_shapes=[pltpu.VMEM((2, rows, D_MODEL), BF16),
                            pltpu.VMEM((N_EXPERTS * MOE_ALIGN, D_MODEL), BF16),
                            pltpu.SemaphoreType.DMA((2,))]),
        out_shape=jax.ShapeDtypeStruct((N_EXPERTS, cap + MOE_PAD, D_MODEL), BF16),
        compiler_params=_params(("arbitrary",)),
        name="moe_dispatch",
    )(starts, kmax, xn, val, tgt, eexp)


def _expert_kernel(x_ref, wr_ref, wg32_ref, wu32_ref, wd32_ref, y_ref, wg_ref, wu_ref, wd_ref, *, npad):
    e = pl.program_id(0)
    i = pl.program_id(1)

    @pl.when(i == npad)
    def _():
        wg_ref[...] = wg32_ref[...].astype(BF16)
        wu_ref[...] = wu32_ref[...].astype(BF16)
        wd_ref[...] = wd32_ref[...].astype(BF16)

    @pl.when(i >= npad)
    def _():
        x = x_ref[...]
        logits = (jnp.dot(x, wr_ref[0], preferred_element_type=F32)
                  + jnp.dot(x, wr_ref[1], preferred_element_type=F32))
        lane = lax.broadcasted_iota(jnp.int32, logits.shape, 1)
        logits = jnp.where(lane < N_EXPERTS, logits, -jnp.inf)
        ex = jnp.exp(logits - jnp.max(logits, axis=-1, keepdims=True))
        gate = (jnp.sum(jnp.where(lane == e, ex, 0.0), axis=-1, keepdims=True)
                / jnp.sum(ex, axis=-1, keepdims=True))
        hid = []
        for c0 in range(0, EXPERT_FF, FF_CHUNK):
            hg = jnp.dot(x, wg_ref[:, c0:c0 + FF_CHUNK], preferred_element_type=F32)
            hu = jnp.dot(x, wu_ref[:, c0:c0 + FF_CHUNK], preferred_element_type=F32)
            hid.append((_silu(hg) * hu).astype(BF16))
        hid = jnp.concatenate(hid, axis=1)
        y_ref[...] = (jnp.dot(hid, wd_ref[...], preferred_element_type=F32) * gate).astype(BF16)

    @pl.when(i < npad)
    def _():
        y_ref[...] = jnp.zeros_like(y_ref)


def _expert_ffn(xe, wr2, wg, wu, wd, layer, cap, tc):
    ne, rows, _ = xe.shape
    ntile = cap // tc
    npad = rows // tc - ntile
    wspec = lambda shape: pl.BlockSpec((None, None) + shape, lambda e, i: (layer, e, 0, 0))
    return pl.pallas_call(
        functools.partial(_expert_kernel, npad=npad),
        grid=(ne, rows // tc),
        in_specs=[pl.BlockSpec((None, tc, D_MODEL), lambda e, i: (e, jnp.maximum(i - npad, 0), 0)),
                  pl.BlockSpec(wr2.shape, lambda e, i: (0, 0, 0)),
                  wspec((D_MODEL, EXPERT_FF)), wspec((D_MODEL, EXPERT_FF)), wspec((EXPERT_FF, D_MODEL))],
        out_specs=pl.BlockSpec((None, tc, D_MODEL),
                               lambda e, i: (e, jnp.where(i < npad, ntile + i, i - npad), 0)),
        out_shape=jax.ShapeDtypeStruct((ne, rows, D_MODEL), BF16),
        scratch_shapes=[pltpu.VMEM((D_MODEL, EXPERT_FF), BF16), pltpu.VMEM((D_MODEL, EXPERT_FF), BF16),
                        pltpu.VMEM((EXPERT_FF, D_MODEL), BF16)],
        compiler_params=_params(("parallel", "arbitrary")),
        name="expert_ffn",
    )(xe, wr2, wg, wu, wd)


COMBINE_SUB = 2


def _combine_kernel(start_ref, kmax_ref, h_ref, p_ref, val_ref, tgt_ref, eexp_ref, gain_ref, wg_ref, wp_ref,
                    ye_ref, out_ref, stage, sem, *, tile):
    j = pl.program_id(0)
    nstep = pl.num_programs(0)
    slot = lax.rem(j, 2)
    ne = N_EXPERTS
    subs = range(COMBINE_SUB)

    def window_copy(sl, sub, e, row0):
        buf = sl * COMBINE_SUB + sub
        return pltpu.make_async_copy(ye_ref.at[e, pl.ds(row0, MOE_WIN)],
                                     stage.at[buf, pl.ds(e * MOE_WIN, MOE_WIN)], sem.at[buf])

    def fetch(sl, sub, tile_idx, lo):
        for e in range(ne):
            base = pl.multiple_of((start_ref[tile_idx * ne + e] & (-MOE_ALIGN)) + lo, MOE_ALIGN)
            window_copy(sl, sub, e, base).start()

    def wait_windows(sl, sub):
        for e in range(ne):
            window_copy(sl, sub, e, 0).wait()

    @pl.when(j == 0)
    def _():
        for sub in subs:
            fetch(slot, sub, sub, 0)

    @pl.when(j + 1 < nstep)
    def _():
        for sub in subs:
            fetch(1 - slot, sub, (j + 1) * COMBINE_SUB + sub, 0)

    reps = [lax.dot_general(val_ref[:, sub * tile:(sub + 1) * tile].astype(BF16), eexp_ref[...],
                            (((0,), (0,)), ((), ())), preferred_element_type=F32) for sub in subs]
    pts = [(reps[sub] == tgt_ref[sub]).astype(BF16) for sub in subs]
    for sub in subs:
        wait_windows(slot, sub)
    accs = [h_ref[sub * tile:(sub + 1) * tile, :]
            + jnp.dot(pts[sub], stage[slot * COMBINE_SUB + sub], preferred_element_type=F32) for sub in subs]
    for sub in subs:
        t = j * COMBINE_SUB + sub

        def extra(k, acc, sub=sub, t=t):
            lo = k * MOE_WIN
            fetch(slot, sub, t, lo)
            wait_windows(slot, sub)
            pk = (reps[sub] == tgt_ref[sub] + lo.astype(F32)).astype(BF16)
            return acc + jnp.dot(pk, stage[slot * COMBINE_SUB + sub], preferred_element_type=F32)

        accs[sub] = lax.fori_loop(1, kmax_ref[t], extra, accs[sub])
    x = jnp.concatenate(accs, axis=0)
    ms = jnp.mean(x * x, axis=-1, keepdims=True)
    xn = (x * lax.rsqrt(ms + NORM_EPS) * gain_ref[...]).astype(BF16)
    gate = _sigmoid(jnp.dot(xn, wg_ref[...], preferred_element_type=F32))
    proj = jnp.dot(p_ref[...].astype(BF16), wp_ref[...], preferred_element_type=F32)
    out_ref[...] = x + gate * proj


def _combine_ple(h2, p3, layer, ye, val, starts, kmax, tgt, eexp, gain, wg, wp, tile):
    n = h2.shape[0]
    step = COMBINE_SUB * tile
    assert n % step == 0
    rows = N_EXPERTS * MOE_WIN
    fixed = lambda j, s, k: (0, 0)
    return pl.pallas_call(
        functools.partial(_combine_kernel, tile=tile),
        grid_spec=pltpu.PrefetchScalarGridSpec(
            num_scalar_prefetch=2, grid=(n // step,),
            in_specs=[pl.BlockSpec((step, D_MODEL), lambda j, s, k: (j, 0)),
                      pl.BlockSpec((None, step, PLE_DIM), lambda j, s, k: (layer, j, 0)),
                      pl.BlockSpec((N_EXPERTS, step), lambda j, s, k: (0, j)),
                      pl.BlockSpec((COMBINE_SUB, 1, rows), lambda j, s, k: (j, 0, 0)),
                      pl.BlockSpec((N_EXPERTS, rows), fixed),
                      pl.BlockSpec((1, D_MODEL), fixed), pl.BlockSpec(wg.shape, fixed),
                      pl.BlockSpec(wp.shape, fixed),
                      pl.BlockSpec(memory_space=pl.ANY)],
            out_specs=pl.BlockSpec((step, D_MODEL), lambda j, s, k: (j, 0)),
            scratch_shapes=[pltpu.VMEM((2 * COMBINE_SUB, rows, D_MODEL), BF16),
                            pltpu.SemaphoreType.DMA((2 * COMBINE_SUB,))]),
        out_shape=jax.ShapeDtypeStruct((n, D_MODEL), F32),
        compiler_params=_params(("arbitrary",)),
        name="moe_combine_ple",
    )(starts, kmax, h2, p3, val, tgt, eexp, gain, wg, wp, ye)


def _relayout_w_in(w_in):
    o_beta = ZA + ZB + DN_WIDTH
    o_alpha = o_beta + 2 * DN_HEADS
    o_glu = o_alpha + 2 * DN_HEADS
    pieces = [w_in[:, :o_beta], w_in[:, o_glu:o_glu + 2 * CONV_CH]]
    for d in range(2):
        pieces.append(w_in[:, o_beta + d * DN_HEADS:o_beta + (d + 1) * DN_HEADS])
        pieces.append(w_in[:, o_alpha + d * DN_HEADS:o_alpha + (d + 1) * DN_HEADS])
    pieces.append(jnp.zeros((w_in.shape[0], LANES - 4 * DN_HEADS), w_in.dtype))
    return jnp.concatenate(pieces, axis=1).astype(BF16)


def _prep_layer(lw):
    (norm_mix, w_in, q_gain, k_gain, sink, dn_conv, dn_a_log, dn_dt_bias, dn_out_gain,
     cv_dw, cv_dw_bias, cv_ln_gain, cv_ln_bias, w_out, norm_ffn, w_router, w_gate, w_up, w_down,
     norm_ple, w_ple_gate, w_ple_proj) = lw
    w_perm = _relayout_w_in(w_in)
    hgain = jnp.concatenate([jnp.tile(q_gain, ATT_HEADS) * (ATT_HEAD_DIM ** -0.5),
                             jnp.tile(k_gain, ATT_KV_HEADS)]).reshape(1, -1)
    zeros4 = jnp.zeros((DN_HEADS,), F32)
    aneg = -jnp.exp(dn_a_log.astype(F32))
    aneg_row = jnp.concatenate([zeros4, aneg[0], zeros4, aneg[1]])
    dtb_row = jnp.concatenate([zeros4, dn_dt_bias[0], zeros4, dn_dt_bias[1]])
    pad = lambda r: jnp.pad(r, (0, LANES - r.shape[0])).reshape(1, LANES)
    wr = jnp.pad(w_router.astype(F32), ((0, 0), (0, LANES - N_EXPERTS)))
    wr_hi = wr.astype(BF16)
    wr2 = jnp.stack([wr_hi, (wr - wr_hi.astype(F32)).astype(BF16)])
    return dict(
        w_router2=wr2,
        norm_mix=norm_mix.reshape(1, -1), w_in=w_perm, hgain=hgain, sink=sink.astype(F32),
        dn_conv=dn_conv, aneg=pad(aneg_row), dtb=pad(dtb_row),
        dn_out_gain=jnp.tile(dn_out_gain, DN_HEADS).reshape(1, -1),
        cv_dw=cv_dw, cv_dw_bias=cv_dw_bias.reshape(1, -1), cv_ln_gain=cv_ln_gain.reshape(1, -1),
        cv_ln_bias=cv_ln_bias.reshape(1, -1), w_out=w_out.astype(BF16),
        norm_ffn=norm_ffn.reshape(1, -1),
        norm_ple=norm_ple.reshape(1, -1), w_ple_gate=w_ple_gate.astype(BF16), w_ple_proj=w_ple_proj.astype(BF16))


def _tiles(bsz, seqlen):
    n = bsz * seqlen
    return dict(tm=min(1024, n), tl=min(512, seqlen), ch=min(1024, seqlen), tcv=min(1024, seqlen))


def _moe_ple(h2, xn, aff_t, p3, layer, pw):
    n = h2.shape[0]
    cap = CAPACITY_FACTOR * n // N_EXPERTS
    tile = min(MOE_TILE, n)
    val, cnt = _select(aff_t, cap, tile)
    starts, kmax, tgt = _moe_plan(cnt, n // tile)
    eexp = _expand_matrix()
    xe = _dispatch(xn, val, starts, kmax, tgt, eexp, cap, tile)
    ye = _expert_ffn(xe, pw["w_router2"], pw["w_gate"], pw["w_up"], pw["w_down"], layer, cap,
                     min(MOE_FFN_TILE, cap))
    return _combine_ple(h2, p3, layer, ye, val, starts, kmax, tgt, eexp, pw["norm_ple"], pw["w_ple_gate"],
                        pw["w_ple_proj"], tile)


def _layer(h2, p3, layer, pw, bsz, seqlen):
    t = _tiles(bsz, seqlen)
    hm_att = _head_mean_matrix(ATT_Q + ATT_KV, ATT_HEAD_DIM)
    hs_dn = _head_sum_matrix(2 * DN_WIDTH, DN_HEAD_DIM)
    hm_dn = _head_mean_matrix(DN_WIDTH, DN_HEAD_DIM)
    za, zb, zg, glu_in, gates = _in_proj(h2, pw["norm_mix"], pw["w_in"], hm_att, pw["hgain"], t["tm"])
    o_a = _attention(za, pw["sink"], bsz, seqlen)
    y, gb = _dn_prep(zb, gates, pw["dn_conv"], hs_dn, pw["aneg"], pw["dtb"], bsz, seqlen, t["tl"])
    o_f, o_b = _dn_chunk(y, gb, bsz, seqlen, t["ch"])
    o_c = _conformer_conv(glu_in, pw["cv_dw"], pw["cv_dw_bias"], pw["cv_ln_gain"], pw["cv_ln_bias"],
                          bsz, seqlen, t["tcv"])
    h2, xn, aff_t = _out_proj_route(h2, o_a, o_f, o_b, zg, o_c, pw["dn_out_gain"], hm_dn, pw["w_out"],
                                    pw["norm_ffn"], pw["w_router2"], t["tm"])
    return _moe_ple(h2, xn, aff_t, p3, layer, pw)


def _trunk(x, p, layer_weights):
    bsz, seqlen, _ = x.shape
    h2 = x.reshape(bsz * seqlen, D_MODEL)
    p3 = p.reshape(p.shape[0], bsz * seqlen, PLE_DIM)
    for i, pw in enumerate(layer_weights):
        h2 = _layer(h2, p3, i, pw, bsz, seqlen)
    return h2.reshape(bsz, seqlen, D_MODEL)


def kernel(x_prompt, x_sample, p_prompt, p_sample, norm_mix, w_in, q_gain, k_gain, sink, dn_conv, dn_a_log,
           dn_dt_bias, dn_out_gain, cv_dw, cv_dw_bias, cv_ln_gain, cv_ln_bias, w_out, norm_ffn, w_router,
           w_gate, w_up, w_down, norm_ple, w_ple_gate, w_ple_proj):
    weights = (norm_mix, w_in, q_gain, k_gain, sink, dn_conv, dn_a_log, dn_dt_bias, dn_out_gain,
               cv_dw, cv_dw_bias, cv_ln_gain, cv_ln_bias, w_out, norm_ffn, w_router, w_gate, w_up, w_down,
               norm_ple, w_ple_gate, w_ple_proj)
    depth = w_in.shape[0]
    experts = dict(w_gate=w_gate, w_up=w_up, w_down=w_down)
    layer_weights = [dict(_prep_layer([w[i] for w in weights]), **experts) for i in range(depth)]
    return (_trunk(x_prompt, p_prompt, layer_weights), _trunk(x_sample, p_sample, layer_weights))
```

```python
import functools
import math

import numpy as np
import jax
import jax.numpy as jnp
from jax import lax
from jax.experimental import pallas as pl
from jax.experimental.pallas import tpu as pltpu

F32 = jnp.float32
BF16 = jnp.bfloat16

D_MODEL = 1024
ATT_HEADS = 8
ATT_KV_HEADS = 2
ATT_HEAD_DIM = 64
ATT_GROUP = ATT_HEADS // ATT_KV_HEADS
WINDOW = 128
ATT_BLOCK = 128
DN_HEADS = 4
DN_HEAD_DIM = 64
DN_WIDTH = DN_HEADS * DN_HEAD_DIM
DN_CHUNK = 64
CONV_CH = 256
CONV_WIDTH = 31
ATT_Q = ATT_HEADS * ATT_HEAD_DIM
ATT_KV = ATT_KV_HEADS * ATT_HEAD_DIM
N_EXPERTS = 16
CAPACITY_FACTOR = 2
EXPERT_FF = 1024
PLE_DIM = 256
NORM_EPS = 1e-6

LANES = 128
SUBLANES = 8
VMEM_LIMIT = 48 * 1024 * 1024

ZA = ATT_Q + 2 * ATT_KV
ZB = 3 * DN_WIDTH
ZW = ZA + ZB + DN_WIDTH + 2 * CONV_CH + LANES


def _params(sem):
    return pltpu.CompilerParams(dimension_semantics=sem, vmem_limit_bytes=VMEM_LIMIT)


def _head_mean_matrix(width, head):
    idx = np.arange(width) // head
    return jnp.asarray((idx[:, None] == idx[None, :]).astype(np.float32) / head, dtype=BF16)


def _head_sum_matrix(width, head):
    idx = np.arange(width) // head
    return jnp.asarray((idx[:, None] == idx[None, :]).astype(np.float32), dtype=BF16)


def _sigmoid(x):
    return 1.0 / (1.0 + jnp.exp(-x))


def _silu(x):
    return x * _sigmoid(x)


def _in_proj_kernel(x_ref, gain_ref, w_ref, hm_ref, hgain_ref, za_ref, zb_ref, zg_ref, glu_ref, gates_ref):
    x = x_ref[...]
    ms = jnp.mean(x * x, axis=-1, keepdims=True)
    a = (x * lax.rsqrt(ms + NORM_EPS) * gain_ref[...]).astype(BF16)
    z = jnp.dot(a, w_ref[...], preferred_element_type=F32)
    nqk = ATT_Q + ATT_KV
    qk = z[:, :nqk]
    hms = jnp.dot((qk * qk).astype(BF16), hm_ref[...], preferred_element_type=F32)
    za_ref[:, :nqk] = (qk * lax.rsqrt(hms + NORM_EPS) * hgain_ref[...]).astype(BF16)
    za_ref[:, nqk:] = z[:, nqk:ZA].astype(BF16)
    zb_ref[...] = z[:, ZA:ZA + ZB]
    zg_ref[...] = z[:, ZA + ZB:ZA + ZB + DN_WIDTH]
    glu_ref[...] = z[:, ZA + ZB + DN_WIDTH:ZA + ZB + DN_WIDTH + 2 * CONV_CH]
    gates_ref[...] = z[:, ZW - LANES:]


def _in_proj(h2, gain, w_perm, hm, hgain, tm):
    n = h2.shape[0]
    row = lambda i: (i, 0)
    fixed = lambda i: (0, 0)
    return pl.pallas_call(
        _in_proj_kernel,
        grid=(n // tm,),
        in_specs=[pl.BlockSpec((tm, D_MODEL), row), pl.BlockSpec((1, D_MODEL), fixed),
                  pl.BlockSpec((D_MODEL, ZW), fixed), pl.BlockSpec(hm.shape, fixed),
                  pl.BlockSpec(hgain.shape, fixed)],
        out_specs=[pl.BlockSpec((tm, ZA), row), pl.BlockSpec((tm, ZB), row), pl.BlockSpec((tm, DN_WIDTH), row),
                   pl.BlockSpec((tm, 2 * CONV_CH), row), pl.BlockSpec((tm, LANES), row)],
        out_shape=[jax.ShapeDtypeStruct((n, ZA), BF16), jax.ShapeDtypeStruct((n, ZB), F32),
                   jax.ShapeDtypeStruct((n, DN_WIDTH), F32), jax.ShapeDtypeStruct((n, 2 * CONV_CH), F32),
                   jax.ShapeDtypeStruct((n, LANES), F32)],
        compiler_params=_params(("parallel",)),
        name="in_proj",
    )(h2, gain, w_perm, hm, hgain)


ATT_MASKED = -1e30


def _attn_bias_table():
    i = np.arange(ATT_BLOCK)[:, None]
    c = np.arange(3 * ATT_BLOCK)[None, :]
    rel = c - ATT_BLOCK - i
    slopes = 2.0 ** (-8.0 * np.arange(1, ATT_HEADS + 1) / ATT_HEADS)
    table = np.empty((3, ATT_KV_HEADS, ATT_GROUP * ATT_BLOCK, 3 * ATT_BLOCK), np.float32)
    for variant in range(3):
        ok = np.abs(rel) <= WINDOW
        if variant == 0:
            ok = ok & (c >= ATT_BLOCK)
        if variant == 2:
            ok = ok & (c < 2 * ATT_BLOCK)
        for hd in range(ATT_HEADS):
            g, j = divmod(hd, ATT_GROUP)
            table[variant, g, j * ATT_BLOCK:(j + 1) * ATT_BLOCK] = np.where(ok, -slopes[hd] * np.abs(rel), ATT_MASKED)
    return jnp.asarray(table)


def _attn_kernel(sink_ref, q_ref, kvp_ref, kvo_ref, kvn_ref, bias_a_ref, bias_b_ref, o_ref):
    kv = jnp.concatenate([kvp_ref[...], kvo_ref[...], kvn_ref[...]], axis=0)
    hd_ = ATT_HEAD_DIM
    groups = range(ATT_KV_HEADS)
    heads = range(ATT_HEADS)
    rows = lambda t, hd: t[(hd % ATT_GROUP) * ATT_BLOCK:(hd % ATT_GROUP + 1) * ATT_BLOCK]
    work = []
    for blk, bias_ref in enumerate((bias_a_ref, bias_b_ref)):
        keys = kv[blk * ATT_BLOCK:(blk + 3) * ATT_BLOCK]
        ks = [keys[:, g * hd_:(g + 1) * hd_] for g in groups]
        vs = [keys[:, ATT_KV + g * hd_:ATT_KV + (g + 1) * hd_] for g in groups]
        q = q_ref[blk * ATT_BLOCK:(blk + 1) * ATT_BLOCK, :]
        qs = [jnp.concatenate([q[:, (g * ATT_GROUP + j) * hd_:(g * ATT_GROUP + j + 1) * hd_]
                               for j in range(ATT_GROUP)], axis=0) for g in groups]
        sg = [lax.dot_general(qs[g], ks[g], (((1,), (1,)), ((), ())), preferred_element_type=F32) + bias_ref[g]
              for g in groups]
        work.append((vs, [rows(sg[hd // ATT_GROUP], hd) for hd in heads]))
    m = [[jnp.maximum(jnp.max(s[hd], axis=-1, keepdims=True), sink_ref[hd]) for hd in heads] for _, s in work]
    e = [[jnp.exp(s[hd] - mb[hd]) for hd in heads] for (_, s), mb in zip(work, m)]
    den = [[jnp.sum(eb[hd], axis=-1, keepdims=True) + jnp.exp(sink_ref[hd] - mb[hd]) for hd in heads]
           for eb, mb in zip(e, m)]
    for blk, ((vs, _), eb, db) in enumerate(zip(work, e, den)):
        eg = [jnp.concatenate([eb[g * ATT_GROUP + j].astype(BF16) for j in range(ATT_GROUP)], axis=0)
              for g in groups]
        og = [jnp.dot(eg[g], vs[g], preferred_element_type=F32) for g in groups]
        for hd in heads:
            o_ref[blk * ATT_BLOCK:(blk + 1) * ATT_BLOCK, hd * hd_:(hd + 1) * hd_] = (
                rows(og[hd // ATT_GROUP], hd) / db[hd]).astype(BF16)


def _attention(za, sink, bsz, seqlen):
    nb = seqlen // ATT_BLOCK
    assert nb >= 2 and nb % 2 == 0
    npair = nb // 2
    za3 = za.reshape(bsz, seqlen, ZA)
    kvw = 2 * ATT_KV
    kvc = ATT_Q // kvw
    bias = _attn_bias_table()
    bias_spec = lambda pick: pl.BlockSpec((None,) + bias.shape[1:], lambda b, n: (pick(n), 0, 0, 0))
    return pl.pallas_call(
        _attn_kernel,
        grid=(bsz, npair),
        in_specs=[pl.BlockSpec(memory_space=pltpu.SMEM),
                  pl.BlockSpec((None, 2 * ATT_BLOCK, ATT_Q), lambda b, n: (b, n, 0)),
                  pl.BlockSpec((None, ATT_BLOCK, kvw), lambda b, n: (b, jnp.maximum(2 * n - 1, 0), kvc)),
                  pl.BlockSpec((None, 2 * ATT_BLOCK, kvw), lambda b, n: (b, n, kvc)),
                  pl.BlockSpec((None, ATT_BLOCK, kvw), lambda b, n: (b, jnp.minimum(2 * n + 2, nb - 1), kvc)),
                  bias_spec(lambda n: jnp.where(n == 0, 0, 1)),
                  bias_spec(lambda n: jnp.where(n == npair - 1, 2, 1))],
        out_specs=pl.BlockSpec((None, 2 * ATT_BLOCK, ATT_Q), lambda b, n: (b, n, 0)),
        out_shape=jax.ShapeDtypeStruct((bsz, seqlen, ATT_Q), BF16),
        compiler_params=_params(("parallel", "parallel")),
        name="window_attention",
    )(sink, za3, za3, za3, za3, bias, bias).reshape(bsz * seqlen, ATT_Q)


DN_HALO = SUBLANES


def _dn_prep_kernel(x_ref, xp_ref, xn_ref, cw_ref, hs_ref, g_ref, aneg_ref, dtb_ref, mf_ref, mb_ref,
                    y_ref, gb_ref, buf_ref, *, tl):
    i = pl.program_id(1)
    nt = pl.num_programs(1)
    buf_ref[0:DN_HALO, :] = jnp.where(i > 0, xp_ref[...], 0.0)
    buf_ref[DN_HALO:DN_HALO + tl, :] = x_ref[...]
    buf_ref[DN_HALO + tl:, :] = jnp.where(i < nt - 1, xn_ref[...], 0.0)
    y = (cw_ref[0:1, :] * buf_ref[DN_HALO - 1:DN_HALO - 1 + tl, :]
         + cw_ref[1:2, :] * buf_ref[DN_HALO:DN_HALO + tl, :]
         + cw_ref[2:3, :] * buf_ref[DN_HALO + 1:DN_HALO + 1 + tl, :])
    y = _silu(y)
    qk = y[:, :2 * DN_WIDTH]
    ss = jnp.dot((qk * qk).astype(BF16), hs_ref[...], preferred_element_type=F32)
    lane = lax.broadcasted_iota(jnp.int32, (tl, 2 * DN_WIDTH), 1)
    scale = jnp.where(lane < DN_WIDTH, DN_HEAD_DIM ** -0.5, 1.0)
    y_ref[:, :2 * DN_WIDTH] = qk * lax.rsqrt(ss + NORM_EPS) * scale
    y_ref[:, 2 * DN_WIDTH:] = y[:, 2 * DN_WIDTH:]
    raw = g_ref[...]
    col = lax.broadcasted_iota(jnp.int32, (tl, LANES), 1)
    is_beta = (col & DN_HEADS) == 0
    t = raw + dtb_ref[...]
    softplus = jnp.maximum(t, 0.0) + jnp.log(1.0 + jnp.exp(-jnp.abs(t)))
    vals = jnp.where(is_beta, _sigmoid(raw), aneg_ref[...] * softplus)
    v_hi = vals.astype(BF16)
    r1 = vals - v_hi.astype(F32)
    v_mid = r1.astype(BF16)
    v_lo = (r1 - v_mid.astype(F32)).astype(BF16)
    terms = jnp.concatenate([v_hi, v_mid, v_lo], axis=1)
    cf3 = jnp.dot(mf_ref[...], terms, preferred_element_type=F32)
    cb3 = jnp.dot(mb_ref[...], terms, preferred_element_type=F32)
    cf = cf3[:, :LANES] + (cf3[:, LANES:2 * LANES] + cf3[:, 2 * LANES:])
    cb = cb3[:, :LANES] + (cb3[:, LANES:2 * LANES] + cb3[:, 2 * LANES:])
    gb_ref[0] = jnp.where(is_beta, vals, cf)
    gb_ref[1] = pltpu.roll(jnp.where(is_beta, vals, cb), LANES - 2 * DN_HEADS, axis=1)


def _dn_prep(zb, gates, conv_w, hs, aneg, dtb, bsz, seqlen, tl):
    zb3 = zb.reshape(bsz, seqlen, ZB)
    g3 = gates.reshape(bsz, seqlen, LANES)
    nt = seqlen // tl
    hb = tl // DN_HALO
    ch = np.arange(tl) // DN_CHUNK
    same = ch[:, None] == ch[None, :]
    pos = np.arange(tl)
    mf = jnp.asarray((same & (pos[None, :] <= pos[:, None])).astype(np.float32), dtype=BF16)
    mb = jnp.asarray((same & (pos[None, :] >= pos[:, None])).astype(np.float32), dtype=BF16)
    fixed = lambda b, i: (0, 0)
    y, gb = pl.pallas_call(
        functools.partial(_dn_prep_kernel, tl=tl),
        grid=(bsz, nt),
        in_specs=[pl.BlockSpec((None, tl, ZB), lambda b, i: (b, i, 0)),
                  pl.BlockSpec((None, DN_HALO, ZB), lambda b, i: (b, jnp.maximum(i * hb - 1, 0), 0)),
                  pl.BlockSpec((None, DN_HALO, ZB), lambda b, i: (b, jnp.minimum((i + 1) * hb, nt * hb - 1), 0)),
                  pl.BlockSpec(conv_w.shape, fixed), pl.BlockSpec(hs.shape, fixed),
                  pl.BlockSpec((None, tl, LANES), lambda b, i: (b, i, 0)),
                  pl.BlockSpec((1, LANES), fixed), pl.BlockSpec((1, LANES), fixed),
                  pl.BlockSpec((tl, tl), fixed), pl.BlockSpec((tl, tl), fixed)],
        out_specs=[pl.BlockSpec((None, tl, ZB), lambda b, i: (b, i, 0)),
                   pl.BlockSpec((2, None, tl, LANES), lambda b, i: (0, b, i, 0))],
        out_shape=[jax.ShapeDtypeStruct((bsz, seqlen, ZB), F32),
                   jax.ShapeDtypeStruct((2, bsz, seqlen, LANES), F32)],
        scratch_shapes=[pltpu.VMEM((tl + 2 * DN_HALO, ZB), F32)],
        compiler_params=_params(("parallel", "parallel")),
        name="deltanet_prep",
    )(zb3, zb3, zb3, conv_w, hs, g3, aneg, dtb, mf, mb)
    return y, gb


def _lane_expand(cols, first):
    c = cols.shape[0]
    lane = lax.broadcasted_iota(jnp.int32, (c, LANES), 1)
    halves = []
    for h in range(0, DN_HEADS, 2):
        a = jnp.broadcast_to(cols[:, first + h:first + h + 1], (c, LANES))
        b = jnp.broadcast_to(cols[:, first + h + 1:first + h + 2], (c, LANES))
        halves.append(jnp.where(lane < DN_HEAD_DIM, a, b))
    return jnp.concatenate(halves, axis=1)


def _dn_pair_kernel(xf_ref, xb_ref, gf_ref, gb_ref, of_ref, ob_ref, sf_ref, sb_ref, *, nsub):
    c = DN_CHUNK
    w = DN_WIDTH

    @pl.when(pl.program_id(1) == 0)
    def _():
        sf_ref[...] = jnp.zeros_like(sf_ref)
        sb_ref[...] = jnp.zeros_like(sb_ref)

    r_cat = lax.broadcasted_iota(jnp.int32, (c, w), 0)
    s_cat = lax.broadcasted_iota(jnp.int32, (c, w), 1) & (DN_HEAD_DIM - 1)
    eye_cat = s_cat == r_cat
    rr = lax.broadcasted_iota(jnp.int32, (w, w), 0)
    cc = lax.broadcasted_iota(jnp.int32, (w, w), 1)
    head = (rr >> 6) == (cc >> 6)
    head_b = head.astype(BF16)
    m16 = (s_cat >> 4) == (r_cat >> 4)
    m32 = (s_cat >> 5) == (r_cat >> 5)
    off16 = m32 & jnp.logical_not(m16)
    off32 = jnp.logical_not(m32)
    eye_f = eye_cat.astype(F32)

    def bd(t):
        return jnp.concatenate([t] * DN_HEADS, axis=0) * head_b

    def mm(a, b):
        return jnp.dot(a, b, preferred_element_type=F32)

    chunks = [(0, i * c) for i in range(nsub)] + [(1, (nsub - 1 - i) * c) for i in range(nsub)]
    xrefs = (xf_ref, xb_ref)
    grefs = (gf_ref, gb_ref)
    orefs = (of_ref, ob_ref)
    srefs = (sf_ref, sb_ref)
    incl = (s_cat <= r_cat, s_cat >= r_cat)
    strict = (s_cat < r_cat, s_cat > r_cat)
    last_row = (c - 1, 0)

    pre = []
    for d, st in chunks:
        x = xrefs[d][st:st + c, :]
        q, k, v = x[:, :w], x[:, w:2 * w], x[:, 2 * w:]
        gbt = grefs[d][st:st + c, :]
        beta = _lane_expand(gbt, 0)
        gc = _lane_expand(gbt, DN_HEADS)
        grow = jnp.sum(jnp.where(eye_cat, gc, 0.0), axis=0, keepdims=True)
        decay = jnp.exp(jnp.where(incl[d], gc - grow, -jnp.inf))
        glast = gc[last_row[d]:last_row[d] + 1, :]
        egc = jnp.exp(gc)
        kb = k * beta
        pre.append(dict(d=d, st=st, q=q, k=k, kb=kb, vb=v * beta, decay=decay, glast=glast, egc=egc,
                        kdec=(k * jnp.exp(glast - gc)).astype(BF16)))

    kks = [lax.dot_general(jnp.concatenate([p["kb"], p["q"]], axis=0).astype(BF16), bd(p["k"].astype(BF16)),
                           (((1,), (1,)), ((), ())), preferred_element_type=F32) for p in pre]
    a = [jnp.where(strict[p["d"]], kk[:c] * p["decay"], 0.0) for p, kk in zip(pre, kks)]
    intra = [jnp.where(incl[p["d"]], kk[c:] * p["decay"], 0.0).astype(BF16) for p, kk in zip(pre, kks)]
    xm = [jnp.where(m16, -t, 0.0) for t in a]
    xm_b = [t.astype(BF16) for t in xm]
    x2_b = [mm(t, bd(t)).astype(BF16) for t in xm_b]
    x2_d = [bd(t) for t in x2_b]
    dinv = [eye_f + t for t in xm]
    r2 = [mm(jnp.concatenate([t.astype(BF16), p2], axis=0), s2) for t, p2, s2 in zip(dinv, x2_b, x2_d)]
    dinv = [t + r[:c] for t, r in zip(dinv, r2)]
    x4_b = [r[c:].astype(BF16) for r in r2]
    x4_d = [bd(t) for t in x4_b]
    r4 = [mm(jnp.concatenate([t.astype(BF16), p4], axis=0), s4) for t, p4, s4 in zip(dinv, x4_b, x4_d)]
    dinv = [t + r[:c] for t, r in zip(dinv, r4)]
    x8_d = [bd(r[c:].astype(BF16)) for r in r4]
    dinv = [t + mm(t.astype(BF16), s8) for t, s8 in zip(dinv, x8_d)]
    dinv_b = [t.astype(BF16) for t in dinv]
    n32 = [bd(mm(jnp.where(off16, t, 0.0).astype(BF16), bd(db)).astype(BF16)) for t, db in zip(a, dinv_b)]
    t32 = [t - mm(db, n) for t, db, n in zip(dinv, dinv_b, n32)]
    t32_b = [t.astype(BF16) for t in t32]
    n64 = [bd(mm(jnp.where(off32, t, 0.0).astype(BF16), bd(tb)).astype(BF16)) for t, tb in zip(a, t32_b)]
    t_cat = [(t - mm(tb, n)).astype(BF16) for t, tb, n in zip(t32, t32_b, n64)]
    it = [mm(i_, bd(tc)).astype(BF16) for i_, tc in zip(intra, t_cat)]
    both = [mm(jnp.concatenate([tc, i_], axis=0),
               jnp.concatenate([bd(p["vb"].astype(BF16)), bd((p["kb"] * p["egc"]).astype(BF16))], axis=1))
            for tc, i_, p in zip(t_cat, it, pre)]
    uw_b = [t[:c].astype(BF16) for t in both]
    qo = [t[c:] for t in both]
    pn = [lax.dot_general(p["kdec"], t, (((0,), (0,)), ((), ())), preferred_element_type=F32)
          for p, t in zip(pre, uw_b)]
    lhs = [jnp.concatenate([n[:, w:].astype(BF16) * head_b, (p["q"] * p["egc"] - o[:, w:]).astype(BF16)], axis=0)
           for n, o, p in zip(pn, qo, pre)]
    for step in range(nsub):
        for d in range(2):
            i = d * nsub + step
            p = pre[i]
            state = srefs[d][...]
            r = jnp.dot(lhs[i], state.astype(BF16), preferred_element_type=F32)
            orefs[d][p["st"]:p["st"] + c, :] = r[w:] + qo[i][:, :w]
            srefs[d][...] = state * jnp.exp(p["glast"]) - r[:w] + jnp.where(head, pn[i][:, :w], 0.0)


def _dn_chunk(y, gb, bsz, seqlen, ch):
    nsub = ch // DN_CHUNK
    nblk = seqlen // ch
    fwd = lambda b, j: (b, j, 0)
    bwd = lambda b, j: (b, nblk - 1 - j, 0)
    o_f, o_b = pl.pallas_call(
        functools.partial(_dn_pair_kernel, nsub=nsub),
        grid=(bsz, nblk),
        in_specs=[pl.BlockSpec((None, ch, ZB), fwd), pl.BlockSpec((None, ch, ZB), bwd),
                  pl.BlockSpec((None, None, ch, LANES), lambda b, j: (0, b, j, 0)),
                  pl.BlockSpec((None, None, ch, LANES), lambda b, j: (1, b, nblk - 1 - j, 0))],
        out_specs=[pl.BlockSpec((None, ch, DN_WIDTH), fwd), pl.BlockSpec((None, ch, DN_WIDTH), bwd)],
        out_shape=[jax.ShapeDtypeStruct((bsz, seqlen, DN_WIDTH), F32)] * 2,
        scratch_shapes=[pltpu.VMEM((DN_WIDTH, DN_WIDTH), F32)] * 2,
        compiler_params=_params(("parallel", "arbitrary")),
        name="deltanet_chunks",
    )(y, y, gb, gb)
    return o_f.reshape(bsz * seqlen, DN_WIDTH), o_b.reshape(bsz * seqlen, DN_WIDTH)


CV_HALO = 2 * SUBLANES
CV_PAD = (CONV_WIDTH - 1) // 2


def _conv_kernel(x_ref, xp_ref, xn_ref, dw_ref, bias_ref, lng_ref, lnb_ref, o_ref, buf_ref, shift_ref, *, tl):
    i = pl.program_id(1)
    nt = pl.num_programs(1)

    def glu(t):
        return t[:, :CONV_CH] * _sigmoid(t[:, CONV_CH:])

    buf_ref[0:CV_HALO, :] = jnp.where(i > 0, glu(xp_ref[...]), 0.0)
    buf_ref[CV_HALO:CV_HALO + tl, :] = glu(x_ref[...])
    buf_ref[CV_HALO + tl:, :] = jnp.where(i < nt - 1, glu(xn_ref[...]), 0.0)
    acc = jnp.zeros((tl, CONV_CH), F32) + bias_ref[...]
    first = CV_HALO - CV_PAD
    span = -(-(first + CONV_WIDTH) // SUBLANES) * SUBLANES - SUBLANES
    for sub in range(SUBLANES):
        shift_ref[...] = buf_ref[sub:sub + tl + span, :]
        for base in range(0, span + 1, SUBLANES):
            j = base + sub - first
            if 0 <= j < CONV_WIDTH:
                acc = acc + dw_ref[j:j + 1, :] * shift_ref[base:base + tl, :]
    mu = jnp.mean(acc, axis=-1, keepdims=True)
    cen = acc - mu
    var = jnp.mean(cen * cen, axis=-1, keepdims=True)
    o_ref[...] = _silu(cen * lax.rsqrt(var + NORM_EPS) * lng_ref[...] + lnb_ref[...]).astype(BF16)


def _conformer_conv(glu_in, dw, bias, lng, lnb, bsz, seqlen, tl):
    x3 = glu_in.reshape(bsz, seqlen, 2 * CONV_CH)
    nt = seqlen // tl
    hb = tl // CV_HALO
    fixed = lambda b, i: (0, 0)
    return pl.pallas_call(
        functools.partial(_conv_kernel, tl=tl),
        grid=(bsz, nt),
        in_specs=[pl.BlockSpec((None, tl, 2 * CONV_CH), lambda b, i: (b, i, 0)),
                  pl.BlockSpec((None, CV_HALO, 2 * CONV_CH), lambda b, i: (b, jnp.maximum(i * hb - 1, 0), 0)),
                  pl.BlockSpec((None, CV_HALO, 2 * CONV_CH),
                               lambda b, i: (b, jnp.minimum((i + 1) * hb, nt * hb - 1), 0)),
                  pl.BlockSpec(dw.shape, fixed), pl.BlockSpec((1, CONV_CH), fixed),
                  pl.BlockSpec((1, CONV_CH), fixed), pl.BlockSpec((1, CONV_CH), fixed)],
        out_specs=pl.BlockSpec((None, tl, CONV_CH), lambda b, i: (b, i, 0)),
        out_shape=jax.ShapeDtypeStruct((bsz, seqlen, CONV_CH), BF16),
        scratch_shapes=[pltpu.VMEM((tl + 2 * CV_HALO, CONV_CH), F32),
                        pltpu.VMEM((tl + 2 * CV_HALO - SUBLANES, CONV_CH), F32)],
        compiler_params=_params(("parallel", "parallel")),
        name="conformer_conv",
    )(x3, x3, x3, dw, bias, lng, lnb).reshape(bsz * seqlen, CONV_CH)


def _out_proj_kernel(h_ref, oa_ref, of_ref, ob_ref, zg_ref, oc_ref, og_ref, hm_ref, w_ref, gain_ref, wr_ref,
                     out_ref, xn_ref, aff_ref):
    half = h_ref.shape[0] // 2
    for r0 in (0, half):
        rs = slice(r0, r0 + half)
        ob = of_ref[rs, :] + ob_ref[rs, :]
        ms = jnp.dot((ob * ob).astype(BF16), hm_ref[...], preferred_element_type=F32)
        obn = ob * lax.rsqrt(ms + NORM_EPS) * og_ref[...]
        ob2 = obn * _silu(zg_ref[rs, :])
        mix = jnp.concatenate([oa_ref[rs, :], ob2.astype(BF16), oc_ref[rs, :]], axis=1)
        x = h_ref[rs, :] + jnp.dot(mix, w_ref[...], preferred_element_type=F32)
        out_ref[rs, :] = x
        ms = jnp.mean(x * x, axis=-1, keepdims=True)
        xn = x * lax.rsqrt(ms + NORM_EPS) * gain_ref[...]
        xn_hi = xn.astype(BF16)
        xn_ref[rs, :] = xn_hi
        xn_lo = (xn - xn_hi.astype(F32)).astype(BF16)
        logits = (jnp.dot(xn_hi, wr_ref[0], preferred_element_type=F32)
                  + (jnp.dot(xn_lo, wr_ref[0], preferred_element_type=F32)
                     + jnp.dot(xn_hi, wr_ref[1], preferred_element_type=F32)))
        lane = lax.broadcasted_iota(jnp.int32, logits.shape, 1)
        logits = jnp.where(lane < N_EXPERTS, logits, -jnp.inf)
        m = jnp.max(logits, axis=-1, keepdims=True)
        e = jnp.exp(logits - m)
        aff = e / jnp.sum(e, axis=-1, keepdims=True)
        aff_ref[:, rs] = jnp.transpose(aff)[:N_EXPERTS, :]


def _out_proj_route(h2, oa, o_f, o_b, zg, oc, og, hm, w, gain, wr2, tm):
    n = h2.shape[0]
    row = lambda i: (i, 0)
    fixed = lambda i: (0, 0)
    return pl.pallas_call(
        _out_proj_kernel,
        grid=(n // tm,),
        in_specs=[pl.BlockSpec((tm, D_MODEL), row), pl.BlockSpec((tm, ATT_Q), row),
                  pl.BlockSpec((tm, DN_WIDTH), row), pl.BlockSpec((tm, DN_WIDTH), row),
                  pl.BlockSpec((tm, DN_WIDTH), row),
                  pl.BlockSpec((tm, CONV_CH), row), pl.BlockSpec((1, DN_WIDTH), fixed),
                  pl.BlockSpec(hm.shape, fixed), pl.BlockSpec(w.shape, fixed),
                  pl.BlockSpec((1, D_MODEL), fixed), pl.BlockSpec((2, D_MODEL, LANES), lambda i: (0, 0, 0))],
        out_specs=[pl.BlockSpec((tm, D_MODEL), row), pl.BlockSpec((tm, D_MODEL), row),
                   pl.BlockSpec((N_EXPERTS, tm), lambda i: (0, i))],
        out_shape=[jax.ShapeDtypeStruct((n, D_MODEL), F32), jax.ShapeDtypeStruct((n, D_MODEL), BF16),
                   jax.ShapeDtypeStruct((N_EXPERTS, n), F32)],
        compiler_params=_params(("parallel",)),
        name="out_proj_route",
    )(h2, oa, o_f, o_b, zg, oc, og, hm, w, gain, wr2)


MOE_TILE = 256
MOE_ALIGN = 2 * SUBLANES
MOE_WIN = 64
MOE_PAD = 1024
MOE_FFN_TILE = 1024
FF_CHUNK = 256
MOE_UNSELECTED = -64.0


def _select_kernel(aff_ref, tri_ref, val_ref, cnt_ref, *, cap, tile):
    ne, n = aff_ref.shape
    nt = n // tile
    capf = float(cap)

    def bits_of(x):
        return lax.bitcast_convert_type(x, jnp.int32)

    def search(i, thr):
        cand = thr | jnp.left_shift(jnp.int32(1), 30 - i)
        cnt = jnp.sum((bits_of(aff_ref[...]) >= cand).astype(F32), axis=1, keepdims=True)
        return jnp.where(cnt >= capf, cand, thr)

    thr = lax.fori_loop(0, 31, search, jnp.zeros((ne, 1), jnp.int32))
    n_gt = jnp.sum((bits_of(aff_ref[...]) > thr).astype(F32), axis=1, keepdims=True)
    need = capf - n_gt
    lane = lax.broadcasted_iota(jnp.int32, (ne, LANES), 1)

    def tile_body(j, carry):
        eq_before, cnt_acc = carry
        off = pl.multiple_of(j * tile, tile)
        b = bits_of(aff_ref[:, pl.ds(off, tile)])
        gt = b > thr
        eqf = (b == thr).astype(F32)
        eq_rank = eq_before + jnp.dot(eqf.astype(BF16), tri_ref[...], preferred_element_type=F32)
        self_ = jnp.where(gt, 1.0, jnp.where(eq_rank <= need, eqf, 0.0))
        rank = jnp.dot(self_.astype(BF16), tri_ref[...], preferred_element_type=F32)
        val_ref[:, pl.ds(off, tile)] = jnp.where(self_ > 0.0, rank, MOE_UNSELECTED)
        cnt = jnp.sum(self_, axis=1, keepdims=True)
        return (eq_before + jnp.sum(eqf, axis=1, keepdims=True), cnt_acc + jnp.where(lane == j, cnt, 0.0))

    init = (jnp.zeros((ne, 1), F32), jnp.zeros((ne, LANES), F32))
    _, cnt_acc = lax.fori_loop(0, nt, tile_body, init, unroll=math.gcd(nt, 4))
    cnt_ref[...] = cnt_acc


def _select(aff_t, cap, tile):
    ne, n = aff_t.shape
    assert n // tile <= LANES
    tri = jnp.asarray(np.triu(np.ones((tile, tile), np.float32)), dtype=BF16)
    return pl.pallas_call(
        functools.partial(_select_kernel, cap=cap, tile=tile),
        out_shape=[jax.ShapeDtypeStruct((ne, n), F32), jax.ShapeDtypeStruct((ne, LANES), F32)],
        compiler_params=pltpu.CompilerParams(vmem_limit_bytes=VMEM_LIMIT),
        name="moe_select",
    )(aff_t, tri)


def _moe_plan(cnt, nt):
    c = cnt[:, :nt].astype(jnp.int32).T
    starts = jnp.concatenate([jnp.zeros((1, N_EXPERTS), jnp.int32), jnp.cumsum(c, axis=0)], axis=0)
    head = starts[:-1] & (MOE_ALIGN - 1)
    kmax = jnp.maximum(jnp.max((head + c + MOE_WIN - 1) // MOE_WIN, axis=1), 1).astype(jnp.int32)
    w = jnp.arange(MOE_WIN, dtype=jnp.int32)
    tgt = (w[None, None, :] + 1 - head[:, :, None]).astype(F32).reshape(nt, 1, N_EXPERTS * MOE_WIN)
    return starts.reshape(-1), kmax, tgt


def _expand_matrix():
    e = np.arange(N_EXPERTS * MOE_WIN) // MOE_WIN
    return jnp.asarray((np.arange(N_EXPERTS)[:, None] == e[None, :]).astype(np.float32), dtype=BF16)


def _slot_onehot(val_ref, eexp_ref):
    return lax.dot_general(val_ref[...].astype(BF16), eexp_ref[...], (((0,), (0,)), ((), ())),
                           preferred_element_type=F32)


def _dispatch_kernel(start_ref, kmax_ref, xn_ref, val_ref, tgt_ref, eexp_ref, xe_ref, stage, carry, sem):
    j = pl.program_id(0)
    nt = pl.num_programs(0)
    slot = lax.rem(j, 2)
    ne = N_EXPERTS

    cap = xe_ref.shape[1] - MOE_PAD

    @pl.when(j == 0)
    def _():
        carry[...] = jnp.zeros_like(carry)
        stage[0, 0:MOE_PAD, :] = jnp.zeros((MOE_PAD, D_MODEL), BF16)
        fills = [pltpu.make_async_copy(stage.at[0, pl.ds(0, MOE_PAD)], xe_ref.at[e, pl.ds(cap, MOE_PAD)], sem.at[0])
                 for e in range(ne)]
        for f in fills:
            f.start()
        for f in fills:
            f.wait()

    def window_copy(sl, e, row0):
        return pltpu.make_async_copy(stage.at[sl, pl.ds(e * MOE_WIN, MOE_WIN)],
                                     xe_ref.at[e, pl.ds(row0, MOE_WIN)], sem.at[sl])

    def wait_windows(sl):
        for e in range(ne):
            window_copy(sl, e, 0).wait()

    rep = _slot_onehot(val_ref, eexp_ref)
    xn = xn_ref[...]
    row = lax.broadcasted_iota(jnp.int32, (MOE_ALIGN, D_MODEL), 0)

    def block(k, _):
        @pl.when(k > 0)
        def _():
            wait_windows(slot)

        lo = k * MOE_WIN
        pt = (rep == tgt_ref[...] + lo.astype(F32)).astype(BF16)
        comp = lax.dot_general(pt, xn, (((0,), (0,)), ((), ())), preferred_element_type=F32)
        stage[slot] = comp.astype(BF16)
        for e in range(ne):
            s = start_ref[j * ne + e]
            head = s & (MOE_ALIGN - 1)
            r0 = e * MOE_WIN
            kept = carry[e * MOE_ALIGN:(e + 1) * MOE_ALIGN, :]
            owned = row < jnp.where(k == 0, head, 0)
            stage[slot, r0:r0 + MOE_ALIGN, :] = jnp.where(owned, kept, stage[slot, r0:r0 + MOE_ALIGN, :])
            nxt = (head + start_ref[(j + 1) * ne + e] - s) & (-MOE_ALIGN)
            here = (nxt >= lo) & (nxt < lo + MOE_WIN)
            off = pl.multiple_of(jnp.clip(nxt - lo, 0, MOE_WIN - MOE_ALIGN), MOE_ALIGN)
            cand = stage[slot, pl.ds(r0 + off, MOE_ALIGN), :]
            carry[e * MOE_ALIGN:(e + 1) * MOE_ALIGN, :] = jnp.where(here, cand, kept)

        @pl.when((k == 0) & (j > 0))
        def _():
            wait_windows(1 - slot)

        for e in range(ne):
            base = pl.multiple_of((start_ref[j * ne + e] & (-MOE_ALIGN)) + lo, MOE_ALIGN)
            window_copy(slot, e, base).start()
        return 0

    lax.fori_loop(0, kmax_ref[j], block, 0)

    @pl.when(j == nt - 1)
    def _():
        wait_windows(slot)


def _dispatch(xn, val, starts, kmax, tgt, eexp, cap, tile):
    n = xn.shape[0]
    nt = n // tile
    rows = N_EXPERTS * MOE_WIN
    return pl.pallas_call(
        _dispatch_kernel,
        grid_spec=pltpu.PrefetchScalarGridSpec(
            num_scalar_prefetch=2, grid=(nt,),
            in_specs=[pl.BlockSpec((tile, D_MODEL), lambda j, s, k: (j, 0)),
                      pl.BlockSpec((N_EXPERTS, tile), lambda j, s, k: (0, j)),
                      pl.BlockSpec((None, 1, rows), lambda j, s, k: (j, 0, 0)),
                      pl.BlockSpec((N_EXPERTS, rows), lambda j, s, k: (0, 0))],
            out_specs=pl.BlockSpec(memory_space=pl.ANY),
            scratch_shapes=[pltpu.VMEM((2, rows, D_MODEL), BF16),
                            pltpu.VMEM((N_EXPERTS * MOE_ALIGN, D_MODEL), BF16),
                            pltpu.SemaphoreType.DMA((2,))]),
        out_shape=jax.ShapeDtypeStruct((N_EXPERTS, cap + MOE_PAD, D_MODEL), BF16),
        compiler_params=_params(("arbitrary",)),
        name="moe_dispatch",
    )(starts, kmax, xn, val, tgt, eexp)


def _expert_kernel(x_ref, wr_ref, wg32_ref, wu32_ref, wd32_ref, y_ref, wg_ref, wu_ref, wd_ref, *, npad):
    e = pl.program_id(0)
    i = pl.program_id(1)

    @pl.when(i == npad)
    def _():
        wg_ref[...] = wg32_ref[...].astype(BF16)
        wu_ref[...] = wu32_ref[...].astype(BF16)
        wd_ref[...] = wd32_ref[...].astype(BF16)

    @pl.when(i >= npad)
    def _():
        x = x_ref[...]
        logits = (jnp.dot(x, wr_ref[0], preferred_element_type=F32)
                  + jnp.dot(x, wr_ref[1], preferred_element_type=F32))
        lane = lax.broadcasted_iota(jnp.int32, logits.shape, 1)
        logits = jnp.where(lane < N_EXPERTS, logits, -jnp.inf)
        ex = jnp.exp(logits - jnp.max(logits, axis=-1, keepdims=True))
        gate = (jnp.sum(jnp.where(lane == e, ex, 0.0), axis=-1, keepdims=True)
                / jnp.sum(ex, axis=-1, keepdims=True))
        hid = []
        for c0 in range(0, EXPERT_FF, FF_CHUNK):
            hg = jnp.dot(x, wg_ref[:, c0:c0 + FF_CHUNK], preferred_element_type=F32)
            hu = jnp.dot(x, wu_ref[:, c0:c0 + FF_CHUNK], preferred_element_type=F32)
            hid.append((_silu(hg) * hu).astype(BF16))
        hid = jnp.concatenate(hid, axis=1)
        y_ref[...] = (jnp.dot(hid, wd_ref[...], preferred_element_type=F32) * gate).astype(BF16)

    @pl.when(i < npad)
    def _():
        y_ref[...] = jnp.zeros_like(y_ref)


def _expert_ffn(xe, wr2, wg, wu, wd, layer, cap, tc):
    ne, rows, _ = xe.shape
    ntile = cap // tc
    npad = rows // tc - ntile
    wspec = lambda shape: pl.BlockSpec((None, None) + shape, lambda e, i: (layer, e, 0, 0))
    return pl.pallas_call(
        functools.partial(_expert_kernel, npad=npad),
        grid=(ne, rows // tc),
        in_specs=[pl.BlockSpec((None, tc, D_MODEL), lambda e, i: (e, jnp.maximum(i - npad, 0), 0)),
                  pl.BlockSpec(wr2.shape, lambda e, i: (0, 0, 0)),
                  wspec((D_MODEL, EXPERT_FF)), wspec((D_MODEL, EXPERT_FF)), wspec((EXPERT_FF, D_MODEL))],
        out_specs=pl.BlockSpec((None, tc, D_MODEL),
                               lambda e, i: (e, jnp.where(i < npad, ntile + i, i - npad), 0)),
        out_shape=jax.ShapeDtypeStruct((ne, rows, D_MODEL), BF16),
        scratch_shapes=[pltpu.VMEM((D_MODEL, EXPERT_FF), BF16), pltpu.VMEM((D_MODEL, EXPERT_FF), BF16),
                        pltpu.VMEM((EXPERT_FF, D_MODEL), BF16)],
        compiler_params=_params(("parallel", "arbitrary")),
        name="expert_ffn",
    )(xe, wr2, wg, wu, wd)


COMBINE_SUB = 2


def _combine_kernel(start_ref, kmax_ref, h_ref, p_ref, val_ref, tgt_ref, eexp_ref, gain_ref, wg_ref, wp_ref,
                    ye_ref, out_ref, stage, sem, *, tile):
    j = pl.program_id(0)
    nstep = pl.num_programs(0)
    slot = lax.rem(j, 2)
    ne = N_EXPERTS
    subs = range(COMBINE_SUB)

    def window_copy(sl, sub, e, row0):
        buf = sl * COMBINE_SUB + sub
        return pltpu.make_async_copy(ye_ref.at[e, pl.ds(row0, MOE_WIN)],
                                     stage.at[buf, pl.ds(e * MOE_WIN, MOE_WIN)], sem.at[buf])

    def fetch(sl, sub, tile_idx, lo):
        for e in range(ne):
            base = pl.multiple_of((start_ref[tile_idx * ne + e] & (-MOE_ALIGN)) + lo, MOE_ALIGN)
            window_copy(sl, sub, e, base).start()

    def wait_windows(sl, sub):
        for e in range(ne):
            window_copy(sl, sub, e, 0).wait()

    @pl.when(j == 0)
    def _():
        for sub in subs:
            fetch(slot, sub, sub, 0)

    @pl.when(j + 1 < nstep)
    def _():
        for sub in subs:
            fetch(1 - slot, sub, (j + 1) * COMBINE_SUB + sub, 0)

    reps = [lax.dot_general(val_ref[:, sub * tile:(sub + 1) * tile].astype(BF16), eexp_ref[...],
                            (((0,), (0,)), ((), ())), preferred_element_type=F32) for sub in subs]
    pts = [(reps[sub] == tgt_ref[sub]).astype(BF16) for sub in subs]
    for sub in subs:
        wait_windows(slot, sub)
    accs = [h_ref[sub * tile:(sub + 1) * tile, :]
            + jnp.dot(pts[sub], stage[slot * COMBINE_SUB + sub], preferred_element_type=F32) for sub in subs]
    for sub in subs:
        t = j * COMBINE_SUB + sub

        def extra(k, acc, sub=sub, t=t):
            lo = k * MOE_WIN
            fetch(slot, sub, t, lo)
            wait_windows(slot, sub)
            pk = (reps[sub] == tgt_ref[sub] + lo.astype(F32)).astype(BF16)
            return acc + jnp.dot(pk, stage[slot * COMBINE_SUB + sub], preferred_element_type=F32)

        accs[sub] = lax.fori_loop(1, kmax_ref[t], extra, accs[sub])
    x = jnp.concatenate(accs, axis=0)
    ms = jnp.mean(x * x, axis=-1, keepdims=True)
    xn = (x * lax.rsqrt(ms + NORM_EPS) * gain_ref[...]).astype(BF16)
    gate = _sigmoid(jnp.dot(xn, wg_ref[...], preferred_element_type=F32))
    proj = jnp.dot(p_ref[...].astype(BF16), wp_ref[...], preferred_element_type=F32)
    out_ref[...] = x + gate * proj


def _combine_ple(h2, p3, layer, ye, val, starts, kmax, tgt, eexp, gain, wg, wp, tile):
    n = h2.shape[0]
    step = COMBINE_SUB * tile
    assert n % step == 0
    rows = N_EXPERTS * MOE_WIN
    fixed = lambda j, s, k: (0, 0)
    return pl.pallas_call(
        functools.partial(_combine_kernel, tile=tile),
        grid_spec=pltpu.PrefetchScalarGridSpec(
            num_scalar_prefetch=2, grid=(n // step,),
            in_specs=[pl.BlockSpec((step, D_MODEL), lambda j, s, k: (j, 0)),
                      pl.BlockSpec((None, step, PLE_DIM), lambda j, s, k: (layer, j, 0)),
                      pl.BlockSpec((N_EXPERTS, step), lambda j, s, k: (0, j)),
                      pl.BlockSpec((COMBINE_SUB, 1, rows), lambda j, s, k: (j, 0, 0)),
                      pl.BlockSpec((N_EXPERTS, rows), fixed),
                      pl.BlockSpec((1, D_MODEL), fixed), pl.BlockSpec(wg.shape, fixed),
                      pl.BlockSpec(wp.shape, fixed),
                      pl.BlockSpec(memory_space=pl.ANY)],
            out_specs=pl.BlockSpec((step, D_MODEL), lambda j, s, k: (j, 0)),
            scratch_shapes=[pltpu.VMEM((2 * COMBINE_SUB, rows, D_MODEL), BF16),
                            pltpu.SemaphoreType.DMA((2 * COMBINE_SUB,))]),
        out_shape=jax.ShapeDtypeStruct((n, D_MODEL), F32),
        compiler_params=_params(("arbitrary",)),
        name="moe_combine_ple",
    )(starts, kmax, h2, p3, val, tgt, eexp, gain, wg, wp, ye)


def _relayout_w_in(w_in):
    o_beta = ZA + ZB + DN_WIDTH
    o_alpha = o_beta + 2 * DN_HEADS
    o_glu = o_alpha + 2 * DN_HEADS
    pieces = [w_in[:, :o_beta], w_in[:, o_glu:o_glu + 2 * CONV_CH]]
    for d in range(2):
        pieces.append(w_in[:, o_beta + d * DN_HEADS:o_beta + (d + 1) * DN_HEADS])
        pieces.append(w_in[:, o_alpha + d * DN_HEADS:o_alpha + (d + 1) * DN_HEADS])
    pieces.append(jnp.zeros((w_in.shape[0], LANES - 4 * DN_HEADS), w_in.dtype))
    return jnp.concatenate(pieces, axis=1).astype(BF16)


def _prep_layer(lw):
    (norm_mix, w_in, q_gain, k_gain, sink, dn_conv, dn_a_log, dn_dt_bias, dn_out_gain,
     cv_dw, cv_dw_bias, cv_ln_gain, cv_ln_bias, w_out, norm_ffn, w_router, w_gate, w_up, w_down,
     norm_ple, w_ple_gate, w_ple_proj) = lw
    w_perm = _relayout_w_in(w_in)
    hgain = jnp.concatenate([jnp.tile(q_gain, ATT_HEADS) * (ATT_HEAD_DIM ** -0.5),
                             jnp.tile(k_gain, ATT_KV_HEADS)]).reshape(1, -1)
    zeros4 = jnp.zeros((DN_HEADS,), F32)
    aneg = -jnp.exp(dn_a_log.astype(F32))
    aneg_row = jnp.concatenate([zeros4, aneg[0], zeros4, aneg[1]])
    dtb_row = jnp.concatenate([zeros4, dn_dt_bias[0], zeros4, dn_dt_bias[1]])
    pad = lambda r: jnp.pad(r, (0, LANES - r.shape[0])).reshape(1, LANES)
    wr = jnp.pad(w_router.astype(F32), ((0, 0), (0, LANES - N_EXPERTS)))
    wr_hi = wr.astype(BF16)
    wr2 = jnp.stack([wr_hi, (wr - wr_hi.astype(F32)).astype(BF16)])
    return dict(
        w_router2=wr2,
        norm_mix=norm_mix.reshape(1, -1), w_in=w_perm, hgain=hgain, sink=sink.astype(F32),
        dn_conv=dn_conv, aneg=pad(aneg_row), dtb=pad(dtb_row),
        dn_out_gain=jnp.tile(dn_out_gain, DN_HEADS).reshape(1, -1),
        cv_dw=cv_dw, cv_dw_bias=cv_dw_bias.reshape(1, -1), cv_ln_gain=cv_ln_gain.reshape(1, -1),
        cv_ln_bias=cv_ln_bias.reshape(1, -1), w_out=w_out.astype(BF16),
        norm_ffn=norm_ffn.reshape(1, -1),
        norm_ple=norm_ple.reshape(1, -1), w_ple_gate=w_ple_gate.astype(BF16), w_ple_proj=w_ple_proj.astype(BF16))


def _tiles(bsz, seqlen):
    n = bsz * seqlen
    return dict(tm=min(1024, n), tl=min(512, seqlen), ch=min(1024, seqlen), tcv=min(1024, seqlen))


def _moe_ple(h2, xn, aff_t, p3, layer, pw):
    n = h2.shape[0]
    cap = CAPACITY_FACTOR * n // N_EXPERTS
    tile = min(MOE_TILE, n)
    val, cnt = _select(aff_t, cap, tile)
    starts, kmax, tgt = _moe_plan(cnt, n // tile)
    eexp = _expand_matrix()
    xe = _dispatch(xn, val, starts, kmax, tgt, eexp, cap, tile)
    ye = _expert_ffn(xe, pw["w_router2"], pw["w_gate"], pw["w_up"], pw["w_down"], layer, cap,
                     min(MOE_FFN_TILE, cap))
    return _combine_ple(h2, p3, layer, ye, val, starts, kmax, tgt, eexp, pw["norm_ple"], pw["w_ple_gate"],
                        pw["w_ple_proj"], tile)


def _layer(h2, p3, layer, pw, bsz, seqlen):
    t = _tiles(bsz, seqlen)
    hm_att = _head_mean_matrix(ATT_Q + ATT_KV, ATT_HEAD_DIM)
    hs_dn = _head_sum_matrix(2 * DN_WIDTH, DN_HEAD_DIM)
    hm_dn = _head_mean_matrix(DN_WIDTH, DN_HEAD_DIM)
    za, zb, zg, glu_in, gates = _in_proj(h2, pw["norm_mix"], pw["w_in"], hm_att, pw["hgain"], t["tm"])
    o_a = _attention(za, pw["sink"], bsz, seqlen)
    y, gb = _dn_prep(zb, gates, pw["dn_conv"], hs_dn, pw["aneg"], pw["dtb"], bsz, seqlen, t["tl"])
    o_f, o_b = _dn_chunk(y, gb, bsz, seqlen, t["ch"])
    o_c = _conformer_conv(glu_in, pw["cv_dw"], pw["cv_dw_bias"], pw["cv_ln_gain"], pw["cv_ln_bias"],
                          bsz, seqlen, t["tcv"])
    h2, xn, aff_t = _out_proj_route(h2, o_a, o_f, o_b, zg, o_c, pw["dn_out_gain"], hm_dn, pw["w_out"],
                                    pw["norm_ffn"], pw["w_router2"], t["tm"])
    return _moe_ple(h2, xn, aff_t, p3, layer, pw)


def _trunk(x, p, layer_weights):
    bsz, seqlen, _ = x.shape
    h2 = x.reshape(bsz * seqlen, D_MODEL)
    p3 = p.reshape(p.shape[0], bsz * seqlen, PLE_DIM)
    for i, pw in enumerate(layer_weights):
        h2 = _layer(h2, p3, i, pw, bsz, seqlen)
    return h2.reshape(bsz, seqlen, D_MODEL)


def kernel(x_prompt, x_sample, p_prompt, p_sample, norm_mix, w_in, q_gain, k_gain, sink, dn_conv, dn_a_log,
           dn_dt_bias, dn_out_gain, cv_dw, cv_dw_bias, cv_ln_gain, cv_ln_bias, w_out, norm_ffn, w_router,
           w_gate, w_up, w_down, norm_ple, w_ple_gate, w_ple_proj):
    weights = (norm_mix, w_in, q_gain, k_gain, sink, dn_conv, dn_a_log, dn_dt_bias, dn_out_gain,
               cv_dw, cv_dw_bias, cv_ln_gain, cv_ln_bias, w_out, norm_ffn, w_router, w_gate, w_up, w_down,
               norm_ple, w_ple_gate, w_ple_proj)
    depth = w_in.shape[0]
    experts = dict(w_gate=w_gate, w_up=w_up, w_down=w_down)
    layer_weights = [dict(_prep_layer([w[i] for w in weights]), **experts) for i in range(depth)]
    return (_trunk(x_prompt, p_prompt, layer_weights), _trunk(x_sample, p_sample, layer_weights))
```
